```python
import math
import jax, jax.numpy as jnp
from jax import lax
import numpy as np

D_MODEL = 2048
BATCH = 8
SEQ = 8192
DEPTH = 4

CHUNK = 64
Q_BLOCK = 128
D_MIX = D_MODEL

MLA_HEADS = 6
MLA_NOPE = 128
MLA_ROPE = 64
MLA_V = 128
MLA_Q_RANK = 512
MLA_KV_RANK = 256
MLA_WIDTH = MLA_HEADS * MLA_V
ROPE_THETA = 10000.0

SG_GROUPS = 4
SG_GROUP_CH = 128
SG_WIDTH = SG_GROUPS * SG_GROUP_CH
SG_CHUNK = 128

SB_HEADS = 4
SB_HEAD_DIM = 128
SB_WIDTH = SB_HEADS * SB_HEAD_DIM

MEM_TOKENS = 256
MEM_HEADS = 4
MEM_HEAD_DIM = 64
MEM_WIDTH = MEM_HEADS * MEM_HEAD_DIM

IN_SIZES = (MLA_Q_RANK, MLA_KV_RANK, MLA_ROPE, MLA_WIDTH,
            SG_WIDTH, SG_WIDTH, SG_WIDTH,
            SB_WIDTH, SB_WIDTH, SB_WIDTH, SB_WIDTH,
            MEM_WIDTH, MEM_WIDTH)
D_IN = sum(IN_SIZES)

DEEPNORM_ALPHA = (2.0 * DEPTH) ** 0.25
DEEPNORM_BETA = (8.0 * DEPTH) ** -0.25
LN_EPS = 1e-5
RMS_EPS = 1e-6

kernel_name = "hybrid_mla_gmlp_stickbreak_deepnorm"


def _layer_norm(x, g, b):
    xf = x.astype(jnp.float32)
    mu = jnp.mean(xf, axis=-1, keepdims=True)
    xc = xf - mu
    var = jnp.mean(xc * xc, axis=-1, keepdims=True)
    return (xc * lax.rsqrt(var + LN_EPS) * g.astype(jnp.float32) + b.astype(jnp.float32)).astype(x.dtype)


def _rms_norm(x, g):
    xf = x.astype(jnp.float32)
    ms = jnp.mean(xf * xf, axis=-1, keepdims=True)
    return (xf * lax.rsqrt(ms + RMS_EPS) * g.astype(jnp.float32)).astype(x.dtype)


def _rope(x, cos, sin):
    half = x.shape[-1] // 2
    x1, x2 = x[..., :half], x[..., half:]
    return jnp.concatenate([x1 * cos - x2 * sin, x1 * sin + x2 * cos], axis=-1)


def _sweep_query_blocks(block_fn, q):
    b, s, h, d = q.shape
    nb = s // Q_BLOCK
    qb = q.reshape(b, nb, Q_BLOCK, h, d).transpose(1, 0, 2, 3, 4)
    out = lax.map(block_fn, (qb, jnp.arange(nb, dtype=jnp.int32)))
    return out.transpose(1, 0, 2, 3, 4).reshape(b, s, h, out.shape[-1])


def _chunk_causal_softmax_attention(q, k, v, scale):
    key_chunk = jnp.arange(k.shape[1]) // CHUNK

    def block(args):
        qblk, i = args
        qpos = i * Q_BLOCK + jnp.arange(Q_BLOCK)
        logits = jnp.einsum('bqhd,bkhd->bhqk', qblk, k).astype(jnp.float32) * scale
        mask = key_chunk[None, :] <= (qpos // CHUNK)[:, None]
        p = jax.nn.softmax(jnp.where(mask, logits, -jnp.inf), axis=-1)
        return jnp.einsum('bhqk,bkhd->bqhd', p.astype(v.dtype), v)

    return _sweep_query_blocks(block, q)


def _stick_breaking_attention(q, k, v, scale):
    kpos = jnp.arange(k.shape[1])

    def block(args):
        qblk, i = args
        qpos = i * Q_BLOCK + jnp.arange(Q_BLOCK)
        z = jnp.einsum('bqhd,bkhd->bhqk', qblk, k).astype(jnp.float32) * scale
        strict = kpos[None, :] < qpos[:, None]
        log_beta = jax.nn.log_sigmoid(z)
        log_1mb = jnp.where(strict, jax.nn.log_sigmoid(-z), 0.0)
        rev = lax.cumsum(log_1mb, axis=3, reverse=True)
        log_a = log_beta + rev - log_1mb
        a = jnp.where(strict, jnp.exp(log_a), 0.0)
        return jnp.einsum('bhqk,bkhd->bqhd', a.astype(v.dtype), v)

    return _sweep_query_blocks(block, q)


def _fwd_setup_inputs(seed: int = 0) -> dict:
    key = jax.random.key(seed)
    ks = jax.random.split(key, 20)
    f32 = jnp.float32
    nrm = lambda k, shape, s: jax.random.normal(k, shape, f32) * s
    x = jax.random.normal(ks[0], (BATCH, SEQ, D_MODEL), f32)
    mem = jax.random.normal(ks[1], (BATCH, MEM_TOKENS, D_MODEL), f32)
    offset = jax.random.randint(ks[2], (BATCH, 1), 0, 4096, dtype=jnp.int32)
    positions = (offset + jnp.arange(SEQ, dtype=jnp.int32)[None, :]).astype(jnp.int32)
    w_in = nrm(ks[3], (DEPTH, D_MODEL, D_IN), D_MODEL ** -0.5)
    q_norm_g = 1.0 + nrm(ks[4], (DEPTH, MLA_Q_RANK), 0.01)
    w_uq = nrm(ks[5], (DEPTH, MLA_Q_RANK, MLA_HEADS * (MLA_NOPE + MLA_ROPE)), MLA_Q_RANK ** -0.5)
    kv_norm_g = 1.0 + nrm(ks[6], (DEPTH, MLA_KV_RANK), 0.01)
    w_ukv = nrm(ks[7], (DEPTH, MLA_KV_RANK, MLA_HEADS * (MLA_NOPE + MLA_V)), MLA_KV_RANK ** -0.5)
    sg_ln_g = 1.0 + nrm(ks[8], (DEPTH, SG_WIDTH), 0.01)
    sg_ln_b = nrm(ks[9], (DEPTH, SG_WIDTH), 0.01)
    sg_w = nrm(ks[10], (DEPTH, SG_GROUPS, SG_CHUNK, SG_CHUNK), SG_CHUNK ** -0.5)
    sg_b = 1.0 + nrm(ks[11], (DEPTH, SG_GROUPS, SG_CHUNK), 0.01)
    w_mem_k = nrm(ks[12], (DEPTH, D_MODEL, MEM_WIDTH), D_MODEL ** -0.5)
    w_mem_v = nrm(ks[13], (DEPTH, D_MODEL, MEM_WIDTH), D_MODEL ** -0.5)
    w_out = nrm(ks[14], (DEPTH, D_MIX, D_MODEL), D_MIX ** -0.5 * DEEPNORM_BETA)
    ln_g = 1.0 + nrm(ks[15], (DEPTH, D_MODEL), 0.01)
    ln_b = nrm(ks[16], (DEPTH, D_MODEL), 0.01)
    return {"x": x, "mem": mem, "positions": positions, "w_in": w_in,
            "q_norm_g": q_norm_g, "w_uq": w_uq, "kv_norm_g": kv_norm_g, "w_ukv": w_ukv,
            "sg_ln_g": sg_ln_g, "sg_ln_b": sg_ln_b, "sg_w": sg_w, "sg_b": sg_b,
            "w_mem_k": w_mem_k, "w_mem_v": w_mem_v, "w_out": w_out,
            "ln_g": ln_g, "ln_b": ln_b}


def _fwd_reference(x, mem, positions, w_in, q_norm_g, w_uq, kv_norm_g, w_ukv,
              sg_ln_g, sg_ln_b, sg_w, sg_b, w_mem_k, w_mem_v, w_out, ln_g, ln_b):
    b, s, _ = x.shape
    inv_freq = ROPE_THETA ** (-jnp.arange(0, MLA_ROPE, 2, dtype=jnp.float32) / MLA_ROPE)
    ang = positions.astype(jnp.float32)[..., None] * inv_freq[None, None, :]
    cos = jnp.cos(ang).astype(x.dtype)
    sin = jnp.sin(ang).astype(x.dtype)
    split_idx = [int(v) for v in np.cumsum(IN_SIZES)[:-1]]
    p_in = jnp.arange(SG_CHUNK) // CHUNK
    sg_mask = (p_in[None, :] <= p_in[:, None]).astype(x.dtype)
    mla_scale = 1.0 / math.sqrt(MLA_NOPE + MLA_ROPE)
    sb_scale = 1.0 / math.sqrt(SB_HEAD_DIM)
    mem_scale = 1.0 / math.sqrt(MEM_HEAD_DIM)

    for l in range(DEPTH):
        h = jnp.einsum('bsd,de->bse', x, w_in[l])
        (c_q, c_kv, k_pe, g_a, sg_u, sg_v, g_b, sb_q, sb_k, sb_v, g_c, m_q, g_m) = jnp.split(h, split_idx, axis=-1)

        q = jnp.einsum('bsr,re->bse', _rms_norm(c_q, q_norm_g[l]), w_uq[l]).reshape(b, s, MLA_HEADS, MLA_NOPE + MLA_ROPE)
        q = jnp.concatenate([q[..., :MLA_NOPE], _rope(q[..., MLA_NOPE:], cos[:, :, None, :], sin[:, :, None, :])], axis=-1)
        kv = jnp.einsum('bsr,re->bse', _rms_norm(c_kv, kv_norm_g[l]), w_ukv[l]).reshape(b, s, MLA_HEADS, MLA_NOPE + MLA_V)
        k_rot = jnp.broadcast_to(_rope(k_pe, cos, sin)[:, :, None, :], (b, s, MLA_HEADS, MLA_ROPE))
        k = jnp.concatenate([kv[..., :MLA_NOPE], k_rot], axis=-1)
        o_a = _chunk_causal_softmax_attention(q, k, kv[..., MLA_NOPE:], mla_scale).reshape(b, s, MLA_WIDTH)

        u = jax.nn.gelu(sg_u)
        vn = _layer_norm(jax.nn.gelu(sg_v), sg_ln_g[l], sg_ln_b[l])
        vn = vn.reshape(b, s // SG_CHUNK, SG_CHUNK, SG_GROUPS, SG_GROUP_CH)
        w_sp = sg_w[l] * sg_mask[None]
        mixed = jnp.einsum('gts,bnsgc->bntgc', w_sp, vn) + sg_b[l].T[None, None, :, :, None]
        o_b = u * mixed.reshape(b, s, SG_WIDTH)

        o_c = _stick_breaking_attention(sb_q.reshape(b, s, SB_HEADS, SB_HEAD_DIM),
                                        sb_k.reshape(b, s, SB_HEADS, SB_HEAD_DIM),
                                        sb_v.reshape(b, s, SB_HEADS, SB_HEAD_DIM), sb_scale).reshape(b, s, SB_WIDTH)

        mk = jnp.einsum('bmd,de->bme', mem, w_mem_k[l]).reshape(b, MEM_TOKENS, MEM_HEADS, MEM_HEAD_DIM)
        mv = jnp.einsum('bmd,de->bme', mem, w_mem_v[l]).reshape(b, MEM_TOKENS, MEM_HEADS, MEM_HEAD_DIM)
        mq = m_q.reshape(b, s, MEM_HEADS, MEM_HEAD_DIM)
        mp = jax.nn.softmax(jnp.einsum('bshe,bmhe->bhsm', mq, mk).astype(jnp.float32) * mem_scale, axis=-1)
        o_m = jnp.einsum('bhsm,bmhe->bshe', mp.astype(mv.dtype), mv).reshape(b, s, MEM_WIDTH)

        y = jnp.concatenate([o_a * jax.nn.silu(g_a), o_b * jax.nn.silu(g_b),
                             o_c * jax.nn.silu(g_c), o_m * jax.nn.silu(g_m)], axis=-1)
        y = jnp.einsum('bse,ed->bsd', y, w_out[l])

        x = _layer_norm(DEEPNORM_ALPHA * x + y, ln_g[l], ln_b[l])
    return x


import jax as _jax
import jax.numpy as _jnp

TWIN_FORMAT = 'train_step'
FWD_PARAMS = ['x', 'mem', 'positions', 'w_in', 'q_norm_g', 'w_uq', 'kv_norm_g', 'w_ukv', 'sg_ln_g', 'sg_ln_b', 'sg_w', 'sg_b', 'w_mem_k', 'w_mem_v', 'w_out', 'ln_g', 'ln_b']
TWIN_WEIGHTS = ['w_in', 'q_norm_g', 'w_uq', 'kv_norm_g', 'w_ukv', 'sg_ln_g', 'sg_ln_b', 'sg_w', 'sg_b', 'w_mem_k', 'w_mem_v', 'w_out', 'ln_g', 'ln_b']
TWIN_DIFF_INPUT = 'x'
TWIN_INPUTS = ['x', 'mem', 'positions', 'w_in', 'q_norm_g', 'w_uq', 'kv_norm_g', 'w_ukv', 'sg_ln_g', 'sg_ln_b', 'sg_w', 'sg_b', 'w_mem_k', 'w_mem_v', 'w_out', 'ln_g', 'ln_b', 'loss_target', 'm_w_in', 'm_q_norm_g', 'm_w_uq', 'm_kv_norm_g', 'm_w_ukv', 'm_sg_ln_g', 'm_sg_ln_b', 'm_sg_w', 'm_sg_b', 'm_w_mem_k', 'm_w_mem_v', 'm_w_out', 'm_ln_g', 'm_ln_b', 'v_w_in', 'v_q_norm_g', 'v_w_uq', 'v_kv_norm_g', 'v_w_ukv', 'v_sg_ln_g', 'v_sg_ln_b', 'v_sg_w', 'v_sg_b', 'v_w_mem_k', 'v_w_mem_v', 'v_w_out', 'v_ln_g', 'v_ln_b']
TWIN_OUTPUTS = ['loss', 'grad_x', 'grad_w_in', 'grad_q_norm_g', 'grad_w_uq', 'grad_kv_norm_g', 'grad_w_ukv', 'grad_sg_ln_g', 'grad_sg_ln_b', 'grad_sg_w', 'grad_sg_b', 'grad_w_mem_k', 'grad_w_mem_v', 'grad_w_out', 'grad_ln_g', 'grad_ln_b', 'delta_w_in', 'delta_q_norm_g', 'delta_w_uq', 'delta_kv_norm_g', 'delta_w_ukv', 'delta_sg_ln_g', 'delta_sg_ln_b', 'delta_sg_w', 'delta_sg_b', 'delta_w_mem_k', 'delta_w_mem_v', 'delta_w_out', 'delta_ln_g', 'delta_ln_b', 'new_m_w_in', 'new_m_q_norm_g', 'new_m_w_uq', 'new_m_kv_norm_g', 'new_m_w_ukv', 'new_m_sg_ln_g', 'new_m_sg_ln_b', 'new_m_sg_w', 'new_m_sg_b', 'new_m_w_mem_k', 'new_m_w_mem_v', 'new_m_w_out', 'new_m_ln_g', 'new_m_ln_b', 'new_v_w_in', 'new_v_q_norm_g', 'new_v_w_uq', 'new_v_kv_norm_g', 'new_v_w_ukv', 'new_v_sg_ln_g', 'new_v_sg_ln_b', 'new_v_sg_w', 'new_v_sg_b', 'new_v_w_mem_k', 'new_v_w_mem_v', 'new_v_w_out', 'new_v_ln_g', 'new_v_ln_b']
TWIN_LEAF_KINDS = {'loss': 'loss', 'grad_x': 'grad_x', 'grad_w_in': 'grad_w', 'grad_q_norm_g': 'grad_w', 'grad_w_uq': 'grad_w', 'grad_kv_norm_g': 'grad_w', 'grad_w_ukv': 'grad_w', 'grad_sg_ln_g': 'grad_w', 'grad_sg_ln_b': 'grad_w', 'grad_sg_w': 'grad_w', 'grad_sg_b': 'grad_w', 'grad_w_mem_k': 'grad_w', 'grad_w_mem_v': 'grad_w', 'grad_w_out': 'grad_w', 'grad_ln_g': 'grad_w', 'grad_ln_b': 'grad_w', 'delta_w_in': 'delta_w', 'delta_q_norm_g': 'delta_w', 'delta_w_uq': 'delta_w', 'delta_kv_norm_g': 'delta_w', 'delta_w_ukv': 'delta_w', 'delta_sg_ln_g': 'delta_w', 'delta_sg_ln_b': 'delta_w', 'delta_sg_w': 'delta_w', 'delta_sg_b': 'delta_w', 'delta_w_mem_k': 'delta_w', 'delta_w_mem_v': 'delta_w', 'delta_w_out': 'delta_w', 'delta_ln_g': 'delta_w', 'delta_ln_b': 'delta_w', 'new_m_w_in': 'new_m', 'new_m_q_norm_g': 'new_m', 'new_m_w_uq': 'new_m', 'new_m_kv_norm_g': 'new_m', 'new_m_w_ukv': 'new_m', 'new_m_sg_ln_g': 'new_m', 'new_m_sg_ln_b': 'new_m', 'new_m_sg_w': 'new_m', 'new_m_sg_b': 'new_m', 'new_m_w_mem_k': 'new_m', 'new_m_w_mem_v': 'new_m', 'new_m_w_out': 'new_m', 'new_m_ln_g': 'new_m', 'new_m_ln_b': 'new_m', 'new_v_w_in': 'new_v', 'new_v_q_norm_g': 'new_v', 'new_v_w_uq': 'new_v', 'new_v_kv_norm_g': 'new_v', 'new_v_w_ukv': 'new_v', 'new_v_sg_ln_g': 'new_v', 'new_v_sg_ln_b': 'new_v', 'new_v_sg_w': 'new_v', 'new_v_sg_b': 'new_v', 'new_v_w_mem_k': 'new_v', 'new_v_w_mem_v': 'new_v', 'new_v_w_out': 'new_v', 'new_v_ln_g': 'new_v', 'new_v_ln_b': 'new_v'}


def _forward(args):
    return _fwd_reference(*[args[k] for k in FWD_PARAMS])


def _output_shape():
    def fwd():
        inp = _fwd_setup_inputs(0)
        return _fwd_reference(*[inp[k] for k in FWD_PARAMS])
    out = _jax.eval_shape(fwd)
    return out.shape, out.dtype

N_MICROBATCH = 1
ADAM_LR = 0.001
ADAM_B1 = 0.9
ADAM_B2 = 0.999
ADAM_EPS = 1e-08
ADAM_WD = 0.01
ADAM_STEP = 10
PER_EXAMPLE_BATCH_AXIS = {'x': 0, 'mem': 0, 'positions': 0, 'loss_target': 0}
SHARED_INPUTS = []
_WEIGHT_DTYPES = {'w_in': _jnp.float32, 'q_norm_g': _jnp.float32, 'w_uq': _jnp.float32, 'kv_norm_g': _jnp.float32, 'w_ukv': _jnp.float32, 'sg_ln_g': _jnp.float32, 'sg_ln_b': _jnp.float32, 'sg_w': _jnp.float32, 'sg_b': _jnp.float32, 'w_mem_k': _jnp.float32, 'w_mem_v': _jnp.float32, 'w_out': _jnp.float32, 'ln_g': _jnp.float32, 'ln_b': _jnp.float32}
MOMENT_SCALE = {'w_in': 1.002379e-02, 'q_norm_g': 3.337038e-03, 'w_uq': 2.246014e-03, 'kv_norm_g': 7.175563e-03, 'w_ukv': 2.612092e-03, 'sg_ln_g': 1.090239e-02, 'sg_ln_b': 1.065476e-02, 'sg_w': 1.063569e-02, 'sg_b': 1.232342e-02, 'w_mem_k': 1.941341e-03, 'w_mem_v': 1.987728e-03, 'w_out': 2.442818e-02, 'ln_g': 1.599035e+01, 'ln_b': 3.569020e-01}


def _to_microbatches(a, axis):
    t = _jnp.moveaxis(a, axis, 0)
    t = t.reshape((N_MICROBATCH, t.shape[0] // N_MICROBATCH) + t.shape[1:])
    return _jnp.moveaxis(t, 1, axis + 1)


def setup_inputs(seed: int = 0) -> dict:
    inp = _fwd_setup_inputs(seed)
    key = _jax.random.fold_in(_jax.random.key(seed), 7919)
    shape, _ = _output_shape()
    out = dict(inp)
    out["loss_target"] = _jax.random.normal(_jax.random.fold_in(key, 0), shape, _jnp.float32)
    for i, name in enumerate(TWIN_WEIGHTS):
        w = inp[name].astype(_jnp.float32)
        if MOMENT_SCALE is None:
            s = _jnp.sqrt(_jnp.mean(_jnp.square(w)) + 1e-30)
        else:
            s = MOMENT_SCALE[name]
        km, kv = _jax.random.split(_jax.random.fold_in(key, i + 1))
        out[name] = w
        out["m_" + name] = s * _jax.random.normal(km, w.shape, _jnp.float32)
        out["v_" + name] = (s * s) * _jax.random.uniform(kv, w.shape, _jnp.float32, 0.5, 1.5)
    if N_MICROBATCH > 1:
        for name, axis in PER_EXAMPLE_BATCH_AXIS.items():
            out[name] = _to_microbatches(out[name], axis)
    return {'x': out['x'], 'mem': out['mem'], 'positions': out['positions'], 'w_in': out['w_in'], 'q_norm_g': out['q_norm_g'], 'w_uq': out['w_uq'], 'kv_norm_g': out['kv_norm_g'], 'w_ukv': out['w_ukv'], 'sg_ln_g': out['sg_ln_g'], 'sg_ln_b': out['sg_ln_b'], 'sg_w': out['sg_w'], 'sg_b': out['sg_b'], 'w_mem_k': out['w_mem_k'], 'w_mem_v': out['w_mem_v'], 'w_out': out['w_out'], 'ln_g': out['ln_g'], 'ln_b': out['ln_b'], 'loss_target': out['loss_target'], 'm_w_in': out['m_w_in'], 'm_q_norm_g': out['m_q_norm_g'], 'm_w_uq': out['m_w_uq'], 'm_kv_norm_g': out['m_kv_norm_g'], 'm_w_ukv': out['m_w_ukv'], 'm_sg_ln_g': out['m_sg_ln_g'], 'm_sg_ln_b': out['m_sg_ln_b'], 'm_sg_w': out['m_sg_w'], 'm_sg_b': out['m_sg_b'], 'm_w_mem_k': out['m_w_mem_k'], 'm_w_mem_v': out['m_w_mem_v'], 'm_w_out': out['m_w_out'], 'm_ln_g': out['m_ln_g'], 'm_ln_b': out['m_ln_b'], 'v_w_in': out['v_w_in'], 'v_q_norm_g': out['v_q_norm_g'], 'v_w_uq': out['v_w_uq'], 'v_kv_norm_g': out['v_kv_norm_g'], 'v_w_ukv': out['v_w_ukv'], 'v_sg_ln_g': out['v_sg_ln_g'], 'v_sg_ln_b': out['v_sg_ln_b'], 'v_sg_w': out['v_sg_w'], 'v_sg_b': out['v_sg_b'], 'v_w_mem_k': out['v_w_mem_k'], 'v_w_mem_v': out['v_w_mem_v'], 'v_w_out': out['v_w_out'], 'v_ln_g': out['v_ln_g'], 'v_ln_b': out['v_ln_b']}


def _loss(weights, diff, rest, loss_target):
    with _jax.named_scope("forward"):
        args = {**rest, TWIN_DIFF_INPUT: diff, **{k: w.astype(_WEIGHT_DTYPES[k]) for k, w in weights.items()}}
        y = _forward(args)
    with _jax.named_scope("loss_head"):
        err = _jnp.square(y.astype(_jnp.float32) - loss_target)
        return 0.5 * _jnp.sum(_jnp.mean(err, axis=-1)) if err.ndim else 0.5 * err


def _adamw(w, g, m, v):
    m = ADAM_B1 * m + (1.0 - ADAM_B1) * g
    v = ADAM_B2 * v + (1.0 - ADAM_B2) * _jnp.square(g)
    m_hat = m / (1.0 - ADAM_B1 ** ADAM_STEP)
    v_hat = v / (1.0 - ADAM_B2 ** ADAM_STEP)
    delta = -ADAM_LR * (m_hat / (_jnp.sqrt(v_hat) + ADAM_EPS) + ADAM_WD * w)
    return delta, m, v


def reference(x, mem, positions, w_in, q_norm_g, w_uq, kv_norm_g, w_ukv, sg_ln_g, sg_ln_b, sg_w, sg_b, w_mem_k, w_mem_v, w_out, ln_g, ln_b, loss_target, m_w_in, m_q_norm_g, m_w_uq, m_kv_norm_g, m_w_ukv, m_sg_ln_g, m_sg_ln_b, m_sg_w, m_sg_b, m_w_mem_k, m_w_mem_v, m_w_out, m_ln_g, m_ln_b, v_w_in, v_q_norm_g, v_w_uq, v_kv_norm_g, v_w_ukv, v_sg_ln_g, v_sg_ln_b, v_sg_w, v_sg_b, v_w_mem_k, v_w_mem_v, v_w_out, v_ln_g, v_ln_b):
    given = dict(x=x, mem=mem, positions=positions, w_in=w_in, q_norm_g=q_norm_g, w_uq=w_uq, kv_norm_g=kv_norm_g, w_ukv=w_ukv, sg_ln_g=sg_ln_g, sg_ln_b=sg_ln_b, sg_w=sg_w, sg_b=sg_b, w_mem_k=w_mem_k, w_mem_v=w_mem_v, w_out=w_out, ln_g=ln_g, ln_b=ln_b, loss_target=loss_target, m_w_in=m_w_in, m_q_norm_g=m_q_norm_g, m_w_uq=m_w_uq, m_kv_norm_g=m_kv_norm_g, m_w_ukv=m_w_ukv, m_sg_ln_g=m_sg_ln_g, m_sg_ln_b=m_sg_ln_b, m_sg_w=m_sg_w, m_sg_b=m_sg_b, m_w_mem_k=m_w_mem_k, m_w_mem_v=m_w_mem_v, m_w_out=m_w_out, m_ln_g=m_ln_g, m_ln_b=m_ln_b, v_w_in=v_w_in, v_q_norm_g=v_q_norm_g, v_w_uq=v_w_uq, v_kv_norm_g=v_kv_norm_g, v_w_ukv=v_w_ukv, v_sg_ln_g=v_sg_ln_g, v_sg_ln_b=v_sg_ln_b, v_sg_w=v_sg_w, v_sg_b=v_sg_b, v_w_mem_k=v_w_mem_k, v_w_mem_v=v_w_mem_v, v_w_out=v_w_out, v_ln_g=v_ln_g, v_ln_b=v_ln_b)
    weights = {n: given[n] for n in TWIN_WEIGHTS}
    shared = {n: given[n] for n in SHARED_INPUTS}
    per_example = {n: given[n] for n in ['x', 'mem', 'positions']}
    grad_fn = _jax.value_and_grad(_loss, argnums=(0, 1))

    def one_microbatch(ex, loss_target):
        ex = dict(ex)
        diff = ex.pop(TWIN_DIFF_INPUT)
        return grad_fn(weights, diff, {**shared, **ex}, loss_target)

    if N_MICROBATCH == 1:
        loss, (grad_w, grad_x) = one_microbatch(per_example, given["loss_target"])
    else:
        def body(carry, xs):
            loss_sum, grad_sum = carry
            l_k, (gw_k, gx_k) = one_microbatch(xs[0], xs[1])
            with _jax.named_scope("update"):
                return (loss_sum + l_k, _jax.tree.map(_jnp.add, grad_sum, gw_k)), gx_k

        init = (_jnp.zeros((), _jnp.float32), _jax.tree.map(_jnp.zeros_like, weights))
        (loss, grad_w), grad_x = _jax.lax.scan(body, init, (per_example, given["loss_target"]))
    with _jax.named_scope("update"):
        delta_w, new_m, new_v = {}, {}, {}
        for n in TWIN_WEIGHTS:
            delta_w[n], new_m[n], new_v[n] = _adamw(weights[n], grad_w[n], given["m_" + n], given["v_" + n])
    return (loss, grad_x, *[grad_w[n] for n in TWIN_WEIGHTS], *[delta_w[n] for n in TWIN_WEIGHTS],
            *[new_m[n] for n in TWIN_WEIGHTS], *[new_v[n] for n in TWIN_WEIGHTS])
```

```python
import math

import jax
import jax.numpy as jnp
from jax import lax
from jax.experimental import pallas as pl
from jax.experimental.pallas import tpu as pltpu

F32, BF16 = jnp.float32, jnp.bfloat16

D_MODEL = 2048
DEPTH = 4
CHUNK = 64
MLA_HEADS = 6
MLA_SCALE = 1.0 / math.sqrt(192.0)
SB_HEADS = 4
SB_SCALE = 1.0 / math.sqrt(128.0)
MEM_HEADS = 4
MEM_SCALE = 1.0 / math.sqrt(64.0)
ROPE_THETA = 10000.0
ALPHA = (2.0 * DEPTH) ** 0.25
LN_EPS = 1e-5
RMS_EPS = 1e-6
ADAM_LR, ADAM_B1, ADAM_B2, ADAM_EPS, ADAM_WD, ADAM_STEP = 0.001, 0.9, 0.999, 1e-08, 0.01, 10

ORIG = dict(c_q=(0, 512), c_kv=(512, 256), k_pe=(768, 64), g_a=(832, 768), sg_u=(1600, 512), sg_v=(2112, 512),
            g_b=(2624, 512), sb_q=(3136, 512), sb_k=(3648, 512), sb_v=(4160, 512), g_c=(4672, 512),
            m_q=(5184, 256), g_m=(5440, 256))
D_IN = 5696
PERM_ORDER = ("c_q", "c_kv", "m_q", "sg_u", "sg_v", "sb_q", "sb_k", "sb_v", "g_a", "g_b", "g_c", "g_m", "k_pe")
HP = 5760
CQ, CKV, MQ, SGU, SGV, SBQ, GATE, KPE = 0, 512, 768, 1024, 1536, 2048, 3584, 5632

ATT_BLK = 256
LANE = 128
VMEM_LIMIT = 56 * 1024 * 1024

FLAT_W = 1024
SHARDED = ("w_in", "w_uq", "w_ukv", "w_mem_k", "w_mem_v", "w_out")
SMALL = ("q_norm_g", "kv_norm_g", "sg_ln_g", "sg_ln_b", "sg_w", "sg_b", "ln_g", "ln_b")


def _params(sem=None):
    return pltpu.CompilerParams(dimension_semantics=sem, vmem_limit_bytes=VMEM_LIMIT)


def _tile(dim, pref):
    if dim <= pref:
        return dim
    t = (pref // LANE) * LANE
    while t >= LANE:
        if dim % t == 0:
            return t
        t -= LANE
    return dim


def _row_tile(rows, bytes_per_row, budget=8 << 20):
    best = None
    for t in range(8, rows + 1, 8):
        if rows % t == 0 and t * bytes_per_row <= budget:
            best = t
    return best if best else rows


def _dot_nt(a, b):
    return lax.dot_general(a, b, (((1,), (1,)), ((), ())), preferred_element_type=F32)


def _dot_tn(a, b):
    return lax.dot_general(a, b, (((0,), (0,)), ((), ())), preferred_element_type=F32)


def _dot(a, b):
    return jnp.dot(a, b, preferred_element_type=F32)


def _mm(a, b, *, ta=False, tb=False, a_win=None, b_win=None, add=None, add_scale=1.0, out_dtype=F32,
        tm=512, tn=512, tk=512, name="mm"):
    a_off, a_w = a_win if a_win else (0, a.shape[1])
    b_off, b_w = b_win if b_win else (0, b.shape[1])
    (K, M) = (a.shape[0], a_w) if ta else (a_w, a.shape[0])
    (N, Kb) = (b.shape[0], b_w) if tb else (b_w, b.shape[0])
    assert K == Kb, (a.shape, b.shape, ta, tb)
    tm, tn, tk = _tile(M, tm), _tile(N, tn), _tile(K, tk)
    nk = K // tk
    if ta:
        assert a_off % tm == 0
        a_spec = pl.BlockSpec((tk, tm), lambda i, j, k: (k, i + a_off // tm))
    else:
        assert a_off % tk == 0
        a_spec = pl.BlockSpec((tm, tk), lambda i, j, k: (i, k + a_off // tk))
    if tb:
        assert b_off % tk == 0
        b_spec = pl.BlockSpec((tn, tk), lambda i, j, k: (j, k + b_off // tk))
    else:
        assert b_off % tn == 0
        b_spec = pl.BlockSpec((tk, tn), lambda i, j, k: (k, j + b_off // tn))
    o_spec = pl.BlockSpec((tm, tn), lambda i, j, k: (i, j))
    dn = (((0 if ta else 1,), (1 if tb else 0,)), ((), ()))
    has_add = add is not None

    def body(*refs):
        if has_add:
            a_ref, b_ref, add_ref, o_ref, acc_ref = refs
        else:
            a_ref, b_ref, o_ref, acc_ref = refs
        k = pl.program_id(2)

        @pl.when(k == 0)
        def _():
            acc_ref[...] = jnp.zeros_like(acc_ref)

        acc_ref[...] += lax.dot_general(a_ref[...].astype(BF16), b_ref[...].astype(BF16), dn,
                                        preferred_element_type=F32)

        @pl.when(k == nk - 1)
        def _():
            r = acc_ref[...]
            if has_add:
                r = r + add_scale * add_ref[...]
            o_ref[...] = r.astype(o_ref.dtype)

    ins = [a, b] + ([add] if has_add else [])
    specs = [a_spec, b_spec] + ([o_spec] if has_add else [])
    return pl.pallas_call(
        body, name=name, grid=(M // tm, N // tn, nk), in_specs=specs, out_specs=o_spec,
        out_shape=jax.ShapeDtypeStruct((M, N), out_dtype), scratch_shapes=[pltpu.VMEM((tm, tn), F32)],
        compiler_params=_params(("parallel", "parallel", "arbitrary")))(*ins)


GELU_K = math.sqrt(2.0 / math.pi)


def _gelu(x):
    t = jnp.tanh(GELU_K * (x + 0.044715 * (x * x * x)))
    return 0.5 * x * (1.0 + t)


def _gelu_grad(x):
    t = jnp.tanh(GELU_K * (x + 0.044715 * (x * x * x)))
    return 0.5 * (1.0 + t) + 0.5 * x * (1.0 - t * t) * GELU_K * (1.0 + 3.0 * 0.044715 * x * x)


def _rope_swap(t):
    lane = lax.broadcasted_iota(jnp.int32, t.shape, 1)
    return jnp.where(lane < 32, pltpu.roll(t, 96, axis=1), pltpu.roll(t, 32, axis=1))


def _rope(t, c, s):
    return t * c + _rope_swap(t) * s


def _rope_bwd(dt, c, s):
    return dt * c - _rope_swap(dt) * s


def _row_spec(tm, w, cb=0):
    return pl.BlockSpec((tm, w), lambda i: (i, cb))


def _fix_spec(shape):
    return pl.BlockSpec(shape, lambda *_: (0,) * len(shape))


def _rms_fwd(h, off, width, g, name):
    S = h.shape[0]
    tm = _tile(S, 512)

    def body(x_ref, g_ref, o_ref):
        x = x_ref[...]
        r = lax.rsqrt(jnp.mean(x * x, axis=1, keepdims=True) + RMS_EPS)
        o_ref[...] = (x * r * g_ref[...]).astype(BF16)

    return pl.pallas_call(
        body, name=name, grid=(S // tm,), in_specs=[_row_spec(tm, width, off // width), _fix_spec((1, width))],
        out_specs=_row_spec(tm, width), out_shape=jax.ShapeDtypeStruct((S, width), BF16),
        compiler_params=_params(("parallel",)))(h, g.reshape(1, width))


def _rms_bwd(h, off, width, g, dxn, name):
    S = h.shape[0]
    tm = _tile(S, 512)

    def body(x_ref, g_ref, d_ref, dx_ref, dg_ref):
        @pl.when(pl.program_id(0) == 0)
        def _():
            dg_ref[...] = jnp.zeros_like(dg_ref)

        x, d = x_ref[...], d_ref[...]
        r = lax.rsqrt(jnp.mean(x * x, axis=1, keepdims=True) + RMS_EPS)
        gd = d * g_ref[...]
        dx_ref[...] = gd * r - x * (r * r * r) * jnp.mean(gd * x, axis=1, keepdims=True)
        dg_ref[...] += jnp.sum(d * x * r, axis=0, keepdims=True)

    return pl.pallas_call(
        body, name=name, grid=(S // tm,),
        in_specs=[_row_spec(tm, width, off // width), _fix_spec((1, width)), _row_spec(tm, width)],
        out_specs=[_row_spec(tm, width), _fix_spec((1, width))],
        out_shape=[jax.ShapeDtypeStruct((S, width), F32), jax.ShapeDtypeStruct((1, width), F32)],
        compiler_params=_params(("arbitrary",)))(h, g.reshape(1, width), dxn)


def _q_proj(xn, w, rc, rs):
    S = xn.shape[0]
    tm = _tile(S, 512)

    def body(x_ref, w_ref, c_ref, s_ref, q_ref):
        q = _dot(x_ref[...], w_ref[...])
        q_ref[:, :LANE] = q[:, :LANE].astype(BF16)
        q_ref[:, LANE:] = _rope(q[:, LANE:], c_ref[...], s_ref[...]).astype(BF16)

    return pl.pallas_call(
        body, name="q_proj", grid=(S // tm, MLA_HEADS),
        in_specs=[pl.BlockSpec((tm, 512), lambda i, j: (i, 0)), pl.BlockSpec((512, 256), lambda i, j: (0, j)),
                  pl.BlockSpec((tm, LANE), lambda i, j: (i, 0)), pl.BlockSpec((tm, LANE), lambda i, j: (i, 0))],
        out_specs=pl.BlockSpec((tm, 256), lambda i, j: (i, j)),
        out_shape=jax.ShapeDtypeStruct((S, MLA_HEADS * 256), BF16),
        compiler_params=_params(("parallel", "parallel")))(xn, w, rc, rs)


def _kv_proj(xn, w, h, rc, rs):
    S = xn.shape[0]
    tm = _tile(S, 512)

    def body(x_ref, w_ref, pe_ref, c_ref, s_ref, k_ref, v_ref):
        kv = _dot(x_ref[...], w_ref[...])
        k_ref[:, :LANE] = kv[:, :LANE].astype(BF16)
        k_ref[:, LANE:] = _rope(pe_ref[...], c_ref[...], s_ref[...]).astype(BF16)
        v_ref[...] = kv[:, LANE:].astype(BF16)

    return pl.pallas_call(
        body, name="kv_proj", grid=(S // tm, MLA_HEADS),
        in_specs=[pl.BlockSpec((tm, 256), lambda i, j: (i, 0)), pl.BlockSpec((256, 256), lambda i, j: (0, j)),
                  pl.BlockSpec((tm, LANE), lambda i, j: (i, KPE // LANE)),
                  pl.BlockSpec((tm, LANE), lambda i, j: (i, 0)), pl.BlockSpec((tm, LANE), lambda i, j: (i, 0))],
        out_specs=[pl.BlockSpec((tm, 256), lambda i, j: (i, j)), pl.BlockSpec((tm, LANE), lambda i, j: (i, j))],
        out_shape=[jax.ShapeDtypeStruct((S, MLA_HEADS * 256), BF16), jax.ShapeDtypeStruct((S, MLA_HEADS * LANE), BF16)],
        compiler_params=_params(("parallel", "parallel")))(xn, w, h, rc, rs)


def _q_rope_bwd(dq, rc, rs):
    S = dq.shape[0]
    tm = _tile(S, 512)

    def body(d_ref, c_ref, s_ref, o_ref):
        o_ref[:, :LANE] = d_ref[:, :LANE].astype(BF16)
        o_ref[:, LANE:] = _rope_bwd(d_ref[:, LANE:], c_ref[...], s_ref[...]).astype(BF16)

    return pl.pallas_call(
        body, name="q_rope_bwd", grid=(S // tm, MLA_HEADS),
        in_specs=[pl.BlockSpec((tm, 256), lambda i, j: (i, j)),
                  pl.BlockSpec((tm, LANE), lambda i, j: (i, 0)), pl.BlockSpec((tm, LANE), lambda i, j: (i, 0))],
        out_specs=pl.BlockSpec((tm, 256), lambda i, j: (i, j)),
        out_shape=jax.ShapeDtypeStruct((S, MLA_HEADS * 256), BF16),
        compiler_params=_params(("parallel", "parallel")))(dq, rc, rs)


def _kv_bwd_prep(dk, dv, rc, rs):
    S = dk.shape[1]
    tm = _tile(S, 512)

    def body(dk_ref, dv_ref, c_ref, s_ref, o_ref, pe_ref):
        rot = jnp.zeros((tm, LANE), F32)
        for hh in range(MLA_HEADS):
            o_ref[:, hh * 256:hh * 256 + LANE] = dk_ref[hh, :, :LANE].astype(BF16)
            o_ref[:, hh * 256 + LANE:(hh + 1) * 256] = dv_ref[hh].astype(BF16)
            rot = rot + dk_ref[hh, :, LANE:]
        pe_ref[...] = _rope_bwd(rot, c_ref[...], s_ref[...])

    return pl.pallas_call(
        body, name="kv_bwd_prep", grid=(S // tm,),
        in_specs=[pl.BlockSpec((MLA_HEADS, tm, 256), lambda i: (0, i, 0)),
                  pl.BlockSpec((MLA_HEADS, tm, LANE), lambda i: (0, i, 0)), _row_spec(tm, LANE), _row_spec(tm, LANE)],
        out_specs=[_row_spec(tm, MLA_HEADS * 256), _row_spec(tm, LANE)],
        out_shape=[jax.ShapeDtypeStruct((S, MLA_HEADS * 256), BF16), jax.ShapeDtypeStruct((S, LANE), F32)],
        compiler_params=_params(("parallel",)))(dk, dv, rc, rs)


def _chunk_mask(T):
    row = lax.broadcasted_iota(jnp.int32, (T, T), 0)
    col = lax.broadcasted_iota(jnp.int32, (T, T), 1)
    return (col // CHUNK) <= (row // CHUNK)


def _mla_fwd(q, kp, v):
    S = q.shape[0]
    T = min(ATT_BLK, S)

    def body(q_ref, k_ref, v_ref, o_ref, lse_ref):
        i = pl.program_id(1)
        qb = q_ref[...]

        def scores(j):
            kb = k_ref[pl.ds(pl.multiple_of(j * T, T), T), :]
            return _dot_nt(qb, kb) * MLA_SCALE

        def update(carry, s, j):
            m, l, acc = carry
            m_new = jnp.maximum(m, jnp.max(s, axis=1, keepdims=True))
            a = jnp.exp(m - m_new)
            p = jnp.exp(s - m_new)
            vb = v_ref[pl.ds(pl.multiple_of(j * T, T), T), :]
            return m_new, a * l + jnp.sum(p, axis=1, keepdims=True), a * acc + _dot(p.astype(BF16), vb)

        init = (jnp.full((T, 1), -1e30, F32), jnp.zeros((T, 1), F32), jnp.zeros((T, LANE), F32))
        carry = lax.fori_loop(0, i, lambda j, c: update(c, scores(j), j), init)
        m, l, acc = update(carry, jnp.where(_chunk_mask(T), scores(i), -1e30), i)
        o_ref[...] = acc / l
        lse_ref[...] = jnp.broadcast_to(m + jnp.log(l), (T, LANE))

    return pl.pallas_call(
        body, name="mla_fwd", grid=(MLA_HEADS, S // T),
        in_specs=[pl.BlockSpec((T, 256), lambda h, i: (i, h)), pl.BlockSpec((S, 256), lambda h, i: (0, h)),
                  pl.BlockSpec((S, LANE), lambda h, i: (0, h))],
        out_specs=[pl.BlockSpec((T, LANE), lambda h, i: (i, h)), pl.BlockSpec((T, LANE), lambda h, i: (i, h))],
        out_shape=[jax.ShapeDtypeStruct((S, MLA_HEADS * LANE), F32), jax.ShapeDtypeStruct((S, MLA_HEADS * LANE), F32)],
        compiler_params=_params(("parallel", "arbitrary")))(q, kp, v)


def _mla_bwd(q, kp, v, do_cat, o_cat, lse):
    S = q.shape[0]
    T = min(ATT_BLK, S)
    nq = S // T

    def body(q_ref, k_ref, v_ref, do_ref, o_ref, lse_ref, dq_ref, dk_hbm, dv_hbm, dk_acc, dv_acc):
        h, i = pl.program_id(0), pl.program_id(1)

        @pl.when(i == 0)
        def _():
            dk_acc[...] = jnp.zeros_like(dk_acc)
            dv_acc[...] = jnp.zeros_like(dv_acc)

        qb = q_ref[...]
        do32 = do_ref[...]
        dob = do32.astype(BF16)
        delta = jnp.sum(do32 * o_ref[...], axis=1, keepdims=True)
        lse_col = lse_ref[:, :1]

        def blk(j, dq, diag):
            sl = pl.ds(pl.multiple_of(j * T, T), T)
            kb, vb = k_ref[sl, :], v_ref[sl, :]
            s = _dot_nt(qb, kb) * MLA_SCALE
            if diag:
                s = jnp.where(_chunk_mask(T), s, -1e30)
            p = jnp.exp(s - lse_col)
            dp = _dot_nt(dob, vb)
            ds = (p * (dp - delta) * MLA_SCALE).astype(BF16)
            dk_acc[sl, :] += _dot_tn(ds, qb)
            dv_acc[sl, :] += _dot_tn(p.astype(BF16), dob)
            return dq + _dot(ds, kb)

        dq = lax.fori_loop(0, i, lambda j, c: blk(j, c, False), jnp.zeros((T, 256), F32))
        dq_ref[...] = blk(i, dq, True)

        @pl.when(i == nq - 1)
        def _():
            pltpu.sync_copy(dk_acc, dk_hbm.at[h])
            pltpu.sync_copy(dv_acc, dv_hbm.at[h])

    any_spec = pl.BlockSpec(memory_space=pl.ANY)
    return pl.pallas_call(
        body, name="mla_bwd", grid=(MLA_HEADS, nq),
        in_specs=[pl.BlockSpec((T, 256), lambda h, i: (i, h)), pl.BlockSpec((S, 256), lambda h, i: (0, h)),
                  pl.BlockSpec((S, LANE), lambda h, i: (0, h)), pl.BlockSpec((T, LANE), lambda h, i: (i, h)),
                  pl.BlockSpec((T, LANE), lambda h, i: (i, h)), pl.BlockSpec((T, LANE), lambda h, i: (i, h))],
        out_specs=[pl.BlockSpec((T, 256), lambda h, i: (i, h)), any_spec, any_spec],
        out_shape=[jax.ShapeDtypeStruct((S, MLA_HEADS * 256), F32), jax.ShapeDtypeStruct((MLA_HEADS, S, 256), F32),
                   jax.ShapeDtypeStruct((MLA_HEADS, S, LANE), F32)],
        scratch_shapes=[pltpu.VMEM((S, 256), F32), pltpu.VMEM((S, LANE), F32)],
        compiler_params=_params(("arbitrary", "arbitrary")))(q, kp, v, do_cat, o_cat, lse)


def _split_dot(x, tri):
    hi = x.astype(BF16)
    lo = (x - hi.astype(F32)).astype(BF16)
    return _dot(hi, tri) + _dot(lo, tri)


def _sb_block(qb, kb, tri, carry, diag, T):
    z = _dot_nt(qb, kb) * SB_SCALE
    lb = jnp.minimum(z, 0.0) - jnp.log(1.0 + jnp.exp(-jnp.abs(z)))
    lm = lb - z
    if diag:
        row = lax.broadcasted_iota(jnp.int32, (T, T), 0)
        col = lax.broadcasted_iota(jnp.int32, (T, T), 1)
        strict = col < row
        lm = jnp.where(strict, lm, 0.0)
    a = jnp.exp(lb + carry + _split_dot(lm, tri))
    if diag:
        a = jnp.where(strict, a, 0.0)
    return a, lb, lm


def _sb_fwd(qkv):
    S = qkv.shape[0]
    T = min(ATT_BLK, S)

    def body(q_ref, k_ref, v_ref, o_ref):
        i = pl.program_id(1)
        qb = q_ref[...]
        row = lax.broadcasted_iota(jnp.int32, (T, T), 0)
        col = lax.broadcasted_iota(jnp.int32, (T, T), 1)
        tri = (row > col).astype(BF16)

        def blk(j, carry, acc, diag):
            sl = pl.ds(pl.multiple_of(j * T, T), T)
            a, _, lm = _sb_block(qb, k_ref[sl, :], tri, carry, diag, T)
            return carry + jnp.sum(lm, axis=1, keepdims=True), acc + _dot(a.astype(BF16), v_ref[sl, :])

        carry, acc = blk(i, jnp.zeros((T, 1), F32), jnp.zeros((T, LANE), F32), True)
        carry, acc = lax.fori_loop(0, i, lambda t, c: blk(i - 1 - t, c[0], c[1], False), (carry, acc))
        o_ref[...] = acc

    return pl.pallas_call(
        body, name="sb_fwd", grid=(SB_HEADS, S // T),
        in_specs=[pl.BlockSpec((T, LANE), lambda h, i: (i, h)), pl.BlockSpec((S, LANE), lambda h, i: (0, 4 + h)),
                  pl.BlockSpec((S, LANE), lambda h, i: (0, 8 + h))],
        out_specs=pl.BlockSpec((T, LANE), lambda h, i: (i, h)),
        out_shape=jax.ShapeDtypeStruct((S, SB_HEADS * LANE), F32),
        compiler_params=_params(("parallel", "arbitrary")))(qkv, qkv, qkv)


def _sb_bwd(qkv, do_cat, o_cat, col0):
    S = qkv.shape[0]
    T = min(ATT_BLK, S)
    nq = S // T

    def body(q_ref, k_ref, v_ref, do_ref, o_ref, dq_ref, dk_hbm, dv_hbm, dk_acc, dv_acc):
        h, i = pl.program_id(0), pl.program_id(1)

        @pl.when(i == 0)
        def _():
            dk_acc[...] = jnp.zeros_like(dk_acc)
            dv_acc[...] = jnp.zeros_like(dv_acc)

        qb = q_ref[...]
        dob = do_ref[...].astype(BF16)
        total = jnp.sum(dob.astype(F32) * o_ref[...], axis=1, keepdims=True)
        row = lax.broadcasted_iota(jnp.int32, (T, T), 0)
        col = lax.broadcasted_iota(jnp.int32, (T, T), 1)
        tri = (row > col).astype(BF16)
        tri_incl = (row >= col).astype(BF16)

        def blk(j, carry, ecarry, dq, diag):
            sl = pl.ds(pl.multiple_of(j * T, T), T)
            kb, vb = k_ref[sl, :], v_ref[sl, :]
            a, lb, lm = _sb_block(qb, kb, tri, carry, diag, T)
            ab = a.astype(BF16)
            e = ab.astype(F32) * _dot_nt(dob, vb)
            prefix = total - ecarry - _split_dot(e, tri_incl)
            b = jnp.exp(lb)
            dz = e * (1.0 - b) - b * prefix
            if diag:
                dz = jnp.where(col < row, dz, 0.0)
            dzb = (dz * SB_SCALE).astype(BF16)
            dk_acc[sl, :] += _dot_tn(dzb, qb)
            dv_acc[sl, :] += _dot_tn(ab, dob)
            return (carry + jnp.sum(lm, axis=1, keepdims=True), ecarry + jnp.sum(e, axis=1, keepdims=True),
                    dq + _dot(dzb, kb))

        zc = jnp.zeros((T, 1), F32)
        state = blk(i, zc, zc, jnp.zeros((T, LANE), F32), True)
        state = lax.fori_loop(0, i, lambda t, c: blk(i - 1 - t, c[0], c[1], c[2], False), state)
        dq_ref[...] = state[2]

        @pl.when(i == nq - 1)
        def _():
            pltpu.sync_copy(dk_acc, dk_hbm.at[h])
            pltpu.sync_copy(dv_acc, dv_hbm.at[h])

    any_spec = pl.BlockSpec(memory_space=pl.ANY)
    return pl.pallas_call(
        body, name="sb_bwd", grid=(SB_HEADS, nq),
        in_specs=[pl.BlockSpec((T, LANE), lambda h, i: (i, h)), pl.BlockSpec((S, LANE), lambda h, i: (0, 4 + h)),
                  pl.BlockSpec((S, LANE), lambda h, i: (0, 8 + h)),
                  pl.BlockSpec((T, LANE), lambda h, i: (i, col0 + h)), pl.BlockSpec((T, LANE), lambda h, i: (i, col0 + h))],
        out_specs=[pl.BlockSpec((T, LANE), lambda h, i: (i, h)), any_spec, any_spec],
        out_shape=[jax.ShapeDtypeStruct((S, SB_HEADS * LANE), F32), jax.ShapeDtypeStruct((SB_HEADS, S, LANE), F32),
                   jax.ShapeDtypeStruct((SB_HEADS, S, LANE), F32)],
        scratch_shapes=[pltpu.VMEM((S, LANE), F32), pltpu.VMEM((S, LANE), F32)],
        compiler_params=_params(("arbitrary", "arbitrary")))(qkv, qkv, qkv, do_cat, o_cat)


def _mem_probs(q, k_ref, hh):
    lane = lax.broadcasted_iota(jnp.int32, (1, 256), 1) // 64
    msk = lane == hh
    qh = jnp.where(msk, q, 0.0).astype(BF16)
    s = _dot_nt(qh, k_ref[...]) * MEM_SCALE
    p = jnp.exp(s - jnp.max(s, axis=1, keepdims=True))
    return msk, qh, p / jnp.sum(p, axis=1, keepdims=True)


def _mem_fwd(h, mk, mv):
    S = h.shape[0]
    tm = _tile(S, 512)

    def body(q_ref, k_ref, v_ref, o_ref):
        q = q_ref[...]
        out = jnp.zeros((tm, 256), F32)
        for hh in range(MEM_HEADS):
            msk, _, p = _mem_probs(q, k_ref, hh)
            out = out + jnp.where(msk, _dot(p.astype(BF16), v_ref[...]), 0.0)
        o_ref[...] = out

    return pl.pallas_call(
        body, name="mem_fwd", grid=(S // tm,),
        in_specs=[_row_spec(tm, 256, MQ // 256), _fix_spec((256, 256)), _fix_spec((256, 256))],
        out_specs=_row_spec(tm, 256), out_shape=jax.ShapeDtypeStruct((S, 256), F32),
        compiler_params=_params(("parallel",)))(h, mk, mv)


def _mem_bwd(h, mk, mv, do_cat, col0):
    S = h.shape[0]
    tm = _tile(S, 512)

    def body(q_ref, k_ref, v_ref, do_ref, dq_ref, dk_ref, dv_ref):
        @pl.when(pl.program_id(0) == 0)
        def _():
            dk_ref[...] = jnp.zeros_like(dk_ref)
            dv_ref[...] = jnp.zeros_like(dv_ref)

        q, do = q_ref[...], do_ref[...]
        dq = jnp.zeros((tm, 256), F32)
        for hh in range(MEM_HEADS):
            msk, qh, p = _mem_probs(q, k_ref, hh)
            doh = jnp.where(msk, do, 0.0).astype(BF16)
            dp = _dot_nt(doh, v_ref[...])
            ds = (p * (dp - jnp.sum(p * dp, axis=1, keepdims=True)) * MEM_SCALE).astype(BF16)
            dq = dq + jnp.where(msk, _dot(ds, k_ref[...]), 0.0)
            dk_ref[...] += _dot_tn(ds, qh)
            dv_ref[...] += _dot_tn(p.astype(BF16), doh)
        dq_ref[...] = dq

    return pl.pallas_call(
        body, name="mem_bwd", grid=(S // tm,),
        in_specs=[_row_spec(tm, 256, MQ // 256), _fix_spec((256, 256)), _fix_spec((256, 256)),
                  _row_spec(tm, 256, col0 // 256)],
        out_specs=[_row_spec(tm, 256), _fix_spec((256, 256)), _fix_spec((256, 256))],
        out_shape=[jax.ShapeDtypeStruct((S, 256), F32), jax.ShapeDtypeStruct((256, 256), F32),
                   jax.ShapeDtypeStruct((256, 256), F32)],
        compiler_params=_params(("arbitrary",)))(h, mk, mv, do_cat)


SG_T = 128


def _sg_norm(sv, g, b):
    gv = _gelu(sv)
    xc = gv - jnp.mean(gv, axis=1, keepdims=True)
    rstd = lax.rsqrt(jnp.mean(xc * xc, axis=1, keepdims=True) + LN_EPS)
    xhat = xc * rstd
    return xhat, rstd, xhat * g + b


def _sg_fwd(h, lng, lnb, w, bias_t):
    S = h.shape[0]
    tm = _tile(S, 512)

    def body(u_ref, v_ref, g_ref, b_ref, w_ref, bias_ref, o_ref):
        mask = _chunk_mask(SG_T)
        for n in range(tm // SG_T):
            rows = slice(n * SG_T, (n + 1) * SG_T)
            u = _gelu(u_ref[rows, :])
            _, _, vn = _sg_norm(v_ref[rows, :], g_ref[...], b_ref[...])
            vb = vn.astype(BF16)
            for gi in range(4):
                cols = slice(gi * LANE, (gi + 1) * LANE)
                wg = jnp.where(mask, w_ref[gi], 0.0).astype(BF16)
                mixed = _dot(wg, vb[:, cols]) + bias_ref[:, gi:gi + 1]
                o_ref[rows, cols] = u[:, cols] * mixed

    return pl.pallas_call(
        body, name="sg_fwd", grid=(S // tm,),
        in_specs=[_row_spec(tm, 512, SGU // 512), _row_spec(tm, 512, SGV // 512), _fix_spec((1, 512)),
                  _fix_spec((1, 512)), _fix_spec((4, SG_T, SG_T)), _fix_spec((SG_T, 4))],
        out_specs=_row_spec(tm, 512), out_shape=jax.ShapeDtypeStruct((S, 512), F32),
        compiler_params=_params(("parallel",)))(h, h, lng.reshape(1, 512), lnb.reshape(1, 512), w, bias_t)


def _sg_bwd(h, lng, lnb, w, bias_t, do_cat, col0):
    S = h.shape[0]
    tm = _tile(S, 512)
    nsteps = S // tm

    def body(u_ref, v_ref, g_ref, b_ref, w_ref, bias_ref, do0_ref, do1_ref, do2_ref, do3_ref,
             du_ref, dv_ref, dw_ref, dbias_ref, dg_ref, db_ref, dvn_scr, dbias_acc):
        do_refs = (do0_ref, do1_ref, do2_ref, do3_ref)
        step = pl.program_id(0)

        @pl.when(step == 0)
        def _():
            dw_ref[...] = jnp.zeros_like(dw_ref)
            dg_ref[...] = jnp.zeros_like(dg_ref)
            db_ref[...] = jnp.zeros_like(db_ref)
            dbias_acc[...] = jnp.zeros_like(dbias_acc)

        mask = _chunk_mask(SG_T)
        for n in range(tm // SG_T):
            rows = slice(n * SG_T, (n + 1) * SG_T)
            su, sv = u_ref[rows, :], v_ref[rows, :]
            u = _gelu(su)
            xhat, rstd, vn = _sg_norm(sv, g_ref[...], b_ref[...])
            vb = vn.astype(BF16)
            ugrad = _gelu_grad(su)
            for gi in range(4):
                cols = slice(gi * LANE, (gi + 1) * LANE)
                do = do_refs[gi][rows, :]
                wg = jnp.where(mask, w_ref[gi], 0.0).astype(BF16)
                mixed = _dot(wg, vb[:, cols]) + bias_ref[:, gi:gi + 1]
                dmixed = do * u[:, cols]
                dmb = dmixed.astype(BF16)
                du_ref[rows, cols] = do * mixed * ugrad[:, cols]
                dvn_scr[:, cols] = _dot_tn(wg, dmb)
                dw_ref[gi] += jnp.where(mask, _dot_nt(dmb, vb[:, cols]), 0.0)
                dbias_acc[gi] += dmixed
            dvn = dvn_scr[...]
            dg_ref[...] += jnp.sum(dvn * xhat, axis=0, keepdims=True)
            db_ref[...] += jnp.sum(dvn, axis=0, keepdims=True)
            dxh = dvn * g_ref[...]
            dgv = rstd * (dxh - jnp.mean(dxh, axis=1, keepdims=True)
                          - xhat * jnp.mean(dxh * xhat, axis=1, keepdims=True))
            dv_ref[rows, :] = dgv * _gelu_grad(sv)

        @pl.when(step == nsteps - 1)
        def _():
            for gi in range(4):
                dbias_ref[:, gi:gi + 1] = jnp.sum(dbias_acc[gi], axis=1, keepdims=True)

    return pl.pallas_call(
        body, name="sg_bwd", grid=(nsteps,),
        in_specs=[_row_spec(tm, 512, SGU // 512), _row_spec(tm, 512, SGV // 512), _fix_spec((1, 512)),
                  _fix_spec((1, 512)), _fix_spec((4, SG_T, SG_T)), _fix_spec((SG_T, 4))]
                 + [_row_spec(tm, LANE, col0 // LANE + gi) for gi in range(4)],
        out_specs=[_row_spec(tm, 512), _row_spec(tm, 512), _fix_spec((4, SG_T, SG_T)), _fix_spec((SG_T, 4)),
                   _fix_spec((1, 512)), _fix_spec((1, 512))],
        out_shape=[jax.ShapeDtypeStruct((S, 512), F32), jax.ShapeDtypeStruct((S, 512), F32),
                   jax.ShapeDtypeStruct((4, SG_T, SG_T), F32), jax.ShapeDtypeStruct((SG_T, 4), F32),
                   jax.ShapeDtypeStruct((1, 512), F32), jax.ShapeDtypeStruct((1, 512), F32)],
        scratch_shapes=[pltpu.VMEM((SG_T, 512), F32), pltpu.VMEM((4, SG_T, SG_T), F32)],
        compiler_params=_params(("arbitrary",)))(h, h, lng.reshape(1, 512), lnb.reshape(1, 512), w, bias_t,
                                                 do_cat, do_cat, do_cat, do_cat)


def _gate_fwd(o_cat, h):
    S = h.shape[0]
    tm = _tile(S, 512)

    def body(o_ref, g_ref, y_ref):
        g = g_ref[...]
        y_ref[...] = (o_ref[...] * (g * jax.nn.sigmoid(g))).astype(BF16)

    return pl.pallas_call(
        body, name="gate_fwd", grid=(S // tm, 4),
        in_specs=[pl.BlockSpec((tm, 512), lambda i, j: (i, j)), pl.BlockSpec((tm, 512), lambda i, j: (i, GATE // 512 + j))],
        out_specs=pl.BlockSpec((tm, 512), lambda i, j: (i, j)), out_shape=jax.ShapeDtypeStruct((S, D_MODEL), BF16),
        compiler_params=_params(("parallel", "parallel")))(o_cat, h)


def _gate_bwd(dyg, o_cat, h):
    S = h.shape[0]
    tm = _tile(S, 512)

    def body(d_ref, o_ref, g_ref, do_ref, dg_ref):
        d, g = d_ref[...], g_ref[...]
        sig = jax.nn.sigmoid(g)
        do_ref[...] = d * (g * sig)
        dg_ref[...] = d * o_ref[...] * (sig * (1.0 + g * (1.0 - sig)))

    blk = pl.BlockSpec((tm, 512), lambda i, j: (i, j))
    return pl.pallas_call(
        body, name="gate_bwd", grid=(S // tm, 4),
        in_specs=[blk, blk, pl.BlockSpec((tm, 512), lambda i, j: (i, GATE // 512 + j))],
        out_specs=[blk, blk],
        out_shape=[jax.ShapeDtypeStruct((S, D_MODEL), F32), jax.ShapeDtypeStruct((S, D_MODEL), F32)],
        compiler_params=_params(("parallel", "parallel")))(dyg, o_cat, h)


def _ln_res_fwd(x, y, g, b):
    S = x.shape[0]
    tm = _tile(S, 256)

    def body(x_ref, y_ref, g_ref, b_ref, o_ref, r_ref):
        r = ALPHA * x_ref[...] + y_ref[...]
        r_ref[...] = r
        xc = r - jnp.mean(r, axis=1, keepdims=True)
        o_ref[...] = xc * lax.rsqrt(jnp.mean(xc * xc, axis=1, keepdims=True) + LN_EPS) * g_ref[...] + b_ref[...]

    return pl.pallas_call(
        body, name="ln_res_fwd", grid=(S // tm,),
        in_specs=[_row_spec(tm, D_MODEL), _row_spec(tm, D_MODEL), _fix_spec((1, D_MODEL)), _fix_spec((1, D_MODEL))],
        out_specs=[_row_spec(tm, D_MODEL), _row_spec(tm, D_MODEL)],
        out_shape=[jax.ShapeDtypeStruct((S, D_MODEL), F32), jax.ShapeDtypeStruct((S, D_MODEL), F32)],
        compiler_params=_params(("parallel",)))(x, y, g.reshape(1, D_MODEL), b.reshape(1, D_MODEL))


def _ln_res_bwd(dout, r, g):
    S = r.shape[0]
    tm = _tile(S, 256)

    def body(d_ref, r_ref, g_ref, dr_ref, dg_ref, db_ref):
        @pl.when(pl.program_id(0) == 0)
        def _():
            dg_ref[...] = jnp.zeros_like(dg_ref)
            db_ref[...] = jnp.zeros_like(db_ref)

        d, r = d_ref[...], r_ref[...]
        xc = r - jnp.mean(r, axis=1, keepdims=True)
        rstd = lax.rsqrt(jnp.mean(xc * xc, axis=1, keepdims=True) + LN_EPS)
        xhat = xc * rstd
        dxh = d * g_ref[...]
        dr_ref[...] = rstd * (dxh - jnp.mean(dxh, axis=1, keepdims=True)
                              - xhat * jnp.mean(dxh * xhat, axis=1, keepdims=True))
        dg_ref[...] += jnp.sum(d * xhat, axis=0, keepdims=True)
        db_ref[...] += jnp.sum(d, axis=0, keepdims=True)

    return pl.pallas_call(
        body, name="ln_res_bwd", grid=(S // tm,),
        in_specs=[_row_spec(tm, D_MODEL), _row_spec(tm, D_MODEL), _fix_spec((1, D_MODEL))],
        out_specs=[_row_spec(tm, D_MODEL), _fix_spec((1, D_MODEL)), _fix_spec((1, D_MODEL))],
        out_shape=[jax.ShapeDtypeStruct((S, D_MODEL), F32), jax.ShapeDtypeStruct((1, D_MODEL), F32),
                   jax.ShapeDtypeStruct((1, D_MODEL), F32)],
        compiler_params=_params(("arbitrary",)))(dout, r, g.reshape(1, D_MODEL))


def _loss_head(y, target):
    S = y.shape[0]
    tm = _tile(S, 256)

    def body(y_ref, t_ref, l_ref, d_ref):
        @pl.when(pl.program_id(0) == 0)
        def _():
            l_ref[...] = jnp.zeros_like(l_ref)

        diff = y_ref[...] - t_ref[...]
        d_ref[...] = diff * (1.0 / D_MODEL)
        per_row = jnp.mean(diff * diff, axis=1, keepdims=True)
        l_ref[...] += 0.5 * jnp.sum(per_row, axis=0, keepdims=True)

    return pl.pallas_call(
        body, name="loss_head", grid=(S // tm,), in_specs=[_row_spec(tm, D_MODEL), _row_spec(tm, D_MODEL)],
        out_specs=[_fix_spec((8, LANE)), _row_spec(tm, D_MODEL)],
        out_shape=[jax.ShapeDtypeStruct((8, LANE), F32), jax.ShapeDtypeStruct((S, D_MODEL), F32)],
        compiler_params=_params(("arbitrary",)))(y, target)


def _perm_table():
    table, at = [], 0
    for name in PERM_ORDER:
        start, width = ORIG[name]
        table.append((name, start, width, at))
        at += width
    return table


def _permute_w_in(w):
    parts = [w[:, start:start + width] for _, start, width, _ in _perm_table()]
    return jnp.concatenate(parts + [jnp.zeros((w.shape[0], HP - D_IN), w.dtype)], axis=1)


def _unpermute_w_in(wp):
    parts = sorted(_perm_table(), key=lambda t: t[1])
    return jnp.concatenate([wp[:, at:at + width] for _, _, width, at in parts], axis=1)


def _rope_tables(positions):
    inv_freq = ROPE_THETA ** (-jnp.arange(0, 64, 2, dtype=F32) / 64)
    ang = positions.astype(F32)[:, None] * inv_freq[None, :]
    cos, sin, zero = jnp.cos(ang), jnp.sin(ang), jnp.zeros((positions.shape[0], 64), F32)
    return jnp.concatenate([cos, cos, zero], axis=1), jnp.concatenate([-sin, sin, zero], axis=1)


def _local_step(x, mem, positions, target, w):
    rc, rs = _rope_tables(positions)
    mem_b = mem.astype(BF16)
    saved = []
    for l in range(DEPTH):
        w_in = _permute_w_in(w["w_in"][l])
        w_uq = jnp.pad(w["w_uq"][l].reshape(512, MLA_HEADS, 192), ((0, 0), (0, 0), (0, 64))).reshape(512, MLA_HEADS * 256)
        w_ukv = w["w_ukv"][l]
        h = _mm(x, w_in, tm=1024, tn=1152, name="in_proj")
        cq_n = _rms_fwd(h, CQ, 512, w["q_norm_g"][l], "rms_q")
        ckv_n = _rms_fwd(h, CKV, 256, w["kv_norm_g"][l], "rms_kv")
        q = _q_proj(cq_n, w_uq, rc, rs)
        kp, v = _kv_proj(ckv_n, w_ukv, h, rc, rs)
        o_a, lse = _mla_fwd(q, kp, v)
        bias_t = w["sg_b"][l].T
        o_b = _sg_fwd(h, w["sg_ln_g"][l], w["sg_ln_b"][l], w["sg_w"][l], bias_t)
        qkv = h[:, SBQ:SBQ + 1536].astype(BF16)
        o_c = _sb_fwd(qkv)
        mk = _mm(mem_b, w["w_mem_k"][l], out_dtype=BF16, name="mem_kv")
        mv = _mm(mem_b, w["w_mem_v"][l], out_dtype=BF16, name="mem_kv")
        o_m = _mem_fwd(h, mk, mv)
        o_cat = jnp.concatenate([o_a, o_b, o_c, o_m], axis=1)
        yg = _gate_fwd(o_cat, h)
        y = _mm(yg, w["w_out"][l], tm=1024, tn=1024, name="out_proj")
        x_new, r = _ln_res_fwd(x, y, w["ln_g"][l], w["ln_b"][l])
        saved.append(dict(x=x, h=h, cq_n=cq_n, ckv_n=ckv_n, q=q, kp=kp, v=v, lse=lse, qkv=qkv, mk=mk, mv=mv,
                          o_cat=o_cat, yg=yg, r=r, w_in=w_in, w_uq=w_uq, w_ukv=w_ukv, bias_t=bias_t))
        x = x_new

    loss, dx = _loss_head(x, target)

    grads = {n: [None] * DEPTH for n in SHARDED + SMALL}
    for l in reversed(range(DEPTH)):
        s = saved[l]
        h = s["h"]
        dr, dlg, dlb = _ln_res_bwd(dx, s["r"], w["ln_g"][l])
        grads["ln_g"][l], grads["ln_b"][l] = dlg[0], dlb[0]
        grads["w_out"][l] = _mm(s["yg"], dr, ta=True, tm=1024, tn=1024, tk=1024, name="dw_out")
        dyg = _mm(dr, w["w_out"][l], tb=True, tm=1024, tn=1024, name="d_out_proj")
        do_cat, dgates = _gate_bwd(dyg, s["o_cat"], h)
        dmq, dmk, dmv = _mem_bwd(h, s["mk"], s["mv"], do_cat, 1792)
        grads["w_mem_k"][l] = _mm(mem_b, dmk, ta=True, name="dw_mem")
        grads["w_mem_v"][l] = _mm(mem_b, dmv, ta=True, name="dw_mem")
        dsq, dsk, dsv = _sb_bwd(s["qkv"], do_cat, s["o_cat"], 1280 // LANE)
        dsk = dsk.transpose(1, 0, 2).reshape(-1, 512)
        dsv = dsv.transpose(1, 0, 2).reshape(-1, 512)
        du, dv, dsgw, dsgb, dsg_g, dsg_b = _sg_bwd(h, w["sg_ln_g"][l], w["sg_ln_b"][l], w["sg_w"][l], s["bias_t"],
                                                   do_cat, 768)
        grads["sg_w"][l], grads["sg_b"][l] = dsgw, dsgb.T
        grads["sg_ln_g"][l], grads["sg_ln_b"][l] = dsg_g[0], dsg_b[0]
        dq, dk, dvv = _mla_bwd(s["q"], s["kp"], s["v"], do_cat, s["o_cat"], s["lse"])
        dq_raw = _q_rope_bwd(dq, rc, rs)
        dkv, dkpe = _kv_bwd_prep(dk, dvv, rc, rs)
        dw_uq = _mm(s["cq_n"], dq_raw, ta=True, tk=1024, name="dw_uq")
        grads["w_uq"][l] = dw_uq.reshape(512, MLA_HEADS, 256)[:, :, :192].reshape(512, MLA_HEADS * 192)
        grads["w_ukv"][l] = _mm(s["ckv_n"], dkv, ta=True, tk=1024, name="dw_ukv")
        dcq_n = _mm(dq_raw, s["w_uq"], tb=True, name="d_cq")
        dckv_n = _mm(dkv, s["w_ukv"], tb=True, name="d_ckv")
        dcq, dqg = _rms_bwd(h, CQ, 512, w["q_norm_g"][l], dcq_n, "rms_q_bwd")
        dckv, dkvg = _rms_bwd(h, CKV, 256, w["kv_norm_g"][l], dckv_n, "rms_kv_bwd")
        grads["q_norm_g"][l], grads["kv_norm_g"][l] = dqg[0], dkvg[0]
        dh = jnp.concatenate([dcq, dckv, dmq, du, dv, dsq, dsk, dsv, dgates, dkpe], axis=1).astype(BF16)
        dw_in = _mm(s["x"], dh, ta=True, tm=1024, tn=1152, tk=1024, name="dw_in")
        grads["w_in"][l] = _unpermute_w_in(dw_in)
        dx = _mm(dh, s["w_in"], tb=True, add=dr, add_scale=ALPHA, tm=1024, tn=1024, tk=1152, name="d_in_proj")

    return loss, dx, {n: jnp.stack(g) for n, g in grads.items()}


MESH = pl.DeviceIdType.MESH
HBM_SPEC = pl.BlockSpec(memory_space=pltpu.HBM)


def _place():
    x, y, c = lax.axis_index("x"), lax.axis_index("y"), lax.axis_index("c")
    return x, y, c, [(1 - x, y), (x, 1 - y), (1 - x, 1 - y)]


def _gather_weights(flat):
    R = flat.shape[0]
    H = R // 2

    def body(src, out, send_sems, recv_sems, own_sem):
        x, y, c, chips = _place()
        mine = pl.ds(c * H, H)
        theirs = pl.ds((1 - c) * H, H)

        def copy(k, src_ref, chip, rows, to):
            return pltpu.make_async_remote_copy(src_ref=src_ref, dst_ref=out.at[chip, rows, :], send_sem=send_sems.at[k],
                                                recv_sem=recv_sems.at[k], device_id=to, device_id_type=MESH)

        own = pltpu.make_async_copy(src, out.at[2 * x + y], own_sem)
        own.start()
        sent = [copy(j, src.at[mine, :], 2 * x + y, mine, (px, py, c)) for j, (px, py) in enumerate(chips)]
        for cp in sent:
            cp.start()
        passed = []
        for j, (px, py) in enumerate(chips):
            copy(j, src.at[mine, :], 2 * px + py, mine, (px, py, c)).wait_recv()
            cp = copy(3 + j, out.at[2 * px + py, mine, :], 2 * px + py, mine, (x, y, 1 - c))
            cp.start()
            passed.append(cp)
        for j, (px, py) in enumerate(chips):
            copy(3 + j, src.at[theirs, :], 2 * px + py, theirs, (x, y, 1 - c)).wait_recv()
        for cp in sent + passed:
            cp.wait_send()
        own.wait()

    return pl.pallas_call(
        body, name="gather_weights", in_specs=[HBM_SPEC], out_specs=HBM_SPEC,
        out_shape=jax.ShapeDtypeStruct((4, R, FLAT_W), flat.dtype),
        scratch_shapes=[pltpu.SemaphoreType.DMA((6,)), pltpu.SemaphoreType.DMA((6,)), pltpu.SemaphoreType.DMA(())],
        compiler_params=pltpu.CompilerParams(has_side_effects=True))(flat)


def _swap_halves(g):
    _, _, H, W = g.shape

    def body(src, out, send_sem, recv_sem):
        x, y, c, _ = _place()
        cp = pltpu.make_async_remote_copy(src_ref=src.at[:, 1 - c], dst_ref=out, send_sem=send_sem, recv_sem=recv_sem,
                                          device_id=(x, y, 1 - c), device_id_type=MESH)
        cp.start()
        cp.wait()

    return pl.pallas_call(
        body, name="swap_halves", in_specs=[HBM_SPEC], out_specs=HBM_SPEC,
        out_shape=jax.ShapeDtypeStruct((4, H, W), g.dtype),
        scratch_shapes=[pltpu.SemaphoreType.DMA(()), pltpu.SemaphoreType.DMA(())],
        compiler_params=pltpu.CompilerParams(has_side_effects=True))(g)


def _pair_sum(g, other, c):
    _, _, H, W = g.shape
    th = _row_tile(H, 3 * W * 4)

    def body(c_ref, a_ref, b_ref, o_ref):
        o_ref[...] = a_ref[...] + b_ref[...]

    return pl.pallas_call(
        body, name="pair_sum",
        grid_spec=pltpu.PrefetchScalarGridSpec(
            num_scalar_prefetch=1, grid=(4, H // th),
            in_specs=[pl.BlockSpec((None, None, th, W), lambda d, i, c_ref: (d, c_ref[0], i, 0)),
                      pl.BlockSpec((None, th, W), lambda d, i, c_ref: (d, i, 0))],
            out_specs=pl.BlockSpec((None, th, W), lambda d, i, c_ref: (d, i, 0))),
        out_shape=jax.ShapeDtypeStruct((4, H, W), F32),
        compiler_params=_params(("parallel", "parallel")))(c, g, other)


def _exchange_chips(p):
    _, H, W = p.shape

    def body(src, out, send_sems, recv_sems, own_sem):
        x, y, c, chips = _place()
        me = 2 * x + y
        own = pltpu.make_async_copy(src.at[me], out.at[me], own_sem)
        own.start()
        sent = []
        for j, (px, py) in enumerate(chips):
            cp = pltpu.make_async_remote_copy(src_ref=src.at[2 * px + py], dst_ref=out.at[me], send_sem=send_sems.at[j],
                                              recv_sem=recv_sems.at[j], device_id=(px, py, c), device_id_type=MESH)
            cp.start()
            sent.append(cp)
        for j, (px, py) in enumerate(chips):
            pltpu.make_async_remote_copy(src_ref=src.at[me], dst_ref=out.at[2 * px + py], send_sem=send_sems.at[j],
                                         recv_sem=recv_sems.at[j], device_id=(px, py, c), device_id_type=MESH).wait_recv()
        for cp in sent:
            cp.wait_send()
        own.wait()

    return pl.pallas_call(
        body, name="exchange_chips", in_specs=[HBM_SPEC], out_specs=HBM_SPEC,
        out_shape=jax.ShapeDtypeStruct((4, H, W), p.dtype),
        scratch_shapes=[pltpu.SemaphoreType.DMA((3,)), pltpu.SemaphoreType.DMA((3,)), pltpu.SemaphoreType.DMA(())],
        compiler_params=pltpu.CompilerParams(has_side_effects=True))(p)


def _sum_parts(t, name):
    n, H, W = t.shape
    th = _row_tile(H, (n + 1) * W * 4)

    def body(t_ref, o_ref):
        acc = t_ref[0]
        for k in range(1, n):
            acc = acc + t_ref[k]
        o_ref[...] = acc

    return pl.pallas_call(
        body, name=name, grid=(H // th,), in_specs=[pl.BlockSpec((n, th, W), lambda i: (0, i, 0))],
        out_specs=pl.BlockSpec((th, W), lambda i: (i, 0)), out_shape=jax.ShapeDtypeStruct((H, W), F32),
        compiler_params=_params(("parallel",)))(t)


def _share_with_sibling(half):
    H, W = half.shape

    def body(src, out, send_sem, recv_sem, own_sem):
        x, y, c, _ = _place()
        own = pltpu.make_async_copy(src, out.at[c], own_sem)
        own.start()
        cp = pltpu.make_async_remote_copy(src_ref=src, dst_ref=out.at[c], send_sem=send_sem, recv_sem=recv_sem,
                                          device_id=(x, y, 1 - c), device_id_type=MESH)
        cp.start()
        pltpu.make_async_remote_copy(src_ref=src, dst_ref=out.at[1 - c], send_sem=send_sem, recv_sem=recv_sem,
                                     device_id=(x, y, 1 - c), device_id_type=MESH).wait_recv()
        cp.wait_send()
        own.wait()

    return pl.pallas_call(
        body, name="share_with_sibling", in_specs=[HBM_SPEC], out_specs=HBM_SPEC,
        out_shape=jax.ShapeDtypeStruct((2, H, W), half.dtype),
        scratch_shapes=[pltpu.SemaphoreType.DMA(()), pltpu.SemaphoreType.DMA(()), pltpu.SemaphoreType.DMA(())],
        compiler_params=pltpu.CompilerParams(has_side_effects=True))(half)


def _gather_all(v):
    n, W = v.shape

    def body(src, out, send_sems, recv_sems, own_sem):
        x, y, c, _ = _place()
        own = pltpu.make_async_copy(src, out.at[4 * x + 2 * y + c], own_sem)
        own.start()
        flips = [(fx, fy, fc) for fx in (0, 1) for fy in (0, 1) for fc in (0, 1)][1:]
        sent = []
        for k, (fx, fy, fc) in enumerate(flips):
            cp = pltpu.make_async_remote_copy(
                src_ref=src, dst_ref=out.at[4 * x + 2 * y + c], send_sem=send_sems.at[k], recv_sem=recv_sems.at[k],
                device_id=(x ^ fx, y ^ fy, c ^ fc), device_id_type=MESH)
            cp.start()
            sent.append(cp)
        for k, (fx, fy, fc) in enumerate(flips):
            pltpu.make_async_remote_copy(
                src_ref=src, dst_ref=out.at[4 * (x ^ fx) + 2 * (y ^ fy) + (c ^ fc)], send_sem=send_sems.at[k],
                recv_sem=recv_sems.at[k], device_id=(x ^ fx, y ^ fy, c ^ fc), device_id_type=MESH).wait_recv()
        for cp in sent:
            cp.wait_send()
        own.wait()

    return pl.pallas_call(
        body, name="gather_all", in_specs=[HBM_SPEC], out_specs=HBM_SPEC,
        out_shape=jax.ShapeDtypeStruct((8, n, W), v.dtype),
        scratch_shapes=[pltpu.SemaphoreType.DMA((7,)), pltpu.SemaphoreType.DMA((7,)), pltpu.SemaphoreType.DMA(())],
        compiler_params=pltpu.CompilerParams(has_side_effects=True))(v)


def _adamw(w, g, m, v):
    shape = w.shape
    cols = shape[-1]
    w2, g2, m2, v2 = (a.reshape(-1, cols) for a in (w, g, m, v))
    rows = w2.shape[0]
    tr = next((t for t in (1024, 512, 256, 128, 64, 32, 16, 8) if rows % t == 0 and t * cols * 4 <= (2 << 20)), rows)

    def body(w_ref, g_ref, m_ref, v_ref, d_ref, nm_ref, nv_ref):
        g_ = g_ref[...]
        nm = ADAM_B1 * m_ref[...] + (1.0 - ADAM_B1) * g_
        nv = ADAM_B2 * v_ref[...] + (1.0 - ADAM_B2) * (g_ * g_)
        m_hat = nm / (1.0 - ADAM_B1 ** ADAM_STEP)
        v_hat = nv / (1.0 - ADAM_B2 ** ADAM_STEP)
        d_ref[...] = -ADAM_LR * (m_hat / (jnp.sqrt(v_hat) + ADAM_EPS) + ADAM_WD * w_ref[...])
        nm_ref[...] = nm
        nv_ref[...] = nv

    blk = pl.BlockSpec((tr, cols), lambda i: (i, 0))
    outs = pl.pallas_call(
        body, name="adamw", grid=(rows // tr,), in_specs=[blk] * 4, out_specs=[blk] * 3,
        out_shape=[jax.ShapeDtypeStruct((rows, cols), F32)] * 3, compiler_params=_params(("parallel",)))(w2, g2, m2, v2)
    return tuple(o.reshape(shape) for o in outs)


def _chip_part(name, a, k):
    n = a.shape[2 if name in ("w_in", "w_uq", "w_ukv") else 1] // 4
    return a[:, :, k * n:(k + 1) * n] if name in ("w_in", "w_uq", "w_ukv") else a[:, k * n:(k + 1) * n]


def _join_chips(name, parts):
    return jnp.concatenate(parts, axis=2 if name in ("w_in", "w_uq", "w_ukv") else 1)


def kernel(x, mem, positions, w_in, q_norm_g, w_uq, kv_norm_g, w_ukv, sg_ln_g, sg_ln_b, sg_w, sg_b, w_mem_k, w_mem_v, w_out, ln_g, ln_b, loss_target, m_w_in, m_q_norm_g, m_w_uq, m_kv_norm_g, m_w_ukv, m_sg_ln_g, m_sg_ln_b, m_sg_w, m_sg_b, m_w_mem_k, m_w_mem_v, m_w_out, m_ln_g, m_ln_b, v_w_in, v_q_norm_g, v_w_uq, v_kv_norm_g, v_w_ukv, v_sg_ln_g, v_sg_ln_b, v_sg_w, v_sg_b, v_w_mem_k, v_w_mem_v, v_w_out, v_ln_g, v_ln_b):
    weights = dict(w_in=w_in, q_norm_g=q_norm_g, w_uq=w_uq, kv_norm_g=kv_norm_g, w_ukv=w_ukv, sg_ln_g=sg_ln_g,
                   sg_ln_b=sg_ln_b, sg_w=sg_w, sg_b=sg_b, w_mem_k=w_mem_k, w_mem_v=w_mem_v, w_out=w_out, ln_g=ln_g, ln_b=ln_b)
    mom_m = dict(w_in=m_w_in, q_norm_g=m_q_norm_g, w_uq=m_w_uq, kv_norm_g=m_kv_norm_g, w_ukv=m_w_ukv, sg_ln_g=m_sg_ln_g,
                 sg_ln_b=m_sg_ln_b, sg_w=m_sg_w, sg_b=m_sg_b, w_mem_k=m_w_mem_k, w_mem_v=m_w_mem_v, w_out=m_w_out,
                 ln_g=m_ln_g, ln_b=m_ln_b)
    mom_v = dict(w_in=v_w_in, q_norm_g=v_q_norm_g, w_uq=v_w_uq, kv_norm_g=v_kv_norm_g, w_ukv=v_w_ukv, sg_ln_g=v_sg_ln_g,
                 sg_ln_b=v_sg_ln_b, sg_w=v_sg_w, sg_b=v_sg_b, w_mem_k=v_w_mem_k, w_mem_v=v_w_mem_v, w_out=v_w_out,
                 ln_g=v_ln_g, ln_b=v_ln_b)
    c_idx = lax.axis_index("c").astype(jnp.int32).reshape(1)

    sizes = [weights[n].size for n in SHARDED]
    flat = jnp.concatenate([weights[n].astype(BF16).reshape(-1) for n in SHARDED]).reshape(-1, FLAT_W)
    gathered = _gather_weights(flat)
    full = dict((n, weights[n]) for n in SMALL)
    at = 0
    for n, size in zip(SHARDED, sizes):
        rows = size // FLAT_W
        full[n] = _join_chips(n, [gathered[k, at:at + rows].reshape(weights[n].shape) for k in range(4)])
        at += rows
    R = at

    loss_dev, grad_x, grads = _local_step(x[0], mem[0], positions[0], loss_target[0], full)

    per_chip = [jnp.concatenate([_chip_part(n, grads[n], k).reshape(-1) for n in SHARDED]) for k in range(4)]
    g4 = jnp.stack(per_chip).reshape(4, 2, R // 2, FLAT_W)
    pair = _pair_sum(g4, _swap_halves(g4), c_idx)
    half = _sum_parts(_exchange_chips(pair), "chip_sum")
    reduced = _share_with_sibling(half).reshape(R, FLAT_W)
    grad_out = {}
    at = 0
    for n, size in zip(SHARDED, sizes):
        rows = size // FLAT_W
        grad_out[n] = reduced[at:at + rows].reshape(weights[n].shape)
        at += rows

    small_sizes = [weights[n].size for n in SMALL]
    vec = jnp.concatenate([grads[n].reshape(-1) for n in SMALL] + [loss_dev[0]])
    n_small = vec.shape[0]
    rows_small = -(-n_small // (8 * FLAT_W)) * 8
    vec = jnp.pad(vec, (0, rows_small * FLAT_W - n_small)).reshape(rows_small, FLAT_W)
    total = _sum_parts(_gather_all(vec), "device_sum").reshape(-1)
    at = 0
    for n, size in zip(SMALL, small_sizes):
        grad_out[n] = total[at:at + size].reshape(weights[n].shape)
        at += size
    loss = total[at]

    names = list(weights)
    upd = {n: _adamw(weights[n], grad_out[n], mom_m[n], mom_v[n]) for n in names}
    return (loss, grad_x[None], *[grad_out[n] for n in names], *[upd[n][0] for n in names],
            *[upd[n][1] for n in names], *[upd[n][2] for n in names])
```

```python
import math

import jax
import jax.numpy as jnp
from jax import lax
from jax.experimental import pallas as pl
from jax.experimental.pallas import tpu as pltpu

F32, BF16 = jnp.float32, jnp.bfloat16

D_MODEL = 2048
DEPTH = 4
CHUNK = 64
MLA_HEADS = 6
MLA_SCALE = 1.0 / math.sqrt(192.0)
SB_HEADS = 4
SB_SCALE = 1.0 / math.sqrt(128.0)
MEM_HEADS = 4
MEM_SCALE = 1.0 / math.sqrt(64.0)
ROPE_THETA = 10000.0
ALPHA = (2.0 * DEPTH) ** 0.25
LN_EPS = 1e-5
RMS_EPS = 1e-6
ADAM_LR, ADAM_B1, ADAM_B2, ADAM_EPS, ADAM_WD, ADAM_STEP = 0.001, 0.9, 0.999, 1e-08, 0.01, 10

ORIG = dict(c_q=(0, 512), c_kv=(512, 256), k_pe=(768, 64), g_a=(832, 768), sg_u=(1600, 512), sg_v=(2112, 512),
            g_b=(2624, 512), sb_q=(3136, 512), sb_k=(3648, 512), sb_v=(4160, 512), g_c=(4672, 512),
            m_q=(5184, 256), g_m=(5440, 256))
D_IN = 5696
PERM_ORDER = ("c_q", "c_kv", "m_q", "sg_u", "sg_v", "sb_q", "sb_k", "sb_v", "g_a", "g_b", "g_c", "g_m", "k_pe")
HP = 5760
CQ, CKV, MQ, SGU, SGV, SBQ, GATE, KPE = 0, 512, 768, 1024, 1536, 2048, 3584, 5632

Q_BLK = 1024
K_BLK = 256
LANE = 128
VMEM_LIMIT = 56 * 1024 * 1024

FLAT_W = 1024
SHARDED = ("w_in", "w_uq", "w_ukv", "w_mem_k", "w_mem_v", "w_out")
SMALL = ("q_norm_g", "kv_norm_g", "sg_ln_g", "sg_ln_b", "sg_w", "sg_b", "ln_g", "ln_b")


def _params(sem=None):
    return pltpu.CompilerParams(dimension_semantics=sem, vmem_limit_bytes=VMEM_LIMIT)


def _tile(dim, pref):
    if dim <= pref:
        return dim
    t = (pref // LANE) * LANE
    while t >= LANE:
        if dim % t == 0:
            return t
        t -= LANE
    return dim


def _row_tile(rows, bytes_per_row, budget=8 << 20):
    best = None
    for t in range(8, rows + 1, 8):
        if rows % t == 0 and t * bytes_per_row <= budget:
            best = t
    return best if best else rows


def _dot_nt(a, b):
    return lax.dot_general(a, b, (((1,), (1,)), ((), ())), preferred_element_type=F32)


def _dot_tn(a, b):
    return lax.dot_general(a, b, (((0,), (0,)), ((), ())), preferred_element_type=F32)


def _dot(a, b):
    return jnp.dot(a, b, preferred_element_type=F32)


def _mm(a, b, *, ta=False, tb=False, a_win=None, b_win=None, add=None, add_scale=1.0, out_dtype=F32,
        tm=512, tn=512, tk=512, name="mm"):
    a_off, a_w = a_win if a_win else (0, a.shape[1])
    b_off, b_w = b_win if b_win else (0, b.shape[1])
    (K, M) = (a.shape[0], a_w) if ta else (a_w, a.shape[0])
    (N, Kb) = (b.shape[0], b_w) if tb else (b_w, b.shape[0])
    assert K == Kb, (a.shape, b.shape, ta, tb)
    tm, tn, tk = _tile(M, tm), _tile(N, tn), _tile(K, tk)
    nk = K // tk
    if ta:
        assert a_off % tm == 0
        a_spec = pl.BlockSpec((tk, tm), lambda i, j, k: (k, i + a_off // tm))
    else:
        assert a_off % tk == 0
        a_spec = pl.BlockSpec((tm, tk), lambda i, j, k: (i, k + a_off // tk))
    if tb:
        assert b_off % tk == 0
        b_spec = pl.BlockSpec((tn, tk), lambda i, j, k: (j, k + b_off // tk))
    else:
        assert b_off % tn == 0
        b_spec = pl.BlockSpec((tk, tn), lambda i, j, k: (k, j + b_off // tn))
    o_spec = pl.BlockSpec((tm, tn), lambda i, j, k: (i, j))
    dn = (((0 if ta else 1,), (1 if tb else 0,)), ((), ()))
    has_add = add is not None

    def body(*refs):
        if has_add:
            a_ref, b_ref, add_ref, o_ref, acc_ref = refs
        else:
            a_ref, b_ref, o_ref, acc_ref = refs
        k = pl.program_id(2)

        @pl.when(k == 0)
        def _():
            acc_ref[...] = jnp.zeros_like(acc_ref)

        acc_ref[...] += lax.dot_general(a_ref[...].astype(BF16), b_ref[...].astype(BF16), dn,
                                        preferred_element_type=F32)

        @pl.when(k == nk - 1)
        def _():
            r = acc_ref[...]
            if has_add:
                r = r + add_scale * add_ref[...]
            o_ref[...] = r.astype(o_ref.dtype)

    ins = [a, b] + ([add] if has_add else [])
    specs = [a_spec, b_spec] + ([o_spec] if has_add else [])
    return pl.pallas_call(
        body, name=name, grid=(M // tm, N // tn, nk), in_specs=specs, out_specs=o_spec,
        out_shape=jax.ShapeDtypeStruct((M, N), out_dtype), scratch_shapes=[pltpu.VMEM((tm, tn), F32)],
        compiler_params=_params(("parallel", "parallel", "arbitrary")))(*ins)


GELU_K = math.sqrt(2.0 / math.pi)


def _gelu(x):
    t = jnp.tanh(GELU_K * (x + 0.044715 * (x * x * x)))
    return 0.5 * x * (1.0 + t)


def _gelu_grad(x):
    t = jnp.tanh(GELU_K * (x + 0.044715 * (x * x * x)))
    return 0.5 * (1.0 + t) + 0.5 * x * (1.0 - t * t) * GELU_K * (1.0 + 3.0 * 0.044715 * x * x)


def _rope_swap(t):
    lane = lax.broadcasted_iota(jnp.int32, t.shape, 1)
    return jnp.where(lane < 32, pltpu.roll(t, 96, axis=1), pltpu.roll(t, 32, axis=1))


def _rope(t, c, s):
    return t * c + _rope_swap(t) * s


def _rope_bwd(dt, c, s):
    return dt * c - _rope_swap(dt) * s


def _row_spec(tm, w, cb=0):
    return pl.BlockSpec((tm, w), lambda i: (i, cb))


def _fix_spec(shape):
    return pl.BlockSpec(shape, lambda *_: (0,) * len(shape))


def _rms_fwd(h, off, width, g, name):
    S = h.shape[0]
    tm = _tile(S, 512)

    def body(x_ref, g_ref, o_ref):
        x = x_ref[...]
        r = lax.rsqrt(jnp.mean(x * x, axis=1, keepdims=True) + RMS_EPS)
        o_ref[...] = (x * r * g_ref[...]).astype(BF16)

    return pl.pallas_call(
        body, name=name, grid=(S // tm,), in_specs=[_row_spec(tm, width, off // width), _fix_spec((1, width))],
        out_specs=_row_spec(tm, width), out_shape=jax.ShapeDtypeStruct((S, width), BF16),
        compiler_params=_params(("parallel",)))(h, g.reshape(1, width))


def _rms_bwd(h, off, width, g, dxn, name):
    S = h.shape[0]
    tm = _tile(S, 512)

    def body(x_ref, g_ref, d_ref, dx_ref, dg_ref):
        @pl.when(pl.program_id(0) == 0)
        def _():
            dg_ref[...] = jnp.zeros_like(dg_ref)

        x, d = x_ref[...], d_ref[...]
        r = lax.rsqrt(jnp.mean(x * x, axis=1, keepdims=True) + RMS_EPS)
        gd = d * g_ref[...]
        dx_ref[...] = gd * r - x * (r * r * r) * jnp.mean(gd * x, axis=1, keepdims=True)
        dg_ref[...] += jnp.sum(d * x * r, axis=0, keepdims=True)

    return pl.pallas_call(
        body, name=name, grid=(S // tm,),
        in_specs=[_row_spec(tm, width, off // width), _fix_spec((1, width)), _row_spec(tm, width)],
        out_specs=[_row_spec(tm, width), _fix_spec((1, width))],
        out_shape=[jax.ShapeDtypeStruct((S, width), F32), jax.ShapeDtypeStruct((1, width), F32)],
        compiler_params=_params(("arbitrary",)))(h, g.reshape(1, width), dxn)


def _q_proj(xn, w, rc, rs):
    S = xn.shape[0]
    tm = _tile(S, 512)

    def body(x_ref, w_ref, c_ref, s_ref, q_ref):
        q = _dot(x_ref[...], w_ref[...])
        q_ref[:, :LANE] = q[:, :LANE].astype(BF16)
        q_ref[:, LANE:] = _rope(q[:, LANE:], c_ref[...], s_ref[...]).astype(BF16)

    return pl.pallas_call(
        body, name="q_proj", grid=(S // tm, MLA_HEADS),
        in_specs=[pl.BlockSpec((tm, 512), lambda i, j: (i, 0)), pl.BlockSpec((512, 256), lambda i, j: (0, j)),
                  pl.BlockSpec((tm, LANE), lambda i, j: (i, 0)), pl.BlockSpec((tm, LANE), lambda i, j: (i, 0))],
        out_specs=pl.BlockSpec((tm, 256), lambda i, j: (i, j)),
        out_shape=jax.ShapeDtypeStruct((S, MLA_HEADS * 256), BF16),
        compiler_params=_params(("parallel", "parallel")))(xn, w, rc, rs)


def _kv_proj(xn, w, h, rc, rs):
    S = xn.shape[0]
    tm = _tile(S, 512)

    def body(x_ref, w_ref, pe_ref, c_ref, s_ref, k_ref, v_ref):
        kv = _dot(x_ref[...], w_ref[...])
        k_ref[:, :LANE] = kv[:, :LANE].astype(BF16)
        k_ref[:, LANE:] = _rope(pe_ref[...], c_ref[...], s_ref[...]).astype(BF16)
        v_ref[...] = kv[:, LANE:].astype(BF16)

    return pl.pallas_call(
        body, name="kv_proj", grid=(S // tm, MLA_HEADS),
        in_specs=[pl.BlockSpec((tm, 256), lambda i, j: (i, 0)), pl.BlockSpec((256, 256), lambda i, j: (0, j)),
                  pl.BlockSpec((tm, LANE), lambda i, j: (i, KPE // LANE)),
                  pl.BlockSpec((tm, LANE), lambda i, j: (i, 0)), pl.BlockSpec((tm, LANE), lambda i, j: (i, 0))],
        out_specs=[pl.BlockSpec((tm, 256), lambda i, j: (i, j)), pl.BlockSpec((tm, LANE), lambda i, j: (i, j))],
        out_shape=[jax.ShapeDtypeStruct((S, MLA_HEADS * 256), BF16), jax.ShapeDtypeStruct((S, MLA_HEADS * LANE), BF16)],
        compiler_params=_params(("parallel", "parallel")))(xn, w, h, rc, rs)


def _q_rope_bwd(dq, rc, rs):
    S = dq.shape[0]
    tm = _tile(S, 512)

    def body(d_ref, c_ref, s_ref, o_ref):
        o_ref[:, :LANE] = d_ref[:, :LANE].astype(BF16)
        o_ref[:, LANE:] = _rope_bwd(d_ref[:, LANE:], c_ref[...], s_ref[...]).astype(BF16)

    return pl.pallas_call(
        body, name="q_rope_bwd", grid=(S // tm, MLA_HEADS),
        in_specs=[pl.BlockSpec((tm, 256), lambda i, j: (i, j)),
                  pl.BlockSpec((tm, LANE), lambda i, j: (i, 0)), pl.BlockSpec((tm, LANE), lambda i, j: (i, 0))],
        out_specs=pl.BlockSpec((tm, 256), lambda i, j: (i, j)),
        out_shape=jax.ShapeDtypeStruct((S, MLA_HEADS * 256), BF16),
        compiler_params=_params(("parallel", "parallel")))(dq, rc, rs)


def _kv_bwd_prep(dk, dv, rc, rs):
    S = dk.shape[1]
    tm = _tile(S, 512)

    def body(dk_ref, dv_ref, c_ref, s_ref, o_ref, pe_ref):
        rot = jnp.zeros((tm, LANE), F32)
        for hh in range(MLA_HEADS):
            o_ref[:, hh * 256:hh * 256 + LANE] = dk_ref[hh, :, :LANE].astype(BF16)
            o_ref[:, hh * 256 + LANE:(hh + 1) * 256] = dv_ref[hh].astype(BF16)
            rot = rot + dk_ref[hh, :, LANE:]
        pe_ref[...] = _rope_bwd(rot, c_ref[...], s_ref[...])

    return pl.pallas_call(
        body, name="kv_bwd_prep", grid=(S // tm,),
        in_specs=[pl.BlockSpec((MLA_HEADS, tm, 256), lambda i: (0, i, 0)),
                  pl.BlockSpec((MLA_HEADS, tm, LANE), lambda i: (0, i, 0)), _row_spec(tm, LANE), _row_spec(tm, LANE)],
        out_specs=[_row_spec(tm, MLA_HEADS * 256), _row_spec(tm, LANE)],
        out_shape=[jax.ShapeDtypeStruct((S, MLA_HEADS * 256), BF16), jax.ShapeDtypeStruct((S, LANE), F32)],
        compiler_params=_params(("parallel",)))(dk, dv, rc, rs)


def _chunk_mask(T):
    row = lax.broadcasted_iota(jnp.int32, (T, T), 0)
    col = lax.broadcasted_iota(jnp.int32, (T, T), 1)
    return (col // CHUNK) <= (row // CHUNK)


def _att_blocks(S):
    tq = min(Q_BLK, S)
    tk = min(K_BLK, tq)
    return tq, tk, tq // tk


def _tail_masks(tq, tk, t):
    row = lax.broadcasted_iota(jnp.int32, (tq, tk), 0)
    col = lax.broadcasted_iota(jnp.int32, (tq, tk), 1) + t * tk
    return (col // CHUNK) <= (row // CHUNK), col < row


def _mla_fwd(q, kp, v):
    S = q.shape[0]
    TQ, TK, n = _att_blocks(S)

    def body(q_ref, k_ref, v_ref, o_ref, lse_ref):
        i = pl.program_id(1)
        qb = q_ref[...]

        def update(j, carry, mask):
            m, l, acc = carry
            sl = pl.ds(pl.multiple_of(j * TK, TK), TK)
            s = _dot_nt(qb, k_ref[sl, :]) * MLA_SCALE
            if mask is not None:
                s = jnp.where(mask, s, -1e30)
            m_new = jnp.maximum(m, jnp.max(s, axis=1, keepdims=True))
            a = jnp.exp(m - m_new)
            p = jnp.exp(s - m_new)
            return (m_new, a * l + jnp.sum(p, axis=1, keepdims=True),
                    a * acc + _dot(p.astype(BF16), v_ref[sl, :]))

        carry = (jnp.full((TQ, 1), -1e30, F32), jnp.zeros((TQ, 1), F32), jnp.zeros((TQ, LANE), F32))
        carry = lax.fori_loop(0, i * n, lambda j, c: update(j, c, None), carry)
        for t in range(n):
            carry = update(i * n + t, carry, _tail_masks(TQ, TK, t)[0])
        m, l, acc = carry
        o_ref[...] = acc / l
        lse_ref[...] = jnp.broadcast_to(m + jnp.log(l), (TQ, LANE))

    return pl.pallas_call(
        body, name="mla_fwd", grid=(MLA_HEADS, S // TQ),
        in_specs=[pl.BlockSpec((TQ, 256), lambda h, i: (i, h)), pl.BlockSpec((S, 256), lambda h, i: (0, h)),
                  pl.BlockSpec((S, LANE), lambda h, i: (0, h))],
        out_specs=[pl.BlockSpec((TQ, LANE), lambda h, i: (i, h)), pl.BlockSpec((TQ, LANE), lambda h, i: (i, h))],
        out_shape=[jax.ShapeDtypeStruct((S, MLA_HEADS * LANE), F32), jax.ShapeDtypeStruct((S, MLA_HEADS * LANE), F32)],
        compiler_params=_params(("parallel", "arbitrary")))(q, kp, v)


def _mla_bwd(q, kp, v, do_cat, o_cat, lse):
    S = q.shape[0]
    TQ, TK, n = _att_blocks(S)
    nq = S // TQ

    def body(q_ref, k_ref, v_ref, do_ref, o_ref, lse_ref, dq_ref, dk_hbm, dv_hbm, dk_acc, dv_acc):
        h, i = pl.program_id(0), pl.program_id(1)

        @pl.when(i == 0)
        def _():
            dk_acc[...] = jnp.zeros_like(dk_acc)
            dv_acc[...] = jnp.zeros_like(dv_acc)

        qb = q_ref[...]
        do32 = do_ref[...]
        dob = do32.astype(BF16)
        delta = jnp.sum(do32 * o_ref[...], axis=1, keepdims=True)
        lse_col = lse_ref[:, :1]

        def blk(j, dq, mask):
            sl = pl.ds(pl.multiple_of(j * TK, TK), TK)
            kb, vb = k_ref[sl, :], v_ref[sl, :]
            s = _dot_nt(qb, kb) * MLA_SCALE
            if mask is not None:
                s = jnp.where(mask, s, -1e30)
            p = jnp.exp(s - lse_col)
            dp = _dot_nt(dob, vb)
            ds = (p * (dp - delta) * MLA_SCALE).astype(BF16)
            dk_acc[sl, :] += _dot_tn(ds, qb)
            dv_acc[sl, :] += _dot_tn(p.astype(BF16), dob)
            return dq + _dot(ds, kb)

        dq = lax.fori_loop(0, i * n, lambda j, c: blk(j, c, None), jnp.zeros((TQ, 256), F32))
        for t in range(n):
            dq = blk(i * n + t, dq, _tail_masks(TQ, TK, t)[0])
        dq_ref[...] = dq

        @pl.when(i == nq - 1)
        def _():
            pltpu.sync_copy(dk_acc, dk_hbm.at[h])
            pltpu.sync_copy(dv_acc, dv_hbm.at[h])

    any_spec = pl.BlockSpec(memory_space=pl.ANY)
    T = TQ
    return pl.pallas_call(
        body, name="mla_bwd", grid=(MLA_HEADS, nq),
        in_specs=[pl.BlockSpec((T, 256), lambda h, i: (i, h)), pl.BlockSpec((S, 256), lambda h, i: (0, h)),
                  pl.BlockSpec((S, LANE), lambda h, i: (0, h)), pl.BlockSpec((T, LANE), lambda h, i: (i, h)),
                  pl.BlockSpec((T, LANE), lambda h, i: (i, h)), pl.BlockSpec((T, LANE), lambda h, i: (i, h))],
        out_specs=[pl.BlockSpec((T, 256), lambda h, i: (i, h)), any_spec, any_spec],
        out_shape=[jax.ShapeDtypeStruct((S, MLA_HEADS * 256), F32), jax.ShapeDtypeStruct((MLA_HEADS, S, 256), F32),
                   jax.ShapeDtypeStruct((MLA_HEADS, S, LANE), F32)],
        scratch_shapes=[pltpu.VMEM((S, 256), F32), pltpu.VMEM((S, LANE), F32)],
        compiler_params=_params(("arbitrary", "arbitrary")))(q, kp, v, do_cat, o_cat, lse)


def _split_dot(x, tri):
    hi = x.astype(BF16)
    lo = (x - hi.astype(F32)).astype(BF16)
    return _dot(hi, tri) + _dot(lo, tri)


def _sb_block(qb, kb, tri, carry, strict):
    z = _dot_nt(qb, kb) * SB_SCALE
    lb = jnp.minimum(z, 0.0) - jnp.log(1.0 + jnp.exp(-jnp.abs(z)))
    lm = lb - z
    if strict is not None:
        lm = jnp.where(strict, lm, 0.0)
    a = jnp.exp(lb + carry + _split_dot(lm, tri))
    if strict is not None:
        a = jnp.where(strict, a, 0.0)
    return a, lb, lm


def _triangles(tk):
    row = lax.broadcasted_iota(jnp.int32, (tk, tk), 0)
    col = lax.broadcasted_iota(jnp.int32, (tk, tk), 1)
    return (row > col).astype(BF16), (row >= col).astype(BF16)


def _sb_fwd(qkv):
    S = qkv.shape[0]
    TQ, TK, n = _att_blocks(S)
    T = TQ

    def body(q_ref, k_ref, v_ref, o_ref):
        i = pl.program_id(1)
        qb = q_ref[...]
        tri, _ = _triangles(TK)

        def blk(j, state, strict):
            carry, acc = state
            sl = pl.ds(pl.multiple_of(j * TK, TK), TK)
            a, _, lm = _sb_block(qb, k_ref[sl, :], tri, carry, strict)
            return carry + jnp.sum(lm, axis=1, keepdims=True), acc + _dot(a.astype(BF16), v_ref[sl, :])

        state = (jnp.zeros((TQ, 1), F32), jnp.zeros((TQ, LANE), F32))
        for t in reversed(range(n)):
            state = blk(i * n + t, state, _tail_masks(TQ, TK, t)[1])
        state = lax.fori_loop(0, i * n, lambda t, c: blk(i * n - 1 - t, c, None), state)
        o_ref[...] = state[1]

    return pl.pallas_call(
        body, name="sb_fwd", grid=(SB_HEADS, S // T),
        in_specs=[pl.BlockSpec((T, LANE), lambda h, i: (i, h)), pl.BlockSpec((S, LANE), lambda h, i: (0, 4 + h)),
                  pl.BlockSpec((S, LANE), lambda h, i: (0, 8 + h))],
        out_specs=pl.BlockSpec((T, LANE), lambda h, i: (i, h)),
        out_shape=jax.ShapeDtypeStruct((S, SB_HEADS * LANE), F32),
        compiler_params=_params(("parallel", "arbitrary")))(qkv, qkv, qkv)


def _sb_bwd(qkv, do_cat, o_cat, col0):
    S = qkv.shape[0]
    TQ, TK, n = _att_blocks(S)
    T = TQ
    nq = S // TQ

    def body(q_ref, k_ref, v_ref, do_ref, o_ref, dq_ref, dk_hbm, dv_hbm, dk_acc, dv_acc):
        h, i = pl.program_id(0), pl.program_id(1)

        @pl.when(i == 0)
        def _():
            dk_acc[...] = jnp.zeros_like(dk_acc)
            dv_acc[...] = jnp.zeros_like(dv_acc)

        qb = q_ref[...]
        dob = do_ref[...].astype(BF16)
        total = jnp.sum(dob.astype(F32) * o_ref[...], axis=1, keepdims=True)
        tri, tri_incl = _triangles(TK)

        def blk(j, state, strict):
            carry, ecarry, dq = state
            sl = pl.ds(pl.multiple_of(j * TK, TK), TK)
            kb, vb = k_ref[sl, :], v_ref[sl, :]
            a, lb, lm = _sb_block(qb, kb, tri, carry, strict)
            ab = a.astype(BF16)
            e = ab.astype(F32) * _dot_nt(dob, vb)
            prefix = total - ecarry - _split_dot(e, tri_incl)
            b = jnp.exp(lb)
            dz = e * (1.0 - b) - b * prefix
            if strict is not None:
                dz = jnp.where(strict, dz, 0.0)
            dzb = (dz * SB_SCALE).astype(BF16)
            dk_acc[sl, :] += _dot_tn(dzb, qb)
            dv_acc[sl, :] += _dot_tn(ab, dob)
            return (carry + jnp.sum(lm, axis=1, keepdims=True), ecarry + jnp.sum(e, axis=1, keepdims=True),
                    dq + _dot(dzb, kb))

        zc = jnp.zeros((TQ, 1), F32)
        state = (zc, zc, jnp.zeros((TQ, LANE), F32))
        for t in reversed(range(n)):
            state = blk(i * n + t, state, _tail_masks(TQ, TK, t)[1])
        state = lax.fori_loop(0, i * n, lambda t, c: blk(i * n - 1 - t, c, None), state)
        dq_ref[...] = state[2]

        @pl.when(i == nq - 1)
        def _():
            pltpu.sync_copy(dk_acc, dk_hbm.at[h])
            pltpu.sync_copy(dv_acc, dv_hbm.at[h])

    any_spec = pl.BlockSpec(memory_space=pl.ANY)
    return pl.pallas_call(
        body, name="sb_bwd", grid=(SB_HEADS, nq),
        in_specs=[pl.BlockSpec((T, LANE), lambda h, i: (i, h)), pl.BlockSpec((S, LANE), lambda h, i: (0, 4 + h)),
                  pl.BlockSpec((S, LANE), lambda h, i: (0, 8 + h)),
                  pl.BlockSpec((T, LANE), lambda h, i: (i, col0 + h)), pl.BlockSpec((T, LANE), lambda h, i: (i, col0 + h))],
        out_specs=[pl.BlockSpec((T, LANE), lambda h, i: (i, h)), any_spec, any_spec],
        out_shape=[jax.ShapeDtypeStruct((S, SB_HEADS * LANE), F32), jax.ShapeDtypeStruct((SB_HEADS, S, LANE), F32),
                   jax.ShapeDtypeStruct((SB_HEADS, S, LANE), F32)],
        scratch_shapes=[pltpu.VMEM((S, LANE), F32), pltpu.VMEM((S, LANE), F32)],
        compiler_params=_params(("arbitrary", "arbitrary")))(qkv, qkv, qkv, do_cat, o_cat)


def _mem_probs(q, k_ref, hh):
    lane = lax.broadcasted_iota(jnp.int32, (1, 256), 1) // 64
    msk = lane == hh
    qh = jnp.where(msk, q, 0.0).astype(BF16)
    s = _dot_nt(qh, k_ref[...]) * MEM_SCALE
    p = jnp.exp(s - jnp.max(s, axis=1, keepdims=True))
    return msk, qh, p / jnp.sum(p, axis=1, keepdims=True)


def _mem_fwd(h, mk, mv):
    S = h.shape[0]
    tm = _tile(S, 512)

    def body(q_ref, k_ref, v_ref, o_ref):
        q = q_ref[...]
        out = jnp.zeros((tm, 256), F32)
        for hh in range(MEM_HEADS):
            msk, _, p = _mem_probs(q, k_ref, hh)
            out = out + jnp.where(msk, _dot(p.astype(BF16), v_ref[...]), 0.0)
        o_ref[...] = out

    return pl.pallas_call(
        body, name="mem_fwd", grid=(S // tm,),
        in_specs=[_row_spec(tm, 256, MQ // 256), _fix_spec((256, 256)), _fix_spec((256, 256))],
        out_specs=_row_spec(tm, 256), out_shape=jax.ShapeDtypeStruct((S, 256), F32),
        compiler_params=_params(("parallel",)))(h, mk, mv)


def _mem_bwd(h, mk, mv, do_cat, col0):
    S = h.shape[0]
    tm = _tile(S, 512)

    def body(q_ref, k_ref, v_ref, do_ref, dq_ref, dk_ref, dv_ref):
        @pl.when(pl.program_id(0) == 0)
        def _():
            dk_ref[...] = jnp.zeros_like(dk_ref)
            dv_ref[...] = jnp.zeros_like(dv_ref)

        q, do = q_ref[...], do_ref[...]
        dq = jnp.zeros((tm, 256), F32)
        for hh in range(MEM_HEADS):
            msk, qh, p = _mem_probs(q, k_ref, hh)
            doh = jnp.where(msk, do, 0.0).astype(BF16)
            dp = _dot_nt(doh, v_ref[...])
            ds = (p * (dp - jnp.sum(p * dp, axis=1, keepdims=True)) * MEM_SCALE).astype(BF16)
            dq = dq + jnp.where(msk, _dot(ds, k_ref[...]), 0.0)
            dk_ref[...] += _dot_tn(ds, qh)
            dv_ref[...] += _dot_tn(p.astype(BF16), doh)
        dq_ref[...] = dq

    return pl.pallas_call(
        body, name="mem_bwd", grid=(S // tm,),
        in_specs=[_row_spec(tm, 256, MQ // 256), _fix_spec((256, 256)), _fix_spec((256, 256)),
                  _row_spec(tm, 256, col0 // 256)],
        out_specs=[_row_spec(tm, 256), _fix_spec((256, 256)), _fix_spec((256, 256))],
        out_shape=[jax.ShapeDtypeStruct((S, 256), F32), jax.ShapeDtypeStruct((256, 256), F32),
                   jax.ShapeDtypeStruct((256, 256), F32)],
        compiler_params=_params(("arbitrary",)))(h, mk, mv, do_cat)


SG_T = 128


def _sg_norm(sv, g, b):
    gv = _gelu(sv)
    xc = gv - jnp.mean(gv, axis=1, keepdims=True)
    rstd = lax.rsqrt(jnp.mean(xc * xc, axis=1, keepdims=True) + LN_EPS)
    xhat = xc * rstd
    return xhat, rstd, xhat * g + b


def _sg_fwd(h, lng, lnb, w, bias_t):
    S = h.shape[0]
    tm = _tile(S, 512)

    def body(u_ref, v_ref, g_ref, b_ref, w_ref, bias_ref, o_ref):
        mask = _chunk_mask(SG_T)
        for n in range(tm // SG_T):
            rows = slice(n * SG_T, (n + 1) * SG_T)
            u = _gelu(u_ref[rows, :])
            _, _, vn = _sg_norm(v_ref[rows, :], g_ref[...], b_ref[...])
            vb = vn.astype(BF16)
            for gi in range(4):
                cols = slice(gi * LANE, (gi + 1) * LANE)
                wg = jnp.where(mask, w_ref[gi], 0.0).astype(BF16)
                mixed = _dot(wg, vb[:, cols]) + bias_ref[:, gi:gi + 1]
                o_ref[rows, cols] = u[:, cols] * mixed

    return pl.pallas_call(
        body, name="sg_fwd", grid=(S // tm,),
        in_specs=[_row_spec(tm, 512, SGU // 512), _row_spec(tm, 512, SGV // 512), _fix_spec((1, 512)),
                  _fix_spec((1, 512)), _fix_spec((4, SG_T, SG_T)), _fix_spec((SG_T, 4))],
        out_specs=_row_spec(tm, 512), out_shape=jax.ShapeDtypeStruct((S, 512), F32),
        compiler_params=_params(("parallel",)))(h, h, lng.reshape(1, 512), lnb.reshape(1, 512), w, bias_t)


def _sg_bwd(h, lng, lnb, w, bias_t, do_cat, col0):
    S = h.shape[0]
    tm = _tile(S, 512)
    nsteps = S // tm

    def body(u_ref, v_ref, g_ref, b_ref, w_ref, bias_ref, do0_ref, do1_ref, do2_ref, do3_ref,
             du_ref, dv_ref, dw_ref, dbias_ref, dg_ref, db_ref, dvn_scr, dbias_acc):
        do_refs = (do0_ref, do1_ref, do2_ref, do3_ref)
        step = pl.program_id(0)

        @pl.when(step == 0)
        def _():
            dw_ref[...] = jnp.zeros_like(dw_ref)
            dg_ref[...] = jnp.zeros_like(dg_ref)
            db_ref[...] = jnp.zeros_like(db_ref)
            dbias_acc[...] = jnp.zeros_like(dbias_acc)

        mask = _chunk_mask(SG_T)
        for n in range(tm // SG_T):
            rows = slice(n * SG_T, (n + 1) * SG_T)
            su, sv = u_ref[rows, :], v_ref[rows, :]
            u = _gelu(su)
            xhat, rstd, vn = _sg_norm(sv, g_ref[...], b_ref[...])
            vb = vn.astype(BF16)
            ugrad = _gelu_grad(su)
            for gi in range(4):
                cols = slice(gi * LANE, (gi + 1) * LANE)
                do = do_refs[gi][rows, :]
                wg = jnp.where(mask, w_ref[gi], 0.0).astype(BF16)
                mixed = _dot(wg, vb[:, cols]) + bias_ref[:, gi:gi + 1]
                dmixed = do * u[:, cols]
                dmb = dmixed.astype(BF16)
                du_ref[rows, cols] = do * mixed * ugrad[:, cols]
                dvn_scr[:, cols] = _dot_tn(wg, dmb)
                dw_ref[gi] += jnp.where(mask, _dot_nt(dmb, vb[:, cols]), 0.0)
                dbias_acc[gi] += dmixed
            dvn = dvn_scr[...]
            dg_ref[...] += jnp.sum(dvn * xhat, axis=0, keepdims=True)
            db_ref[...] += jnp.sum(dvn, axis=0, keepdims=True)
            dxh = dvn * g_ref[...]
            dgv = rstd * (dxh - jnp.mean(dxh, axis=1, keepdims=True)
                          - xhat * jnp.mean(dxh * xhat, axis=1, keepdims=True))
            dv_ref[rows, :] = dgv * _gelu_grad(sv)

        @pl.when(step == nsteps - 1)
        def _():
            for gi in range(4):
                dbias_ref[:, gi:gi + 1] = jnp.sum(dbias_acc[gi], axis=1, keepdims=True)

    return pl.pallas_call(
        body, name="sg_bwd", grid=(nsteps,),
        in_specs=[_row_spec(tm, 512, SGU // 512), _row_spec(tm, 512, SGV // 512), _fix_spec((1, 512)),
                  _fix_spec((1, 512)), _fix_spec((4, SG_T, SG_T)), _fix_spec((SG_T, 4))]
                 + [_row_spec(tm, LANE, col0 // LANE + gi) for gi in range(4)],
        out_specs=[_row_spec(tm, 512), _row_spec(tm, 512), _fix_spec((4, SG_T, SG_T)), _fix_spec((SG_T, 4)),
                   _fix_spec((1, 512)), _fix_spec((1, 512))],
        out_shape=[jax.ShapeDtypeStruct((S, 512), F32), jax.ShapeDtypeStruct((S, 512), F32),
                   jax.ShapeDtypeStruct((4, SG_T, SG_T), F32), jax.ShapeDtypeStruct((SG_T, 4), F32),
                   jax.ShapeDtypeStruct((1, 512), F32), jax.ShapeDtypeStruct((1, 512), F32)],
        scratch_shapes=[pltpu.VMEM((SG_T, 512), F32), pltpu.VMEM((4, SG_T, SG_T), F32)],
        compiler_params=_params(("arbitrary",)))(h, h, lng.reshape(1, 512), lnb.reshape(1, 512), w, bias_t,
                                                 do_cat, do_cat, do_cat, do_cat)


def _gate_fwd(o_cat, h):
    S = h.shape[0]
    tm = _tile(S, 512)

    def body(o_ref, g_ref, y_ref):
        g = g_ref[...]
        y_ref[...] = (o_ref[...] * (g * jax.nn.sigmoid(g))).astype(BF16)

    return pl.pallas_call(
        body, name="gate_fwd", grid=(S // tm, 4),
        in_specs=[pl.BlockSpec((tm, 512), lambda i, j: (i, j)), pl.BlockSpec((tm, 512), lambda i, j: (i, GATE // 512 + j))],
        out_specs=pl.BlockSpec((tm, 512), lambda i, j: (i, j)), out_shape=jax.ShapeDtypeStruct((S, D_MODEL), BF16),
        compiler_params=_params(("parallel", "parallel")))(o_cat, h)


def _gate_bwd(dyg, o_cat, h):
    S = h.shape[0]
    tm = _tile(S, 512)

    def body(d_ref, o_ref, g_ref, do_ref, dg_ref):
        d, g = d_ref[...], g_ref[...]
        sig = jax.nn.sigmoid(g)
        do_ref[...] = d * (g * sig)
        dg_ref[...] = d * o_ref[...] * (sig * (1.0 + g * (1.0 - sig)))

    blk = pl.BlockSpec((tm, 512), lambda i, j: (i, j))
    return pl.pallas_call(
        body, name="gate_bwd", grid=(S // tm, 4),
        in_specs=[blk, blk, pl.BlockSpec((tm, 512), lambda i, j: (i, GATE // 512 + j))],
        out_specs=[blk, blk],
        out_shape=[jax.ShapeDtypeStruct((S, D_MODEL), F32), jax.ShapeDtypeStruct((S, D_MODEL), F32)],
        compiler_params=_params(("parallel", "parallel")))(dyg, o_cat, h)


def _ln_res_fwd(x, y, g, b):
    S = x.shape[0]
    tm = _tile(S, 256)

    def body(x_ref, y_ref, g_ref, b_ref, o_ref, r_ref):
        r = ALPHA * x_ref[...] + y_ref[...]
        r_ref[...] = r
        xc = r - jnp.mean(r, axis=1, keepdims=True)
        o_ref[...] = xc * lax.rsqrt(jnp.mean(xc * xc, axis=1, keepdims=True) + LN_EPS) * g_ref[...] + b_ref[...]

    return pl.pallas_call(
        body, name="ln_res_fwd", grid=(S // tm,),
        in_specs=[_row_spec(tm, D_MODEL), _row_spec(tm, D_MODEL), _fix_spec((1, D_MODEL)), _fix_spec((1, D_MODEL))],
        out_specs=[_row_spec(tm, D_MODEL), _row_spec(tm, D_MODEL)],
        out_shape=[jax.ShapeDtypeStruct((S, D_MODEL), F32), jax.ShapeDtypeStruct((S, D_MODEL), F32)],
        compiler_params=_params(("parallel",)))(x, y, g.reshape(1, D_MODEL), b.reshape(1, D_MODEL))


def _ln_res_bwd(dout, r, g):
    S = r.shape[0]
    tm = _tile(S, 256)

    def body(d_ref, r_ref, g_ref, dr_ref, dg_ref, db_ref):
        @pl.when(pl.program_id(0) == 0)
        def _():
            dg_ref[...] = jnp.zeros_like(dg_ref)
            db_ref[...] = jnp.zeros_like(db_ref)

        d, r = d_ref[...], r_ref[...]
        xc = r - jnp.mean(r, axis=1, keepdims=True)
        rstd = lax.rsqrt(jnp.mean(xc * xc, axis=1, keepdims=True) + LN_EPS)
        xhat = xc * rstd
        dxh = d * g_ref[...]
        dr_ref[...] = rstd * (dxh - jnp.mean(dxh, axis=1, keepdims=True)
                              - xhat * jnp.mean(dxh * xhat, axis=1, keepdims=True))
        dg_ref[...] += jnp.sum(d * xhat, axis=0, keepdims=True)
        db_ref[...] += jnp.sum(d, axis=0, keepdims=True)

    return pl.pallas_call(
        body, name="ln_res_bwd", grid=(S // tm,),
        in_specs=[_row_spec(tm, D_MODEL), _row_spec(tm, D_MODEL), _fix_spec((1, D_MODEL))],
        out_specs=[_row_spec(tm, D_MODEL), _fix_spec((1, D_MODEL)), _fix_spec((1, D_MODEL))],
        out_shape=[jax.ShapeDtypeStruct((S, D_MODEL), F32), jax.ShapeDtypeStruct((1, D_MODEL), F32),
                   jax.ShapeDtypeStruct((1, D_MODEL), F32)],
        compiler_params=_params(("arbitrary",)))(dout, r, g.reshape(1, D_MODEL))


def _loss_head(y, target):
    S = y.shape[0]
    tm = _tile(S, 256)

    def body(y_ref, t_ref, l_ref, d_ref):
        @pl.when(pl.program_id(0) == 0)
        def _():
            l_ref[...] = jnp.zeros_like(l_ref)

        diff = y_ref[...] - t_ref[...]
        d_ref[...] = diff * (1.0 / D_MODEL)
        per_row = jnp.mean(diff * diff, axis=1, keepdims=True)
        l_ref[...] += 0.5 * jnp.sum(per_row, axis=0, keepdims=True)

    return pl.pallas_call(
        body, name="loss_head", grid=(S // tm,), in_specs=[_row_spec(tm, D_MODEL), _row_spec(tm, D_MODEL)],
        out_specs=[_fix_spec((8, LANE)), _row_spec(tm, D_MODEL)],
        out_shape=[jax.ShapeDtypeStruct((8, LANE), F32), jax.ShapeDtypeStruct((S, D_MODEL), F32)],
        compiler_params=_params(("arbitrary",)))(y, target)


def _perm_table():
    table, at = [], 0
    for name in PERM_ORDER:
        start, width = ORIG[name]
        table.append((name, start, width, at))
        at += width
    return table


def _permute_w_in(w):
    parts = [w[:, start:start + width] for _, start, width, _ in _perm_table()]
    return jnp.concatenate(parts + [jnp.zeros((w.shape[0], HP - D_IN), w.dtype)], axis=1)


def _unpermute_w_in(wp):
    parts = sorted(_perm_table(), key=lambda t: t[1])
    return jnp.concatenate([wp[:, at:at + width] for _, _, width, at in parts], axis=1)


def _rope_tables(positions):
    inv_freq = ROPE_THETA ** (-jnp.arange(0, 64, 2, dtype=F32) / 64)
    ang = positions.astype(F32)[:, None] * inv_freq[None, :]
    cos, sin, zero = jnp.cos(ang), jnp.sin(ang), jnp.zeros((positions.shape[0], 64), F32)
    return jnp.concatenate([cos, cos, zero], axis=1), jnp.concatenate([-sin, sin, zero], axis=1)


def _local_step(x, mem, positions, target, w):
    rc, rs = _rope_tables(positions)
    mem_b = mem.astype(BF16)
    saved = []
    for l in range(DEPTH):
        w_in = _permute_w_in(w["w_in"][l])
        w_uq = jnp.pad(w["w_uq"][l].reshape(512, MLA_HEADS, 192), ((0, 0), (0, 0), (0, 64))).reshape(512, MLA_HEADS * 256)
        w_ukv = w["w_ukv"][l]
        h = _mm(x, w_in, tm=1024, tn=1152, name="in_proj")
        cq_n = _rms_fwd(h, CQ, 512, w["q_norm_g"][l], "rms_q")
        ckv_n = _rms_fwd(h, CKV, 256, w["kv_norm_g"][l], "rms_kv")
        q = _q_proj(cq_n, w_uq, rc, rs)
        kp, v = _kv_proj(ckv_n, w_ukv, h, rc, rs)
        o_a, lse = _mla_fwd(q, kp, v)
        bias_t = w["sg_b"][l].T
        o_b = _sg_fwd(h, w["sg_ln_g"][l], w["sg_ln_b"][l], w["sg_w"][l], bias_t)
        qkv = h[:, SBQ:SBQ + 1536].astype(BF16)
        o_c = _sb_fwd(qkv)
        mk = _mm(mem_b, w["w_mem_k"][l], out_dtype=BF16, name="mem_kv")
        mv = _mm(mem_b, w["w_mem_v"][l], out_dtype=BF16, name="mem_kv")
        o_m = _mem_fwd(h, mk, mv)
        o_cat = jnp.concatenate([o_a, o_b, o_c, o_m], axis=1)
        yg = _gate_fwd(o_cat, h)
        y = _mm(yg, w["w_out"][l], tm=1024, tn=1024, name="out_proj")
        x_new, r = _ln_res_fwd(x, y, w["ln_g"][l], w["ln_b"][l])
        saved.append(dict(x=x, h=h, cq_n=cq_n, ckv_n=ckv_n, q=q, kp=kp, v=v, lse=lse, qkv=qkv, mk=mk, mv=mv,
                          o_cat=o_cat, yg=yg, r=r, w_in=w_in, w_uq=w_uq, w_ukv=w_ukv, bias_t=bias_t))
        x = x_new

    loss, dx = _loss_head(x, target)

    grads = {n: [None] * DEPTH for n in SHARDED + SMALL}
    for l in reversed(range(DEPTH)):
        s = saved[l]
        h = s["h"]
        dr, dlg, dlb = _ln_res_bwd(dx, s["r"], w["ln_g"][l])
        grads["ln_g"][l], grads["ln_b"][l] = dlg[0], dlb[0]
        grads["w_out"][l] = _mm(s["yg"], dr, ta=True, tm=1024, tn=1024, tk=1024, name="dw_out")
        dyg = _mm(dr, w["w_out"][l], tb=True, tm=1024, tn=1024, name="d_out_proj")
        do_cat, dgates = _gate_bwd(dyg, s["o_cat"], h)
        dmq, dmk, dmv = _mem_bwd(h, s["mk"], s["mv"], do_cat, 1792)
        grads["w_mem_k"][l] = _mm(mem_b, dmk, ta=True, name="dw_mem")
        grads["w_mem_v"][l] = _mm(mem_b, dmv, ta=True, name="dw_mem")
        dsq, dsk, dsv = _sb_bwd(s["qkv"], do_cat, s["o_cat"], 1280 // LANE)
        dsk = dsk.transpose(1, 0, 2).reshape(-1, 512)
        dsv = dsv.transpose(1, 0, 2).reshape(-1, 512)
        du, dv, dsgw, dsgb, dsg_g, dsg_b = _sg_bwd(h, w["sg_ln_g"][l], w["sg_ln_b"][l], w["sg_w"][l], s["bias_t"],
                                                   do_cat, 768)
        grads["sg_w"][l], grads["sg_b"][l] = dsgw, dsgb.T
        grads["sg_ln_g"][l], grads["sg_ln_b"][l] = dsg_g[0], dsg_b[0]
        dq, dk, dvv = _mla_bwd(s["q"], s["kp"], s["v"], do_cat, s["o_cat"], s["lse"])
        dq_raw = _q_rope_bwd(dq, rc, rs)
        dkv, dkpe = _kv_bwd_prep(dk, dvv, rc, rs)
        dw_uq = _mm(s["cq_n"], dq_raw, ta=True, tk=1024, name="dw_uq")
        grads["w_uq"][l] = dw_uq.reshape(512, MLA_HEADS, 256)[:, :, :192].reshape(512, MLA_HEADS * 192)
        grads["w_ukv"][l] = _mm(s["ckv_n"], dkv, ta=True, tk=1024, name="dw_ukv")
        dcq_n = _mm(dq_raw, s["w_uq"], tb=True, name="d_cq")
        dckv_n = _mm(dkv, s["w_ukv"], tb=True, name="d_ckv")
        dcq, dqg = _rms_bwd(h, CQ, 512, w["q_norm_g"][l], dcq_n, "rms_q_bwd")
        dckv, dkvg = _rms_bwd(h, CKV, 256, w["kv_norm_g"][l], dckv_n, "rms_kv_bwd")
        grads["q_norm_g"][l], grads["kv_norm_g"][l] = dqg[0], dkvg[0]
        dh = jnp.concatenate([dcq, dckv, dmq, du, dv, dsq, dsk, dsv, dgates, dkpe], axis=1).astype(BF16)
        dw_in = _mm(s["x"], dh, ta=True, tm=1024, tn=1152, tk=1024, name="dw_in")
        grads["w_in"][l] = _unpermute_w_in(dw_in)
        dx = _mm(dh, s["w_in"], tb=True, add=dr, add_scale=ALPHA, tm=1024, tn=1024, tk=1152, name="d_in_proj")

    return loss, dx, {n: jnp.stack(g) for n, g in grads.items()}


MESH = pl.DeviceIdType.MESH
HBM_SPEC = pl.BlockSpec(memory_space=pltpu.HBM)


def _place():
    x, y, c = lax.axis_index("x"), lax.axis_index("y"), lax.axis_index("c")
    return x, y, c, [(1 - x, y), (x, 1 - y), (1 - x, 1 - y)]


def _gather_weights(flat):
    R = flat.shape[0]
    H = R // 2

    def body(src, out, send_sems, recv_sems):
        x, y, c, chips = _place()
        mine = pl.ds(c * H, H)
        theirs = pl.ds((1 - c) * H, H)

        def copy(k, src_ref, chip, rows, to):
            return pltpu.make_async_remote_copy(src_ref=src_ref, dst_ref=out.at[chip, rows, :], send_sem=send_sems.at[k],
                                                recv_sem=recv_sems.at[k], device_id=to, device_id_type=MESH)

        sent = [copy(j, src.at[mine, :], 2 * x + y, mine, (px, py, c)) for j, (px, py) in enumerate(chips)]
        for cp in sent:
            cp.start()
        passed = []
        for j, (px, py) in enumerate(chips):
            copy(j, src.at[mine, :], 2 * px + py, mine, (px, py, c)).wait_recv()
            cp = copy(3 + j, out.at[2 * px + py, mine, :], 2 * px + py, mine, (x, y, 1 - c))
            cp.start()
            passed.append(cp)
        for j, (px, py) in enumerate(chips):
            copy(3 + j, src.at[theirs, :], 2 * px + py, theirs, (x, y, 1 - c)).wait_recv()
        for cp in sent + passed:
            cp.wait_send()

    return pl.pallas_call(
        body, name="gather_weights", in_specs=[HBM_SPEC], out_specs=HBM_SPEC,
        out_shape=jax.ShapeDtypeStruct((4, R, FLAT_W), flat.dtype),
        scratch_shapes=[pltpu.SemaphoreType.DMA((6,)), pltpu.SemaphoreType.DMA((6,))],
        compiler_params=pltpu.CompilerParams(has_side_effects=True))(flat)


def _swap_halves(g):
    _, _, H, W = g.shape

    def body(src, out, send_sem, recv_sem):
        x, y, c, _ = _place()
        cp = pltpu.make_async_remote_copy(src_ref=src.at[:, 1 - c], dst_ref=out, send_sem=send_sem, recv_sem=recv_sem,
                                          device_id=(x, y, 1 - c), device_id_type=MESH)
        cp.start()
        cp.wait()

    return pl.pallas_call(
        body, name="swap_halves", in_specs=[HBM_SPEC], out_specs=HBM_SPEC,
        out_shape=jax.ShapeDtypeStruct((4, H, W), g.dtype),
        scratch_shapes=[pltpu.SemaphoreType.DMA(()), pltpu.SemaphoreType.DMA(())],
        compiler_params=pltpu.CompilerParams(has_side_effects=True))(g)


def _pair_sum(g, other, c):
    _, _, H, W = g.shape
    th = _row_tile(H, 3 * W * 4)

    def body(c_ref, a_ref, b_ref, o_ref):
        o_ref[...] = (a_ref[...] + b_ref[...]).astype(BF16)

    return pl.pallas_call(
        body, name="pair_sum",
        grid_spec=pltpu.PrefetchScalarGridSpec(
            num_scalar_prefetch=1, grid=(4, H // th),
            in_specs=[pl.BlockSpec((None, None, th, W), lambda d, i, c_ref: (d, c_ref[0], i, 0)),
                      pl.BlockSpec((None, th, W), lambda d, i, c_ref: (d, i, 0))],
            out_specs=pl.BlockSpec((None, th, W), lambda d, i, c_ref: (d, i, 0))),
        out_shape=jax.ShapeDtypeStruct((4, H, W), BF16),
        compiler_params=_params(("parallel", "parallel")))(c, g, other)


def _exchange_chips(p):
    _, H, W = p.shape

    def body(src, out, send_sems, recv_sems):
        x, y, c, chips = _place()
        sent = []
        for j, (px, py) in enumerate(chips):
            cp = pltpu.make_async_remote_copy(src_ref=src.at[2 * px + py], dst_ref=out.at[j], send_sem=send_sems.at[j],
                                              recv_sem=recv_sems.at[j], device_id=(px, py, c), device_id_type=MESH)
            cp.start()
            sent.append(cp)
        for cp in sent:
            cp.wait()

    return pl.pallas_call(
        body, name="exchange_chips", in_specs=[HBM_SPEC], out_specs=HBM_SPEC,
        out_shape=jax.ShapeDtypeStruct((3, H, W), p.dtype),
        scratch_shapes=[pltpu.SemaphoreType.DMA((3,)), pltpu.SemaphoreType.DMA((3,))],
        compiler_params=pltpu.CompilerParams(has_side_effects=True))(p)


def _chip_sum(p, got, me):
    _, H, W = p.shape
    th = _row_tile(H, 4 * W * 4)

    def body(me_ref, p_ref, g_ref, o_ref):
        acc = p_ref[...].astype(F32)
        for k in range(3):
            acc = acc + g_ref[k].astype(F32)
        o_ref[...] = acc

    return pl.pallas_call(
        body, name="chip_sum",
        grid_spec=pltpu.PrefetchScalarGridSpec(
            num_scalar_prefetch=1, grid=(H // th,),
            in_specs=[pl.BlockSpec((None, th, W), lambda i, me_ref: (me_ref[0], i, 0)),
                      pl.BlockSpec((3, th, W), lambda i, me_ref: (0, i, 0))],
            out_specs=pl.BlockSpec((th, W), lambda i, me_ref: (i, 0))),
        out_shape=jax.ShapeDtypeStruct((H, W), F32), compiler_params=_params(("parallel",)))(me, p, got)


def _sum_parts(t, name):
    n, H, W = t.shape
    th = _row_tile(H, (n + 1) * W * 4)

    def body(t_ref, o_ref):
        acc = t_ref[0]
        for k in range(1, n):
            acc = acc + t_ref[k]
        o_ref[...] = acc

    return pl.pallas_call(
        body, name=name, grid=(H // th,), in_specs=[pl.BlockSpec((n, th, W), lambda i: (0, i, 0))],
        out_specs=pl.BlockSpec((th, W), lambda i: (i, 0)), out_shape=jax.ShapeDtypeStruct((H, W), F32),
        compiler_params=_params(("parallel",)))(t)


def _share_with_sibling(half):
    H, W = half.shape

    def body(src, out, send_sem, recv_sem):
        x, y, c, _ = _place()
        cp = pltpu.make_async_remote_copy(src_ref=src, dst_ref=out.at[c], send_sem=send_sem, recv_sem=recv_sem,
                                          device_id=(x, y, 1 - c), device_id_type=MESH)
        cp.start()
        pltpu.make_async_remote_copy(src_ref=src, dst_ref=out.at[1 - c], send_sem=send_sem, recv_sem=recv_sem,
                                     device_id=(x, y, 1 - c), device_id_type=MESH).wait_recv()
        cp.wait_send()

    return pl.pallas_call(
        body, name="share_with_sibling", in_specs=[HBM_SPEC], out_specs=HBM_SPEC,
        out_shape=jax.ShapeDtypeStruct((2, H, W), half.dtype),
        scratch_shapes=[pltpu.SemaphoreType.DMA(()), pltpu.SemaphoreType.DMA(())],
        compiler_params=pltpu.CompilerParams(has_side_effects=True))(half)


def _gather_all(v):
    n, W = v.shape

    def body(src, out, send_sems, recv_sems, own_sem):
        x, y, c, _ = _place()
        own = pltpu.make_async_copy(src, out.at[4 * x + 2 * y + c], own_sem)
        own.start()
        flips = [(fx, fy, fc) for fx in (0, 1) for fy in (0, 1) for fc in (0, 1)][1:]
        sent = []
        for k, (fx, fy, fc) in enumerate(flips):
            cp = pltpu.make_async_remote_copy(
                src_ref=src, dst_ref=out.at[4 * x + 2 * y + c], send_sem=send_sems.at[k], recv_sem=recv_sems.at[k],
                device_id=(x ^ fx, y ^ fy, c ^ fc), device_id_type=MESH)
            cp.start()
            sent.append(cp)
        for k, (fx, fy, fc) in enumerate(flips):
            pltpu.make_async_remote_copy(
                src_ref=src, dst_ref=out.at[4 * (x ^ fx) + 2 * (y ^ fy) + (c ^ fc)], send_sem=send_sems.at[k],
                recv_sem=recv_sems.at[k], device_id=(x ^ fx, y ^ fy, c ^ fc), device_id_type=MESH).wait_recv()
        for cp in sent:
            cp.wait_send()
        own.wait()

    return pl.pallas_call(
        body, name="gather_all", in_specs=[HBM_SPEC], out_specs=HBM_SPEC,
        out_shape=jax.ShapeDtypeStruct((8, n, W), v.dtype),
        scratch_shapes=[pltpu.SemaphoreType.DMA((7,)), pltpu.SemaphoreType.DMA((7,)), pltpu.SemaphoreType.DMA(())],
        compiler_params=pltpu.CompilerParams(has_side_effects=True))(v)


def _adamw(w, g, m, v):
    shape = w.shape
    cols = shape[-1]
    w2, g2, m2, v2 = (a.reshape(-1, cols) for a in (w, g, m, v))
    rows = w2.shape[0]
    tr = next((t for t in (1024, 512, 256, 128, 64, 32, 16, 8) if rows % t == 0 and t * cols * 4 <= (2 << 20)), rows)

    def body(w_ref, g_ref, m_ref, v_ref, d_ref, nm_ref, nv_ref):
        g_ = g_ref[...]
        nm = ADAM_B1 * m_ref[...] + (1.0 - ADAM_B1) * g_
        nv = ADAM_B2 * v_ref[...] + (1.0 - ADAM_B2) * (g_ * g_)
        m_hat = nm / (1.0 - ADAM_B1 ** ADAM_STEP)
        v_hat = nv / (1.0 - ADAM_B2 ** ADAM_STEP)
        d_ref[...] = -ADAM_LR * (m_hat / (jnp.sqrt(v_hat) + ADAM_EPS) + ADAM_WD * w_ref[...])
        nm_ref[...] = nm
        nv_ref[...] = nv

    blk = pl.BlockSpec((tr, cols), lambda i: (i, 0))
    outs = pl.pallas_call(
        body, name="adamw", grid=(rows // tr,), in_specs=[blk] * 4, out_specs=[blk] * 3,
        out_shape=[jax.ShapeDtypeStruct((rows, cols), F32)] * 3, compiler_params=_params(("parallel",)))(w2, g2, m2, v2)
    return tuple(o.reshape(shape) for o in outs)


def _chip_part(name, a, k):
    n = a.shape[2 if name in ("w_in", "w_uq", "w_ukv") else 1] // 4
    return a[:, :, k * n:(k + 1) * n] if name in ("w_in", "w_uq", "w_ukv") else a[:, k * n:(k + 1) * n]


def _join_chips(name, parts):
    return jnp.concatenate(parts, axis=2 if name in ("w_in", "w_uq", "w_ukv") else 1)


def kernel(x, mem, positions, w_in, q_norm_g, w_uq, kv_norm_g, w_ukv, sg_ln_g, sg_ln_b, sg_w, sg_b, w_mem_k, w_mem_v, w_out, ln_g, ln_b, loss_target, m_w_in, m_q_norm_g, m_w_uq, m_kv_norm_g, m_w_ukv, m_sg_ln_g, m_sg_ln_b, m_sg_w, m_sg_b, m_w_mem_k, m_w_mem_v, m_w_out, m_ln_g, m_ln_b, v_w_in, v_q_norm_g, v_w_uq, v_kv_norm_g, v_w_ukv, v_sg_ln_g, v_sg_ln_b, v_sg_w, v_sg_b, v_w_mem_k, v_w_mem_v, v_w_out, v_ln_g, v_ln_b):
    weights = dict(w_in=w_in, q_norm_g=q_norm_g, w_uq=w_uq, kv_norm_g=kv_norm_g, w_ukv=w_ukv, sg_ln_g=sg_ln_g,
                   sg_ln_b=sg_ln_b, sg_w=sg_w, sg_b=sg_b, w_mem_k=w_mem_k, w_mem_v=w_mem_v, w_out=w_out, ln_g=ln_g, ln_b=ln_b)
    mom_m = dict(w_in=m_w_in, q_norm_g=m_q_norm_g, w_uq=m_w_uq, kv_norm_g=m_kv_norm_g, w_ukv=m_w_ukv, sg_ln_g=m_sg_ln_g,
                 sg_ln_b=m_sg_ln_b, sg_w=m_sg_w, sg_b=m_sg_b, w_mem_k=m_w_mem_k, w_mem_v=m_w_mem_v, w_out=m_w_out,
                 ln_g=m_ln_g, ln_b=m_ln_b)
    mom_v = dict(w_in=v_w_in, q_norm_g=v_q_norm_g, w_uq=v_w_uq, kv_norm_g=v_kv_norm_g, w_ukv=v_w_ukv, sg_ln_g=v_sg_ln_g,
                 sg_ln_b=v_sg_ln_b, sg_w=v_sg_w, sg_b=v_sg_b, w_mem_k=v_w_mem_k, w_mem_v=v_w_mem_v, w_out=v_w_out,
                 ln_g=v_ln_g, ln_b=v_ln_b)
    c_idx = lax.axis_index("c").astype(jnp.int32).reshape(1)

    sizes = [weights[n].size for n in SHARDED]
    flat = jnp.concatenate([weights[n].astype(BF16).reshape(-1) for n in SHARDED]).reshape(-1, FLAT_W)
    me = 2 * lax.axis_index("x") + lax.axis_index("y")
    gathered = lax.dynamic_update_slice(_gather_weights(flat), flat[None], (me, 0, 0))
    full = dict((n, weights[n]) for n in SMALL)
    at = 0
    for n, size in zip(SHARDED, sizes):
        rows = size // FLAT_W
        full[n] = _join_chips(n, [gathered[k, at:at + rows].reshape(weights[n].shape) for k in range(4)])
        at += rows
    R = at

    loss_dev, grad_x, grads = _local_step(x[0], mem[0], positions[0], loss_target[0], full)

    per_chip = [jnp.concatenate([_chip_part(n, grads[n], k).reshape(-1) for n in SHARDED]) for k in range(4)]
    g4 = jnp.stack(per_chip).reshape(4, 2, R // 2, FLAT_W)
    pair = _pair_sum(g4, _swap_halves(g4), c_idx)
    half = _chip_sum(pair, _exchange_chips(pair), me.astype(jnp.int32).reshape(1))
    reduced = lax.dynamic_update_slice(_share_with_sibling(half), half[None], (c_idx[0], 0, 0)).reshape(R, FLAT_W)
    grad_out = {}
    at = 0
    for n, size in zip(SHARDED, sizes):
        rows = size // FLAT_W
        grad_out[n] = reduced[at:at + rows].reshape(weights[n].shape)
        at += rows

    small_sizes = [weights[n].size for n in SMALL]
    vec = jnp.concatenate([grads[n].reshape(-1) for n in SMALL] + [loss_dev[0]])
    n_small = vec.shape[0]
    rows_small = -(-n_small // (8 * FLAT_W)) * 8
    vec = jnp.pad(vec, (0, rows_small * FLAT_W - n_small)).reshape(rows_small, FLAT_W)
    total = _sum_parts(_gather_all(vec), "device_sum").reshape(-1)
    at = 0
    for n, size in zip(SMALL, small_sizes):
        grad_out[n] = total[at:at + size].reshape(weights[n].shape)
        at += size
    loss = total[at]

    names = list(weights)
    upd = {n: _adamw(weights[n], grad_out[n], mom_m[n], mom_v[n]) for n in names}
    return (loss, grad_x[None], *[grad_out[n] for n in names], *[upd[n][0] for n in names],
            *[upd[n][1] for n in names], *[upd[n][2] for n in names])
```

```python
import math

import jax
import jax.numpy as jnp
from jax import lax
from jax.experimental import pallas as pl
from jax.experimental.pallas import tpu as pltpu

F32, BF16 = jnp.float32, jnp.bfloat16

D_MODEL = 2048
DEPTH = 4
CHUNK = 64
MLA_HEADS = 6
MLA_SCALE = 1.0 / math.sqrt(192.0)
SB_HEADS = 4
SB_SCALE = 1.0 / math.sqrt(128.0)
MEM_HEADS = 4
MEM_SCALE = 1.0 / math.sqrt(64.0)
ROPE_THETA = 10000.0
ALPHA = (2.0 * DEPTH) ** 0.25
LN_EPS = 1e-5
RMS_EPS = 1e-6
ADAM_LR, ADAM_B1, ADAM_B2, ADAM_EPS, ADAM_WD, ADAM_STEP = 0.001, 0.9, 0.999, 1e-08, 0.01, 10

ORIG = dict(c_q=(0, 512), c_kv=(512, 256), k_pe=(768, 64), g_a=(832, 768), sg_u=(1600, 512), sg_v=(2112, 512),
            g_b=(2624, 512), sb_q=(3136, 512), sb_k=(3648, 512), sb_v=(4160, 512), g_c=(4672, 512),
            m_q=(5184, 256), g_m=(5440, 256))
D_IN = 5696
PERM_ORDER = ("c_q", "c_kv", "m_q", "sg_u", "sg_v", "sb_q", "sb_k", "sb_v", "g_a", "g_b", "g_c", "g_m", "k_pe")
HP = 5760
CQ, CKV, MQ, SGU, SGV, SBQ, GATE, KPE = 0, 512, 768, 1024, 1536, 2048, 3584, 5632

Q_BLK = 1024
K_BLK = 256
LANE = 128
VMEM_LIMIT = 56 * 1024 * 1024

FLAT_W = 1024
SHARDED = ("w_in", "w_uq", "w_ukv", "w_mem_k", "w_mem_v", "w_out")
SMALL = ("q_norm_g", "kv_norm_g", "sg_ln_g", "sg_ln_b", "sg_w", "sg_b", "ln_g", "ln_b")


def _params(sem=None):
    return pltpu.CompilerParams(dimension_semantics=sem, vmem_limit_bytes=VMEM_LIMIT)


def _tile(dim, pref):
    if dim <= pref:
        return dim
    t = (pref // LANE) * LANE
    while t >= LANE:
        if dim % t == 0:
            return t
        t -= LANE
    return dim


def _row_tile(rows, bytes_per_row, budget=8 << 20):
    best = None
    for t in range(8, rows + 1, 8):
        if rows % t == 0 and t * bytes_per_row <= budget:
            best = t
    return best if best else rows


def _dot_nt(a, b):
    return lax.dot_general(a, b, (((1,), (1,)), ((), ())), preferred_element_type=F32)


def _dot_tn(a, b):
    return lax.dot_general(a, b, (((0,), (0,)), ((), ())), preferred_element_type=F32)


def _dot(a, b):
    return jnp.dot(a, b, preferred_element_type=F32)


def _mm(a, b, *, ta=False, tb=False, a_win=None, b_win=None, add=None, add_scale=1.0, out_dtype=F32,
        tm=512, tn=512, tk=512, name="mm"):
    a_off, a_w = a_win if a_win else (0, a.shape[1])
    b_off, b_w = b_win if b_win else (0, b.shape[1])
    (K, M) = (a.shape[0], a_w) if ta else (a_w, a.shape[0])
    (N, Kb) = (b.shape[0], b_w) if tb else (b_w, b.shape[0])
    assert K == Kb, (a.shape, b.shape, ta, tb)
    tm, tn, tk = _tile(M, tm), _tile(N, tn), _tile(K, tk)
    nk = K // tk
    if ta:
        assert a_off % tm == 0
        a_spec = pl.BlockSpec((tk, tm), lambda i, j, k: (k, i + a_off // tm))
    else:
        assert a_off % tk == 0
        a_spec = pl.BlockSpec((tm, tk), lambda i, j, k: (i, k + a_off // tk))
    if tb:
        assert b_off % tk == 0
        b_spec = pl.BlockSpec((tn, tk), lambda i, j, k: (j, k + b_off // tk))
    else:
        assert b_off % tn == 0
        b_spec = pl.BlockSpec((tk, tn), lambda i, j, k: (k, j + b_off // tn))
    o_spec = pl.BlockSpec((tm, tn), lambda i, j, k: (i, j))
    dn = (((0 if ta else 1,), (1 if tb else 0,)), ((), ()))
    has_add = add is not None

    def body(*refs):
        a_ref, b_ref = refs[:2]
        add_ref = refs[2] if has_add else None
        o_ref = refs[3 if has_add else 2]
        part = lax.dot_general(a_ref[...].astype(BF16), b_ref[...].astype(BF16), dn, preferred_element_type=F32)

        def finish(r):
            if has_add:
                r = r + add_scale * add_ref[...]
            o_ref[...] = r.astype(o_ref.dtype)

        if nk == 1:
            finish(part)
            return
        acc_ref = refs[-1]
        k = pl.program_id(2)

        @pl.when(k == 0)
        def _():
            acc_ref[...] = part

        @pl.when(k > 0)
        def _():
            acc_ref[...] += part

        @pl.when(k == nk - 1)
        def _():
            finish(acc_ref[...])

    ins = [a, b] + ([add] if has_add else [])
    specs = [a_spec, b_spec] + ([o_spec] if has_add else [])
    return pl.pallas_call(
        body, name=name, grid=(M // tm, N // tn, nk), in_specs=specs, out_specs=o_spec,
        out_shape=jax.ShapeDtypeStruct((M, N), out_dtype),
        scratch_shapes=[pltpu.VMEM((tm, tn), F32)] if nk > 1 else [],
        compiler_params=_params(("parallel", "parallel", "arbitrary")))(*ins)


GELU_K = math.sqrt(2.0 / math.pi)


def _gelu(x):
    t = jnp.tanh(GELU_K * (x + 0.044715 * (x * x * x)))
    return 0.5 * x * (1.0 + t)


def _gelu_grad(x):
    t = jnp.tanh(GELU_K * (x + 0.044715 * (x * x * x)))
    return 0.5 * (1.0 + t) + 0.5 * x * (1.0 - t * t) * GELU_K * (1.0 + 3.0 * 0.044715 * x * x)


def _rope_swap(t):
    lane = lax.broadcasted_iota(jnp.int32, t.shape, 1)
    return jnp.where(lane < 32, pltpu.roll(t, 96, axis=1), pltpu.roll(t, 32, axis=1))


def _rope(t, c, s):
    return t * c + _rope_swap(t) * s


def _rope_bwd(dt, c, s):
    return dt * c - _rope_swap(dt) * s


def _row_spec(tm, w, cb=0):
    return pl.BlockSpec((tm, w), lambda i: (i, cb))


def _fix_spec(shape):
    return pl.BlockSpec(shape, lambda *_: (0,) * len(shape))


def _rms_fwd(h, off, width, g, name):
    S = h.shape[0]
    tm = _tile(S, 512)

    def body(x_ref, g_ref, o_ref):
        x = x_ref[...]
        r = lax.rsqrt(jnp.mean(x * x, axis=1, keepdims=True) + RMS_EPS)
        o_ref[...] = (x * r * g_ref[...]).astype(BF16)

    return pl.pallas_call(
        body, name=name, grid=(S // tm,), in_specs=[_row_spec(tm, width, off // width), _fix_spec((1, width))],
        out_specs=_row_spec(tm, width), out_shape=jax.ShapeDtypeStruct((S, width), BF16),
        compiler_params=_params(("parallel",)))(h, g.reshape(1, width))


def _rms_bwd(h, off, width, g, dxn, name):
    S = h.shape[0]
    tm = _tile(S, 512)

    def body(x_ref, g_ref, d_ref, dx_ref, dg_ref):
        @pl.when(pl.program_id(0) == 0)
        def _():
            dg_ref[...] = jnp.zeros_like(dg_ref)

        x, d = x_ref[...], d_ref[...]
        r = lax.rsqrt(jnp.mean(x * x, axis=1, keepdims=True) + RMS_EPS)
        gd = d * g_ref[...]
        dx_ref[...] = gd * r - x * (r * r * r) * jnp.mean(gd * x, axis=1, keepdims=True)
        dg_ref[...] += jnp.sum(d * x * r, axis=0, keepdims=True)

    return pl.pallas_call(
        body, name=name, grid=(S // tm,),
        in_specs=[_row_spec(tm, width, off // width), _fix_spec((1, width)), _row_spec(tm, width)],
        out_specs=[_row_spec(tm, width), _fix_spec((1, width))],
        out_shape=[jax.ShapeDtypeStruct((S, width), F32), jax.ShapeDtypeStruct((1, width), F32)],
        compiler_params=_params(("arbitrary",)))(h, g.reshape(1, width), dxn)


def _q_proj(xn, w, rc, rs):
    S = xn.shape[0]
    tm = _tile(S, 512)

    def body(x_ref, w_ref, c_ref, s_ref, q_ref):
        q = _dot(x_ref[...], w_ref[...]) * MLA_SCALE
        q_ref[:, :LANE] = q[:, :LANE].astype(BF16)
        q_ref[:, LANE:] = _rope(q[:, LANE:], c_ref[...], s_ref[...]).astype(BF16)

    return pl.pallas_call(
        body, name="q_proj", grid=(S // tm, MLA_HEADS),
        in_specs=[pl.BlockSpec((tm, 512), lambda i, j: (i, 0)), pl.BlockSpec((512, 256), lambda i, j: (0, j)),
                  pl.BlockSpec((tm, LANE), lambda i, j: (i, 0)), pl.BlockSpec((tm, LANE), lambda i, j: (i, 0))],
        out_specs=pl.BlockSpec((tm, 256), lambda i, j: (i, j)),
        out_shape=jax.ShapeDtypeStruct((S, MLA_HEADS * 256), BF16),
        compiler_params=_params(("parallel", "parallel")))(xn, w, rc, rs)


def _kv_proj(xn, w, h, rc, rs):
    S = xn.shape[0]
    tm = _tile(S, 512)

    def body(x_ref, w_ref, pe_ref, c_ref, s_ref, k_ref, v_ref):
        kv = _dot(x_ref[...], w_ref[...])
        k_ref[:, :LANE] = kv[:, :LANE].astype(BF16)
        k_ref[:, LANE:] = _rope(pe_ref[...], c_ref[...], s_ref[...]).astype(BF16)
        v_ref[...] = kv[:, LANE:].astype(BF16)

    return pl.pallas_call(
        body, name="kv_proj", grid=(S // tm, MLA_HEADS),
        in_specs=[pl.BlockSpec((tm, 256), lambda i, j: (i, 0)), pl.BlockSpec((256, 256), lambda i, j: (0, j)),
                  pl.BlockSpec((tm, LANE), lambda i, j: (i, KPE // LANE)),
                  pl.BlockSpec((tm, LANE), lambda i, j: (i, 0)), pl.BlockSpec((tm, LANE), lambda i, j: (i, 0))],
        out_specs=[pl.BlockSpec((tm, 256), lambda i, j: (i, j)), pl.BlockSpec((tm, LANE), lambda i, j: (i, j))],
        out_shape=[jax.ShapeDtypeStruct((S, MLA_HEADS * 256), BF16), jax.ShapeDtypeStruct((S, MLA_HEADS * LANE), BF16)],
        compiler_params=_params(("parallel", "parallel")))(xn, w, h, rc, rs)


def _q_rope_bwd(dq, rc, rs):
    S = dq.shape[0]
    tm = _tile(S, 512)

    def body(d_ref, c_ref, s_ref, o_ref):
        o_ref[:, :LANE] = d_ref[:, :LANE].astype(BF16)
        o_ref[:, LANE:] = _rope_bwd(d_ref[:, LANE:], c_ref[...], s_ref[...]).astype(BF16)

    return pl.pallas_call(
        body, name="q_rope_bwd", grid=(S // tm, MLA_HEADS),
        in_specs=[pl.BlockSpec((tm, 256), lambda i, j: (i, j)),
                  pl.BlockSpec((tm, LANE), lambda i, j: (i, 0)), pl.BlockSpec((tm, LANE), lambda i, j: (i, 0))],
        out_specs=pl.BlockSpec((tm, 256), lambda i, j: (i, j)),
        out_shape=jax.ShapeDtypeStruct((S, MLA_HEADS * 256), BF16),
        compiler_params=_params(("parallel", "parallel")))(dq, rc, rs)


def _kv_bwd_prep(dk, dv, rc, rs):
    S = dk.shape[1]
    tm = _tile(S, 512)

    def body(dk_ref, dv_ref, c_ref, s_ref, o_ref, pe_ref):
        rot = jnp.zeros((tm, LANE), F32)
        for hh in range(MLA_HEADS):
            o_ref[:, hh * 256:hh * 256 + LANE] = dk_ref[hh, :, :LANE].astype(BF16)
            o_ref[:, hh * 256 + LANE:(hh + 1) * 256] = dv_ref[hh].astype(BF16)
            rot = rot + dk_ref[hh, :, LANE:]
        pe_ref[...] = _rope_bwd(rot, c_ref[...], s_ref[...])

    return pl.pallas_call(
        body, name="kv_bwd_prep", grid=(S // tm,),
        in_specs=[pl.BlockSpec((MLA_HEADS, tm, 256), lambda i: (0, i, 0)),
                  pl.BlockSpec((MLA_HEADS, tm, LANE), lambda i: (0, i, 0)), _row_spec(tm, LANE), _row_spec(tm, LANE)],
        out_specs=[_row_spec(tm, MLA_HEADS * 256), _row_spec(tm, LANE)],
        out_shape=[jax.ShapeDtypeStruct((S, MLA_HEADS * 256), BF16), jax.ShapeDtypeStruct((S, LANE), F32)],
        compiler_params=_params(("parallel",)))(dk, dv, rc, rs)


def _chunk_mask(T):
    row = lax.broadcasted_iota(jnp.int32, (T, T), 0)
    col = lax.broadcasted_iota(jnp.int32, (T, T), 1)
    return (col // CHUNK) <= (row // CHUNK)


def _att_blocks(S):
    tq = min(Q_BLK, S)
    tk = min(K_BLK, tq)
    return tq, tk, tq // tk


def _tail_masks(rows, tk):
    row = lax.broadcasted_iota(jnp.int32, (rows, tk), 0)
    col = lax.broadcasted_iota(jnp.int32, (rows, tk), 1)
    return (col // CHUNK) <= (row // CHUNK), col < row


def _put_rows(old, new, r0):
    return new if r0 == 0 else jnp.concatenate([old[:r0], new], axis=0)


def _mla_fwd(q, kp, v):
    S = q.shape[0]
    TQ, TK, n = _att_blocks(S)

    def body(q_ref, k_ref, v_ref, o_ref, lse_ref):
        i = pl.program_id(1)

        def update(j, carry, r0, masked):
            m, l, acc = (c[r0:] for c in carry)
            sl = pl.ds(pl.multiple_of(j * TK, TK), TK)
            s = _dot_nt(q_ref[r0:, :], k_ref[sl, :])
            if masked:
                s = jnp.where(_tail_masks(TQ - r0, TK)[0], s, -1e30)
            m_new = jnp.maximum(m, jnp.max(s, axis=1, keepdims=True))
            a = jnp.exp(m - m_new)
            p = jnp.exp(s - m_new)
            new = (m_new, a * l + jnp.sum(p, axis=1, keepdims=True), a * acc + _dot(p.astype(BF16), v_ref[sl, :]))
            return tuple(_put_rows(c, u, r0) for c, u in zip(carry, new))

        carry = (jnp.full((TQ, 1), -1e30, F32), jnp.zeros((TQ, 1), F32), jnp.zeros((TQ, LANE), F32))
        carry = lax.fori_loop(0, i * n, lambda j, c: update(j, c, 0, False), carry)
        for t in range(n):
            carry = update(i * n + t, carry, t * TK, True)
        m, l, acc = carry
        o_ref[...] = acc / l
        lse_ref[...] = jnp.broadcast_to(m + jnp.log(l), (TQ, LANE))

    return pl.pallas_call(
        body, name="mla_fwd", grid=(MLA_HEADS, S // TQ),
        in_specs=[pl.BlockSpec((TQ, 256), lambda h, i: (i, h)), pl.BlockSpec((S, 256), lambda h, i: (0, h)),
                  pl.BlockSpec((S, LANE), lambda h, i: (0, h))],
        out_specs=[pl.BlockSpec((TQ, LANE), lambda h, i: (i, h)), pl.BlockSpec((TQ, LANE), lambda h, i: (i, h))],
        out_shape=[jax.ShapeDtypeStruct((S, MLA_HEADS * LANE), F32), jax.ShapeDtypeStruct((S, MLA_HEADS * LANE), F32)],
        compiler_params=_params(("parallel", "arbitrary")))(q, kp, v)


def _mla_bwd(q, kp, v, do_cat, o_cat, lse):
    S = q.shape[0]
    TQ, TK, n = _att_blocks(S)
    nq = S // TQ

    def body(q_ref, k_ref, v_ref, do_ref, o_ref, lse_ref, dq_ref, dk_hbm, dv_hbm, dk_acc, dv_acc):
        h, i = pl.program_id(0), pl.program_id(1)

        @pl.when(i == 0)
        def _():
            dk_acc[...] = jnp.zeros_like(dk_acc)
            dv_acc[...] = jnp.zeros_like(dv_acc)

        do32 = do_ref[...]
        dob = do32.astype(BF16)
        delta = jnp.sum(do32 * o_ref[...], axis=1, keepdims=True)
        lse_col = lse_ref[:, :1]

        def blk(j, dq, r0, masked):
            sl = pl.ds(pl.multiple_of(j * TK, TK), TK)
            kb, vb, qb = k_ref[sl, :], v_ref[sl, :], q_ref[r0:, :]
            s = _dot_nt(qb, kb)
            if masked:
                s = jnp.where(_tail_masks(TQ - r0, TK)[0], s, -1e30)
            p = jnp.exp(s - lse_col[r0:])
            ds = (p * (_dot_nt(dob[r0:], vb) - delta[r0:])).astype(BF16)
            dk_acc[sl, :] += _dot_tn(ds, qb)
            dv_acc[sl, :] += _dot_tn(p.astype(BF16), dob[r0:])
            return _put_rows(dq, dq[r0:] + _dot(ds, kb), r0)

        dq = lax.fori_loop(0, i * n, lambda j, c: blk(j, c, 0, False), jnp.zeros((TQ, 256), F32))
        for t in range(n):
            dq = blk(i * n + t, dq, t * TK, True)
        dq_ref[...] = dq * MLA_SCALE

        @pl.when(i == nq - 1)
        def _():
            pltpu.sync_copy(dk_acc, dk_hbm.at[h])
            pltpu.sync_copy(dv_acc, dv_hbm.at[h])

    any_spec = pl.BlockSpec(memory_space=pl.ANY)
    T = TQ
    return pl.pallas_call(
        body, name="mla_bwd", grid=(MLA_HEADS, nq),
        in_specs=[pl.BlockSpec((T, 256), lambda h, i: (i, h)), pl.BlockSpec((S, 256), lambda h, i: (0, h)),
                  pl.BlockSpec((S, LANE), lambda h, i: (0, h)), pl.BlockSpec((T, LANE), lambda h, i: (i, h)),
                  pl.BlockSpec((T, LANE), lambda h, i: (i, h)), pl.BlockSpec((T, LANE), lambda h, i: (i, h))],
        out_specs=[pl.BlockSpec((T, 256), lambda h, i: (i, h)), any_spec, any_spec],
        out_shape=[jax.ShapeDtypeStruct((S, MLA_HEADS * 256), F32), jax.ShapeDtypeStruct((MLA_HEADS, S, 256), F32),
                   jax.ShapeDtypeStruct((MLA_HEADS, S, LANE), F32)],
        scratch_shapes=[pltpu.VMEM((S, 256), F32), pltpu.VMEM((S, LANE), F32)],
        compiler_params=_params(("arbitrary", "arbitrary")))(q, kp, v, do_cat, o_cat, lse)


def _split_dot(x, tri):
    top = lax.bitcast_convert_type(lax.bitcast_convert_type(x, jnp.uint32) & jnp.uint32(0xFFFF0000), F32)
    return _dot(top.astype(BF16), tri) + _dot((x - top).astype(BF16), tri)


def _sb_block(qb, kb, tri, carry, masked):
    z = _dot_nt(qb, kb)
    lb = jnp.minimum(z, 0.0) - jnp.log(1.0 + jnp.exp(-jnp.abs(z)))
    lm = lb - z
    strict = None
    if masked:
        strict = _tail_masks(z.shape[0], z.shape[1])[1]
        lm = jnp.where(strict, lm, 0.0)
    a = jnp.exp(lb + carry + _split_dot(lm, tri))
    if masked:
        a = jnp.where(strict, a, 0.0)
    return a, lb, lm, strict


def _triangle(tk):
    row = lax.broadcasted_iota(jnp.int32, (tk, tk), 0)
    col = lax.broadcasted_iota(jnp.int32, (tk, tk), 1)
    return (row > col).astype(BF16)


def _sb_fwd(qkv):
    S = qkv.shape[0]
    TQ, TK, n = _att_blocks(S)
    T = TQ

    def body(q_ref, k_ref, v_ref, o_ref):
        i = pl.program_id(1)
        tri = _triangle(TK)

        def blk(j, state, r0, masked):
            carry, acc = (c[r0:] for c in state)
            sl = pl.ds(pl.multiple_of(j * TK, TK), TK)
            a, _, lm, _ = _sb_block(q_ref[r0:, :], k_ref[sl, :], tri, carry, masked)
            new = (carry + jnp.sum(lm, axis=1, keepdims=True), acc + _dot(a.astype(BF16), v_ref[sl, :]))
            return tuple(_put_rows(c, u, r0) for c, u in zip(state, new))

        state = (jnp.zeros((TQ, 1), F32), jnp.zeros((TQ, LANE), F32))
        for t in reversed(range(n)):
            state = blk(i * n + t, state, t * TK, True)
        state = lax.fori_loop(0, i * n, lambda t, c: blk(i * n - 1 - t, c, 0, False), state)
        o_ref[...] = state[1]

    return pl.pallas_call(
        body, name="sb_fwd", grid=(SB_HEADS, S // T),
        in_specs=[pl.BlockSpec((T, LANE), lambda h, i: (i, h)), pl.BlockSpec((S, LANE), lambda h, i: (0, 4 + h)),
                  pl.BlockSpec((S, LANE), lambda h, i: (0, 8 + h))],
        out_specs=pl.BlockSpec((T, LANE), lambda h, i: (i, h)),
        out_shape=jax.ShapeDtypeStruct((S, SB_HEADS * LANE), F32),
        compiler_params=_params(("parallel", "arbitrary")))(qkv, qkv, qkv)


def _sb_bwd(qkv, do_cat, o_cat, col0):
    S = qkv.shape[0]
    TQ, TK, n = _att_blocks(S)
    T = TQ
    nq = S // TQ

    def body(q_ref, k_ref, v_ref, do_ref, o_ref, dq_ref, dk_hbm, dv_hbm, dk_acc, dv_acc):
        h, i = pl.program_id(0), pl.program_id(1)

        @pl.when(i == 0)
        def _():
            dk_acc[...] = jnp.zeros_like(dk_acc)
            dv_acc[...] = jnp.zeros_like(dv_acc)

        dob = do_ref[...].astype(BF16)
        tri = _triangle(TK)
        rest0 = jnp.sum(dob.astype(F32) * o_ref[...], axis=1, keepdims=True)

        def blk(j, state, r0, masked):
            carry, rest, dq = (c[r0:] for c in state)
            sl = pl.ds(pl.multiple_of(j * TK, TK), TK)
            kb, vb, qb = k_ref[sl, :], v_ref[sl, :], q_ref[r0:, :]
            a, lb, lm, strict = _sb_block(qb, kb, tri, carry, masked)
            ab = a.astype(BF16)
            e = ab.astype(F32) * _dot_nt(dob[r0:], vb)
            dz = e - jnp.exp(lb) * (rest - _split_dot(e, tri))
            if masked:
                dz = jnp.where(strict, dz, 0.0)
            dzb = dz.astype(BF16)
            dk_acc[sl, :] += _dot_tn(dzb, qb)
            dv_acc[sl, :] += _dot_tn(ab, dob[r0:])
            new = (carry + jnp.sum(lm, axis=1, keepdims=True), rest - jnp.sum(e, axis=1, keepdims=True),
                   dq + _dot(dzb, kb))
            return tuple(_put_rows(c, u, r0) for c, u in zip(state, new))

        state = (jnp.zeros((TQ, 1), F32), rest0, jnp.zeros((TQ, LANE), F32))
        for t in reversed(range(n)):
            state = blk(i * n + t, state, t * TK, True)
        state = lax.fori_loop(0, i * n, lambda t, c: blk(i * n - 1 - t, c, 0, False), state)
        dq_ref[...] = state[2] * SB_SCALE

        @pl.when(i == nq - 1)
        def _():
            pltpu.sync_copy(dk_acc, dk_hbm.at[h])
            pltpu.sync_copy(dv_acc, dv_hbm.at[h])

    any_spec = pl.BlockSpec(memory_space=pl.ANY)
    return pl.pallas_call(
        body, name="sb_bwd", grid=(SB_HEADS, nq),
        in_specs=[pl.BlockSpec((T, LANE), lambda h, i: (i, h)), pl.BlockSpec((S, LANE), lambda h, i: (0, 4 + h)),
                  pl.BlockSpec((S, LANE), lambda h, i: (0, 8 + h)),
                  pl.BlockSpec((T, LANE), lambda h, i: (i, col0 + h)), pl.BlockSpec((T, LANE), lambda h, i: (i, col0 + h))],
        out_specs=[pl.BlockSpec((T, LANE), lambda h, i: (i, h)), any_spec, any_spec],
        out_shape=[jax.ShapeDtypeStruct((S, SB_HEADS * LANE), F32), jax.ShapeDtypeStruct((SB_HEADS, S, LANE), F32),
                   jax.ShapeDtypeStruct((SB_HEADS, S, LANE), F32)],
        scratch_shapes=[pltpu.VMEM((S, LANE), F32), pltpu.VMEM((S, LANE), F32)],
        compiler_params=_params(("arbitrary", "arbitrary")))(qkv, qkv, qkv, do_cat, o_cat)


def _mem_probs(q, k_ref, hh):
    lane = lax.broadcasted_iota(jnp.int32, (1, 256), 1) // 64
    msk = lane == hh
    qh = jnp.where(msk, q, 0.0).astype(BF16)
    s = _dot_nt(qh, k_ref[...]) * MEM_SCALE
    p = jnp.exp(s - jnp.max(s, axis=1, keepdims=True))
    return msk, qh, p / jnp.sum(p, axis=1, keepdims=True)


def _mem_fwd(h, mk, mv):
    S = h.shape[0]
    tm = _tile(S, 512)

    def body(q_ref, k_ref, v_ref, o_ref):
        q = q_ref[...]
        out = jnp.zeros((tm, 256), F32)
        for hh in range(MEM_HEADS):
            msk, _, p = _mem_probs(q, k_ref, hh)
            out = out + jnp.where(msk, _dot(p.astype(BF16), v_ref[...]), 0.0)
        o_ref[...] = out

    return pl.pallas_call(
        body, name="mem_fwd", grid=(S // tm,),
        in_specs=[_row_spec(tm, 256, MQ // 256), _fix_spec((256, 256)), _fix_spec((256, 256))],
        out_specs=_row_spec(tm, 256), out_shape=jax.ShapeDtypeStruct((S, 256), F32),
        compiler_params=_params(("parallel",)))(h, mk, mv)


def _mem_bwd(h, mk, mv, do_cat, col0):
    S = h.shape[0]
    tm = _tile(S, 512)

    def body(q_ref, k_ref, v_ref, do_ref, dq_ref, dk_ref, dv_ref):
        @pl.when(pl.program_id(0) == 0)
        def _():
            dk_ref[...] = jnp.zeros_like(dk_ref)
            dv_ref[...] = jnp.zeros_like(dv_ref)

        q, do = q_ref[...], do_ref[...]
        dq = jnp.zeros((tm, 256), F32)
        for hh in range(MEM_HEADS):
            msk, qh, p = _mem_probs(q, k_ref, hh)
            doh = jnp.where(msk, do, 0.0).astype(BF16)
            dp = _dot_nt(doh, v_ref[...])
            ds = (p * (dp - jnp.sum(p * dp, axis=1, keepdims=True)) * MEM_SCALE).astype(BF16)
            dq = dq + jnp.where(msk, _dot(ds, k_ref[...]), 0.0)
            dk_ref[...] += _dot_tn(ds, qh)
            dv_ref[...] += _dot_tn(p.astype(BF16), doh)
        dq_ref[...] = dq

    return pl.pallas_call(
        body, name="mem_bwd", grid=(S // tm,),
        in_specs=[_row_spec(tm, 256, MQ // 256), _fix_spec((256, 256)), _fix_spec((256, 256)),
                  _row_spec(tm, 256, col0 // 256)],
        out_specs=[_row_spec(tm, 256), _fix_spec((256, 256)), _fix_spec((256, 256))],
        out_shape=[jax.ShapeDtypeStruct((S, 256), F32), jax.ShapeDtypeStruct((256, 256), F32),
                   jax.ShapeDtypeStruct((256, 256), F32)],
        compiler_params=_params(("arbitrary",)))(h, mk, mv, do_cat)


SG_T = 128


def _sg_norm(sv, g, b):
    gv = _gelu(sv)
    xc = gv - jnp.mean(gv, axis=1, keepdims=True)
    rstd = lax.rsqrt(jnp.mean(xc * xc, axis=1, keepdims=True) + LN_EPS)
    xhat = xc * rstd
    return xhat, rstd, xhat * g + b


def _sg_fwd(h, lng, lnb, w, bias_t):
    S = h.shape[0]
    tm = _tile(S, 512)

    def body(u_ref, v_ref, g_ref, b_ref, w_ref, bias_ref, o_ref):
        mask = _chunk_mask(SG_T)
        for n in range(tm // SG_T):
            rows = slice(n * SG_T, (n + 1) * SG_T)
            u = _gelu(u_ref[rows, :])
            _, _, vn = _sg_norm(v_ref[rows, :], g_ref[...], b_ref[...])
            vb = vn.astype(BF16)
            for gi in range(4):
                cols = slice(gi * LANE, (gi + 1) * LANE)
                wg = jnp.where(mask, w_ref[gi], 0.0).astype(BF16)
                mixed = _dot(wg, vb[:, cols]) + bias_ref[:, gi:gi + 1]
                o_ref[rows, cols] = u[:, cols] * mixed

    return pl.pallas_call(
        body, name="sg_fwd", grid=(S // tm,),
        in_specs=[_row_spec(tm, 512, SGU // 512), _row_spec(tm, 512, SGV // 512), _fix_spec((1, 512)),
                  _fix_spec((1, 512)), _fix_spec((4, SG_T, SG_T)), _fix_spec((SG_T, 4))],
        out_specs=_row_spec(tm, 512), out_shape=jax.ShapeDtypeStruct((S, 512), F32),
        compiler_params=_params(("parallel",)))(h, h, lng.reshape(1, 512), lnb.reshape(1, 512), w, bias_t)


def _sg_bwd(h, lng, lnb, w, bias_t, do_cat, col0):
    S = h.shape[0]
    tm = _tile(S, 512)
    nsteps = S // tm

    def body(u_ref, v_ref, g_ref, b_ref, w_ref, bias_ref, do0_ref, do1_ref, do2_ref, do3_ref,
             du_ref, dv_ref, dw_ref, dbias_ref, dg_ref, db_ref, dvn_scr, dbias_acc):
        do_refs = (do0_ref, do1_ref, do2_ref, do3_ref)
        step = pl.program_id(0)

        @pl.when(step == 0)
        def _():
            dw_ref[...] = jnp.zeros_like(dw_ref)
            dg_ref[...] = jnp.zeros_like(dg_ref)
            db_ref[...] = jnp.zeros_like(db_ref)
            dbias_acc[...] = jnp.zeros_like(dbias_acc)

        mask = _chunk_mask(SG_T)
        for n in range(tm // SG_T):
            rows = slice(n * SG_T, (n + 1) * SG_T)
            su, sv = u_ref[rows, :], v_ref[rows, :]
            u = _gelu(su)
            xhat, rstd, vn = _sg_norm(sv, g_ref[...], b_ref[...])
            vb = vn.astype(BF16)
            ugrad = _gelu_grad(su)
            for gi in range(4):
                cols = slice(gi * LANE, (gi + 1) * LANE)
                do = do_refs[gi][rows, :]
                wg = jnp.where(mask, w_ref[gi], 0.0).astype(BF16)
                mixed = _dot(wg, vb[:, cols]) + bias_ref[:, gi:gi + 1]
                dmixed = do * u[:, cols]
                dmb = dmixed.astype(BF16)
                du_ref[rows, cols] = do * mixed * ugrad[:, cols]
                dvn_scr[:, cols] = _dot_tn(wg, dmb)
                dw_ref[gi] += jnp.where(mask, _dot_nt(dmb, vb[:, cols]), 0.0)
                dbias_acc[gi] += dmixed
            dvn = dvn_scr[...]
            dg_ref[...] += jnp.sum(dvn * xhat, axis=0, keepdims=True)
            db_ref[...] += jnp.sum(dvn, axis=0, keepdims=True)
            dxh = dvn * g_ref[...]
            dgv = rstd * (dxh - jnp.mean(dxh, axis=1, keepdims=True)
                          - xhat * jnp.mean(dxh * xhat, axis=1, keepdims=True))
            dv_ref[rows, :] = dgv * _gelu_grad(sv)

        @pl.when(step == nsteps - 1)
        def _():
            for gi in range(4):
                dbias_ref[:, gi:gi + 1] = jnp.sum(dbias_acc[gi], axis=1, keepdims=True)

    return pl.pallas_call(
        body, name="sg_bwd", grid=(nsteps,),
        in_specs=[_row_spec(tm, 512, SGU // 512), _row_spec(tm, 512, SGV // 512), _fix_spec((1, 512)),
                  _fix_spec((1, 512)), _fix_spec((4, SG_T, SG_T)), _fix_spec((SG_T, 4))]
                 + [_row_spec(tm, LANE, col0 // LANE + gi) for gi in range(4)],
        out_specs=[_row_spec(tm, 512), _row_spec(tm, 512), _fix_spec((4, SG_T, SG_T)), _fix_spec((SG_T, 4)),
                   _fix_spec((1, 512)), _fix_spec((1, 512))],
        out_shape=[jax.ShapeDtypeStruct((S, 512), F32), jax.ShapeDtypeStruct((S, 512), F32),
                   jax.ShapeDtypeStruct((4, SG_T, SG_T), F32), jax.ShapeDtypeStruct((SG_T, 4), F32),
                   jax.ShapeDtypeStruct((1, 512), F32), jax.ShapeDtypeStruct((1, 512), F32)],
        scratch_shapes=[pltpu.VMEM((SG_T, 512), F32), pltpu.VMEM((4, SG_T, SG_T), F32)],
        compiler_params=_params(("arbitrary",)))(h, h, lng.reshape(1, 512), lnb.reshape(1, 512), w, bias_t,
                                                 do_cat, do_cat, do_cat, do_cat)


def _gate_fwd(o_cat, h):
    S = h.shape[0]
    tm = _tile(S, 512)

    def body(o_ref, g_ref, y_ref):
        g = g_ref[...]
        y_ref[...] = (o_ref[...] * (g * jax.nn.sigmoid(g))).astype(BF16)

    return pl.pallas_call(
        body, name="gate_fwd", grid=(S // tm, 4),
        in_specs=[pl.BlockSpec((tm, 512), lambda i, j: (i, j)), pl.BlockSpec((tm, 512), lambda i, j: (i, GATE // 512 + j))],
        out_specs=pl.BlockSpec((tm, 512), lambda i, j: (i, j)), out_shape=jax.ShapeDtypeStruct((S, D_MODEL), BF16),
        compiler_params=_params(("parallel", "parallel")))(o_cat, h)


def _gate_bwd(dyg, o_cat, h):
    S = h.shape[0]
    tm = _tile(S, 512)

    def body(d_ref, o_ref, g_ref, do_ref, dg_ref):
        d, g = d_ref[...], g_ref[...]
        sig = jax.nn.sigmoid(g)
        do_ref[...] = d * (g * sig)
        dg_ref[...] = d * o_ref[...] * (sig * (1.0 + g * (1.0 - sig)))

    blk = pl.BlockSpec((tm, 512), lambda i, j: (i, j))
    return pl.pallas_call(
        body, name="gate_bwd", grid=(S // tm, 4),
        in_specs=[blk, blk, pl.BlockSpec((tm, 512), lambda i, j: (i, GATE // 512 + j))],
        out_specs=[blk, blk],
        out_shape=[jax.ShapeDtypeStruct((S, D_MODEL), F32), jax.ShapeDtypeStruct((S, D_MODEL), F32)],
        compiler_params=_params(("parallel", "parallel")))(dyg, o_cat, h)


def _ln_res_fwd(x, y, g, b):
    S = x.shape[0]
    tm = _tile(S, 256)

    def body(x_ref, y_ref, g_ref, b_ref, o_ref, ob_ref, r_ref):
        r = ALPHA * x_ref[...] + y_ref[...]
        r_ref[...] = r
        xc = r - jnp.mean(r, axis=1, keepdims=True)
        o = xc * lax.rsqrt(jnp.mean(xc * xc, axis=1, keepdims=True) + LN_EPS) * g_ref[...] + b_ref[...]
        o_ref[...] = o
        ob_ref[...] = o.astype(BF16)

    return pl.pallas_call(
        body, name="ln_res_fwd", grid=(S // tm,),
        in_specs=[_row_spec(tm, D_MODEL), _row_spec(tm, D_MODEL), _fix_spec((1, D_MODEL)), _fix_spec((1, D_MODEL))],
        out_specs=[_row_spec(tm, D_MODEL), _row_spec(tm, D_MODEL), _row_spec(tm, D_MODEL)],
        out_shape=[jax.ShapeDtypeStruct((S, D_MODEL), F32), jax.ShapeDtypeStruct((S, D_MODEL), BF16),
                   jax.ShapeDtypeStruct((S, D_MODEL), F32)],
        compiler_params=_params(("parallel",)))(x, y, g.reshape(1, D_MODEL), b.reshape(1, D_MODEL))


def _ln_res_bwd(dout, r, g):
    S = r.shape[0]
    tm = _tile(S, 256)

    def body(d_ref, r_ref, g_ref, dr_ref, dg_ref, db_ref):
        @pl.when(pl.program_id(0) == 0)
        def _():
            dg_ref[...] = jnp.zeros_like(dg_ref)
            db_ref[...] = jnp.zeros_like(db_ref)

        d, r = d_ref[...], r_ref[...]
        xc = r - jnp.mean(r, axis=1, keepdims=True)
        rstd = lax.rsqrt(jnp.mean(xc * xc, axis=1, keepdims=True) + LN_EPS)
        xhat = xc * rstd
        dxh = d * g_ref[...]
        dr_ref[...] = rstd * (dxh - jnp.mean(dxh, axis=1, keepdims=True)
                              - xhat * jnp.mean(dxh * xhat, axis=1, keepdims=True))
        dg_ref[...] += jnp.sum(d * xhat, axis=0, keepdims=True)
        db_ref[...] += jnp.sum(d, axis=0, keepdims=True)

    return pl.pallas_call(
        body, name="ln_res_bwd", grid=(S // tm,),
        in_specs=[_row_spec(tm, D_MODEL), _row_spec(tm, D_MODEL), _fix_spec((1, D_MODEL))],
        out_specs=[_row_spec(tm, D_MODEL), _fix_spec((1, D_MODEL)), _fix_spec((1, D_MODEL))],
        out_shape=[jax.ShapeDtypeStruct((S, D_MODEL), F32), jax.ShapeDtypeStruct((1, D_MODEL), F32),
                   jax.ShapeDtypeStruct((1, D_MODEL), F32)],
        compiler_params=_params(("arbitrary",)))(dout, r, g.reshape(1, D_MODEL))


def _loss_head(y, target):
    S = y.shape[0]
    tm = _tile(S, 256)

    def body(y_ref, t_ref, l_ref, d_ref):
        @pl.when(pl.program_id(0) == 0)
        def _():
            l_ref[...] = jnp.zeros_like(l_ref)

        diff = y_ref[...] - t_ref[...]
        d_ref[...] = diff * (1.0 / D_MODEL)
        per_row = jnp.mean(diff * diff, axis=1, keepdims=True)
        l_ref[...] += 0.5 * jnp.sum(per_row, axis=0, keepdims=True)

    return pl.pallas_call(
        body, name="loss_head", grid=(S // tm,), in_specs=[_row_spec(tm, D_MODEL), _row_spec(tm, D_MODEL)],
        out_specs=[_fix_spec((8, LANE)), _row_spec(tm, D_MODEL)],
        out_shape=[jax.ShapeDtypeStruct((8, LANE), F32), jax.ShapeDtypeStruct((S, D_MODEL), F32)],
        compiler_params=_params(("arbitrary",)))(y, target)


def _perm_table():
    table, at = [], 0
    for name in PERM_ORDER:
        start, width = ORIG[name]
        table.append((name, start, width, at))
        at += width
    return table


def _permute_w_in(w):
    parts = [w[..., start:start + width] for _, start, width, _ in _perm_table()]
    return jnp.concatenate(parts + [jnp.zeros(w.shape[:-1] + (HP - D_IN,), w.dtype)], axis=-1)


def _unpermute_w_in(wp):
    parts = sorted(_perm_table(), key=lambda t: t[1])
    return jnp.concatenate([wp[..., at:at + width] for _, _, width, at in parts], axis=-1)


def _rope_tables(positions):
    inv_freq = ROPE_THETA ** (-jnp.arange(0, 64, 2, dtype=F32) / 64)
    ang = positions.astype(F32)[:, None] * inv_freq[None, :]
    cos, sin, zero = jnp.cos(ang), jnp.sin(ang), jnp.zeros((positions.shape[0], 64), F32)
    return jnp.concatenate([cos, cos, zero], axis=1), jnp.concatenate([-sin, sin, zero], axis=1)


def _local_step(x, mem, positions, target, w):
    rc, rs = _rope_tables(positions)
    mem_b = mem.astype(BF16)
    xb = x.astype(BF16)
    w_in_all = _permute_w_in(w["w_in"])
    w_uq_all = jnp.pad(w["w_uq"].reshape(DEPTH, 512, MLA_HEADS, 192),
                       ((0, 0), (0, 0), (0, 0), (0, 64))).reshape(DEPTH, 512, MLA_HEADS * 256)
    saved = []
    for l in range(DEPTH):
        w_in, w_uq, w_ukv = w_in_all[l], w_uq_all[l], w["w_ukv"][l]
        h = _mm(xb, w_in, tm=1024, tn=1152, tk=2048, name="in_proj")
        cq_n = _rms_fwd(h, CQ, 512, w["q_norm_g"][l], "rms_q")
        ckv_n = _rms_fwd(h, CKV, 256, w["kv_norm_g"][l], "rms_kv")
        q = _q_proj(cq_n, w_uq, rc, rs)
        kp, v = _kv_proj(ckv_n, w_ukv, h, rc, rs)
        o_a, lse = _mla_fwd(q, kp, v)
        bias_t = w["sg_b"][l].T
        o_b = _sg_fwd(h, w["sg_ln_g"][l], w["sg_ln_b"][l], w["sg_w"][l], bias_t)
        qkv = jnp.concatenate([h[:, SBQ:SBQ + 512] * SB_SCALE, h[:, SBQ + 512:SBQ + 1536]], axis=1).astype(BF16)
        o_c = _sb_fwd(qkv)
        mk = _mm(mem_b, w["w_mem_k"][l], out_dtype=BF16, name="mem_kv")
        mv = _mm(mem_b, w["w_mem_v"][l], out_dtype=BF16, name="mem_kv")
        o_m = _mem_fwd(h, mk, mv)
        o_cat = jnp.concatenate([o_a, o_b, o_c, o_m], axis=1)
        yg = _gate_fwd(o_cat, h)
        y = _mm(yg, w["w_out"][l], tm=1024, tn=1024, tk=2048, name="out_proj")
        x_new, xb_new, r = _ln_res_fwd(x, y, w["ln_g"][l], w["ln_b"][l])
        saved.append(dict(xb=xb, h=h, cq_n=cq_n, ckv_n=ckv_n, q=q, kp=kp, v=v, lse=lse, qkv=qkv, mk=mk, mv=mv,
                          o_cat=o_cat, yg=yg, r=r, w_in=w_in, w_uq=w_uq, w_ukv=w_ukv, bias_t=bias_t))
        x, xb = x_new, xb_new

    loss, dx = _loss_head(x, target)

    grads = {n: [None] * DEPTH for n in SHARDED + SMALL}
    for l in reversed(range(DEPTH)):
        s = saved[l]
        h = s["h"]
        dr, dlg, dlb = _ln_res_bwd(dx, s["r"], w["ln_g"][l])
        grads["ln_g"][l], grads["ln_b"][l] = dlg[0], dlb[0]
        grads["w_out"][l] = _mm(s["yg"], dr, ta=True, tm=1024, tn=1024, tk=1024, name="dw_out")
        dyg = _mm(dr, w["w_out"][l], tb=True, tm=1024, tn=1024, name="d_out_proj")
        do_cat, dgates = _gate_bwd(dyg, s["o_cat"], h)
        dmq, dmk, dmv = _mem_bwd(h, s["mk"], s["mv"], do_cat, 1792)
        grads["w_mem_k"][l] = _mm(mem_b, dmk, ta=True, name="dw_mem")
        grads["w_mem_v"][l] = _mm(mem_b, dmv, ta=True, name="dw_mem")
        dsq, dsk, dsv = _sb_bwd(s["qkv"], do_cat, s["o_cat"], 1280 // LANE)
        dsk = dsk.transpose(1, 0, 2).reshape(-1, 512)
        dsv = dsv.transpose(1, 0, 2).reshape(-1, 512)
        du, dv, dsgw, dsgb, dsg_g, dsg_b = _sg_bwd(h, w["sg_ln_g"][l], w["sg_ln_b"][l], w["sg_w"][l], s["bias_t"],
                                                   do_cat, 768)
        grads["sg_w"][l], grads["sg_b"][l] = dsgw, dsgb.T
        grads["sg_ln_g"][l], grads["sg_ln_b"][l] = dsg_g[0], dsg_b[0]
        dq, dk, dvv = _mla_bwd(s["q"], s["kp"], s["v"], do_cat, s["o_cat"], s["lse"])
        dq_raw = _q_rope_bwd(dq, rc, rs)
        dkv, dkpe = _kv_bwd_prep(dk, dvv, rc, rs)
        dw_uq = _mm(s["cq_n"], dq_raw, ta=True, tk=1024, name="dw_uq")
        grads["w_uq"][l] = dw_uq.reshape(512, MLA_HEADS, 256)[:, :, :192].reshape(512, MLA_HEADS * 192)
        grads["w_ukv"][l] = _mm(s["ckv_n"], dkv, ta=True, tk=1024, name="dw_ukv")
        dcq_n = _mm(dq_raw, s["w_uq"], tb=True, name="d_cq")
        dckv_n = _mm(dkv, s["w_ukv"], tb=True, name="d_ckv")
        dcq, dqg = _rms_bwd(h, CQ, 512, w["q_norm_g"][l], dcq_n, "rms_q_bwd")
        dckv, dkvg = _rms_bwd(h, CKV, 256, w["kv_norm_g"][l], dckv_n, "rms_kv_bwd")
        grads["q_norm_g"][l], grads["kv_norm_g"][l] = dqg[0], dkvg[0]
        dh = jnp.concatenate([dcq, dckv, dmq, du, dv, dsq, dsk, dsv, dgates, dkpe], axis=1).astype(BF16)
        dw_in = _mm(s["xb"], dh, ta=True, tm=1024, tn=1152, tk=1024, name="dw_in")
        grads["w_in"][l] = _unpermute_w_in(dw_in)
        dx = _mm(dh, s["w_in"], tb=True, add=dr, add_scale=ALPHA, tm=1024, tn=1024, tk=1152, name="d_in_proj")

    return loss, dx, grads


MESH = pl.DeviceIdType.MESH
HBM_SPEC = pl.BlockSpec(memory_space=pltpu.HBM)


def _place():
    x, y, c = lax.axis_index("x"), lax.axis_index("y"), lax.axis_index("c")
    return x, y, c, [(1 - x, y), (x, 1 - y), (1 - x, 1 - y)]


def _gather_weights(flat):
    R = flat.shape[0]
    H = R // 2

    def body(src, out, send_sems, recv_sems):
        x, y, c, chips = _place()
        mine = pl.ds(c * H, H)
        theirs = pl.ds((1 - c) * H, H)

        def copy(k, src_ref, chip, rows, to):
            return pltpu.make_async_remote_copy(src_ref=src_ref, dst_ref=out.at[chip, rows, :], send_sem=send_sems.at[k],
                                                recv_sem=recv_sems.at[k], device_id=to, device_id_type=MESH)

        sent = [copy(j, src.at[mine, :], 2 * x + y, mine, (px, py, c)) for j, (px, py) in enumerate(chips)]
        for cp in sent:
            cp.start()
        passed = []
        for j, (px, py) in enumerate(chips):
            copy(j, src.at[mine, :], 2 * px + py, mine, (px, py, c)).wait_recv()
            cp = copy(3 + j, out.at[2 * px + py, mine, :], 2 * px + py, mine, (x, y, 1 - c))
            cp.start()
            passed.append(cp)
        for j, (px, py) in enumerate(chips):
            copy(3 + j, src.at[theirs, :], 2 * px + py, theirs, (x, y, 1 - c)).wait_recv()
        for cp in sent + passed:
            cp.wait_send()

    return pl.pallas_call(
        body, name="gather_weights", in_specs=[HBM_SPEC], out_specs=HBM_SPEC,
        out_shape=jax.ShapeDtypeStruct((4, R, FLAT_W), flat.dtype),
        scratch_shapes=[pltpu.SemaphoreType.DMA((6,)), pltpu.SemaphoreType.DMA((6,))],
        compiler_params=pltpu.CompilerParams(has_side_effects=True))(flat)


def _swap_halves(g):
    _, _, H, W = g.shape

    def body(src, out, send_sem, recv_sem):
        x, y, c, _ = _place()
        cp = pltpu.make_async_remote_copy(src_ref=src.at[:, 1 - c], dst_ref=out, send_sem=send_sem, recv_sem=recv_sem,
                                          device_id=(x, y, 1 - c), device_id_type=MESH)
        cp.start()
        cp.wait()

    return pl.pallas_call(
        body, name="swap_halves", in_specs=[HBM_SPEC], out_specs=HBM_SPEC,
        out_shape=jax.ShapeDtypeStruct((4, H, W), g.dtype),
        scratch_shapes=[pltpu.SemaphoreType.DMA(()), pltpu.SemaphoreType.DMA(())],
        compiler_params=pltpu.CompilerParams(has_side_effects=True))(g)


def _pair_sum(g, other, c):
    _, _, H, W = g.shape
    th = _row_tile(H, 3 * W * 4)

    def body(c_ref, a_ref, b_ref, o_ref):
        o_ref[...] = (a_ref[...] + b_ref[...]).astype(BF16)

    return pl.pallas_call(
        body, name="pair_sum",
        grid_spec=pltpu.PrefetchScalarGridSpec(
            num_scalar_prefetch=1, grid=(4, H // th),
            in_specs=[pl.BlockSpec((None, None, th, W), lambda d, i, c_ref: (d, c_ref[0], i, 0)),
                      pl.BlockSpec((None, th, W), lambda d, i, c_ref: (d, i, 0))],
            out_specs=pl.BlockSpec((None, th, W), lambda d, i, c_ref: (d, i, 0))),
        out_shape=jax.ShapeDtypeStruct((4, H, W), BF16),
        compiler_params=_params(("parallel", "parallel")))(c, g, other)


def _exchange_chips(p):
    _, H, W = p.shape

    def body(src, out, send_sems, recv_sems):
        x, y, c, chips = _place()
        sent = []
        for j, (px, py) in enumerate(chips):
            cp = pltpu.make_async_remote_copy(src_ref=src.at[2 * px + py], dst_ref=out.at[j], send_sem=send_sems.at[j],
                                              recv_sem=recv_sems.at[j], device_id=(px, py, c), device_id_type=MESH)
            cp.start()
            sent.append(cp)
        for cp in sent:
            cp.wait()

    return pl.pallas_call(
        body, name="exchange_chips", in_specs=[HBM_SPEC], out_specs=HBM_SPEC,
        out_shape=jax.ShapeDtypeStruct((3, H, W), p.dtype),
        scratch_shapes=[pltpu.SemaphoreType.DMA((3,)), pltpu.SemaphoreType.DMA((3,))],
        compiler_params=pltpu.CompilerParams(has_side_effects=True))(p)


def _chip_sum(p, got, me):
    _, H, W = p.shape
    th = _row_tile(H, 4 * W * 4)

    def body(me_ref, p_ref, g_ref, o_ref):
        acc = p_ref[...].astype(F32)
        for k in range(3):
            acc = acc + g_ref[k].astype(F32)
        o_ref[...] = acc

    return pl.pallas_call(
        body, name="chip_sum",
        grid_spec=pltpu.PrefetchScalarGridSpec(
            num_scalar_prefetch=1, grid=(H // th,),
            in_specs=[pl.BlockSpec((None, th, W), lambda i, me_ref: (me_ref[0], i, 0)),
                      pl.BlockSpec((3, th, W), lambda i, me_ref: (0, i, 0))],
            out_specs=pl.BlockSpec((th, W), lambda i, me_ref: (i, 0))),
        out_shape=jax.ShapeDtypeStruct((H, W), F32), compiler_params=_params(("parallel",)))(me, p, got)


def _sum_parts(t, name):
    n, H, W = t.shape
    th = _row_tile(H, (n + 1) * W * 4)

    def body(t_ref, o_ref):
        acc = t_ref[0]
        for k in range(1, n):
            acc = acc + t_ref[k]
        o_ref[...] = acc

    return pl.pallas_call(
        body, name=name, grid=(H // th,), in_specs=[pl.BlockSpec((n, th, W), lambda i: (0, i, 0))],
        out_specs=pl.BlockSpec((th, W), lambda i: (i, 0)), out_shape=jax.ShapeDtypeStruct((H, W), F32),
        compiler_params=_params(("parallel",)))(t)


def _share_with_sibling(half):
    H, W = half.shape

    def body(src, out, send_sem, recv_sem):
        x, y, c, _ = _place()
        cp = pltpu.make_async_remote_copy(src_ref=src, dst_ref=out.at[c], send_sem=send_sem, recv_sem=recv_sem,
                                          device_id=(x, y, 1 - c), device_id_type=MESH)
        cp.start()
        pltpu.make_async_remote_copy(src_ref=src, dst_ref=out.at[1 - c], send_sem=send_sem, recv_sem=recv_sem,
                                     device_id=(x, y, 1 - c), device_id_type=MESH).wait_recv()
        cp.wait_send()

    return pl.pallas_call(
        body, name="share_with_sibling", in_specs=[HBM_SPEC], out_specs=HBM_SPEC,
        out_shape=jax.ShapeDtypeStruct((2, H, W), half.dtype),
        scratch_shapes=[pltpu.SemaphoreType.DMA(()), pltpu.SemaphoreType.DMA(())],
        compiler_params=pltpu.CompilerParams(has_side_effects=True))(half)


def _gather_all(v):
    n, W = v.shape

    def body(src, out, send_sems, recv_sems, own_sem):
        x, y, c, _ = _place()
        own = pltpu.make_async_copy(src, out.at[4 * x + 2 * y + c], own_sem)
        own.start()
        flips = [(fx, fy, fc) for fx in (0, 1) for fy in (0, 1) for fc in (0, 1)][1:]
        sent = []
        for k, (fx, fy, fc) in enumerate(flips):
            cp = pltpu.make_async_remote_copy(
                src_ref=src, dst_ref=out.at[4 * x + 2 * y + c], send_sem=send_sems.at[k], recv_sem=recv_sems.at[k],
                device_id=(x ^ fx, y ^ fy, c ^ fc), device_id_type=MESH)
            cp.start()
            sent.append(cp)
        for k, (fx, fy, fc) in enumerate(flips):
            pltpu.make_async_remote_copy(
                src_ref=src, dst_ref=out.at[4 * (x ^ fx) + 2 * (y ^ fy) + (c ^ fc)], send_sem=send_sems.at[k],
                recv_sem=recv_sems.at[k], device_id=(x ^ fx, y ^ fy, c ^ fc), device_id_type=MESH).wait_recv()
        for cp in sent:
            cp.wait_send()
        own.wait()

    return pl.pallas_call(
        body, name="gather_all", in_specs=[HBM_SPEC], out_specs=HBM_SPEC,
        out_shape=jax.ShapeDtypeStruct((8, n, W), v.dtype),
        scratch_shapes=[pltpu.SemaphoreType.DMA((7,)), pltpu.SemaphoreType.DMA((7,)), pltpu.SemaphoreType.DMA(())],
        compiler_params=pltpu.CompilerParams(has_side_effects=True))(v)


def _adamw(w, g, m, v):
    shape = w.shape
    cols = shape[-1]
    w2, g2, m2, v2 = (a.reshape(-1, cols) for a in (w, g, m, v))
    rows = w2.shape[0]
    tr = next((t for t in (1024, 512, 256, 128, 64, 32, 16, 8) if rows % t == 0 and t * cols * 4 <= (2 << 20)), rows)

    def body(w_ref, g_ref, m_ref, v_ref, d_ref, nm_ref, nv_ref):
        g_ = g_ref[...]
        nm = ADAM_B1 * m_ref[...] + (1.0 - ADAM_B1) * g_
        nv = ADAM_B2 * v_ref[...] + (1.0 - ADAM_B2) * (g_ * g_)
        m_hat = nm / (1.0 - ADAM_B1 ** ADAM_STEP)
        v_hat = nv / (1.0 - ADAM_B2 ** ADAM_STEP)
        d_ref[...] = -ADAM_LR * (m_hat / (jnp.sqrt(v_hat) + ADAM_EPS) + ADAM_WD * w_ref[...])
        nm_ref[...] = nm
        nv_ref[...] = nv

    blk = pl.BlockSpec((tr, cols), lambda i: (i, 0))
    outs = pl.pallas_call(
        body, name="adamw", grid=(rows // tr,), in_specs=[blk] * 4, out_specs=[blk] * 3,
        out_shape=[jax.ShapeDtypeStruct((rows, cols), F32)] * 3, compiler_params=_params(("parallel",)))(w2, g2, m2, v2)
    return tuple(o.reshape(shape) for o in outs)


def _chip_part(name, a, k):
    n = a.shape[1 if name in ("w_in", "w_uq", "w_ukv") else 0] // 4
    return a[:, k * n:(k + 1) * n] if name in ("w_in", "w_uq", "w_ukv") else a[k * n:(k + 1) * n]


def _join_chips(name, parts):
    return jnp.concatenate(parts, axis=2 if name in ("w_in", "w_uq", "w_ukv") else 1)


def kernel(x, mem, positions, w_in, q_norm_g, w_uq, kv_norm_g, w_ukv, sg_ln_g, sg_ln_b, sg_w, sg_b, w_mem_k, w_mem_v, w_out, ln_g, ln_b, loss_target, m_w_in, m_q_norm_g, m_w_uq, m_kv_norm_g, m_w_ukv, m_sg_ln_g, m_sg_ln_b, m_sg_w, m_sg_b, m_w_mem_k, m_w_mem_v, m_w_out, m_ln_g, m_ln_b, v_w_in, v_q_norm_g, v_w_uq, v_kv_norm_g, v_w_ukv, v_sg_ln_g, v_sg_ln_b, v_sg_w, v_sg_b, v_w_mem_k, v_w_mem_v, v_w_out, v_ln_g, v_ln_b):
    weights = dict(w_in=w_in, q_norm_g=q_norm_g, w_uq=w_uq, kv_norm_g=kv_norm_g, w_ukv=w_ukv, sg_ln_g=sg_ln_g,
                   sg_ln_b=sg_ln_b, sg_w=sg_w, sg_b=sg_b, w_mem_k=w_mem_k, w_mem_v=w_mem_v, w_out=w_out, ln_g=ln_g, ln_b=ln_b)
    mom_m = dict(w_in=m_w_in, q_norm_g=m_q_norm_g, w_uq=m_w_uq, kv_norm_g=m_kv_norm_g, w_ukv=m_w_ukv, sg_ln_g=m_sg_ln_g,
                 sg_ln_b=m_sg_ln_b, sg_w=m_sg_w, sg_b=m_sg_b, w_mem_k=m_w_mem_k, w_mem_v=m_w_mem_v, w_out=m_w_out,
                 ln_g=m_ln_g, ln_b=m_ln_b)
    mom_v = dict(w_in=v_w_in, q_norm_g=v_q_norm_g, w_uq=v_w_uq, kv_norm_g=v_kv_norm_g, w_ukv=v_w_ukv, sg_ln_g=v_sg_ln_g,
                 sg_ln_b=v_sg_ln_b, sg_w=v_sg_w, sg_b=v_sg_b, w_mem_k=v_w_mem_k, w_mem_v=v_w_mem_v, w_out=v_w_out,
                 ln_g=v_ln_g, ln_b=v_ln_b)
    c_idx = lax.axis_index("c").astype(jnp.int32).reshape(1)

    sizes = [weights[n].size for n in SHARDED]
    flat = jnp.concatenate([weights[n].astype(BF16).reshape(-1) for n in SHARDED]).reshape(-1, FLAT_W)
    me = 2 * lax.axis_index("x") + lax.axis_index("y")
    gathered = lax.dynamic_update_slice(_gather_weights(flat), flat[None], (me, 0, 0))
    full = dict((n, weights[n]) for n in SMALL)
    at = 0
    for n, size in zip(SHARDED, sizes):
        rows = size // FLAT_W
        full[n] = _join_chips(n, [gathered[k, at:at + rows].reshape(weights[n].shape) for k in range(4)])
        at += rows
    R = at

    loss_dev, grad_x, grads = _local_step(x[0], mem[0], positions[0], loss_target[0], full)

    per_chip = [jnp.concatenate([_chip_part(n, g, k).reshape(-1) for n in SHARDED for g in grads[n]]) for k in range(4)]
    g4 = jnp.stack(per_chip).reshape(4, 2, R // 2, FLAT_W)
    pair = _pair_sum(g4, _swap_halves(g4), c_idx)
    half = _chip_sum(pair, _exchange_chips(pair), me.astype(jnp.int32).reshape(1))
    reduced = lax.dynamic_update_slice(_share_with_sibling(half), half[None], (c_idx[0], 0, 0)).reshape(R, FLAT_W)
    grad_out = {}
    at = 0
    for n, size in zip(SHARDED, sizes):
        rows = size // FLAT_W
        grad_out[n] = reduced[at:at + rows].reshape(weights[n].shape)
        at += rows

    small_sizes = [weights[n].size for n in SMALL]
    vec = jnp.concatenate([g.reshape(-1) for n in SMALL for g in grads[n]] + [loss_dev[0]])
    n_small = vec.shape[0]
    rows_small = -(-n_small // (8 * FLAT_W)) * 8
    vec = jnp.pad(vec, (0, rows_small * FLAT_W - n_small)).reshape(rows_small, FLAT_W)
    total = _sum_parts(_gather_all(vec), "device_sum").reshape(-1)
    at = 0
    for n, size in zip(SMALL, small_sizes):
        grad_out[n] = total[at:at + size].reshape(weights[n].shape)
        at += size
    loss = total[at]

    names = list(weights)
    upd = {n: _adamw(weights[n], grad_out[n], mom_m[n], mom_v[n]) for n in names}
    return (loss, grad_x[None], *[grad_out[n] for n in names], *[upd[n][0] for n in names],
            *[upd[n][1] for n in names], *[upd[n][2] for n in names])
```

```python
import math

import jax
import jax.numpy as jnp
from jax import lax
from jax.experimental import pallas as pl
from jax.experimental.pallas import tpu as pltpu

F32, BF16 = jnp.float32, jnp.bfloat16

D_MODEL = 2048
DEPTH = 4
CHUNK = 64
MLA_HEADS = 6
MLA_SCALE = 1.0 / math.sqrt(192.0)
SB_HEADS = 4
SB_SCALE = 1.0 / math.sqrt(128.0)
MEM_HEADS = 4
MEM_SCALE = 1.0 / math.sqrt(64.0)
ROPE_THETA = 10000.0
ALPHA = (2.0 * DEPTH) ** 0.25
LN_EPS = 1e-5
RMS_EPS = 1e-6
ADAM_LR, ADAM_B1, ADAM_B2, ADAM_EPS, ADAM_WD, ADAM_STEP = 0.001, 0.9, 0.999, 1e-08, 0.01, 10

ORIG = dict(c_q=(0, 512), c_kv=(512, 256), k_pe=(768, 64), g_a=(832, 768), sg_u=(1600, 512), sg_v=(2112, 512),
            g_b=(2624, 512), sb_q=(3136, 512), sb_k=(3648, 512), sb_v=(4160, 512), g_c=(4672, 512),
            m_q=(5184, 256), g_m=(5440, 256))
D_IN = 5696
PERM_ORDER = ("c_q", "c_kv", "m_q", "sg_u", "sg_v", "sb_q", "sb_k", "sb_v", "g_a", "g_b", "g_c", "g_m", "k_pe")
HP = 5760
CQ, CKV, MQ, SGU, SGV, SBQ, GATE, KPE = 0, 512, 768, 1024, 1536, 2048, 3584, 5632

Q_BLK = 1024
K_BLK = 256
SB_Q_BLK = 512
SB_DEAD = -110.0
LANE = 128
VMEM_LIMIT = 56 * 1024 * 1024

FLAT_W = 1024
SHARDED = ("w_in", "w_uq", "w_ukv", "w_mem_k", "w_mem_v", "w_out")
SMALL = ("q_norm_g", "kv_norm_g", "sg_ln_g", "sg_ln_b", "sg_w", "sg_b", "ln_g", "ln_b")


def _params(sem=None):
    return pltpu.CompilerParams(dimension_semantics=sem, vmem_limit_bytes=VMEM_LIMIT)


def _tile(dim, pref):
    if dim <= pref:
        return dim
    t = (pref // LANE) * LANE
    while t >= LANE:
        if dim % t == 0:
            return t
        t -= LANE
    return dim


def _row_tile(rows, bytes_per_row, budget=8 << 20):
    best = None
    for t in range(8, rows + 1, 8):
        if rows % t == 0 and t * bytes_per_row <= budget:
            best = t
    return best if best else rows


def _dot_nt(a, b):
    return lax.dot_general(a, b, (((1,), (1,)), ((), ())), preferred_element_type=F32)


def _dot_tn(a, b):
    return lax.dot_general(a, b, (((0,), (0,)), ((), ())), preferred_element_type=F32)


def _dot(a, b):
    return jnp.dot(a, b, preferred_element_type=F32)


def _mm(a, b, *, ta=False, tb=False, a_win=None, b_win=None, add=None, add_scale=1.0, out_dtype=F32,
        tm=512, tn=512, tk=512, name="mm"):
    a_off, a_w = a_win if a_win else (0, a.shape[1])
    b_off, b_w = b_win if b_win else (0, b.shape[1])
    (K, M) = (a.shape[0], a_w) if ta else (a_w, a.shape[0])
    (N, Kb) = (b.shape[0], b_w) if tb else (b_w, b.shape[0])
    assert K == Kb, (a.shape, b.shape, ta, tb)
    tm, tn, tk = _tile(M, tm), _tile(N, tn), _tile(K, tk)
    nk = K // tk
    if ta:
        assert a_off % tm == 0
        a_spec = pl.BlockSpec((tk, tm), lambda i, j, k: (k, i + a_off // tm))
    else:
        assert a_off % tk == 0
        a_spec = pl.BlockSpec((tm, tk), lambda i, j, k: (i, k + a_off // tk))
    if tb:
        assert b_off % tk == 0
        b_spec = pl.BlockSpec((tn, tk), lambda i, j, k: (j, k + b_off // tk))
    else:
        assert b_off % tn == 0
        b_spec = pl.BlockSpec((tk, tn), lambda i, j, k: (k, j + b_off // tn))
    o_spec = pl.BlockSpec((tm, tn), lambda i, j, k: (i, j))
    dn = (((0 if ta else 1,), (1 if tb else 0,)), ((), ()))
    has_add = add is not None

    def body(*refs):
        a_ref, b_ref = refs[:2]
        add_ref = refs[2] if has_add else None
        o_ref = refs[3 if has_add else 2]
        part = lax.dot_general(a_ref[...].astype(BF16), b_ref[...].astype(BF16), dn, preferred_element_type=F32)

        def finish(r):
            if has_add:
                r = r + add_scale * add_ref[...]
            o_ref[...] = r.astype(o_ref.dtype)

        if nk == 1:
            finish(part)
            return
        acc_ref = refs[-1]
        k = pl.program_id(2)

        @pl.when(k == 0)
        def _():
            acc_ref[...] = part

        @pl.when(k > 0)
        def _():
            acc_ref[...] += part

        @pl.when(k == nk - 1)
        def _():
            finish(acc_ref[...])

    ins = [a, b] + ([add] if has_add else [])
    specs = [a_spec, b_spec] + ([o_spec] if has_add else [])
    return pl.pallas_call(
        body, name=name, grid=(M // tm, N // tn, nk), in_specs=specs, out_specs=o_spec,
        out_shape=jax.ShapeDtypeStruct((M, N), out_dtype),
        scratch_shapes=[pltpu.VMEM((tm, tn), F32)] if nk > 1 else [],
        compiler_params=_params(("parallel", "parallel", "arbitrary")))(*ins)


GELU_K = math.sqrt(2.0 / math.pi)


def _gelu(x):
    t = jnp.tanh(GELU_K * (x + 0.044715 * (x * x * x)))
    return 0.5 * x * (1.0 + t)


def _gelu_grad(x):
    t = jnp.tanh(GELU_K * (x + 0.044715 * (x * x * x)))
    return 0.5 * (1.0 + t) + 0.5 * x * (1.0 - t * t) * GELU_K * (1.0 + 3.0 * 0.044715 * x * x)


def _rope_swap(t):
    lane = lax.broadcasted_iota(jnp.int32, t.shape, 1)
    return jnp.where(lane < 32, pltpu.roll(t, 96, axis=1), pltpu.roll(t, 32, axis=1))


def _rope(t, c, s):
    return t * c + _rope_swap(t) * s


def _rope_bwd(dt, c, s):
    return dt * c - _rope_swap(dt) * s


def _row_spec(tm, w, cb=0):
    return pl.BlockSpec((tm, w), lambda i: (i, cb))


def _fix_spec(shape):
    return pl.BlockSpec(shape, lambda *_: (0,) * len(shape))


def _rms_fwd(h, off, width, g, name):
    S = h.shape[0]
    tm = _tile(S, 512)

    def body(x_ref, g_ref, o_ref):
        x = x_ref[...]
        r = lax.rsqrt(jnp.mean(x * x, axis=1, keepdims=True) + RMS_EPS)
        o_ref[...] = (x * r * g_ref[...]).astype(BF16)

    return pl.pallas_call(
        body, name=name, grid=(S // tm,), in_specs=[_row_spec(tm, width, off // width), _fix_spec((1, width))],
        out_specs=_row_spec(tm, width), out_shape=jax.ShapeDtypeStruct((S, width), BF16),
        compiler_params=_params(("parallel",)))(h, g.reshape(1, width))


def _rms_bwd(h, off, width, g, dxn, name):
    S = h.shape[0]
    tm = _tile(S, 512)

    def body(x_ref, g_ref, d_ref, dx_ref, dg_ref):
        @pl.when(pl.program_id(0) == 0)
        def _():
            dg_ref[...] = jnp.zeros_like(dg_ref)

        x, d = x_ref[...], d_ref[...]
        r = lax.rsqrt(jnp.mean(x * x, axis=1, keepdims=True) + RMS_EPS)
        gd = d * g_ref[...]
        dx_ref[...] = gd * r - x * (r * r * r) * jnp.mean(gd * x, axis=1, keepdims=True)
        dg_ref[...] += jnp.sum(d * x * r, axis=0, keepdims=True)

    return pl.pallas_call(
        body, name=name, grid=(S // tm,),
        in_specs=[_row_spec(tm, width, off // width), _fix_spec((1, width)), _row_spec(tm, width)],
        out_specs=[_row_spec(tm, width), _fix_spec((1, width))],
        out_shape=[jax.ShapeDtypeStruct((S, width), F32), jax.ShapeDtypeStruct((1, width), F32)],
        compiler_params=_params(("arbitrary",)))(h, g.reshape(1, width), dxn)


def _q_proj(xn, w, rc, rs):
    S = xn.shape[0]
    tm = _tile(S, 512)

    def body(x_ref, w_ref, c_ref, s_ref, q_ref):
        q = _dot(x_ref[...], w_ref[...]) * MLA_SCALE
        q_ref[:, :LANE] = q[:, :LANE].astype(BF16)
        q_ref[:, LANE:] = _rope(q[:, LANE:], c_ref[...], s_ref[...]).astype(BF16)

    return pl.pallas_call(
        body, name="q_proj", grid=(S // tm, MLA_HEADS),
        in_specs=[pl.BlockSpec((tm, 512), lambda i, j: (i, 0)), pl.BlockSpec((512, 256), lambda i, j: (0, j)),
                  pl.BlockSpec((tm, LANE), lambda i, j: (i, 0)), pl.BlockSpec((tm, LANE), lambda i, j: (i, 0))],
        out_specs=pl.BlockSpec((tm, 256), lambda i, j: (i, j)),
        out_shape=jax.ShapeDtypeStruct((S, MLA_HEADS * 256), BF16),
        compiler_params=_params(("parallel", "parallel")))(xn, w, rc, rs)


def _kv_proj(xn, w, h, rc, rs):
    S = xn.shape[0]
    tm = _tile(S, 512)

    def body(x_ref, w_ref, pe_ref, c_ref, s_ref, k_ref, v_ref):
        kv = _dot(x_ref[...], w_ref[...])
        k_ref[:, :LANE] = kv[:, :LANE].astype(BF16)
        k_ref[:, LANE:] = _rope(pe_ref[...], c_ref[...], s_ref[...]).astype(BF16)
        v_ref[...] = kv[:, LANE:].astype(BF16)

    return pl.pallas_call(
        body, name="kv_proj", grid=(S // tm, MLA_HEADS),
        in_specs=[pl.BlockSpec((tm, 256), lambda i, j: (i, 0)), pl.BlockSpec((256, 256), lambda i, j: (0, j)),
                  pl.BlockSpec((tm, LANE), lambda i, j: (i, KPE // LANE)),
                  pl.BlockSpec((tm, LANE), lambda i, j: (i, 0)), pl.BlockSpec((tm, LANE), lambda i, j: (i, 0))],
        out_specs=[pl.BlockSpec((tm, 256), lambda i, j: (i, j)), pl.BlockSpec((tm, LANE), lambda i, j: (i, j))],
        out_shape=[jax.ShapeDtypeStruct((S, MLA_HEADS * 256), BF16), jax.ShapeDtypeStruct((S, MLA_HEADS * LANE), BF16)],
        compiler_params=_params(("parallel", "parallel")))(xn, w, h, rc, rs)


def _q_rope_bwd(dq, rc, rs):
    S = dq.shape[0]
    tm = _tile(S, 512)

    def body(d_ref, c_ref, s_ref, o_ref):
        o_ref[:, :LANE] = d_ref[:, :LANE].astype(BF16)
        o_ref[:, LANE:] = _rope_bwd(d_ref[:, LANE:], c_ref[...], s_ref[...]).astype(BF16)

    return pl.pallas_call(
        body, name="q_rope_bwd", grid=(S // tm, MLA_HEADS),
        in_specs=[pl.BlockSpec((tm, 256), lambda i, j: (i, j)),
                  pl.BlockSpec((tm, LANE), lambda i, j: (i, 0)), pl.BlockSpec((tm, LANE), lambda i, j: (i, 0))],
        out_specs=pl.BlockSpec((tm, 256), lambda i, j: (i, j)),
        out_shape=jax.ShapeDtypeStruct((S, MLA_HEADS * 256), BF16),
        compiler_params=_params(("parallel", "parallel")))(dq, rc, rs)


def _kv_bwd_prep(dk, dv, rc, rs):
    S = dk.shape[1]
    tm = _tile(S, 512)

    def body(dk_ref, dv_ref, c_ref, s_ref, o_ref, pe_ref):
        rot = jnp.zeros((tm, LANE), F32)
        for hh in range(MLA_HEADS):
            o_ref[:, hh * 256:hh * 256 + LANE] = dk_ref[hh, :, :LANE].astype(BF16)
            o_ref[:, hh * 256 + LANE:(hh + 1) * 256] = dv_ref[hh].astype(BF16)
            rot = rot + dk_ref[hh, :, LANE:]
        pe_ref[...] = _rope_bwd(rot, c_ref[...], s_ref[...])

    return pl.pallas_call(
        body, name="kv_bwd_prep", grid=(S // tm,),
        in_specs=[pl.BlockSpec((MLA_HEADS, tm, 256), lambda i: (0, i, 0)),
                  pl.BlockSpec((MLA_HEADS, tm, LANE), lambda i: (0, i, 0)), _row_spec(tm, LANE), _row_spec(tm, LANE)],
        out_specs=[_row_spec(tm, MLA_HEADS * 256), _row_spec(tm, LANE)],
        out_shape=[jax.ShapeDtypeStruct((S, MLA_HEADS * 256), BF16), jax.ShapeDtypeStruct((S, LANE), F32)],
        compiler_params=_params(("parallel",)))(dk, dv, rc, rs)


def _chunk_mask(T):
    row = lax.broadcasted_iota(jnp.int32, (T, T), 0)
    col = lax.broadcasted_iota(jnp.int32, (T, T), 1)
    return (col // CHUNK) <= (row // CHUNK)


def _att_blocks(S, q_blk=None):
    tq = min(q_blk or Q_BLK, S)
    tk = min(K_BLK, tq)
    return tq, tk, tq // tk


def _tail_masks(rows, tk):
    row = lax.broadcasted_iota(jnp.int32, (rows, tk), 0)
    col = lax.broadcasted_iota(jnp.int32, (rows, tk), 1)
    return (col // CHUNK) <= (row // CHUNK), col < row


def _put_rows(old, new, r0):
    return new if r0 == 0 else jnp.concatenate([old[:r0], new], axis=0)


def _mla_fwd(q, kp, v):
    S = q.shape[0]
    TQ, TK, n = _att_blocks(S)

    def body(q_ref, k_ref, v_ref, o_ref, lse_ref):
        i = pl.program_id(1)

        def update(j, carry, r0, masked):
            m, l, acc = (c[r0:] for c in carry)
            sl = pl.ds(pl.multiple_of(j * TK, TK), TK)
            s = _dot_nt(q_ref[r0:, :], k_ref[sl, :])
            if masked:
                s = jnp.where(_tail_masks(TQ - r0, TK)[0], s, -1e30)
            m_new = jnp.maximum(m, jnp.max(s, axis=1, keepdims=True))
            a = jnp.exp(m - m_new)
            p = jnp.exp(s - m_new)
            new = (m_new, a * l + jnp.sum(p, axis=1, keepdims=True), a * acc + _dot(p.astype(BF16), v_ref[sl, :]))
            return tuple(_put_rows(c, u, r0) for c, u in zip(carry, new))

        carry = (jnp.full((TQ, 1), -1e30, F32), jnp.zeros((TQ, 1), F32), jnp.zeros((TQ, LANE), F32))
        carry = lax.fori_loop(0, i * n, lambda j, c: update(j, c, 0, False), carry)
        for t in range(n):
            carry = update(i * n + t, carry, t * TK, True)
        m, l, acc = carry
        o_ref[...] = acc / l
        lse_ref[...] = jnp.broadcast_to(m + jnp.log(l), (TQ, LANE))

    return pl.pallas_call(
        body, name="mla_fwd", grid=(MLA_HEADS, S // TQ),
        in_specs=[pl.BlockSpec((TQ, 256), lambda h, i: (i, h)), pl.BlockSpec((S, 256), lambda h, i: (0, h)),
                  pl.BlockSpec((S, LANE), lambda h, i: (0, h))],
        out_specs=[pl.BlockSpec((TQ, LANE), lambda h, i: (i, h)), pl.BlockSpec((TQ, LANE), lambda h, i: (i, h))],
        out_shape=[jax.ShapeDtypeStruct((S, MLA_HEADS * LANE), F32), jax.ShapeDtypeStruct((S, MLA_HEADS * LANE), F32)],
        compiler_params=_params(("parallel", "arbitrary")))(q, kp, v)


def _mla_bwd(q, kp, v, do_cat, o_cat, lse):
    S = q.shape[0]
    TQ, TK, n = _att_blocks(S)
    nq = S // TQ

    def body(q_ref, k_ref, v_ref, do_ref, o_ref, lse_ref, dq_ref, dk_hbm, dv_hbm, dk_acc, dv_acc):
        h, i = pl.program_id(0), pl.program_id(1)

        @pl.when(i == 0)
        def _():
            dk_acc[...] = jnp.zeros_like(dk_acc)
            dv_acc[...] = jnp.zeros_like(dv_acc)

        do32 = do_ref[...]
        dob = do32.astype(BF16)
        delta = jnp.sum(do32 * o_ref[...], axis=1, keepdims=True)
        lse_col = lse_ref[:, :1]

        def blk(j, dq, r0, masked):
            sl = pl.ds(pl.multiple_of(j * TK, TK), TK)
            kb, vb, qb = k_ref[sl, :], v_ref[sl, :], q_ref[r0:, :]
            s = _dot_nt(qb, kb)
            if masked:
                s = jnp.where(_tail_masks(TQ - r0, TK)[0], s, -1e30)
            p = jnp.exp(s - lse_col[r0:])
            ds = (p * (_dot_nt(dob[r0:], vb) - delta[r0:])).astype(BF16)
            dk_acc[sl, :] += _dot_tn(ds, qb)
            dv_acc[sl, :] += _dot_tn(p.astype(BF16), dob[r0:])
            return _put_rows(dq, dq[r0:] + _dot(ds, kb), r0)

        dq = lax.fori_loop(0, i * n, lambda j, c: blk(j, c, 0, False), jnp.zeros((TQ, 256), F32))
        for t in range(n):
            dq = blk(i * n + t, dq, t * TK, True)
        dq_ref[...] = dq * MLA_SCALE

        @pl.when(i == nq - 1)
        def _():
            pltpu.sync_copy(dk_acc, dk_hbm.at[h])
            pltpu.sync_copy(dv_acc, dv_hbm.at[h])

    any_spec = pl.BlockSpec(memory_space=pl.ANY)
    T = TQ
    return pl.pallas_call(
        body, name="mla_bwd", grid=(MLA_HEADS, nq),
        in_specs=[pl.BlockSpec((T, 256), lambda h, i: (i, h)), pl.BlockSpec((S, 256), lambda h, i: (0, h)),
                  pl.BlockSpec((S, LANE), lambda h, i: (0, h)), pl.BlockSpec((T, LANE), lambda h, i: (i, h)),
                  pl.BlockSpec((T, LANE), lambda h, i: (i, h)), pl.BlockSpec((T, LANE), lambda h, i: (i, h))],
        out_specs=[pl.BlockSpec((T, 256), lambda h, i: (i, h)), any_spec, any_spec],
        out_shape=[jax.ShapeDtypeStruct((S, MLA_HEADS * 256), F32), jax.ShapeDtypeStruct((MLA_HEADS, S, 256), F32),
                   jax.ShapeDtypeStruct((MLA_HEADS, S, LANE), F32)],
        scratch_shapes=[pltpu.VMEM((S, 256), F32), pltpu.VMEM((S, LANE), F32)],
        compiler_params=_params(("arbitrary", "arbitrary")))(q, kp, v, do_cat, o_cat, lse)


def _split_dot(x, tri):
    top = lax.bitcast_convert_type(lax.bitcast_convert_type(x, jnp.uint32) & jnp.uint32(0xFFFF0000), F32)
    return _dot(top.astype(BF16), tri) + _dot((x - top).astype(BF16), tri)


def _sb_block(qb, kb, tri, carry, masked):
    z = _dot_nt(qb, kb)
    lb = jnp.minimum(z, 0.0) - jnp.log(1.0 + jnp.exp(-jnp.abs(z)))
    lm = lb - z
    strict = None
    if masked:
        strict = _tail_masks(z.shape[0], z.shape[1])[1]
        lm = jnp.where(strict, lm, 0.0)
    a = jnp.exp(lb + carry + _split_dot(lm, tri))
    if masked:
        a = jnp.where(strict, a, 0.0)
    return a, lb, lm, strict


def _sb_walk(blk, j0, state):
    def alive(c):
        return jnp.logical_and(c[0] >= 0, jnp.max(c[1][0]) > SB_DEAD)

    return lax.while_loop(alive, lambda c: (c[0] - 1, blk(c[0], c[1], 0, False)), (j0, state))[1]


def _triangle(tk):
    row = lax.broadcasted_iota(jnp.int32, (tk, tk), 0)
    col = lax.broadcasted_iota(jnp.int32, (tk, tk), 1)
    return (row > col).astype(BF16)


def _sb_fwd(qkv):
    S = qkv.shape[0]
    TQ, TK, n = _att_blocks(S, SB_Q_BLK)
    T = TQ

    def body(q_ref, k_ref, v_ref, o_ref):
        i = pl.program_id(1)
        tri = _triangle(TK)

        def blk(j, state, r0, masked):
            carry, acc = (c[r0:] for c in state)
            sl = pl.ds(pl.multiple_of(j * TK, TK), TK)
            a, _, lm, _ = _sb_block(q_ref[r0:, :], k_ref[sl, :], tri, carry, masked)
            new = (carry + jnp.sum(lm, axis=1, keepdims=True), acc + _dot(a.astype(BF16), v_ref[sl, :]))
            return tuple(_put_rows(c, u, r0) for c, u in zip(state, new))

        state = (jnp.zeros((TQ, 1), F32), jnp.zeros((TQ, LANE), F32))
        for t in reversed(range(n)):
            state = blk(i * n + t, state, t * TK, True)
        state = _sb_walk(blk, i * n - 1, state)
        o_ref[...] = state[1]

    return pl.pallas_call(
        body, name="sb_fwd", grid=(SB_HEADS, S // T),
        in_specs=[pl.BlockSpec((T, LANE), lambda h, i: (i, h)), pl.BlockSpec((S, LANE), lambda h, i: (0, 4 + h)),
                  pl.BlockSpec((S, LANE), lambda h, i: (0, 8 + h))],
        out_specs=pl.BlockSpec((T, LANE), lambda h, i: (i, h)),
        out_shape=jax.ShapeDtypeStruct((S, SB_HEADS * LANE), F32),
        compiler_params=_params(("parallel", "arbitrary")))(qkv, qkv, qkv)


def _sb_bwd(qkv, do_cat, o_cat, col0):
    S = qkv.shape[0]
    TQ, TK, n = _att_blocks(S, SB_Q_BLK)
    T = TQ
    nq = S // TQ

    def body(q_ref, k_ref, v_ref, do_ref, o_ref, dq_ref, dk_hbm, dv_hbm, dk_acc, dv_acc):
        h, i = pl.program_id(0), pl.program_id(1)

        @pl.when(i == 0)
        def _():
            dk_acc[...] = jnp.zeros_like(dk_acc)
            dv_acc[...] = jnp.zeros_like(dv_acc)

        dob = do_ref[...].astype(BF16)
        tri = _triangle(TK)
        rest0 = jnp.sum(dob.astype(F32) * o_ref[...], axis=1, keepdims=True)

        def blk(j, state, r0, masked):
            carry, rest, dq = (c[r0:] for c in state)
            sl = pl.ds(pl.multiple_of(j * TK, TK), TK)
            kb, vb, qb = k_ref[sl, :], v_ref[sl, :], q_ref[r0:, :]
            a, lb, lm, strict = _sb_block(qb, kb, tri, carry, masked)
            ab = a.astype(BF16)
            e = ab.astype(F32) * _dot_nt(dob[r0:], vb)
            dz = e - jnp.exp(lb) * (rest - _split_dot(e, tri))
            if masked:
                dz = jnp.where(strict, dz, 0.0)
            dzb = dz.astype(BF16)
            dk_acc[sl, :] += _dot_tn(dzb, qb)
            dv_acc[sl, :] += _dot_tn(ab, dob[r0:])
            new = (carry + jnp.sum(lm, axis=1, keepdims=True), rest - jnp.sum(e, axis=1, keepdims=True),
                   dq + _dot(dzb, kb))
            return tuple(_put_rows(c, u, r0) for c, u in zip(state, new))

        state = (jnp.zeros((TQ, 1), F32), rest0, jnp.zeros((TQ, LANE), F32))
        for t in reversed(range(n)):
            state = blk(i * n + t, state, t * TK, True)
        state = _sb_walk(blk, i * n - 1, state)
        dq_ref[...] = state[2] * SB_SCALE

        @pl.when(i == nq - 1)
        def _():
            pltpu.sync_copy(dk_acc, dk_hbm.at[h])
            pltpu.sync_copy(dv_acc, dv_hbm.at[h])

    any_spec = pl.BlockSpec(memory_space=pl.ANY)
    return pl.pallas_call(
        body, name="sb_bwd", grid=(SB_HEADS, nq),
        in_specs=[pl.BlockSpec((T, LANE), lambda h, i: (i, h)), pl.BlockSpec((S, LANE), lambda h, i: (0, 4 + h)),
                  pl.BlockSpec((S, LANE), lambda h, i: (0, 8 + h)),
                  pl.BlockSpec((T, LANE), lambda h, i: (i, col0 + h)), pl.BlockSpec((T, LANE), lambda h, i: (i, col0 + h))],
        out_specs=[pl.BlockSpec((T, LANE), lambda h, i: (i, h)), any_spec, any_spec],
        out_shape=[jax.ShapeDtypeStruct((S, SB_HEADS * LANE), F32), jax.ShapeDtypeStruct((SB_HEADS, S, LANE), F32),
                   jax.ShapeDtypeStruct((SB_HEADS, S, LANE), F32)],
        scratch_shapes=[pltpu.VMEM((S, LANE), F32), pltpu.VMEM((S, LANE), F32)],
        compiler_params=_params(("arbitrary", "arbitrary")))(qkv, qkv, qkv, do_cat, o_cat)


def _mem_probs(q, k_ref, hh):
    lane = lax.broadcasted_iota(jnp.int32, (1, 256), 1) // 64
    msk = lane == hh
    qh = jnp.where(msk, q, 0.0).astype(BF16)
    s = _dot_nt(qh, k_ref[...]) * MEM_SCALE
    p = jnp.exp(s - jnp.max(s, axis=1, keepdims=True))
    return msk, qh, p / jnp.sum(p, axis=1, keepdims=True)


def _mem_fwd(h, mk, mv):
    S = h.shape[0]
    tm = _tile(S, 512)

    def body(q_ref, k_ref, v_ref, o_ref):
        q = q_ref[...]
        out = jnp.zeros((tm, 256), F32)
        for hh in range(MEM_HEADS):
            msk, _, p = _mem_probs(q, k_ref, hh)
            out = out + jnp.where(msk, _dot(p.astype(BF16), v_ref[...]), 0.0)
        o_ref[...] = out

    return pl.pallas_call(
        body, name="mem_fwd", grid=(S // tm,),
        in_specs=[_row_spec(tm, 256, MQ // 256), _fix_spec((256, 256)), _fix_spec((256, 256))],
        out_specs=_row_spec(tm, 256), out_shape=jax.ShapeDtypeStruct((S, 256), F32),
        compiler_params=_params(("parallel",)))(h, mk, mv)


def _mem_bwd(h, mk, mv, do_cat, col0):
    S = h.shape[0]
    tm = _tile(S, 512)

    def body(q_ref, k_ref, v_ref, do_ref, dq_ref, dk_ref, dv_ref):
        @pl.when(pl.program_id(0) == 0)
        def _():
            dk_ref[...] = jnp.zeros_like(dk_ref)
            dv_ref[...] = jnp.zeros_like(dv_ref)

        q, do = q_ref[...], do_ref[...]
        dq = jnp.zeros((tm, 256), F32)
        for hh in range(MEM_HEADS):
            msk, qh, p = _mem_probs(q, k_ref, hh)
            doh = jnp.where(msk, do, 0.0).astype(BF16)
            dp = _dot_nt(doh, v_ref[...])
            ds = (p * (dp - jnp.sum(p * dp, axis=1, keepdims=True)) * MEM_SCALE).astype(BF16)
            dq = dq + jnp.where(msk, _dot(ds, k_ref[...]), 0.0)
            dk_ref[...] += _dot_tn(ds, qh)
            dv_ref[...] += _dot_tn(p.astype(BF16), doh)
        dq_ref[...] = dq

    return pl.pallas_call(
        body, name="mem_bwd", grid=(S // tm,),
        in_specs=[_row_spec(tm, 256, MQ // 256), _fix_spec((256, 256)), _fix_spec((256, 256)),
                  _row_spec(tm, 256, col0 // 256)],
        out_specs=[_row_spec(tm, 256), _fix_spec((256, 256)), _fix_spec((256, 256))],
        out_shape=[jax.ShapeDtypeStruct((S, 256), F32), jax.ShapeDtypeStruct((256, 256), F32),
                   jax.ShapeDtypeStruct((256, 256), F32)],
        compiler_params=_params(("arbitrary",)))(h, mk, mv, do_cat)


SG_T = 128


def _sg_norm(sv, g, b):
    gv = _gelu(sv)
    xc = gv - jnp.mean(gv, axis=1, keepdims=True)
    rstd = lax.rsqrt(jnp.mean(xc * xc, axis=1, keepdims=True) + LN_EPS)
    xhat = xc * rstd
    return xhat, rstd, xhat * g + b


def _sg_fwd(h, lng, lnb, w, bias_t):
    S = h.shape[0]
    tm = _tile(S, 512)

    def body(u_ref, v_ref, g_ref, b_ref, w_ref, bias_ref, o_ref):
        mask = _chunk_mask(SG_T)
        for n in range(tm // SG_T):
            rows = slice(n * SG_T, (n + 1) * SG_T)
            u = _gelu(u_ref[rows, :])
            _, _, vn = _sg_norm(v_ref[rows, :], g_ref[...], b_ref[...])
            vb = vn.astype(BF16)
            for gi in range(4):
                cols = slice(gi * LANE, (gi + 1) * LANE)
                wg = jnp.where(mask, w_ref[gi], 0.0).astype(BF16)
                mixed = _dot(wg, vb[:, cols]) + bias_ref[:, gi:gi + 1]
                o_ref[rows, cols] = u[:, cols] * mixed

    return pl.pallas_call(
        body, name="sg_fwd", grid=(S // tm,),
        in_specs=[_row_spec(tm, 512, SGU // 512), _row_spec(tm, 512, SGV // 512), _fix_spec((1, 512)),
                  _fix_spec((1, 512)), _fix_spec((4, SG_T, SG_T)), _fix_spec((SG_T, 4))],
        out_specs=_row_spec(tm, 512), out_shape=jax.ShapeDtypeStruct((S, 512), F32),
        compiler_params=_params(("parallel",)))(h, h, lng.reshape(1, 512), lnb.reshape(1, 512), w, bias_t)


def _sg_bwd(h, lng, lnb, w, bias_t, do_cat, col0):
    S = h.shape[0]
    tm = _tile(S, 512)
    nsteps = S // tm

    def body(u_ref, v_ref, g_ref, b_ref, w_ref, bias_ref, do0_ref, do1_ref, do2_ref, do3_ref,
             du_ref, dv_ref, dw_ref, dbias_ref, dg_ref, db_ref, dvn_scr, dbias_acc):
        do_refs = (do0_ref, do1_ref, do2_ref, do3_ref)
        step = pl.program_id(0)

        @pl.when(step == 0)
        def _():
            dw_ref[...] = jnp.zeros_like(dw_ref)
            dg_ref[...] = jnp.zeros_like(dg_ref)
            db_ref[...] = jnp.zeros_like(db_ref)
            dbias_acc[...] = jnp.zeros_like(dbias_acc)

        mask = _chunk_mask(SG_T)
        for n in range(tm // SG_T):
            rows = slice(n * SG_T, (n + 1) * SG_T)
            su, sv = u_ref[rows, :], v_ref[rows, :]
            u = _gelu(su)
            xhat, rstd, vn = _sg_norm(sv, g_ref[...], b_ref[...])
            vb = vn.astype(BF16)
            ugrad = _gelu_grad(su)
            for gi in range(4):
                cols = slice(gi * LANE, (gi + 1) * LANE)
                do = do_refs[gi][rows, :]
                wg = jnp.where(mask, w_ref[gi], 0.0).astype(BF16)
                mixed = _dot(wg, vb[:, cols]) + bias_ref[:, gi:gi + 1]
                dmixed = do * u[:, cols]
                dmb = dmixed.astype(BF16)
                du_ref[rows, cols] = do * mixed * ugrad[:, cols]
                dvn_scr[:, cols] = _dot_tn(wg, dmb)
                dw_ref[gi] += jnp.where(mask, _dot_nt(dmb, vb[:, cols]), 0.0)
                dbias_acc[gi] += dmixed
            dvn = dvn_scr[...]
            dg_ref[...] += jnp.sum(dvn * xhat, axis=0, keepdims=True)
            db_ref[...] += jnp.sum(dvn, axis=0, keepdims=True)
            dxh = dvn * g_ref[...]
            dgv = rstd * (dxh - jnp.mean(dxh, axis=1, keepdims=True)
                          - xhat * jnp.mean(dxh * xhat, axis=1, keepdims=True))
            dv_ref[rows, :] = dgv * _gelu_grad(sv)

        @pl.when(step == nsteps - 1)
        def _():
            for gi in range(4):
                dbias_ref[:, gi:gi + 1] = jnp.sum(dbias_acc[gi], axis=1, keepdims=True)

    return pl.pallas_call(
        body, name="sg_bwd", grid=(nsteps,),
        in_specs=[_row_spec(tm, 512, SGU // 512), _row_spec(tm, 512, SGV // 512), _fix_spec((1, 512)),
                  _fix_spec((1, 512)), _fix_spec((4, SG_T, SG_T)), _fix_spec((SG_T, 4))]
                 + [_row_spec(tm, LANE, col0 // LANE + gi) for gi in range(4)],
        out_specs=[_row_spec(tm, 512), _row_spec(tm, 512), _fix_spec((4, SG_T, SG_T)), _fix_spec((SG_T, 4)),
                   _fix_spec((1, 512)), _fix_spec((1, 512))],
        out_shape=[jax.ShapeDtypeStruct((S, 512), F32), jax.ShapeDtypeStruct((S, 512), F32),
                   jax.ShapeDtypeStruct((4, SG_T, SG_T), F32), jax.ShapeDtypeStruct((SG_T, 4), F32),
                   jax.ShapeDtypeStruct((1, 512), F32), jax.ShapeDtypeStruct((1, 512), F32)],
        scratch_shapes=[pltpu.VMEM((SG_T, 512), F32), pltpu.VMEM((4, SG_T, SG_T), F32)],
        compiler_params=_params(("arbitrary",)))(h, h, lng.reshape(1, 512), lnb.reshape(1, 512), w, bias_t,
                                                 do_cat, do_cat, do_cat, do_cat)


def _gate_fwd(o_cat, h):
    S = h.shape[0]
    tm = _tile(S, 512)

    def body(o_ref, g_ref, y_ref):
        g = g_ref[...]
        y_ref[...] = (o_ref[...] * (g * jax.nn.sigmoid(g))).astype(BF16)

    return pl.pallas_call(
        body, name="gate_fwd", grid=(S // tm, 4),
        in_specs=[pl.BlockSpec((tm, 512), lambda i, j: (i, j)), pl.BlockSpec((tm, 512), lambda i, j: (i, GATE // 512 + j))],
        out_specs=pl.BlockSpec((tm, 512), lambda i, j: (i, j)), out_shape=jax.ShapeDtypeStruct((S, D_MODEL), BF16),
        compiler_params=_params(("parallel", "parallel")))(o_cat, h)


def _gate_bwd(dyg, o_cat, h):
    S = h.shape[0]
    tm = _tile(S, 512)

    def body(d_ref, o_ref, g_ref, do_ref, dg_ref):
        d, g = d_ref[...], g_ref[...]
        sig = jax.nn.sigmoid(g)
        do_ref[...] = d * (g * sig)
        dg_ref[...] = d * o_ref[...] * (sig * (1.0 + g * (1.0 - sig)))

    blk = pl.BlockSpec((tm, 512), lambda i, j: (i, j))
    return pl.pallas_call(
        body, name="gate_bwd", grid=(S // tm, 4),
        in_specs=[blk, blk, pl.BlockSpec((tm, 512), lambda i, j: (i, GATE // 512 + j))],
        out_specs=[blk, blk],
        out_shape=[jax.ShapeDtypeStruct((S, D_MODEL), F32), jax.ShapeDtypeStruct((S, D_MODEL), F32)],
        compiler_params=_params(("parallel", "parallel")))(dyg, o_cat, h)


def _ln_res_fwd(x, y, g, b):
    S = x.shape[0]
    tm = _tile(S, 256)

    def body(x_ref, y_ref, g_ref, b_ref, o_ref, ob_ref, r_ref):
        r = ALPHA * x_ref[...] + y_ref[...]
        r_ref[...] = r
        xc = r - jnp.mean(r, axis=1, keepdims=True)
        o = xc * lax.rsqrt(jnp.mean(xc * xc, axis=1, keepdims=True) + LN_EPS) * g_ref[...] + b_ref[...]
        o_ref[...] = o
        ob_ref[...] = o.astype(BF16)

    return pl.pallas_call(
        body, name="ln_res_fwd", grid=(S // tm,),
        in_specs=[_row_spec(tm, D_MODEL), _row_spec(tm, D_MODEL), _fix_spec((1, D_MODEL)), _fix_spec((1, D_MODEL))],
        out_specs=[_row_spec(tm, D_MODEL), _row_spec(tm, D_MODEL), _row_spec(tm, D_MODEL)],
        out_shape=[jax.ShapeDtypeStruct((S, D_MODEL), F32), jax.ShapeDtypeStruct((S, D_MODEL), BF16),
                   jax.ShapeDtypeStruct((S, D_MODEL), F32)],
        compiler_params=_params(("parallel",)))(x, y, g.reshape(1, D_MODEL), b.reshape(1, D_MODEL))


def _ln_res_bwd(dout, r, g):
    S = r.shape[0]
    tm = _tile(S, 256)

    def body(d_ref, r_ref, g_ref, dr_ref, dg_ref, db_ref):
        @pl.when(pl.program_id(0) == 0)
        def _():
            dg_ref[...] = jnp.zeros_like(dg_ref)
            db_ref[...] = jnp.zeros_like(db_ref)

        d, r = d_ref[...], r_ref[...]
        xc = r - jnp.mean(r, axis=1, keepdims=True)
        rstd = lax.rsqrt(jnp.mean(xc * xc, axis=1, keepdims=True) + LN_EPS)
        xhat = xc * rstd
        dxh = d * g_ref[...]
        dr_ref[...] = rstd * (dxh - jnp.mean(dxh, axis=1, keepdims=True)
                              - xhat * jnp.mean(dxh * xhat, axis=1, keepdims=True))
        dg_ref[...] += jnp.sum(d * xhat, axis=0, keepdims=True)
        db_ref[...] += jnp.sum(d, axis=0, keepdims=True)

    return pl.pallas_call(
        body, name="ln_res_bwd", grid=(S // tm,),
        in_specs=[_row_spec(tm, D_MODEL), _row_spec(tm, D_MODEL), _fix_spec((1, D_MODEL))],
        out_specs=[_row_spec(tm, D_MODEL), _fix_spec((1, D_MODEL)), _fix_spec((1, D_MODEL))],
        out_shape=[jax.ShapeDtypeStruct((S, D_MODEL), F32), jax.ShapeDtypeStruct((1, D_MODEL), F32),
                   jax.ShapeDtypeStruct((1, D_MODEL), F32)],
        compiler_params=_params(("arbitrary",)))(dout, r, g.reshape(1, D_MODEL))


def _loss_head(y, target):
    S = y.shape[0]
    tm = _tile(S, 256)

    def body(y_ref, t_ref, l_ref, d_ref):
        @pl.when(pl.program_id(0) == 0)
        def _():
            l_ref[...] = jnp.zeros_like(l_ref)

        diff = y_ref[...] - t_ref[...]
        d_ref[...] = diff * (1.0 / D_MODEL)
        per_row = jnp.mean(diff * diff, axis=1, keepdims=True)
        l_ref[...] += 0.5 * jnp.sum(per_row, axis=0, keepdims=True)

    return pl.pallas_call(
        body, name="loss_head", grid=(S // tm,), in_specs=[_row_spec(tm, D_MODEL), _row_spec(tm, D_MODEL)],
        out_specs=[_fix_spec((8, LANE)), _row_spec(tm, D_MODEL)],
        out_shape=[jax.ShapeDtypeStruct((8, LANE), F32), jax.ShapeDtypeStruct((S, D_MODEL), F32)],
        compiler_params=_params(("arbitrary",)))(y, target)


def _perm_table():
    table, at = [], 0
    for name in PERM_ORDER:
        start, width = ORIG[name]
        table.append((name, start, width, at))
        at += width
    return table


def _permute_w_in(w):
    parts = [w[..., start:start + width] for _, start, width, _ in _perm_table()]
    return jnp.concatenate(parts + [jnp.zeros(w.shape[:-1] + (HP - D_IN,), w.dtype)], axis=-1)


def _unpermute_w_in(wp):
    parts = sorted(_perm_table(), key=lambda t: t[1])
    return jnp.concatenate([wp[..., at:at + width] for _, _, width, at in parts], axis=-1)


def _rope_tables(positions):
    inv_freq = ROPE_THETA ** (-jnp.arange(0, 64, 2, dtype=F32) / 64)
    ang = positions.astype(F32)[:, None] * inv_freq[None, :]
    cos, sin, zero = jnp.cos(ang), jnp.sin(ang), jnp.zeros((positions.shape[0], 64), F32)
    return jnp.concatenate([cos, cos, zero], axis=1), jnp.concatenate([-sin, sin, zero], axis=1)


def _local_step(x, mem, positions, target, w):
    rc, rs = _rope_tables(positions)
    mem_b = mem.astype(BF16)
    xb = x.astype(BF16)
    w_in_all = _permute_w_in(w["w_in"])
    w_uq_all = jnp.pad(w["w_uq"].reshape(DEPTH, 512, MLA_HEADS, 192),
                       ((0, 0), (0, 0), (0, 0), (0, 64))).reshape(DEPTH, 512, MLA_HEADS * 256)
    saved = []
    for l in range(DEPTH):
        w_in, w_uq, w_ukv = w_in_all[l], w_uq_all[l], w["w_ukv"][l]
        h = _mm(xb, w_in, tm=1024, tn=1152, tk=2048, name="in_proj")
        cq_n = _rms_fwd(h, CQ, 512, w["q_norm_g"][l], "rms_q")
        ckv_n = _rms_fwd(h, CKV, 256, w["kv_norm_g"][l], "rms_kv")
        q = _q_proj(cq_n, w_uq, rc, rs)
        kp, v = _kv_proj(ckv_n, w_ukv, h, rc, rs)
        o_a, lse = _mla_fwd(q, kp, v)
        bias_t = w["sg_b"][l].T
        o_b = _sg_fwd(h, w["sg_ln_g"][l], w["sg_ln_b"][l], w["sg_w"][l], bias_t)
        qkv = jnp.concatenate([h[:, SBQ:SBQ + 512] * SB_SCALE, h[:, SBQ + 512:SBQ + 1536]], axis=1).astype(BF16)
        o_c = _sb_fwd(qkv)
        mk = _mm(mem_b, w["w_mem_k"][l], out_dtype=BF16, name="mem_kv")
        mv = _mm(mem_b, w["w_mem_v"][l], out_dtype=BF16, name="mem_kv")
        o_m = _mem_fwd(h, mk, mv)
        o_cat = jnp.concatenate([o_a, o_b, o_c, o_m], axis=1)
        yg = _gate_fwd(o_cat, h)
        y = _mm(yg, w["w_out"][l], tm=1024, tn=1024, tk=2048, name="out_proj")
        x_new, xb_new, r = _ln_res_fwd(x, y, w["ln_g"][l], w["ln_b"][l])
        saved.append(dict(xb=xb, h=h, cq_n=cq_n, ckv_n=ckv_n, q=q, kp=kp, v=v, lse=lse, qkv=qkv, mk=mk, mv=mv,
                          o_cat=o_cat, yg=yg, r=r, w_in=w_in, w_uq=w_uq, w_ukv=w_ukv, bias_t=bias_t))
        x, xb = x_new, xb_new

    loss, dx = _loss_head(x, target)

    grads = {n: [None] * DEPTH for n in SHARDED + SMALL}
    for l in reversed(range(DEPTH)):
        s = saved[l]
        h = s["h"]
        dr, dlg, dlb = _ln_res_bwd(dx, s["r"], w["ln_g"][l])
        grads["ln_g"][l], grads["ln_b"][l] = dlg[0], dlb[0]
        grads["w_out"][l] = _mm(s["yg"], dr, ta=True, tm=1024, tn=1024, tk=1024, name="dw_out")
        dyg = _mm(dr, w["w_out"][l], tb=True, tm=1024, tn=1024, name="d_out_proj")
        do_cat, dgates = _gate_bwd(dyg, s["o_cat"], h)
        dmq, dmk, dmv = _mem_bwd(h, s["mk"], s["mv"], do_cat, 1792)
        grads["w_mem_k"][l] = _mm(mem_b, dmk, ta=True, name="dw_mem")
        grads["w_mem_v"][l] = _mm(mem_b, dmv, ta=True, name="dw_mem")
        dsq, dsk, dsv = _sb_bwd(s["qkv"], do_cat, s["o_cat"], 1280 // LANE)
        dsk = dsk.transpose(1, 0, 2).reshape(-1, 512)
        dsv = dsv.transpose(1, 0, 2).reshape(-1, 512)
        du, dv, dsgw, dsgb, dsg_g, dsg_b = _sg_bwd(h, w["sg_ln_g"][l], w["sg_ln_b"][l], w["sg_w"][l], s["bias_t"],
                                                   do_cat, 768)
        grads["sg_w"][l], grads["sg_b"][l] = dsgw, dsgb.T
        grads["sg_ln_g"][l], grads["sg_ln_b"][l] = dsg_g[0], dsg_b[0]
        dq, dk, dvv = _mla_bwd(s["q"], s["kp"], s["v"], do_cat, s["o_cat"], s["lse"])
        dq_raw = _q_rope_bwd(dq, rc, rs)
        dkv, dkpe = _kv_bwd_prep(dk, dvv, rc, rs)
        dw_uq = _mm(s["cq_n"], dq_raw, ta=True, tk=1024, name="dw_uq")
        grads["w_uq"][l] = dw_uq.reshape(512, MLA_HEADS, 256)[:, :, :192].reshape(512, MLA_HEADS * 192)
        grads["w_ukv"][l] = _mm(s["ckv_n"], dkv, ta=True, tk=1024, name="dw_ukv")
        dcq_n = _mm(dq_raw, s["w_uq"], tb=True, name="d_cq")
        dckv_n = _mm(dkv, s["w_ukv"], tb=True, name="d_ckv")
        dcq, dqg = _rms_bwd(h, CQ, 512, w["q_norm_g"][l], dcq_n, "rms_q_bwd")
        dckv, dkvg = _rms_bwd(h, CKV, 256, w["kv_norm_g"][l], dckv_n, "rms_kv_bwd")
        grads["q_norm_g"][l], grads["kv_norm_g"][l] = dqg[0], dkvg[0]
        dh = jnp.concatenate([dcq, dckv, dmq, du, dv, dsq, dsk, dsv, dgates, dkpe], axis=1).astype(BF16)
        dw_in = _mm(s["xb"], dh, ta=True, tm=1024, tn=1152, tk=1024, name="dw_in")
        grads["w_in"][l] = _unpermute_w_in(dw_in)
        dx = _mm(dh, s["w_in"], tb=True, add=dr, add_scale=ALPHA, tm=1024, tn=1024, tk=1152, name="d_in_proj")

    return loss, dx, grads


MESH = pl.DeviceIdType.MESH
HBM_SPEC = pl.BlockSpec(memory_space=pltpu.HBM)


def _place():
    x, y, c = lax.axis_index("x"), lax.axis_index("y"), lax.axis_index("c")
    return x, y, c, [(1 - x, y), (x, 1 - y), (1 - x, 1 - y)]


def _gather_weights(flat):
    R = flat.shape[0]
    H = R // 2

    def body(src, out, send_sems, recv_sems):
        x, y, c, chips = _place()
        mine = pl.ds(c * H, H)
        theirs = pl.ds((1 - c) * H, H)

        def copy(k, src_ref, chip, rows, to):
            return pltpu.make_async_remote_copy(src_ref=src_ref, dst_ref=out.at[chip, rows, :], send_sem=send_sems.at[k],
                                                recv_sem=recv_sems.at[k], device_id=to, device_id_type=MESH)

        sent = [copy(j, src.at[mine, :], 2 * x + y, mine, (px, py, c)) for j, (px, py) in enumerate(chips)]
        for cp in sent:
            cp.start()
        passed = []
        for j, (px, py) in enumerate(chips):
            copy(j, src.at[mine, :], 2 * px + py, mine, (px, py, c)).wait_recv()
            cp = copy(3 + j, out.at[2 * px + py, mine, :], 2 * px + py, mine, (x, y, 1 - c))
            cp.start()
            passed.append(cp)
        for j, (px, py) in enumerate(chips):
            copy(3 + j, src.at[theirs, :], 2 * px + py, theirs, (x, y, 1 - c)).wait_recv()
        for cp in sent + passed:
            cp.wait_send()

    return pl.pallas_call(
        body, name="gather_weights", in_specs=[HBM_SPEC], out_specs=HBM_SPEC,
        out_shape=jax.ShapeDtypeStruct((4, R, FLAT_W), flat.dtype),
        scratch_shapes=[pltpu.SemaphoreType.DMA((6,)), pltpu.SemaphoreType.DMA((6,))],
        compiler_params=pltpu.CompilerParams(has_side_effects=True))(flat)


def _swap_halves(g):
    _, _, H, W = g.shape

    def body(src, out, send_sem, recv_sem):
        x, y, c, _ = _place()
        cp = pltpu.make_async_remote_copy(src_ref=src.at[:, 1 - c], dst_ref=out, send_sem=send_sem, recv_sem=recv_sem,
                                          device_id=(x, y, 1 - c), device_id_type=MESH)
        cp.start()
        cp.wait()

    return pl.pallas_call(
        body, name="swap_halves", in_specs=[HBM_SPEC], out_specs=HBM_SPEC,
        out_shape=jax.ShapeDtypeStruct((4, H, W), g.dtype),
        scratch_shapes=[pltpu.SemaphoreType.DMA(()), pltpu.SemaphoreType.DMA(())],
        compiler_params=pltpu.CompilerParams(has_side_effects=True))(g)


def _pair_sum(g, other, c):
    _, _, H, W = g.shape
    th = _row_tile(H, 3 * W * 4)

    def body(c_ref, a_ref, b_ref, o_ref):
        o_ref[...] = (a_ref[...] + b_ref[...]).astype(BF16)

    return pl.pallas_call(
        body, name="pair_sum",
        grid_spec=pltpu.PrefetchScalarGridSpec(
            num_scalar_prefetch=1, grid=(4, H // th),
            in_specs=[pl.BlockSpec((None, None, th, W), lambda d, i, c_ref: (d, c_ref[0], i, 0)),
                      pl.BlockSpec((None, th, W), lambda d, i, c_ref: (d, i, 0))],
            out_specs=pl.BlockSpec((None, th, W), lambda d, i, c_ref: (d, i, 0))),
        out_shape=jax.ShapeDtypeStruct((4, H, W), BF16),
        compiler_params=_params(("parallel", "parallel")))(c, g, other)


def _exchange_chips(p):
    _, H, W = p.shape

    def body(src, out, send_sems, recv_sems):
        x, y, c, chips = _place()
        sent = []
        for j, (px, py) in enumerate(chips):
            cp = pltpu.make_async_remote_copy(src_ref=src.at[2 * px + py], dst_ref=out.at[j], send_sem=send_sems.at[j],
                                              recv_sem=recv_sems.at[j], device_id=(px, py, c), device_id_type=MESH)
            cp.start()
            sent.append(cp)
        for cp in sent:
            cp.wait()

    return pl.pallas_call(
        body, name="exchange_chips", in_specs=[HBM_SPEC], out_specs=HBM_SPEC,
        out_shape=jax.ShapeDtypeStruct((3, H, W), p.dtype),
        scratch_shapes=[pltpu.SemaphoreType.DMA((3,)), pltpu.SemaphoreType.DMA((3,))],
        compiler_params=pltpu.CompilerParams(has_side_effects=True))(p)


def _chip_sum(p, got, me):
    _, H, W = p.shape
    th = _row_tile(H, 4 * W * 4)

    def body(me_ref, p_ref, g_ref, o_ref):
        acc = p_ref[...].astype(F32)
        for k in range(3):
            acc = acc + g_ref[k].astype(F32)
        o_ref[...] = acc

    return pl.pallas_call(
        body, name="chip_sum",
        grid_spec=pltpu.PrefetchScalarGridSpec(
            num_scalar_prefetch=1, grid=(H // th,),
            in_specs=[pl.BlockSpec((None, th, W), lambda i, me_ref: (me_ref[0], i, 0)),
                      pl.BlockSpec((3, th, W), lambda i, me_ref: (0, i, 0))],
            out_specs=pl.BlockSpec((th, W), lambda i, me_ref: (i, 0))),
        out_shape=jax.ShapeDtypeStruct((H, W), F32), compiler_params=_params(("parallel",)))(me, p, got)


def _sum_parts(t, name):
    n, H, W = t.shape
    th = _row_tile(H, (n + 1) * W * 4)

    def body(t_ref, o_ref):
        acc = t_ref[0]
        for k in range(1, n):
            acc = acc + t_ref[k]
        o_ref[...] = acc

    return pl.pallas_call(
        body, name=name, grid=(H // th,), in_specs=[pl.BlockSpec((n, th, W), lambda i: (0, i, 0))],
        out_specs=pl.BlockSpec((th, W), lambda i: (i, 0)), out_shape=jax.ShapeDtypeStruct((H, W), F32),
        compiler_params=_params(("parallel",)))(t)


def _share_with_sibling(half):
    H, W = half.shape

    def body(src, out, send_sem, recv_sem):
        x, y, c, _ = _place()
        cp = pltpu.make_async_remote_copy(src_ref=src, dst_ref=out.at[c], send_sem=send_sem, recv_sem=recv_sem,
                                          device_id=(x, y, 1 - c), device_id_type=MESH)
        cp.start()
        pltpu.make_async_remote_copy(src_ref=src, dst_ref=out.at[1 - c], send_sem=send_sem, recv_sem=recv_sem,
                                     device_id=(x, y, 1 - c), device_id_type=MESH).wait_recv()
        cp.wait_send()

    return pl.pallas_call(
        body, name="share_with_sibling", in_specs=[HBM_SPEC], out_specs=HBM_SPEC,
        out_shape=jax.ShapeDtypeStruct((2, H, W), half.dtype),
        scratch_shapes=[pltpu.SemaphoreType.DMA(()), pltpu.SemaphoreType.DMA(())],
        compiler_params=pltpu.CompilerParams(has_side_effects=True))(half)


def _gather_all(v):
    n, W = v.shape

    def body(src, out, send_sems, recv_sems, own_sem):
        x, y, c, _ = _place()
        own = pltpu.make_async_copy(src, out.at[4 * x + 2 * y + c], own_sem)
        own.start()
        flips = [(fx, fy, fc) for fx in (0, 1) for fy in (0, 1) for fc in (0, 1)][1:]
        sent = []
        for k, (fx, fy, fc) in enumerate(flips):
            cp = pltpu.make_async_remote_copy(
                src_ref=src, dst_ref=out.at[4 * x + 2 * y + c], send_sem=send_sems.at[k], recv_sem=recv_sems.at[k],
                device_id=(x ^ fx, y ^ fy, c ^ fc), device_id_type=MESH)
            cp.start()
            sent.append(cp)
        for k, (fx, fy, fc) in enumerate(flips):
            pltpu.make_async_remote_copy(
                src_ref=src, dst_ref=out.at[4 * (x ^ fx) + 2 * (y ^ fy) + (c ^ fc)], send_sem=send_sems.at[k],
                recv_sem=recv_sems.at[k], device_id=(x ^ fx, y ^ fy, c ^ fc), device_id_type=MESH).wait_recv()
        for cp in sent:
            cp.wait_send()
        own.wait()

    return pl.pallas_call(
        body, name="gather_all", in_specs=[HBM_SPEC], out_specs=HBM_SPEC,
        out_shape=jax.ShapeDtypeStruct((8, n, W), v.dtype),
        scratch_shapes=[pltpu.SemaphoreType.DMA((7,)), pltpu.SemaphoreType.DMA((7,)), pltpu.SemaphoreType.DMA(())],
        compiler_params=pltpu.CompilerParams(has_side_effects=True))(v)


def _adamw(w, g, m, v):
    shape = w.shape
    cols = shape[-1]
    w2, g2, m2, v2 = (a.reshape(-1, cols) for a in (w, g, m, v))
    rows = w2.shape[0]
    tr = next((t for t in (1024, 512, 256, 128, 64, 32, 16, 8) if rows % t == 0 and t * cols * 4 <= (2 << 20)), rows)

    def body(w_ref, g_ref, m_ref, v_ref, d_ref, nm_ref, nv_ref):
        g_ = g_ref[...]
        nm = ADAM_B1 * m_ref[...] + (1.0 - ADAM_B1) * g_
        nv = ADAM_B2 * v_ref[...] + (1.0 - ADAM_B2) * (g_ * g_)
        m_hat = nm / (1.0 - ADAM_B1 ** ADAM_STEP)
        v_hat = nv / (1.0 - ADAM_B2 ** ADAM_STEP)
        d_ref[...] = -ADAM_LR * (m_hat / (jnp.sqrt(v_hat) + ADAM_EPS) + ADAM_WD * w_ref[...])
        nm_ref[...] = nm
        nv_ref[...] = nv

    blk = pl.BlockSpec((tr, cols), lambda i: (i, 0))
    outs = pl.pallas_call(
        body, name="adamw", grid=(rows // tr,), in_specs=[blk] * 4, out_specs=[blk] * 3,
        out_shape=[jax.ShapeDtypeStruct((rows, cols), F32)] * 3, compiler_params=_params(("parallel",)))(w2, g2, m2, v2)
    return tuple(o.reshape(shape) for o in outs)


def _chip_part(name, a, k):
    n = a.shape[1 if name in ("w_in", "w_uq", "w_ukv") else 0] // 4
    return a[:, k * n:(k + 1) * n] if name in ("w_in", "w_uq", "w_ukv") else a[k * n:(k + 1) * n]


def _join_chips(name, parts):
    return jnp.concatenate(parts, axis=2 if name in ("w_in", "w_uq", "w_ukv") else 1)


def kernel(x, mem, positions, w_in, q_norm_g, w_uq, kv_norm_g, w_ukv, sg_ln_g, sg_ln_b, sg_w, sg_b, w_mem_k, w_mem_v, w_out, ln_g, ln_b, loss_target, m_w_in, m_q_norm_g, m_w_uq, m_kv_norm_g, m_w_ukv, m_sg_ln_g, m_sg_ln_b, m_sg_w, m_sg_b, m_w_mem_k, m_w_mem_v, m_w_out, m_ln_g, m_ln_b, v_w_in, v_q_norm_g, v_w_uq, v_kv_norm_g, v_w_ukv, v_sg_ln_g, v_sg_ln_b, v_sg_w, v_sg_b, v_w_mem_k, v_w_mem_v, v_w_out, v_ln_g, v_ln_b):
    weights = dict(w_in=w_in, q_norm_g=q_norm_g, w_uq=w_uq, kv_norm_g=kv_norm_g, w_ukv=w_ukv, sg_ln_g=sg_ln_g,
                   sg_ln_b=sg_ln_b, sg_w=sg_w, sg_b=sg_b, w_mem_k=w_mem_k, w_mem_v=w_mem_v, w_out=w_out, ln_g=ln_g, ln_b=ln_b)
    mom_m = dict(w_in=m_w_in, q_norm_g=m_q_norm_g, w_uq=m_w_uq, kv_norm_g=m_kv_norm_g, w_ukv=m_w_ukv, sg_ln_g=m_sg_ln_g,
                 sg_ln_b=m_sg_ln_b, sg_w=m_sg_w, sg_b=m_sg_b, w_mem_k=m_w_mem_k, w_mem_v=m_w_mem_v, w_out=m_w_out,
                 ln_g=m_ln_g, ln_b=m_ln_b)
    mom_v = dict(w_in=v_w_in, q_norm_g=v_q_norm_g, w_uq=v_w_uq, kv_norm_g=v_kv_norm_g, w_ukv=v_w_ukv, sg_ln_g=v_sg_ln_g,
                 sg_ln_b=v_sg_ln_b, sg_w=v_sg_w, sg_b=v_sg_b, w_mem_k=v_w_mem_k, w_mem_v=v_w_mem_v, w_out=v_w_out,
                 ln_g=v_ln_g, ln_b=v_ln_b)
    c_idx = lax.axis_index("c").astype(jnp.int32).reshape(1)

    sizes = [weights[n].size for n in SHARDED]
    flat = jnp.concatenate([weights[n].astype(BF16).reshape(-1) for n in SHARDED]).reshape(-1, FLAT_W)
    me = 2 * lax.axis_index("x") + lax.axis_index("y")
    gathered = lax.dynamic_update_slice(_gather_weights(flat), flat[None], (me, 0, 0))
    full = dict((n, weights[n]) for n in SMALL)
    at = 0
    for n, size in zip(SHARDED, sizes):
        rows = size // FLAT_W
        full[n] = _join_chips(n, [gathered[k, at:at + rows].reshape(weights[n].shape) for k in range(4)])
        at += rows
    R = at

    loss_dev, grad_x, grads = _local_step(x[0], mem[0], positions[0], loss_target[0], full)

    per_chip = [jnp.concatenate([_chip_part(n, g, k).reshape(-1) for n in SHARDED for g in grads[n]]) for k in range(4)]
    g4 = jnp.stack(per_chip).reshape(4, 2, R // 2, FLAT_W)
    pair = _pair_sum(g4, _swap_halves(g4), c_idx)
    half = _chip_sum(pair, _exchange_chips(pair), me.astype(jnp.int32).reshape(1))
    reduced = lax.dynamic_update_slice(_share_with_sibling(half), half[None], (c_idx[0], 0, 0)).reshape(R, FLAT_W)
    grad_out = {}
    at = 0
    for n, size in zip(SHARDED, sizes):
        rows = size // FLAT_W
        grad_out[n] = reduced[at:at + rows].reshape(weights[n].shape)
        at += rows

    small_sizes = [weights[n].size for n in SMALL]
    vec = jnp.concatenate([g.reshape(-1) for n in SMALL for g in grads[n]] + [loss_dev[0]])
    n_small = vec.shape[0]
    rows_small = -(-n_small // (8 * FLAT_W)) * 8
    vec = jnp.pad(vec, (0, rows_small * FLAT_W - n_small)).reshape(rows_small, FLAT_W)
    total = _sum_parts(_gather_all(vec), "device_sum").reshape(-1)
    at = 0
    for n, size in zip(SMALL, small_sizes):
        grad_out[n] = total[at:at + size].reshape(weights[n].shape)
        at += size
    loss = total[at]

    names = list(weights)
    upd = {n: _adamw(weights[n], grad_out[n], mom_m[n], mom_v[n]) for n in names}
    return (loss, grad_x[None], *[grad_out[n] for n in names], *[upd[n][0] for n in names],
            *[upd[n][1] for n in names], *[upd[n][2] for n in names])
```

```python
import math

import jax
import jax.numpy as jnp
from jax import lax
from jax.experimental import pallas as pl
from jax.experimental.pallas import tpu as pltpu

F32, BF16 = jnp.float32, jnp.bfloat16

D_MODEL = 2048
DEPTH = 4
CHUNK = 64
MLA_HEADS = 6
MLA_SCALE = 1.0 / math.sqrt(192.0)
SB_HEADS = 4
SB_SCALE = 1.0 / math.sqrt(128.0)
MEM_HEADS = 4
MEM_SCALE = 1.0 / math.sqrt(64.0)
ROPE_THETA = 10000.0
ALPHA = (2.0 * DEPTH) ** 0.25
LN_EPS = 1e-5
RMS_EPS = 1e-6
ADAM_LR, ADAM_B1, ADAM_B2, ADAM_EPS, ADAM_WD, ADAM_STEP = 0.001, 0.9, 0.999, 1e-08, 0.01, 10

ORIG = dict(c_q=(0, 512), c_kv=(512, 256), k_pe=(768, 64), g_a=(832, 768), sg_u=(1600, 512), sg_v=(2112, 512),
            g_b=(2624, 512), sb_q=(3136, 512), sb_k=(3648, 512), sb_v=(4160, 512), g_c=(4672, 512),
            m_q=(5184, 256), g_m=(5440, 256))
D_IN = 5696
PERM_ORDER = ("c_q", "c_kv", "m_q", "sg_u", "sg_v", "sb_q", "sb_k", "sb_v", "g_a", "g_b", "g_c", "g_m", "k_pe")
HP = 5760
CQ, CKV, MQ, SGU, SGV, SBQ, GATE, KPE = 0, 512, 768, 1024, 1536, 2048, 3584, 5632

Q_BLK = 1024
K_BLK = 256
SB_Q_BLK = 512
SB_DEAD = -110.0
LANE = 128
VMEM_LIMIT = 56 * 1024 * 1024

FLAT_W = 1024
SHARDED = ("w_in", "w_uq", "w_ukv", "w_mem_k", "w_mem_v", "w_out")
SMALL = ("q_norm_g", "kv_norm_g", "sg_ln_g", "sg_ln_b", "sg_w", "sg_b", "ln_g", "ln_b")


def _params(sem=None):
    return pltpu.CompilerParams(dimension_semantics=sem, vmem_limit_bytes=VMEM_LIMIT)


def _tile(dim, pref):
    if dim <= pref:
        return dim
    t = (pref // LANE) * LANE
    while t >= LANE:
        if dim % t == 0:
            return t
        t -= LANE
    return dim


def _row_tile(rows, bytes_per_row, budget=8 << 20):
    best = None
    for t in range(8, rows + 1, 8):
        if rows % t == 0 and t * bytes_per_row <= budget:
            best = t
    return best if best else rows


def _dot_nt(a, b):
    return lax.dot_general(a, b, (((1,), (1,)), ((), ())), preferred_element_type=F32)


def _dot_tn(a, b):
    return lax.dot_general(a, b, (((0,), (0,)), ((), ())), preferred_element_type=F32)


def _dot(a, b):
    return jnp.dot(a, b, preferred_element_type=F32)


def _mm(a, b, *, ta=False, tb=False, a_win=None, b_win=None, add=None, add_scale=1.0, out_dtype=F32,
        tm=512, tn=512, tk=512, name="mm"):
    a_off, a_w = a_win if a_win else (0, a.shape[1])
    b_off, b_w = b_win if b_win else (0, b.shape[1])
    (K, M) = (a.shape[0], a_w) if ta else (a_w, a.shape[0])
    (N, Kb) = (b.shape[0], b_w) if tb else (b_w, b.shape[0])
    assert K == Kb, (a.shape, b.shape, ta, tb)
    tm, tn, tk = _tile(M, tm), _tile(N, tn), _tile(K, tk)
    nk = K // tk
    if ta:
        assert a_off % tm == 0
        a_spec = pl.BlockSpec((tk, tm), lambda i, j, k: (k, i + a_off // tm))
    else:
        assert a_off % tk == 0
        a_spec = pl.BlockSpec((tm, tk), lambda i, j, k: (i, k + a_off // tk))
    if tb:
        assert b_off % tk == 0
        b_spec = pl.BlockSpec((tn, tk), lambda i, j, k: (j, k + b_off // tk))
    else:
        assert b_off % tn == 0
        b_spec = pl.BlockSpec((tk, tn), lambda i, j, k: (k, j + b_off // tn))
    o_spec = pl.BlockSpec((tm, tn), lambda i, j, k: (i, j))
    dn = (((0 if ta else 1,), (1 if tb else 0,)), ((), ()))
    has_add = add is not None

    def body(*refs):
        a_ref, b_ref = refs[:2]
        add_ref = refs[2] if has_add else None
        o_ref = refs[3 if has_add else 2]
        part = lax.dot_general(a_ref[...].astype(BF16), b_ref[...].astype(BF16), dn, preferred_element_type=F32)

        def finish(r):
            if has_add:
                r = r + add_scale * add_ref[...]
            o_ref[...] = r.astype(o_ref.dtype)

        if nk == 1:
            finish(part)
            return
        acc_ref = refs[-1]
        k = pl.program_id(2)

        @pl.when(k == 0)
        def _():
            acc_ref[...] = part

        @pl.when(k > 0)
        def _():
            acc_ref[...] += part

        @pl.when(k == nk - 1)
        def _():
            finish(acc_ref[...])

    ins = [a, b] + ([add] if has_add else [])
    specs = [a_spec, b_spec] + ([o_spec] if has_add else [])
    return pl.pallas_call(
        body, name=name, grid=(M // tm, N // tn, nk), in_specs=specs, out_specs=o_spec,
        out_shape=jax.ShapeDtypeStruct((M, N), out_dtype),
        scratch_shapes=[pltpu.VMEM((tm, tn), F32)] if nk > 1 else [],
        compiler_params=_params(("parallel", "parallel", "arbitrary")))(*ins)


GELU_K = math.sqrt(2.0 / math.pi)


def _gelu(x):
    t = jnp.tanh(GELU_K * (x + 0.044715 * (x * x * x)))
    return 0.5 * x * (1.0 + t)


def _gelu_grad(x):
    t = jnp.tanh(GELU_K * (x + 0.044715 * (x * x * x)))
    return 0.5 * (1.0 + t) + 0.5 * x * (1.0 - t * t) * GELU_K * (1.0 + 3.0 * 0.044715 * x * x)


def _rope_swap(t):
    lane = lax.broadcasted_iota(jnp.int32, t.shape, 1)
    return jnp.where(lane < 32, pltpu.roll(t, 96, axis=1), pltpu.roll(t, 32, axis=1))


def _rope(t, c, s):
    return t * c + _rope_swap(t) * s


def _rope_bwd(dt, c, s):
    return dt * c - _rope_swap(dt) * s


def _row_spec(tm, w, cb=0):
    return pl.BlockSpec((tm, w), lambda i: (i, cb))


def _fix_spec(shape):
    return pl.BlockSpec(shape, lambda *_: (0,) * len(shape))


def _rms_fwd(h, off, width, g, name):
    S = h.shape[0]
    tm = _tile(S, 512)

    def body(x_ref, g_ref, o_ref):
        x = x_ref[...]
        r = lax.rsqrt(jnp.mean(x * x, axis=1, keepdims=True) + RMS_EPS)
        o_ref[...] = (x * r * g_ref[...]).astype(BF16)

    return pl.pallas_call(
        body, name=name, grid=(S // tm,), in_specs=[_row_spec(tm, width, off // width), _fix_spec((1, width))],
        out_specs=_row_spec(tm, width), out_shape=jax.ShapeDtypeStruct((S, width), BF16),
        compiler_params=_params(("parallel",)))(h, g.reshape(1, width))


def _rms_bwd(h, off, width, g, dxn, name):
    S = h.shape[0]
    tm = _tile(S, 512)

    def body(x_ref, g_ref, d_ref, dx_ref, dg_ref):
        @pl.when(pl.program_id(0) == 0)
        def _():
            dg_ref[...] = jnp.zeros_like(dg_ref)

        x, d = x_ref[...], d_ref[...]
        r = lax.rsqrt(jnp.mean(x * x, axis=1, keepdims=True) + RMS_EPS)
        gd = d * g_ref[...]
        dx_ref[...] = gd * r - x * (r * r * r) * jnp.mean(gd * x, axis=1, keepdims=True)
        dg_ref[...] += jnp.sum(d * x * r, axis=0, keepdims=True)

    return pl.pallas_call(
        body, name=name, grid=(S // tm,),
        in_specs=[_row_spec(tm, width, off // width), _fix_spec((1, width)), _row_spec(tm, width)],
        out_specs=[_row_spec(tm, width), _fix_spec((1, width))],
        out_shape=[jax.ShapeDtypeStruct((S, width), F32), jax.ShapeDtypeStruct((1, width), F32)],
        compiler_params=_params(("arbitrary",)))(h, g.reshape(1, width), dxn)


def _q_proj(xn, w, rc, rs):
    S = xn.shape[0]
    tm = _tile(S, 512)

    def body(x_ref, w_ref, c_ref, s_ref, q_ref):
        q = _dot(x_ref[...], w_ref[...]) * MLA_SCALE
        q_ref[:, :LANE] = q[:, :LANE].astype(BF16)
        q_ref[:, LANE:] = _rope(q[:, LANE:], c_ref[...], s_ref[...]).astype(BF16)

    return pl.pallas_call(
        body, name="q_proj", grid=(S // tm, MLA_HEADS),
        in_specs=[pl.BlockSpec((tm, 512), lambda i, j: (i, 0)), pl.BlockSpec((512, 256), lambda i, j: (0, j)),
                  pl.BlockSpec((tm, LANE), lambda i, j: (i, 0)), pl.BlockSpec((tm, LANE), lambda i, j: (i, 0))],
        out_specs=pl.BlockSpec((tm, 256), lambda i, j: (i, j)),
        out_shape=jax.ShapeDtypeStruct((S, MLA_HEADS * 256), BF16),
        compiler_params=_params(("parallel", "parallel")))(xn, w, rc, rs)


def _kv_proj(xn, w, h, rc, rs):
    S = xn.shape[0]
    tm = _tile(S, 512)

    def body(x_ref, w_ref, pe_ref, c_ref, s_ref, k_ref, v_ref):
        kv = _dot(x_ref[...], w_ref[...])
        k_ref[:, :LANE] = kv[:, :LANE].astype(BF16)
        k_ref[:, LANE:] = _rope(pe_ref[...], c_ref[...], s_ref[...]).astype(BF16)
        v_ref[...] = kv[:, LANE:].astype(BF16)

    return pl.pallas_call(
        body, name="kv_proj", grid=(S // tm, MLA_HEADS),
        in_specs=[pl.BlockSpec((tm, 256), lambda i, j: (i, 0)), pl.BlockSpec((256, 256), lambda i, j: (0, j)),
                  pl.BlockSpec((tm, LANE), lambda i, j: (i, KPE // LANE)),
                  pl.BlockSpec((tm, LANE), lambda i, j: (i, 0)), pl.BlockSpec((tm, LANE), lambda i, j: (i, 0))],
        out_specs=[pl.BlockSpec((tm, 256), lambda i, j: (i, j)), pl.BlockSpec((tm, LANE), lambda i, j: (i, j))],
        out_shape=[jax.ShapeDtypeStruct((S, MLA_HEADS * 256), BF16), jax.ShapeDtypeStruct((S, MLA_HEADS * LANE), BF16)],
        compiler_params=_params(("parallel", "parallel")))(xn, w, h, rc, rs)


def _q_rope_bwd(dq, rc, rs):
    S = dq.shape[0]
    tm = _tile(S, 512)

    def body(d_ref, c_ref, s_ref, o_ref):
        o_ref[:, :LANE] = d_ref[:, :LANE].astype(BF16)
        o_ref[:, LANE:] = _rope_bwd(d_ref[:, LANE:], c_ref[...], s_ref[...]).astype(BF16)

    return pl.pallas_call(
        body, name="q_rope_bwd", grid=(S // tm, MLA_HEADS),
        in_specs=[pl.BlockSpec((tm, 256), lambda i, j: (i, j)),
                  pl.BlockSpec((tm, LANE), lambda i, j: (i, 0)), pl.BlockSpec((tm, LANE), lambda i, j: (i, 0))],
        out_specs=pl.BlockSpec((tm, 256), lambda i, j: (i, j)),
        out_shape=jax.ShapeDtypeStruct((S, MLA_HEADS * 256), BF16),
        compiler_params=_params(("parallel", "parallel")))(dq, rc, rs)


def _kv_bwd_prep(dk, dv, rc, rs):
    S = dk.shape[1]
    tm = _tile(S, 512)

    def body(dk_ref, dv_ref, c_ref, s_ref, o_ref, pe_ref):
        rot = jnp.zeros((tm, LANE), F32)
        for hh in range(MLA_HEADS):
            o_ref[:, hh * 256:hh * 256 + LANE] = dk_ref[hh, :, :LANE].astype(BF16)
            o_ref[:, hh * 256 + LANE:(hh + 1) * 256] = dv_ref[hh].astype(BF16)
            rot = rot + dk_ref[hh, :, LANE:]
        pe_ref[...] = _rope_bwd(rot, c_ref[...], s_ref[...])

    return pl.pallas_call(
        body, name="kv_bwd_prep", grid=(S // tm,),
        in_specs=[pl.BlockSpec((MLA_HEADS, tm, 256), lambda i: (0, i, 0)),
                  pl.BlockSpec((MLA_HEADS, tm, LANE), lambda i: (0, i, 0)), _row_spec(tm, LANE), _row_spec(tm, LANE)],
        out_specs=[_row_spec(tm, MLA_HEADS * 256), _row_spec(tm, LANE)],
        out_shape=[jax.ShapeDtypeStruct((S, MLA_HEADS * 256), BF16), jax.ShapeDtypeStruct((S, LANE), F32)],
        compiler_params=_params(("parallel",)))(dk, dv, rc, rs)


def _chunk_mask(T):
    row = lax.broadcasted_iota(jnp.int32, (T, T), 0)
    col = lax.broadcasted_iota(jnp.int32, (T, T), 1)
    return (col // CHUNK) <= (row // CHUNK)


def _att_blocks(S, q_blk=None):
    tq = min(q_blk or Q_BLK, S)
    tk = min(K_BLK, tq)
    return tq, tk, tq // tk


def _tail_masks(rows, tk):
    row = lax.broadcasted_iota(jnp.int32, (rows, tk), 0)
    col = lax.broadcasted_iota(jnp.int32, (rows, tk), 1)
    return (col // CHUNK) <= (row // CHUNK), col < row


def _span_masks(tk, r):
    row = lax.broadcasted_iota(jnp.int32, (tk, (r + 1) * tk), 0) + r * tk
    col = lax.broadcasted_iota(jnp.int32, (tk, (r + 1) * tk), 1)
    return (col // CHUNK) <= (row // CHUNK), col < row


def _put_rows(old, new, r0):
    return new if r0 == 0 else jnp.concatenate([old[:r0], new], axis=0)


def _mla_fwd(q, kp, v):
    S = q.shape[0]
    TQ, TK, n = _att_blocks(S)

    def body(q_ref, k_ref, v_ref, o_ref, lse_ref):
        i = pl.program_id(1)

        def update(carry, qb, keys, mask):
            m, l, acc = carry
            s = _dot_nt(qb, k_ref[keys, :])
            if mask is not None:
                s = jnp.where(mask, s, -1e30)
            m_new = jnp.maximum(m, jnp.max(s, axis=1, keepdims=True))
            a = jnp.exp(m - m_new)
            p = jnp.exp(s - m_new)
            return m_new, a * l + jnp.sum(p, axis=1, keepdims=True), a * acc + _dot(p.astype(BF16), v_ref[keys, :])

        carry = (jnp.full((TQ, 1), -1e30, F32), jnp.zeros((TQ, 1), F32), jnp.zeros((TQ, LANE), F32))
        carry = lax.fori_loop(
            0, i * n, lambda j, c: update(c, q_ref[...], pl.ds(pl.multiple_of(j * TK, TK), TK), None), carry)
        for r in range(n):
            rows = slice(r * TK, (r + 1) * TK)
            m, l, acc = update(tuple(c[rows] for c in carry), q_ref[rows, :],
                               pl.ds(pl.multiple_of(i * TQ, TQ), (r + 1) * TK), _span_masks(TK, r)[0])
            o_ref[rows, :] = acc / l
            lse_ref[rows, :] = jnp.broadcast_to(m + jnp.log(l), (TK, LANE))

    return pl.pallas_call(
        body, name="mla_fwd", grid=(MLA_HEADS, S // TQ),
        in_specs=[pl.BlockSpec((TQ, 256), lambda h, i: (i, h)), pl.BlockSpec((S, 256), lambda h, i: (0, h)),
                  pl.BlockSpec((S, LANE), lambda h, i: (0, h))],
        out_specs=[pl.BlockSpec((TQ, LANE), lambda h, i: (i, h)), pl.BlockSpec((TQ, LANE), lambda h, i: (i, h))],
        out_shape=[jax.ShapeDtypeStruct((S, MLA_HEADS * LANE), F32), jax.ShapeDtypeStruct((S, MLA_HEADS * LANE), F32)],
        compiler_params=_params(("parallel", "arbitrary")))(q, kp, v)


def _mla_bwd(q, kp, v, do_cat, o_cat, lse):
    S = q.shape[0]
    TQ, TK, n = _att_blocks(S)
    nq = S // TQ

    def body(q_ref, k_ref, v_ref, do_ref, o_ref, lse_ref, dq_ref, dk_hbm, dv_hbm, dk_acc, dv_acc):
        h, i = pl.program_id(0), pl.program_id(1)

        @pl.when(i == 0)
        def _():
            dk_acc[...] = jnp.zeros_like(dk_acc)
            dv_acc[...] = jnp.zeros_like(dv_acc)

        do32 = do_ref[...]
        dob = do32.astype(BF16)
        delta = jnp.sum(do32 * o_ref[...], axis=1, keepdims=True)
        lse_col = lse_ref[:, :1]

        def blk(j, dq, r0, masked):
            sl = pl.ds(pl.multiple_of(j * TK, TK), TK)
            kb, vb, qb = k_ref[sl, :], v_ref[sl, :], q_ref[r0:, :]
            s = _dot_nt(qb, kb)
            if masked:
                s = jnp.where(_tail_masks(TQ - r0, TK)[0], s, -1e30)
            p = jnp.exp(s - lse_col[r0:])
            ds = (p * (_dot_nt(dob[r0:], vb) - delta[r0:])).astype(BF16)
            dk_acc[sl, :] += _dot_tn(ds, qb)
            dv_acc[sl, :] += _dot_tn(p.astype(BF16), dob[r0:])
            return _put_rows(dq, dq[r0:] + _dot(ds, kb), r0)

        dq = lax.fori_loop(0, i * n, lambda j, c: blk(j, c, 0, False), jnp.zeros((TQ, 256), F32))
        for t in range(n):
            dq = blk(i * n + t, dq, t * TK, True)
        dq_ref[...] = dq * MLA_SCALE

        @pl.when(i == nq - 1)
        def _():
            pltpu.sync_copy(dk_acc, dk_hbm.at[h])
            pltpu.sync_copy(dv_acc, dv_hbm.at[h])

    any_spec = pl.BlockSpec(memory_space=pl.ANY)
    T = TQ
    return pl.pallas_call(
        body, name="mla_bwd", grid=(MLA_HEADS, nq),
        in_specs=[pl.BlockSpec((T, 256), lambda h, i: (i, h)), pl.BlockSpec((S, 256), lambda h, i: (0, h)),
                  pl.BlockSpec((S, LANE), lambda h, i: (0, h)), pl.BlockSpec((T, LANE), lambda h, i: (i, h)),
                  pl.BlockSpec((T, LANE), lambda h, i: (i, h)), pl.BlockSpec((T, LANE), lambda h, i: (i, h))],
        out_specs=[pl.BlockSpec((T, 256), lambda h, i: (i, h)), any_spec, any_spec],
        out_shape=[jax.ShapeDtypeStruct((S, MLA_HEADS * 256), F32), jax.ShapeDtypeStruct((MLA_HEADS, S, 256), F32),
                   jax.ShapeDtypeStruct((MLA_HEADS, S, LANE), F32)],
        scratch_shapes=[pltpu.VMEM((S, 256), F32), pltpu.VMEM((S, LANE), F32)],
        compiler_params=_params(("arbitrary", "arbitrary")))(q, kp, v, do_cat, o_cat, lse)


def _split_dot(x, tri):
    top = lax.bitcast_convert_type(lax.bitcast_convert_type(x, jnp.uint32) & jnp.uint32(0xFFFF0000), F32)
    return _dot(top.astype(BF16), tri) + _dot((x - top).astype(BF16), tri)


def _sb_block(qb, kb, tri, carry, masked):
    z = _dot_nt(qb, kb)
    lb = jnp.minimum(z, 0.0) - jnp.log(1.0 + jnp.exp(-jnp.abs(z)))
    lm = lb - z
    strict = None
    if masked:
        strict = _tail_masks(z.shape[0], z.shape[1])[1]
        lm = jnp.where(strict, lm, 0.0)
    a = jnp.exp(lb + carry + _split_dot(lm, tri))
    if masked:
        a = jnp.where(strict, a, 0.0)
    return a, lb, lm, strict


def _sb_walk(blk, j0, state):
    def alive(c):
        return jnp.logical_and(c[0] >= 0, jnp.max(c[1][0]) > SB_DEAD)

    return lax.while_loop(alive, lambda c: (c[0] - 1, blk(c[0], c[1], 0, False)), (j0, state))[1]


def _triangle(tk):
    row = lax.broadcasted_iota(jnp.int32, (tk, tk), 0)
    col = lax.broadcasted_iota(jnp.int32, (tk, tk), 1)
    return (row > col).astype(BF16)


def _sb_fwd(qkv):
    S = qkv.shape[0]
    TQ, TK, n = _att_blocks(S, SB_Q_BLK)
    T = TQ

    def body(q_ref, k_ref, v_ref, o_ref):
        i = pl.program_id(1)
        tri = _triangle(TK)

        def blk(j, state, r0, masked):
            carry, acc = (c[r0:] for c in state)
            sl = pl.ds(pl.multiple_of(j * TK, TK), TK)
            a, _, lm, _ = _sb_block(q_ref[r0:, :], k_ref[sl, :], tri, carry, masked)
            new = (carry + jnp.sum(lm, axis=1, keepdims=True), acc + _dot(a.astype(BF16), v_ref[sl, :]))
            return tuple(_put_rows(c, u, r0) for c, u in zip(state, new))

        state = (jnp.zeros((TQ, 1), F32), jnp.zeros((TQ, LANE), F32))
        for t in reversed(range(n)):
            state = blk(i * n + t, state, t * TK, True)
        state = _sb_walk(blk, i * n - 1, state)
        o_ref[...] = state[1]

    return pl.pallas_call(
        body, name="sb_fwd", grid=(SB_HEADS, S // T),
        in_specs=[pl.BlockSpec((T, LANE), lambda h, i: (i, h)), pl.BlockSpec((S, LANE), lambda h, i: (0, 4 + h)),
                  pl.BlockSpec((S, LANE), lambda h, i: (0, 8 + h))],
        out_specs=pl.BlockSpec((T, LANE), lambda h, i: (i, h)),
        out_shape=jax.ShapeDtypeStruct((S, SB_HEADS * LANE), F32),
        compiler_params=_params(("parallel", "arbitrary")))(qkv, qkv, qkv)


def _sb_bwd(qkv, do_cat, o_cat, col0):
    S = qkv.shape[0]
    TQ, TK, n = _att_blocks(S, SB_Q_BLK)
    T = TQ
    nq = S // TQ

    def body(q_ref, k_ref, v_ref, do_ref, o_ref, dq_ref, dk_hbm, dv_hbm, dk_acc, dv_acc):
        h, i = pl.program_id(0), pl.program_id(1)

        @pl.when(i == 0)
        def _():
            dk_acc[...] = jnp.zeros_like(dk_acc)
            dv_acc[...] = jnp.zeros_like(dv_acc)

        dob = do_ref[...].astype(BF16)
        tri = _triangle(TK)
        rest0 = jnp.sum(dob.astype(F32) * o_ref[...], axis=1, keepdims=True)

        def blk(j, state, r0, masked):
            carry, rest, dq = (c[r0:] for c in state)
            sl = pl.ds(pl.multiple_of(j * TK, TK), TK)
            kb, vb, qb = k_ref[sl, :], v_ref[sl, :], q_ref[r0:, :]
            a, lb, lm, strict = _sb_block(qb, kb, tri, carry, masked)
            ab = a.astype(BF16)
            e = ab.astype(F32) * _dot_nt(dob[r0:], vb)
            dz = e - jnp.exp(lb) * (rest - _split_dot(e, tri))
            if masked:
                dz = jnp.where(strict, dz, 0.0)
            dzb = dz.astype(BF16)
            dk_acc[sl, :] += _dot_tn(dzb, qb)
            dv_acc[sl, :] += _dot_tn(ab, dob[r0:])
            new = (carry + jnp.sum(lm, axis=1, keepdims=True), rest - jnp.sum(e, axis=1, keepdims=True),
                   dq + _dot(dzb, kb))
            return tuple(_put_rows(c, u, r0) for c, u in zip(state, new))

        state = (jnp.zeros((TQ, 1), F32), rest0, jnp.zeros((TQ, LANE), F32))
        for t in reversed(range(n)):
            state = blk(i * n + t, state, t * TK, True)
        state = _sb_walk(blk, i * n - 1, state)
        dq_ref[...] = state[2] * SB_SCALE

        @pl.when(i == nq - 1)
        def _():
            pltpu.sync_copy(dk_acc, dk_hbm.at[h])
            pltpu.sync_copy(dv_acc, dv_hbm.at[h])

    any_spec = pl.BlockSpec(memory_space=pl.ANY)
    return pl.pallas_call(
        body, name="sb_bwd", grid=(SB_HEADS, nq),
        in_specs=[pl.BlockSpec((T, LANE), lambda h, i: (i, h)), pl.BlockSpec((S, LANE), lambda h, i: (0, 4 + h)),
                  pl.BlockSpec((S, LANE), lambda h, i: (0, 8 + h)),
                  pl.BlockSpec((T, LANE), lambda h, i: (i, col0 + h)), pl.BlockSpec((T, LANE), lambda h, i: (i, col0 + h))],
        out_specs=[pl.BlockSpec((T, LANE), lambda h, i: (i, h)), any_spec, any_spec],
        out_shape=[jax.ShapeDtypeStruct((S, SB_HEADS * LANE), F32), jax.ShapeDtypeStruct((SB_HEADS, S, LANE), F32),
                   jax.ShapeDtypeStruct((SB_HEADS, S, LANE), F32)],
        scratch_shapes=[pltpu.VMEM((S, LANE), F32), pltpu.VMEM((S, LANE), F32)],
        compiler_params=_params(("arbitrary", "arbitrary")))(qkv, qkv, qkv, do_cat, o_cat)


def _mem_probs(q, k_ref, hh):
    lane = lax.broadcasted_iota(jnp.int32, (1, 256), 1) // 64
    msk = lane == hh
    qh = jnp.where(msk, q, 0.0).astype(BF16)
    s = _dot_nt(qh, k_ref[...]) * MEM_SCALE
    p = jnp.exp(s - jnp.max(s, axis=1, keepdims=True))
    return msk, qh, p / jnp.sum(p, axis=1, keepdims=True)


def _mem_fwd(h, mk, mv):
    S = h.shape[0]
    tm = _tile(S, 512)

    def body(q_ref, k_ref, v_ref, o_ref):
        q = q_ref[...]
        out = jnp.zeros((tm, 256), F32)
        for hh in range(MEM_HEADS):
            msk, _, p = _mem_probs(q, k_ref, hh)
            out = out + jnp.where(msk, _dot(p.astype(BF16), v_ref[...]), 0.0)
        o_ref[...] = out

    return pl.pallas_call(
        body, name="mem_fwd", grid=(S // tm,),
        in_specs=[_row_spec(tm, 256, MQ // 256), _fix_spec((256, 256)), _fix_spec((256, 256))],
        out_specs=_row_spec(tm, 256), out_shape=jax.ShapeDtypeStruct((S, 256), F32),
        compiler_params=_params(("parallel",)))(h, mk, mv)


def _mem_bwd(h, mk, mv, do_cat, col0):
    S = h.shape[0]
    tm = _tile(S, 512)

    def body(q_ref, k_ref, v_ref, do_ref, dq_ref, dk_ref, dv_ref):
        @pl.when(pl.program_id(0) == 0)
        def _():
            dk_ref[...] = jnp.zeros_like(dk_ref)
            dv_ref[...] = jnp.zeros_like(dv_ref)

        q, do = q_ref[...], do_ref[...]
        dq = jnp.zeros((tm, 256), F32)
        for hh in range(MEM_HEADS):
            msk, qh, p = _mem_probs(q, k_ref, hh)
            doh = jnp.where(msk, do, 0.0).astype(BF16)
            dp = _dot_nt(doh, v_ref[...])
            ds = (p * (dp - jnp.sum(p * dp, axis=1, keepdims=True)) * MEM_SCALE).astype(BF16)
            dq = dq + jnp.where(msk, _dot(ds, k_ref[...]), 0.0)
            dk_ref[...] += _dot_tn(ds, qh)
            dv_ref[...] += _dot_tn(p.astype(BF16), doh)
        dq_ref[...] = dq

    return pl.pallas_call(
        body, name="mem_bwd", grid=(S // tm,),
        in_specs=[_row_spec(tm, 256, MQ // 256), _fix_spec((256, 256)), _fix_spec((256, 256)),
                  _row_spec(tm, 256, col0 // 256)],
        out_specs=[_row_spec(tm, 256), _fix_spec((256, 256)), _fix_spec((256, 256))],
        out_shape=[jax.ShapeDtypeStruct((S, 256), F32), jax.ShapeDtypeStruct((256, 256), F32),
                   jax.ShapeDtypeStruct((256, 256), F32)],
        compiler_params=_params(("arbitrary",)))(h, mk, mv, do_cat)


SG_T = 128


def _sg_norm(sv, g, b):
    gv = _gelu(sv)
    xc = gv - jnp.mean(gv, axis=1, keepdims=True)
    rstd = lax.rsqrt(jnp.mean(xc * xc, axis=1, keepdims=True) + LN_EPS)
    xhat = xc * rstd
    return xhat, rstd, xhat * g + b


def _sg_fwd(h, lng, lnb, w, bias_t):
    S = h.shape[0]
    tm = _tile(S, 512)

    def body(u_ref, v_ref, g_ref, b_ref, w_ref, bias_ref, o_ref):
        mask = _chunk_mask(SG_T)
        for n in range(tm // SG_T):
            rows = slice(n * SG_T, (n + 1) * SG_T)
            u = _gelu(u_ref[rows, :])
            _, _, vn = _sg_norm(v_ref[rows, :], g_ref[...], b_ref[...])
            vb = vn.astype(BF16)
            for gi in range(4):
                cols = slice(gi * LANE, (gi + 1) * LANE)
                wg = jnp.where(mask, w_ref[gi], 0.0).astype(BF16)
                mixed = _dot(wg, vb[:, cols]) + bias_ref[:, gi:gi + 1]
                o_ref[rows, cols] = u[:, cols] * mixed

    return pl.pallas_call(
        body, name="sg_fwd", grid=(S // tm,),
        in_specs=[_row_spec(tm, 512, SGU // 512), _row_spec(tm, 512, SGV // 512), _fix_spec((1, 512)),
                  _fix_spec((1, 512)), _fix_spec((4, SG_T, SG_T)), _fix_spec((SG_T, 4))],
        out_specs=_row_spec(tm, 512), out_shape=jax.ShapeDtypeStruct((S, 512), F32),
        compiler_params=_params(("parallel",)))(h, h, lng.reshape(1, 512), lnb.reshape(1, 512), w, bias_t)


def _sg_bwd(h, lng, lnb, w, bias_t, do_cat, col0):
    S = h.shape[0]
    tm = _tile(S, 512)
    nsteps = S // tm

    def body(u_ref, v_ref, g_ref, b_ref, w_ref, bias_ref, do0_ref, do1_ref, do2_ref, do3_ref,
             du_ref, dv_ref, dw_ref, dbias_ref, dg_ref, db_ref, dvn_scr, dbias_acc):
        do_refs = (do0_ref, do1_ref, do2_ref, do3_ref)
        step = pl.program_id(0)

        @pl.when(step == 0)
        def _():
            dw_ref[...] = jnp.zeros_like(dw_ref)
            dg_ref[...] = jnp.zeros_like(dg_ref)
            db_ref[...] = jnp.zeros_like(db_ref)
            dbias_acc[...] = jnp.zeros_like(dbias_acc)

        mask = _chunk_mask(SG_T)
        for n in range(tm // SG_T):
            rows = slice(n * SG_T, (n + 1) * SG_T)
            su, sv = u_ref[rows, :], v_ref[rows, :]
            u = _gelu(su)
            xhat, rstd, vn = _sg_norm(sv, g_ref[...], b_ref[...])
            vb = vn.astype(BF16)
            ugrad = _gelu_grad(su)
            for gi in range(4):
                cols = slice(gi * LANE, (gi + 1) * LANE)
                do = do_refs[gi][rows, :]
                wg = jnp.where(mask, w_ref[gi], 0.0).astype(BF16)
                mixed = _dot(wg, vb[:, cols]) + bias_ref[:, gi:gi + 1]
                dmixed = do * u[:, cols]
                dmb = dmixed.astype(BF16)
                du_ref[rows, cols] = do * mixed * ugrad[:, cols]
                dvn_scr[:, cols] = _dot_tn(wg, dmb)
                dw_ref[gi] += jnp.where(mask, _dot_nt(dmb, vb[:, cols]), 0.0)
                dbias_acc[gi] += dmixed
            dvn = dvn_scr[...]
            dg_ref[...] += jnp.sum(dvn * xhat, axis=0, keepdims=True)
            db_ref[...] += jnp.sum(dvn, axis=0, keepdims=True)
            dxh = dvn * g_ref[...]
            dgv = rstd * (dxh - jnp.mean(dxh, axis=1, keepdims=True)
                          - xhat * jnp.mean(dxh * xhat, axis=1, keepdims=True))
            dv_ref[rows, :] = dgv * _gelu_grad(sv)

        @pl.when(step == nsteps - 1)
        def _():
            for gi in range(4):
                dbias_ref[:, gi:gi + 1] = jnp.sum(dbias_acc[gi], axis=1, keepdims=True)

    return pl.pallas_call(
        body, name="sg_bwd", grid=(nsteps,),
        in_specs=[_row_spec(tm, 512, SGU // 512), _row_spec(tm, 512, SGV // 512), _fix_spec((1, 512)),
                  _fix_spec((1, 512)), _fix_spec((4, SG_T, SG_T)), _fix_spec((SG_T, 4))]
                 + [_row_spec(tm, LANE, col0 // LANE + gi) for gi in range(4)],
        out_specs=[_row_spec(tm, 512), _row_spec(tm, 512), _fix_spec((4, SG_T, SG_T)), _fix_spec((SG_T, 4)),
                   _fix_spec((1, 512)), _fix_spec((1, 512))],
        out_shape=[jax.ShapeDtypeStruct((S, 512), F32), jax.ShapeDtypeStruct((S, 512), F32),
                   jax.ShapeDtypeStruct((4, SG_T, SG_T), F32), jax.ShapeDtypeStruct((SG_T, 4), F32),
                   jax.ShapeDtypeStruct((1, 512), F32), jax.ShapeDtypeStruct((1, 512), F32)],
        scratch_shapes=[pltpu.VMEM((SG_T, 512), F32), pltpu.VMEM((4, SG_T, SG_T), F32)],
        compiler_params=_params(("arbitrary",)))(h, h, lng.reshape(1, 512), lnb.reshape(1, 512), w, bias_t,
                                                 do_cat, do_cat, do_cat, do_cat)


def _gate_fwd(o_cat, h):
    S = h.shape[0]
    tm = _tile(S, 512)

    def body(o_ref, g_ref, y_ref):
        g = g_ref[...]
        y_ref[...] = (o_ref[...] * (g * jax.nn.sigmoid(g))).astype(BF16)

    return pl.pallas_call(
        body, name="gate_fwd", grid=(S // tm, 4),
        in_specs=[pl.BlockSpec((tm, 512), lambda i, j: (i, j)), pl.BlockSpec((tm, 512), lambda i, j: (i, GATE // 512 + j))],
        out_specs=pl.BlockSpec((tm, 512), lambda i, j: (i, j)), out_shape=jax.ShapeDtypeStruct((S, D_MODEL), BF16),
        compiler_params=_params(("parallel", "parallel")))(o_cat, h)


def _gate_bwd(dyg, o_cat, h):
    S = h.shape[0]
    tm = _tile(S, 512)

    def body(d_ref, o_ref, g_ref, do_ref, dg_ref):
        d, g = d_ref[...], g_ref[...]
        sig = jax.nn.sigmoid(g)
        do_ref[...] = d * (g * sig)
        dg_ref[...] = d * o_ref[...] * (sig * (1.0 + g * (1.0 - sig)))

    blk = pl.BlockSpec((tm, 512), lambda i, j: (i, j))
    return pl.pallas_call(
        body, name="gate_bwd", grid=(S // tm, 4),
        in_specs=[blk, blk, pl.BlockSpec((tm, 512), lambda i, j: (i, GATE // 512 + j))],
        out_specs=[blk, blk],
        out_shape=[jax.ShapeDtypeStruct((S, D_MODEL), F32), jax.ShapeDtypeStruct((S, D_MODEL), F32)],
        compiler_params=_params(("parallel", "parallel")))(dyg, o_cat, h)


def _ln_res_fwd(x, y, g, b):
    S = x.shape[0]
    tm = _tile(S, 256)

    def body(x_ref, y_ref, g_ref, b_ref, o_ref, ob_ref, r_ref):
        r = ALPHA * x_ref[...] + y_ref[...]
        r_ref[...] = r
        xc = r - jnp.mean(r, axis=1, keepdims=True)
        o = xc * lax.rsqrt(jnp.mean(xc * xc, axis=1, keepdims=True) + LN_EPS) * g_ref[...] + b_ref[...]
        o_ref[...] = o
        ob_ref[...] = o.astype(BF16)

    return pl.pallas_call(
        body, name="ln_res_fwd", grid=(S // tm,),
        in_specs=[_row_spec(tm, D_MODEL), _row_spec(tm, D_MODEL), _fix_spec((1, D_MODEL)), _fix_spec((1, D_MODEL))],
        out_specs=[_row_spec(tm, D_MODEL), _row_spec(tm, D_MODEL), _row_spec(tm, D_MODEL)],
        out_shape=[jax.ShapeDtypeStruct((S, D_MODEL), F32), jax.ShapeDtypeStruct((S, D_MODEL), BF16),
                   jax.ShapeDtypeStruct((S, D_MODEL), F32)],
        compiler_params=_params(("parallel",)))(x, y, g.reshape(1, D_MODEL), b.reshape(1, D_MODEL))


def _ln_res_bwd(dout, r, g):
    S = r.shape[0]
    tm = _tile(S, 256)

    def body(d_ref, r_ref, g_ref, dr_ref, dg_ref, db_ref):
        @pl.when(pl.program_id(0) == 0)
        def _():
            dg_ref[...] = jnp.zeros_like(dg_ref)
            db_ref[...] = jnp.zeros_like(db_ref)

        d, r = d_ref[...], r_ref[...]
        xc = r - jnp.mean(r, axis=1, keepdims=True)
        rstd = lax.rsqrt(jnp.mean(xc * xc, axis=1, keepdims=True) + LN_EPS)
        xhat = xc * rstd
        dxh = d * g_ref[...]
        dr_ref[...] = rstd * (dxh - jnp.mean(dxh, axis=1, keepdims=True)
                              - xhat * jnp.mean(dxh * xhat, axis=1, keepdims=True))
        dg_ref[...] += jnp.sum(d * xhat, axis=0, keepdims=True)
        db_ref[...] += jnp.sum(d, axis=0, keepdims=True)

    return pl.pallas_call(
        body, name="ln_res_bwd", grid=(S // tm,),
        in_specs=[_row_spec(tm, D_MODEL), _row_spec(tm, D_MODEL), _fix_spec((1, D_MODEL))],
        out_specs=[_row_spec(tm, D_MODEL), _fix_spec((1, D_MODEL)), _fix_spec((1, D_MODEL))],
        out_shape=[jax.ShapeDtypeStruct((S, D_MODEL), F32), jax.ShapeDtypeStruct((1, D_MODEL), F32),
                   jax.ShapeDtypeStruct((1, D_MODEL), F32)],
        compiler_params=_params(("arbitrary",)))(dout, r, g.reshape(1, D_MODEL))


def _loss_head(y, target):
    S = y.shape[0]
    tm = _tile(S, 256)

    def body(y_ref, t_ref, l_ref, d_ref):
        @pl.when(pl.program_id(0) == 0)
        def _():
            l_ref[...] = jnp.zeros_like(l_ref)

        diff = y_ref[...] - t_ref[...]
        d_ref[...] = diff * (1.0 / D_MODEL)
        per_row = jnp.mean(diff * diff, axis=1, keepdims=True)
        l_ref[...] += 0.5 * jnp.sum(per_row, axis=0, keepdims=True)

    return pl.pallas_call(
        body, name="loss_head", grid=(S // tm,), in_specs=[_row_spec(tm, D_MODEL), _row_spec(tm, D_MODEL)],
        out_specs=[_fix_spec((8, LANE)), _row_spec(tm, D_MODEL)],
        out_shape=[jax.ShapeDtypeStruct((8, LANE), F32), jax.ShapeDtypeStruct((S, D_MODEL), F32)],
        compiler_params=_params(("arbitrary",)))(y, target)


def _perm_table():
    table, at = [], 0
    for name in PERM_ORDER:
        start, width = ORIG[name]
        table.append((name, start, width, at))
        at += width
    return table


def _permute_w_in(w):
    parts = [w[..., start:start + width] for _, start, width, _ in _perm_table()]
    return jnp.concatenate(parts + [jnp.zeros(w.shape[:-1] + (HP - D_IN,), w.dtype)], axis=-1)


def _unpermute_w_in(wp):
    parts = sorted(_perm_table(), key=lambda t: t[1])
    return jnp.concatenate([wp[..., at:at + width] for _, _, width, at in parts], axis=-1)


def _rope_tables(positions):
    inv_freq = ROPE_THETA ** (-jnp.arange(0, 64, 2, dtype=F32) / 64)
    ang = positions.astype(F32)[:, None] * inv_freq[None, :]
    cos, sin, zero = jnp.cos(ang), jnp.sin(ang), jnp.zeros((positions.shape[0], 64), F32)
    return jnp.concatenate([cos, cos, zero], axis=1), jnp.concatenate([-sin, sin, zero], axis=1)


def _local_step(x, mem, positions, target, w):
    rc, rs = _rope_tables(positions)
    mem_b = mem.astype(BF16)
    xb = x.astype(BF16)
    w_in_all = _permute_w_in(w["w_in"])
    w_uq_all = jnp.pad(w["w_uq"].reshape(DEPTH, 512, MLA_HEADS, 192),
                       ((0, 0), (0, 0), (0, 0), (0, 64))).reshape(DEPTH, 512, MLA_HEADS * 256)
    saved = []
    for l in range(DEPTH):
        w_in, w_uq, w_ukv = w_in_all[l], w_uq_all[l], w["w_ukv"][l]
        h = _mm(xb, w_in, tm=1024, tn=1152, tk=2048, name="in_proj")
        cq_n = _rms_fwd(h, CQ, 512, w["q_norm_g"][l], "rms_q")
        ckv_n = _rms_fwd(h, CKV, 256, w["kv_norm_g"][l], "rms_kv")
        q = _q_proj(cq_n, w_uq, rc, rs)
        kp, v = _kv_proj(ckv_n, w_ukv, h, rc, rs)
        o_a, lse = _mla_fwd(q, kp, v)
        bias_t = w["sg_b"][l].T
        o_b = _sg_fwd(h, w["sg_ln_g"][l], w["sg_ln_b"][l], w["sg_w"][l], bias_t)
        qkv = jnp.concatenate([h[:, SBQ:SBQ + 512] * SB_SCALE, h[:, SBQ + 512:SBQ + 1536]], axis=1).astype(BF16)
        o_c = _sb_fwd(qkv)
        mk = _mm(mem_b, w["w_mem_k"][l], out_dtype=BF16, name="mem_kv")
        mv = _mm(mem_b, w["w_mem_v"][l], out_dtype=BF16, name="mem_kv")
        o_m = _mem_fwd(h, mk, mv)
        o_cat = jnp.concatenate([o_a, o_b, o_c, o_m], axis=1)
        yg = _gate_fwd(o_cat, h)
        y = _mm(yg, w["w_out"][l], tm=1024, tn=1024, tk=2048, name="out_proj")
        x_new, xb_new, r = _ln_res_fwd(x, y, w["ln_g"][l], w["ln_b"][l])
        saved.append(dict(xb=xb, h=h, cq_n=cq_n, ckv_n=ckv_n, q=q, kp=kp, v=v, lse=lse, qkv=qkv, mk=mk, mv=mv,
                          o_cat=o_cat, yg=yg, r=r, w_in=w_in, w_uq=w_uq, w_ukv=w_ukv, bias_t=bias_t))
        x, xb = x_new, xb_new

    loss, dx = _loss_head(x, target)

    grads = {n: [None] * DEPTH for n in SHARDED + SMALL}
    for l in reversed(range(DEPTH)):
        s = saved[l]
        h = s["h"]
        dr, dlg, dlb = _ln_res_bwd(dx, s["r"], w["ln_g"][l])
        grads["ln_g"][l], grads["ln_b"][l] = dlg[0], dlb[0]
        grads["w_out"][l] = _mm(s["yg"], dr, ta=True, tm=1024, tn=1024, tk=2048, name="dw_out")
        dyg = _mm(dr, w["w_out"][l], tb=True, tm=1024, tn=1024, tk=2048, name="d_out_proj")
        do_cat, dgates = _gate_bwd(dyg, s["o_cat"], h)
        dmq, dmk, dmv = _mem_bwd(h, s["mk"], s["mv"], do_cat, 1792)
        grads["w_mem_k"][l] = _mm(mem_b, dmk, ta=True, name="dw_mem")
        grads["w_mem_v"][l] = _mm(mem_b, dmv, ta=True, name="dw_mem")
        dsq, dsk, dsv = _sb_bwd(s["qkv"], do_cat, s["o_cat"], 1280 // LANE)
        dsk = dsk.transpose(1, 0, 2).reshape(-1, 512)
        dsv = dsv.transpose(1, 0, 2).reshape(-1, 512)
        du, dv, dsgw, dsgb, dsg_g, dsg_b = _sg_bwd(h, w["sg_ln_g"][l], w["sg_ln_b"][l], w["sg_w"][l], s["bias_t"],
                                                   do_cat, 768)
        grads["sg_w"][l], grads["sg_b"][l] = dsgw, dsgb.T
        grads["sg_ln_g"][l], grads["sg_ln_b"][l] = dsg_g[0], dsg_b[0]
        dq, dk, dvv = _mla_bwd(s["q"], s["kp"], s["v"], do_cat, s["o_cat"], s["lse"])
        dq_raw = _q_rope_bwd(dq, rc, rs)
        dkv, dkpe = _kv_bwd_prep(dk, dvv, rc, rs)
        dw_uq = _mm(s["cq_n"], dq_raw, ta=True, tk=1024, name="dw_uq")
        grads["w_uq"][l] = dw_uq.reshape(512, MLA_HEADS, 256)[:, :, :192].reshape(512, MLA_HEADS * 192)
        grads["w_ukv"][l] = _mm(s["ckv_n"], dkv, ta=True, tk=1024, name="dw_ukv")
        dcq_n = _mm(dq_raw, s["w_uq"], tb=True, name="d_cq")
        dckv_n = _mm(dkv, s["w_ukv"], tb=True, name="d_ckv")
        dcq, dqg = _rms_bwd(h, CQ, 512, w["q_norm_g"][l], dcq_n, "rms_q_bwd")
        dckv, dkvg = _rms_bwd(h, CKV, 256, w["kv_norm_g"][l], dckv_n, "rms_kv_bwd")
        grads["q_norm_g"][l], grads["kv_norm_g"][l] = dqg[0], dkvg[0]
        dh = jnp.concatenate([dcq, dckv, dmq, du, dv, dsq, dsk, dsv, dgates, dkpe], axis=1).astype(BF16)
        dw_in = _mm(s["xb"], dh, ta=True, tm=1024, tn=1152, tk=2048, name="dw_in")
        grads["w_in"][l] = _unpermute_w_in(dw_in)
        dx = _mm(dh, s["w_in"], tb=True, add=dr, add_scale=ALPHA, tm=1024, tn=1024, tk=1920, name="d_in_proj")

    return loss, dx, grads


MESH = pl.DeviceIdType.MESH
HBM_SPEC = pl.BlockSpec(memory_space=pltpu.HBM)


def _place():
    x, y, c = lax.axis_index("x"), lax.axis_index("y"), lax.axis_index("c")
    return x, y, c, [(1 - x, y), (x, 1 - y), (1 - x, 1 - y)]


def _gather_weights(flat):
    R = flat.shape[0]
    H = R // 2

    def body(src, out, send_sems, recv_sems):
        x, y, c, chips = _place()
        mine = pl.ds(c * H, H)
        theirs = pl.ds((1 - c) * H, H)

        def copy(k, src_ref, chip, rows, to):
            return pltpu.make_async_remote_copy(src_ref=src_ref, dst_ref=out.at[chip, rows, :], send_sem=send_sems.at[k],
                                                recv_sem=recv_sems.at[k], device_id=to, device_id_type=MESH)

        sent = [copy(j, src.at[mine, :], 2 * x + y, mine, (px, py, c)) for j, (px, py) in enumerate(chips)]
        for cp in sent:
            cp.start()
        passed = []
        for j, (px, py) in enumerate(chips):
            copy(j, src.at[mine, :], 2 * px + py, mine, (px, py, c)).wait_recv()
            cp = copy(3 + j, out.at[2 * px + py, mine, :], 2 * px + py, mine, (x, y, 1 - c))
            cp.start()
            passed.append(cp)
        for j, (px, py) in enumerate(chips):
            copy(3 + j, src.at[theirs, :], 2 * px + py, theirs, (x, y, 1 - c)).wait_recv()
        for cp in sent + passed:
            cp.wait_send()

    return pl.pallas_call(
        body, name="gather_weights", in_specs=[HBM_SPEC], out_specs=HBM_SPEC,
        out_shape=jax.ShapeDtypeStruct((4, R, FLAT_W), flat.dtype),
        scratch_shapes=[pltpu.SemaphoreType.DMA((6,)), pltpu.SemaphoreType.DMA((6,))],
        compiler_params=pltpu.CompilerParams(has_side_effects=True))(flat)


def _swap_halves(g):
    _, _, H, W = g.shape

    def body(src, out, send_sem, recv_sem):
        x, y, c, _ = _place()
        cp = pltpu.make_async_remote_copy(src_ref=src.at[:, 1 - c], dst_ref=out, send_sem=send_sem, recv_sem=recv_sem,
                                          device_id=(x, y, 1 - c), device_id_type=MESH)
        cp.start()
        cp.wait()

    return pl.pallas_call(
        body, name="swap_halves", in_specs=[HBM_SPEC], out_specs=HBM_SPEC,
        out_shape=jax.ShapeDtypeStruct((4, H, W), g.dtype),
        scratch_shapes=[pltpu.SemaphoreType.DMA(()), pltpu.SemaphoreType.DMA(())],
        compiler_params=pltpu.CompilerParams(has_side_effects=True))(g)


def _pair_sum(g, other, c):
    _, _, H, W = g.shape
    th = _row_tile(H, 3 * W * 4)

    def body(c_ref, a_ref, b_ref, o_ref):
        o_ref[...] = (a_ref[...] + b_ref[...]).astype(BF16)

    return pl.pallas_call(
        body, name="pair_sum",
        grid_spec=pltpu.PrefetchScalarGridSpec(
            num_scalar_prefetch=1, grid=(4, H // th),
            in_specs=[pl.BlockSpec((None, None, th, W), lambda d, i, c_ref: (d, c_ref[0], i, 0)),
                      pl.BlockSpec((None, th, W), lambda d, i, c_ref: (d, i, 0))],
            out_specs=pl.BlockSpec((None, th, W), lambda d, i, c_ref: (d, i, 0))),
        out_shape=jax.ShapeDtypeStruct((4, H, W), BF16),
        compiler_params=_params(("parallel", "parallel")))(c, g, other)


def _exchange_chips(p):
    _, H, W = p.shape

    def body(src, out, send_sems, recv_sems):
        x, y, c, chips = _place()
        sent = []
        for j, (px, py) in enumerate(chips):
            cp = pltpu.make_async_remote_copy(src_ref=src.at[2 * px + py], dst_ref=out.at[j], send_sem=send_sems.at[j],
                                              recv_sem=recv_sems.at[j], device_id=(px, py, c), device_id_type=MESH)
            cp.start()
            sent.append(cp)
        for cp in sent:
            cp.wait()

    return pl.pallas_call(
        body, name="exchange_chips", in_specs=[HBM_SPEC], out_specs=HBM_SPEC,
        out_shape=jax.ShapeDtypeStruct((3, H, W), p.dtype),
        scratch_shapes=[pltpu.SemaphoreType.DMA((3,)), pltpu.SemaphoreType.DMA((3,))],
        compiler_params=pltpu.CompilerParams(has_side_effects=True))(p)


def _chip_sum(p, got, me):
    _, H, W = p.shape
    th = _row_tile(H, 4 * W * 4)

    def body(me_ref, p_ref, g_ref, o_ref):
        acc = p_ref[...].astype(F32)
        for k in range(3):
            acc = acc + g_ref[k].astype(F32)
        o_ref[...] = acc

    return pl.pallas_call(
        body, name="chip_sum",
        grid_spec=pltpu.PrefetchScalarGridSpec(
            num_scalar_prefetch=1, grid=(H // th,),
            in_specs=[pl.BlockSpec((None, th, W), lambda i, me_ref: (me_ref[0], i, 0)),
                      pl.BlockSpec((3, th, W), lambda i, me_ref: (0, i, 0))],
            out_specs=pl.BlockSpec((th, W), lambda i, me_ref: (i, 0))),
        out_shape=jax.ShapeDtypeStruct((H, W), F32), compiler_params=_params(("parallel",)))(me, p, got)


def _sum_parts(t, name):
    n, H, W = t.shape
    th = _row_tile(H, (n + 1) * W * 4)

    def body(t_ref, o_ref):
        acc = t_ref[0]
        for k in range(1, n):
            acc = acc + t_ref[k]
        o_ref[...] = acc

    return pl.pallas_call(
        body, name=name, grid=(H // th,), in_specs=[pl.BlockSpec((n, th, W), lambda i: (0, i, 0))],
        out_specs=pl.BlockSpec((th, W), lambda i: (i, 0)), out_shape=jax.ShapeDtypeStruct((H, W), F32),
        compiler_params=_params(("parallel",)))(t)


def _share_with_sibling(half):
    H, W = half.shape

    def body(src, out, send_sem, recv_sem):
        x, y, c, _ = _place()
        cp = pltpu.make_async_remote_copy(src_ref=src, dst_ref=out.at[c], send_sem=send_sem, recv_sem=recv_sem,
                                          device_id=(x, y, 1 - c), device_id_type=MESH)
        cp.start()
        pltpu.make_async_remote_copy(src_ref=src, dst_ref=out.at[1 - c], send_sem=send_sem, recv_sem=recv_sem,
                                     device_id=(x, y, 1 - c), device_id_type=MESH).wait_recv()
        cp.wait_send()

    return pl.pallas_call(
        body, name="share_with_sibling", in_specs=[HBM_SPEC], out_specs=HBM_SPEC,
        out_shape=jax.ShapeDtypeStruct((2, H, W), half.dtype),
        scratch_shapes=[pltpu.SemaphoreType.DMA(()), pltpu.SemaphoreType.DMA(())],
        compiler_params=pltpu.CompilerParams(has_side_effects=True))(half)


def _gather_all(v):
    n, W = v.shape

    def body(src, out, send_sems, recv_sems, own_sem):
        x, y, c, _ = _place()
        own = pltpu.make_async_copy(src, out.at[4 * x + 2 * y + c], own_sem)
        own.start()
        flips = [(fx, fy, fc) for fx in (0, 1) for fy in (0, 1) for fc in (0, 1)][1:]
        sent = []
        for k, (fx, fy, fc) in enumerate(flips):
            cp = pltpu.make_async_remote_copy(
                src_ref=src, dst_ref=out.at[4 * x + 2 * y + c], send_sem=send_sems.at[k], recv_sem=recv_sems.at[k],
                device_id=(x ^ fx, y ^ fy, c ^ fc), device_id_type=MESH)
            cp.start()
            sent.append(cp)
        for k, (fx, fy, fc) in enumerate(flips):
            pltpu.make_async_remote_copy(
                src_ref=src, dst_ref=out.at[4 * (x ^ fx) + 2 * (y ^ fy) + (c ^ fc)], send_sem=send_sems.at[k],
                recv_sem=recv_sems.at[k], device_id=(x ^ fx, y ^ fy, c ^ fc), device_id_type=MESH).wait_recv()
        for cp in sent:
            cp.wait_send()
        own.wait()

    return pl.pallas_call(
        body, name="gather_all", in_specs=[HBM_SPEC], out_specs=HBM_SPEC,
        out_shape=jax.ShapeDtypeStruct((8, n, W), v.dtype),
        scratch_shapes=[pltpu.SemaphoreType.DMA((7,)), pltpu.SemaphoreType.DMA((7,)), pltpu.SemaphoreType.DMA(())],
        compiler_params=pltpu.CompilerParams(has_side_effects=True))(v)


def _adamw(w, g, m, v):
    shape = w.shape
    cols = shape[-1]
    w2, g2, m2, v2 = (a.reshape(-1, cols) for a in (w, g, m, v))
    rows = w2.shape[0]
    tr = next((t for t in (1024, 512, 256, 128, 64, 32, 16, 8) if rows % t == 0 and t * cols * 4 <= (2 << 20)), rows)

    def body(w_ref, g_ref, m_ref, v_ref, d_ref, nm_ref, nv_ref):
        g_ = g_ref[...]
        nm = ADAM_B1 * m_ref[...] + (1.0 - ADAM_B1) * g_
        nv = ADAM_B2 * v_ref[...] + (1.0 - ADAM_B2) * (g_ * g_)
        m_hat = nm / (1.0 - ADAM_B1 ** ADAM_STEP)
        v_hat = nv / (1.0 - ADAM_B2 ** ADAM_STEP)
        d_ref[...] = -ADAM_LR * (m_hat / (jnp.sqrt(v_hat) + ADAM_EPS) + ADAM_WD * w_ref[...])
        nm_ref[...] = nm
        nv_ref[...] = nv

    blk = pl.BlockSpec((tr, cols), lambda i: (i, 0))
    outs = pl.pallas_call(
        body, name="adamw", grid=(rows // tr,), in_specs=[blk] * 4, out_specs=[blk] * 3,
        out_shape=[jax.ShapeDtypeStruct((rows, cols), F32)] * 3, compiler_params=_params(("parallel",)))(w2, g2, m2, v2)
    return tuple(o.reshape(shape) for o in outs)


def _chip_part(name, a, k):
    n = a.shape[1 if name in ("w_in", "w_uq", "w_ukv") else 0] // 4
    return a[:, k * n:(k + 1) * n] if name in ("w_in", "w_uq", "w_ukv") else a[k * n:(k + 1) * n]


def _join_chips(name, parts):
    return jnp.concatenate(parts, axis=2 if name in ("w_in", "w_uq", "w_ukv") else 1)


def kernel(x, mem, positions, w_in, q_norm_g, w_uq, kv_norm_g, w_ukv, sg_ln_g, sg_ln_b, sg_w, sg_b, w_mem_k, w_mem_v, w_out, ln_g, ln_b, loss_target, m_w_in, m_q_norm_g, m_w_uq, m_kv_norm_g, m_w_ukv, m_sg_ln_g, m_sg_ln_b, m_sg_w, m_sg_b, m_w_mem_k, m_w_mem_v, m_w_out, m_ln_g, m_ln_b, v_w_in, v_q_norm_g, v_w_uq, v_kv_norm_g, v_w_ukv, v_sg_ln_g, v_sg_ln_b, v_sg_w, v_sg_b, v_w_mem_k, v_w_mem_v, v_w_out, v_ln_g, v_ln_b):
    weights = dict(w_in=w_in, q_norm_g=q_norm_g, w_uq=w_uq, kv_norm_g=kv_norm_g, w_ukv=w_ukv, sg_ln_g=sg_ln_g,
                   sg_ln_b=sg_ln_b, sg_w=sg_w, sg_b=sg_b, w_mem_k=w_mem_k, w_mem_v=w_mem_v, w_out=w_out, ln_g=ln_g, ln_b=ln_b)
    mom_m = dict(w_in=m_w_in, q_norm_g=m_q_norm_g, w_uq=m_w_uq, kv_norm_g=m_kv_norm_g, w_ukv=m_w_ukv, sg_ln_g=m_sg_ln_g,
                 sg_ln_b=m_sg_ln_b, sg_w=m_sg_w, sg_b=m_sg_b, w_mem_k=m_w_mem_k, w_mem_v=m_w_mem_v, w_out=m_w_out,
                 ln_g=m_ln_g, ln_b=m_ln_b)
    mom_v = dict(w_in=v_w_in, q_norm_g=v_q_norm_g, w_uq=v_w_uq, kv_norm_g=v_kv_norm_g, w_ukv=v_w_ukv, sg_ln_g=v_sg_ln_g,
                 sg_ln_b=v_sg_ln_b, sg_w=v_sg_w, sg_b=v_sg_b, w_mem_k=v_w_mem_k, w_mem_v=v_w_mem_v, w_out=v_w_out,
                 ln_g=v_ln_g, ln_b=v_ln_b)
    c_idx = lax.axis_index("c").astype(jnp.int32).reshape(1)

    sizes = [weights[n].size for n in SHARDED]
    flat = jnp.concatenate([weights[n].astype(BF16).reshape(-1) for n in SHARDED]).reshape(-1, FLAT_W)
    me = 2 * lax.axis_index("x") + lax.axis_index("y")
    gathered = lax.dynamic_update_slice(_gather_weights(flat), flat[None], (me, 0, 0))
    full = dict((n, weights[n]) for n in SMALL)
    at = 0
    for n, size in zip(SHARDED, sizes):
        rows = size // FLAT_W
        full[n] = _join_chips(n, [gathered[k, at:at + rows].reshape(weights[n].shape) for k in range(4)])
        at += rows
    R = at

    loss_dev, grad_x, grads = _local_step(x[0], mem[0], positions[0], loss_target[0], full)

    g4 = jnp.concatenate([_chip_part(n, g, k).reshape(-1) for k in range(4) for n in SHARDED for g in grads[n]])
    g4 = g4.reshape(4, 2, R // 2, FLAT_W)
    pair = _pair_sum(g4, _swap_halves(g4), c_idx)
    half = _chip_sum(pair, _exchange_chips(pair), me.astype(jnp.int32).reshape(1))
    reduced = lax.dynamic_update_slice(_share_with_sibling(half), half[None], (c_idx[0], 0, 0)).reshape(R, FLAT_W)
    grad_out = {}
    at = 0
    for n, size in zip(SHARDED, sizes):
        rows = size // FLAT_W
        grad_out[n] = reduced[at:at + rows].reshape(weights[n].shape)
        at += rows

    small_sizes = [weights[n].size for n in SMALL]
    vec = jnp.concatenate([g.reshape(-1) for n in SMALL for g in grads[n]] + [loss_dev[0]])
    n_small = vec.shape[0]
    rows_small = -(-n_small // (8 * FLAT_W)) * 8
    vec = jnp.pad(vec, (0, rows_small * FLAT_W - n_small)).reshape(rows_small, FLAT_W)
    total = _sum_parts(_gather_all(vec), "device_sum").reshape(-1)
    at = 0
    for n, size in zip(SMALL, small_sizes):
        grad_out[n] = total[at:at + size].reshape(weights[n].shape)
        at += size
    loss = total[at]

    names = list(weights)
    upd = {n: _adamw(weights[n], grad_out[n], mom_m[n], mom_v[n]) for n in names}
    return (loss, grad_x[None], *[grad_out[n] for n in names], *[upd[n][0] for n in names],
            *[upd[n][1] for n in names], *[upd[n][2] for n in names])
```

```python
import math

import jax
import jax.numpy as jnp
from jax import lax
from jax.experimental import pallas as pl
from jax.experimental.pallas import tpu as pltpu

F32, BF16 = jnp.float32, jnp.bfloat16

D_MODEL = 2048
DEPTH = 4
CHUNK = 64
MLA_HEADS = 6
MLA_SCALE = 1.0 / math.sqrt(192.0)
SB_HEADS = 4
SB_SCALE = 1.0 / math.sqrt(128.0)
MEM_HEADS = 4
MEM_SCALE = 1.0 / math.sqrt(64.0)
ROPE_THETA = 10000.0
ALPHA = (2.0 * DEPTH) ** 0.25
LN_EPS = 1e-5
RMS_EPS = 1e-6
ADAM_LR, ADAM_B1, ADAM_B2, ADAM_EPS, ADAM_WD, ADAM_STEP = 0.001, 0.9, 0.999, 1e-08, 0.01, 10

ORIG = dict(c_q=(0, 512), c_kv=(512, 256), k_pe=(768, 64), g_a=(832, 768), sg_u=(1600, 512), sg_v=(2112, 512),
            g_b=(2624, 512), sb_q=(3136, 512), sb_k=(3648, 512), sb_v=(4160, 512), g_c=(4672, 512),
            m_q=(5184, 256), g_m=(5440, 256))
D_IN = 5696
PERM_ORDER = ("c_q", "c_kv", "m_q", "sg_u", "sg_v", "sb_q", "sb_k", "sb_v", "g_a", "g_b", "g_c", "g_m", "k_pe")
HP = 5760
CQ, CKV, MQ, SGU, SGV, SBQ, GATE, KPE = 0, 512, 768, 1024, 1536, 2048, 3584, 5632

Q_BLK = 2048
K_BLK = 256
SB_Q_BLK = 512
SB_DEAD = -110.0
LANE = 128
VMEM_LIMIT = 56 * 1024 * 1024

FLAT_W = 1024
SHARDED = ("w_in", "w_uq", "w_ukv", "w_mem_k", "w_mem_v", "w_out")
SMALL = ("q_norm_g", "kv_norm_g", "sg_ln_g", "sg_ln_b", "sg_w", "sg_b", "ln_g", "ln_b")


def _params(sem=None):
    return pltpu.CompilerParams(dimension_semantics=sem, vmem_limit_bytes=VMEM_LIMIT)


def _tile(dim, pref):
    if dim <= pref:
        return dim
    t = (pref // LANE) * LANE
    while t >= LANE:
        if dim % t == 0:
            return t
        t -= LANE
    return dim


def _row_tile(rows, bytes_per_row, budget=8 << 20):
    best = None
    for t in range(8, rows + 1, 8):
        if rows % t == 0 and t * bytes_per_row <= budget:
            best = t
    return best if best else rows


def _dot_nt(a, b):
    return lax.dot_general(a, b, (((1,), (1,)), ((), ())), preferred_element_type=F32)


def _dot_tn(a, b):
    return lax.dot_general(a, b, (((0,), (0,)), ((), ())), preferred_element_type=F32)


def _dot(a, b):
    return jnp.dot(a, b, preferred_element_type=F32)


def _mm(a, b, *, ta=False, tb=False, a_win=None, b_win=None, add=None, add_scale=1.0, out_dtype=F32,
        tm=512, tn=512, tk=512, name="mm"):
    a_off, a_w = a_win if a_win else (0, a.shape[1])
    b_off, b_w = b_win if b_win else (0, b.shape[1])
    (K, M) = (a.shape[0], a_w) if ta else (a_w, a.shape[0])
    (N, Kb) = (b.shape[0], b_w) if tb else (b_w, b.shape[0])
    assert K == Kb, (a.shape, b.shape, ta, tb)
    tm, tn, tk = _tile(M, tm), _tile(N, tn), _tile(K, tk)
    nk = K // tk
    if ta:
        assert a_off % tm == 0
        a_spec = pl.BlockSpec((tk, tm), lambda i, j, k: (k, i + a_off // tm))
    else:
        assert a_off % tk == 0
        a_spec = pl.BlockSpec((tm, tk), lambda i, j, k: (i, k + a_off // tk))
    if tb:
        assert b_off % tk == 0
        b_spec = pl.BlockSpec((tn, tk), lambda i, j, k: (j, k + b_off // tk))
    else:
        assert b_off % tn == 0
        b_spec = pl.BlockSpec((tk, tn), lambda i, j, k: (k, j + b_off // tn))
    o_spec = pl.BlockSpec((tm, tn), lambda i, j, k: (i, j))
    dn = (((0 if ta else 1,), (1 if tb else 0,)), ((), ()))
    has_add = add is not None

    def body(*refs):
        a_ref, b_ref = refs[:2]
        add_ref = refs[2] if has_add else None
        o_ref = refs[3 if has_add else 2]
        part = lax.dot_general(a_ref[...].astype(BF16), b_ref[...].astype(BF16), dn, preferred_element_type=F32)

        def finish(r):
            if has_add:
                r = r + add_scale * add_ref[...]
            o_ref[...] = r.astype(o_ref.dtype)

        if nk == 1:
            finish(part)
            return
        acc_ref = refs[-1]
        k = pl.program_id(2)

        @pl.when(k == 0)
        def _():
            acc_ref[...] = part

        @pl.when(k > 0)
        def _():
            acc_ref[...] += part

        @pl.when(k == nk - 1)
        def _():
            finish(acc_ref[...])

    ins = [a, b] + ([add] if has_add else [])
    specs = [a_spec, b_spec] + ([o_spec] if has_add else [])
    return pl.pallas_call(
        body, name=name, grid=(M // tm, N // tn, nk), in_specs=specs, out_specs=o_spec,
        out_shape=jax.ShapeDtypeStruct((M, N), out_dtype),
        scratch_shapes=[pltpu.VMEM((tm, tn), F32)] if nk > 1 else [],
        compiler_params=_params(("parallel", "parallel", "arbitrary")))(*ins)


GELU_K = math.sqrt(2.0 / math.pi)


def _gelu(x):
    t = jnp.tanh(GELU_K * (x + 0.044715 * (x * x * x)))
    return 0.5 * x * (1.0 + t)


def _gelu_grad(x):
    t = jnp.tanh(GELU_K * (x + 0.044715 * (x * x * x)))
    return 0.5 * (1.0 + t) + 0.5 * x * (1.0 - t * t) * GELU_K * (1.0 + 3.0 * 0.044715 * x * x)


def _rope_swap(t):
    lane = lax.broadcasted_iota(jnp.int32, t.shape, 1)
    return jnp.where(lane < 32, pltpu.roll(t, 96, axis=1), pltpu.roll(t, 32, axis=1))


def _rope(t, c, s):
    return t * c + _rope_swap(t) * s


def _rope_bwd(dt, c, s):
    return dt * c - _rope_swap(dt) * s


def _row_spec(tm, w, cb=0):
    return pl.BlockSpec((tm, w), lambda i: (i, cb))


def _fix_spec(shape):
    return pl.BlockSpec(shape, lambda *_: (0,) * len(shape))


def _rms_fwd(h, off, width, g, name):
    S = h.shape[0]
    tm = _tile(S, 512)

    def body(x_ref, g_ref, o_ref):
        x = x_ref[...]
        r = lax.rsqrt(jnp.mean(x * x, axis=1, keepdims=True) + RMS_EPS)
        o_ref[...] = (x * r * g_ref[...]).astype(BF16)

    return pl.pallas_call(
        body, name=name, grid=(S // tm,), in_specs=[_row_spec(tm, width, off // width), _fix_spec((1, width))],
        out_specs=_row_spec(tm, width), out_shape=jax.ShapeDtypeStruct((S, width), BF16),
        compiler_params=_params(("parallel",)))(h, g.reshape(1, width))


def _rms_bwd(h, off, width, g, dxn, name):
    S = h.shape[0]
    tm = _tile(S, 512)

    def body(x_ref, g_ref, d_ref, dx_ref, dg_ref):
        @pl.when(pl.program_id(0) == 0)
        def _():
            dg_ref[...] = jnp.zeros_like(dg_ref)

        x, d = x_ref[...], d_ref[...]
        r = lax.rsqrt(jnp.mean(x * x, axis=1, keepdims=True) + RMS_EPS)
        gd = d * g_ref[...]
        dx_ref[...] = gd * r - x * (r * r * r) * jnp.mean(gd * x, axis=1, keepdims=True)
        dg_ref[...] += jnp.sum(d * x * r, axis=0, keepdims=True)

    return pl.pallas_call(
        body, name=name, grid=(S // tm,),
        in_specs=[_row_spec(tm, width, off // width), _fix_spec((1, width)), _row_spec(tm, width)],
        out_specs=[_row_spec(tm, width), _fix_spec((1, width))],
        out_shape=[jax.ShapeDtypeStruct((S, width), F32), jax.ShapeDtypeStruct((1, width), F32)],
        compiler_params=_params(("arbitrary",)))(h, g.reshape(1, width), dxn)


def _q_proj(xn, w, rc, rs):
    S = xn.shape[0]
    tm = _tile(S, 512)

    def body(x_ref, w_ref, c_ref, s_ref, q_ref):
        q = _dot(x_ref[...], w_ref[...]) * MLA_SCALE
        q_ref[:, :LANE] = q[:, :LANE].astype(BF16)
        q_ref[:, LANE:] = _rope(q[:, LANE:], c_ref[...], s_ref[...]).astype(BF16)

    return pl.pallas_call(
        body, name="q_proj", grid=(S // tm, MLA_HEADS),
        in_specs=[pl.BlockSpec((tm, 512), lambda i, j: (i, 0)), pl.BlockSpec((512, 256), lambda i, j: (0, j)),
                  pl.BlockSpec((tm, LANE), lambda i, j: (i, 0)), pl.BlockSpec((tm, LANE), lambda i, j: (i, 0))],
        out_specs=pl.BlockSpec((tm, 256), lambda i, j: (i, j)),
        out_shape=jax.ShapeDtypeStruct((S, MLA_HEADS * 256), BF16),
        compiler_params=_params(("parallel", "parallel")))(xn, w, rc, rs)


def _kv_proj(xn, w, h, rc, rs):
    S = xn.shape[0]
    tm = _tile(S, 512)

    def body(x_ref, w_ref, pe_ref, c_ref, s_ref, k_ref, v_ref):
        kv = _dot(x_ref[...], w_ref[...])
        k_ref[:, :LANE] = kv[:, :LANE].astype(BF16)
        k_ref[:, LANE:] = _rope(pe_ref[...], c_ref[...], s_ref[...]).astype(BF16)
        v_ref[...] = kv[:, LANE:].astype(BF16)

    return pl.pallas_call(
        body, name="kv_proj", grid=(S // tm, MLA_HEADS),
        in_specs=[pl.BlockSpec((tm, 256), lambda i, j: (i, 0)), pl.BlockSpec((256, 256), lambda i, j: (0, j)),
                  pl.BlockSpec((tm, LANE), lambda i, j: (i, KPE // LANE)),
                  pl.BlockSpec((tm, LANE), lambda i, j: (i, 0)), pl.BlockSpec((tm, LANE), lambda i, j: (i, 0))],
        out_specs=[pl.BlockSpec((tm, 256), lambda i, j: (i, j)), pl.BlockSpec((tm, LANE), lambda i, j: (i, j))],
        out_shape=[jax.ShapeDtypeStruct((S, MLA_HEADS * 256), BF16), jax.ShapeDtypeStruct((S, MLA_HEADS * LANE), BF16)],
        compiler_params=_params(("parallel", "parallel")))(xn, w, h, rc, rs)


def _q_rope_bwd(dq, rc, rs):
    S = dq.shape[0]
    tm = _tile(S, 512)

    def body(d_ref, c_ref, s_ref, o_ref):
        o_ref[:, :LANE] = d_ref[:, :LANE].astype(BF16)
        o_ref[:, LANE:] = _rope_bwd(d_ref[:, LANE:], c_ref[...], s_ref[...]).astype(BF16)

    return pl.pallas_call(
        body, name="q_rope_bwd", grid=(S // tm, MLA_HEADS),
        in_specs=[pl.BlockSpec((tm, 256), lambda i, j: (i, j)),
                  pl.BlockSpec((tm, LANE), lambda i, j: (i, 0)), pl.BlockSpec((tm, LANE), lambda i, j: (i, 0))],
        out_specs=pl.BlockSpec((tm, 256), lambda i, j: (i, j)),
        out_shape=jax.ShapeDtypeStruct((S, MLA_HEADS * 256), BF16),
        compiler_params=_params(("parallel", "parallel")))(dq, rc, rs)


def _kv_bwd_prep(dk, dv, rc, rs):
    S = dk.shape[1]
    tm = _tile(S, 512)

    def body(dk_ref, dv_ref, c_ref, s_ref, o_ref, pe_ref):
        rot = jnp.zeros((tm, LANE), F32)
        for hh in range(MLA_HEADS):
            o_ref[:, hh * 256:hh * 256 + LANE] = dk_ref[hh, :, :LANE].astype(BF16)
            o_ref[:, hh * 256 + LANE:(hh + 1) * 256] = dv_ref[hh].astype(BF16)
            rot = rot + dk_ref[hh, :, LANE:]
        pe_ref[...] = _rope_bwd(rot, c_ref[...], s_ref[...])

    return pl.pallas_call(
        body, name="kv_bwd_prep", grid=(S // tm,),
        in_specs=[pl.BlockSpec((MLA_HEADS, tm, 256), lambda i: (0, i, 0)),
                  pl.BlockSpec((MLA_HEADS, tm, LANE), lambda i: (0, i, 0)), _row_spec(tm, LANE), _row_spec(tm, LANE)],
        out_specs=[_row_spec(tm, MLA_HEADS * 256), _row_spec(tm, LANE)],
        out_shape=[jax.ShapeDtypeStruct((S, MLA_HEADS * 256), BF16), jax.ShapeDtypeStruct((S, LANE), F32)],
        compiler_params=_params(("parallel",)))(dk, dv, rc, rs)


def _chunk_mask(T):
    row = lax.broadcasted_iota(jnp.int32, (T, T), 0)
    col = lax.broadcasted_iota(jnp.int32, (T, T), 1)
    return (col // CHUNK) <= (row // CHUNK)


def _att_blocks(S, q_blk=None):
    tq = min(q_blk or Q_BLK, S)
    tk = min(K_BLK, tq)
    return tq, tk, tq // tk


def _tail_masks(rows, tk):
    row = lax.broadcasted_iota(jnp.int32, (rows, tk), 0)
    col = lax.broadcasted_iota(jnp.int32, (rows, tk), 1)
    return (col // CHUNK) <= (row // CHUNK), col < row


def _span_masks(tk, r):
    row = lax.broadcasted_iota(jnp.int32, (tk, (r + 1) * tk), 0) + r * tk
    col = lax.broadcasted_iota(jnp.int32, (tk, (r + 1) * tk), 1)
    return (col // CHUNK) <= (row // CHUNK), col < row


def _put_rows(old, new, r0):
    return new if r0 == 0 else jnp.concatenate([old[:r0], new], axis=0)


def _mla_fwd(q, kp, v):
    S = q.shape[0]
    TQ, TK, n = _att_blocks(S)

    def body(q_ref, k_ref, v_ref, o_ref, lse_ref):
        i = pl.program_id(1)

        def update(carry, qb, keys, mask):
            m, l, acc = carry
            s = _dot_nt(qb, k_ref[keys, :])
            if mask is not None:
                s = jnp.where(mask, s, -1e30)
            m_new = jnp.maximum(m, jnp.max(s, axis=1, keepdims=True))
            a = jnp.exp(m - m_new)
            p = jnp.exp(s - m_new)
            return m_new, a * l + jnp.sum(p, axis=1, keepdims=True), a * acc + _dot(p.astype(BF16), v_ref[keys, :])

        carry = (jnp.full((TQ, 1), -1e30, F32), jnp.zeros((TQ, 1), F32), jnp.zeros((TQ, LANE), F32))
        carry = lax.fori_loop(
            0, i * n, lambda j, c: update(c, q_ref[...], pl.ds(pl.multiple_of(j * TK, TK), TK), None), carry)
        for r in range(n):
            rows = slice(r * TK, (r + 1) * TK)
            m, l, acc = update(tuple(c[rows] for c in carry), q_ref[rows, :],
                               pl.ds(pl.multiple_of(i * TQ, TQ), (r + 1) * TK), _span_masks(TK, r)[0])
            o_ref[rows, :] = acc / l
            lse_ref[rows, :] = jnp.broadcast_to(m + jnp.log(l), (TK, LANE))

    return pl.pallas_call(
        body, name="mla_fwd", grid=(MLA_HEADS, S // TQ),
        in_specs=[pl.BlockSpec((TQ, 256), lambda h, i: (i, h)), pl.BlockSpec((S, 256), lambda h, i: (0, h)),
                  pl.BlockSpec((S, LANE), lambda h, i: (0, h))],
        out_specs=[pl.BlockSpec((TQ, LANE), lambda h, i: (i, h)), pl.BlockSpec((TQ, LANE), lambda h, i: (i, h))],
        out_shape=[jax.ShapeDtypeStruct((S, MLA_HEADS * LANE), F32), jax.ShapeDtypeStruct((S, MLA_HEADS * LANE), F32)],
        compiler_params=_params(("parallel", "arbitrary")))(q, kp, v)


def _mla_bwd(q, kp, v, do_cat, o_cat, lse):
    S = q.shape[0]
    TQ, TK, n = _att_blocks(S)
    nq = S // TQ

    def body(q_ref, k_ref, v_ref, do_ref, o_ref, lse_ref, dq_ref, dk_hbm, dv_hbm, dk_acc, dv_acc):
        h, i = pl.program_id(0), pl.program_id(1)

        @pl.when(i == 0)
        def _():
            dk_acc[...] = jnp.zeros_like(dk_acc)
            dv_acc[...] = jnp.zeros_like(dv_acc)

        do32 = do_ref[...]
        dob = do32.astype(BF16)
        delta = jnp.sum(do32 * o_ref[...], axis=1, keepdims=True)
        lse_col = lse_ref[:, :1]

        def blk(j, dq, r0, masked):
            sl = pl.ds(pl.multiple_of(j * TK, TK), TK)
            kb, vb, qb = k_ref[sl, :], v_ref[sl, :], q_ref[r0:, :]
            s = _dot_nt(qb, kb)
            if masked:
                s = jnp.where(_tail_masks(TQ - r0, TK)[0], s, -1e30)
            p = jnp.exp(s - lse_col[r0:])
            ds = (p * (_dot_nt(dob[r0:], vb) - delta[r0:])).astype(BF16)
            dk_acc[sl, :] += _dot_tn(ds, qb)
            dv_acc[sl, :] += _dot_tn(p.astype(BF16), dob[r0:])
            return _put_rows(dq, dq[r0:] + _dot(ds, kb), r0)

        dq = lax.fori_loop(0, i * n, lambda j, c: blk(j, c, 0, False), jnp.zeros((TQ, 256), F32))
        for t in range(n):
            dq = blk(i * n + t, dq, t * TK, True)
        dq_ref[...] = dq * MLA_SCALE

        @pl.when(i == nq - 1)
        def _():
            pltpu.sync_copy(dk_acc, dk_hbm.at[h])
            pltpu.sync_copy(dv_acc, dv_hbm.at[h])

    any_spec = pl.BlockSpec(memory_space=pl.ANY)
    T = TQ
    return pl.pallas_call(
        body, name="mla_bwd", grid=(MLA_HEADS, nq),
        in_specs=[pl.BlockSpec((T, 256), lambda h, i: (i, h)), pl.BlockSpec((S, 256), lambda h, i: (0, h)),
                  pl.BlockSpec((S, LANE), lambda h, i: (0, h)), pl.BlockSpec((T, LANE), lambda h, i: (i, h)),
                  pl.BlockSpec((T, LANE), lambda h, i: (i, h)), pl.BlockSpec((T, LANE), lambda h, i: (i, h))],
        out_specs=[pl.BlockSpec((T, 256), lambda h, i: (i, h)), any_spec, any_spec],
        out_shape=[jax.ShapeDtypeStruct((S, MLA_HEADS * 256), F32), jax.ShapeDtypeStruct((MLA_HEADS, S, 256), F32),
                   jax.ShapeDtypeStruct((MLA_HEADS, S, LANE), F32)],
        scratch_shapes=[pltpu.VMEM((S, 256), F32), pltpu.VMEM((S, LANE), F32)],
        compiler_params=_params(("arbitrary", "arbitrary")))(q, kp, v, do_cat, o_cat, lse)


def _split_dot(x, tri):
    top = lax.bitcast_convert_type(lax.bitcast_convert_type(x, jnp.uint32) & jnp.uint32(0xFFFF0000), F32)
    return _dot(top.astype(BF16), tri) + _dot((x - top).astype(BF16), tri)


def _sb_block(qb, kb, tri, carry, masked):
    z = _dot_nt(qb, kb)
    lb = jnp.minimum(z, 0.0) - jnp.log(1.0 + jnp.exp(-jnp.abs(z)))
    lm = lb - z
    strict = None
    if masked:
        strict = _tail_masks(z.shape[0], z.shape[1])[1]
        lm = jnp.where(strict, lm, 0.0)
    a = jnp.exp(lb + carry + _split_dot(lm, tri))
    if masked:
        a = jnp.where(strict, a, 0.0)
    return a, lb, lm, strict


def _sb_walk(blk, j0, state):
    def alive(c):
        return jnp.logical_and(c[0] >= 0, jnp.max(c[1][0]) > SB_DEAD)

    return lax.while_loop(alive, lambda c: (c[0] - 1, blk(c[0], c[1], 0, False)), (j0, state))[1]


def _triangle(tk):
    row = lax.broadcasted_iota(jnp.int32, (tk, tk), 0)
    col = lax.broadcasted_iota(jnp.int32, (tk, tk), 1)
    return (row > col).astype(BF16)


def _sb_fwd(qkv):
    S = qkv.shape[0]
    TQ, TK, n = _att_blocks(S, SB_Q_BLK)
    T = TQ

    def body(q_ref, k_ref, v_ref, o_ref):
        i = pl.program_id(1)
        tri = _triangle(TK)

        def blk(j, state, r0, masked):
            carry, acc = (c[r0:] for c in state)
            sl = pl.ds(pl.multiple_of(j * TK, TK), TK)
            a, _, lm, _ = _sb_block(q_ref[r0:, :], k_ref[sl, :], tri, carry, masked)
            new = (carry + jnp.sum(lm, axis=1, keepdims=True), acc + _dot(a.astype(BF16), v_ref[sl, :]))
            return tuple(_put_rows(c, u, r0) for c, u in zip(state, new))

        state = (jnp.zeros((TQ, 1), F32), jnp.zeros((TQ, LANE), F32))
        for t in reversed(range(n)):
            state = blk(i * n + t, state, t * TK, True)
        state = _sb_walk(blk, i * n - 1, state)
        o_ref[...] = state[1]

    return pl.pallas_call(
        body, name="sb_fwd", grid=(SB_HEADS, S // T),
        in_specs=[pl.BlockSpec((T, LANE), lambda h, i: (i, h)), pl.BlockSpec((S, LANE), lambda h, i: (0, 4 + h)),
                  pl.BlockSpec((S, LANE), lambda h, i: (0, 8 + h))],
        out_specs=pl.BlockSpec((T, LANE), lambda h, i: (i, h)),
        out_shape=jax.ShapeDtypeStruct((S, SB_HEADS * LANE), F32),
        compiler_params=_params(("parallel", "arbitrary")))(qkv, qkv, qkv)


def _sb_bwd(qkv, do_cat, o_cat, col0):
    S = qkv.shape[0]
    TQ, TK, n = _att_blocks(S, SB_Q_BLK)
    T = TQ
    nq = S // TQ

    def body(q_ref, k_ref, v_ref, do_ref, o_ref, dq_ref, dk_hbm, dv_hbm, dk_acc, dv_acc):
        h, i = pl.program_id(0), pl.program_id(1)

        @pl.when(i == 0)
        def _():
            dk_acc[...] = jnp.zeros_like(dk_acc)
            dv_acc[...] = jnp.zeros_like(dv_acc)

        dob = do_ref[...].astype(BF16)
        tri = _triangle(TK)
        rest0 = jnp.sum(dob.astype(F32) * o_ref[...], axis=1, keepdims=True)

        def blk(j, state, r0, masked):
            carry, rest, dq = (c[r0:] for c in state)
            sl = pl.ds(pl.multiple_of(j * TK, TK), TK)
            kb, vb, qb = k_ref[sl, :], v_ref[sl, :], q_ref[r0:, :]
            a, lb, lm, strict = _sb_block(qb, kb, tri, carry, masked)
            ab = a.astype(BF16)
            e = ab.astype(F32) * _dot_nt(dob[r0:], vb)
            dz = e - jnp.exp(lb) * (rest - _split_dot(e, tri))
            if masked:
                dz = jnp.where(strict, dz, 0.0)
            dzb = dz.astype(BF16)
            dk_acc[sl, :] += _dot_tn(dzb, qb)
            dv_acc[sl, :] += _dot_tn(ab, dob[r0:])
            new = (carry + jnp.sum(lm, axis=1, keepdims=True), rest - jnp.sum(e, axis=1, keepdims=True),
                   dq + _dot(dzb, kb))
            return tuple(_put_rows(c, u, r0) for c, u in zip(state, new))

        state = (jnp.zeros((TQ, 1), F32), rest0, jnp.zeros((TQ, LANE), F32))
        for t in reversed(range(n)):
            state = blk(i * n + t, state, t * TK, True)
        state = _sb_walk(blk, i * n - 1, state)
        dq_ref[...] = state[2] * SB_SCALE

        @pl.when(i == nq - 1)
        def _():
            pltpu.sync_copy(dk_acc, dk_hbm.at[h])
            pltpu.sync_copy(dv_acc, dv_hbm.at[h])

    any_spec = pl.BlockSpec(memory_space=pl.ANY)
    return pl.pallas_call(
        body, name="sb_bwd", grid=(SB_HEADS, nq),
        in_specs=[pl.BlockSpec((T, LANE), lambda h, i: (i, h)), pl.BlockSpec((S, LANE), lambda h, i: (0, 4 + h)),
                  pl.BlockSpec((S, LANE), lambda h, i: (0, 8 + h)),
                  pl.BlockSpec((T, LANE), lambda h, i: (i, col0 + h)), pl.BlockSpec((T, LANE), lambda h, i: (i, col0 + h))],
        out_specs=[pl.BlockSpec((T, LANE), lambda h, i: (i, h)), any_spec, any_spec],
        out_shape=[jax.ShapeDtypeStruct((S, SB_HEADS * LANE), F32), jax.ShapeDtypeStruct((SB_HEADS, S, LANE), F32),
                   jax.ShapeDtypeStruct((SB_HEADS, S, LANE), F32)],
        scratch_shapes=[pltpu.VMEM((S, LANE), F32), pltpu.VMEM((S, LANE), F32)],
        compiler_params=_params(("arbitrary", "arbitrary")))(qkv, qkv, qkv, do_cat, o_cat)


def _mem_probs(q, k_ref, hh):
    lane = lax.broadcasted_iota(jnp.int32, (1, 256), 1) // 64
    msk = lane == hh
    qh = jnp.where(msk, q, 0.0).astype(BF16)
    s = _dot_nt(qh, k_ref[...]) * MEM_SCALE
    p = jnp.exp(s - jnp.max(s, axis=1, keepdims=True))
    return msk, qh, p / jnp.sum(p, axis=1, keepdims=True)


def _mem_fwd(h, mk, mv):
    S = h.shape[0]
    tm = _tile(S, 512)

    def body(q_ref, k_ref, v_ref, o_ref):
        q = q_ref[...]
        out = jnp.zeros((tm, 256), F32)
        for hh in range(MEM_HEADS):
            msk, _, p = _mem_probs(q, k_ref, hh)
            out = out + jnp.where(msk, _dot(p.astype(BF16), v_ref[...]), 0.0)
        o_ref[...] = out

    return pl.pallas_call(
        body, name="mem_fwd", grid=(S // tm,),
        in_specs=[_row_spec(tm, 256, MQ // 256), _fix_spec((256, 256)), _fix_spec((256, 256))],
        out_specs=_row_spec(tm, 256), out_shape=jax.ShapeDtypeStruct((S, 256), F32),
        compiler_params=_params(("parallel",)))(h, mk, mv)


def _mem_bwd(h, mk, mv, do_cat, col0):
    S = h.shape[0]
    tm = _tile(S, 512)

    def body(q_ref, k_ref, v_ref, do_ref, dq_ref, dk_ref, dv_ref):
        @pl.when(pl.program_id(0) == 0)
        def _():
            dk_ref[...] = jnp.zeros_like(dk_ref)
            dv_ref[...] = jnp.zeros_like(dv_ref)

        q, do = q_ref[...], do_ref[...]
        dq = jnp.zeros((tm, 256), F32)
        for hh in range(MEM_HEADS):
            msk, qh, p = _mem_probs(q, k_ref, hh)
            doh = jnp.where(msk, do, 0.0).astype(BF16)
            dp = _dot_nt(doh, v_ref[...])
            ds = (p * (dp - jnp.sum(p * dp, axis=1, keepdims=True)) * MEM_SCALE).astype(BF16)
            dq = dq + jnp.where(msk, _dot(ds, k_ref[...]), 0.0)
            dk_ref[...] += _dot_tn(ds, qh)
            dv_ref[...] += _dot_tn(p.astype(BF16), doh)
        dq_ref[...] = dq

    return pl.pallas_call(
        body, name="mem_bwd", grid=(S // tm,),
        in_specs=[_row_spec(tm, 256, MQ // 256), _fix_spec((256, 256)), _fix_spec((256, 256)),
                  _row_spec(tm, 256, col0 // 256)],
        out_specs=[_row_spec(tm, 256), _fix_spec((256, 256)), _fix_spec((256, 256))],
        out_shape=[jax.ShapeDtypeStruct((S, 256), F32), jax.ShapeDtypeStruct((256, 256), F32),
                   jax.ShapeDtypeStruct((256, 256), F32)],
        compiler_params=_params(("arbitrary",)))(h, mk, mv, do_cat)


SG_T = 128


def _sg_norm(sv, g, b):
    gv = _gelu(sv)
    xc = gv - jnp.mean(gv, axis=1, keepdims=True)
    rstd = lax.rsqrt(jnp.mean(xc * xc, axis=1, keepdims=True) + LN_EPS)
    xhat = xc * rstd
    return xhat, rstd, xhat * g + b


def _sg_fwd(h, lng, lnb, w, bias_t):
    S = h.shape[0]
    tm = _tile(S, 512)

    def body(u_ref, v_ref, g_ref, b_ref, w_ref, bias_ref, o_ref):
        mask = _chunk_mask(SG_T)
        for n in range(tm // SG_T):
            rows = slice(n * SG_T, (n + 1) * SG_T)
            u = _gelu(u_ref[rows, :])
            _, _, vn = _sg_norm(v_ref[rows, :], g_ref[...], b_ref[...])
            vb = vn.astype(BF16)
            for gi in range(4):
                cols = slice(gi * LANE, (gi + 1) * LANE)
                wg = jnp.where(mask, w_ref[gi], 0.0).astype(BF16)
                mixed = _dot(wg, vb[:, cols]) + bias_ref[:, gi:gi + 1]
                o_ref[rows, cols] = u[:, cols] * mixed

    return pl.pallas_call(
        body, name="sg_fwd", grid=(S // tm,),
        in_specs=[_row_spec(tm, 512, SGU // 512), _row_spec(tm, 512, SGV // 512), _fix_spec((1, 512)),
                  _fix_spec((1, 512)), _fix_spec((4, SG_T, SG_T)), _fix_spec((SG_T, 4))],
        out_specs=_row_spec(tm, 512), out_shape=jax.ShapeDtypeStruct((S, 512), F32),
        compiler_params=_params(("parallel",)))(h, h, lng.reshape(1, 512), lnb.reshape(1, 512), w, bias_t)


def _sg_bwd(h, lng, lnb, w, bias_t, do_cat, col0):
    S = h.shape[0]
    tm = _tile(S, 512)
    nsteps = S // tm

    def body(u_ref, v_ref, g_ref, b_ref, w_ref, bias_ref, do0_ref, do1_ref, do2_ref, do3_ref,
             du_ref, dv_ref, dw_ref, dbias_ref, dg_ref, db_ref, dvn_scr, dbias_acc):
        do_refs = (do0_ref, do1_ref, do2_ref, do3_ref)
        step = pl.program_id(0)

        @pl.when(step == 0)
        def _():
            dw_ref[...] = jnp.zeros_like(dw_ref)
            dg_ref[...] = jnp.zeros_like(dg_ref)
            db_ref[...] = jnp.zeros_like(db_ref)
            dbias_acc[...] = jnp.zeros_like(dbias_acc)

        mask = _chunk_mask(SG_T)
        for n in range(tm // SG_T):
            rows = slice(n * SG_T, (n + 1) * SG_T)
            su, sv = u_ref[rows, :], v_ref[rows, :]
            u = _gelu(su)
            xhat, rstd, vn = _sg_norm(sv, g_ref[...], b_ref[...])
            vb = vn.astype(BF16)
            ugrad = _gelu_grad(su)
            for gi in range(4):
                cols = slice(gi * LANE, (gi + 1) * LANE)
                do = do_refs[gi][rows, :]
                wg = jnp.where(mask, w_ref[gi], 0.0).astype(BF16)
                mixed = _dot(wg, vb[:, cols]) + bias_ref[:, gi:gi + 1]
                dmixed = do * u[:, cols]
                dmb = dmixed.astype(BF16)
                du_ref[rows, cols] = do * mixed * ugrad[:, cols]
                dvn_scr[:, cols] = _dot_tn(wg, dmb)
                dw_ref[gi] += jnp.where(mask, _dot_nt(dmb, vb[:, cols]), 0.0)
                dbias_acc[gi] += dmixed
            dvn = dvn_scr[...]
            dg_ref[...] += jnp.sum(dvn * xhat, axis=0, keepdims=True)
            db_ref[...] += jnp.sum(dvn, axis=0, keepdims=True)
            dxh = dvn * g_ref[...]
            dgv = rstd * (dxh - jnp.mean(dxh, axis=1, keepdims=True)
                          - xhat * jnp.mean(dxh * xhat, axis=1, keepdims=True))
            dv_ref[rows, :] = dgv * _gelu_grad(sv)

        @pl.when(step == nsteps - 1)
        def _():
            for gi in range(4):
                dbias_ref[:, gi:gi + 1] = jnp.sum(dbias_acc[gi], axis=1, keepdims=True)

    return pl.pallas_call(
        body, name="sg_bwd", grid=(nsteps,),
        in_specs=[_row_spec(tm, 512, SGU // 512), _row_spec(tm, 512, SGV // 512), _fix_spec((1, 512)),
                  _fix_spec((1, 512)), _fix_spec((4, SG_T, SG_T)), _fix_spec((SG_T, 4))]
                 + [_row_spec(tm, LANE, col0 // LANE + gi) for gi in range(4)],
        out_specs=[_row_spec(tm, 512), _row_spec(tm, 512), _fix_spec((4, SG_T, SG_T)), _fix_spec((SG_T, 4)),
                   _fix_spec((1, 512)), _fix_spec((1, 512))],
        out_shape=[jax.ShapeDtypeStruct((S, 512), F32), jax.ShapeDtypeStruct((S, 512), F32),
                   jax.ShapeDtypeStruct((4, SG_T, SG_T), F32), jax.ShapeDtypeStruct((SG_T, 4), F32),
                   jax.ShapeDtypeStruct((1, 512), F32), jax.ShapeDtypeStruct((1, 512), F32)],
        scratch_shapes=[pltpu.VMEM((SG_T, 512), F32), pltpu.VMEM((4, SG_T, SG_T), F32)],
        compiler_params=_params(("arbitrary",)))(h, h, lng.reshape(1, 512), lnb.reshape(1, 512), w, bias_t,
                                                 do_cat, do_cat, do_cat, do_cat)


def _gate_fwd(o_cat, h):
    S = h.shape[0]
    tm = _tile(S, 512)

    def body(o_ref, g_ref, y_ref):
        g = g_ref[...]
        y_ref[...] = (o_ref[...] * (g * jax.nn.sigmoid(g))).astype(BF16)

    return pl.pallas_call(
        body, name="gate_fwd", grid=(S // tm, 4),
        in_specs=[pl.BlockSpec((tm, 512), lambda i, j: (i, j)), pl.BlockSpec((tm, 512), lambda i, j: (i, GATE // 512 + j))],
        out_specs=pl.BlockSpec((tm, 512), lambda i, j: (i, j)), out_shape=jax.ShapeDtypeStruct((S, D_MODEL), BF16),
        compiler_params=_params(("parallel", "parallel")))(o_cat, h)


def _gate_bwd(dyg, o_cat, h):
    S = h.shape[0]
    tm = _tile(S, 512)

    def body(d_ref, o_ref, g_ref, do_ref, dg_ref):
        d, g = d_ref[...], g_ref[...]
        sig = jax.nn.sigmoid(g)
        do_ref[...] = d * (g * sig)
        dg_ref[...] = d * o_ref[...] * (sig * (1.0 + g * (1.0 - sig)))

    blk = pl.BlockSpec((tm, 512), lambda i, j: (i, j))
    return pl.pallas_call(
        body, name="gate_bwd", grid=(S // tm, 4),
        in_specs=[blk, blk, pl.BlockSpec((tm, 512), lambda i, j: (i, GATE // 512 + j))],
        out_specs=[blk, blk],
        out_shape=[jax.ShapeDtypeStruct((S, D_MODEL), F32), jax.ShapeDtypeStruct((S, D_MODEL), F32)],
        compiler_params=_params(("parallel", "parallel")))(dyg, o_cat, h)


def _ln_res_fwd(x, y, g, b):
    S = x.shape[0]
    tm = _tile(S, 256)

    def body(x_ref, y_ref, g_ref, b_ref, o_ref, ob_ref, r_ref):
        r = ALPHA * x_ref[...] + y_ref[...]
        r_ref[...] = r
        xc = r - jnp.mean(r, axis=1, keepdims=True)
        o = xc * lax.rsqrt(jnp.mean(xc * xc, axis=1, keepdims=True) + LN_EPS) * g_ref[...] + b_ref[...]
        o_ref[...] = o
        ob_ref[...] = o.astype(BF16)

    return pl.pallas_call(
        body, name="ln_res_fwd", grid=(S // tm,),
        in_specs=[_row_spec(tm, D_MODEL), _row_spec(tm, D_MODEL), _fix_spec((1, D_MODEL)), _fix_spec((1, D_MODEL))],
        out_specs=[_row_spec(tm, D_MODEL), _row_spec(tm, D_MODEL), _row_spec(tm, D_MODEL)],
        out_shape=[jax.ShapeDtypeStruct((S, D_MODEL), F32), jax.ShapeDtypeStruct((S, D_MODEL), BF16),
                   jax.ShapeDtypeStruct((S, D_MODEL), F32)],
        compiler_params=_params(("parallel",)))(x, y, g.reshape(1, D_MODEL), b.reshape(1, D_MODEL))


def _ln_res_bwd(dout, r, g):
    S = r.shape[0]
    tm = _tile(S, 256)

    def body(d_ref, r_ref, g_ref, dr_ref, dg_ref, db_ref):
        @pl.when(pl.program_id(0) == 0)
        def _():
            dg_ref[...] = jnp.zeros_like(dg_ref)
            db_ref[...] = jnp.zeros_like(db_ref)

        d, r = d_ref[...], r_ref[...]
        xc = r - jnp.mean(r, axis=1, keepdims=True)
        rstd = lax.rsqrt(jnp.mean(xc * xc, axis=1, keepdims=True) + LN_EPS)
        xhat = xc * rstd
        dxh = d * g_ref[...]
        dr_ref[...] = rstd * (dxh - jnp.mean(dxh, axis=1, keepdims=True)
                              - xhat * jnp.mean(dxh * xhat, axis=1, keepdims=True))
        dg_ref[...] += jnp.sum(d * xhat, axis=0, keepdims=True)
        db_ref[...] += jnp.sum(d, axis=0, keepdims=True)

    return pl.pallas_call(
        body, name="ln_res_bwd", grid=(S // tm,),
        in_specs=[_row_spec(tm, D_MODEL), _row_spec(tm, D_MODEL), _fix_spec((1, D_MODEL))],
        out_specs=[_row_spec(tm, D_MODEL), _fix_spec((1, D_MODEL)), _fix_spec((1, D_MODEL))],
        out_shape=[jax.ShapeDtypeStruct((S, D_MODEL), F32), jax.ShapeDtypeStruct((1, D_MODEL), F32),
                   jax.ShapeDtypeStruct((1, D_MODEL), F32)],
        compiler_params=_params(("arbitrary",)))(dout, r, g.reshape(1, D_MODEL))


def _loss_head(y, target):
    S = y.shape[0]
    tm = _tile(S, 256)

    def body(y_ref, t_ref, l_ref, d_ref):
        @pl.when(pl.program_id(0) == 0)
        def _():
            l_ref[...] = jnp.zeros_like(l_ref)

        diff = y_ref[...] - t_ref[...]
        d_ref[...] = diff * (1.0 / D_MODEL)
        per_row = jnp.mean(diff * diff, axis=1, keepdims=True)
        l_ref[...] += 0.5 * jnp.sum(per_row, axis=0, keepdims=True)

    return pl.pallas_call(
        body, name="loss_head", grid=(S // tm,), in_specs=[_row_spec(tm, D_MODEL), _row_spec(tm, D_MODEL)],
        out_specs=[_fix_spec((8, LANE)), _row_spec(tm, D_MODEL)],
        out_shape=[jax.ShapeDtypeStruct((8, LANE), F32), jax.ShapeDtypeStruct((S, D_MODEL), F32)],
        compiler_params=_params(("arbitrary",)))(y, target)


def _perm_table():
    table, at = [], 0
    for name in PERM_ORDER:
        start, width = ORIG[name]
        table.append((name, start, width, at))
        at += width
    return table


def _permute_w_in(by_chip):
    wc = by_chip.shape[-1]
    parts = []
    for _, start, width, _ in _perm_table():
        lo = start
        while lo < start + width:
            k = lo // wc
            hi = min(start + width, (k + 1) * wc)
            parts.append(by_chip[k, ..., lo - k * wc:hi - k * wc])
            lo = hi
    parts.append(jnp.zeros(by_chip.shape[1:-1] + (HP - D_IN,), by_chip.dtype))
    return jnp.concatenate(parts, axis=-1)


def _model_cols(wp, lo, hi):
    parts = []
    for _, start, width, at in sorted(_perm_table(), key=lambda t: t[1]):
        a, b = max(lo, start), min(hi, start + width)
        if a < b:
            parts.append(wp[..., at + a - start:at + b - start])
    return jnp.concatenate(parts, axis=-1)


def _rope_tables(positions):
    inv_freq = ROPE_THETA ** (-jnp.arange(0, 64, 2, dtype=F32) / 64)
    ang = positions.astype(F32)[:, None] * inv_freq[None, :]
    cos, sin, zero = jnp.cos(ang), jnp.sin(ang), jnp.zeros((positions.shape[0], 64), F32)
    return jnp.concatenate([cos, cos, zero], axis=1), jnp.concatenate([-sin, sin, zero], axis=1)


def _local_step(x, mem, positions, target, w):
    rc, rs = _rope_tables(positions)
    mem_b = mem.astype(BF16)
    xb = x.astype(BF16)
    w_in_all = _permute_w_in(w["w_in"])
    w_uq_all = jnp.pad(w["w_uq"].reshape(DEPTH, 512, MLA_HEADS, 192),
                       ((0, 0), (0, 0), (0, 0), (0, 64))).reshape(DEPTH, 512, MLA_HEADS * 256)
    saved = []
    for l in range(DEPTH):
        w_in, w_uq, w_ukv = w_in_all[l], w_uq_all[l], w["w_ukv"][l]
        h = _mm(xb, w_in, tm=1024, tn=1152, tk=2048, name="in_proj")
        cq_n = _rms_fwd(h, CQ, 512, w["q_norm_g"][l], "rms_q")
        ckv_n = _rms_fwd(h, CKV, 256, w["kv_norm_g"][l], "rms_kv")
        q = _q_proj(cq_n, w_uq, rc, rs)
        kp, v = _kv_proj(ckv_n, w_ukv, h, rc, rs)
        o_a, lse = _mla_fwd(q, kp, v)
        bias_t = w["sg_b"][l].T
        o_b = _sg_fwd(h, w["sg_ln_g"][l], w["sg_ln_b"][l], w["sg_w"][l], bias_t)
        qkv = jnp.concatenate([h[:, SBQ:SBQ + 512] * SB_SCALE, h[:, SBQ + 512:SBQ + 1536]], axis=1).astype(BF16)
        o_c = _sb_fwd(qkv)
        mk = _mm(mem_b, w["w_mem_k"][l], out_dtype=BF16, name="mem_kv")
        mv = _mm(mem_b, w["w_mem_v"][l], out_dtype=BF16, name="mem_kv")
        o_m = _mem_fwd(h, mk, mv)
        o_cat = jnp.concatenate([o_a, o_b, o_c, o_m], axis=1)
        yg = _gate_fwd(o_cat, h)
        y = _mm(yg, w["w_out"][l], tm=1024, tn=1024, tk=2048, name="out_proj")
        x_new, xb_new, r = _ln_res_fwd(x, y, w["ln_g"][l], w["ln_b"][l])
        saved.append(dict(xb=xb, h=h, cq_n=cq_n, ckv_n=ckv_n, q=q, kp=kp, v=v, lse=lse, qkv=qkv, mk=mk, mv=mv,
                          o_cat=o_cat, yg=yg, r=r, w_in=w_in, w_uq=w_uq, w_ukv=w_ukv, bias_t=bias_t))
        x, xb = x_new, xb_new

    loss, dx = _loss_head(x, target)

    grads = {n: [None] * DEPTH for n in SHARDED + SMALL}
    for l in reversed(range(DEPTH)):
        s = saved[l]
        h = s["h"]
        dr, dlg, dlb = _ln_res_bwd(dx, s["r"], w["ln_g"][l])
        grads["ln_g"][l], grads["ln_b"][l] = dlg[0], dlb[0]
        grads["w_out"][l] = _mm(s["yg"], dr, ta=True, tm=1024, tn=1024, tk=2048, name="dw_out")
        dyg = _mm(dr, w["w_out"][l], tb=True, tm=1024, tn=1024, tk=2048, name="d_out_proj")
        do_cat, dgates = _gate_bwd(dyg, s["o_cat"], h)
        dmq, dmk, dmv = _mem_bwd(h, s["mk"], s["mv"], do_cat, 1792)
        grads["w_mem_k"][l] = _mm(mem_b, dmk, ta=True, name="dw_mem")
        grads["w_mem_v"][l] = _mm(mem_b, dmv, ta=True, name="dw_mem")
        dsq, dsk, dsv = _sb_bwd(s["qkv"], do_cat, s["o_cat"], 1280 // LANE)
        dsk = dsk.transpose(1, 0, 2).reshape(-1, 512)
        dsv = dsv.transpose(1, 0, 2).reshape(-1, 512)
        du, dv, dsgw, dsgb, dsg_g, dsg_b = _sg_bwd(h, w["sg_ln_g"][l], w["sg_ln_b"][l], w["sg_w"][l], s["bias_t"],
                                                   do_cat, 768)
        grads["sg_w"][l], grads["sg_b"][l] = dsgw, dsgb.T
        grads["sg_ln_g"][l], grads["sg_ln_b"][l] = dsg_g[0], dsg_b[0]
        dq, dk, dvv = _mla_bwd(s["q"], s["kp"], s["v"], do_cat, s["o_cat"], s["lse"])
        dq_raw = _q_rope_bwd(dq, rc, rs)
        dkv, dkpe = _kv_bwd_prep(dk, dvv, rc, rs)
        dw_uq = _mm(s["cq_n"], dq_raw, ta=True, tk=1024, name="dw_uq")
        grads["w_uq"][l] = dw_uq.reshape(512, MLA_HEADS, 256)[:, :, :192].reshape(512, MLA_HEADS * 192)
        grads["w_ukv"][l] = _mm(s["ckv_n"], dkv, ta=True, tk=1024, name="dw_ukv")
        dcq_n = _mm(dq_raw, s["w_uq"], tb=True, name="d_cq")
        dckv_n = _mm(dkv, s["w_ukv"], tb=True, name="d_ckv")
        dcq, dqg = _rms_bwd(h, CQ, 512, w["q_norm_g"][l], dcq_n, "rms_q_bwd")
        dckv, dkvg = _rms_bwd(h, CKV, 256, w["kv_norm_g"][l], dckv_n, "rms_kv_bwd")
        grads["q_norm_g"][l], grads["kv_norm_g"][l] = dqg[0], dkvg[0]
        dh = jnp.concatenate([dcq, dckv, dmq, du, dv, dsq, dsk, dsv, dgates, dkpe], axis=1).astype(BF16)
        dw_in = _mm(s["xb"], dh, ta=True, tm=1024, tn=1152, tk=2048, name="dw_in")
        grads["w_in"][l] = dw_in
        dx = _mm(dh, s["w_in"], tb=True, add=dr, add_scale=ALPHA, tm=1024, tn=1024, tk=1920, name="d_in_proj")

    return loss, dx, grads


MESH = pl.DeviceIdType.MESH
HBM_SPEC = pl.BlockSpec(memory_space=pltpu.HBM)


def _place():
    x, y, c = lax.axis_index("x"), lax.axis_index("y"), lax.axis_index("c")
    return x, y, c, [(1 - x, y), (x, 1 - y), (1 - x, 1 - y)]


HALF = DEPTH // 2


def _comm_call(body, name, arrays, out_shapes, n_sems):
    return pl.pallas_call(
        body, name=name, in_specs=[HBM_SPEC] * len(arrays), out_specs=[HBM_SPEC] * len(out_shapes), out_shape=out_shapes,
        scratch_shapes=[pltpu.SemaphoreType.DMA((n_sems,)), pltpu.SemaphoreType.DMA((n_sems,))],
        compiler_params=pltpu.CompilerParams(has_side_effects=True))(*arrays)


def _gather_weights(shards):
    na = len(shards)

    def body(*refs):
        srcs, outs, (send_sems, recv_sems) = refs[:na], refs[na:2 * na], refs[2 * na:]
        x, y, c, chips = _place()
        mine, theirs = pl.ds(HALF * c, HALF), pl.ds(HALF * (1 - c), HALF)

        def copy(a, k, src_ref, chip, layers, to):
            return pltpu.make_async_remote_copy(
                src_ref=src_ref, dst_ref=outs[a].at[chip, layers], send_sem=send_sems.at[6 * a + k],
                recv_sem=recv_sems.at[6 * a + k], device_id=to, device_id_type=MESH)

        sent = [copy(a, j, srcs[a].at[mine], 2 * x + y, mine, (px, py, c))
                for a in range(na) for j, (px, py) in enumerate(chips)]
        for cp in sent:
            cp.start()
        passed = []
        for j, (px, py) in enumerate(chips):
            for a in range(na):
                copy(a, j, srcs[a].at[mine], 2 * px + py, mine, (px, py, c)).wait_recv()
                cp = copy(a, 3 + j, outs[a].at[2 * px + py, mine], 2 * px + py, mine, (x, y, 1 - c))
                cp.start()
                passed.append(cp)
        for j, (px, py) in enumerate(chips):
            for a in range(na):
                copy(a, 3 + j, srcs[a].at[theirs], 2 * px + py, theirs, (x, y, 1 - c)).wait_recv()
        for cp in sent + passed:
            cp.wait_send()

    return _comm_call(body, "gather_weights", shards, [jax.ShapeDtypeStruct((4,) + s.shape, s.dtype) for s in shards], 6 * na)


def _swap_halves(gs):
    na = len(gs)

    def body(*refs):
        srcs, outs, (send_sems, recv_sems) = refs[:na], refs[na:2 * na], refs[2 * na:]
        x, y, c, _ = _place()
        cps = [pltpu.make_async_remote_copy(
            src_ref=srcs[a].at[:, pl.ds(HALF * (1 - c), HALF)], dst_ref=outs[a], send_sem=send_sems.at[a],
            recv_sem=recv_sems.at[a], device_id=(x, y, 1 - c), device_id_type=MESH) for a in range(na)]
        for cp in cps:
            cp.start()
        for cp in cps:
            cp.wait()

    return _comm_call(body, "swap_halves", gs,
                      [jax.ShapeDtypeStruct((4, HALF) + g.shape[2:], g.dtype) for g in gs], na)


def _pair_sum(g, other, c):
    _, _, R, C = g.shape
    tr = _row_tile(R, 3 * C * 4)

    def body(c_ref, a_ref, b_ref, o_ref):
        o_ref[...] = (a_ref[...] + b_ref[...]).astype(BF16)

    blk = pl.BlockSpec((None, None, tr, C), lambda d, l, i, c_ref: (d, l, i, 0))
    return pl.pallas_call(
        body, name="pair_sum",
        grid_spec=pltpu.PrefetchScalarGridSpec(
            num_scalar_prefetch=1, grid=(4, HALF, R // tr),
            in_specs=[pl.BlockSpec((None, None, tr, C), lambda d, l, i, c_ref: (d, HALF * c_ref[0] + l, i, 0)), blk],
            out_specs=blk),
        out_shape=jax.ShapeDtypeStruct((4, HALF, R, C), BF16),
        compiler_params=_params(("parallel", "parallel", "parallel")))(c, g, other)


def _exchange_chips(ps):
    na = len(ps)

    def body(*refs):
        srcs, outs, (send_sems, recv_sems) = refs[:na], refs[na:2 * na], refs[2 * na:]
        x, y, c, chips = _place()
        cps = [pltpu.make_async_remote_copy(
            src_ref=srcs[a].at[2 * px + py], dst_ref=outs[a].at[j], send_sem=send_sems.at[3 * a + j],
            recv_sem=recv_sems.at[3 * a + j], device_id=(px, py, c), device_id_type=MESH)
            for a in range(na) for j, (px, py) in enumerate(chips)]
        for cp in cps:
            cp.start()
        for cp in cps:
            cp.wait()

    return _comm_call(body, "exchange_chips", ps, [jax.ShapeDtypeStruct((3,) + p.shape[1:], p.dtype) for p in ps], 3 * na)


def _chip_sum(p, got, me):
    _, _, R, C = p.shape
    tr = _row_tile(R, 4 * C * 4)

    def body(me_ref, p_ref, g_ref, o_ref):
        acc = p_ref[...].astype(F32)
        for k in range(3):
            acc = acc + g_ref[k].astype(F32)
        o_ref[...] = acc

    return pl.pallas_call(
        body, name="chip_sum",
        grid_spec=pltpu.PrefetchScalarGridSpec(
            num_scalar_prefetch=1, grid=(HALF, R // tr),
            in_specs=[pl.BlockSpec((None, None, tr, C), lambda l, i, me_ref: (me_ref[0], l, i, 0)),
                      pl.BlockSpec((3, None, tr, C), lambda l, i, me_ref: (0, l, i, 0))],
            out_specs=pl.BlockSpec((None, tr, C), lambda l, i, me_ref: (l, i, 0))),
        out_shape=jax.ShapeDtypeStruct((HALF, R, C), F32), compiler_params=_params(("parallel", "parallel")))(me, p, got)


def _sum_parts(t, name):
    n, H, W = t.shape
    th = _row_tile(H, (n + 1) * W * 4)

    def body(t_ref, o_ref):
        acc = t_ref[0]
        for k in range(1, n):
            acc = acc + t_ref[k]
        o_ref[...] = acc

    return pl.pallas_call(
        body, name=name, grid=(H // th,), in_specs=[pl.BlockSpec((n, th, W), lambda i: (0, i, 0))],
        out_specs=pl.BlockSpec((th, W), lambda i: (i, 0)), out_shape=jax.ShapeDtypeStruct((H, W), F32),
        compiler_params=_params(("parallel",)))(t)


def _share_with_sibling(halves):
    na = len(halves)

    def body(*refs):
        srcs, outs, (send_sems, recv_sems) = refs[:na], refs[na:2 * na], refs[2 * na:]
        x, y, c, _ = _place()

        def copy(a, layers):
            return pltpu.make_async_remote_copy(
                src_ref=srcs[a], dst_ref=outs[a].at[layers], send_sem=send_sems.at[a], recv_sem=recv_sems.at[a],
                device_id=(x, y, 1 - c), device_id_type=MESH)

        sent = [copy(a, pl.ds(HALF * c, HALF)) for a in range(na)]
        for cp in sent:
            cp.start()
        for a in range(na):
            copy(a, pl.ds(HALF * (1 - c), HALF)).wait_recv()
        for cp in sent:
            cp.wait_send()

    return _comm_call(body, "share_with_sibling", halves,
                      [jax.ShapeDtypeStruct((DEPTH,) + h.shape[1:], h.dtype) for h in halves], na)


def _gather_all(v):
    n, W = v.shape

    def body(src, out, send_sems, recv_sems, own_sem):
        x, y, c, _ = _place()
        own = pltpu.make_async_copy(src, out.at[4 * x + 2 * y + c], own_sem)
        own.start()
        flips = [(fx, fy, fc) for fx in (0, 1) for fy in (0, 1) for fc in (0, 1)][1:]
        sent = []
        for k, (fx, fy, fc) in enumerate(flips):
            cp = pltpu.make_async_remote_copy(
                src_ref=src, dst_ref=out.at[4 * x + 2 * y + c], send_sem=send_sems.at[k], recv_sem=recv_sems.at[k],
                device_id=(x ^ fx, y ^ fy, c ^ fc), device_id_type=MESH)
            cp.start()
            sent.append(cp)
        for k, (fx, fy, fc) in enumerate(flips):
            pltpu.make_async_remote_copy(
                src_ref=src, dst_ref=out.at[4 * (x ^ fx) + 2 * (y ^ fy) + (c ^ fc)], send_sem=send_sems.at[k],
                recv_sem=recv_sems.at[k], device_id=(x ^ fx, y ^ fy, c ^ fc), device_id_type=MESH).wait_recv()
        for cp in sent:
            cp.wait_send()
        own.wait()

    return pl.pallas_call(
        body, name="gather_all", in_specs=[HBM_SPEC], out_specs=HBM_SPEC,
        out_shape=jax.ShapeDtypeStruct((8, n, W), v.dtype),
        scratch_shapes=[pltpu.SemaphoreType.DMA((7,)), pltpu.SemaphoreType.DMA((7,)), pltpu.SemaphoreType.DMA(())],
        compiler_params=pltpu.CompilerParams(has_side_effects=True))(v)


def _adamw(w, g, m, v):
    shape = w.shape
    cols = shape[-1]
    w2, g2, m2, v2 = (a.reshape(-1, cols) for a in (w, g, m, v))
    rows = w2.shape[0]
    tr = next((t for t in (1024, 512, 256, 128, 64, 32, 16, 8) if rows % t == 0 and t * cols * 4 <= (2 << 20)), rows)

    def body(w_ref, g_ref, m_ref, v_ref, d_ref, nm_ref, nv_ref):
        g_ = g_ref[...]
        nm = ADAM_B1 * m_ref[...] + (1.0 - ADAM_B1) * g_
        nv = ADAM_B2 * v_ref[...] + (1.0 - ADAM_B2) * (g_ * g_)
        m_hat = nm / (1.0 - ADAM_B1 ** ADAM_STEP)
        v_hat = nv / (1.0 - ADAM_B2 ** ADAM_STEP)
        d_ref[...] = -ADAM_LR * (m_hat / (jnp.sqrt(v_hat) + ADAM_EPS) + ADAM_WD * w_ref[...])
        nm_ref[...] = nm
        nv_ref[...] = nv

    blk = pl.BlockSpec((tr, cols), lambda i: (i, 0))
    outs = pl.pallas_call(
        body, name="adamw", grid=(rows // tr,), in_specs=[blk] * 4, out_specs=[blk] * 3,
        out_shape=[jax.ShapeDtypeStruct((rows, cols), F32)] * 3, compiler_params=_params(("parallel",)))(w2, g2, m2, v2)
    return tuple(o.reshape(shape) for o in outs)


BY_COLUMNS = ("w_in", "w_uq", "w_ukv")


def _chip_part(name, a, k):
    if name == "w_in":
        n = D_IN // 4
        return _model_cols(a, k * n, (k + 1) * n)
    n = a.shape[1 if name in BY_COLUMNS else 0] // 4
    return a[:, k * n:(k + 1) * n] if name in BY_COLUMNS else a[k * n:(k + 1) * n]


def kernel(x, mem, positions, w_in, q_norm_g, w_uq, kv_norm_g, w_ukv, sg_ln_g, sg_ln_b, sg_w, sg_b, w_mem_k, w_mem_v, w_out, ln_g, ln_b, loss_target, m_w_in, m_q_norm_g, m_w_uq, m_kv_norm_g, m_w_ukv, m_sg_ln_g, m_sg_ln_b, m_sg_w, m_sg_b, m_w_mem_k, m_w_mem_v, m_w_out, m_ln_g, m_ln_b, v_w_in, v_q_norm_g, v_w_uq, v_kv_norm_g, v_w_ukv, v_sg_ln_g, v_sg_ln_b, v_sg_w, v_sg_b, v_w_mem_k, v_w_mem_v, v_w_out, v_ln_g, v_ln_b):
    weights = dict(w_in=w_in, q_norm_g=q_norm_g, w_uq=w_uq, kv_norm_g=kv_norm_g, w_ukv=w_ukv, sg_ln_g=sg_ln_g,
                   sg_ln_b=sg_ln_b, sg_w=sg_w, sg_b=sg_b, w_mem_k=w_mem_k, w_mem_v=w_mem_v, w_out=w_out, ln_g=ln_g, ln_b=ln_b)
    mom_m = dict(w_in=m_w_in, q_norm_g=m_q_norm_g, w_uq=m_w_uq, kv_norm_g=m_kv_norm_g, w_ukv=m_w_ukv, sg_ln_g=m_sg_ln_g,
                 sg_ln_b=m_sg_ln_b, sg_w=m_sg_w, sg_b=m_sg_b, w_mem_k=m_w_mem_k, w_mem_v=m_w_mem_v, w_out=m_w_out,
                 ln_g=m_ln_g, ln_b=m_ln_b)
    mom_v = dict(w_in=v_w_in, q_norm_g=v_q_norm_g, w_uq=v_w_uq, kv_norm_g=v_kv_norm_g, w_ukv=v_w_ukv, sg_ln_g=v_sg_ln_g,
                 sg_ln_b=v_sg_ln_b, sg_w=v_sg_w, sg_b=v_sg_b, w_mem_k=v_w_mem_k, w_mem_v=v_w_mem_v, w_out=v_w_out,
                 ln_g=v_ln_g, ln_b=v_ln_b)
    c_idx = lax.axis_index("c").astype(jnp.int32).reshape(1)

    me = 2 * lax.axis_index("x") + lax.axis_index("y")
    shards = [weights[n].astype(BF16) for n in SHARDED]
    by_chip = [lax.dynamic_update_slice(g, s[None], (me, 0, 0, 0)) for g, s in zip(_gather_weights(shards), shards)]
    full = dict((n, weights[n]) for n in SMALL)
    full["w_in"] = by_chip[0]
    for n, g in zip(SHARDED[1:], by_chip[1:]):
        full[n] = jnp.concatenate([g[k] for k in range(4)], axis=2 if n in BY_COLUMNS else 1)

    loss_dev, grad_x, grads = _local_step(x[0], mem[0], positions[0], loss_target[0], full)

    gs = [jnp.stack([jnp.stack([_chip_part(n, g, k) for g in grads[n]]) for k in range(4)]) for n in SHARDED]
    pairs = [_pair_sum(g, o, c_idx) for g, o in zip(gs, _swap_halves(gs))]
    me1 = me.astype(jnp.int32).reshape(1)
    halves = [_chip_sum(p, o, me1) for p, o in zip(pairs, _exchange_chips(pairs))]
    grad_out = {n: lax.dynamic_update_slice(r, h, (HALF * c_idx[0], 0, 0))
                for n, r, h in zip(SHARDED, _share_with_sibling(halves), halves)}

    small_sizes = [weights[n].size for n in SMALL]
    vec = jnp.concatenate([g.reshape(-1) for n in SMALL for g in grads[n]] + [loss_dev[0]])
    n_small = vec.shape[0]
    rows_small = -(-n_small // (8 * FLAT_W)) * 8
    vec = jnp.pad(vec, (0, rows_small * FLAT_W - n_small)).reshape(rows_small, FLAT_W)
    total = _sum_parts(_gather_all(vec), "device_sum").reshape(-1)
    at = 0
    for n, size in zip(SMALL, small_sizes):
        grad_out[n] = total[at:at + size].reshape(weights[n].shape)
        at += size
    loss = total[at]

    names = list(weights)
    upd = {n: _adamw(weights[n], grad_out[n], mom_m[n], mom_v[n]) for n in names}
    return (loss, grad_x[None], *[grad_out[n] for n in names], *[upd[n][0] for n in names],
            *[upd[n][1] for n in names], *[upd[n][2] for n in names])
```

```python
import math

import jax
import jax.numpy as jnp
from jax import lax
from jax.experimental import pallas as pl
from jax.experimental.pallas import tpu as pltpu

F32, BF16 = jnp.float32, jnp.bfloat16

D_MODEL = 2048
DEPTH = 4
CHUNK = 64
MLA_HEADS = 6
MLA_SCALE = 1.0 / math.sqrt(192.0)
SB_HEADS = 4
SB_SCALE = 1.0 / math.sqrt(128.0)
MEM_HEADS = 4
MEM_SCALE = 1.0 / math.sqrt(64.0)
ROPE_THETA = 10000.0
ALPHA = (2.0 * DEPTH) ** 0.25
LN_EPS = 1e-5
RMS_EPS = 1e-6
ADAM_LR, ADAM_B1, ADAM_B2, ADAM_EPS, ADAM_WD, ADAM_STEP = 0.001, 0.9, 0.999, 1e-08, 0.01, 10

ORIG = dict(c_q=(0, 512), c_kv=(512, 256), k_pe=(768, 64), g_a=(832, 768), sg_u=(1600, 512), sg_v=(2112, 512),
            g_b=(2624, 512), sb_q=(3136, 512), sb_k=(3648, 512), sb_v=(4160, 512), g_c=(4672, 512),
            m_q=(5184, 256), g_m=(5440, 256))
D_IN = 5696
PERM_ORDER = ("c_q", "c_kv", "m_q", "sg_u", "sg_v", "sb_q", "sb_k", "sb_v", "g_a", "g_b", "g_c", "g_m", "k_pe")
HP = 5760
CQ, CKV, MQ, SGU, SGV, SBQ, GATE, KPE = 0, 512, 768, 1024, 1536, 2048, 3584, 5632

Q_BLK = 2048
K_BLK = 512
SB_Q_BLK = 512
SB_K_BLK = 256
SB_DEAD = -110.0
LANE = 128
VMEM_LIMIT = 56 * 1024 * 1024

FLAT_W = 1024
SHARDED = ("w_in", "w_uq", "w_ukv", "w_mem_k", "w_mem_v", "w_out")
SMALL = ("q_norm_g", "kv_norm_g", "sg_ln_g", "sg_ln_b", "sg_w", "sg_b", "ln_g", "ln_b")


def _params(sem=None):
    return pltpu.CompilerParams(dimension_semantics=sem, vmem_limit_bytes=VMEM_LIMIT)


def _tile(dim, pref):
    if dim <= pref:
        return dim
    t = (pref // LANE) * LANE
    while t >= LANE:
        if dim % t == 0:
            return t
        t -= LANE
    return dim


def _row_tile(rows, bytes_per_row, budget=8 << 20):
    best = None
    for t in range(8, rows + 1, 8):
        if rows % t == 0 and t * bytes_per_row <= budget:
            best = t
    return best if best else rows


def _dot_nt(a, b):
    return lax.dot_general(a, b, (((1,), (1,)), ((), ())), preferred_element_type=F32)


def _dot_tn(a, b):
    return lax.dot_general(a, b, (((0,), (0,)), ((), ())), preferred_element_type=F32)


def _dot(a, b):
    return jnp.dot(a, b, preferred_element_type=F32)


def _mm(a, b, *, ta=False, tb=False, a_win=None, b_win=None, add=None, add_scale=1.0, out_dtype=F32,
        tm=512, tn=512, tk=512, name="mm"):
    a_off, a_w = a_win if a_win else (0, a.shape[1])
    b_off, b_w = b_win if b_win else (0, b.shape[1])
    (K, M) = (a.shape[0], a_w) if ta else (a_w, a.shape[0])
    (N, Kb) = (b.shape[0], b_w) if tb else (b_w, b.shape[0])
    assert K == Kb, (a.shape, b.shape, ta, tb)
    tm, tn, tk = _tile(M, tm), _tile(N, tn), _tile(K, tk)
    nk = K // tk
    if ta:
        assert a_off % tm == 0
        a_spec = pl.BlockSpec((tk, tm), lambda i, j, k: (k, i + a_off // tm))
    else:
        assert a_off % tk == 0
        a_spec = pl.BlockSpec((tm, tk), lambda i, j, k: (i, k + a_off // tk))
    if tb:
        assert b_off % tk == 0
        b_spec = pl.BlockSpec((tn, tk), lambda i, j, k: (j, k + b_off // tk))
    else:
        assert b_off % tn == 0
        b_spec = pl.BlockSpec((tk, tn), lambda i, j, k: (k, j + b_off // tn))
    o_spec = pl.BlockSpec((tm, tn), lambda i, j, k: (i, j))
    dn = (((0 if ta else 1,), (1 if tb else 0,)), ((), ()))
    has_add = add is not None

    def body(*refs):
        a_ref, b_ref = refs[:2]
        add_ref = refs[2] if has_add else None
        o_ref = refs[3 if has_add else 2]
        part = lax.dot_general(a_ref[...].astype(BF16), b_ref[...].astype(BF16), dn, preferred_element_type=F32)

        def finish(r):
            if has_add:
                r = r + add_scale * add_ref[...]
            o_ref[...] = r.astype(o_ref.dtype)

        if nk == 1:
            finish(part)
            return
        acc_ref = refs[-1]
        k = pl.program_id(2)

        @pl.when(k == 0)
        def _():
            acc_ref[...] = part

        @pl.when(k > 0)
        def _():
            acc_ref[...] += part

        @pl.when(k == nk - 1)
        def _():
            finish(acc_ref[...])

    ins = [a, b] + ([add] if has_add else [])
    specs = [a_spec, b_spec] + ([o_spec] if has_add else [])
    return pl.pallas_call(
        body, name=name, grid=(M // tm, N // tn, nk), in_specs=specs, out_specs=o_spec,
        out_shape=jax.ShapeDtypeStruct((M, N), out_dtype),
        scratch_shapes=[pltpu.VMEM((tm, tn), F32)] if nk > 1 else [],
        compiler_params=_params(("parallel", "parallel", "arbitrary")))(*ins)


GELU_K = math.sqrt(2.0 / math.pi)


def _gelu(x):
    t = jnp.tanh(GELU_K * (x + 0.044715 * (x * x * x)))
    return 0.5 * x * (1.0 + t)


def _gelu_grad(x):
    t = jnp.tanh(GELU_K * (x + 0.044715 * (x * x * x)))
    return 0.5 * (1.0 + t) + 0.5 * x * (1.0 - t * t) * GELU_K * (1.0 + 3.0 * 0.044715 * x * x)


def _rope_swap(t):
    lane = lax.broadcasted_iota(jnp.int32, t.shape, 1)
    return jnp.where(lane < 32, pltpu.roll(t, 96, axis=1), pltpu.roll(t, 32, axis=1))


def _rope(t, c, s):
    return t * c + _rope_swap(t) * s


def _rope_bwd(dt, c, s):
    return dt * c - _rope_swap(dt) * s


def _row_spec(tm, w, cb=0):
    return pl.BlockSpec((tm, w), lambda i: (i, cb))


def _fix_spec(shape):
    return pl.BlockSpec(shape, lambda *_: (0,) * len(shape))


def _rms_fwd(h, off, width, g, name):
    S = h.shape[0]
    tm = _tile(S, 512)

    def body(x_ref, g_ref, o_ref):
        x = x_ref[...]
        r = lax.rsqrt(jnp.mean(x * x, axis=1, keepdims=True) + RMS_EPS)
        o_ref[...] = (x * r * g_ref[...]).astype(BF16)

    return pl.pallas_call(
        body, name=name, grid=(S // tm,), in_specs=[_row_spec(tm, width, off // width), _fix_spec((1, width))],
        out_specs=_row_spec(tm, width), out_shape=jax.ShapeDtypeStruct((S, width), BF16),
        compiler_params=_params(("parallel",)))(h, g.reshape(1, width))


def _rms_bwd(h, off, width, g, dxn, name):
    S = h.shape[0]
    tm = _tile(S, 512)

    def body(x_ref, g_ref, d_ref, dx_ref, dg_ref):
        @pl.when(pl.program_id(0) == 0)
        def _():
            dg_ref[...] = jnp.zeros_like(dg_ref)

        x, d = x_ref[...], d_ref[...]
        r = lax.rsqrt(jnp.mean(x * x, axis=1, keepdims=True) + RMS_EPS)
        gd = d * g_ref[...]
        dx_ref[...] = gd * r - x * (r * r * r) * jnp.mean(gd * x, axis=1, keepdims=True)
        dg_ref[...] += jnp.sum(d * x * r, axis=0, keepdims=True)

    return pl.pallas_call(
        body, name=name, grid=(S // tm,),
        in_specs=[_row_spec(tm, width, off // width), _fix_spec((1, width)), _row_spec(tm, width)],
        out_specs=[_row_spec(tm, width), _fix_spec((1, width))],
        out_shape=[jax.ShapeDtypeStruct((S, width), F32), jax.ShapeDtypeStruct((1, width), F32)],
        compiler_params=_params(("arbitrary",)))(h, g.reshape(1, width), dxn)


def _q_proj(xn, w, rc, rs):
    S = xn.shape[0]
    tm = _tile(S, 512)

    def body(x_ref, w_ref, c_ref, s_ref, q_ref):
        q = _dot(x_ref[...], w_ref[...]) * MLA_SCALE
        q_ref[:, :LANE] = q[:, :LANE].astype(BF16)
        q_ref[:, LANE:] = _rope(q[:, LANE:], c_ref[...], s_ref[...]).astype(BF16)

    return pl.pallas_call(
        body, name="q_proj", grid=(S // tm, MLA_HEADS),
        in_specs=[pl.BlockSpec((tm, 512), lambda i, j: (i, 0)), pl.BlockSpec((512, 256), lambda i, j: (0, j)),
                  pl.BlockSpec((tm, LANE), lambda i, j: (i, 0)), pl.BlockSpec((tm, LANE), lambda i, j: (i, 0))],
        out_specs=pl.BlockSpec((tm, 256), lambda i, j: (i, j)),
        out_shape=jax.ShapeDtypeStruct((S, MLA_HEADS * 256), BF16),
        compiler_params=_params(("parallel", "parallel")))(xn, w, rc, rs)


def _kv_proj(xn, w, h, rc, rs):
    S = xn.shape[0]
    tm = _tile(S, 512)

    def body(x_ref, w_ref, pe_ref, c_ref, s_ref, k_ref, v_ref):
        kv = _dot(x_ref[...], w_ref[...])
        k_ref[:, :LANE] = kv[:, :LANE].astype(BF16)
        k_ref[:, LANE:] = _rope(pe_ref[...], c_ref[...], s_ref[...]).astype(BF16)
        v_ref[...] = kv[:, LANE:].astype(BF16)

    return pl.pallas_call(
        body, name="kv_proj", grid=(S // tm, MLA_HEADS),
        in_specs=[pl.BlockSpec((tm, 256), lambda i, j: (i, 0)), pl.BlockSpec((256, 256), lambda i, j: (0, j)),
                  pl.BlockSpec((tm, LANE), lambda i, j: (i, KPE // LANE)),
                  pl.BlockSpec((tm, LANE), lambda i, j: (i, 0)), pl.BlockSpec((tm, LANE), lambda i, j: (i, 0))],
        out_specs=[pl.BlockSpec((tm, 256), lambda i, j: (i, j)), pl.BlockSpec((tm, LANE), lambda i, j: (i, j))],
        out_shape=[jax.ShapeDtypeStruct((S, MLA_HEADS * 256), BF16), jax.ShapeDtypeStruct((S, MLA_HEADS * LANE), BF16)],
        compiler_params=_params(("parallel", "parallel")))(xn, w, h, rc, rs)


def _q_rope_bwd(dq, rc, rs):
    S = dq.shape[0]
    tm = _tile(S, 512)

    def body(d_ref, c_ref, s_ref, o_ref):
        o_ref[:, :LANE] = d_ref[:, :LANE].astype(BF16)
        o_ref[:, LANE:] = _rope_bwd(d_ref[:, LANE:], c_ref[...], s_ref[...]).astype(BF16)

    return pl.pallas_call(
        body, name="q_rope_bwd", grid=(S // tm, MLA_HEADS),
        in_specs=[pl.BlockSpec((tm, 256), lambda i, j: (i, j)),
                  pl.BlockSpec((tm, LANE), lambda i, j: (i, 0)), pl.BlockSpec((tm, LANE), lambda i, j: (i, 0))],
        out_specs=pl.BlockSpec((tm, 256), lambda i, j: (i, j)),
        out_shape=jax.ShapeDtypeStruct((S, MLA_HEADS * 256), BF16),
        compiler_params=_params(("parallel", "parallel")))(dq, rc, rs)


def _kv_bwd_prep(dk, dv, rc, rs):
    S = dk.shape[1]
    tm = _tile(S, 512)

    def body(dk_ref, dv_ref, c_ref, s_ref, o_ref, pe_ref):
        rot = jnp.zeros((tm, LANE), F32)
        for hh in range(MLA_HEADS):
            o_ref[:, hh * 256:hh * 256 + LANE] = dk_ref[hh, :, :LANE].astype(BF16)
            o_ref[:, hh * 256 + LANE:(hh + 1) * 256] = dv_ref[hh].astype(BF16)
            rot = rot + dk_ref[hh, :, LANE:]
        pe_ref[...] = _rope_bwd(rot, c_ref[...], s_ref[...])

    return pl.pallas_call(
        body, name="kv_bwd_prep", grid=(S // tm,),
        in_specs=[pl.BlockSpec((MLA_HEADS, tm, 256), lambda i: (0, i, 0)),
                  pl.BlockSpec((MLA_HEADS, tm, LANE), lambda i: (0, i, 0)), _row_spec(tm, LANE), _row_spec(tm, LANE)],
        out_specs=[_row_spec(tm, MLA_HEADS * 256), _row_spec(tm, LANE)],
        out_shape=[jax.ShapeDtypeStruct((S, MLA_HEADS * 256), BF16), jax.ShapeDtypeStruct((S, LANE), F32)],
        compiler_params=_params(("parallel",)))(dk, dv, rc, rs)


def _chunk_mask(T):
    row = lax.broadcasted_iota(jnp.int32, (T, T), 0)
    col = lax.broadcasted_iota(jnp.int32, (T, T), 1)
    return (col // CHUNK) <= (row // CHUNK)


def _att_blocks(S, q_blk=None, k_blk=None):
    tq = min(q_blk or Q_BLK, S)
    tk = min(k_blk or K_BLK, tq)
    return tq, tk, tq // tk


def _tail_masks(rows, tk):
    row = lax.broadcasted_iota(jnp.int32, (rows, tk), 0)
    col = lax.broadcasted_iota(jnp.int32, (rows, tk), 1)
    return (col // CHUNK) <= (row // CHUNK), col < row


def _span_masks(tk, r):
    row = lax.broadcasted_iota(jnp.int32, (tk, (r + 1) * tk), 0) + r * tk
    col = lax.broadcasted_iota(jnp.int32, (tk, (r + 1) * tk), 1)
    return (col // CHUNK) <= (row // CHUNK), col < row


def _put_rows(old, new, r0):
    return new if r0 == 0 else jnp.concatenate([old[:r0], new], axis=0)


def _mla_fwd(q, kp, v):
    S = q.shape[0]
    TQ, TK, n = _att_blocks(S)

    def body(q_ref, k_ref, v_ref, o_ref, lse_ref):
        i = pl.program_id(1)

        def update(carry, qb, keys, mask):
            m, l, acc = carry
            s = _dot_nt(qb, k_ref[keys, :])
            if mask is not None:
                s = jnp.where(mask, s, -1e30)
            m_new = jnp.maximum(m, jnp.max(s, axis=1, keepdims=True))
            a = jnp.exp(m - m_new)
            p = jnp.exp(s - m_new)
            return m_new, a * l + jnp.sum(p, axis=1, keepdims=True), a * acc + _dot(p.astype(BF16), v_ref[keys, :])

        carry = (jnp.full((TQ, 1), -1e30, F32), jnp.zeros((TQ, 1), F32), jnp.zeros((TQ, LANE), F32))
        carry = lax.fori_loop(
            0, i * n, lambda j, c: update(c, q_ref[...], pl.ds(pl.multiple_of(j * TK, TK), TK), None), carry)
        for r in range(n):
            rows = slice(r * TK, (r + 1) * TK)
            m, l, acc = update(tuple(c[rows] for c in carry), q_ref[rows, :],
                               pl.ds(pl.multiple_of(i * TQ, TQ), (r + 1) * TK), _span_masks(TK, r)[0])
            o_ref[rows, :] = acc / l
            lse_ref[rows, :] = jnp.broadcast_to(m + jnp.log(l), (TK, LANE))

    return pl.pallas_call(
        body, name="mla_fwd", grid=(MLA_HEADS, S // TQ),
        in_specs=[pl.BlockSpec((TQ, 256), lambda h, i: (i, h)), pl.BlockSpec((S, 256), lambda h, i: (0, h)),
                  pl.BlockSpec((S, LANE), lambda h, i: (0, h))],
        out_specs=[pl.BlockSpec((TQ, LANE), lambda h, i: (i, h)), pl.BlockSpec((TQ, LANE), lambda h, i: (i, h))],
        out_shape=[jax.ShapeDtypeStruct((S, MLA_HEADS * LANE), F32), jax.ShapeDtypeStruct((S, MLA_HEADS * LANE), F32)],
        compiler_params=_params(("parallel", "arbitrary")))(q, kp, v)


def _mla_bwd(q, kp, v, do_cat, o_cat, lse):
    S = q.shape[0]
    TQ, TK, n = _att_blocks(S)
    nq = S // TQ

    def body(q_ref, k_ref, v_ref, do_ref, o_ref, lse_ref, dq_ref, dk_hbm, dv_hbm, dk_acc, dv_acc):
        h, i = pl.program_id(0), pl.program_id(1)

        @pl.when(i == 0)
        def _():
            dk_acc[...] = jnp.zeros_like(dk_acc)
            dv_acc[...] = jnp.zeros_like(dv_acc)

        do32 = do_ref[...]
        dob = do32.astype(BF16)
        delta = jnp.sum(do32 * o_ref[...], axis=1, keepdims=True)
        lse_col = lse_ref[:, :1]

        def blk(j, dq, r0, masked):
            sl = pl.ds(pl.multiple_of(j * TK, TK), TK)
            kb, vb, qb = k_ref[sl, :], v_ref[sl, :], q_ref[r0:, :]
            s = _dot_nt(qb, kb)
            if masked:
                s = jnp.where(_tail_masks(TQ - r0, TK)[0], s, -1e30)
            p = jnp.exp(s - lse_col[r0:])
            ds = (p * (_dot_nt(dob[r0:], vb) - delta[r0:])).astype(BF16)
            dk_acc[sl, :] += _dot_tn(ds, qb)
            dv_acc[sl, :] += _dot_tn(p.astype(BF16), dob[r0:])
            return _put_rows(dq, dq[r0:] + _dot(ds, kb), r0)

        dq = lax.fori_loop(0, i * n, lambda j, c: blk(j, c, 0, False), jnp.zeros((TQ, 256), F32))
        for t in range(n):
            dq = blk(i * n + t, dq, t * TK, True)
        dq_ref[...] = dq * MLA_SCALE

        @pl.when(i == nq - 1)
        def _():
            pltpu.sync_copy(dk_acc, dk_hbm.at[h])
            pltpu.sync_copy(dv_acc, dv_hbm.at[h])

    any_spec = pl.BlockSpec(memory_space=pl.ANY)
    T = TQ
    return pl.pallas_call(
        body, name="mla_bwd", grid=(MLA_HEADS, nq),
        in_specs=[pl.BlockSpec((T, 256), lambda h, i: (i, h)), pl.BlockSpec((S, 256), lambda h, i: (0, h)),
                  pl.BlockSpec((S, LANE), lambda h, i: (0, h)), pl.BlockSpec((T, LANE), lambda h, i: (i, h)),
                  pl.BlockSpec((T, LANE), lambda h, i: (i, h)), pl.BlockSpec((T, LANE), lambda h, i: (i, h))],
        out_specs=[pl.BlockSpec((T, 256), lambda h, i: (i, h)), any_spec, any_spec],
        out_shape=[jax.ShapeDtypeStruct((S, MLA_HEADS * 256), F32), jax.ShapeDtypeStruct((MLA_HEADS, S, 256), F32),
                   jax.ShapeDtypeStruct((MLA_HEADS, S, LANE), F32)],
        scratch_shapes=[pltpu.VMEM((S, 256), F32), pltpu.VMEM((S, LANE), F32)],
        compiler_params=_params(("arbitrary", "arbitrary")))(q, kp, v, do_cat, o_cat, lse)


def _split_dot(x, tri):
    top = lax.bitcast_convert_type(lax.bitcast_convert_type(x, jnp.uint32) & jnp.uint32(0xFFFF0000), F32)
    return _dot(top.astype(BF16), tri) + _dot((x - top).astype(BF16), tri)


def _sb_block(qb, kb, tri, carry, masked):
    z = _dot_nt(qb, kb)
    lb = jnp.minimum(z, 0.0) - jnp.log(1.0 + jnp.exp(-jnp.abs(z)))
    lm = lb - z
    strict = None
    if masked:
        strict = _tail_masks(z.shape[0], z.shape[1])[1]
        lm = jnp.where(strict, lm, 0.0)
    a = jnp.exp(lb + carry + _split_dot(lm, tri))
    if masked:
        a = jnp.where(strict, a, 0.0)
    return a, lb, lm, strict


def _sb_walk(blk, j0, state):
    def alive(c):
        return jnp.logical_and(c[0] >= 0, jnp.max(c[1][0]) > SB_DEAD)

    return lax.while_loop(alive, lambda c: (c[0] - 1, blk(c[0], c[1], 0, False)), (j0, state))[1]


def _triangle(tk):
    row = lax.broadcasted_iota(jnp.int32, (tk, tk), 0)
    col = lax.broadcasted_iota(jnp.int32, (tk, tk), 1)
    return (row > col).astype(BF16)


def _sb_fwd(qkv):
    S = qkv.shape[0]
    TQ, TK, n = _att_blocks(S, SB_Q_BLK, SB_K_BLK)
    T = TQ

    def body(q_ref, k_ref, v_ref, o_ref):
        i = pl.program_id(1)
        tri = _triangle(TK)

        def blk(j, state, r0, masked):
            carry, acc = (c[r0:] for c in state)
            sl = pl.ds(pl.multiple_of(j * TK, TK), TK)
            a, _, lm, _ = _sb_block(q_ref[r0:, :], k_ref[sl, :], tri, carry, masked)
            new = (carry + jnp.sum(lm, axis=1, keepdims=True), acc + _dot(a.astype(BF16), v_ref[sl, :]))
            return tuple(_put_rows(c, u, r0) for c, u in zip(state, new))

        state = (jnp.zeros((TQ, 1), F32), jnp.zeros((TQ, LANE), F32))
        for t in reversed(range(n)):
            state = blk(i * n + t, state, t * TK, True)
        state = _sb_walk(blk, i * n - 1, state)
        o_ref[...] = state[1]

    return pl.pallas_call(
        body, name="sb_fwd", grid=(SB_HEADS, S // T),
        in_specs=[pl.BlockSpec((T, LANE), lambda h, i: (i, h)), pl.BlockSpec((S, LANE), lambda h, i: (0, 4 + h)),
                  pl.BlockSpec((S, LANE), lambda h, i: (0, 8 + h))],
        out_specs=pl.BlockSpec((T, LANE), lambda h, i: (i, h)),
        out_shape=jax.ShapeDtypeStruct((S, SB_HEADS * LANE), F32),
        compiler_params=_params(("parallel", "arbitrary")))(qkv, qkv, qkv)


def _sb_bwd(qkv, do_cat, o_cat, col0):
    S = qkv.shape[0]
    TQ, TK, n = _att_blocks(S, SB_Q_BLK, SB_K_BLK)
    T = TQ
    nq = S // TQ

    def body(q_ref, k_ref, v_ref, do_ref, o_ref, dq_ref, dk_hbm, dv_hbm, dk_acc, dv_acc):
        h, i = pl.program_id(0), pl.program_id(1)

        @pl.when(i == 0)
        def _():
            dk_acc[...] = jnp.zeros_like(dk_acc)
            dv_acc[...] = jnp.zeros_like(dv_acc)

        dob = do_ref[...].astype(BF16)
        tri = _triangle(TK)
        rest0 = jnp.sum(dob.astype(F32) * o_ref[...], axis=1, keepdims=True)

        def blk(j, state, r0, masked):
            carry, rest, dq = (c[r0:] for c in state)
            sl = pl.ds(pl.multiple_of(j * TK, TK), TK)
            kb, vb, qb = k_ref[sl, :], v_ref[sl, :], q_ref[r0:, :]
            a, lb, lm, strict = _sb_block(qb, kb, tri, carry, masked)
            ab = a.astype(BF16)
            e = ab.astype(F32) * _dot_nt(dob[r0:], vb)
            dz = e - jnp.exp(lb) * (rest - _split_dot(e, tri))
            if masked:
                dz = jnp.where(strict, dz, 0.0)
            dzb = dz.astype(BF16)
            dk_acc[sl, :] += _dot_tn(dzb, qb)
            dv_acc[sl, :] += _dot_tn(ab, dob[r0:])
            new = (carry + jnp.sum(lm, axis=1, keepdims=True), rest - jnp.sum(e, axis=1, keepdims=True),
                   dq + _dot(dzb, kb))
            return tuple(_put_rows(c, u, r0) for c, u in zip(state, new))

        state = (jnp.zeros((TQ, 1), F32), rest0, jnp.zeros((TQ, LANE), F32))
        for t in reversed(range(n)):
            state = blk(i * n + t, state, t * TK, True)
        state = _sb_walk(blk, i * n - 1, state)
        dq_ref[...] = state[2] * SB_SCALE

        @pl.when(i == nq - 1)
        def _():
            pltpu.sync_copy(dk_acc, dk_hbm.at[h])
            pltpu.sync_copy(dv_acc, dv_hbm.at[h])

    any_spec = pl.BlockSpec(memory_space=pl.ANY)
    return pl.pallas_call(
        body, name="sb_bwd", grid=(SB_HEADS, nq),
        in_specs=[pl.BlockSpec((T, LANE), lambda h, i: (i, h)), pl.BlockSpec((S, LANE), lambda h, i: (0, 4 + h)),
                  pl.BlockSpec((S, LANE), lambda h, i: (0, 8 + h)),
                  pl.BlockSpec((T, LANE), lambda h, i: (i, col0 + h)), pl.BlockSpec((T, LANE), lambda h, i: (i, col0 + h))],
        out_specs=[pl.BlockSpec((T, LANE), lambda h, i: (i, h)), any_spec, any_spec],
        out_shape=[jax.ShapeDtypeStruct((S, SB_HEADS * LANE), F32), jax.ShapeDtypeStruct((SB_HEADS, S, LANE), F32),
                   jax.ShapeDtypeStruct((SB_HEADS, S, LANE), F32)],
        scratch_shapes=[pltpu.VMEM((S, LANE), F32), pltpu.VMEM((S, LANE), F32)],
        compiler_params=_params(("arbitrary", "arbitrary")))(qkv, qkv, qkv, do_cat, o_cat)


def _mem_probs(q, k_ref, hh):
    lane = lax.broadcasted_iota(jnp.int32, (1, 256), 1) // 64
    msk = lane == hh
    qh = jnp.where(msk, q, 0.0).astype(BF16)
    s = _dot_nt(qh, k_ref[...]) * MEM_SCALE
    p = jnp.exp(s - jnp.max(s, axis=1, keepdims=True))
    return msk, qh, p / jnp.sum(p, axis=1, keepdims=True)


def _mem_fwd(h, mk, mv):
    S = h.shape[0]
    tm = _tile(S, 512)

    def body(q_ref, k_ref, v_ref, o_ref):
        q = q_ref[...]
        out = jnp.zeros((tm, 256), F32)
        for hh in range(MEM_HEADS):
            msk, _, p = _mem_probs(q, k_ref, hh)
            out = out + jnp.where(msk, _dot(p.astype(BF16), v_ref[...]), 0.0)
        o_ref[...] = out

    return pl.pallas_call(
        body, name="mem_fwd", grid=(S // tm,),
        in_specs=[_row_spec(tm, 256, MQ // 256), _fix_spec((256, 256)), _fix_spec((256, 256))],
        out_specs=_row_spec(tm, 256), out_shape=jax.ShapeDtypeStruct((S, 256), F32),
        compiler_params=_params(("parallel",)))(h, mk, mv)


def _mem_bwd(h, mk, mv, do_cat, col0):
    S = h.shape[0]
    tm = _tile(S, 512)

    def body(q_ref, k_ref, v_ref, do_ref, dq_ref, dk_ref, dv_ref):
        @pl.when(pl.program_id(0) == 0)
        def _():
            dk_ref[...] = jnp.zeros_like(dk_ref)
            dv_ref[...] = jnp.zeros_like(dv_ref)

        q, do = q_ref[...], do_ref[...]
        dq = jnp.zeros((tm, 256), F32)
        for hh in range(MEM_HEADS):
            msk, qh, p = _mem_probs(q, k_ref, hh)
            doh = jnp.where(msk, do, 0.0).astype(BF16)
            dp = _dot_nt(doh, v_ref[...])
            ds = (p * (dp - jnp.sum(p * dp, axis=1, keepdims=True)) * MEM_SCALE).astype(BF16)
            dq = dq + jnp.where(msk, _dot(ds, k_ref[...]), 0.0)
            dk_ref[...] += _dot_tn(ds, qh)
            dv_ref[...] += _dot_tn(p.astype(BF16), doh)
        dq_ref[...] = dq

    return pl.pallas_call(
        body, name="mem_bwd", grid=(S // tm,),
        in_specs=[_row_spec(tm, 256, MQ // 256), _fix_spec((256, 256)), _fix_spec((256, 256)),
                  _row_spec(tm, 256, col0 // 256)],
        out_specs=[_row_spec(tm, 256), _fix_spec((256, 256)), _fix_spec((256, 256))],
        out_shape=[jax.ShapeDtypeStruct((S, 256), F32), jax.ShapeDtypeStruct((256, 256), F32),
                   jax.ShapeDtypeStruct((256, 256), F32)],
        compiler_params=_params(("arbitrary",)))(h, mk, mv, do_cat)


SG_T = 128


def _sg_norm(sv, g, b):
    gv = _gelu(sv)
    xc = gv - jnp.mean(gv, axis=1, keepdims=True)
    rstd = lax.rsqrt(jnp.mean(xc * xc, axis=1, keepdims=True) + LN_EPS)
    xhat = xc * rstd
    return xhat, rstd, xhat * g + b


def _sg_fwd(h, lng, lnb, w, bias_t):
    S = h.shape[0]
    tm = _tile(S, 512)

    def body(u_ref, v_ref, g_ref, b_ref, w_ref, bias_ref, o_ref):
        mask = _chunk_mask(SG_T)
        for n in range(tm // SG_T):
            rows = slice(n * SG_T, (n + 1) * SG_T)
            u = _gelu(u_ref[rows, :])
            _, _, vn = _sg_norm(v_ref[rows, :], g_ref[...], b_ref[...])
            vb = vn.astype(BF16)
            for gi in range(4):
                cols = slice(gi * LANE, (gi + 1) * LANE)
                wg = jnp.where(mask, w_ref[gi], 0.0).astype(BF16)
                mixed = _dot(wg, vb[:, cols]) + bias_ref[:, gi:gi + 1]
                o_ref[rows, cols] = u[:, cols] * mixed

    return pl.pallas_call(
        body, name="sg_fwd", grid=(S // tm,),
        in_specs=[_row_spec(tm, 512, SGU // 512), _row_spec(tm, 512, SGV // 512), _fix_spec((1, 512)),
                  _fix_spec((1, 512)), _fix_spec((4, SG_T, SG_T)), _fix_spec((SG_T, 4))],
        out_specs=_row_spec(tm, 512), out_shape=jax.ShapeDtypeStruct((S, 512), F32),
        compiler_params=_params(("parallel",)))(h, h, lng.reshape(1, 512), lnb.reshape(1, 512), w, bias_t)


def _sg_bwd(h, lng, lnb, w, bias_t, do_cat, col0):
    S = h.shape[0]
    tm = _tile(S, 512)
    nsteps = S // tm

    def body(u_ref, v_ref, g_ref, b_ref, w_ref, bias_ref, do0_ref, do1_ref, do2_ref, do3_ref,
             du_ref, dv_ref, dw_ref, dbias_ref, dg_ref, db_ref, dvn_scr, dbias_acc):
        do_refs = (do0_ref, do1_ref, do2_ref, do3_ref)
        step = pl.program_id(0)

        @pl.when(step == 0)
        def _():
            dw_ref[...] = jnp.zeros_like(dw_ref)
            dg_ref[...] = jnp.zeros_like(dg_ref)
            db_ref[...] = jnp.zeros_like(db_ref)
            dbias_acc[...] = jnp.zeros_like(dbias_acc)

        mask = _chunk_mask(SG_T)
        for n in range(tm // SG_T):
            rows = slice(n * SG_T, (n + 1) * SG_T)
            su, sv = u_ref[rows, :], v_ref[rows, :]
            u = _gelu(su)
            xhat, rstd, vn = _sg_norm(sv, g_ref[...], b_ref[...])
            vb = vn.astype(BF16)
            ugrad = _gelu_grad(su)
            for gi in range(4):
                cols = slice(gi * LANE, (gi + 1) * LANE)
                do = do_refs[gi][rows, :]
                wg = jnp.where(mask, w_ref[gi], 0.0).astype(BF16)
                mixed = _dot(wg, vb[:, cols]) + bias_ref[:, gi:gi + 1]
                dmixed = do * u[:, cols]
                dmb = dmixed.astype(BF16)
                du_ref[rows, cols] = do * mixed * ugrad[:, cols]
                dvn_scr[:, cols] = _dot_tn(wg, dmb)
                dw_ref[gi] += jnp.where(mask, _dot_nt(dmb, vb[:, cols]), 0.0)
                dbias_acc[gi] += dmixed
            dvn = dvn_scr[...]
            dg_ref[...] += jnp.sum(dvn * xhat, axis=0, keepdims=True)
            db_ref[...] += jnp.sum(dvn, axis=0, keepdims=True)
            dxh = dvn * g_ref[...]
            dgv = rstd * (dxh - jnp.mean(dxh, axis=1, keepdims=True)
                          - xhat * jnp.mean(dxh * xhat, axis=1, keepdims=True))
            dv_ref[rows, :] = dgv * _gelu_grad(sv)

        @pl.when(step == nsteps - 1)
        def _():
            for gi in range(4):
                dbias_ref[:, gi:gi + 1] = jnp.sum(dbias_acc[gi], axis=1, keepdims=True)

    return pl.pallas_call(
        body, name="sg_bwd", grid=(nsteps,),
        in_specs=[_row_spec(tm, 512, SGU // 512), _row_spec(tm, 512, SGV // 512), _fix_spec((1, 512)),
                  _fix_spec((1, 512)), _fix_spec((4, SG_T, SG_T)), _fix_spec((SG_T, 4))]
                 + [_row_spec(tm, LANE, col0 // LANE + gi) for gi in range(4)],
        out_specs=[_row_spec(tm, 512), _row_spec(tm, 512), _fix_spec((4, SG_T, SG_T)), _fix_spec((SG_T, 4)),
                   _fix_spec((1, 512)), _fix_spec((1, 512))],
        out_shape=[jax.ShapeDtypeStruct((S, 512), F32), jax.ShapeDtypeStruct((S, 512), F32),
                   jax.ShapeDtypeStruct((4, SG_T, SG_T), F32), jax.ShapeDtypeStruct((SG_T, 4), F32),
                   jax.ShapeDtypeStruct((1, 512), F32), jax.ShapeDtypeStruct((1, 512), F32)],
        scratch_shapes=[pltpu.VMEM((SG_T, 512), F32), pltpu.VMEM((4, SG_T, SG_T), F32)],
        compiler_params=_params(("arbitrary",)))(h, h, lng.reshape(1, 512), lnb.reshape(1, 512), w, bias_t,
                                                 do_cat, do_cat, do_cat, do_cat)


def _gate_out_ln(branches, h, w_out, x, g, b):
    S = h.shape[0]
    tm = _tile(S, 256)
    widths = [a.shape[1] for a in branches]

    def body(oa_ref, ob_ref, oc_ref, om_ref, g0_ref, g1_ref, g2_ref, g3_ref, w_ref, x_ref, lg_ref, lb_ref,
             cat_ref, yg_ref, xo_ref, xb_ref, r_ref):
        at = 0
        for ref, width in zip((oa_ref, ob_ref, oc_ref, om_ref), widths):
            cat_ref[:, at:at + width] = ref[...]
            at += width
        for j, g_ref in enumerate((g0_ref, g1_ref, g2_ref, g3_ref)):
            gate = g_ref[...]
            cols = slice(j * 512, (j + 1) * 512)
            yg_ref[:, cols] = (cat_ref[:, cols] * (gate * jax.nn.sigmoid(gate))).astype(BF16)
        r = ALPHA * x_ref[...] + _dot(yg_ref[...], w_ref[...])
        r_ref[...] = r
        xc = r - jnp.mean(r, axis=1, keepdims=True)
        o = xc * lax.rsqrt(jnp.mean(xc * xc, axis=1, keepdims=True) + LN_EPS) * lg_ref[...] + lb_ref[...]
        xo_ref[...] = o
        xb_ref[...] = o.astype(BF16)

    row = _row_spec(tm, D_MODEL)
    return pl.pallas_call(
        body, name="gate_out_ln", grid=(S // tm,),
        in_specs=[_row_spec(tm, width) for width in widths] + [_row_spec(tm, 512, GATE // 512 + j) for j in range(4)]
                 + [_fix_spec((D_MODEL, D_MODEL)), row, _fix_spec((1, D_MODEL)), _fix_spec((1, D_MODEL))],
        out_specs=[row] * 5,
        out_shape=[jax.ShapeDtypeStruct((S, D_MODEL), t) for t in (F32, BF16, F32, BF16, F32)],
        compiler_params=_params(("parallel",)))(*branches, h, h, h, h, w_out, x, g.reshape(1, D_MODEL), b.reshape(1, D_MODEL))


def _out_proj_gate_bwd(dr, w_out, o_cat, h):
    S = h.shape[0]
    tm = _tile(S, 1024)

    def body(dr_ref, w_ref, o_ref, g_ref, do_ref, dg_ref, drb):
        @pl.when(pl.program_id(1) == 0)
        def _():
            drb[...] = dr_ref[...].astype(BF16)

        d = _dot_nt(drb[...], w_ref[...])
        g = g_ref[...]
        sig = jax.nn.sigmoid(g)
        do_ref[...] = d * (g * sig)
        dg_ref[...] = d * o_ref[...] * (sig * (1.0 + g * (1.0 - sig)))

    blk = pl.BlockSpec((tm, 512), lambda i, j: (i, j))
    return pl.pallas_call(
        body, name="d_out_proj_gate", grid=(S // tm, 4),
        in_specs=[pl.BlockSpec((tm, D_MODEL), lambda i, j: (i, 0)), pl.BlockSpec((512, D_MODEL), lambda i, j: (j, 0)),
                  blk, pl.BlockSpec((tm, 512), lambda i, j: (i, GATE // 512 + j))],
        out_specs=[blk, blk],
        out_shape=[jax.ShapeDtypeStruct((S, D_MODEL), F32), jax.ShapeDtypeStruct((S, D_MODEL), F32)],
        scratch_shapes=[pltpu.VMEM((tm, D_MODEL), BF16)],
        compiler_params=_params(("parallel", "arbitrary")))(dr, w_out, o_cat, h)


def _ln_res_bwd(dout, r, g):
    S = r.shape[0]
    tm = _tile(S, 256)

    def body(d_ref, r_ref, g_ref, dr_ref, dg_ref, db_ref):
        @pl.when(pl.program_id(0) == 0)
        def _():
            dg_ref[...] = jnp.zeros_like(dg_ref)
            db_ref[...] = jnp.zeros_like(db_ref)

        d, r = d_ref[...], r_ref[...]
        xc = r - jnp.mean(r, axis=1, keepdims=True)
        rstd = lax.rsqrt(jnp.mean(xc * xc, axis=1, keepdims=True) + LN_EPS)
        xhat = xc * rstd
        dxh = d * g_ref[...]
        dr_ref[...] = rstd * (dxh - jnp.mean(dxh, axis=1, keepdims=True)
                              - xhat * jnp.mean(dxh * xhat, axis=1, keepdims=True))
        dg_ref[...] += jnp.sum(d * xhat, axis=0, keepdims=True)
        db_ref[...] += jnp.sum(d, axis=0, keepdims=True)

    return pl.pallas_call(
        body, name="ln_res_bwd", grid=(S // tm,),
        in_specs=[_row_spec(tm, D_MODEL), _row_spec(tm, D_MODEL), _fix_spec((1, D_MODEL))],
        out_specs=[_row_spec(tm, D_MODEL), _fix_spec((1, D_MODEL)), _fix_spec((1, D_MODEL))],
        out_shape=[jax.ShapeDtypeStruct((S, D_MODEL), F32), jax.ShapeDtypeStruct((1, D_MODEL), F32),
                   jax.ShapeDtypeStruct((1, D_MODEL), F32)],
        compiler_params=_params(("arbitrary",)))(dout, r, g.reshape(1, D_MODEL))


def _loss_head(y, target):
    S = y.shape[0]
    tm = _tile(S, 256)

    def body(y_ref, t_ref, l_ref, d_ref):
        @pl.when(pl.program_id(0) == 0)
        def _():
            l_ref[...] = jnp.zeros_like(l_ref)

        diff = y_ref[...] - t_ref[...]
        d_ref[...] = diff * (1.0 / D_MODEL)
        per_row = jnp.mean(diff * diff, axis=1, keepdims=True)
        l_ref[...] += 0.5 * jnp.sum(per_row, axis=0, keepdims=True)

    return pl.pallas_call(
        body, name="loss_head", grid=(S // tm,), in_specs=[_row_spec(tm, D_MODEL), _row_spec(tm, D_MODEL)],
        out_specs=[_fix_spec((8, LANE)), _row_spec(tm, D_MODEL)],
        out_shape=[jax.ShapeDtypeStruct((8, LANE), F32), jax.ShapeDtypeStruct((S, D_MODEL), F32)],
        compiler_params=_params(("arbitrary",)))(y, target)


def _perm_table():
    table, at = [], 0
    for name in PERM_ORDER:
        start, width = ORIG[name]
        table.append((name, start, width, at))
        at += width
    return table


def _permute_w_in(by_chip):
    wc = by_chip.shape[-1]
    parts = []
    for _, start, width, _ in _perm_table():
        lo = start
        while lo < start + width:
            k = lo // wc
            hi = min(start + width, (k + 1) * wc)
            parts.append(by_chip[k, ..., lo - k * wc:hi - k * wc])
            lo = hi
    parts.append(jnp.zeros(by_chip.shape[1:-1] + (HP - D_IN,), by_chip.dtype))
    return jnp.concatenate(parts, axis=-1)


def _model_cols(wp, lo, hi):
    parts = []
    for _, start, width, at in sorted(_perm_table(), key=lambda t: t[1]):
        a, b = max(lo, start), min(hi, start + width)
        if a < b:
            parts.append(wp[..., at + a - start:at + b - start])
    return jnp.concatenate(parts, axis=-1)


def _rope_tables(positions):
    inv_freq = ROPE_THETA ** (-jnp.arange(0, 64, 2, dtype=F32) / 64)
    ang = positions.astype(F32)[:, None] * inv_freq[None, :]
    cos, sin, zero = jnp.cos(ang), jnp.sin(ang), jnp.zeros((positions.shape[0], 64), F32)
    return jnp.concatenate([cos, cos, zero], axis=1), jnp.concatenate([-sin, sin, zero], axis=1)


def _local_step(x, mem, positions, target, w):
    rc, rs = _rope_tables(positions)
    mem_b = mem.astype(BF16)
    xb = x.astype(BF16)
    w_in_all = _permute_w_in(w["w_in"])
    w_uq_all = jnp.pad(w["w_uq"].reshape(DEPTH, 512, MLA_HEADS, 192),
                       ((0, 0), (0, 0), (0, 0), (0, 64))).reshape(DEPTH, 512, MLA_HEADS * 256)
    saved = []
    for l in range(DEPTH):
        w_in, w_uq, w_ukv = w_in_all[l], w_uq_all[l], w["w_ukv"][l]
        h = _mm(xb, w_in, tm=1024, tn=1152, tk=2048, name="in_proj")
        cq_n = _rms_fwd(h, CQ, 512, w["q_norm_g"][l], "rms_q")
        ckv_n = _rms_fwd(h, CKV, 256, w["kv_norm_g"][l], "rms_kv")
        q = _q_proj(cq_n, w_uq, rc, rs)
        kp, v = _kv_proj(ckv_n, w_ukv, h, rc, rs)
        o_a, lse = _mla_fwd(q, kp, v)
        bias_t = w["sg_b"][l].T
        o_b = _sg_fwd(h, w["sg_ln_g"][l], w["sg_ln_b"][l], w["sg_w"][l], bias_t)
        qkv = jnp.concatenate([h[:, SBQ:SBQ + 512] * SB_SCALE, h[:, SBQ + 512:SBQ + 1536]], axis=1).astype(BF16)
        o_c = _sb_fwd(qkv)
        mk = _mm(mem_b, w["w_mem_k"][l], out_dtype=BF16, name="mem_kv")
        mv = _mm(mem_b, w["w_mem_v"][l], out_dtype=BF16, name="mem_kv")
        o_m = _mem_fwd(h, mk, mv)
        o_cat, yg, x_new, xb_new, r = _gate_out_ln((o_a, o_b, o_c, o_m), h, w["w_out"][l], x, w["ln_g"][l], w["ln_b"][l])
        saved.append(dict(xb=xb, h=h, cq_n=cq_n, ckv_n=ckv_n, q=q, kp=kp, v=v, lse=lse, qkv=qkv, mk=mk, mv=mv,
                          o_cat=o_cat, yg=yg, r=r, w_in=w_in, w_uq=w_uq, w_ukv=w_ukv, bias_t=bias_t))
        x, xb = x_new, xb_new

    loss, dx = _loss_head(x, target)

    grads = {n: [None] * DEPTH for n in SHARDED + SMALL}
    for l in reversed(range(DEPTH)):
        s = saved[l]
        h = s["h"]
        dr, dlg, dlb = _ln_res_bwd(dx, s["r"], w["ln_g"][l])
        grads["ln_g"][l], grads["ln_b"][l] = dlg[0], dlb[0]
        grads["w_out"][l] = _mm(s["yg"], dr, ta=True, tm=1024, tn=1024, tk=2048, name="dw_out")
        do_cat, dgates = _out_proj_gate_bwd(dr, w["w_out"][l], s["o_cat"], h)
        dmq, dmk, dmv = _mem_bwd(h, s["mk"], s["mv"], do_cat, 1792)
        grads["w_mem_k"][l] = _mm(mem_b, dmk, ta=True, name="dw_mem")
        grads["w_mem_v"][l] = _mm(mem_b, dmv, ta=True, name="dw_mem")
        dsq, dsk, dsv = _sb_bwd(s["qkv"], do_cat, s["o_cat"], 1280 // LANE)
        dsk = dsk.transpose(1, 0, 2).reshape(-1, 512)
        dsv = dsv.transpose(1, 0, 2).reshape(-1, 512)
        du, dv, dsgw, dsgb, dsg_g, dsg_b = _sg_bwd(h, w["sg_ln_g"][l], w["sg_ln_b"][l], w["sg_w"][l], s["bias_t"],
                                                   do_cat, 768)
        grads["sg_w"][l], grads["sg_b"][l] = dsgw, dsgb.T
        grads["sg_ln_g"][l], grads["sg_ln_b"][l] = dsg_g[0], dsg_b[0]
        dq, dk, dvv = _mla_bwd(s["q"], s["kp"], s["v"], do_cat, s["o_cat"], s["lse"])
        dq_raw = _q_rope_bwd(dq, rc, rs)
        dkv, dkpe = _kv_bwd_prep(dk, dvv, rc, rs)
        dw_uq = _mm(s["cq_n"], dq_raw, ta=True, tk=1024, name="dw_uq")
        grads["w_uq"][l] = dw_uq.reshape(512, MLA_HEADS, 256)[:, :, :192].reshape(512, MLA_HEADS * 192)
        grads["w_ukv"][l] = _mm(s["ckv_n"], dkv, ta=True, tk=1024, name="dw_ukv")
        dcq_n = _mm(dq_raw, s["w_uq"], tb=True, name="d_cq")
        dckv_n = _mm(dkv, s["w_ukv"], tb=True, name="d_ckv")
        dcq, dqg = _rms_bwd(h, CQ, 512, w["q_norm_g"][l], dcq_n, "rms_q_bwd")
        dckv, dkvg = _rms_bwd(h, CKV, 256, w["kv_norm_g"][l], dckv_n, "rms_kv_bwd")
        grads["q_norm_g"][l], grads["kv_norm_g"][l] = dqg[0], dkvg[0]
        dh = jnp.concatenate([dcq, dckv, dmq, du, dv, dsq, dsk, dsv, dgates, dkpe], axis=1).astype(BF16)
        dw_in = _mm(s["xb"], dh, ta=True, tm=1024, tn=1152, tk=2048, name="dw_in")
        grads["w_in"][l] = dw_in
        dx = _mm(dh, s["w_in"], tb=True, add=dr, add_scale=ALPHA, tm=1024, tn=1024, tk=1920, name="d_in_proj")

    return loss, dx, grads


MESH = pl.DeviceIdType.MESH
HBM_SPEC = pl.BlockSpec(memory_space=pltpu.HBM)


def _place():
    x, y, c = lax.axis_index("x"), lax.axis_index("y"), lax.axis_index("c")
    return x, y, c, [(1 - x, y), (x, 1 - y), (1 - x, 1 - y)]


HALF = DEPTH // 2


def _comm_call(body, name, arrays, out_shapes, n_sems):
    return pl.pallas_call(
        body, name=name, in_specs=[HBM_SPEC] * len(arrays), out_specs=[HBM_SPEC] * len(out_shapes), out_shape=out_shapes,
        scratch_shapes=[pltpu.SemaphoreType.DMA((n_sems,)), pltpu.SemaphoreType.DMA((n_sems,))],
        compiler_params=pltpu.CompilerParams(has_side_effects=True))(*arrays)


def _gather_weights(shards):
    na = len(shards)

    def body(*refs):
        srcs, outs, (send_sems, recv_sems) = refs[:na], refs[na:2 * na], refs[2 * na:]
        x, y, c, chips = _place()
        mine, theirs = pl.ds(HALF * c, HALF), pl.ds(HALF * (1 - c), HALF)

        def copy(a, k, src_ref, chip, layers, to):
            return pltpu.make_async_remote_copy(
                src_ref=src_ref, dst_ref=outs[a].at[chip, layers], send_sem=send_sems.at[6 * a + k],
                recv_sem=recv_sems.at[6 * a + k], device_id=to, device_id_type=MESH)

        sent = [copy(a, j, srcs[a].at[mine], 2 * x + y, mine, (px, py, c))
                for a in range(na) for j, (px, py) in enumerate(chips)]
        for cp in sent:
            cp.start()
        passed = []
        for j, (px, py) in enumerate(chips):
            for a in range(na):
                copy(a, j, srcs[a].at[mine], 2 * px + py, mine, (px, py, c)).wait_recv()
                cp = copy(a, 3 + j, outs[a].at[2 * px + py, mine], 2 * px + py, mine, (x, y, 1 - c))
                cp.start()
                passed.append(cp)
        for j, (px, py) in enumerate(chips):
            for a in range(na):
                copy(a, 3 + j, srcs[a].at[theirs], 2 * px + py, theirs, (x, y, 1 - c)).wait_recv()
        for cp in sent + passed:
            cp.wait_send()

    return _comm_call(body, "gather_weights", shards, [jax.ShapeDtypeStruct((4,) + s.shape, s.dtype) for s in shards], 6 * na)


def _swap_halves(gs):
    na = len(gs)

    def body(*refs):
        srcs, outs, (send_sems, recv_sems) = refs[:na], refs[na:2 * na], refs[2 * na:]
        x, y, c, _ = _place()
        cps = [pltpu.make_async_remote_copy(
            src_ref=srcs[a].at[:, pl.ds(HALF * (1 - c), HALF)], dst_ref=outs[a], send_sem=send_sems.at[a],
            recv_sem=recv_sems.at[a], device_id=(x, y, 1 - c), device_id_type=MESH) for a in range(na)]
        for cp in cps:
            cp.start()
        for cp in cps:
            cp.wait()

    return _comm_call(body, "swap_halves", gs,
                      [jax.ShapeDtypeStruct((4, HALF) + g.shape[2:], g.dtype) for g in gs], na)


def _pair_sum(g, other, c):
    _, _, R, C = g.shape
    tr = _row_tile(R, 3 * C * 4)

    def body(c_ref, a_ref, b_ref, o_ref):
        o_ref[...] = (a_ref[...] + b_ref[...]).astype(BF16)

    blk = pl.BlockSpec((None, None, tr, C), lambda d, l, i, c_ref: (d, l, i, 0))
    return pl.pallas_call(
        body, name="pair_sum",
        grid_spec=pltpu.PrefetchScalarGridSpec(
            num_scalar_prefetch=1, grid=(4, HALF, R // tr),
            in_specs=[pl.BlockSpec((None, None, tr, C), lambda d, l, i, c_ref: (d, HALF * c_ref[0] + l, i, 0)), blk],
            out_specs=blk),
        out_shape=jax.ShapeDtypeStruct((4, HALF, R, C), BF16),
        compiler_params=_params(("parallel", "parallel", "parallel")))(c, g, other)


def _exchange_chips(ps):
    na = len(ps)

    def body(*refs):
        srcs, outs, (send_sems, recv_sems) = refs[:na], refs[na:2 * na], refs[2 * na:]
        x, y, c, chips = _place()
        cps = [pltpu.make_async_remote_copy(
            src_ref=srcs[a].at[2 * px + py], dst_ref=outs[a].at[j], send_sem=send_sems.at[3 * a + j],
            recv_sem=recv_sems.at[3 * a + j], device_id=(px, py, c), device_id_type=MESH)
            for a in range(na) for j, (px, py) in enumerate(chips)]
        for cp in cps:
            cp.start()
        for cp in cps:
            cp.wait()

    return _comm_call(body, "exchange_chips", ps, [jax.ShapeDtypeStruct((3,) + p.shape[1:], p.dtype) for p in ps], 3 * na)


def _chip_sum(p, got, me):
    _, _, R, C = p.shape
    tr = _row_tile(R, 4 * C * 4)

    def body(me_ref, p_ref, g_ref, o_ref):
        acc = p_ref[...].astype(F32)
        for k in range(3):
            acc = acc + g_ref[k].astype(F32)
        o_ref[...] = acc

    return pl.pallas_call(
        body, name="chip_sum",
        grid_spec=pltpu.PrefetchScalarGridSpec(
            num_scalar_prefetch=1, grid=(HALF, R // tr),
            in_specs=[pl.BlockSpec((None, None, tr, C), lambda l, i, me_ref: (me_ref[0], l, i, 0)),
                      pl.BlockSpec((3, None, tr, C), lambda l, i, me_ref: (0, l, i, 0))],
            out_specs=pl.BlockSpec((None, tr, C), lambda l, i, me_ref: (l, i, 0))),
        out_shape=jax.ShapeDtypeStruct((HALF, R, C), F32), compiler_params=_params(("parallel", "parallel")))(me, p, got)


def _sum_parts(t, name):
    n, H, W = t.shape
    th = _row_tile(H, (n + 1) * W * 4)

    def body(t_ref, o_ref):
        acc = t_ref[0]
        for k in range(1, n):
            acc = acc + t_ref[k]
        o_ref[...] = acc

    return pl.pallas_call(
        body, name=name, grid=(H // th,), in_specs=[pl.BlockSpec((n, th, W), lambda i: (0, i, 0))],
        out_specs=pl.BlockSpec((th, W), lambda i: (i, 0)), out_shape=jax.ShapeDtypeStruct((H, W), F32),
        compiler_params=_params(("parallel",)))(t)


def _share_with_sibling(halves):
    na = len(halves)

    def body(*refs):
        srcs, outs, (send_sems, recv_sems) = refs[:na], refs[na:2 * na], refs[2 * na:]
        x, y, c, _ = _place()

        def copy(a, layers):
            return pltpu.make_async_remote_copy(
                src_ref=srcs[a], dst_ref=outs[a].at[layers], send_sem=send_sems.at[a], recv_sem=recv_sems.at[a],
                device_id=(x, y, 1 - c), device_id_type=MESH)

        sent = [copy(a, pl.ds(HALF * c, HALF)) for a in range(na)]
        for cp in sent:
            cp.start()
        for a in range(na):
            copy(a, pl.ds(HALF * (1 - c), HALF)).wait_recv()
        for cp in sent:
            cp.wait_send()

    return _comm_call(body, "share_with_sibling", halves,
                      [jax.ShapeDtypeStruct((DEPTH,) + h.shape[1:], h.dtype) for h in halves], na)


def _gather_all(v):
    n, W = v.shape

    def body(src, out, send_sems, recv_sems, own_sem):
        x, y, c, _ = _place()
        own = pltpu.make_async_copy(src, out.at[4 * x + 2 * y + c], own_sem)
        own.start()
        flips = [(fx, fy, fc) for fx in (0, 1) for fy in (0, 1) for fc in (0, 1)][1:]
        sent = []
        for k, (fx, fy, fc) in enumerate(flips):
            cp = pltpu.make_async_remote_copy(
                src_ref=src, dst_ref=out.at[4 * x + 2 * y + c], send_sem=send_sems.at[k], recv_sem=recv_sems.at[k],
                device_id=(x ^ fx, y ^ fy, c ^ fc), device_id_type=MESH)
            cp.start()
            sent.append(cp)
        for k, (fx, fy, fc) in enumerate(flips):
            pltpu.make_async_remote_copy(
                src_ref=src, dst_ref=out.at[4 * (x ^ fx) + 2 * (y ^ fy) + (c ^ fc)], send_sem=send_sems.at[k],
                recv_sem=recv_sems.at[k], device_id=(x ^ fx, y ^ fy, c ^ fc), device_id_type=MESH).wait_recv()
        for cp in sent:
            cp.wait_send()
        own.wait()

    return pl.pallas_call(
        body, name="gather_all", in_specs=[HBM_SPEC], out_specs=HBM_SPEC,
        out_shape=jax.ShapeDtypeStruct((8, n, W), v.dtype),
        scratch_shapes=[pltpu.SemaphoreType.DMA((7,)), pltpu.SemaphoreType.DMA((7,)), pltpu.SemaphoreType.DMA(())],
        compiler_params=pltpu.CompilerParams(has_side_effects=True))(v)


def _adamw(w, g, m, v):
    shape = w.shape
    cols = shape[-1]
    w2, g2, m2, v2 = (a.reshape(-1, cols) for a in (w, g, m, v))
    rows = w2.shape[0]
    tr = next((t for t in (1024, 512, 256, 128, 64, 32, 16, 8) if rows % t == 0 and t * cols * 4 <= (2 << 20)), rows)

    def body(w_ref, g_ref, m_ref, v_ref, d_ref, nm_ref, nv_ref):
        g_ = g_ref[...]
        nm = ADAM_B1 * m_ref[...] + (1.0 - ADAM_B1) * g_
        nv = ADAM_B2 * v_ref[...] + (1.0 - ADAM_B2) * (g_ * g_)
        m_hat = nm / (1.0 - ADAM_B1 ** ADAM_STEP)
        v_hat = nv / (1.0 - ADAM_B2 ** ADAM_STEP)
        d_ref[...] = -ADAM_LR * (m_hat / (jnp.sqrt(v_hat) + ADAM_EPS) + ADAM_WD * w_ref[...])
        nm_ref[...] = nm
        nv_ref[...] = nv

    blk = pl.BlockSpec((tr, cols), lambda i: (i, 0))
    outs = pl.pallas_call(
        body, name="adamw", grid=(rows // tr,), in_specs=[blk] * 4, out_specs=[blk] * 3,
        out_shape=[jax.ShapeDtypeStruct((rows, cols), F32)] * 3, compiler_params=_params(("parallel",)))(w2, g2, m2, v2)
    return tuple(o.reshape(shape) for o in outs)


BY_COLUMNS = ("w_in", "w_uq", "w_ukv")


def _chip_part(name, a, k):
    if name == "w_in":
        n = D_IN // 4
        return _model_cols(a, k * n, (k + 1) * n)
    n = a.shape[1 if name in BY_COLUMNS else 0] // 4
    return a[:, k * n:(k + 1) * n] if name in BY_COLUMNS else a[k * n:(k + 1) * n]


def kernel(x, mem, positions, w_in, q_norm_g, w_uq, kv_norm_g, w_ukv, sg_ln_g, sg_ln_b, sg_w, sg_b, w_mem_k, w_mem_v, w_out, ln_g, ln_b, loss_target, m_w_in, m_q_norm_g, m_w_uq, m_kv_norm_g, m_w_ukv, m_sg_ln_g, m_sg_ln_b, m_sg_w, m_sg_b, m_w_mem_k, m_w_mem_v, m_w_out, m_ln_g, m_ln_b, v_w_in, v_q_norm_g, v_w_uq, v_kv_norm_g, v_w_ukv, v_sg_ln_g, v_sg_ln_b, v_sg_w, v_sg_b, v_w_mem_k, v_w_mem_v, v_w_out, v_ln_g, v_ln_b):
    weights = dict(w_in=w_in, q_norm_g=q_norm_g, w_uq=w_uq, kv_norm_g=kv_norm_g, w_ukv=w_ukv, sg_ln_g=sg_ln_g,
                   sg_ln_b=sg_ln_b, sg_w=sg_w, sg_b=sg_b, w_mem_k=w_mem_k, w_mem_v=w_mem_v, w_out=w_out, ln_g=ln_g, ln_b=ln_b)
    mom_m = dict(w_in=m_w_in, q_norm_g=m_q_norm_g, w_uq=m_w_uq, kv_norm_g=m_kv_norm_g, w_ukv=m_w_ukv, sg_ln_g=m_sg_ln_g,
                 sg_ln_b=m_sg_ln_b, sg_w=m_sg_w, sg_b=m_sg_b, w_mem_k=m_w_mem_k, w_mem_v=m_w_mem_v, w_out=m_w_out,
                 ln_g=m_ln_g, ln_b=m_ln_b)
    mom_v = dict(w_in=v_w_in, q_norm_g=v_q_norm_g, w_uq=v_w_uq, kv_norm_g=v_kv_norm_g, w_ukv=v_w_ukv, sg_ln_g=v_sg_ln_g,
                 sg_ln_b=v_sg_ln_b, sg_w=v_sg_w, sg_b=v_sg_b, w_mem_k=v_w_mem_k, w_mem_v=v_w_mem_v, w_out=v_w_out,
                 ln_g=v_ln_g, ln_b=v_ln_b)
    c_idx = lax.axis_index("c").astype(jnp.int32).reshape(1)

    me = 2 * lax.axis_index("x") + lax.axis_index("y")
    shards = [weights[n].astype(BF16) for n in SHARDED]
    by_chip = [lax.dynamic_update_slice(g, s[None], (me, 0, 0, 0)) for g, s in zip(_gather_weights(shards), shards)]
    full = dict((n, weights[n]) for n in SMALL)
    full["w_in"] = by_chip[0]
    for n, g in zip(SHARDED[1:], by_chip[1:]):
        full[n] = jnp.concatenate([g[k] for k in range(4)], axis=2 if n in BY_COLUMNS else 1)

    loss_dev, grad_x, grads = _local_step(x[0], mem[0], positions[0], loss_target[0], full)

    gs = [jnp.stack([jnp.stack([_chip_part(n, g, k) for g in grads[n]]) for k in range(4)]) for n in SHARDED]
    pairs = [_pair_sum(g, o, c_idx) for g, o in zip(gs, _swap_halves(gs))]
    me1 = me.astype(jnp.int32).reshape(1)
    halves = [_chip_sum(p, o, me1) for p, o in zip(pairs, _exchange_chips(pairs))]
    grad_out = {n: lax.dynamic_update_slice(r, h, (HALF * c_idx[0], 0, 0))
                for n, r, h in zip(SHARDED, _share_with_sibling(halves), halves)}

    small_sizes = [weights[n].size for n in SMALL]
    vec = jnp.concatenate([g.reshape(-1) for n in SMALL for g in grads[n]] + [loss_dev[0]])
    n_small = vec.shape[0]
    rows_small = -(-n_small // (8 * FLAT_W)) * 8
    vec = jnp.pad(vec, (0, rows_small * FLAT_W - n_small)).reshape(rows_small, FLAT_W)
    total = _sum_parts(_gather_all(vec), "device_sum").reshape(-1)
    at = 0
    for n, size in zip(SMALL, small_sizes):
        grad_out[n] = total[at:at + size].reshape(weights[n].shape)
        at += size
    loss = total[at]

    names = list(weights)
    upd = {n: _adamw(weights[n], grad_out[n], mom_m[n], mom_v[n]) for n in names}
    return (loss, grad_x[None], *[grad_out[n] for n in names], *[upd[n][0] for n in names],
            *[upd[n][1] for n in names], *[upd[n][2] for n in names])
```

```python
import math

import jax
import jax.numpy as jnp
from jax import lax
from jax.experimental import pallas as pl
from jax.experimental.pallas import tpu as pltpu

F32, BF16 = jnp.float32, jnp.bfloat16

D_MODEL = 2048
DEPTH = 4
CHUNK = 64
MLA_HEADS = 6
MLA_SCALE = 1.0 / math.sqrt(192.0)
SB_HEADS = 4
SB_SCALE = 1.0 / math.sqrt(128.0)
MEM_HEADS = 4
MEM_SCALE = 1.0 / math.sqrt(64.0)
ROPE_THETA = 10000.0
ALPHA = (2.0 * DEPTH) ** 0.25
LN_EPS = 1e-5
RMS_EPS = 1e-6
ADAM_LR, ADAM_B1, ADAM_B2, ADAM_EPS, ADAM_WD, ADAM_STEP = 0.001, 0.9, 0.999, 1e-08, 0.01, 10

ORIG = dict(c_q=(0, 512), c_kv=(512, 256), k_pe=(768, 64), g_a=(832, 768), sg_u=(1600, 512), sg_v=(2112, 512),
            g_b=(2624, 512), sb_q=(3136, 512), sb_k=(3648, 512), sb_v=(4160, 512), g_c=(4672, 512),
            m_q=(5184, 256), g_m=(5440, 256))
D_IN = 5696
PERM_ORDER = ("c_q", "c_kv", "m_q", "sg_u", "sg_v", "sb_q", "sb_k", "sb_v", "g_a", "g_b", "g_c", "g_m", "k_pe")
HP = 5760
CQ, CKV, MQ, SGU, SGV, SBQ, GATE, KPE = 0, 512, 768, 1024, 1536, 2048, 3584, 5632

Q_BLK = 2048
K_BLK = 512
SB_Q_BLK = 512
SB_K_BLK = 256
SB_DEAD = -110.0
LANE = 128
VMEM_LIMIT = 56 * 1024 * 1024

FLAT_W = 1024
SHARDED = ("w_in", "w_uq", "w_ukv", "w_mem_k", "w_mem_v", "w_out")
SMALL = ("q_norm_g", "kv_norm_g", "sg_ln_g", "sg_ln_b", "sg_w", "sg_b", "ln_g", "ln_b")


def _params(sem=None):
    return pltpu.CompilerParams(dimension_semantics=sem, vmem_limit_bytes=VMEM_LIMIT)


def _tile(dim, pref):
    if dim <= pref:
        return dim
    t = (pref // LANE) * LANE
    while t >= LANE:
        if dim % t == 0:
            return t
        t -= LANE
    return dim


def _row_tile(rows, bytes_per_row, budget=8 << 20):
    best = None
    for t in range(8, rows + 1, 8):
        if rows % t == 0 and t * bytes_per_row <= budget:
            best = t
    return best if best else rows


def _dot_nt(a, b):
    return lax.dot_general(a, b, (((1,), (1,)), ((), ())), preferred_element_type=F32)


def _dot_tn(a, b):
    return lax.dot_general(a, b, (((0,), (0,)), ((), ())), preferred_element_type=F32)


def _dot(a, b):
    return jnp.dot(a, b, preferred_element_type=F32)


def _mm(a, b, *, ta=False, tb=False, a_win=None, b_win=None, add=None, add_scale=1.0, out_dtype=F32,
        tm=512, tn=512, tk=512, name="mm"):
    a_off, a_w = a_win if a_win else (0, a.shape[1])
    b_off, b_w = b_win if b_win else (0, b.shape[1])
    (K, M) = (a.shape[0], a_w) if ta else (a_w, a.shape[0])
    (N, Kb) = (b.shape[0], b_w) if tb else (b_w, b.shape[0])
    assert K == Kb, (a.shape, b.shape, ta, tb)
    tm, tn, tk = _tile(M, tm), _tile(N, tn), _tile(K, tk)
    nk = K // tk
    if ta:
        assert a_off % tm == 0
        a_spec = pl.BlockSpec((tk, tm), lambda i, j, k: (k, i + a_off // tm))
    else:
        assert a_off % tk == 0
        a_spec = pl.BlockSpec((tm, tk), lambda i, j, k: (i, k + a_off // tk))
    if tb:
        assert b_off % tk == 0
        b_spec = pl.BlockSpec((tn, tk), lambda i, j, k: (j, k + b_off // tk))
    else:
        assert b_off % tn == 0
        b_spec = pl.BlockSpec((tk, tn), lambda i, j, k: (k, j + b_off // tn))
    o_spec = pl.BlockSpec((tm, tn), lambda i, j, k: (i, j))
    dn = (((0 if ta else 1,), (1 if tb else 0,)), ((), ()))
    has_add = add is not None

    def body(*refs):
        a_ref, b_ref = refs[:2]
        add_ref = refs[2] if has_add else None
        o_ref = refs[3 if has_add else 2]
        part = lax.dot_general(a_ref[...].astype(BF16), b_ref[...].astype(BF16), dn, preferred_element_type=F32)

        def finish(r):
            if has_add:
                r = r + add_scale * add_ref[...]
            o_ref[...] = r.astype(o_ref.dtype)

        if nk == 1:
            finish(part)
            return
        acc_ref = refs[-1]
        k = pl.program_id(2)

        @pl.when(k == 0)
        def _():
            acc_ref[...] = part

        @pl.when(k > 0)
        def _():
            acc_ref[...] += part

        @pl.when(k == nk - 1)
        def _():
            finish(acc_ref[...])

    ins = [a, b] + ([add] if has_add else [])
    specs = [a_spec, b_spec] + ([o_spec] if has_add else [])
    return pl.pallas_call(
        body, name=name, grid=(M // tm, N // tn, nk), in_specs=specs, out_specs=o_spec,
        out_shape=jax.ShapeDtypeStruct((M, N), out_dtype),
        scratch_shapes=[pltpu.VMEM((tm, tn), F32)] if nk > 1 else [],
        compiler_params=_params(("parallel", "parallel", "arbitrary")))(*ins)


GELU_K = math.sqrt(2.0 / math.pi)


def _gelu(x):
    t = jnp.tanh(GELU_K * (x + 0.044715 * (x * x * x)))
    return 0.5 * x * (1.0 + t)


def _gelu_grad(x):
    t = jnp.tanh(GELU_K * (x + 0.044715 * (x * x * x)))
    return 0.5 * (1.0 + t) + 0.5 * x * (1.0 - t * t) * GELU_K * (1.0 + 3.0 * 0.044715 * x * x)


def _rope_swap(t):
    lane = lax.broadcasted_iota(jnp.int32, t.shape, 1)
    return jnp.where(lane < 32, pltpu.roll(t, 96, axis=1), pltpu.roll(t, 32, axis=1))


def _rope(t, c, s):
    return t * c + _rope_swap(t) * s


def _rope_bwd(dt, c, s):
    return dt * c - _rope_swap(dt) * s


def _row_spec(tm, w, cb=0):
    return pl.BlockSpec((tm, w), lambda i: (i, cb))


def _fix_spec(shape):
    return pl.BlockSpec(shape, lambda *_: (0,) * len(shape))


def _rms_fwd(h, off, width, g, name):
    S = h.shape[0]
    tm = _tile(S, 512)

    def body(x_ref, g_ref, o_ref):
        x = x_ref[...]
        r = lax.rsqrt(jnp.mean(x * x, axis=1, keepdims=True) + RMS_EPS)
        o_ref[...] = (x * r * g_ref[...]).astype(BF16)

    return pl.pallas_call(
        body, name=name, grid=(S // tm,), in_specs=[_row_spec(tm, width, off // width), _fix_spec((1, width))],
        out_specs=_row_spec(tm, width), out_shape=jax.ShapeDtypeStruct((S, width), BF16),
        compiler_params=_params(("parallel",)))(h, g.reshape(1, width))


def _rms_bwd(h, off, width, g, dxn, name):
    S = h.shape[0]
    tm = _tile(S, 512)

    def body(x_ref, g_ref, d_ref, dx_ref, dg_ref):
        @pl.when(pl.program_id(0) == 0)
        def _():
            dg_ref[...] = jnp.zeros_like(dg_ref)

        x, d = x_ref[...], d_ref[...]
        r = lax.rsqrt(jnp.mean(x * x, axis=1, keepdims=True) + RMS_EPS)
        gd = d * g_ref[...]
        dx_ref[...] = gd * r - x * (r * r * r) * jnp.mean(gd * x, axis=1, keepdims=True)
        dg_ref[...] += jnp.sum(d * x * r, axis=0, keepdims=True)

    return pl.pallas_call(
        body, name=name, grid=(S // tm,),
        in_specs=[_row_spec(tm, width, off // width), _fix_spec((1, width)), _row_spec(tm, width)],
        out_specs=[_row_spec(tm, width), _fix_spec((1, width))],
        out_shape=[jax.ShapeDtypeStruct((S, width), F32), jax.ShapeDtypeStruct((1, width), F32)],
        compiler_params=_params(("arbitrary",)))(h, g.reshape(1, width), dxn)


def _q_proj(xn, w, rc, rs):
    S = xn.shape[0]
    tm = _tile(S, 2048)

    def body(x_ref, w_ref, c_ref, s_ref, q_ref):
        q = _dot(x_ref[...], w_ref[...]) * MLA_SCALE
        q_ref[:, :LANE] = q[:, :LANE].astype(BF16)
        q_ref[:, LANE:] = _rope(q[:, LANE:], c_ref[...], s_ref[...]).astype(BF16)

    return pl.pallas_call(
        body, name="q_proj", grid=(S // tm, MLA_HEADS),
        in_specs=[pl.BlockSpec((tm, 512), lambda i, j: (i, 0)), pl.BlockSpec((512, 256), lambda i, j: (0, j)),
                  pl.BlockSpec((tm, LANE), lambda i, j: (i, 0)), pl.BlockSpec((tm, LANE), lambda i, j: (i, 0))],
        out_specs=pl.BlockSpec((tm, 256), lambda i, j: (i, j)),
        out_shape=jax.ShapeDtypeStruct((S, MLA_HEADS * 256), BF16),
        compiler_params=_params(("parallel", "parallel")))(xn, w, rc, rs)


def _kv_proj(xn, w, h, rc, rs):
    S = xn.shape[0]
    tm = _tile(S, 2048)

    def body(x_ref, w_ref, pe_ref, c_ref, s_ref, k_ref, v_ref):
        kv = _dot(x_ref[...], w_ref[...])
        k_ref[:, :LANE] = kv[:, :LANE].astype(BF16)
        k_ref[:, LANE:] = _rope(pe_ref[...], c_ref[...], s_ref[...]).astype(BF16)
        v_ref[...] = kv[:, LANE:].astype(BF16)

    return pl.pallas_call(
        body, name="kv_proj", grid=(S // tm, MLA_HEADS),
        in_specs=[pl.BlockSpec((tm, 256), lambda i, j: (i, 0)), pl.BlockSpec((256, 256), lambda i, j: (0, j)),
                  pl.BlockSpec((tm, LANE), lambda i, j: (i, KPE // LANE)),
                  pl.BlockSpec((tm, LANE), lambda i, j: (i, 0)), pl.BlockSpec((tm, LANE), lambda i, j: (i, 0))],
        out_specs=[pl.BlockSpec((tm, 256), lambda i, j: (i, j)), pl.BlockSpec((tm, LANE), lambda i, j: (i, j))],
        out_shape=[jax.ShapeDtypeStruct((S, MLA_HEADS * 256), BF16), jax.ShapeDtypeStruct((S, MLA_HEADS * LANE), BF16)],
        compiler_params=_params(("parallel", "parallel")))(xn, w, h, rc, rs)


def _kv_bwd_prep(dk, dv, rc, rs):
    S = dk.shape[1]
    tm = _tile(S, 1024)

    def body(dk_ref, dv_ref, c_ref, s_ref, o_ref, pe_ref):
        rot = jnp.zeros((tm, LANE), F32)
        for hh in range(MLA_HEADS):
            o_ref[:, hh * 256:hh * 256 + LANE] = dk_ref[hh, :, :LANE].astype(BF16)
            o_ref[:, hh * 256 + LANE:(hh + 1) * 256] = dv_ref[hh].astype(BF16)
            rot = rot + dk_ref[hh, :, LANE:]
        pe_ref[...] = _rope_bwd(rot, c_ref[...], s_ref[...])

    return pl.pallas_call(
        body, name="kv_bwd_prep", grid=(S // tm,),
        in_specs=[pl.BlockSpec((MLA_HEADS, tm, 256), lambda i: (0, i, 0)),
                  pl.BlockSpec((MLA_HEADS, tm, LANE), lambda i: (0, i, 0)), _row_spec(tm, LANE), _row_spec(tm, LANE)],
        out_specs=[_row_spec(tm, MLA_HEADS * 256), _row_spec(tm, LANE)],
        out_shape=[jax.ShapeDtypeStruct((S, MLA_HEADS * 256), BF16), jax.ShapeDtypeStruct((S, LANE), F32)],
        compiler_params=_params(("parallel",)))(dk, dv, rc, rs)


def _chunk_mask(T):
    row = lax.broadcasted_iota(jnp.int32, (T, T), 0)
    col = lax.broadcasted_iota(jnp.int32, (T, T), 1)
    return (col // CHUNK) <= (row // CHUNK)


def _att_blocks(S, q_blk=None, k_blk=None):
    tq = min(q_blk or Q_BLK, S)
    tk = min(k_blk or K_BLK, tq)
    return tq, tk, tq // tk


def _tail_masks(rows, tk):
    row = lax.broadcasted_iota(jnp.int32, (rows, tk), 0)
    col = lax.broadcasted_iota(jnp.int32, (rows, tk), 1)
    return (col // CHUNK) <= (row // CHUNK), col < row


def _span_masks(tk, r):
    row = lax.broadcasted_iota(jnp.int32, (tk, (r + 1) * tk), 0) + r * tk
    col = lax.broadcasted_iota(jnp.int32, (tk, (r + 1) * tk), 1)
    return (col // CHUNK) <= (row // CHUNK), col < row


def _put_rows(old, new, r0):
    return new if r0 == 0 else jnp.concatenate([old[:r0], new], axis=0)


def _mla_fwd(q, kp, v):
    S = q.shape[0]
    TQ, TK, n = _att_blocks(S)

    def body(q_ref, k_ref, v_ref, o_ref, lse_ref):
        i = pl.program_id(1)

        def update(carry, qb, keys, mask):
            m, l, acc = carry
            s = _dot_nt(qb, k_ref[keys, :])
            if mask is not None:
                s = jnp.where(mask, s, -1e30)
            m_new = jnp.maximum(m, jnp.max(s, axis=1, keepdims=True))
            a = jnp.exp(m - m_new)
            p = jnp.exp(s - m_new)
            return m_new, a * l + jnp.sum(p, axis=1, keepdims=True), a * acc + _dot(p.astype(BF16), v_ref[keys, :])

        carry = (jnp.full((TQ, 1), -1e30, F32), jnp.zeros((TQ, 1), F32), jnp.zeros((TQ, LANE), F32))
        carry = lax.fori_loop(
            0, i * n, lambda j, c: update(c, q_ref[...], pl.ds(pl.multiple_of(j * TK, TK), TK), None), carry)
        for r in range(n):
            rows = slice(r * TK, (r + 1) * TK)
            m, l, acc = update(tuple(c[rows] for c in carry), q_ref[rows, :],
                               pl.ds(pl.multiple_of(i * TQ, TQ), (r + 1) * TK), _span_masks(TK, r)[0])
            o_ref[rows, :] = acc / l
            lse_ref[rows, :] = jnp.broadcast_to(m + jnp.log(l), (TK, LANE))

    return pl.pallas_call(
        body, name="mla_fwd", grid=(MLA_HEADS, S // TQ),
        in_specs=[pl.BlockSpec((TQ, 256), lambda h, i: (i, h)), pl.BlockSpec((S, 256), lambda h, i: (0, h)),
                  pl.BlockSpec((S, LANE), lambda h, i: (0, h))],
        out_specs=[pl.BlockSpec((TQ, LANE), lambda h, i: (i, h)), pl.BlockSpec((TQ, LANE), lambda h, i: (i, h))],
        out_shape=[jax.ShapeDtypeStruct((S, MLA_HEADS * LANE), F32), jax.ShapeDtypeStruct((S, MLA_HEADS * LANE), F32)],
        compiler_params=_params(("parallel", "arbitrary")))(q, kp, v)


def _mla_bwd(q, kp, v, do_cat, o_cat, lse, rc, rs):
    S = q.shape[0]
    TQ, TK, n = _att_blocks(S)
    nq = S // TQ

    def body(q_ref, k_ref, v_ref, do_ref, o_ref, lse_ref, c_ref, s_ref, dq_ref, dk_hbm, dv_hbm, dk_acc, dv_acc):
        h, i = pl.program_id(0), pl.program_id(1)

        @pl.when(i == 0)
        def _():
            dk_acc[...] = jnp.zeros_like(dk_acc)
            dv_acc[...] = jnp.zeros_like(dv_acc)

        do32 = do_ref[...]
        dob = do32.astype(BF16)
        delta = jnp.sum(do32 * o_ref[...], axis=1, keepdims=True)
        lse_col = lse_ref[:, :1]

        def blk(j, dq, r0, masked):
            sl = pl.ds(pl.multiple_of(j * TK, TK), TK)
            kb, vb, qb = k_ref[sl, :], v_ref[sl, :], q_ref[r0:, :]
            s = _dot_nt(qb, kb)
            if masked:
                s = jnp.where(_tail_masks(TQ - r0, TK)[0], s, -1e30)
            p = jnp.exp(s - lse_col[r0:])
            ds = (p * (_dot_nt(dob[r0:], vb) - delta[r0:])).astype(BF16)
            dk_acc[sl, :] += _dot_tn(ds, qb)
            dv_acc[sl, :] += _dot_tn(p.astype(BF16), dob[r0:])
            return _put_rows(dq, dq[r0:] + _dot(ds, kb), r0)

        dq = lax.fori_loop(0, i * n, lambda j, c: blk(j, c, 0, False), jnp.zeros((TQ, 256), F32))
        for t in range(n):
            dq = blk(i * n + t, dq, t * TK, True)
        dq_ref[:, :LANE] = (dq[:, :LANE] * MLA_SCALE).astype(BF16)
        dq_ref[:, LANE:] = _rope_bwd(dq[:, LANE:] * MLA_SCALE, c_ref[...], s_ref[...]).astype(BF16)

        @pl.when(i == nq - 1)
        def _():
            pltpu.sync_copy(dk_acc, dk_hbm.at[h])
            pltpu.sync_copy(dv_acc, dv_hbm.at[h])

    any_spec = pl.BlockSpec(memory_space=pl.ANY)
    T = TQ
    rows = pl.BlockSpec((T, LANE), lambda h, i: (i, 0))
    return pl.pallas_call(
        body, name="mla_bwd", grid=(MLA_HEADS, nq),
        in_specs=[pl.BlockSpec((T, 256), lambda h, i: (i, h)),
                  pl.BlockSpec((S, 256), lambda h, i: (0, h), pipeline_mode=pl.Buffered(1)),
                  pl.BlockSpec((S, LANE), lambda h, i: (0, h), pipeline_mode=pl.Buffered(1)),
                  pl.BlockSpec((T, LANE), lambda h, i: (i, h)),
                  pl.BlockSpec((T, LANE), lambda h, i: (i, h)), pl.BlockSpec((T, LANE), lambda h, i: (i, h)), rows, rows],
        out_specs=[pl.BlockSpec((T, 256), lambda h, i: (i, h)), any_spec, any_spec],
        out_shape=[jax.ShapeDtypeStruct((S, MLA_HEADS * 256), BF16), jax.ShapeDtypeStruct((MLA_HEADS, S, 256), F32),
                   jax.ShapeDtypeStruct((MLA_HEADS, S, LANE), F32)],
        scratch_shapes=[pltpu.VMEM((S, 256), F32), pltpu.VMEM((S, LANE), F32)],
        compiler_params=_params(("arbitrary", "arbitrary")))(q, kp, v, do_cat, o_cat, lse, rc, rs)


def _split_dot(x, tri):
    top = lax.bitcast_convert_type(lax.bitcast_convert_type(x, jnp.uint32) & jnp.uint32(0xFFFF0000), F32)
    return _dot(top.astype(BF16), tri) + _dot((x - top).astype(BF16), tri)


def _sb_block(qb, kb, tri, carry, masked):
    z = _dot_nt(qb, kb)
    lb = jnp.minimum(z, 0.0) - jnp.log(1.0 + jnp.exp(-jnp.abs(z)))
    lm = lb - z
    strict = None
    if masked:
        strict = _tail_masks(z.shape[0], z.shape[1])[1]
        lm = jnp.where(strict, lm, 0.0)
    a = jnp.exp(lb + carry + _split_dot(lm, tri))
    if masked:
        a = jnp.where(strict, a, 0.0)
    return a, lb, lm, strict


def _sb_walk(blk, j0, state):
    def alive(c):
        return jnp.logical_and(c[0] >= 0, jnp.max(c[1][0]) > SB_DEAD)

    return lax.while_loop(alive, lambda c: (c[0] - 1, blk(c[0], c[1], 0, False)), (j0, state))[1]


def _triangle(tk):
    row = lax.broadcasted_iota(jnp.int32, (tk, tk), 0)
    col = lax.broadcasted_iota(jnp.int32, (tk, tk), 1)
    return (row > col).astype(BF16)


def _sb_fwd(qkv):
    S = qkv.shape[0]
    TQ, TK, n = _att_blocks(S, SB_Q_BLK, SB_K_BLK)
    T = TQ

    def body(q_ref, k_ref, v_ref, o_ref):
        i = pl.program_id(1)
        tri = _triangle(TK)

        def blk(j, state, r0, masked):
            carry, acc = (c[r0:] for c in state)
            sl = pl.ds(pl.multiple_of(j * TK, TK), TK)
            a, _, lm, _ = _sb_block(q_ref[r0:, :], k_ref[sl, :], tri, carry, masked)
            new = (carry + jnp.sum(lm, axis=1, keepdims=True), acc + _dot(a.astype(BF16), v_ref[sl, :]))
            return tuple(_put_rows(c, u, r0) for c, u in zip(state, new))

        state = (jnp.zeros((TQ, 1), F32), jnp.zeros((TQ, LANE), F32))
        for t in reversed(range(n)):
            state = blk(i * n + t, state, t * TK, True)
        state = _sb_walk(blk, i * n - 1, state)
        o_ref[...] = state[1]

    return pl.pallas_call(
        body, name="sb_fwd", grid=(SB_HEADS, S // T),
        in_specs=[pl.BlockSpec((T, LANE), lambda h, i: (i, h)), pl.BlockSpec((S, LANE), lambda h, i: (0, 4 + h)),
                  pl.BlockSpec((S, LANE), lambda h, i: (0, 8 + h))],
        out_specs=pl.BlockSpec((T, LANE), lambda h, i: (i, h)),
        out_shape=jax.ShapeDtypeStruct((S, SB_HEADS * LANE), F32),
        compiler_params=_params(("parallel", "arbitrary")))(qkv, qkv, qkv)


def _sb_bwd(qkv, do_cat, o_cat, col0):
    S = qkv.shape[0]
    TQ, TK, n = _att_blocks(S, SB_Q_BLK, SB_K_BLK)
    T = TQ
    nq = S // TQ

    def body(q_ref, k_ref, v_ref, do_ref, o_ref, dq_ref, dk_hbm, dv_hbm, dk_acc, dv_acc):
        h, i = pl.program_id(0), pl.program_id(1)

        @pl.when(i == 0)
        def _():
            dk_acc[...] = jnp.zeros_like(dk_acc)
            dv_acc[...] = jnp.zeros_like(dv_acc)

        dob = do_ref[...].astype(BF16)
        tri = _triangle(TK)
        rest0 = jnp.sum(dob.astype(F32) * o_ref[...], axis=1, keepdims=True)

        def blk(j, state, r0, masked):
            carry, rest, dq = (c[r0:] for c in state)
            sl = pl.ds(pl.multiple_of(j * TK, TK), TK)
            kb, vb, qb = k_ref[sl, :], v_ref[sl, :], q_ref[r0:, :]
            a, lb, lm, strict = _sb_block(qb, kb, tri, carry, masked)
            ab = a.astype(BF16)
            e = ab.astype(F32) * _dot_nt(dob[r0:], vb)
            dz = e - jnp.exp(lb) * (rest - _split_dot(e, tri))
            if masked:
                dz = jnp.where(strict, dz, 0.0)
            dzb = dz.astype(BF16)
            dk_acc[sl, :] += _dot_tn(dzb, qb)
            dv_acc[sl, :] += _dot_tn(ab, dob[r0:])
            new = (carry + jnp.sum(lm, axis=1, keepdims=True), rest - jnp.sum(e, axis=1, keepdims=True),
                   dq + _dot(dzb, kb))
            return tuple(_put_rows(c, u, r0) for c, u in zip(state, new))

        state = (jnp.zeros((TQ, 1), F32), rest0, jnp.zeros((TQ, LANE), F32))
        for t in reversed(range(n)):
            state = blk(i * n + t, state, t * TK, True)
        state = _sb_walk(blk, i * n - 1, state)
        dq_ref[...] = state[2] * SB_SCALE

        @pl.when(i == nq - 1)
        def _():
            pltpu.sync_copy(dk_acc, dk_hbm.at[h])
            pltpu.sync_copy(dv_acc, dv_hbm.at[h])

    any_spec = pl.BlockSpec(memory_space=pl.ANY)
    return pl.pallas_call(
        body, name="sb_bwd", grid=(SB_HEADS, nq),
        in_specs=[pl.BlockSpec((T, LANE), lambda h, i: (i, h)), pl.BlockSpec((S, LANE), lambda h, i: (0, 4 + h)),
                  pl.BlockSpec((S, LANE), lambda h, i: (0, 8 + h)),
                  pl.BlockSpec((T, LANE), lambda h, i: (i, col0 + h)), pl.BlockSpec((T, LANE), lambda h, i: (i, col0 + h))],
        out_specs=[pl.BlockSpec((T, LANE), lambda h, i: (i, h)), any_spec, any_spec],
        out_shape=[jax.ShapeDtypeStruct((S, SB_HEADS * LANE), F32), jax.ShapeDtypeStruct((SB_HEADS, S, LANE), F32),
                   jax.ShapeDtypeStruct((SB_HEADS, S, LANE), F32)],
        scratch_shapes=[pltpu.VMEM((S, LANE), F32), pltpu.VMEM((S, LANE), F32)],
        compiler_params=_params(("arbitrary", "arbitrary")))(qkv, qkv, qkv, do_cat, o_cat)


def _mem_probs(q, k_ref, hh):
    lane = lax.broadcasted_iota(jnp.int32, (1, 256), 1) // 64
    msk = lane == hh
    qh = jnp.where(msk, q, 0.0).astype(BF16)
    s = _dot_nt(qh, k_ref[...]) * MEM_SCALE
    p = jnp.exp(s - jnp.max(s, axis=1, keepdims=True))
    return msk, qh, p / jnp.sum(p, axis=1, keepdims=True)


def _mem_fwd(h, mk, mv):
    S = h.shape[0]
    tm = _tile(S, 512)

    def body(q_ref, k_ref, v_ref, o_ref):
        q = q_ref[...]
        out = jnp.zeros((tm, 256), F32)
        for hh in range(MEM_HEADS):
            msk, _, p = _mem_probs(q, k_ref, hh)
            out = out + jnp.where(msk, _dot(p.astype(BF16), v_ref[...]), 0.0)
        o_ref[...] = out

    return pl.pallas_call(
        body, name="mem_fwd", grid=(S // tm,),
        in_specs=[_row_spec(tm, 256, MQ // 256), _fix_spec((256, 256)), _fix_spec((256, 256))],
        out_specs=_row_spec(tm, 256), out_shape=jax.ShapeDtypeStruct((S, 256), F32),
        compiler_params=_params(("parallel",)))(h, mk, mv)


def _mem_bwd(h, mk, mv, do_cat, col0):
    S = h.shape[0]
    tm = _tile(S, 512)

    def body(q_ref, k_ref, v_ref, do_ref, dq_ref, dk_ref, dv_ref):
        @pl.when(pl.program_id(0) == 0)
        def _():
            dk_ref[...] = jnp.zeros_like(dk_ref)
            dv_ref[...] = jnp.zeros_like(dv_ref)

        q, do = q_ref[...], do_ref[...]
        dq = jnp.zeros((tm, 256), F32)
        for hh in range(MEM_HEADS):
            msk, qh, p = _mem_probs(q, k_ref, hh)
            doh = jnp.where(msk, do, 0.0).astype(BF16)
            dp = _dot_nt(doh, v_ref[...])
            ds = (p * (dp - jnp.sum(p * dp, axis=1, keepdims=True)) * MEM_SCALE).astype(BF16)
            dq = dq + jnp.where(msk, _dot(ds, k_ref[...]), 0.0)
            dk_ref[...] += _dot_tn(ds, qh)
            dv_ref[...] += _dot_tn(p.astype(BF16), doh)
        dq_ref[...] = dq

    return pl.pallas_call(
        body, name="mem_bwd", grid=(S // tm,),
        in_specs=[_row_spec(tm, 256, MQ // 256), _fix_spec((256, 256)), _fix_spec((256, 256)),
                  _row_spec(tm, 256, col0 // 256)],
        out_specs=[_row_spec(tm, 256), _fix_spec((256, 256)), _fix_spec((256, 256))],
        out_shape=[jax.ShapeDtypeStruct((S, 256), F32), jax.ShapeDtypeStruct((256, 256), F32),
                   jax.ShapeDtypeStruct((256, 256), F32)],
        compiler_params=_params(("arbitrary",)))(h, mk, mv, do_cat)


SG_T = 128


def _sg_norm(sv, g, b):
    gv = _gelu(sv)
    xc = gv - jnp.mean(gv, axis=1, keepdims=True)
    rstd = lax.rsqrt(jnp.mean(xc * xc, axis=1, keepdims=True) + LN_EPS)
    xhat = xc * rstd
    return xhat, rstd, xhat * g + b


def _sg_fwd(h, lng, lnb, w, bias_t):
    S = h.shape[0]
    tm = _tile(S, 512)

    def body(u_ref, v_ref, g_ref, b_ref, w_ref, bias_ref, o_ref):
        mask = _chunk_mask(SG_T)
        for n in range(tm // SG_T):
            rows = slice(n * SG_T, (n + 1) * SG_T)
            u = _gelu(u_ref[rows, :])
            _, _, vn = _sg_norm(v_ref[rows, :], g_ref[...], b_ref[...])
            vb = vn.astype(BF16)
            for gi in range(4):
                cols = slice(gi * LANE, (gi + 1) * LANE)
                wg = jnp.where(mask, w_ref[gi], 0.0).astype(BF16)
                mixed = _dot(wg, vb[:, cols]) + bias_ref[:, gi:gi + 1]
                o_ref[rows, cols] = u[:, cols] * mixed

    return pl.pallas_call(
        body, name="sg_fwd", grid=(S // tm,),
        in_specs=[_row_spec(tm, 512, SGU // 512), _row_spec(tm, 512, SGV // 512), _fix_spec((1, 512)),
                  _fix_spec((1, 512)), _fix_spec((4, SG_T, SG_T)), _fix_spec((SG_T, 4))],
        out_specs=_row_spec(tm, 512), out_shape=jax.ShapeDtypeStruct((S, 512), F32),
        compiler_params=_params(("parallel",)))(h, h, lng.reshape(1, 512), lnb.reshape(1, 512), w, bias_t)


def _sg_bwd(h, lng, lnb, w, bias_t, do_cat, col0):
    S = h.shape[0]
    tm = _tile(S, 512)
    nsteps = S // tm

    def body(u_ref, v_ref, g_ref, b_ref, w_ref, bias_ref, do0_ref, do1_ref, do2_ref, do3_ref,
             du_ref, dv_ref, dw_ref, dbias_ref, dg_ref, db_ref, dvn_scr, dbias_acc):
        do_refs = (do0_ref, do1_ref, do2_ref, do3_ref)
        step = pl.program_id(0)

        @pl.when(step == 0)
        def _():
            dw_ref[...] = jnp.zeros_like(dw_ref)
            dg_ref[...] = jnp.zeros_like(dg_ref)
            db_ref[...] = jnp.zeros_like(db_ref)
            dbias_acc[...] = jnp.zeros_like(dbias_acc)

        mask = _chunk_mask(SG_T)
        for n in range(tm // SG_T):
            rows = slice(n * SG_T, (n + 1) * SG_T)
            su, sv = u_ref[rows, :], v_ref[rows, :]
            u = _gelu(su)
            xhat, rstd, vn = _sg_norm(sv, g_ref[...], b_ref[...])
            vb = vn.astype(BF16)
            ugrad = _gelu_grad(su)
            for gi in range(4):
                cols = slice(gi * LANE, (gi + 1) * LANE)
                do = do_refs[gi][rows, :]
                wg = jnp.where(mask, w_ref[gi], 0.0).astype(BF16)
                mixed = _dot(wg, vb[:, cols]) + bias_ref[:, gi:gi + 1]
                dmixed = do * u[:, cols]
                dmb = dmixed.astype(BF16)
                du_ref[rows, cols] = do * mixed * ugrad[:, cols]
                dvn_scr[:, cols] = _dot_tn(wg, dmb)
                dw_ref[gi] += jnp.where(mask, _dot_nt(dmb, vb[:, cols]), 0.0)
                dbias_acc[gi] += dmixed
            dvn = dvn_scr[...]
            dg_ref[...] += jnp.sum(dvn * xhat, axis=0, keepdims=True)
            db_ref[...] += jnp.sum(dvn, axis=0, keepdims=True)
            dxh = dvn * g_ref[...]
            dgv = rstd * (dxh - jnp.mean(dxh, axis=1, keepdims=True)
                          - xhat * jnp.mean(dxh * xhat, axis=1, keepdims=True))
            dv_ref[rows, :] = dgv * _gelu_grad(sv)

        @pl.when(step == nsteps - 1)
        def _():
            for gi in range(4):
                dbias_ref[:, gi:gi + 1] = jnp.sum(dbias_acc[gi], axis=1, keepdims=True)

    return pl.pallas_call(
        body, name="sg_bwd", grid=(nsteps,),
        in_specs=[_row_spec(tm, 512, SGU // 512), _row_spec(tm, 512, SGV // 512), _fix_spec((1, 512)),
                  _fix_spec((1, 512)), _fix_spec((4, SG_T, SG_T)), _fix_spec((SG_T, 4))]
                 + [_row_spec(tm, LANE, col0 // LANE + gi) for gi in range(4)],
        out_specs=[_row_spec(tm, 512), _row_spec(tm, 512), _fix_spec((4, SG_T, SG_T)), _fix_spec((SG_T, 4)),
                   _fix_spec((1, 512)), _fix_spec((1, 512))],
        out_shape=[jax.ShapeDtypeStruct((S, 512), F32), jax.ShapeDtypeStruct((S, 512), F32),
                   jax.ShapeDtypeStruct((4, SG_T, SG_T), F32), jax.ShapeDtypeStruct((SG_T, 4), F32),
                   jax.ShapeDtypeStruct((1, 512), F32), jax.ShapeDtypeStruct((1, 512), F32)],
        scratch_shapes=[pltpu.VMEM((SG_T, 512), F32), pltpu.VMEM((4, SG_T, SG_T), F32)],
        compiler_params=_params(("arbitrary",)))(h, h, lng.reshape(1, 512), lnb.reshape(1, 512), w, bias_t,
                                                 do_cat, do_cat, do_cat, do_cat)


def _gate_out_ln(branches, h, w_out, x, g, b):
    S = h.shape[0]
    tm = _tile(S, 256)
    widths = [a.shape[1] for a in branches]

    def body(oa_ref, ob_ref, oc_ref, om_ref, g0_ref, g1_ref, g2_ref, g3_ref, w_ref, x_ref, lg_ref, lb_ref,
             cat_ref, yg_ref, xo_ref, xb_ref, r_ref):
        at = 0
        for ref, width in zip((oa_ref, ob_ref, oc_ref, om_ref), widths):
            cat_ref[:, at:at + width] = ref[...]
            at += width
        for j, g_ref in enumerate((g0_ref, g1_ref, g2_ref, g3_ref)):
            gate = g_ref[...]
            cols = slice(j * 512, (j + 1) * 512)
            yg_ref[:, cols] = (cat_ref[:, cols] * (gate * jax.nn.sigmoid(gate))).astype(BF16)
        r = ALPHA * x_ref[...] + _dot(yg_ref[...], w_ref[...])
        r_ref[...] = r
        xc = r - jnp.mean(r, axis=1, keepdims=True)
        o = xc * lax.rsqrt(jnp.mean(xc * xc, axis=1, keepdims=True) + LN_EPS) * lg_ref[...] + lb_ref[...]
        xo_ref[...] = o
        xb_ref[...] = o.astype(BF16)

    row = _row_spec(tm, D_MODEL)
    return pl.pallas_call(
        body, name="gate_out_ln", grid=(S // tm,),
        in_specs=[_row_spec(tm, width) for width in widths] + [_row_spec(tm, 512, GATE // 512 + j) for j in range(4)]
                 + [_fix_spec((D_MODEL, D_MODEL)), row, _fix_spec((1, D_MODEL)), _fix_spec((1, D_MODEL))],
        out_specs=[row] * 5,
        out_shape=[jax.ShapeDtypeStruct((S, D_MODEL), t) for t in (F32, BF16, F32, BF16, F32)],
        compiler_params=_params(("parallel",)))(*branches, h, h, h, h, w_out, x, g.reshape(1, D_MODEL), b.reshape(1, D_MODEL))


def _out_proj_gate_bwd(dr, w_out, o_cat, h):
    S = h.shape[0]
    tm = _tile(S, 1024)

    def body(dr_ref, w_ref, o_ref, g_ref, do_ref, dg_ref, drb):
        @pl.when(pl.program_id(1) == 0)
        def _():
            drb[...] = dr_ref[...].astype(BF16)

        d = _dot_nt(drb[...], w_ref[...])
        g = g_ref[...]
        sig = jax.nn.sigmoid(g)
        do_ref[...] = d * (g * sig)
        dg_ref[...] = d * o_ref[...] * (sig * (1.0 + g * (1.0 - sig)))

    blk = pl.BlockSpec((tm, 512), lambda i, j: (i, j))
    return pl.pallas_call(
        body, name="d_out_proj_gate", grid=(S // tm, 4),
        in_specs=[pl.BlockSpec((tm, D_MODEL), lambda i, j: (i, 0)), pl.BlockSpec((512, D_MODEL), lambda i, j: (j, 0)),
                  blk, pl.BlockSpec((tm, 512), lambda i, j: (i, GATE // 512 + j))],
        out_specs=[blk, blk],
        out_shape=[jax.ShapeDtypeStruct((S, D_MODEL), F32), jax.ShapeDtypeStruct((S, D_MODEL), F32)],
        scratch_shapes=[pltpu.VMEM((tm, D_MODEL), BF16)],
        compiler_params=_params(("parallel", "arbitrary")))(dr, w_out, o_cat, h)


def _ln_res_bwd(dout, r, g):
    S = r.shape[0]
    tm = _tile(S, 256)

    def body(d_ref, r_ref, g_ref, dr_ref, dg_ref, db_ref):
        @pl.when(pl.program_id(0) == 0)
        def _():
            dg_ref[...] = jnp.zeros_like(dg_ref)
            db_ref[...] = jnp.zeros_like(db_ref)

        d, r = d_ref[...], r_ref[...]
        xc = r - jnp.mean(r, axis=1, keepdims=True)
        rstd = lax.rsqrt(jnp.mean(xc * xc, axis=1, keepdims=True) + LN_EPS)
        xhat = xc * rstd
        dxh = d * g_ref[...]
        dr_ref[...] = rstd * (dxh - jnp.mean(dxh, axis=1, keepdims=True)
                              - xhat * jnp.mean(dxh * xhat, axis=1, keepdims=True))
        dg_ref[...] += jnp.sum(d * xhat, axis=0, keepdims=True)
        db_ref[...] += jnp.sum(d, axis=0, keepdims=True)

    return pl.pallas_call(
        body, name="ln_res_bwd", grid=(S // tm,),
        in_specs=[_row_spec(tm, D_MODEL), _row_spec(tm, D_MODEL), _fix_spec((1, D_MODEL))],
        out_specs=[_row_spec(tm, D_MODEL), _fix_spec((1, D_MODEL)), _fix_spec((1, D_MODEL))],
        out_shape=[jax.ShapeDtypeStruct((S, D_MODEL), F32), jax.ShapeDtypeStruct((1, D_MODEL), F32),
                   jax.ShapeDtypeStruct((1, D_MODEL), F32)],
        compiler_params=_params(("arbitrary",)))(dout, r, g.reshape(1, D_MODEL))


def _loss_head(y, target):
    S = y.shape[0]
    tm = _tile(S, 256)

    def body(y_ref, t_ref, l_ref, d_ref):
        @pl.when(pl.program_id(0) == 0)
        def _():
            l_ref[...] = jnp.zeros_like(l_ref)

        diff = y_ref[...] - t_ref[...]
        d_ref[...] = diff * (1.0 / D_MODEL)
        per_row = jnp.mean(diff * diff, axis=1, keepdims=True)
        l_ref[...] += 0.5 * jnp.sum(per_row, axis=0, keepdims=True)

    return pl.pallas_call(
        body, name="loss_head", grid=(S // tm,), in_specs=[_row_spec(tm, D_MODEL), _row_spec(tm, D_MODEL)],
        out_specs=[_fix_spec((8, LANE)), _row_spec(tm, D_MODEL)],
        out_shape=[jax.ShapeDtypeStruct((8, LANE), F32), jax.ShapeDtypeStruct((S, D_MODEL), F32)],
        compiler_params=_params(("arbitrary",)))(y, target)


def _perm_table():
    table, at = [], 0
    for name in PERM_ORDER:
        start, width = ORIG[name]
        table.append((name, start, width, at))
        at += width
    return table


def _permute_w_in(by_chip):
    wc = by_chip.shape[-1]
    parts = []
    for _, start, width, _ in _perm_table():
        lo = start
        while lo < start + width:
            k = lo // wc
            hi = min(start + width, (k + 1) * wc)
            parts.append(by_chip[k, ..., lo - k * wc:hi - k * wc])
            lo = hi
    parts.append(jnp.zeros(by_chip.shape[1:-1] + (HP - D_IN,), by_chip.dtype))
    return jnp.concatenate(parts, axis=-1)


def _model_cols(wp, lo, hi):
    parts = []
    for _, start, width, at in sorted(_perm_table(), key=lambda t: t[1]):
        a, b = max(lo, start), min(hi, start + width)
        if a < b:
            parts.append(wp[..., at + a - start:at + b - start])
    return jnp.concatenate(parts, axis=-1)


def _rope_tables(positions):
    inv_freq = ROPE_THETA ** (-jnp.arange(0, 64, 2, dtype=F32) / 64)
    ang = positions.astype(F32)[:, None] * inv_freq[None, :]
    cos, sin, zero = jnp.cos(ang), jnp.sin(ang), jnp.zeros((positions.shape[0], 64), F32)
    return jnp.concatenate([cos, cos, zero], axis=1), jnp.concatenate([-sin, sin, zero], axis=1)


def _local_step(x, mem, positions, target, w):
    rc, rs = _rope_tables(positions)
    mem_b = mem.astype(BF16)
    xb = x.astype(BF16)
    w_in_all = _permute_w_in(w["w_in"])
    w_uq_all = jnp.pad(w["w_uq"].reshape(DEPTH, 512, MLA_HEADS, 192),
                       ((0, 0), (0, 0), (0, 0), (0, 64))).reshape(DEPTH, 512, MLA_HEADS * 256)
    saved = []
    for l in range(DEPTH):
        w_in, w_uq, w_ukv = w_in_all[l], w_uq_all[l], w["w_ukv"][l]
        h = _mm(xb, w_in, tm=1024, tn=1152, tk=2048, name="in_proj")
        cq_n = _rms_fwd(h, CQ, 512, w["q_norm_g"][l], "rms_q")
        ckv_n = _rms_fwd(h, CKV, 256, w["kv_norm_g"][l], "rms_kv")
        q = _q_proj(cq_n, w_uq, rc, rs)
        kp, v = _kv_proj(ckv_n, w_ukv, h, rc, rs)
        o_a, lse = _mla_fwd(q, kp, v)
        bias_t = w["sg_b"][l].T
        o_b = _sg_fwd(h, w["sg_ln_g"][l], w["sg_ln_b"][l], w["sg_w"][l], bias_t)
        qkv = jnp.concatenate([h[:, SBQ:SBQ + 512] * SB_SCALE, h[:, SBQ + 512:SBQ + 1536]], axis=1).astype(BF16)
        o_c = _sb_fwd(qkv)
        mk = _mm(mem_b, w["w_mem_k"][l], out_dtype=BF16, name="mem_kv")
        mv = _mm(mem_b, w["w_mem_v"][l], out_dtype=BF16, name="mem_kv")
        o_m = _mem_fwd(h, mk, mv)
        o_cat, yg, x_new, xb_new, r = _gate_out_ln((o_a, o_b, o_c, o_m), h, w["w_out"][l], x, w["ln_g"][l], w["ln_b"][l])
        saved.append(dict(xb=xb, h=h, cq_n=cq_n, ckv_n=ckv_n, q=q, kp=kp, v=v, lse=lse, qkv=qkv, mk=mk, mv=mv,
                          o_cat=o_cat, yg=yg, r=r, w_in=w_in, w_uq=w_uq, w_ukv=w_ukv, bias_t=bias_t))
        x, xb = x_new, xb_new

    loss, dx = _loss_head(x, target)

    grads = {n: [None] * DEPTH for n in SHARDED + SMALL}
    for l in reversed(range(DEPTH)):
        s = saved[l]
        h = s["h"]
        dr, dlg, dlb = _ln_res_bwd(dx, s["r"], w["ln_g"][l])
        grads["ln_g"][l], grads["ln_b"][l] = dlg[0], dlb[0]
        grads["w_out"][l] = _mm(s["yg"], dr, ta=True, tm=1024, tn=1024, tk=2048, name="dw_out")
        do_cat, dgates = _out_proj_gate_bwd(dr, w["w_out"][l], s["o_cat"], h)
        dmq, dmk, dmv = _mem_bwd(h, s["mk"], s["mv"], do_cat, 1792)
        grads["w_mem_k"][l] = _mm(mem_b, dmk, ta=True, name="dw_mem")
        grads["w_mem_v"][l] = _mm(mem_b, dmv, ta=True, name="dw_mem")
        dsq, dsk, dsv = _sb_bwd(s["qkv"], do_cat, s["o_cat"], 1280 // LANE)
        dsk = dsk.transpose(1, 0, 2).reshape(-1, 512)
        dsv = dsv.transpose(1, 0, 2).reshape(-1, 512)
        du, dv, dsgw, dsgb, dsg_g, dsg_b = _sg_bwd(h, w["sg_ln_g"][l], w["sg_ln_b"][l], w["sg_w"][l], s["bias_t"],
                                                   do_cat, 768)
        grads["sg_w"][l], grads["sg_b"][l] = dsgw, dsgb.T
        grads["sg_ln_g"][l], grads["sg_ln_b"][l] = dsg_g[0], dsg_b[0]
        dq_raw, dk, dvv = _mla_bwd(s["q"], s["kp"], s["v"], do_cat, s["o_cat"], s["lse"], rc, rs)
        dkv, dkpe = _kv_bwd_prep(dk, dvv, rc, rs)
        dw_uq = _mm(s["cq_n"], dq_raw, ta=True, tk=1024, name="dw_uq")
        grads["w_uq"][l] = dw_uq.reshape(512, MLA_HEADS, 256)[:, :, :192].reshape(512, MLA_HEADS * 192)
        grads["w_ukv"][l] = _mm(s["ckv_n"], dkv, ta=True, tk=1024, name="dw_ukv")
        dcq_n = _mm(dq_raw, s["w_uq"], tb=True, name="d_cq")
        dckv_n = _mm(dkv, s["w_ukv"], tb=True, name="d_ckv")
        dcq, dqg = _rms_bwd(h, CQ, 512, w["q_norm_g"][l], dcq_n, "rms_q_bwd")
        dckv, dkvg = _rms_bwd(h, CKV, 256, w["kv_norm_g"][l], dckv_n, "rms_kv_bwd")
        grads["q_norm_g"][l], grads["kv_norm_g"][l] = dqg[0], dkvg[0]
        dh = jnp.concatenate([dcq, dckv, dmq, du, dv, dsq, dsk, dsv, dgates, dkpe], axis=1).astype(BF16)
        dw_in = _mm(s["xb"], dh, ta=True, tm=1024, tn=1152, tk=2048, name="dw_in")
        grads["w_in"][l] = dw_in
        dx = _mm(dh, s["w_in"], tb=True, add=dr, add_scale=ALPHA, tm=1024, tn=1024, tk=1920, name="d_in_proj")

    return loss, dx, grads


MESH = pl.DeviceIdType.MESH
HBM_SPEC = pl.BlockSpec(memory_space=pltpu.HBM)


def _place():
    x, y, c = lax.axis_index("x"), lax.axis_index("y"), lax.axis_index("c")
    return x, y, c, [(1 - x, y), (x, 1 - y), (1 - x, 1 - y)]


HALF = DEPTH // 2


def _comm_call(body, name, arrays, out_shapes, n_sems):
    return pl.pallas_call(
        body, name=name, in_specs=[HBM_SPEC] * len(arrays), out_specs=[HBM_SPEC] * len(out_shapes), out_shape=out_shapes,
        scratch_shapes=[pltpu.SemaphoreType.DMA((n_sems,)), pltpu.SemaphoreType.DMA((n_sems,))],
        compiler_params=pltpu.CompilerParams(has_side_effects=True))(*arrays)


def _gather_weights(shards):
    na = len(shards)

    def body(*refs):
        srcs, outs, (send_sems, recv_sems) = refs[:na], refs[na:2 * na], refs[2 * na:]
        x, y, c, chips = _place()
        mine, theirs = pl.ds(HALF * c, HALF), pl.ds(HALF * (1 - c), HALF)

        def copy(a, k, src_ref, chip, layers, to):
            return pltpu.make_async_remote_copy(
                src_ref=src_ref, dst_ref=outs[a].at[chip, layers], send_sem=send_sems.at[6 * a + k],
                recv_sem=recv_sems.at[6 * a + k], device_id=to, device_id_type=MESH)

        sent = [copy(a, j, srcs[a].at[mine], 2 * x + y, mine, (px, py, c))
                for a in range(na) for j, (px, py) in enumerate(chips)]
        for cp in sent:
            cp.start()
        passed = []
        for j, (px, py) in enumerate(chips):
            for a in range(na):
                copy(a, j, srcs[a].at[mine], 2 * px + py, mine, (px, py, c)).wait_recv()
                cp = copy(a, 3 + j, outs[a].at[2 * px + py, mine], 2 * px + py, mine, (x, y, 1 - c))
                cp.start()
                passed.append(cp)
        for j, (px, py) in enumerate(chips):
            for a in range(na):
                copy(a, 3 + j, srcs[a].at[theirs], 2 * px + py, theirs, (x, y, 1 - c)).wait_recv()
        for cp in sent + passed:
            cp.wait_send()

    return _comm_call(body, "gather_weights", shards, [jax.ShapeDtypeStruct((4,) + s.shape, s.dtype) for s in shards], 6 * na)


def _swap_halves(gs):
    na = len(gs)

    def body(*refs):
        srcs, outs, (send_sems, recv_sems) = refs[:na], refs[na:2 * na], refs[2 * na:]
        x, y, c, _ = _place()
        cps = [pltpu.make_async_remote_copy(
            src_ref=srcs[a].at[:, pl.ds(HALF * (1 - c), HALF)], dst_ref=outs[a], send_sem=send_sems.at[a],
            recv_sem=recv_sems.at[a], device_id=(x, y, 1 - c), device_id_type=MESH) for a in range(na)]
        for cp in cps:
            cp.start()
        for cp in cps:
            cp.wait()

    return _comm_call(body, "swap_halves", gs,
                      [jax.ShapeDtypeStruct((4, HALF) + g.shape[2:], g.dtype) for g in gs], na)


def _pair_sum(g, other, c):
    _, _, R, C = g.shape
    tr = _row_tile(R, 3 * C * 4)

    def body(c_ref, a_ref, b_ref, o_ref):
        o_ref[...] = (a_ref[...] + b_ref[...]).astype(BF16)

    blk = pl.BlockSpec((None, None, tr, C), lambda d, l, i, c_ref: (d, l, i, 0))
    return pl.pallas_call(
        body, name="pair_sum",
        grid_spec=pltpu.PrefetchScalarGridSpec(
            num_scalar_prefetch=1, grid=(4, HALF, R // tr),
            in_specs=[pl.BlockSpec((None, None, tr, C), lambda d, l, i, c_ref: (d, HALF * c_ref[0] + l, i, 0)), blk],
            out_specs=blk),
        out_shape=jax.ShapeDtypeStruct((4, HALF, R, C), BF16),
        compiler_params=_params(("parallel", "parallel", "parallel")))(c, g, other)


def _exchange_chips(ps):
    na = len(ps)

    def body(*refs):
        srcs, outs, (send_sems, recv_sems) = refs[:na], refs[na:2 * na], refs[2 * na:]
        x, y, c, chips = _place()
        cps = [pltpu.make_async_remote_copy(
            src_ref=srcs[a].at[2 * px + py], dst_ref=outs[a].at[j], send_sem=send_sems.at[3 * a + j],
            recv_sem=recv_sems.at[3 * a + j], device_id=(px, py, c), device_id_type=MESH)
            for a in range(na) for j, (px, py) in enumerate(chips)]
        for cp in cps:
            cp.start()
        for cp in cps:
            cp.wait()

    return _comm_call(body, "exchange_chips", ps, [jax.ShapeDtypeStruct((3,) + p.shape[1:], p.dtype) for p in ps], 3 * na)


def _chip_sum(p, got, me):
    _, _, R, C = p.shape
    tr = _row_tile(R, 4 * C * 4)

    def body(me_ref, p_ref, g_ref, o_ref):
        acc = p_ref[...].astype(F32)
        for k in range(3):
            acc = acc + g_ref[k].astype(F32)
        o_ref[...] = acc

    return pl.pallas_call(
        body, name="chip_sum",
        grid_spec=pltpu.PrefetchScalarGridSpec(
            num_scalar_prefetch=1, grid=(HALF, R // tr),
            in_specs=[pl.BlockSpec((None, None, tr, C), lambda l, i, me_ref: (me_ref[0], l, i, 0)),
                      pl.BlockSpec((3, None, tr, C), lambda l, i, me_ref: (0, l, i, 0))],
            out_specs=pl.BlockSpec((None, tr, C), lambda l, i, me_ref: (l, i, 0))),
        out_shape=jax.ShapeDtypeStruct((HALF, R, C), F32), compiler_params=_params(("parallel", "parallel")))(me, p, got)


def _sum_parts(t, name):
    n, H, W = t.shape
    th = _row_tile(H, (n + 1) * W * 4)

    def body(t_ref, o_ref):
        acc = t_ref[0]
        for k in range(1, n):
            acc = acc + t_ref[k]
        o_ref[...] = acc

    return pl.pallas_call(
        body, name=name, grid=(H // th,), in_specs=[pl.BlockSpec((n, th, W), lambda i: (0, i, 0))],
        out_specs=pl.BlockSpec((th, W), lambda i: (i, 0)), out_shape=jax.ShapeDtypeStruct((H, W), F32),
        compiler_params=_params(("parallel",)))(t)


def _share_with_sibling(halves):
    na = len(halves)

    def body(*refs):
        srcs, outs, (send_sems, recv_sems) = refs[:na], refs[na:2 * na], refs[2 * na:]
        x, y, c, _ = _place()

        def copy(a, layers):
            return pltpu.make_async_remote_copy(
                src_ref=srcs[a], dst_ref=outs[a].at[layers], send_sem=send_sems.at[a], recv_sem=recv_sems.at[a],
                device_id=(x, y, 1 - c), device_id_type=MESH)

        sent = [copy(a, pl.ds(HALF * c, HALF)) for a in range(na)]
        for cp in sent:
            cp.start()
        for a in range(na):
            copy(a, pl.ds(HALF * (1 - c), HALF)).wait_recv()
        for cp in sent:
            cp.wait_send()

    return _comm_call(body, "share_with_sibling", halves,
                      [jax.ShapeDtypeStruct((DEPTH,) + h.shape[1:], h.dtype) for h in halves], na)


def _gather_all(v):
    n, W = v.shape

    def body(src, out, send_sems, recv_sems, own_sem):
        x, y, c, _ = _place()
        own = pltpu.make_async_copy(src, out.at[4 * x + 2 * y + c], own_sem)
        own.start()
        flips = [(fx, fy, fc) for fx in (0, 1) for fy in (0, 1) for fc in (0, 1)][1:]
        sent = []
        for k, (fx, fy, fc) in enumerate(flips):
            cp = pltpu.make_async_remote_copy(
                src_ref=src, dst_ref=out.at[4 * x + 2 * y + c], send_sem=send_sems.at[k], recv_sem=recv_sems.at[k],
                device_id=(x ^ fx, y ^ fy, c ^ fc), device_id_type=MESH)
            cp.start()
            sent.append(cp)
        for k, (fx, fy, fc) in enumerate(flips):
            pltpu.make_async_remote_copy(
                src_ref=src, dst_ref=out.at[4 * (x ^ fx) + 2 * (y ^ fy) + (c ^ fc)], send_sem=send_sems.at[k],
                recv_sem=recv_sems.at[k], device_id=(x ^ fx, y ^ fy, c ^ fc), device_id_type=MESH).wait_recv()
        for cp in sent:
            cp.wait_send()
        own.wait()

    return pl.pallas_call(
        body, name="gather_all", in_specs=[HBM_SPEC], out_specs=HBM_SPEC,
        out_shape=jax.ShapeDtypeStruct((8, n, W), v.dtype),
        scratch_shapes=[pltpu.SemaphoreType.DMA((7,)), pltpu.SemaphoreType.DMA((7,)), pltpu.SemaphoreType.DMA(())],
        compiler_params=pltpu.CompilerParams(has_side_effects=True))(v)


def _adamw(w, g, m, v):
    shape = w.shape
    cols = shape[-1]
    w2, g2, m2, v2 = (a.reshape(-1, cols) for a in (w, g, m, v))
    rows = w2.shape[0]
    tr = next((t for t in (1024, 512, 256, 128, 64, 32, 16, 8) if rows % t == 0 and t * cols * 4 <= (2 << 20)), rows)

    def body(w_ref, g_ref, m_ref, v_ref, d_ref, nm_ref, nv_ref):
        g_ = g_ref[...]
        nm = ADAM_B1 * m_ref[...] + (1.0 - ADAM_B1) * g_
        nv = ADAM_B2 * v_ref[...] + (1.0 - ADAM_B2) * (g_ * g_)
        m_hat = nm / (1.0 - ADAM_B1 ** ADAM_STEP)
        v_hat = nv / (1.0 - ADAM_B2 ** ADAM_STEP)
        d_ref[...] = -ADAM_LR * (m_hat / (jnp.sqrt(v_hat) + ADAM_EPS) + ADAM_WD * w_ref[...])
        nm_ref[...] = nm
        nv_ref[...] = nv

    blk = pl.BlockSpec((tr, cols), lambda i: (i, 0))
    outs = pl.pallas_call(
        body, name="adamw", grid=(rows // tr,), in_specs=[blk] * 4, out_specs=[blk] * 3,
        out_shape=[jax.ShapeDtypeStruct((rows, cols), F32)] * 3, compiler_params=_params(("parallel",)))(w2, g2, m2, v2)
    return tuple(o.reshape(shape) for o in outs)


BY_COLUMNS = ("w_in", "w_uq", "w_ukv")


def _chip_part(name, a, k):
    if name == "w_in":
        n = D_IN // 4
        return _model_cols(a, k * n, (k + 1) * n)
    n = a.shape[1 if name in BY_COLUMNS else 0] // 4
    return a[:, k * n:(k + 1) * n] if name in BY_COLUMNS else a[k * n:(k + 1) * n]


def kernel(x, mem, positions, w_in, q_norm_g, w_uq, kv_norm_g, w_ukv, sg_ln_g, sg_ln_b, sg_w, sg_b, w_mem_k, w_mem_v, w_out, ln_g, ln_b, loss_target, m_w_in, m_q_norm_g, m_w_uq, m_kv_norm_g, m_w_ukv, m_sg_ln_g, m_sg_ln_b, m_sg_w, m_sg_b, m_w_mem_k, m_w_mem_v, m_w_out, m_ln_g, m_ln_b, v_w_in, v_q_norm_g, v_w_uq, v_kv_norm_g, v_w_ukv, v_sg_ln_g, v_sg_ln_b, v_sg_w, v_sg_b, v_w_mem_k, v_w_mem_v, v_w_out, v_ln_g, v_ln_b):
    weights = dict(w_in=w_in, q_norm_g=q_norm_g, w_uq=w_uq, kv_norm_g=kv_norm_g, w_ukv=w_ukv, sg_ln_g=sg_ln_g,
                   sg_ln_b=sg_ln_b, sg_w=sg_w, sg_b=sg_b, w_mem_k=w_mem_k, w_mem_v=w_mem_v, w_out=w_out, ln_g=ln_g, ln_b=ln_b)
    mom_m = dict(w_in=m_w_in, q_norm_g=m_q_norm_g, w_uq=m_w_uq, kv_norm_g=m_kv_norm_g, w_ukv=m_w_ukv, sg_ln_g=m_sg_ln_g,
                 sg_ln_b=m_sg_ln_b, sg_w=m_sg_w, sg_b=m_sg_b, w_mem_k=m_w_mem_k, w_mem_v=m_w_mem_v, w_out=m_w_out,
                 ln_g=m_ln_g, ln_b=m_ln_b)
    mom_v = dict(w_in=v_w_in, q_norm_g=v_q_norm_g, w_uq=v_w_uq, kv_norm_g=v_kv_norm_g, w_ukv=v_w_ukv, sg_ln_g=v_sg_ln_g,
                 sg_ln_b=v_sg_ln_b, sg_w=v_sg_w, sg_b=v_sg_b, w_mem_k=v_w_mem_k, w_mem_v=v_w_mem_v, w_out=v_w_out,
                 ln_g=v_ln_g, ln_b=v_ln_b)
    c_idx = lax.axis_index("c").astype(jnp.int32).reshape(1)

    me = 2 * lax.axis_index("x") + lax.axis_index("y")
    shards = [weights[n].astype(BF16) for n in SHARDED]
    by_chip = [lax.dynamic_update_slice(g, s[None], (me, 0, 0, 0)) for g, s in zip(_gather_weights(shards), shards)]
    full = dict((n, weights[n]) for n in SMALL)
    full["w_in"] = by_chip[0]
    for n, g in zip(SHARDED[1:], by_chip[1:]):
        full[n] = jnp.concatenate([g[k] for k in range(4)], axis=2 if n in BY_COLUMNS else 1)

    loss_dev, grad_x, grads = _local_step(x[0], mem[0], positions[0], loss_target[0], full)

    gs = [jnp.stack([jnp.stack([_chip_part(n, g, k) for g in grads[n]]) for k in range(4)]) for n in SHARDED]
    pairs = [_pair_sum(g, o, c_idx) for g, o in zip(gs, _swap_halves(gs))]
    me1 = me.astype(jnp.int32).reshape(1)
    halves = [_chip_sum(p, o, me1) for p, o in zip(pairs, _exchange_chips(pairs))]
    grad_out = {n: lax.dynamic_update_slice(r, h, (HALF * c_idx[0], 0, 0))
                for n, r, h in zip(SHARDED, _share_with_sibling(halves), halves)}

    small_sizes = [weights[n].size for n in SMALL]
    vec = jnp.concatenate([g.reshape(-1) for n in SMALL for g in grads[n]] + [loss_dev[0]])
    n_small = vec.shape[0]
    rows_small = -(-n_small // (8 * FLAT_W)) * 8
    vec = jnp.pad(vec, (0, rows_small * FLAT_W - n_small)).reshape(rows_small, FLAT_W)
    total = _sum_parts(_gather_all(vec), "device_sum").reshape(-1)
    at = 0
    for n, size in zip(SMALL, small_sizes):
        grad_out[n] = total[at:at + size].reshape(weights[n].shape)
        at += size
    loss = total[at]

    names = list(weights)
    upd = {n: _adamw(weights[n], grad_out[n], mom_m[n], mom_v[n]) for n in names}
    return (loss, grad_x[None], *[grad_out[n] for n in names], *[upd[n][0] for n in names],
            *[upd[n][1] for n in names], *[upd[n][2] for n in names])
```

```python
import math

import jax
import jax.numpy as jnp
from jax import lax
from jax.experimental import pallas as pl
from jax.experimental.pallas import tpu as pltpu

F32, BF16 = jnp.float32, jnp.bfloat16

D_MODEL = 2048
DEPTH = 4
CHUNK = 64
MLA_HEADS = 6
MLA_SCALE = 1.0 / math.sqrt(192.0)
SB_HEADS = 4
SB_SCALE = 1.0 / math.sqrt(128.0)
MEM_HEADS = 4
MEM_SCALE = 1.0 / math.sqrt(64.0)
ROPE_THETA = 10000.0
ALPHA = (2.0 * DEPTH) ** 0.25
LN_EPS = 1e-5
RMS_EPS = 1e-6
ADAM_LR, ADAM_B1, ADAM_B2, ADAM_EPS, ADAM_WD, ADAM_STEP = 0.001, 0.9, 0.999, 1e-08, 0.01, 10

ORIG = dict(c_q=(0, 512), c_kv=(512, 256), k_pe=(768, 64), g_a=(832, 768), sg_u=(1600, 512), sg_v=(2112, 512),
            g_b=(2624, 512), sb_q=(3136, 512), sb_k=(3648, 512), sb_v=(4160, 512), g_c=(4672, 512),
            m_q=(5184, 256), g_m=(5440, 256))
D_IN = 5696
PERM_ORDER = ("c_q", "c_kv", "m_q", "sg_u", "sg_v", "sb_q", "sb_k", "sb_v", "g_a", "g_b", "g_c", "g_m", "k_pe")
HP = 5760
CQ, CKV, MQ, SGU, SGV, SBQ, GATE, KPE = 0, 512, 768, 1024, 1536, 2048, 3584, 5632

Q_BLK = 2048
K_BLK = 512
MLA_FWD_K_BLK = 1024
SB_Q_BLK = 512
SB_K_BLK = 256
SB_DEAD = -110.0
LANE = 128
VMEM_LIMIT = 56 * 1024 * 1024

FLAT_W = 1024
SHARDED = ("w_in", "w_uq", "w_ukv", "w_mem_k", "w_mem_v", "w_out")
SMALL = ("q_norm_g", "kv_norm_g", "sg_ln_g", "sg_ln_b", "sg_w", "sg_b", "ln_g", "ln_b")


def _params(sem=None):
    return pltpu.CompilerParams(dimension_semantics=sem, vmem_limit_bytes=VMEM_LIMIT)


def _tile(dim, pref):
    if dim <= pref:
        return dim
    t = (pref // LANE) * LANE
    while t >= LANE:
        if dim % t == 0:
            return t
        t -= LANE
    return dim


def _row_tile(rows, bytes_per_row, budget=8 << 20):
    best = None
    for t in range(8, rows + 1, 8):
        if rows % t == 0 and t * bytes_per_row <= budget:
            best = t
    return best if best else rows


def _dot_nt(a, b):
    return lax.dot_general(a, b, (((1,), (1,)), ((), ())), preferred_element_type=F32)


def _dot_tn(a, b):
    return lax.dot_general(a, b, (((0,), (0,)), ((), ())), preferred_element_type=F32)


def _dot(a, b):
    return jnp.dot(a, b, preferred_element_type=F32)


def _mm(a, b, *, ta=False, tb=False, a_win=None, b_win=None, add=None, add_scale=1.0, out_dtype=F32,
        tm=512, tn=512, tk=512, name="mm"):
    a_off, a_w = a_win if a_win else (0, a.shape[1])
    b_off, b_w = b_win if b_win else (0, b.shape[1])
    (K, M) = (a.shape[0], a_w) if ta else (a_w, a.shape[0])
    (N, Kb) = (b.shape[0], b_w) if tb else (b_w, b.shape[0])
    assert K == Kb, (a.shape, b.shape, ta, tb)
    tm, tn, tk = _tile(M, tm), _tile(N, tn), _tile(K, tk)
    nk = K // tk
    if ta:
        assert a_off % tm == 0
        a_spec = pl.BlockSpec((tk, tm), lambda i, j, k: (k, i + a_off // tm))
    else:
        assert a_off % tk == 0
        a_spec = pl.BlockSpec((tm, tk), lambda i, j, k: (i, k + a_off // tk))
    if tb:
        assert b_off % tk == 0
        b_spec = pl.BlockSpec((tn, tk), lambda i, j, k: (j, k + b_off // tk))
    else:
        assert b_off % tn == 0
        b_spec = pl.BlockSpec((tk, tn), lambda i, j, k: (k, j + b_off // tn))
    o_spec = pl.BlockSpec((tm, tn), lambda i, j, k: (i, j))
    dn = (((0 if ta else 1,), (1 if tb else 0,)), ((), ()))
    has_add = add is not None

    def body(*refs):
        a_ref, b_ref = refs[:2]
        add_ref = refs[2] if has_add else None
        o_ref = refs[3 if has_add else 2]
        part = lax.dot_general(a_ref[...].astype(BF16), b_ref[...].astype(BF16), dn, preferred_element_type=F32)

        def finish(r):
            if has_add:
                r = r + add_scale * add_ref[...]
            o_ref[...] = r.astype(o_ref.dtype)

        if nk == 1:
            finish(part)
            return
        acc_ref = refs[-1]
        k = pl.program_id(2)

        @pl.when(k == 0)
        def _():
            acc_ref[...] = part

        @pl.when(k > 0)
        def _():
            acc_ref[...] += part

        @pl.when(k == nk - 1)
        def _():
            finish(acc_ref[...])

    ins = [a, b] + ([add] if has_add else [])
    specs = [a_spec, b_spec] + ([o_spec] if has_add else [])
    return pl.pallas_call(
        body, name=name, grid=(M // tm, N // tn, nk), in_specs=specs, out_specs=o_spec,
        out_shape=jax.ShapeDtypeStruct((M, N), out_dtype),
        scratch_shapes=[pltpu.VMEM((tm, tn), F32)] if nk > 1 else [],
        compiler_params=_params(("parallel", "parallel", "arbitrary")))(*ins)


GELU_K = math.sqrt(2.0 / math.pi)


def _gelu(x):
    t = jnp.tanh(GELU_K * (x + 0.044715 * (x * x * x)))
    return 0.5 * x * (1.0 + t)


def _gelu_grad(x):
    t = jnp.tanh(GELU_K * (x + 0.044715 * (x * x * x)))
    return 0.5 * (1.0 + t) + 0.5 * x * (1.0 - t * t) * GELU_K * (1.0 + 3.0 * 0.044715 * x * x)


def _rope_swap(t):
    lane = lax.broadcasted_iota(jnp.int32, t.shape, 1)
    return jnp.where(lane < 32, pltpu.roll(t, 96, axis=1), pltpu.roll(t, 32, axis=1))


def _rope(t, c, s):
    return t * c + _rope_swap(t) * s


def _rope_bwd(dt, c, s):
    return dt * c - _rope_swap(dt) * s


def _row_spec(tm, w, cb=0):
    return pl.BlockSpec((tm, w), lambda i: (i, cb))


def _fix_spec(shape):
    return pl.BlockSpec(shape, lambda *_: (0,) * len(shape))


def _rms_fwd(h, off, width, g, name):
    S = h.shape[0]
    tm = _tile(S, 2048)

    def body(x_ref, g_ref, o_ref):
        x = x_ref[...]
        r = lax.rsqrt(jnp.mean(x * x, axis=1, keepdims=True) + RMS_EPS)
        o_ref[...] = (x * r * g_ref[...]).astype(BF16)

    return pl.pallas_call(
        body, name=name, grid=(S // tm,), in_specs=[_row_spec(tm, width, off // width), _fix_spec((1, width))],
        out_specs=_row_spec(tm, width), out_shape=jax.ShapeDtypeStruct((S, width), BF16),
        compiler_params=_params(("parallel",)))(h, g.reshape(1, width))


def _rms_bwd(h, off, width, g, dxn, name):
    S = h.shape[0]
    tm = _tile(S, 2048)

    def body(x_ref, g_ref, d_ref, dx_ref, dg_ref):
        @pl.when(pl.program_id(0) == 0)
        def _():
            dg_ref[...] = jnp.zeros_like(dg_ref)

        x, d = x_ref[...], d_ref[...]
        r = lax.rsqrt(jnp.mean(x * x, axis=1, keepdims=True) + RMS_EPS)
        gd = d * g_ref[...]
        dx_ref[...] = gd * r - x * (r * r * r) * jnp.mean(gd * x, axis=1, keepdims=True)
        dg_ref[...] += jnp.sum(d * x * r, axis=0, keepdims=True)

    return pl.pallas_call(
        body, name=name, grid=(S // tm,),
        in_specs=[_row_spec(tm, width, off // width), _fix_spec((1, width)), _row_spec(tm, width)],
        out_specs=[_row_spec(tm, width), _fix_spec((1, width))],
        out_shape=[jax.ShapeDtypeStruct((S, width), F32), jax.ShapeDtypeStruct((1, width), F32)],
        compiler_params=_params(("arbitrary",)))(h, g.reshape(1, width), dxn)


def _q_proj(xn, w, rc, rs):
    S = xn.shape[0]
    tm = _tile(S, 2048)

    def body(x_ref, w_ref, c_ref, s_ref, q_ref):
        q = _dot(x_ref[...], w_ref[...]) * MLA_SCALE
        q_ref[:, :LANE] = q[:, :LANE].astype(BF16)
        q_ref[:, LANE:] = _rope(q[:, LANE:], c_ref[...], s_ref[...]).astype(BF16)

    return pl.pallas_call(
        body, name="q_proj", grid=(S // tm, MLA_HEADS),
        in_specs=[pl.BlockSpec((tm, 512), lambda i, j: (i, 0)), pl.BlockSpec((512, 256), lambda i, j: (0, j)),
                  pl.BlockSpec((tm, LANE), lambda i, j: (i, 0)), pl.BlockSpec((tm, LANE), lambda i, j: (i, 0))],
        out_specs=pl.BlockSpec((tm, 256), lambda i, j: (i, j)),
        out_shape=jax.ShapeDtypeStruct((S, MLA_HEADS * 256), BF16),
        compiler_params=_params(("parallel", "parallel")))(xn, w, rc, rs)


def _kv_proj(xn, w, h, rc, rs):
    S = xn.shape[0]
    tm = _tile(S, 2048)

    def body(x_ref, w_ref, pe_ref, c_ref, s_ref, k_ref, v_ref):
        kv = _dot(x_ref[...], w_ref[...])
        k_ref[:, :LANE] = kv[:, :LANE].astype(BF16)
        k_ref[:, LANE:] = _rope(pe_ref[...], c_ref[...], s_ref[...]).astype(BF16)
        v_ref[...] = kv[:, LANE:].astype(BF16)

    return pl.pallas_call(
        body, name="kv_proj", grid=(S // tm, MLA_HEADS),
        in_specs=[pl.BlockSpec((tm, 256), lambda i, j: (i, 0)), pl.BlockSpec((256, 256), lambda i, j: (0, j)),
                  pl.BlockSpec((tm, LANE), lambda i, j: (i, KPE // LANE)),
                  pl.BlockSpec((tm, LANE), lambda i, j: (i, 0)), pl.BlockSpec((tm, LANE), lambda i, j: (i, 0))],
        out_specs=[pl.BlockSpec((tm, 256), lambda i, j: (i, j)), pl.BlockSpec((tm, LANE), lambda i, j: (i, j))],
        out_shape=[jax.ShapeDtypeStruct((S, MLA_HEADS * 256), BF16), jax.ShapeDtypeStruct((S, MLA_HEADS * LANE), BF16)],
        compiler_params=_params(("parallel", "parallel")))(xn, w, h, rc, rs)


def _kv_bwd_prep(dk, dv, rc, rs):
    S = dk.shape[1]
    tm = _tile(S, 1024)

    def body(dk_ref, dv_ref, c_ref, s_ref, o_ref, pe_ref):
        rot = jnp.zeros((tm, LANE), F32)
        for hh in range(MLA_HEADS):
            o_ref[:, hh * 256:hh * 256 + LANE] = dk_ref[hh, :, :LANE].astype(BF16)
            o_ref[:, hh * 256 + LANE:(hh + 1) * 256] = dv_ref[hh].astype(BF16)
            rot = rot + dk_ref[hh, :, LANE:]
        pe_ref[...] = _rope_bwd(rot, c_ref[...], s_ref[...])

    return pl.pallas_call(
        body, name="kv_bwd_prep", grid=(S // tm,),
        in_specs=[pl.BlockSpec((MLA_HEADS, tm, 256), lambda i: (0, i, 0)),
                  pl.BlockSpec((MLA_HEADS, tm, LANE), lambda i: (0, i, 0)), _row_spec(tm, LANE), _row_spec(tm, LANE)],
        out_specs=[_row_spec(tm, MLA_HEADS * 256), _row_spec(tm, LANE)],
        out_shape=[jax.ShapeDtypeStruct((S, MLA_HEADS * 256), BF16), jax.ShapeDtypeStruct((S, LANE), F32)],
        compiler_params=_params(("parallel",)))(dk, dv, rc, rs)


def _chunk_mask(T):
    row = lax.broadcasted_iota(jnp.int32, (T, T), 0)
    col = lax.broadcasted_iota(jnp.int32, (T, T), 1)
    return (col // CHUNK) <= (row // CHUNK)


def _att_blocks(S, q_blk=None, k_blk=None):
    tq = min(q_blk or Q_BLK, S)
    tk = min(k_blk or K_BLK, tq)
    return tq, tk, tq // tk


def _tail_masks(rows, tk):
    row = lax.broadcasted_iota(jnp.int32, (rows, tk), 0)
    col = lax.broadcasted_iota(jnp.int32, (rows, tk), 1)
    return (col // CHUNK) <= (row // CHUNK), col < row


def _span_masks(tk, r):
    row = lax.broadcasted_iota(jnp.int32, (tk, (r + 1) * tk), 0) + r * tk
    col = lax.broadcasted_iota(jnp.int32, (tk, (r + 1) * tk), 1)
    return (col // CHUNK) <= (row // CHUNK), col < row


def _put_rows(old, new, r0):
    return new if r0 == 0 else jnp.concatenate([old[:r0], new], axis=0)


def _mla_fwd(q, kp, v):
    S = q.shape[0]
    TQ, TK, n = _att_blocks(S, None, MLA_FWD_K_BLK)

    def body(q_ref, k_ref, v_ref, o_ref, lse_ref):
        i = pl.program_id(1)

        def update(carry, qb, keys, mask):
            m, l, acc = carry
            s = _dot_nt(qb, k_ref[keys, :])
            if mask is not None:
                s = jnp.where(mask, s, -1e30)
            m_new = jnp.maximum(m, jnp.max(s, axis=1, keepdims=True))
            a = jnp.exp(m - m_new)
            p = jnp.exp(s - m_new)
            return m_new, a * l + jnp.sum(p, axis=1, keepdims=True), a * acc + _dot(p.astype(BF16), v_ref[keys, :])

        carry = (jnp.full((TQ, 1), -1e30, F32), jnp.zeros((TQ, 1), F32), jnp.zeros((TQ, LANE), F32))
        carry = lax.fori_loop(
            0, i * n, lambda j, c: update(c, q_ref[...], pl.ds(pl.multiple_of(j * TK, TK), TK), None), carry)
        for r in range(n):
            rows = slice(r * TK, (r + 1) * TK)
            m, l, acc = update(tuple(c[rows] for c in carry), q_ref[rows, :],
                               pl.ds(pl.multiple_of(i * TQ, TQ), (r + 1) * TK), _span_masks(TK, r)[0])
            o_ref[rows, :] = acc / l
            lse_ref[rows, :] = jnp.broadcast_to(m + jnp.log(l), (TK, LANE))

    return pl.pallas_call(
        body, name="mla_fwd", grid=(MLA_HEADS, S // TQ),
        in_specs=[pl.BlockSpec((TQ, 256), lambda h, i: (i, h)),
                  pl.BlockSpec((S, 256), lambda h, i: (0, h), pipeline_mode=pl.Buffered(1)),
                  pl.BlockSpec((S, LANE), lambda h, i: (0, h), pipeline_mode=pl.Buffered(1))],
        out_specs=[pl.BlockSpec((TQ, LANE), lambda h, i: (i, h)), pl.BlockSpec((TQ, LANE), lambda h, i: (i, h))],
        out_shape=[jax.ShapeDtypeStruct((S, MLA_HEADS * LANE), F32), jax.ShapeDtypeStruct((S, MLA_HEADS * LANE), F32)],
        compiler_params=_params(("parallel", "arbitrary")))(q, kp, v)


def _mla_bwd(q, kp, v, do_cat, o_cat, lse, rc, rs):
    S = q.shape[0]
    TQ, TK, n = _att_blocks(S)
    nq = S // TQ

    def body(q_ref, k_ref, v_ref, do_ref, o_ref, lse_ref, c_ref, s_ref, dq_ref, dk_hbm, dv_hbm, dk_acc, dv_acc):
        h, i = pl.program_id(0), pl.program_id(1)

        @pl.when(i == 0)
        def _():
            dk_acc[...] = jnp.zeros_like(dk_acc)
            dv_acc[...] = jnp.zeros_like(dv_acc)

        do32 = do_ref[...]
        dob = do32.astype(BF16)
        delta = jnp.sum(do32 * o_ref[...], axis=1, keepdims=True)
        lse_col = lse_ref[:, :1]

        def blk(j, dq, r0, masked):
            sl = pl.ds(pl.multiple_of(j * TK, TK), TK)
            kb, vb, qb = k_ref[sl, :], v_ref[sl, :], q_ref[r0:, :]
            s = _dot_nt(qb, kb)
            if masked:
                s = jnp.where(_tail_masks(TQ - r0, TK)[0], s, -1e30)
            p = jnp.exp(s - lse_col[r0:])
            ds = (p * (_dot_nt(dob[r0:], vb) - delta[r0:])).astype(BF16)
            dk_acc[sl, :] += _dot_tn(ds, qb)
            dv_acc[sl, :] += _dot_tn(p.astype(BF16), dob[r0:])
            return _put_rows(dq, dq[r0:] + _dot(ds, kb), r0)

        dq = lax.fori_loop(0, i * n, lambda j, c: blk(j, c, 0, False), jnp.zeros((TQ, 256), F32))
        for t in range(n):
            dq = blk(i * n + t, dq, t * TK, True)
        dq_ref[:, :LANE] = (dq[:, :LANE] * MLA_SCALE).astype(BF16)
        dq_ref[:, LANE:] = _rope_bwd(dq[:, LANE:] * MLA_SCALE, c_ref[...], s_ref[...]).astype(BF16)

        @pl.when(i == nq - 1)
        def _():
            pltpu.sync_copy(dk_acc, dk_hbm.at[h])
            pltpu.sync_copy(dv_acc, dv_hbm.at[h])

    any_spec = pl.BlockSpec(memory_space=pl.ANY)
    T = TQ
    rows = pl.BlockSpec((T, LANE), lambda h, i: (i, 0))
    return pl.pallas_call(
        body, name="mla_bwd", grid=(MLA_HEADS, nq),
        in_specs=[pl.BlockSpec((T, 256), lambda h, i: (i, h)),
                  pl.BlockSpec((S, 256), lambda h, i: (0, h), pipeline_mode=pl.Buffered(1)),
                  pl.BlockSpec((S, LANE), lambda h, i: (0, h), pipeline_mode=pl.Buffered(1)),
                  pl.BlockSpec((T, LANE), lambda h, i: (i, h)),
                  pl.BlockSpec((T, LANE), lambda h, i: (i, h)), pl.BlockSpec((T, LANE), lambda h, i: (i, h)), rows, rows],
        out_specs=[pl.BlockSpec((T, 256), lambda h, i: (i, h)), any_spec, any_spec],
        out_shape=[jax.ShapeDtypeStruct((S, MLA_HEADS * 256), BF16), jax.ShapeDtypeStruct((MLA_HEADS, S, 256), F32),
                   jax.ShapeDtypeStruct((MLA_HEADS, S, LANE), F32)],
        scratch_shapes=[pltpu.VMEM((S, 256), F32), pltpu.VMEM((S, LANE), F32)],
        compiler_params=_params(("arbitrary", "arbitrary")))(q, kp, v, do_cat, o_cat, lse, rc, rs)


def _split_dot(x, tri):
    top = lax.bitcast_convert_type(lax.bitcast_convert_type(x, jnp.uint32) & jnp.uint32(0xFFFF0000), F32)
    return _dot(top.astype(BF16), tri) + _dot((x - top).astype(BF16), tri)


def _sb_block(qb, kb, tri, carry, masked):
    z = _dot_nt(qb, kb)
    lb = jnp.minimum(z, 0.0) - jnp.log(1.0 + jnp.exp(-jnp.abs(z)))
    lm = lb - z
    strict = None
    if masked:
        strict = _tail_masks(z.shape[0], z.shape[1])[1]
        lm = jnp.where(strict, lm, 0.0)
    a = jnp.exp(lb + carry + _split_dot(lm, tri))
    if masked:
        a = jnp.where(strict, a, 0.0)
    return a, lb, lm, strict


def _sb_walk(blk, j0, state):
    def alive(c):
        return jnp.logical_and(c[0] >= 0, jnp.max(c[1][0]) > SB_DEAD)

    return lax.while_loop(alive, lambda c: (c[0] - 1, blk(c[0], c[1], 0, False)), (j0, state))[1]


def _triangle(tk):
    row = lax.broadcasted_iota(jnp.int32, (tk, tk), 0)
    col = lax.broadcasted_iota(jnp.int32, (tk, tk), 1)
    return (row > col).astype(BF16)


def _sb_fwd(qkv):
    S = qkv.shape[0]
    TQ, TK, n = _att_blocks(S, SB_Q_BLK, SB_K_BLK)
    T = TQ

    def body(q_ref, k_ref, v_ref, o_ref):
        i = pl.program_id(1)
        tri = _triangle(TK)

        def blk(j, state, r0, masked):
            carry, acc = (c[r0:] for c in state)
            sl = pl.ds(pl.multiple_of(j * TK, TK), TK)
            a, _, lm, _ = _sb_block(q_ref[r0:, :], k_ref[sl, :], tri, carry, masked)
            new = (carry + jnp.sum(lm, axis=1, keepdims=True), acc + _dot(a.astype(BF16), v_ref[sl, :]))
            return tuple(_put_rows(c, u, r0) for c, u in zip(state, new))

        state = (jnp.zeros((TQ, 1), F32), jnp.zeros((TQ, LANE), F32))
        for t in reversed(range(n)):
            state = blk(i * n + t, state, t * TK, True)
        state = _sb_walk(blk, i * n - 1, state)
        o_ref[...] = state[1]

    return pl.pallas_call(
        body, name="sb_fwd", grid=(SB_HEADS, S // T),
        in_specs=[pl.BlockSpec((T, LANE), lambda h, i: (i, h)), pl.BlockSpec((S, LANE), lambda h, i: (0, 4 + h)),
                  pl.BlockSpec((S, LANE), lambda h, i: (0, 8 + h))],
        out_specs=pl.BlockSpec((T, LANE), lambda h, i: (i, h)),
        out_shape=jax.ShapeDtypeStruct((S, SB_HEADS * LANE), F32),
        compiler_params=_params(("parallel", "arbitrary")))(qkv, qkv, qkv)


def _sb_bwd(qkv, do_cat, o_cat, col0):
    S = qkv.shape[0]
    TQ, TK, n = _att_blocks(S, SB_Q_BLK, SB_K_BLK)
    T = TQ
    nq = S // TQ

    def body(q_ref, k_ref, v_ref, do_ref, o_ref, dq_ref, dk_hbm, dv_hbm, dk_acc, dv_acc):
        h, i = pl.program_id(0), pl.program_id(1)

        @pl.when(i == 0)
        def _():
            dk_acc[...] = jnp.zeros_like(dk_acc)
            dv_acc[...] = jnp.zeros_like(dv_acc)

        dob = do_ref[...].astype(BF16)
        tri = _triangle(TK)
        rest0 = jnp.sum(dob.astype(F32) * o_ref[...], axis=1, keepdims=True)

        def blk(j, state, r0, masked):
            carry, rest, dq = (c[r0:] for c in state)
            sl = pl.ds(pl.multiple_of(j * TK, TK), TK)
            kb, vb, qb = k_ref[sl, :], v_ref[sl, :], q_ref[r0:, :]
            a, lb, lm, strict = _sb_block(qb, kb, tri, carry, masked)
            ab = a.astype(BF16)
            e = ab.astype(F32) * _dot_nt(dob[r0:], vb)
            dz = e - jnp.exp(lb) * (rest - _split_dot(e, tri))
            if masked:
                dz = jnp.where(strict, dz, 0.0)
            dzb = dz.astype(BF16)
            dk_acc[sl, :] += _dot_tn(dzb, qb)
            dv_acc[sl, :] += _dot_tn(ab, dob[r0:])
            new = (carry + jnp.sum(lm, axis=1, keepdims=True), rest - jnp.sum(e, axis=1, keepdims=True),
                   dq + _dot(dzb, kb))
            return tuple(_put_rows(c, u, r0) for c, u in zip(state, new))

        state = (jnp.zeros((TQ, 1), F32), rest0, jnp.zeros((TQ, LANE), F32))
        for t in reversed(range(n)):
            state = blk(i * n + t, state, t * TK, True)
        state = _sb_walk(blk, i * n - 1, state)
        dq_ref[...] = state[2] * SB_SCALE

        @pl.when(i == nq - 1)
        def _():
            lanes = pl.ds(pl.multiple_of(h * LANE, LANE), LANE)
            pltpu.sync_copy(dk_acc, dk_hbm.at[:, lanes])
            pltpu.sync_copy(dv_acc, dv_hbm.at[:, lanes])

    any_spec = pl.BlockSpec(memory_space=pl.ANY)
    return pl.pallas_call(
        body, name="sb_bwd", grid=(SB_HEADS, nq),
        in_specs=[pl.BlockSpec((T, LANE), lambda h, i: (i, h)), pl.BlockSpec((S, LANE), lambda h, i: (0, 4 + h)),
                  pl.BlockSpec((S, LANE), lambda h, i: (0, 8 + h)),
                  pl.BlockSpec((T, LANE), lambda h, i: (i, col0 + h)), pl.BlockSpec((T, LANE), lambda h, i: (i, col0 + h))],
        out_specs=[pl.BlockSpec((T, LANE), lambda h, i: (i, h)), any_spec, any_spec],
        out_shape=[jax.ShapeDtypeStruct((S, SB_HEADS * LANE), F32)] * 3,
        scratch_shapes=[pltpu.VMEM((S, LANE), F32), pltpu.VMEM((S, LANE), F32)],
        compiler_params=_params(("arbitrary", "arbitrary")))(qkv, qkv, qkv, do_cat, o_cat)


def _mem_probs(q, k_ref, hh):
    lane = lax.broadcasted_iota(jnp.int32, (1, 256), 1) // 64
    msk = lane == hh
    qh = jnp.where(msk, q, 0.0).astype(BF16)
    s = _dot_nt(qh, k_ref[...]) * MEM_SCALE
    p = jnp.exp(s - jnp.max(s, axis=1, keepdims=True))
    return msk, qh, p / jnp.sum(p, axis=1, keepdims=True)


def _mem_fwd(h, mk, mv):
    S = h.shape[0]
    tm = _tile(S, 512)

    def body(q_ref, k_ref, v_ref, o_ref):
        q = q_ref[...]
        out = jnp.zeros((tm, 256), F32)
        for hh in range(MEM_HEADS):
            msk, _, p = _mem_probs(q, k_ref, hh)
            out = out + jnp.where(msk, _dot(p.astype(BF16), v_ref[...]), 0.0)
        o_ref[...] = out

    return pl.pallas_call(
        body, name="mem_fwd", grid=(S // tm,),
        in_specs=[_row_spec(tm, 256, MQ // 256), _fix_spec((256, 256)), _fix_spec((256, 256))],
        out_specs=_row_spec(tm, 256), out_shape=jax.ShapeDtypeStruct((S, 256), F32),
        compiler_params=_params(("parallel",)))(h, mk, mv)


def _mem_bwd(h, mk, mv, do_cat, col0):
    S = h.shape[0]
    tm = _tile(S, 512)

    def body(q_ref, k_ref, v_ref, do_ref, dq_ref, dk_ref, dv_ref):
        @pl.when(pl.program_id(0) == 0)
        def _():
            dk_ref[...] = jnp.zeros_like(dk_ref)
            dv_ref[...] = jnp.zeros_like(dv_ref)

        q, do = q_ref[...], do_ref[...]
        dq = jnp.zeros((tm, 256), F32)
        for hh in range(MEM_HEADS):
            msk, qh, p = _mem_probs(q, k_ref, hh)
            doh = jnp.where(msk, do, 0.0).astype(BF16)
            dp = _dot_nt(doh, v_ref[...])
            ds = (p * (dp - jnp.sum(p * dp, axis=1, keepdims=True)) * MEM_SCALE).astype(BF16)
            dq = dq + jnp.where(msk, _dot(ds, k_ref[...]), 0.0)
            dk_ref[...] += _dot_tn(ds, qh)
            dv_ref[...] += _dot_tn(p.astype(BF16), doh)
        dq_ref[...] = dq

    return pl.pallas_call(
        body, name="mem_bwd", grid=(S // tm,),
        in_specs=[_row_spec(tm, 256, MQ // 256), _fix_spec((256, 256)), _fix_spec((256, 256)),
                  _row_spec(tm, 256, col0 // 256)],
        out_specs=[_row_spec(tm, 256), _fix_spec((256, 256)), _fix_spec((256, 256))],
        out_shape=[jax.ShapeDtypeStruct((S, 256), F32), jax.ShapeDtypeStruct((256, 256), F32),
                   jax.ShapeDtypeStruct((256, 256), F32)],
        compiler_params=_params(("arbitrary",)))(h, mk, mv, do_cat)


SG_T = 128


def _sg_norm(sv, g, b):
    gv = _gelu(sv)
    xc = gv - jnp.mean(gv, axis=1, keepdims=True)
    rstd = lax.rsqrt(jnp.mean(xc * xc, axis=1, keepdims=True) + LN_EPS)
    xhat = xc * rstd
    return xhat, rstd, xhat * g + b


def _sg_fwd(h, lng, lnb, w, bias_t):
    S = h.shape[0]
    tm = _tile(S, 512)

    def body(u_ref, v_ref, g_ref, b_ref, w_ref, bias_ref, o_ref):
        mask = _chunk_mask(SG_T)
        for n in range(tm // SG_T):
            rows = slice(n * SG_T, (n + 1) * SG_T)
            u = _gelu(u_ref[rows, :])
            _, _, vn = _sg_norm(v_ref[rows, :], g_ref[...], b_ref[...])
            vb = vn.astype(BF16)
            for gi in range(4):
                cols = slice(gi * LANE, (gi + 1) * LANE)
                wg = jnp.where(mask, w_ref[gi], 0.0).astype(BF16)
                mixed = _dot(wg, vb[:, cols]) + bias_ref[:, gi:gi + 1]
                o_ref[rows, cols] = u[:, cols] * mixed

    return pl.pallas_call(
        body, name="sg_fwd", grid=(S // tm,),
        in_specs=[_row_spec(tm, 512, SGU // 512), _row_spec(tm, 512, SGV // 512), _fix_spec((1, 512)),
                  _fix_spec((1, 512)), _fix_spec((4, SG_T, SG_T)), _fix_spec((SG_T, 4))],
        out_specs=_row_spec(tm, 512), out_shape=jax.ShapeDtypeStruct((S, 512), F32),
        compiler_params=_params(("parallel",)))(h, h, lng.reshape(1, 512), lnb.reshape(1, 512), w, bias_t)


def _sg_bwd(h, lng, lnb, w, bias_t, do_cat, col0):
    S = h.shape[0]
    tm = _tile(S, 512)
    nsteps = S // tm

    def body(u_ref, v_ref, g_ref, b_ref, w_ref, bias_ref, do0_ref, do1_ref, do2_ref, do3_ref,
             du_ref, dv_ref, dw_ref, dbias_ref, dg_ref, db_ref, dvn_scr, dbias_acc):
        do_refs = (do0_ref, do1_ref, do2_ref, do3_ref)
        step = pl.program_id(0)

        @pl.when(step == 0)
        def _():
            dw_ref[...] = jnp.zeros_like(dw_ref)
            dg_ref[...] = jnp.zeros_like(dg_ref)
            db_ref[...] = jnp.zeros_like(db_ref)
            dbias_acc[...] = jnp.zeros_like(dbias_acc)

        mask = _chunk_mask(SG_T)
        for n in range(tm // SG_T):
            rows = slice(n * SG_T, (n + 1) * SG_T)
            su, sv = u_ref[rows, :], v_ref[rows, :]
            u = _gelu(su)
            xhat, rstd, vn = _sg_norm(sv, g_ref[...], b_ref[...])
            vb = vn.astype(BF16)
            ugrad = _gelu_grad(su)
            for gi in range(4):
                cols = slice(gi * LANE, (gi + 1) * LANE)
                do = do_refs[gi][rows, :]
                wg = jnp.where(mask, w_ref[gi], 0.0).astype(BF16)
                mixed = _dot(wg, vb[:, cols]) + bias_ref[:, gi:gi + 1]
                dmixed = do * u[:, cols]
                dmb = dmixed.astype(BF16)
                du_ref[rows, cols] = do * mixed * ugrad[:, cols]
                dvn_scr[:, cols] = _dot_tn(wg, dmb)
                dw_ref[gi] += jnp.where(mask, _dot_nt(dmb, vb[:, cols]), 0.0)
                dbias_acc[gi] += dmixed
            dvn = dvn_scr[...]
            dg_ref[...] += jnp.sum(dvn * xhat, axis=0, keepdims=True)
            db_ref[...] += jnp.sum(dvn, axis=0, keepdims=True)
            dxh = dvn * g_ref[...]
            dgv = rstd * (dxh - jnp.mean(dxh, axis=1, keepdims=True)
                          - xhat * jnp.mean(dxh * xhat, axis=1, keepdims=True))
            dv_ref[rows, :] = dgv * _gelu_grad(sv)

        @pl.when(step == nsteps - 1)
        def _():
            for gi in range(4):
                dbias_ref[:, gi:gi + 1] = jnp.sum(dbias_acc[gi], axis=1, keepdims=True)

    return pl.pallas_call(
        body, name="sg_bwd", grid=(nsteps,),
        in_specs=[_row_spec(tm, 512, SGU // 512), _row_spec(tm, 512, SGV // 512), _fix_spec((1, 512)),
                  _fix_spec((1, 512)), _fix_spec((4, SG_T, SG_T)), _fix_spec((SG_T, 4))]
                 + [_row_spec(tm, LANE, col0 // LANE + gi) for gi in range(4)],
        out_specs=[_row_spec(tm, 512), _row_spec(tm, 512), _fix_spec((4, SG_T, SG_T)), _fix_spec((SG_T, 4)),
                   _fix_spec((1, 512)), _fix_spec((1, 512))],
        out_shape=[jax.ShapeDtypeStruct((S, 512), F32), jax.ShapeDtypeStruct((S, 512), F32),
                   jax.ShapeDtypeStruct((4, SG_T, SG_T), F32), jax.ShapeDtypeStruct((SG_T, 4), F32),
                   jax.ShapeDtypeStruct((1, 512), F32), jax.ShapeDtypeStruct((1, 512), F32)],
        scratch_shapes=[pltpu.VMEM((SG_T, 512), F32), pltpu.VMEM((4, SG_T, SG_T), F32)],
        compiler_params=_params(("arbitrary",)))(h, h, lng.reshape(1, 512), lnb.reshape(1, 512), w, bias_t,
                                                 do_cat, do_cat, do_cat, do_cat)


def _gate_out_ln(branches, h, w_out, x, g, b):
    S = h.shape[0]
    tm = _tile(S, 256)
    widths = [a.shape[1] for a in branches]

    def body(oa_ref, ob_ref, oc_ref, om_ref, g0_ref, g1_ref, g2_ref, g3_ref, w_ref, x_ref, lg_ref, lb_ref,
             cat_ref, yg_ref, xo_ref, xb_ref, r_ref):
        at = 0
        for ref, width in zip((oa_ref, ob_ref, oc_ref, om_ref), widths):
            cat_ref[:, at:at + width] = ref[...]
            at += width
        for j, g_ref in enumerate((g0_ref, g1_ref, g2_ref, g3_ref)):
            gate = g_ref[...]
            cols = slice(j * 512, (j + 1) * 512)
            yg_ref[:, cols] = (cat_ref[:, cols] * (gate * jax.nn.sigmoid(gate))).astype(BF16)
        r = ALPHA * x_ref[...] + _dot(yg_ref[...], w_ref[...])
        r_ref[...] = r
        xc = r - jnp.mean(r, axis=1, keepdims=True)
        o = xc * lax.rsqrt(jnp.mean(xc * xc, axis=1, keepdims=True) + LN_EPS) * lg_ref[...] + lb_ref[...]
        xo_ref[...] = o
        xb_ref[...] = o.astype(BF16)

    row = _row_spec(tm, D_MODEL)
    return pl.pallas_call(
        body, name="gate_out_ln", grid=(S // tm,),
        in_specs=[_row_spec(tm, width) for width in widths] + [_row_spec(tm, 512, GATE // 512 + j) for j in range(4)]
                 + [_fix_spec((D_MODEL, D_MODEL)), row, _fix_spec((1, D_MODEL)), _fix_spec((1, D_MODEL))],
        out_specs=[row] * 5,
        out_shape=[jax.ShapeDtypeStruct((S, D_MODEL), t) for t in (F32, BF16, F32, BF16, F32)],
        compiler_params=_params(("parallel",)))(*branches, h, h, h, h, w_out, x, g.reshape(1, D_MODEL), b.reshape(1, D_MODEL))


def _out_proj_gate_bwd(dr, w_out, o_cat, h):
    S = h.shape[0]
    tm = _tile(S, 1024)

    def body(dr_ref, w_ref, o_ref, g_ref, do_ref, dg_ref, drb):
        @pl.when(pl.program_id(1) == 0)
        def _():
            drb[...] = dr_ref[...].astype(BF16)

        d = _dot_nt(drb[...], w_ref[...])
        g = g_ref[...]
        sig = jax.nn.sigmoid(g)
        do_ref[...] = d * (g * sig)
        dg_ref[...] = d * o_ref[...] * (sig * (1.0 + g * (1.0 - sig)))

    blk = pl.BlockSpec((tm, 512), lambda i, j: (i, j))
    return pl.pallas_call(
        body, name="d_out_proj_gate", grid=(S // tm, 4),
        in_specs=[pl.BlockSpec((tm, D_MODEL), lambda i, j: (i, 0)), pl.BlockSpec((512, D_MODEL), lambda i, j: (j, 0)),
                  blk, pl.BlockSpec((tm, 512), lambda i, j: (i, GATE // 512 + j))],
        out_specs=[blk, blk],
        out_shape=[jax.ShapeDtypeStruct((S, D_MODEL), F32), jax.ShapeDtypeStruct((S, D_MODEL), F32)],
        scratch_shapes=[pltpu.VMEM((tm, D_MODEL), BF16)],
        compiler_params=_params(("parallel", "arbitrary")))(dr, w_out, o_cat, h)


def _ln_res_bwd(dout, r, g):
    S = r.shape[0]
    tm = _tile(S, 512)

    def body(d_ref, r_ref, g_ref, dr_ref, dg_ref, db_ref):
        @pl.when(pl.program_id(0) == 0)
        def _():
            dg_ref[...] = jnp.zeros_like(dg_ref)
            db_ref[...] = jnp.zeros_like(db_ref)

        d, r = d_ref[...], r_ref[...]
        xc = r - jnp.mean(r, axis=1, keepdims=True)
        rstd = lax.rsqrt(jnp.mean(xc * xc, axis=1, keepdims=True) + LN_EPS)
        xhat = xc * rstd
        dxh = d * g_ref[...]
        dr_ref[...] = rstd * (dxh - jnp.mean(dxh, axis=1, keepdims=True)
                              - xhat * jnp.mean(dxh * xhat, axis=1, keepdims=True))
        dg_ref[...] += jnp.sum(d * xhat, axis=0, keepdims=True)
        db_ref[...] += jnp.sum(d, axis=0, keepdims=True)

    return pl.pallas_call(
        body, name="ln_res_bwd", grid=(S // tm,),
        in_specs=[_row_spec(tm, D_MODEL), _row_spec(tm, D_MODEL), _fix_spec((1, D_MODEL))],
        out_specs=[_row_spec(tm, D_MODEL), _fix_spec((1, D_MODEL)), _fix_spec((1, D_MODEL))],
        out_shape=[jax.ShapeDtypeStruct((S, D_MODEL), F32), jax.ShapeDtypeStruct((1, D_MODEL), F32),
                   jax.ShapeDtypeStruct((1, D_MODEL), F32)],
        compiler_params=_params(("arbitrary",)))(dout, r, g.reshape(1, D_MODEL))


def _loss_head(y, target):
    S = y.shape[0]
    tm = _tile(S, 512)

    def body(y_ref, t_ref, l_ref, d_ref):
        @pl.when(pl.program_id(0) == 0)
        def _():
            l_ref[...] = jnp.zeros_like(l_ref)

        diff = y_ref[...] - t_ref[...]
        d_ref[...] = diff * (1.0 / D_MODEL)
        per_row = jnp.mean(diff * diff, axis=1, keepdims=True)
        l_ref[...] += 0.5 * jnp.sum(per_row, axis=0, keepdims=True)

    return pl.pallas_call(
        body, name="loss_head", grid=(S // tm,), in_specs=[_row_spec(tm, D_MODEL), _row_spec(tm, D_MODEL)],
        out_specs=[_fix_spec((8, LANE)), _row_spec(tm, D_MODEL)],
        out_shape=[jax.ShapeDtypeStruct((8, LANE), F32), jax.ShapeDtypeStruct((S, D_MODEL), F32)],
        compiler_params=_params(("arbitrary",)))(y, target)


def _perm_table():
    table, at = [], 0
    for name in PERM_ORDER:
        start, width = ORIG[name]
        table.append((name, start, width, at))
        at += width
    return table


def _permute_w_in(by_chip):
    wc = by_chip.shape[-1]
    parts = []
    for _, start, width, _ in _perm_table():
        lo = start
        while lo < start + width:
            k = lo // wc
            hi = min(start + width, (k + 1) * wc)
            parts.append(by_chip[k, ..., lo - k * wc:hi - k * wc])
            lo = hi
    parts.append(jnp.zeros(by_chip.shape[1:-1] + (HP - D_IN,), by_chip.dtype))
    return jnp.concatenate(parts, axis=-1)


def _model_cols(wp, lo, hi):
    parts = []
    for _, start, width, at in sorted(_perm_table(), key=lambda t: t[1]):
        a, b = max(lo, start), min(hi, start + width)
        if a < b:
            parts.append(wp[..., at + a - start:at + b - start])
    return jnp.concatenate(parts, axis=-1)


def _rope_tables(positions):
    inv_freq = ROPE_THETA ** (-jnp.arange(0, 64, 2, dtype=F32) / 64)
    ang = positions.astype(F32)[:, None] * inv_freq[None, :]
    cos, sin, zero = jnp.cos(ang), jnp.sin(ang), jnp.zeros((positions.shape[0], 64), F32)
    return jnp.concatenate([cos, cos, zero], axis=1), jnp.concatenate([-sin, sin, zero], axis=1)


def _local_step(x, mem, positions, target, w):
    rc, rs = _rope_tables(positions)
    mem_b = mem.astype(BF16)
    xb = x.astype(BF16)
    w_in_all = _permute_w_in(w["w_in"])
    w_uq_all = jnp.pad(w["w_uq"].reshape(DEPTH, 512, MLA_HEADS, 192),
                       ((0, 0), (0, 0), (0, 0), (0, 64))).reshape(DEPTH, 512, MLA_HEADS * 256)
    saved = []
    for l in range(DEPTH):
        w_in, w_uq, w_ukv = w_in_all[l], w_uq_all[l], w["w_ukv"][l]
        h = _mm(xb, w_in, tm=1024, tn=1152, tk=2048, name="in_proj")
        cq_n = _rms_fwd(h, CQ, 512, w["q_norm_g"][l], "rms_q")
        ckv_n = _rms_fwd(h, CKV, 256, w["kv_norm_g"][l], "rms_kv")
        q = _q_proj(cq_n, w_uq, rc, rs)
        kp, v = _kv_proj(ckv_n, w_ukv, h, rc, rs)
        o_a, lse = _mla_fwd(q, kp, v)
        bias_t = w["sg_b"][l].T
        o_b = _sg_fwd(h, w["sg_ln_g"][l], w["sg_ln_b"][l], w["sg_w"][l], bias_t)
        qkv = jnp.concatenate([h[:, SBQ:SBQ + 512] * SB_SCALE, h[:, SBQ + 512:SBQ + 1536]], axis=1).astype(BF16)
        o_c = _sb_fwd(qkv)
        mk = _mm(mem_b, w["w_mem_k"][l], out_dtype=BF16, name="mem_kv")
        mv = _mm(mem_b, w["w_mem_v"][l], out_dtype=BF16, name="mem_kv")
        o_m = _mem_fwd(h, mk, mv)
        o_cat, yg, x_new, xb_new, r = _gate_out_ln((o_a, o_b, o_c, o_m), h, w["w_out"][l], x, w["ln_g"][l], w["ln_b"][l])
        saved.append(dict(xb=xb, h=h, cq_n=cq_n, ckv_n=ckv_n, q=q, kp=kp, v=v, lse=lse, qkv=qkv, mk=mk, mv=mv,
                          o_cat=o_cat, yg=yg, r=r, w_in=w_in, w_uq=w_uq, w_ukv=w_ukv, bias_t=bias_t))
        x, xb = x_new, xb_new

    loss, dx = _loss_head(x, target)

    grads = {n: [None] * DEPTH for n in SHARDED + SMALL}
    for l in reversed(range(DEPTH)):
        s = saved[l]
        h = s["h"]
        dr, dlg, dlb = _ln_res_bwd(dx, s["r"], w["ln_g"][l])
        grads["ln_g"][l], grads["ln_b"][l] = dlg[0], dlb[0]
        grads["w_out"][l] = _mm(s["yg"], dr, ta=True, tm=1024, tn=1024, tk=2048, name="dw_out")
        do_cat, dgates = _out_proj_gate_bwd(dr, w["w_out"][l], s["o_cat"], h)
        dmq, dmk, dmv = _mem_bwd(h, s["mk"], s["mv"], do_cat, 1792)
        grads["w_mem_k"][l] = _mm(mem_b, dmk, ta=True, name="dw_mem")
        grads["w_mem_v"][l] = _mm(mem_b, dmv, ta=True, name="dw_mem")
        dsq, dsk, dsv = _sb_bwd(s["qkv"], do_cat, s["o_cat"], 1280 // LANE)
        du, dv, dsgw, dsgb, dsg_g, dsg_b = _sg_bwd(h, w["sg_ln_g"][l], w["sg_ln_b"][l], w["sg_w"][l], s["bias_t"],
                                                   do_cat, 768)
        grads["sg_w"][l], grads["sg_b"][l] = dsgw, dsgb.T
        grads["sg_ln_g"][l], grads["sg_ln_b"][l] = dsg_g[0], dsg_b[0]
        dq_raw, dk, dvv = _mla_bwd(s["q"], s["kp"], s["v"], do_cat, s["o_cat"], s["lse"], rc, rs)
        dkv, dkpe = _kv_bwd_prep(dk, dvv, rc, rs)
        dw_uq = _mm(s["cq_n"], dq_raw, ta=True, tn=1536, tk=2048, name="dw_uq")
        grads["w_uq"][l] = dw_uq.reshape(512, MLA_HEADS, 256)[:, :, :192].reshape(512, MLA_HEADS * 192)
        grads["w_ukv"][l] = _mm(s["ckv_n"], dkv, ta=True, tn=1536, tk=2048, name="dw_ukv")
        dcq_n = _mm(dq_raw, s["w_uq"], tb=True, tm=2048, tk=1536, name="d_cq")
        dckv_n = _mm(dkv, s["w_ukv"], tb=True, tm=2048, tk=1536, name="d_ckv")
        dcq, dqg = _rms_bwd(h, CQ, 512, w["q_norm_g"][l], dcq_n, "rms_q_bwd")
        dckv, dkvg = _rms_bwd(h, CKV, 256, w["kv_norm_g"][l], dckv_n, "rms_kv_bwd")
        grads["q_norm_g"][l], grads["kv_norm_g"][l] = dqg[0], dkvg[0]
        dh = jnp.concatenate([dcq, dckv, dmq, du, dv, dsq, dsk, dsv, dgates, dkpe], axis=1).astype(BF16)
        dw_in = _mm(s["xb"], dh, ta=True, tm=1024, tn=1152, tk=2048, name="dw_in")
        grads["w_in"][l] = dw_in
        dx = _mm(dh, s["w_in"], tb=True, add=dr, add_scale=ALPHA, tm=1024, tn=1024, tk=1920, name="d_in_proj")

    return loss, dx, grads


MESH = pl.DeviceIdType.MESH
HBM_SPEC = pl.BlockSpec(memory_space=pltpu.HBM)


def _place():
    x, y, c = lax.axis_index("x"), lax.axis_index("y"), lax.axis_index("c")
    return x, y, c, [(1 - x, y), (x, 1 - y), (1 - x, 1 - y)]


HALF = DEPTH // 2


def _comm_call(body, name, arrays, out_shapes, n_sems):
    return pl.pallas_call(
        body, name=name, in_specs=[HBM_SPEC] * len(arrays), out_specs=[HBM_SPEC] * len(out_shapes), out_shape=out_shapes,
        scratch_shapes=[pltpu.SemaphoreType.DMA((n_sems,)), pltpu.SemaphoreType.DMA((n_sems,))],
        compiler_params=pltpu.CompilerParams(has_side_effects=True))(*arrays)


def _gather_weights(shards):
    na = len(shards)

    def body(*refs):
        srcs, outs, (send_sems, recv_sems) = refs[:na], refs[na:2 * na], refs[2 * na:]
        x, y, c, chips = _place()
        mine, theirs = pl.ds(HALF * c, HALF), pl.ds(HALF * (1 - c), HALF)

        def copy(a, k, src_ref, chip, layers, to):
            return pltpu.make_async_remote_copy(
                src_ref=src_ref, dst_ref=outs[a].at[chip, layers], send_sem=send_sems.at[6 * a + k],
                recv_sem=recv_sems.at[6 * a + k], device_id=to, device_id_type=MESH)

        sent = [copy(a, j, srcs[a].at[mine], 2 * x + y, mine, (px, py, c))
                for a in range(na) for j, (px, py) in enumerate(chips)]
        for cp in sent:
            cp.start()
        passed = []
        for j, (px, py) in enumerate(chips):
            for a in range(na):
                copy(a, j, srcs[a].at[mine], 2 * px + py, mine, (px, py, c)).wait_recv()
                cp = copy(a, 3 + j, outs[a].at[2 * px + py, mine], 2 * px + py, mine, (x, y, 1 - c))
                cp.start()
                passed.append(cp)
        for j, (px, py) in enumerate(chips):
            for a in range(na):
                copy(a, 3 + j, srcs[a].at[theirs], 2 * px + py, theirs, (x, y, 1 - c)).wait_recv()
        for cp in sent + passed:
            cp.wait_send()

    return _comm_call(body, "gather_weights", shards, [jax.ShapeDtypeStruct((4,) + s.shape, s.dtype) for s in shards], 6 * na)


def _swap_halves(gs):
    na = len(gs)

    def body(*refs):
        srcs, outs, (send_sems, recv_sems) = refs[:na], refs[na:2 * na], refs[2 * na:]
        x, y, c, _ = _place()
        cps = [pltpu.make_async_remote_copy(
            src_ref=srcs[a].at[:, pl.ds(HALF * (1 - c), HALF)], dst_ref=outs[a], send_sem=send_sems.at[a],
            recv_sem=recv_sems.at[a], device_id=(x, y, 1 - c), device_id_type=MESH) for a in range(na)]
        for cp in cps:
            cp.start()
        for cp in cps:
            cp.wait()

    return _comm_call(body, "swap_halves", gs,
                      [jax.ShapeDtypeStruct((4, HALF) + g.shape[2:], g.dtype) for g in gs], na)


def _pair_sum(g, other, c):
    _, _, R, C = g.shape
    tr = _row_tile(R, 3 * C * 4)

    def body(c_ref, a_ref, b_ref, o_ref):
        o_ref[...] = (a_ref[...] + b_ref[...]).astype(BF16)

    blk = pl.BlockSpec((None, None, tr, C), lambda d, l, i, c_ref: (d, l, i, 0))
    return pl.pallas_call(
        body, name="pair_sum",
        grid_spec=pltpu.PrefetchScalarGridSpec(
            num_scalar_prefetch=1, grid=(4, HALF, R // tr),
            in_specs=[pl.BlockSpec((None, None, tr, C), lambda d, l, i, c_ref: (d, HALF * c_ref[0] + l, i, 0)), blk],
            out_specs=blk),
        out_shape=jax.ShapeDtypeStruct((4, HALF, R, C), BF16),
        compiler_params=_params(("parallel", "parallel", "parallel")))(c, g, other)


def _exchange_chips(ps):
    na = len(ps)

    def body(*refs):
        srcs, outs, (send_sems, recv_sems) = refs[:na], refs[na:2 * na], refs[2 * na:]
        x, y, c, chips = _place()
        cps = [pltpu.make_async_remote_copy(
            src_ref=srcs[a].at[2 * px + py], dst_ref=outs[a].at[j], send_sem=send_sems.at[3 * a + j],
            recv_sem=recv_sems.at[3 * a + j], device_id=(px, py, c), device_id_type=MESH)
            for a in range(na) for j, (px, py) in enumerate(chips)]
        for cp in cps:
            cp.start()
        for cp in cps:
            cp.wait()

    return _comm_call(body, "exchange_chips", ps, [jax.ShapeDtypeStruct((3,) + p.shape[1:], p.dtype) for p in ps], 3 * na)


def _chip_sum(p, got, me):
    _, _, R, C = p.shape
    tr = _row_tile(R, 4 * C * 4)

    def body(me_ref, p_ref, g_ref, o_ref):
        acc = p_ref[...].astype(F32)
        for k in range(3):
            acc = acc + g_ref[k].astype(F32)
        o_ref[...] = acc

    return pl.pallas_call(
        body, name="chip_sum",
        grid_spec=pltpu.PrefetchScalarGridSpec(
            num_scalar_prefetch=1, grid=(HALF, R // tr),
            in_specs=[pl.BlockSpec((None, None, tr, C), lambda l, i, me_ref: (me_ref[0], l, i, 0)),
                      pl.BlockSpec((3, None, tr, C), lambda l, i, me_ref: (0, l, i, 0))],
            out_specs=pl.BlockSpec((None, tr, C), lambda l, i, me_ref: (l, i, 0))),
        out_shape=jax.ShapeDtypeStruct((HALF, R, C), F32), compiler_params=_params(("parallel", "parallel")))(me, p, got)


def _sum_parts(t, name):
    n, H, W = t.shape
    th = _row_tile(H, (n + 1) * W * 4)

    def body(t_ref, o_ref):
        acc = t_ref[0]
        for k in range(1, n):
            acc = acc + t_ref[k]
        o_ref[...] = acc

    return pl.pallas_call(
        body, name=name, grid=(H // th,), in_specs=[pl.BlockSpec((n, th, W), lambda i: (0, i, 0))],
        out_specs=pl.BlockSpec((th, W), lambda i: (i, 0)), out_shape=jax.ShapeDtypeStruct((H, W), F32),
        compiler_params=_params(("parallel",)))(t)


def _share_with_sibling(halves):
    na = len(halves)

    def body(*refs):
        srcs, outs, (send_sems, recv_sems) = refs[:na], refs[na:2 * na], refs[2 * na:]
        x, y, c, _ = _place()

        def copy(a, layers):
            return pltpu.make_async_remote_copy(
                src_ref=srcs[a], dst_ref=outs[a].at[layers], send_sem=send_sems.at[a], recv_sem=recv_sems.at[a],
                device_id=(x, y, 1 - c), device_id_type=MESH)

        sent = [copy(a, pl.ds(HALF * c, HALF)) for a in range(na)]
        for cp in sent:
            cp.start()
        for a in range(na):
            copy(a, pl.ds(HALF * (1 - c), HALF)).wait_recv()
        for cp in sent:
            cp.wait_send()

    return _comm_call(body, "share_with_sibling", halves,
                      [jax.ShapeDtypeStruct((DEPTH,) + h.shape[1:], h.dtype) for h in halves], na)


def _gather_all(v):
    n, W = v.shape

    def body(src, out, send_sems, recv_sems, own_sem):
        x, y, c, _ = _place()
        own = pltpu.make_async_copy(src, out.at[4 * x + 2 * y + c], own_sem)
        own.start()
        flips = [(fx, fy, fc) for fx in (0, 1) for fy in (0, 1) for fc in (0, 1)][1:]
        sent = []
        for k, (fx, fy, fc) in enumerate(flips):
            cp = pltpu.make_async_remote_copy(
                src_ref=src, dst_ref=out.at[4 * x + 2 * y + c], send_sem=send_sems.at[k], recv_sem=recv_sems.at[k],
                device_id=(x ^ fx, y ^ fy, c ^ fc), device_id_type=MESH)
            cp.start()
            sent.append(cp)
        for k, (fx, fy, fc) in enumerate(flips):
            pltpu.make_async_remote_copy(
                src_ref=src, dst_ref=out.at[4 * (x ^ fx) + 2 * (y ^ fy) + (c ^ fc)], send_sem=send_sems.at[k],
                recv_sem=recv_sems.at[k], device_id=(x ^ fx, y ^ fy, c ^ fc), device_id_type=MESH).wait_recv()
        for cp in sent:
            cp.wait_send()
        own.wait()

    return pl.pallas_call(
        body, name="gather_all", in_specs=[HBM_SPEC], out_specs=HBM_SPEC,
        out_shape=jax.ShapeDtypeStruct((8, n, W), v.dtype),
        scratch_shapes=[pltpu.SemaphoreType.DMA((7,)), pltpu.SemaphoreType.DMA((7,)), pltpu.SemaphoreType.DMA(())],
        compiler_params=pltpu.CompilerParams(has_side_effects=True))(v)


def _adamw(w, g, m, v):
    shape = w.shape
    cols = shape[-1]
    w2, g2, m2, v2 = (a.reshape(-1, cols) for a in (w, g, m, v))
    rows = w2.shape[0]
    tr = next((t for t in (1024, 512, 256, 128, 64, 32, 16, 8) if rows % t == 0 and t * cols * 4 <= (2 << 20)), rows)

    def body(w_ref, g_ref, m_ref, v_ref, d_ref, nm_ref, nv_ref):
        g_ = g_ref[...]
        nm = ADAM_B1 * m_ref[...] + (1.0 - ADAM_B1) * g_
        nv = ADAM_B2 * v_ref[...] + (1.0 - ADAM_B2) * (g_ * g_)
        m_hat = nm / (1.0 - ADAM_B1 ** ADAM_STEP)
        v_hat = nv / (1.0 - ADAM_B2 ** ADAM_STEP)
        d_ref[...] = -ADAM_LR * (m_hat / (jnp.sqrt(v_hat) + ADAM_EPS) + ADAM_WD * w_ref[...])
        nm_ref[...] = nm
        nv_ref[...] = nv

    blk = pl.BlockSpec((tr, cols), lambda i: (i, 0))
    outs = pl.pallas_call(
        body, name="adamw", grid=(rows // tr,), in_specs=[blk] * 4, out_specs=[blk] * 3,
        out_shape=[jax.ShapeDtypeStruct((rows, cols), F32)] * 3, compiler_params=_params(("parallel",)))(w2, g2, m2, v2)
    return tuple(o.reshape(shape) for o in outs)


BY_COLUMNS = ("w_in", "w_uq", "w_ukv")


def _chip_part(name, a, k):
    if name == "w_in":
        n = D_IN // 4
        return _model_cols(a, k * n, (k + 1) * n)
    n = a.shape[1 if name in BY_COLUMNS else 0] // 4
    return a[:, k * n:(k + 1) * n] if name in BY_COLUMNS else a[k * n:(k + 1) * n]


def kernel(x, mem, positions, w_in, q_norm_g, w_uq, kv_norm_g, w_ukv, sg_ln_g, sg_ln_b, sg_w, sg_b, w_mem_k, w_mem_v, w_out, ln_g, ln_b, loss_target, m_w_in, m_q_norm_g, m_w_uq, m_kv_norm_g, m_w_ukv, m_sg_ln_g, m_sg_ln_b, m_sg_w, m_sg_b, m_w_mem_k, m_w_mem_v, m_w_out, m_ln_g, m_ln_b, v_w_in, v_q_norm_g, v_w_uq, v_kv_norm_g, v_w_ukv, v_sg_ln_g, v_sg_ln_b, v_sg_w, v_sg_b, v_w_mem_k, v_w_mem_v, v_w_out, v_ln_g, v_ln_b):
    weights = dict(w_in=w_in, q_norm_g=q_norm_g, w_uq=w_uq, kv_norm_g=kv_norm_g, w_ukv=w_ukv, sg_ln_g=sg_ln_g,
                   sg_ln_b=sg_ln_b, sg_w=sg_w, sg_b=sg_b, w_mem_k=w_mem_k, w_mem_v=w_mem_v, w_out=w_out, ln_g=ln_g, ln_b=ln_b)
    mom_m = dict(w_in=m_w_in, q_norm_g=m_q_norm_g, w_uq=m_w_uq, kv_norm_g=m_kv_norm_g, w_ukv=m_w_ukv, sg_ln_g=m_sg_ln_g,
                 sg_ln_b=m_sg_ln_b, sg_w=m_sg_w, sg_b=m_sg_b, w_mem_k=m_w_mem_k, w_mem_v=m_w_mem_v, w_out=m_w_out,
                 ln_g=m_ln_g, ln_b=m_ln_b)
    mom_v = dict(w_in=v_w_in, q_norm_g=v_q_norm_g, w_uq=v_w_uq, kv_norm_g=v_kv_norm_g, w_ukv=v_w_ukv, sg_ln_g=v_sg_ln_g,
                 sg_ln_b=v_sg_ln_b, sg_w=v_sg_w, sg_b=v_sg_b, w_mem_k=v_w_mem_k, w_mem_v=v_w_mem_v, w_out=v_w_out,
                 ln_g=v_ln_g, ln_b=v_ln_b)
    c_idx = lax.axis_index("c").astype(jnp.int32).reshape(1)

    me = 2 * lax.axis_index("x") + lax.axis_index("y")
    shards = [weights[n].astype(BF16) for n in SHARDED]
    by_chip = [lax.dynamic_update_slice(g, s[None], (me, 0, 0, 0)) for g, s in zip(_gather_weights(shards), shards)]
    full = dict((n, weights[n]) for n in SMALL)
    full["w_in"] = by_chip[0]
    for n, g in zip(SHARDED[1:], by_chip[1:]):
        full[n] = jnp.concatenate([g[k] for k in range(4)], axis=2 if n in BY_COLUMNS else 1)

    loss_dev, grad_x, grads = _local_step(x[0], mem[0], positions[0], loss_target[0], full)

    gs = [jnp.stack([jnp.stack([_chip_part(n, g, k) for g in grads[n]]) for k in range(4)]) for n in SHARDED]
    pairs = [_pair_sum(g, o, c_idx) for g, o in zip(gs, _swap_halves(gs))]
    me1 = me.astype(jnp.int32).reshape(1)
    halves = [_chip_sum(p, o, me1) for p, o in zip(pairs, _exchange_chips(pairs))]
    grad_out = {n: lax.dynamic_update_slice(r, h, (HALF * c_idx[0], 0, 0))
                for n, r, h in zip(SHARDED, _share_with_sibling(halves), halves)}

    small_sizes = [weights[n].size for n in SMALL]
    vec = jnp.concatenate([g.reshape(-1) for n in SMALL for g in grads[n]] + [loss_dev[0]])
    n_small = vec.shape[0]
    rows_small = -(-n_small // (8 * FLAT_W)) * 8
    vec = jnp.pad(vec, (0, rows_small * FLAT_W - n_small)).reshape(rows_small, FLAT_W)
    total = _sum_parts(_gather_all(vec), "device_sum").reshape(-1)
    at = 0
    for n, size in zip(SMALL, small_sizes):
        grad_out[n] = total[at:at + size].reshape(weights[n].shape)
        at += size
    loss = total[at]

    names = list(weights)
    upd = {n: _adamw(weights[n], grad_out[n], mom_m[n], mom_v[n]) for n in names}
    return (loss, grad_x[None], *[grad_out[n] for n in names], *[upd[n][0] for n in names],
            *[upd[n][1] for n in names], *[upd[n][2] for n in names])
```

```python
import math

import jax
import jax.numpy as jnp
from jax import lax
from jax.experimental import pallas as pl
from jax.experimental.pallas import tpu as pltpu

F32, BF16 = jnp.float32, jnp.bfloat16

D_MODEL = 2048
DEPTH = 4
CHUNK = 64
MLA_HEADS = 6
MLA_SCALE = 1.0 / math.sqrt(192.0)
SB_HEADS = 4
SB_SCALE = 1.0 / math.sqrt(128.0)
MEM_HEADS = 4
MEM_SCALE = 1.0 / math.sqrt(64.0)
ROPE_THETA = 10000.0
ALPHA = (2.0 * DEPTH) ** 0.25
LN_EPS = 1e-5
RMS_EPS = 1e-6
ADAM_LR, ADAM_B1, ADAM_B2, ADAM_EPS, ADAM_WD, ADAM_STEP = 0.001, 0.9, 0.999, 1e-08, 0.01, 10

ORIG = dict(c_q=(0, 512), c_kv=(512, 256), k_pe=(768, 64), g_a=(832, 768), sg_u=(1600, 512), sg_v=(2112, 512),
            g_b=(2624, 512), sb_q=(3136, 512), sb_k=(3648, 512), sb_v=(4160, 512), g_c=(4672, 512),
            m_q=(5184, 256), g_m=(5440, 256))
D_IN = 5696
PERM_ORDER = ("c_q", "c_kv", "m_q", "sg_u", "sg_v", "sb_q", "sb_k", "sb_v", "g_a", "g_b", "g_c", "g_m", "k_pe")
HP = 5760
CQ, CKV, MQ, SGU, SGV, SBQ, GATE, KPE = 0, 512, 768, 1024, 1536, 2048, 3584, 5632

Q_BLK = 2048
K_BLK = 512
MLA_FWD_K_BLK = 1024
SB_Q_BLK = 512
SB_K_BLK = 256
SB_DEAD = -110.0
LANE = 128
VMEM_LIMIT = 56 * 1024 * 1024

FLAT_W = 1024
SHARDED = ("w_in", "w_uq", "w_ukv", "w_mem_k", "w_mem_v", "w_out")
SMALL = ("q_norm_g", "kv_norm_g", "sg_ln_g", "sg_ln_b", "sg_w", "sg_b", "ln_g", "ln_b")


def _params(sem=None):
    return pltpu.CompilerParams(dimension_semantics=sem, vmem_limit_bytes=VMEM_LIMIT)


def _tile(dim, pref):
    if dim <= pref:
        return dim
    t = (pref // LANE) * LANE
    while t >= LANE:
        if dim % t == 0:
            return t
        t -= LANE
    return dim


def _row_tile(rows, bytes_per_row, budget=8 << 20):
    best = None
    for t in range(8, rows + 1, 8):
        if rows % t == 0 and t * bytes_per_row <= budget:
            best = t
    return best if best else rows


def _dot_nt(a, b):
    return lax.dot_general(a, b, (((1,), (1,)), ((), ())), preferred_element_type=F32)


def _dot_tn(a, b):
    return lax.dot_general(a, b, (((0,), (0,)), ((), ())), preferred_element_type=F32)


def _dot(a, b):
    return jnp.dot(a, b, preferred_element_type=F32)


def _mm(a, b, *, ta=False, tb=False, a_win=None, b_win=None, add=None, add_scale=1.0, out_dtype=F32,
        tm=512, tn=512, tk=512, name="mm"):
    a_off, a_w = a_win if a_win else (0, a.shape[1])
    b_off, b_w = b_win if b_win else (0, b.shape[1])
    (K, M) = (a.shape[0], a_w) if ta else (a_w, a.shape[0])
    (N, Kb) = (b.shape[0], b_w) if tb else (b_w, b.shape[0])
    assert K == Kb, (a.shape, b.shape, ta, tb)
    tm, tn, tk = _tile(M, tm), _tile(N, tn), _tile(K, tk)
    nk = K // tk
    if ta:
        assert a_off % tm == 0
        a_spec = pl.BlockSpec((tk, tm), lambda i, j, k: (k, i + a_off // tm))
    else:
        assert a_off % tk == 0
        a_spec = pl.BlockSpec((tm, tk), lambda i, j, k: (i, k + a_off // tk))
    if tb:
        assert b_off % tk == 0
        b_spec = pl.BlockSpec((tn, tk), lambda i, j, k: (j, k + b_off // tk))
    else:
        assert b_off % tn == 0
        b_spec = pl.BlockSpec((tk, tn), lambda i, j, k: (k, j + b_off // tn))
    o_spec = pl.BlockSpec((tm, tn), lambda i, j, k: (i, j))
    dn = (((0 if ta else 1,), (1 if tb else 0,)), ((), ()))
    has_add = add is not None

    def body(*refs):
        a_ref, b_ref = refs[:2]
        add_ref = refs[2] if has_add else None
        o_ref = refs[3 if has_add else 2]
        part = lax.dot_general(a_ref[...].astype(BF16), b_ref[...].astype(BF16), dn, preferred_element_type=F32)

        def finish(r):
            if has_add:
                r = r + add_scale * add_ref[...]
            o_ref[...] = r.astype(o_ref.dtype)

        if nk == 1:
            finish(part)
            return
        acc_ref = refs[-1]
        k = pl.program_id(2)

        @pl.when(k == 0)
        def _():
            acc_ref[...] = part

        @pl.when(k > 0)
        def _():
            acc_ref[...] += part

        @pl.when(k == nk - 1)
        def _():
            finish(acc_ref[...])

    ins = [a, b] + ([add] if has_add else [])
    specs = [a_spec, b_spec] + ([o_spec] if has_add else [])
    return pl.pallas_call(
        body, name=name, grid=(M // tm, N // tn, nk), in_specs=specs, out_specs=o_spec,
        out_shape=jax.ShapeDtypeStruct((M, N), out_dtype),
        scratch_shapes=[pltpu.VMEM((tm, tn), F32)] if nk > 1 else [],
        compiler_params=_params(("parallel", "parallel", "arbitrary")))(*ins)


GELU_K = math.sqrt(2.0 / math.pi)


def _gelu(x):
    t = jnp.tanh(GELU_K * (x + 0.044715 * (x * x * x)))
    return 0.5 * x * (1.0 + t)


def _gelu_grad(x):
    t = jnp.tanh(GELU_K * (x + 0.044715 * (x * x * x)))
    return 0.5 * (1.0 + t) + 0.5 * x * (1.0 - t * t) * GELU_K * (1.0 + 3.0 * 0.044715 * x * x)


def _rope_swap(t):
    lane = lax.broadcasted_iota(jnp.int32, t.shape, 1)
    return jnp.where(lane < 32, pltpu.roll(t, 96, axis=1), pltpu.roll(t, 32, axis=1))


def _rope(t, c, s):
    return t * c + _rope_swap(t) * s


def _rope_bwd(dt, c, s):
    return dt * c - _rope_swap(dt) * s


def _row_spec(tm, w, cb=0):
    return pl.BlockSpec((tm, w), lambda i: (i, cb))


def _fix_spec(shape):
    return pl.BlockSpec(shape, lambda *_: (0,) * len(shape))


def _rms_fwd(h, off, width, g, name):
    S = h.shape[0]
    tm = _tile(S, 2048)

    def body(x_ref, g_ref, o_ref):
        x = x_ref[...]
        r = lax.rsqrt(jnp.mean(x * x, axis=1, keepdims=True) + RMS_EPS)
        o_ref[...] = (x * r * g_ref[...]).astype(BF16)

    return pl.pallas_call(
        body, name=name, grid=(S // tm,), in_specs=[_row_spec(tm, width, off // width), _fix_spec((1, width))],
        out_specs=_row_spec(tm, width), out_shape=jax.ShapeDtypeStruct((S, width), BF16),
        compiler_params=_params(("parallel",)))(h, g.reshape(1, width))


def _rms_bwd(h, off, width, g, dxn, name):
    S = h.shape[0]
    tm = _tile(S, 2048)

    def body(x_ref, g_ref, d_ref, dx_ref, dg_ref):
        @pl.when(pl.program_id(0) == 0)
        def _():
            dg_ref[...] = jnp.zeros_like(dg_ref)

        x, d = x_ref[...], d_ref[...]
        r = lax.rsqrt(jnp.mean(x * x, axis=1, keepdims=True) + RMS_EPS)
        gd = d * g_ref[...]
        dx_ref[...] = gd * r - x * (r * r * r) * jnp.mean(gd * x, axis=1, keepdims=True)
        dg_ref[...] += jnp.sum(d * x * r, axis=0, keepdims=True)

    return pl.pallas_call(
        body, name=name, grid=(S // tm,),
        in_specs=[_row_spec(tm, width, off // width), _fix_spec((1, width)), _row_spec(tm, width)],
        out_specs=[_row_spec(tm, width), _fix_spec((1, width))],
        out_shape=[jax.ShapeDtypeStruct((S, width), F32), jax.ShapeDtypeStruct((1, width), F32)],
        compiler_params=_params(("arbitrary",)))(h, g.reshape(1, width), dxn)


def _q_proj(xn, w, rc, rs):
    S = xn.shape[0]
    tm = _tile(S, 2048)

    def body(x_ref, w_ref, c_ref, s_ref, q_ref):
        q = _dot(x_ref[...], w_ref[...]) * MLA_SCALE
        q_ref[:, :LANE] = q[:, :LANE].astype(BF16)
        q_ref[:, LANE:] = _rope(q[:, LANE:], c_ref[...], s_ref[...]).astype(BF16)

    return pl.pallas_call(
        body, name="q_proj", grid=(S // tm, MLA_HEADS),
        in_specs=[pl.BlockSpec((tm, 512), lambda i, j: (i, 0)), pl.BlockSpec((512, 256), lambda i, j: (0, j)),
                  pl.BlockSpec((tm, LANE), lambda i, j: (i, 0)), pl.BlockSpec((tm, LANE), lambda i, j: (i, 0))],
        out_specs=pl.BlockSpec((tm, 256), lambda i, j: (i, j)),
        out_shape=jax.ShapeDtypeStruct((S, MLA_HEADS * 256), BF16),
        compiler_params=_params(("parallel", "parallel")))(xn, w, rc, rs)


def _kv_proj(xn, w, h, rc, rs):
    S = xn.shape[0]
    tm = _tile(S, 2048)

    def body(x_ref, w_ref, pe_ref, c_ref, s_ref, k_ref, v_ref):
        kv = _dot(x_ref[...], w_ref[...])
        k_ref[:, :LANE] = kv[:, :LANE].astype(BF16)
        k_ref[:, LANE:] = _rope(pe_ref[...], c_ref[...], s_ref[...]).astype(BF16)
        v_ref[...] = kv[:, LANE:].astype(BF16)

    return pl.pallas_call(
        body, name="kv_proj", grid=(S // tm, MLA_HEADS),
        in_specs=[pl.BlockSpec((tm, 256), lambda i, j: (i, 0)), pl.BlockSpec((256, 256), lambda i, j: (0, j)),
                  pl.BlockSpec((tm, LANE), lambda i, j: (i, KPE // LANE)),
                  pl.BlockSpec((tm, LANE), lambda i, j: (i, 0)), pl.BlockSpec((tm, LANE), lambda i, j: (i, 0))],
        out_specs=[pl.BlockSpec((tm, 256), lambda i, j: (i, j)), pl.BlockSpec((tm, LANE), lambda i, j: (i, j))],
        out_shape=[jax.ShapeDtypeStruct((S, MLA_HEADS * 256), BF16), jax.ShapeDtypeStruct((S, MLA_HEADS * LANE), BF16)],
        compiler_params=_params(("parallel", "parallel")))(xn, w, h, rc, rs)


def _kv_bwd_prep(dk, dv, rc, rs):
    S = dk.shape[1]
    tm = _tile(S, 1024)

    def body(dk_ref, dv_ref, c_ref, s_ref, o_ref, pe_ref):
        rot = jnp.zeros((tm, LANE), F32)
        for hh in range(MLA_HEADS):
            o_ref[:, hh * 256:hh * 256 + LANE] = dk_ref[hh, :, :LANE].astype(BF16)
            o_ref[:, hh * 256 + LANE:(hh + 1) * 256] = dv_ref[hh].astype(BF16)
            rot = rot + dk_ref[hh, :, LANE:]
        pe_ref[...] = _rope_bwd(rot, c_ref[...], s_ref[...])

    return pl.pallas_call(
        body, name="kv_bwd_prep", grid=(S // tm,),
        in_specs=[pl.BlockSpec((MLA_HEADS, tm, 256), lambda i: (0, i, 0)),
                  pl.BlockSpec((MLA_HEADS, tm, LANE), lambda i: (0, i, 0)), _row_spec(tm, LANE), _row_spec(tm, LANE)],
        out_specs=[_row_spec(tm, MLA_HEADS * 256), _row_spec(tm, LANE)],
        out_shape=[jax.ShapeDtypeStruct((S, MLA_HEADS * 256), BF16), jax.ShapeDtypeStruct((S, LANE), F32)],
        compiler_params=_params(("parallel",)))(dk, dv, rc, rs)


def _chunk_mask(T):
    row = lax.broadcasted_iota(jnp.int32, (T, T), 0)
    col = lax.broadcasted_iota(jnp.int32, (T, T), 1)
    return (col // CHUNK) <= (row // CHUNK)


def _att_blocks(S, q_blk=None, k_blk=None):
    tq = min(q_blk or Q_BLK, S)
    tk = min(k_blk or K_BLK, tq)
    return tq, tk, tq // tk


def _tail_masks(rows, tk):
    row = lax.broadcasted_iota(jnp.int32, (rows, tk), 0)
    col = lax.broadcasted_iota(jnp.int32, (rows, tk), 1)
    return (col // CHUNK) <= (row // CHUNK), col < row


def _span_masks(tk, r):
    row = lax.broadcasted_iota(jnp.int32, (tk, (r + 1) * tk), 0) + r * tk
    col = lax.broadcasted_iota(jnp.int32, (tk, (r + 1) * tk), 1)
    return (col // CHUNK) <= (row // CHUNK), col < row


def _put_rows(old, new, r0):
    return new if r0 == 0 else jnp.concatenate([old[:r0], new], axis=0)


def _mla_fwd(q, kp, v, nxt=None):
    S = q.shape[0]
    TQ, TK, n = _att_blocks(S, None, MLA_FWD_K_BLK)
    nq = S // TQ
    na = len(nxt) if nxt else 0

    def ride_along(srcs, outs, send_sems, recv_sems):
        x, y, c, chips = _place()

        def copy(a, j, px, py, chip):
            return pltpu.make_async_remote_copy(
                src_ref=srcs[a], dst_ref=outs[a].at[chip], send_sem=send_sems.at[3 * a + j],
                recv_sem=recv_sems.at[3 * a + j], device_id=(px, py, c), device_id_type=MESH)

        first = jnp.logical_and(pl.program_id(0) == 0, pl.program_id(1) == 0)
        last = jnp.logical_and(pl.program_id(0) == MLA_HEADS - 1, pl.program_id(1) == nq - 1)

        @pl.when(first)
        def _():
            for a in range(na):
                for j, (px, py) in enumerate(chips):
                    copy(a, j, px, py, 2 * x + y).start()

        @pl.when(last)
        def _():
            for a in range(na):
                for j, (px, py) in enumerate(chips):
                    copy(a, j, px, py, 2 * x + y).wait_send()
                    copy(a, j, px, py, 2 * px + py).wait_recv()

    def body(q_ref, k_ref, v_ref, *rest):
        if na:
            o_ref, lse_ref = rest[na:na + 2]
            ride_along(rest[:na], rest[na + 2:2 * na + 2], *rest[2 * na + 2:])
        else:
            o_ref, lse_ref = rest
        i = pl.program_id(1)

        def update(carry, qb, keys, mask):
            m, l, acc = carry
            s = _dot_nt(qb, k_ref[keys, :])
            if mask is not None:
                s = jnp.where(mask, s, -1e30)
            m_new = jnp.maximum(m, jnp.max(s, axis=1, keepdims=True))
            a = jnp.exp(m - m_new)
            p = jnp.exp(s - m_new)
            return m_new, a * l + jnp.sum(p, axis=1, keepdims=True), a * acc + _dot(p.astype(BF16), v_ref[keys, :])

        carry = (jnp.full((TQ, 1), -1e30, F32), jnp.zeros((TQ, 1), F32), jnp.zeros((TQ, LANE), F32))
        carry = lax.fori_loop(
            0, i * n, lambda j, c: update(c, q_ref[...], pl.ds(pl.multiple_of(j * TK, TK), TK), None), carry)
        for r in range(n):
            rows = slice(r * TK, (r + 1) * TK)
            m, l, acc = update(tuple(c[rows] for c in carry), q_ref[rows, :],
                               pl.ds(pl.multiple_of(i * TQ, TQ), (r + 1) * TK), _span_masks(TK, r)[0])
            o_ref[rows, :] = acc / l
            lse_ref[rows, :] = jnp.broadcast_to(m + jnp.log(l), (TK, LANE))

    res = pl.pallas_call(
        body, name="mla_fwd_gather" if na else "mla_fwd", grid=(MLA_HEADS, nq),
        in_specs=[pl.BlockSpec((TQ, 256), lambda h, i: (i, h)),
                  pl.BlockSpec((S, 256), lambda h, i: (0, h), pipeline_mode=pl.Buffered(1)),
                  pl.BlockSpec((S, LANE), lambda h, i: (0, h), pipeline_mode=pl.Buffered(1))] + [HBM_SPEC] * na,
        out_specs=[pl.BlockSpec((TQ, LANE), lambda h, i: (i, h)), pl.BlockSpec((TQ, LANE), lambda h, i: (i, h))]
                  + [HBM_SPEC] * na,
        out_shape=[jax.ShapeDtypeStruct((S, MLA_HEADS * LANE), F32), jax.ShapeDtypeStruct((S, MLA_HEADS * LANE), F32)]
                  + [jax.ShapeDtypeStruct((4,) + s.shape, s.dtype) for s in (nxt or [])],
        scratch_shapes=[pltpu.SemaphoreType.DMA((3 * na,)), pltpu.SemaphoreType.DMA((3 * na,))] if na else [],
        compiler_params=pltpu.CompilerParams(dimension_semantics=("arbitrary", "arbitrary"), vmem_limit_bytes=VMEM_LIMIT,
                                             has_side_effects=bool(na)))(q, kp, v, *(nxt or []))
    return res[0], res[1], (list(res[2:]) if na else None)


def _mla_bwd(q, kp, v, do_cat, o_cat, lse, rc, rs):
    S = q.shape[0]
    TQ, TK, n = _att_blocks(S)
    nq = S // TQ

    def body(q_ref, k_ref, v_ref, do_ref, o_ref, lse_ref, c_ref, s_ref, dq_ref, dk_hbm, dv_hbm, dk_acc, dv_acc):
        h, i = pl.program_id(0), pl.program_id(1)

        @pl.when(i == 0)
        def _():
            dk_acc[...] = jnp.zeros_like(dk_acc)
            dv_acc[...] = jnp.zeros_like(dv_acc)

        do32 = do_ref[...]
        dob = do32.astype(BF16)
        delta = jnp.sum(do32 * o_ref[...], axis=1, keepdims=True)
        lse_col = lse_ref[:, :1]

        def blk(j, dq, r0, masked):
            sl = pl.ds(pl.multiple_of(j * TK, TK), TK)
            kb, vb, qb = k_ref[sl, :], v_ref[sl, :], q_ref[r0:, :]
            s = _dot_nt(qb, kb)
            if masked:
                s = jnp.where(_tail_masks(TQ - r0, TK)[0], s, -1e30)
            p = jnp.exp(s - lse_col[r0:])
            ds = (p * (_dot_nt(dob[r0:], vb) - delta[r0:])).astype(BF16)
            dk_acc[sl, :] += _dot_tn(ds, qb)
            dv_acc[sl, :] += _dot_tn(p.astype(BF16), dob[r0:])
            return _put_rows(dq, dq[r0:] + _dot(ds, kb), r0)

        dq = lax.fori_loop(0, i * n, lambda j, c: blk(j, c, 0, False), jnp.zeros((TQ, 256), F32))
        for t in range(n):
            dq = blk(i * n + t, dq, t * TK, True)
        dq_ref[:, :LANE] = (dq[:, :LANE] * MLA_SCALE).astype(BF16)
        dq_ref[:, LANE:] = _rope_bwd(dq[:, LANE:] * MLA_SCALE, c_ref[...], s_ref[...]).astype(BF16)

        @pl.when(i == nq - 1)
        def _():
            pltpu.sync_copy(dk_acc, dk_hbm.at[h])
            pltpu.sync_copy(dv_acc, dv_hbm.at[h])

    any_spec = pl.BlockSpec(memory_space=pl.ANY)
    T = TQ
    rows = pl.BlockSpec((T, LANE), lambda h, i: (i, 0))
    return pl.pallas_call(
        body, name="mla_bwd", grid=(MLA_HEADS, nq),
        in_specs=[pl.BlockSpec((T, 256), lambda h, i: (i, h)),
                  pl.BlockSpec((S, 256), lambda h, i: (0, h), pipeline_mode=pl.Buffered(1)),
                  pl.BlockSpec((S, LANE), lambda h, i: (0, h), pipeline_mode=pl.Buffered(1)),
                  pl.BlockSpec((T, LANE), lambda h, i: (i, h)),
                  pl.BlockSpec((T, LANE), lambda h, i: (i, h)), pl.BlockSpec((T, LANE), lambda h, i: (i, h)), rows, rows],
        out_specs=[pl.BlockSpec((T, 256), lambda h, i: (i, h)), any_spec, any_spec],
        out_shape=[jax.ShapeDtypeStruct((S, MLA_HEADS * 256), BF16), jax.ShapeDtypeStruct((MLA_HEADS, S, 256), F32),
                   jax.ShapeDtypeStruct((MLA_HEADS, S, LANE), F32)],
        scratch_shapes=[pltpu.VMEM((S, 256), F32), pltpu.VMEM((S, LANE), F32)],
        compiler_params=_params(("arbitrary", "arbitrary")))(q, kp, v, do_cat, o_cat, lse, rc, rs)


def _split_dot(x, tri):
    top = lax.bitcast_convert_type(lax.bitcast_convert_type(x, jnp.uint32) & jnp.uint32(0xFFFF0000), F32)
    return _dot(top.astype(BF16), tri) + _dot((x - top).astype(BF16), tri)


def _sb_block(qb, kb, tri, carry, masked):
    z = _dot_nt(qb, kb)
    lb = jnp.minimum(z, 0.0) - jnp.log(1.0 + jnp.exp(-jnp.abs(z)))
    lm = lb - z
    strict = None
    if masked:
        strict = _tail_masks(z.shape[0], z.shape[1])[1]
        lm = jnp.where(strict, lm, 0.0)
    a = jnp.exp(lb + carry + _split_dot(lm, tri))
    if masked:
        a = jnp.where(strict, a, 0.0)
    return a, lb, lm, strict


def _sb_walk(blk, j0, state):
    def alive(c):
        return jnp.logical_and(c[0] >= 0, jnp.max(c[1][0]) > SB_DEAD)

    return lax.while_loop(alive, lambda c: (c[0] - 1, blk(c[0], c[1], 0, False)), (j0, state))[1]


def _triangle(tk):
    row = lax.broadcasted_iota(jnp.int32, (tk, tk), 0)
    col = lax.broadcasted_iota(jnp.int32, (tk, tk), 1)
    return (row > col).astype(BF16)


def _sb_fwd(qkv):
    S = qkv.shape[0]
    TQ, TK, n = _att_blocks(S, SB_Q_BLK, SB_K_BLK)
    T = TQ

    def body(q_ref, k_ref, v_ref, o_ref):
        i = pl.program_id(1)
        tri = _triangle(TK)

        def blk(j, state, r0, masked):
            carry, acc = (c[r0:] for c in state)
            sl = pl.ds(pl.multiple_of(j * TK, TK), TK)
            a, _, lm, _ = _sb_block(q_ref[r0:, :], k_ref[sl, :], tri, carry, masked)
            new = (carry + jnp.sum(lm, axis=1, keepdims=True), acc + _dot(a.astype(BF16), v_ref[sl, :]))
            return tuple(_put_rows(c, u, r0) for c, u in zip(state, new))

        state = (jnp.zeros((TQ, 1), F32), jnp.zeros((TQ, LANE), F32))
        for t in reversed(range(n)):
            state = blk(i * n + t, state, t * TK, True)
        state = _sb_walk(blk, i * n - 1, state)
        o_ref[...] = state[1]

    return pl.pallas_call(
        body, name="sb_fwd", grid=(SB_HEADS, S // T),
        in_specs=[pl.BlockSpec((T, LANE), lambda h, i: (i, h)), pl.BlockSpec((S, LANE), lambda h, i: (0, 4 + h)),
                  pl.BlockSpec((S, LANE), lambda h, i: (0, 8 + h))],
        out_specs=pl.BlockSpec((T, LANE), lambda h, i: (i, h)),
        out_shape=jax.ShapeDtypeStruct((S, SB_HEADS * LANE), F32),
        compiler_params=_params(("parallel", "arbitrary")))(qkv, qkv, qkv)


def _sb_bwd(qkv, do_cat, o_cat, col0):
    S = qkv.shape[0]
    TQ, TK, n = _att_blocks(S, SB_Q_BLK, SB_K_BLK)
    T = TQ
    nq = S // TQ

    def body(q_ref, k_ref, v_ref, do_ref, o_ref, dq_ref, dk_hbm, dv_hbm, dk_acc, dv_acc):
        h, i = pl.program_id(0), pl.program_id(1)

        @pl.when(i == 0)
        def _():
            dk_acc[...] = jnp.zeros_like(dk_acc)
            dv_acc[...] = jnp.zeros_like(dv_acc)

        dob = do_ref[...].astype(BF16)
        tri = _triangle(TK)
        rest0 = jnp.sum(dob.astype(F32) * o_ref[...], axis=1, keepdims=True)

        def blk(j, state, r0, masked):
            carry, rest, dq = (c[r0:] for c in state)
            sl = pl.ds(pl.multiple_of(j * TK, TK), TK)
            kb, vb, qb = k_ref[sl, :], v_ref[sl, :], q_ref[r0:, :]
            a, lb, lm, strict = _sb_block(qb, kb, tri, carry, masked)
            ab = a.astype(BF16)
            e = ab.astype(F32) * _dot_nt(dob[r0:], vb)
            dz = e - jnp.exp(lb) * (rest - _split_dot(e, tri))
            if masked:
                dz = jnp.where(strict, dz, 0.0)
            dzb = dz.astype(BF16)
            dk_acc[sl, :] += _dot_tn(dzb, qb)
            dv_acc[sl, :] += _dot_tn(ab, dob[r0:])
            new = (carry + jnp.sum(lm, axis=1, keepdims=True), rest - jnp.sum(e, axis=1, keepdims=True),
                   dq + _dot(dzb, kb))
            return tuple(_put_rows(c, u, r0) for c, u in zip(state, new))

        state = (jnp.zeros((TQ, 1), F32), rest0, jnp.zeros((TQ, LANE), F32))
        for t in reversed(range(n)):
            state = blk(i * n + t, state, t * TK, True)
        state = _sb_walk(blk, i * n - 1, state)
        dq_ref[...] = state[2] * SB_SCALE

        @pl.when(i == nq - 1)
        def _():
            lanes = pl.ds(pl.multiple_of(h * LANE, LANE), LANE)
            pltpu.sync_copy(dk_acc, dk_hbm.at[:, lanes])
            pltpu.sync_copy(dv_acc, dv_hbm.at[:, lanes])

    any_spec = pl.BlockSpec(memory_space=pl.ANY)
    return pl.pallas_call(
        body, name="sb_bwd", grid=(SB_HEADS, nq),
        in_specs=[pl.BlockSpec((T, LANE), lambda h, i: (i, h)), pl.BlockSpec((S, LANE), lambda h, i: (0, 4 + h)),
                  pl.BlockSpec((S, LANE), lambda h, i: (0, 8 + h)),
                  pl.BlockSpec((T, LANE), lambda h, i: (i, col0 + h)), pl.BlockSpec((T, LANE), lambda h, i: (i, col0 + h))],
        out_specs=[pl.BlockSpec((T, LANE), lambda h, i: (i, h)), any_spec, any_spec],
        out_shape=[jax.ShapeDtypeStruct((S, SB_HEADS * LANE), F32)] * 3,
        scratch_shapes=[pltpu.VMEM((S, LANE), F32), pltpu.VMEM((S, LANE), F32)],
        compiler_params=_params(("arbitrary", "arbitrary")))(qkv, qkv, qkv, do_cat, o_cat)


def _mem_probs(q, k_ref, hh):
    lane = lax.broadcasted_iota(jnp.int32, (1, 256), 1) // 64
    msk = lane == hh
    qh = jnp.where(msk, q, 0.0).astype(BF16)
    s = _dot_nt(qh, k_ref[...]) * MEM_SCALE
    p = jnp.exp(s - jnp.max(s, axis=1, keepdims=True))
    return msk, qh, p / jnp.sum(p, axis=1, keepdims=True)


def _mem_fwd(h, mk, mv):
    S = h.shape[0]
    tm = _tile(S, 512)

    def body(q_ref, k_ref, v_ref, o_ref):
        q = q_ref[...]
        out = jnp.zeros((tm, 256), F32)
        for hh in range(MEM_HEADS):
            msk, _, p = _mem_probs(q, k_ref, hh)
            out = out + jnp.where(msk, _dot(p.astype(BF16), v_ref[...]), 0.0)
        o_ref[...] = out

    return pl.pallas_call(
        body, name="mem_fwd", grid=(S // tm,),
        in_specs=[_row_spec(tm, 256, MQ // 256), _fix_spec((256, 256)), _fix_spec((256, 256))],
        out_specs=_row_spec(tm, 256), out_shape=jax.ShapeDtypeStruct((S, 256), F32),
        compiler_params=_params(("parallel",)))(h, mk, mv)


def _mem_bwd(h, mk, mv, do_cat, col0):
    S = h.shape[0]
    tm = _tile(S, 512)

    def body(q_ref, k_ref, v_ref, do_ref, dq_ref, dk_ref, dv_ref):
        @pl.when(pl.program_id(0) == 0)
        def _():
            dk_ref[...] = jnp.zeros_like(dk_ref)
            dv_ref[...] = jnp.zeros_like(dv_ref)

        q, do = q_ref[...], do_ref[...]
        dq = jnp.zeros((tm, 256), F32)
        for hh in range(MEM_HEADS):
            msk, qh, p = _mem_probs(q, k_ref, hh)
            doh = jnp.where(msk, do, 0.0).astype(BF16)
            dp = _dot_nt(doh, v_ref[...])
            ds = (p * (dp - jnp.sum(p * dp, axis=1, keepdims=True)) * MEM_SCALE).astype(BF16)
            dq = dq + jnp.where(msk, _dot(ds, k_ref[...]), 0.0)
            dk_ref[...] += _dot_tn(ds, qh)
            dv_ref[...] += _dot_tn(p.astype(BF16), doh)
        dq_ref[...] = dq

    return pl.pallas_call(
        body, name="mem_bwd", grid=(S // tm,),
        in_specs=[_row_spec(tm, 256, MQ // 256), _fix_spec((256, 256)), _fix_spec((256, 256)),
                  _row_spec(tm, 256, col0 // 256)],
        out_specs=[_row_spec(tm, 256), _fix_spec((256, 256)), _fix_spec((256, 256))],
        out_shape=[jax.ShapeDtypeStruct((S, 256), F32), jax.ShapeDtypeStruct((256, 256), F32),
                   jax.ShapeDtypeStruct((256, 256), F32)],
        compiler_params=_params(("arbitrary",)))(h, mk, mv, do_cat)


SG_T = 128


def _sg_norm(sv, g, b):
    gv = _gelu(sv)
    xc = gv - jnp.mean(gv, axis=1, keepdims=True)
    rstd = lax.rsqrt(jnp.mean(xc * xc, axis=1, keepdims=True) + LN_EPS)
    xhat = xc * rstd
    return xhat, rstd, xhat * g + b


def _sg_fwd(h, lng, lnb, w, bias_t):
    S = h.shape[0]
    tm = _tile(S, 512)

    def body(u_ref, v_ref, g_ref, b_ref, w_ref, bias_ref, o_ref):
        mask = _chunk_mask(SG_T)
        for n in range(tm // SG_T):
            rows = slice(n * SG_T, (n + 1) * SG_T)
            u = _gelu(u_ref[rows, :])
            _, _, vn = _sg_norm(v_ref[rows, :], g_ref[...], b_ref[...])
            vb = vn.astype(BF16)
            for gi in range(4):
                cols = slice(gi * LANE, (gi + 1) * LANE)
                wg = jnp.where(mask, w_ref[gi], 0.0).astype(BF16)
                mixed = _dot(wg, vb[:, cols]) + bias_ref[:, gi:gi + 1]
                o_ref[rows, cols] = u[:, cols] * mixed

    return pl.pallas_call(
        body, name="sg_fwd", grid=(S // tm,),
        in_specs=[_row_spec(tm, 512, SGU // 512), _row_spec(tm, 512, SGV // 512), _fix_spec((1, 512)),
                  _fix_spec((1, 512)), _fix_spec((4, SG_T, SG_T)), _fix_spec((SG_T, 4))],
        out_specs=_row_spec(tm, 512), out_shape=jax.ShapeDtypeStruct((S, 512), F32),
        compiler_params=_params(("parallel",)))(h, h, lng.reshape(1, 512), lnb.reshape(1, 512), w, bias_t)


def _sg_bwd(h, lng, lnb, w, bias_t, do_cat, col0):
    S = h.shape[0]
    tm = _tile(S, 512)
    nsteps = S // tm

    def body(u_ref, v_ref, g_ref, b_ref, w_ref, bias_ref, do0_ref, do1_ref, do2_ref, do3_ref,
             du_ref, dv_ref, dw_ref, dbias_ref, dg_ref, db_ref, dvn_scr, dbias_acc):
        do_refs = (do0_ref, do1_ref, do2_ref, do3_ref)
        step = pl.program_id(0)

        @pl.when(step == 0)
        def _():
            dw_ref[...] = jnp.zeros_like(dw_ref)
            dg_ref[...] = jnp.zeros_like(dg_ref)
            db_ref[...] = jnp.zeros_like(db_ref)
            dbias_acc[...] = jnp.zeros_like(dbias_acc)

        mask = _chunk_mask(SG_T)
        for n in range(tm // SG_T):
            rows = slice(n * SG_T, (n + 1) * SG_T)
            su, sv = u_ref[rows, :], v_ref[rows, :]
            u = _gelu(su)
            xhat, rstd, vn = _sg_norm(sv, g_ref[...], b_ref[...])
            vb = vn.astype(BF16)
            ugrad = _gelu_grad(su)
            for gi in range(4):
                cols = slice(gi * LANE, (gi + 1) * LANE)
                do = do_refs[gi][rows, :]
                wg = jnp.where(mask, w_ref[gi], 0.0).astype(BF16)
                mixed = _dot(wg, vb[:, cols]) + bias_ref[:, gi:gi + 1]
                dmixed = do * u[:, cols]
                dmb = dmixed.astype(BF16)
                du_ref[rows, cols] = do * mixed * ugrad[:, cols]
                dvn_scr[:, cols] = _dot_tn(wg, dmb)
                dw_ref[gi] += jnp.where(mask, _dot_nt(dmb, vb[:, cols]), 0.0)
                dbias_acc[gi] += dmixed
            dvn = dvn_scr[...]
            dg_ref[...] += jnp.sum(dvn * xhat, axis=0, keepdims=True)
            db_ref[...] += jnp.sum(dvn, axis=0, keepdims=True)
            dxh = dvn * g_ref[...]
            dgv = rstd * (dxh - jnp.mean(dxh, axis=1, keepdims=True)
                          - xhat * jnp.mean(dxh * xhat, axis=1, keepdims=True))
            dv_ref[rows, :] = dgv * _gelu_grad(sv)

        @pl.when(step == nsteps - 1)
        def _():
            for gi in range(4):
                dbias_ref[:, gi:gi + 1] = jnp.sum(dbias_acc[gi], axis=1, keepdims=True)

    return pl.pallas_call(
        body, name="sg_bwd", grid=(nsteps,),
        in_specs=[_row_spec(tm, 512, SGU // 512), _row_spec(tm, 512, SGV // 512), _fix_spec((1, 512)),
                  _fix_spec((1, 512)), _fix_spec((4, SG_T, SG_T)), _fix_spec((SG_T, 4))]
                 + [_row_spec(tm, LANE, col0 // LANE + gi) for gi in range(4)],
        out_specs=[_row_spec(tm, 512), _row_spec(tm, 512), _fix_spec((4, SG_T, SG_T)), _fix_spec((SG_T, 4)),
                   _fix_spec((1, 512)), _fix_spec((1, 512))],
        out_shape=[jax.ShapeDtypeStruct((S, 512), F32), jax.ShapeDtypeStruct((S, 512), F32),
                   jax.ShapeDtypeStruct((4, SG_T, SG_T), F32), jax.ShapeDtypeStruct((SG_T, 4), F32),
                   jax.ShapeDtypeStruct((1, 512), F32), jax.ShapeDtypeStruct((1, 512), F32)],
        scratch_shapes=[pltpu.VMEM((SG_T, 512), F32), pltpu.VMEM((4, SG_T, SG_T), F32)],
        compiler_params=_params(("arbitrary",)))(h, h, lng.reshape(1, 512), lnb.reshape(1, 512), w, bias_t,
                                                 do_cat, do_cat, do_cat, do_cat)


def _gate_out_ln(branches, h, w_out, x, g, b):
    S = h.shape[0]
    tm = _tile(S, 256)
    widths = [a.shape[1] for a in branches]

    def body(oa_ref, ob_ref, oc_ref, om_ref, g0_ref, g1_ref, g2_ref, g3_ref, w_ref, x_ref, lg_ref, lb_ref,
             cat_ref, yg_ref, xo_ref, xb_ref, r_ref):
        at = 0
        for ref, width in zip((oa_ref, ob_ref, oc_ref, om_ref), widths):
            cat_ref[:, at:at + width] = ref[...]
            at += width
        for j, g_ref in enumerate((g0_ref, g1_ref, g2_ref, g3_ref)):
            gate = g_ref[...]
            cols = slice(j * 512, (j + 1) * 512)
            yg_ref[:, cols] = (cat_ref[:, cols] * (gate * jax.nn.sigmoid(gate))).astype(BF16)
        r = ALPHA * x_ref[...] + _dot(yg_ref[...], w_ref[...])
        r_ref[...] = r
        xc = r - jnp.mean(r, axis=1, keepdims=True)
        o = xc * lax.rsqrt(jnp.mean(xc * xc, axis=1, keepdims=True) + LN_EPS) * lg_ref[...] + lb_ref[...]
        xo_ref[...] = o
        xb_ref[...] = o.astype(BF16)

    row = _row_spec(tm, D_MODEL)
    return pl.pallas_call(
        body, name="gate_out_ln", grid=(S // tm,),
        in_specs=[_row_spec(tm, width) for width in widths] + [_row_spec(tm, 512, GATE // 512 + j) for j in range(4)]
                 + [_fix_spec((D_MODEL, D_MODEL)), row, _fix_spec((1, D_MODEL)), _fix_spec((1, D_MODEL))],
        out_specs=[row] * 5,
        out_shape=[jax.ShapeDtypeStruct((S, D_MODEL), t) for t in (F32, BF16, F32, BF16, F32)],
        compiler_params=_params(("parallel",)))(*branches, h, h, h, h, w_out, x, g.reshape(1, D_MODEL), b.reshape(1, D_MODEL))


def _out_proj_gate_bwd(dr, w_out, o_cat, h):
    S = h.shape[0]
    tm = _tile(S, 1024)

    def body(dr_ref, w_ref, o_ref, g_ref, do_ref, dg_ref, drb):
        @pl.when(pl.program_id(1) == 0)
        def _():
            drb[...] = dr_ref[...].astype(BF16)

        d = _dot_nt(drb[...], w_ref[...])
        g = g_ref[...]
        sig = jax.nn.sigmoid(g)
        do_ref[...] = d * (g * sig)
        dg_ref[...] = d * o_ref[...] * (sig * (1.0 + g * (1.0 - sig)))

    blk = pl.BlockSpec((tm, 512), lambda i, j: (i, j))
    return pl.pallas_call(
        body, name="d_out_proj_gate", grid=(S // tm, 4),
        in_specs=[pl.BlockSpec((tm, D_MODEL), lambda i, j: (i, 0)), pl.BlockSpec((512, D_MODEL), lambda i, j: (j, 0)),
                  blk, pl.BlockSpec((tm, 512), lambda i, j: (i, GATE // 512 + j))],
        out_specs=[blk, blk],
        out_shape=[jax.ShapeDtypeStruct((S, D_MODEL), F32), jax.ShapeDtypeStruct((S, D_MODEL), F32)],
        scratch_shapes=[pltpu.VMEM((tm, D_MODEL), BF16)],
        compiler_params=_params(("parallel", "arbitrary")))(dr, w_out, o_cat, h)


def _ln_res_bwd(dout, r, g):
    S = r.shape[0]
    tm = _tile(S, 512)

    def body(d_ref, r_ref, g_ref, dr_ref, dg_ref, db_ref):
        @pl.when(pl.program_id(0) == 0)
        def _():
            dg_ref[...] = jnp.zeros_like(dg_ref)
            db_ref[...] = jnp.zeros_like(db_ref)

        d, r = d_ref[...], r_ref[...]
        xc = r - jnp.mean(r, axis=1, keepdims=True)
        rstd = lax.rsqrt(jnp.mean(xc * xc, axis=1, keepdims=True) + LN_EPS)
        xhat = xc * rstd
        dxh = d * g_ref[...]
        dr_ref[...] = rstd * (dxh - jnp.mean(dxh, axis=1, keepdims=True)
                              - xhat * jnp.mean(dxh * xhat, axis=1, keepdims=True))
        dg_ref[...] += jnp.sum(d * xhat, axis=0, keepdims=True)
        db_ref[...] += jnp.sum(d, axis=0, keepdims=True)

    return pl.pallas_call(
        body, name="ln_res_bwd", grid=(S // tm,),
        in_specs=[_row_spec(tm, D_MODEL), _row_spec(tm, D_MODEL), _fix_spec((1, D_MODEL))],
        out_specs=[_row_spec(tm, D_MODEL), _fix_spec((1, D_MODEL)), _fix_spec((1, D_MODEL))],
        out_shape=[jax.ShapeDtypeStruct((S, D_MODEL), F32), jax.ShapeDtypeStruct((1, D_MODEL), F32),
                   jax.ShapeDtypeStruct((1, D_MODEL), F32)],
        compiler_params=_params(("arbitrary",)))(dout, r, g.reshape(1, D_MODEL))


def _loss_head(y, target):
    S = y.shape[0]
    tm = _tile(S, 512)

    def body(y_ref, t_ref, l_ref, d_ref):
        @pl.when(pl.program_id(0) == 0)
        def _():
            l_ref[...] = jnp.zeros_like(l_ref)

        diff = y_ref[...] - t_ref[...]
        d_ref[...] = diff * (1.0 / D_MODEL)
        per_row = jnp.mean(diff * diff, axis=1, keepdims=True)
        l_ref[...] += 0.5 * jnp.sum(per_row, axis=0, keepdims=True)

    return pl.pallas_call(
        body, name="loss_head", grid=(S // tm,), in_specs=[_row_spec(tm, D_MODEL), _row_spec(tm, D_MODEL)],
        out_specs=[_fix_spec((8, LANE)), _row_spec(tm, D_MODEL)],
        out_shape=[jax.ShapeDtypeStruct((8, LANE), F32), jax.ShapeDtypeStruct((S, D_MODEL), F32)],
        compiler_params=_params(("arbitrary",)))(y, target)


def _perm_table():
    table, at = [], 0
    for name in PERM_ORDER:
        start, width = ORIG[name]
        table.append((name, start, width, at))
        at += width
    return table


def _permute_w_in(by_chip):
    wc = by_chip.shape[-1]
    parts = []
    for _, start, width, _ in _perm_table():
        lo = start
        while lo < start + width:
            k = lo // wc
            hi = min(start + width, (k + 1) * wc)
            parts.append(by_chip[k, ..., lo - k * wc:hi - k * wc])
            lo = hi
    parts.append(jnp.zeros(by_chip.shape[1:-1] + (HP - D_IN,), by_chip.dtype))
    return jnp.concatenate(parts, axis=-1)


def _model_cols(wp, lo, hi):
    parts = []
    for _, start, width, at in sorted(_perm_table(), key=lambda t: t[1]):
        a, b = max(lo, start), min(hi, start + width)
        if a < b:
            parts.append(wp[..., at + a - start:at + b - start])
    return jnp.concatenate(parts, axis=-1)


def _rope_tables(positions):
    inv_freq = ROPE_THETA ** (-jnp.arange(0, 64, 2, dtype=F32) / 64)
    ang = positions.astype(F32)[:, None] * inv_freq[None, :]
    cos, sin, zero = jnp.cos(ang), jnp.sin(ang), jnp.zeros((positions.shape[0], 64), F32)
    return jnp.concatenate([cos, cos, zero], axis=1), jnp.concatenate([-sin, sin, zero], axis=1)


def _layer_weights(by_chip):
    w_in, w_uq, w_ukv, w_mem_k, w_mem_v, w_out = by_chip
    w_uq = jnp.concatenate([w_uq[k] for k in range(4)], axis=1)
    w_uq = jnp.pad(w_uq.reshape(512, MLA_HEADS, 192), ((0, 0), (0, 0), (0, 64))).reshape(512, MLA_HEADS * 256)
    return (_permute_w_in(w_in), w_uq, jnp.concatenate([w_ukv[k] for k in range(4)], axis=1),
            w_mem_k.reshape(D_MODEL, 256), w_mem_v.reshape(D_MODEL, 256), w_out.reshape(D_MODEL, D_MODEL))


def _local_step(x, mem, positions, target, w, layer_source, next_shards):
    rc, rs = _rope_tables(positions)
    mem_b = mem.astype(BF16)
    xb = x.astype(BF16)
    saved = []
    fetched = None
    for l in range(DEPTH):
        w_in, w_uq, w_ukv, w_mem_k, w_mem_v, w_out = layer_source(l, fetched)
        h = _mm(xb, w_in, tm=1024, tn=1152, tk=2048, name="in_proj")
        cq_n = _rms_fwd(h, CQ, 512, w["q_norm_g"][l], "rms_q")
        ckv_n = _rms_fwd(h, CKV, 256, w["kv_norm_g"][l], "rms_kv")
        q = _q_proj(cq_n, w_uq, rc, rs)
        kp, v = _kv_proj(ckv_n, w_ukv, h, rc, rs)
        o_a, lse, fetched = _mla_fwd(q, kp, v, next_shards(l))
        bias_t = w["sg_b"][l].T
        o_b = _sg_fwd(h, w["sg_ln_g"][l], w["sg_ln_b"][l], w["sg_w"][l], bias_t)
        qkv = jnp.concatenate([h[:, SBQ:SBQ + 512] * SB_SCALE, h[:, SBQ + 512:SBQ + 1536]], axis=1).astype(BF16)
        o_c = _sb_fwd(qkv)
        mk = _mm(mem_b, w_mem_k, out_dtype=BF16, name="mem_kv")
        mv = _mm(mem_b, w_mem_v, out_dtype=BF16, name="mem_kv")
        o_m = _mem_fwd(h, mk, mv)
        o_cat, yg, x_new, xb_new, r = _gate_out_ln((o_a, o_b, o_c, o_m), h, w_out, x, w["ln_g"][l], w["ln_b"][l])
        saved.append(dict(xb=xb, h=h, cq_n=cq_n, ckv_n=ckv_n, q=q, kp=kp, v=v, lse=lse, qkv=qkv, mk=mk, mv=mv,
                          o_cat=o_cat, yg=yg, r=r, w_in=w_in, w_uq=w_uq, w_ukv=w_ukv, w_out=w_out, bias_t=bias_t))
        x, xb = x_new, xb_new

    loss, dx = _loss_head(x, target)

    grads = {n: [None] * DEPTH for n in SHARDED + SMALL}
    for l in reversed(range(DEPTH)):
        s = saved[l]
        h = s["h"]
        dr, dlg, dlb = _ln_res_bwd(dx, s["r"], w["ln_g"][l])
        grads["ln_g"][l], grads["ln_b"][l] = dlg[0], dlb[0]
        grads["w_out"][l] = _mm(s["yg"], dr, ta=True, tm=1024, tn=1024, tk=2048, name="dw_out")
        do_cat, dgates = _out_proj_gate_bwd(dr, s["w_out"], s["o_cat"], h)
        dmq, dmk, dmv = _mem_bwd(h, s["mk"], s["mv"], do_cat, 1792)
        grads["w_mem_k"][l] = _mm(mem_b, dmk, ta=True, name="dw_mem")
        grads["w_mem_v"][l] = _mm(mem_b, dmv, ta=True, name="dw_mem")
        dsq, dsk, dsv = _sb_bwd(s["qkv"], do_cat, s["o_cat"], 1280 // LANE)
        du, dv, dsgw, dsgb, dsg_g, dsg_b = _sg_bwd(h, w["sg_ln_g"][l], w["sg_ln_b"][l], w["sg_w"][l], s["bias_t"],
                                                   do_cat, 768)
        grads["sg_w"][l], grads["sg_b"][l] = dsgw, dsgb.T
        grads["sg_ln_g"][l], grads["sg_ln_b"][l] = dsg_g[0], dsg_b[0]
        dq_raw, dk, dvv = _mla_bwd(s["q"], s["kp"], s["v"], do_cat, s["o_cat"], s["lse"], rc, rs)
        dkv, dkpe = _kv_bwd_prep(dk, dvv, rc, rs)
        dw_uq = _mm(s["cq_n"], dq_raw, ta=True, tn=1536, tk=2048, name="dw_uq")
        grads["w_uq"][l] = dw_uq.reshape(512, MLA_HEADS, 256)[:, :, :192].reshape(512, MLA_HEADS * 192)
        grads["w_ukv"][l] = _mm(s["ckv_n"], dkv, ta=True, tn=1536, tk=2048, name="dw_ukv")
        dcq_n = _mm(dq_raw, s["w_uq"], tb=True, tm=2048, tk=1536, name="d_cq")
        dckv_n = _mm(dkv, s["w_ukv"], tb=True, tm=2048, tk=1536, name="d_ckv")
        dcq, dqg = _rms_bwd(h, CQ, 512, w["q_norm_g"][l], dcq_n, "rms_q_bwd")
        dckv, dkvg = _rms_bwd(h, CKV, 256, w["kv_norm_g"][l], dckv_n, "rms_kv_bwd")
        grads["q_norm_g"][l], grads["kv_norm_g"][l] = dqg[0], dkvg[0]
        dh = jnp.concatenate([dcq, dckv, dmq, du, dv, dsq, dsk, dsv, dgates, dkpe], axis=1).astype(BF16)
        dw_in = _mm(s["xb"], dh, ta=True, tm=1024, tn=1152, tk=2048, name="dw_in")
        grads["w_in"][l] = dw_in
        dx = _mm(dh, s["w_in"], tb=True, add=dr, add_scale=ALPHA, tm=1024, tn=1024, tk=1920, name="d_in_proj")

    return loss, dx, grads


MESH = pl.DeviceIdType.MESH
HBM_SPEC = pl.BlockSpec(memory_space=pltpu.HBM)


def _place():
    x, y, c = lax.axis_index("x"), lax.axis_index("y"), lax.axis_index("c")
    return x, y, c, [(1 - x, y), (x, 1 - y), (1 - x, 1 - y)]


HALF = DEPTH // 2


def _comm_call(body, name, arrays, out_shapes, n_sems):
    return pl.pallas_call(
        body, name=name, in_specs=[HBM_SPEC] * len(arrays), out_specs=[HBM_SPEC] * len(out_shapes), out_shape=out_shapes,
        scratch_shapes=[pltpu.SemaphoreType.DMA((n_sems,)), pltpu.SemaphoreType.DMA((n_sems,))],
        compiler_params=pltpu.CompilerParams(has_side_effects=True))(*arrays)


def _gather_weights(shards):
    na = len(shards)

    def body(*refs):
        srcs, outs, (send_sems, recv_sems) = refs[:na], refs[na:2 * na], refs[2 * na:]
        x, y, c, chips = _place()
        mine = [pl.ds((s.shape[0] // 2) * c, s.shape[0] // 2) for s in shards]
        theirs = [pl.ds((s.shape[0] // 2) * (1 - c), s.shape[0] // 2) for s in shards]

        def copy(a, k, src_ref, chip, part, to):
            return pltpu.make_async_remote_copy(
                src_ref=src_ref, dst_ref=outs[a].at[chip, part], send_sem=send_sems.at[6 * a + k],
                recv_sem=recv_sems.at[6 * a + k], device_id=to, device_id_type=MESH)

        sent = [copy(a, j, srcs[a].at[mine[a]], 2 * x + y, mine[a], (px, py, c))
                for a in range(na) for j, (px, py) in enumerate(chips)]
        for cp in sent:
            cp.start()
        passed = []
        for j, (px, py) in enumerate(chips):
            for a in range(na):
                copy(a, j, srcs[a].at[mine[a]], 2 * px + py, mine[a], (px, py, c)).wait_recv()
                cp = copy(a, 3 + j, outs[a].at[2 * px + py, mine[a]], 2 * px + py, mine[a], (x, y, 1 - c))
                cp.start()
                passed.append(cp)
        for j, (px, py) in enumerate(chips):
            for a in range(na):
                copy(a, 3 + j, srcs[a].at[theirs[a]], 2 * px + py, theirs[a], (x, y, 1 - c)).wait_recv()
        for cp in sent + passed:
            cp.wait_send()

    return _comm_call(body, "gather_weights", shards, [jax.ShapeDtypeStruct((4,) + s.shape, s.dtype) for s in shards], 6 * na)


def _swap_halves(gs):
    na = len(gs)

    def body(*refs):
        srcs, outs, (send_sems, recv_sems) = refs[:na], refs[na:2 * na], refs[2 * na:]
        x, y, c, _ = _place()
        cps = [pltpu.make_async_remote_copy(
            src_ref=srcs[a].at[:, pl.ds(HALF * (1 - c), HALF)], dst_ref=outs[a], send_sem=send_sems.at[a],
            recv_sem=recv_sems.at[a], device_id=(x, y, 1 - c), device_id_type=MESH) for a in range(na)]
        for cp in cps:
            cp.start()
        for cp in cps:
            cp.wait()

    return _comm_call(body, "swap_halves", gs,
                      [jax.ShapeDtypeStruct((4, HALF) + g.shape[2:], g.dtype) for g in gs], na)


def _pair_sum(g, other, c):
    _, _, R, C = g.shape
    tr = _row_tile(R, 3 * C * 4)

    def body(c_ref, a_ref, b_ref, o_ref):
        o_ref[...] = (a_ref[...] + b_ref[...]).astype(BF16)

    blk = pl.BlockSpec((None, None, tr, C), lambda d, l, i, c_ref: (d, l, i, 0))
    return pl.pallas_call(
        body, name="pair_sum",
        grid_spec=pltpu.PrefetchScalarGridSpec(
            num_scalar_prefetch=1, grid=(4, HALF, R // tr),
            in_specs=[pl.BlockSpec((None, None, tr, C), lambda d, l, i, c_ref: (d, HALF * c_ref[0] + l, i, 0)), blk],
            out_specs=blk),
        out_shape=jax.ShapeDtypeStruct((4, HALF, R, C), BF16),
        compiler_params=_params(("parallel", "parallel", "parallel")))(c, g, other)


def _exchange_chips(ps):
    na = len(ps)

    def body(*refs):
        srcs, outs, (send_sems, recv_sems) = refs[:na], refs[na:2 * na], refs[2 * na:]
        x, y, c, chips = _place()
        cps = [pltpu.make_async_remote_copy(
            src_ref=srcs[a].at[2 * px + py], dst_ref=outs[a].at[j], send_sem=send_sems.at[3 * a + j],
            recv_sem=recv_sems.at[3 * a + j], device_id=(px, py, c), device_id_type=MESH)
            for a in range(na) for j, (px, py) in enumerate(chips)]
        for cp in cps:
            cp.start()
        for cp in cps:
            cp.wait()

    return _comm_call(body, "exchange_chips", ps, [jax.ShapeDtypeStruct((3,) + p.shape[1:], p.dtype) for p in ps], 3 * na)


def _chip_sum(p, got, me):
    _, _, R, C = p.shape
    tr = _row_tile(R, 4 * C * 4)

    def body(me_ref, p_ref, g_ref, o_ref):
        acc = p_ref[...].astype(F32)
        for k in range(3):
            acc = acc + g_ref[k].astype(F32)
        o_ref[...] = acc

    return pl.pallas_call(
        body, name="chip_sum",
        grid_spec=pltpu.PrefetchScalarGridSpec(
            num_scalar_prefetch=1, grid=(HALF, R // tr),
            in_specs=[pl.BlockSpec((None, None, tr, C), lambda l, i, me_ref: (me_ref[0], l, i, 0)),
                      pl.BlockSpec((3, None, tr, C), lambda l, i, me_ref: (0, l, i, 0))],
            out_specs=pl.BlockSpec((None, tr, C), lambda l, i, me_ref: (l, i, 0))),
        out_shape=jax.ShapeDtypeStruct((HALF, R, C), F32), compiler_params=_params(("parallel", "parallel")))(me, p, got)


def _sum_parts(t, name):
    n, H, W = t.shape
    th = _row_tile(H, (n + 1) * W * 4)

    def body(t_ref, o_ref):
        acc = t_ref[0]
        for k in range(1, n):
            acc = acc + t_ref[k]
        o_ref[...] = acc

    return pl.pallas_call(
        body, name=name, grid=(H // th,), in_specs=[pl.BlockSpec((n, th, W), lambda i: (0, i, 0))],
        out_specs=pl.BlockSpec((th, W), lambda i: (i, 0)), out_shape=jax.ShapeDtypeStruct((H, W), F32),
        compiler_params=_params(("parallel",)))(t)


def _share_with_sibling(halves):
    na = len(halves)

    def body(*refs):
        srcs, outs, (send_sems, recv_sems) = refs[:na], refs[na:2 * na], refs[2 * na:]
        x, y, c, _ = _place()

        def copy(a, layers):
            return pltpu.make_async_remote_copy(
                src_ref=srcs[a], dst_ref=outs[a].at[layers], send_sem=send_sems.at[a], recv_sem=recv_sems.at[a],
                device_id=(x, y, 1 - c), device_id_type=MESH)

        sent = [copy(a, pl.ds(HALF * c, HALF)) for a in range(na)]
        for cp in sent:
            cp.start()
        for a in range(na):
            copy(a, pl.ds(HALF * (1 - c), HALF)).wait_recv()
        for cp in sent:
            cp.wait_send()

    return _comm_call(body, "share_with_sibling", halves,
                      [jax.ShapeDtypeStruct((DEPTH,) + h.shape[1:], h.dtype) for h in halves], na)


def _gather_all(v):
    n, W = v.shape

    def body(src, out, send_sems, recv_sems, own_sem):
        x, y, c, _ = _place()
        own = pltpu.make_async_copy(src, out.at[4 * x + 2 * y + c], own_sem)
        own.start()
        flips = [(fx, fy, fc) for fx in (0, 1) for fy in (0, 1) for fc in (0, 1)][1:]
        sent = []
        for k, (fx, fy, fc) in enumerate(flips):
            cp = pltpu.make_async_remote_copy(
                src_ref=src, dst_ref=out.at[4 * x + 2 * y + c], send_sem=send_sems.at[k], recv_sem=recv_sems.at[k],
                device_id=(x ^ fx, y ^ fy, c ^ fc), device_id_type=MESH)
            cp.start()
            sent.append(cp)
        for k, (fx, fy, fc) in enumerate(flips):
            pltpu.make_async_remote_copy(
                src_ref=src, dst_ref=out.at[4 * (x ^ fx) + 2 * (y ^ fy) + (c ^ fc)], send_sem=send_sems.at[k],
                recv_sem=recv_sems.at[k], device_id=(x ^ fx, y ^ fy, c ^ fc), device_id_type=MESH).wait_recv()
        for cp in sent:
            cp.wait_send()
        own.wait()

    return pl.pallas_call(
        body, name="gather_all", in_specs=[HBM_SPEC], out_specs=HBM_SPEC,
        out_shape=jax.ShapeDtypeStruct((8, n, W), v.dtype),
        scratch_shapes=[pltpu.SemaphoreType.DMA((7,)), pltpu.SemaphoreType.DMA((7,)), pltpu.SemaphoreType.DMA(())],
        compiler_params=pltpu.CompilerParams(has_side_effects=True))(v)


def _adamw(w, g, m, v):
    shape = w.shape
    cols = shape[-1]
    w2, g2, m2, v2 = (a.reshape(-1, cols) for a in (w, g, m, v))
    rows = w2.shape[0]
    tr = next((t for t in (1024, 512, 256, 128, 64, 32, 16, 8) if rows % t == 0 and t * cols * 4 <= (2 << 20)), rows)

    def body(w_ref, g_ref, m_ref, v_ref, d_ref, nm_ref, nv_ref):
        g_ = g_ref[...]
        nm = ADAM_B1 * m_ref[...] + (1.0 - ADAM_B1) * g_
        nv = ADAM_B2 * v_ref[...] + (1.0 - ADAM_B2) * (g_ * g_)
        m_hat = nm / (1.0 - ADAM_B1 ** ADAM_STEP)
        v_hat = nv / (1.0 - ADAM_B2 ** ADAM_STEP)
        d_ref[...] = -ADAM_LR * (m_hat / (jnp.sqrt(v_hat) + ADAM_EPS) + ADAM_WD * w_ref[...])
        nm_ref[...] = nm
        nv_ref[...] = nv

    blk = pl.BlockSpec((tr, cols), lambda i: (i, 0))
    outs = pl.pallas_call(
        body, name="adamw", grid=(rows // tr,), in_specs=[blk] * 4, out_specs=[blk] * 3,
        out_shape=[jax.ShapeDtypeStruct((rows, cols), F32)] * 3, compiler_params=_params(("parallel",)))(w2, g2, m2, v2)
    return tuple(o.reshape(shape) for o in outs)


BY_COLUMNS = ("w_in", "w_uq", "w_ukv")


def _chip_part(name, a, k):
    if name == "w_in":
        n = D_IN // 4
        return _model_cols(a, k * n, (k + 1) * n)
    n = a.shape[1 if name in BY_COLUMNS else 0] // 4
    return a[:, k * n:(k + 1) * n] if name in BY_COLUMNS else a[k * n:(k + 1) * n]


def kernel(x, mem, positions, w_in, q_norm_g, w_uq, kv_norm_g, w_ukv, sg_ln_g, sg_ln_b, sg_w, sg_b, w_mem_k, w_mem_v, w_out, ln_g, ln_b, loss_target, m_w_in, m_q_norm_g, m_w_uq, m_kv_norm_g, m_w_ukv, m_sg_ln_g, m_sg_ln_b, m_sg_w, m_sg_b, m_w_mem_k, m_w_mem_v, m_w_out, m_ln_g, m_ln_b, v_w_in, v_q_norm_g, v_w_uq, v_kv_norm_g, v_w_ukv, v_sg_ln_g, v_sg_ln_b, v_sg_w, v_sg_b, v_w_mem_k, v_w_mem_v, v_w_out, v_ln_g, v_ln_b):
    weights = dict(w_in=w_in, q_norm_g=q_norm_g, w_uq=w_uq, kv_norm_g=kv_norm_g, w_ukv=w_ukv, sg_ln_g=sg_ln_g,
                   sg_ln_b=sg_ln_b, sg_w=sg_w, sg_b=sg_b, w_mem_k=w_mem_k, w_mem_v=w_mem_v, w_out=w_out, ln_g=ln_g, ln_b=ln_b)
    mom_m = dict(w_in=m_w_in, q_norm_g=m_q_norm_g, w_uq=m_w_uq, kv_norm_g=m_kv_norm_g, w_ukv=m_w_ukv, sg_ln_g=m_sg_ln_g,
                 sg_ln_b=m_sg_ln_b, sg_w=m_sg_w, sg_b=m_sg_b, w_mem_k=m_w_mem_k, w_mem_v=m_w_mem_v, w_out=m_w_out,
                 ln_g=m_ln_g, ln_b=m_ln_b)
    mom_v = dict(w_in=v_w_in, q_norm_g=v_q_norm_g, w_uq=v_w_uq, kv_norm_g=v_kv_norm_g, w_ukv=v_w_ukv, sg_ln_g=v_sg_ln_g,
                 sg_ln_b=v_sg_ln_b, sg_w=v_sg_w, sg_b=v_sg_b, w_mem_k=v_w_mem_k, w_mem_v=v_w_mem_v, w_out=v_w_out,
                 ln_g=v_ln_g, ln_b=v_ln_b)
    c_idx = lax.axis_index("c").astype(jnp.int32).reshape(1)

    me = 2 * lax.axis_index("x") + lax.axis_index("y")
    shards = [[weights[n][l].astype(BF16) for n in SHARDED] for l in range(DEPTH)]

    def layer_source(l, fetched):
        if l == 0:
            fetched = _gather_weights(shards[0])
        return _layer_weights([lax.dynamic_update_slice(g, s[None], (me, 0, 0)) for g, s in zip(fetched, shards[l])])

    loss_dev, grad_x, grads = _local_step(x[0], mem[0], positions[0], loss_target[0], {n: weights[n] for n in SMALL},
                                          layer_source, lambda l: shards[l + 1] if l + 1 < DEPTH else None)

    gs = [jnp.stack([jnp.stack([_chip_part(n, g, k) for g in grads[n]]) for k in range(4)]) for n in SHARDED]
    pairs = [_pair_sum(g, o, c_idx) for g, o in zip(gs, _swap_halves(gs))]
    me1 = me.astype(jnp.int32).reshape(1)
    halves = [_chip_sum(p, o, me1) for p, o in zip(pairs, _exchange_chips(pairs))]
    grad_out = {n: lax.dynamic_update_slice(r, h, (HALF * c_idx[0], 0, 0))
                for n, r, h in zip(SHARDED, _share_with_sibling(halves), halves)}

    small_sizes = [weights[n].size for n in SMALL]
    vec = jnp.concatenate([g.reshape(-1) for n in SMALL for g in grads[n]] + [loss_dev[0]])
    n_small = vec.shape[0]
    rows_small = -(-n_small // (8 * FLAT_W)) * 8
    vec = jnp.pad(vec, (0, rows_small * FLAT_W - n_small)).reshape(rows_small, FLAT_W)
    total = _sum_parts(_gather_all(vec), "device_sum").reshape(-1)
    at = 0
    for n, size in zip(SMALL, small_sizes):
        grad_out[n] = total[at:at + size].reshape(weights[n].shape)
        at += size
    loss = total[at]

    names = list(weights)
    upd = {n: _adamw(weights[n], grad_out[n], mom_m[n], mom_v[n]) for n in names}
    return (loss, grad_x[None], *[grad_out[n] for n in names], *[upd[n][0] for n in names],
            *[upd[n][1] for n in names], *[upd[n][2] for n in names])
```

```python
import math

import jax
import jax.numpy as jnp
from jax import lax
from jax.experimental import pallas as pl
from jax.experimental.pallas import tpu as pltpu

F32, BF16 = jnp.float32, jnp.bfloat16

D_MODEL = 2048
DEPTH = 4
CHUNK = 64
MLA_HEADS = 6
MLA_SCALE = 1.0 / math.sqrt(192.0)
SB_HEADS = 4
SB_SCALE = 1.0 / math.sqrt(128.0)
MEM_HEADS = 4
MEM_SCALE = 1.0 / math.sqrt(64.0)
ROPE_THETA = 10000.0
ALPHA = (2.0 * DEPTH) ** 0.25
LN_EPS = 1e-5
RMS_EPS = 1e-6
ADAM_LR, ADAM_B1, ADAM_B2, ADAM_EPS, ADAM_WD, ADAM_STEP = 0.001, 0.9, 0.999, 1e-08, 0.01, 10

ORIG = dict(c_q=(0, 512), c_kv=(512, 256), k_pe=(768, 64), g_a=(832, 768), sg_u=(1600, 512), sg_v=(2112, 512),
            g_b=(2624, 512), sb_q=(3136, 512), sb_k=(3648, 512), sb_v=(4160, 512), g_c=(4672, 512),
            m_q=(5184, 256), g_m=(5440, 256))
D_IN = 5696
PERM_ORDER = ("c_q", "c_kv", "m_q", "sg_u", "sg_v", "sb_q", "sb_k", "sb_v", "g_a", "g_b", "g_c", "g_m", "k_pe")
HP = 5760
CQ, CKV, MQ, SGU, SGV, SBQ, GATE, KPE = 0, 512, 768, 1024, 1536, 2048, 3584, 5632

Q_BLK = 2048
K_BLK = 512
MLA_FWD_K_BLK = 1024
SB_Q_BLK = 512
SB_K_BLK = 256
SB_DEAD = -110.0
LANE = 128
VMEM_LIMIT = 56 * 1024 * 1024

FLAT_W = 1024
SHARDED = ("w_in", "w_uq", "w_ukv", "w_mem_k", "w_mem_v", "w_out")
SMALL = ("q_norm_g", "kv_norm_g", "sg_ln_g", "sg_ln_b", "sg_w", "sg_b", "ln_g", "ln_b")


def _params(sem=None):
    return pltpu.CompilerParams(dimension_semantics=sem, vmem_limit_bytes=VMEM_LIMIT)


def _tile(dim, pref):
    if dim <= pref:
        return dim
    t = (pref // LANE) * LANE
    while t >= LANE:
        if dim % t == 0:
            return t
        t -= LANE
    return dim


def _row_tile(rows, bytes_per_row, budget=8 << 20):
    best = None
    for t in range(8, rows + 1, 8):
        if rows % t == 0 and t * bytes_per_row <= budget:
            best = t
    return best if best else rows


def _dot_nt(a, b):
    return lax.dot_general(a, b, (((1,), (1,)), ((), ())), preferred_element_type=F32)


def _dot_tn(a, b):
    return lax.dot_general(a, b, (((0,), (0,)), ((), ())), preferred_element_type=F32)


def _dot(a, b):
    return jnp.dot(a, b, preferred_element_type=F32)


def _mm(a, b, *, ta=False, tb=False, a_win=None, b_win=None, add=None, add_scale=1.0, out_dtype=F32,
        tm=512, tn=512, tk=512, name="mm"):
    a_off, a_w = a_win if a_win else (0, a.shape[1])
    b_off, b_w = b_win if b_win else (0, b.shape[1])
    (K, M) = (a.shape[0], a_w) if ta else (a_w, a.shape[0])
    (N, Kb) = (b.shape[0], b_w) if tb else (b_w, b.shape[0])
    assert K == Kb, (a.shape, b.shape, ta, tb)
    tm, tn, tk = _tile(M, tm), _tile(N, tn), _tile(K, tk)
    nk = K // tk
    if ta:
        assert a_off % tm == 0
        a_spec = pl.BlockSpec((tk, tm), lambda i, j, k: (k, i + a_off // tm))
    else:
        assert a_off % tk == 0
        a_spec = pl.BlockSpec((tm, tk), lambda i, j, k: (i, k + a_off // tk))
    if tb:
        assert b_off % tk == 0
        b_spec = pl.BlockSpec((tn, tk), lambda i, j, k: (j, k + b_off // tk))
    else:
        assert b_off % tn == 0
        b_spec = pl.BlockSpec((tk, tn), lambda i, j, k: (k, j + b_off // tn))
    o_spec = pl.BlockSpec((tm, tn), lambda i, j, k: (i, j))
    dn = (((0 if ta else 1,), (1 if tb else 0,)), ((), ()))
    has_add = add is not None

    def body(*refs):
        a_ref, b_ref = refs[:2]
        add_ref = refs[2] if has_add else None
        o_ref = refs[3 if has_add else 2]
        part = lax.dot_general(a_ref[...].astype(BF16), b_ref[...].astype(BF16), dn, preferred_element_type=F32)

        def finish(r):
            if has_add:
                r = r + add_scale * add_ref[...]
            o_ref[...] = r.astype(o_ref.dtype)

        if nk == 1:
            finish(part)
            return
        acc_ref = refs[-1]
        k = pl.program_id(2)

        @pl.when(k == 0)
        def _():
            acc_ref[...] = part

        @pl.when(k > 0)
        def _():
            acc_ref[...] += part

        @pl.when(k == nk - 1)
        def _():
            finish(acc_ref[...])

    ins = [a, b] + ([add] if has_add else [])
    specs = [a_spec, b_spec] + ([o_spec] if has_add else [])
    return pl.pallas_call(
        body, name=name, grid=(M // tm, N // tn, nk), in_specs=specs, out_specs=o_spec,
        out_shape=jax.ShapeDtypeStruct((M, N), out_dtype),
        scratch_shapes=[pltpu.VMEM((tm, tn), F32)] if nk > 1 else [],
        compiler_params=_params(("parallel", "parallel", "arbitrary")))(*ins)


GELU_K = math.sqrt(2.0 / math.pi)


def _gelu(x):
    t = jnp.tanh(GELU_K * (x + 0.044715 * (x * x * x)))
    return 0.5 * x * (1.0 + t)


def _gelu_grad(x):
    t = jnp.tanh(GELU_K * (x + 0.044715 * (x * x * x)))
    return 0.5 * (1.0 + t) + 0.5 * x * (1.0 - t * t) * GELU_K * (1.0 + 3.0 * 0.044715 * x * x)


def _rope_swap(t):
    lane = lax.broadcasted_iota(jnp.int32, t.shape, 1)
    return jnp.where(lane < 32, pltpu.roll(t, 96, axis=1), pltpu.roll(t, 32, axis=1))


def _rope(t, c, s):
    return t * c + _rope_swap(t) * s


def _rope_bwd(dt, c, s):
    return dt * c - _rope_swap(dt) * s


def _row_spec(tm, w, cb=0):
    return pl.BlockSpec((tm, w), lambda i: (i, cb))


def _fix_spec(shape):
    return pl.BlockSpec(shape, lambda *_: (0,) * len(shape))


def _rms_fwd(h, off, width, g, name):
    S = h.shape[0]
    tm = _tile(S, 2048)

    def body(x_ref, g_ref, o_ref):
        x = x_ref[...]
        r = lax.rsqrt(jnp.mean(x * x, axis=1, keepdims=True) + RMS_EPS)
        o_ref[...] = (x * r * g_ref[...]).astype(BF16)

    return pl.pallas_call(
        body, name=name, grid=(S // tm,), in_specs=[_row_spec(tm, width, off // width), _fix_spec((1, width))],
        out_specs=_row_spec(tm, width), out_shape=jax.ShapeDtypeStruct((S, width), BF16),
        compiler_params=_params(("parallel",)))(h, g.reshape(1, width))


def _rms_bwd(h, off, width, g, dxn, name):
    S = h.shape[0]
    tm = _tile(S, 2048)

    def body(x_ref, g_ref, d_ref, dx_ref, dg_ref):
        @pl.when(pl.program_id(0) == 0)
        def _():
            dg_ref[...] = jnp.zeros_like(dg_ref)

        x, d = x_ref[...], d_ref[...]
        r = lax.rsqrt(jnp.mean(x * x, axis=1, keepdims=True) + RMS_EPS)
        gd = d * g_ref[...]
        dx_ref[...] = gd * r - x * (r * r * r) * jnp.mean(gd * x, axis=1, keepdims=True)
        dg_ref[...] += jnp.sum(d * x * r, axis=0, keepdims=True)

    return pl.pallas_call(
        body, name=name, grid=(S // tm,),
        in_specs=[_row_spec(tm, width, off // width), _fix_spec((1, width)), _row_spec(tm, width)],
        out_specs=[_row_spec(tm, width), _fix_spec((1, width))],
        out_shape=[jax.ShapeDtypeStruct((S, width), F32), jax.ShapeDtypeStruct((1, width), F32)],
        compiler_params=_params(("arbitrary",)))(h, g.reshape(1, width), dxn)


def _q_proj(xn, w, rc, rs):
    S = xn.shape[0]
    tm = _tile(S, 2048)

    def body(x_ref, w_ref, c_ref, s_ref, q_ref):
        q = _dot(x_ref[...], w_ref[...]) * MLA_SCALE
        q_ref[:, :LANE] = q[:, :LANE].astype(BF16)
        q_ref[:, LANE:] = _rope(q[:, LANE:], c_ref[...], s_ref[...]).astype(BF16)

    return pl.pallas_call(
        body, name="q_proj", grid=(S // tm, MLA_HEADS),
        in_specs=[pl.BlockSpec((tm, 512), lambda i, j: (i, 0)), pl.BlockSpec((512, 256), lambda i, j: (0, j)),
                  pl.BlockSpec((tm, LANE), lambda i, j: (i, 0)), pl.BlockSpec((tm, LANE), lambda i, j: (i, 0))],
        out_specs=pl.BlockSpec((tm, 256), lambda i, j: (i, j)),
        out_shape=jax.ShapeDtypeStruct((S, MLA_HEADS * 256), BF16),
        compiler_params=_params(("parallel", "parallel")))(xn, w, rc, rs)


def _kv_proj(xn, w, h, rc, rs):
    S = xn.shape[0]
    tm = _tile(S, 2048)

    def body(x_ref, w_ref, pe_ref, c_ref, s_ref, k_ref, v_ref):
        kv = _dot(x_ref[...], w_ref[...])
        k_ref[:, :LANE] = kv[:, :LANE].astype(BF16)
        k_ref[:, LANE:] = _rope(pe_ref[...], c_ref[...], s_ref[...]).astype(BF16)
        v_ref[...] = kv[:, LANE:].astype(BF16)

    return pl.pallas_call(
        body, name="kv_proj", grid=(S // tm, MLA_HEADS),
        in_specs=[pl.BlockSpec((tm, 256), lambda i, j: (i, 0)), pl.BlockSpec((256, 256), lambda i, j: (0, j)),
                  pl.BlockSpec((tm, LANE), lambda i, j: (i, KPE // LANE)),
                  pl.BlockSpec((tm, LANE), lambda i, j: (i, 0)), pl.BlockSpec((tm, LANE), lambda i, j: (i, 0))],
        out_specs=[pl.BlockSpec((tm, 256), lambda i, j: (i, j)), pl.BlockSpec((tm, LANE), lambda i, j: (i, j))],
        out_shape=[jax.ShapeDtypeStruct((S, MLA_HEADS * 256), BF16), jax.ShapeDtypeStruct((S, MLA_HEADS * LANE), BF16)],
        compiler_params=_params(("parallel", "parallel")))(xn, w, h, rc, rs)


def _kv_bwd_prep(dk, dv, rc, rs):
    S = dk.shape[1]
    tm = _tile(S, 1024)

    def body(dk_ref, dv_ref, c_ref, s_ref, o_ref, pe_ref):
        rot = jnp.zeros((tm, LANE), F32)
        for hh in range(MLA_HEADS):
            o_ref[:, hh * 256:hh * 256 + LANE] = dk_ref[hh, :, :LANE].astype(BF16)
            o_ref[:, hh * 256 + LANE:(hh + 1) * 256] = dv_ref[hh].astype(BF16)
            rot = rot + dk_ref[hh, :, LANE:]
        pe_ref[...] = _rope_bwd(rot, c_ref[...], s_ref[...])

    return pl.pallas_call(
        body, name="kv_bwd_prep", grid=(S // tm,),
        in_specs=[pl.BlockSpec((MLA_HEADS, tm, 256), lambda i: (0, i, 0)),
                  pl.BlockSpec((MLA_HEADS, tm, LANE), lambda i: (0, i, 0)), _row_spec(tm, LANE), _row_spec(tm, LANE)],
        out_specs=[_row_spec(tm, MLA_HEADS * 256), _row_spec(tm, LANE)],
        out_shape=[jax.ShapeDtypeStruct((S, MLA_HEADS * 256), BF16), jax.ShapeDtypeStruct((S, LANE), F32)],
        compiler_params=_params(("parallel",)))(dk, dv, rc, rs)


def _chunk_mask(T):
    row = lax.broadcasted_iota(jnp.int32, (T, T), 0)
    col = lax.broadcasted_iota(jnp.int32, (T, T), 1)
    return (col // CHUNK) <= (row // CHUNK)


def _att_blocks(S, q_blk=None, k_blk=None):
    tq = min(q_blk or Q_BLK, S)
    tk = min(k_blk or K_BLK, tq)
    return tq, tk, tq // tk


def _tail_masks(rows, tk):
    row = lax.broadcasted_iota(jnp.int32, (rows, tk), 0)
    col = lax.broadcasted_iota(jnp.int32, (rows, tk), 1)
    return (col // CHUNK) <= (row // CHUNK), col < row


def _span_masks(tk, r):
    row = lax.broadcasted_iota(jnp.int32, (tk, (r + 1) * tk), 0) + r * tk
    col = lax.broadcasted_iota(jnp.int32, (tk, (r + 1) * tk), 1)
    return (col // CHUNK) <= (row // CHUNK), col < row


def _put_rows(old, new, r0):
    return new if r0 == 0 else jnp.concatenate([old[:r0], new], axis=0)


def _mla_fwd(q, kp, v, nxt=None):
    S = q.shape[0]
    TQ, TK, n = _att_blocks(S, None, MLA_FWD_K_BLK)
    nq = S // TQ
    na = len(nxt) if nxt else 0

    def ride_along(srcs, outs, send_sems, recv_sems):
        x, y, c, chips = _place()

        def copy(a, j, px, py, chip):
            return pltpu.make_async_remote_copy(
                src_ref=srcs[a], dst_ref=outs[a].at[chip], send_sem=send_sems.at[3 * a + j],
                recv_sem=recv_sems.at[3 * a + j], device_id=(px, py, c), device_id_type=MESH)

        first = jnp.logical_and(pl.program_id(0) == 0, pl.program_id(1) == 0)
        last = jnp.logical_and(pl.program_id(0) == MLA_HEADS - 1, pl.program_id(1) == nq - 1)

        @pl.when(first)
        def _():
            for a in range(na):
                for j, (px, py) in enumerate(chips):
                    copy(a, j, px, py, 2 * x + y).start()

        @pl.when(last)
        def _():
            for a in range(na):
                for j, (px, py) in enumerate(chips):
                    copy(a, j, px, py, 2 * x + y).wait_send()
                    copy(a, j, px, py, 2 * px + py).wait_recv()

    def body(q_ref, k_ref, v_ref, *rest):
        if na:
            o_ref, lse_ref = rest[na:na + 2]
            ride_along(rest[:na], rest[na + 2:2 * na + 2], *rest[2 * na + 2:])
        else:
            o_ref, lse_ref = rest
        i = pl.program_id(1)

        def update(carry, qb, keys, mask):
            m, l, acc = carry
            s = _dot_nt(qb, k_ref[keys, :])
            if mask is not None:
                s = jnp.where(mask, s, -1e30)
            m_new = jnp.maximum(m, jnp.max(s, axis=1, keepdims=True))
            a = jnp.exp(m - m_new)
            p = jnp.exp(s - m_new)
            return m_new, a * l + jnp.sum(p, axis=1, keepdims=True), a * acc + _dot(p.astype(BF16), v_ref[keys, :])

        carry = (jnp.full((TQ, 1), -1e30, F32), jnp.zeros((TQ, 1), F32), jnp.zeros((TQ, LANE), F32))
        carry = lax.fori_loop(
            0, i * n, lambda j, c: update(c, q_ref[...], pl.ds(pl.multiple_of(j * TK, TK), TK), None), carry)
        for r in range(n):
            rows = slice(r * TK, (r + 1) * TK)
            m, l, acc = update(tuple(c[rows] for c in carry), q_ref[rows, :],
                               pl.ds(pl.multiple_of(i * TQ, TQ), (r + 1) * TK), _span_masks(TK, r)[0])
            o_ref[rows, :] = acc / l
            lse_ref[rows, :] = jnp.broadcast_to(m + jnp.log(l), (TK, LANE))

    res = pl.pallas_call(
        body, name="mla_fwd_gather" if na else "mla_fwd", grid=(MLA_HEADS, nq),
        in_specs=[pl.BlockSpec((TQ, 256), lambda h, i: (i, h)),
                  pl.BlockSpec((S, 256), lambda h, i: (0, h), pipeline_mode=pl.Buffered(1)),
                  pl.BlockSpec((S, LANE), lambda h, i: (0, h), pipeline_mode=pl.Buffered(1))] + [HBM_SPEC] * na,
        out_specs=[pl.BlockSpec((TQ, LANE), lambda h, i: (i, h)), pl.BlockSpec((TQ, LANE), lambda h, i: (i, h))]
                  + [HBM_SPEC] * na,
        out_shape=[jax.ShapeDtypeStruct((S, MLA_HEADS * LANE), F32), jax.ShapeDtypeStruct((S, MLA_HEADS * LANE), F32)]
                  + [jax.ShapeDtypeStruct((4,) + s.shape, s.dtype) for s in (nxt or [])],
        scratch_shapes=[pltpu.SemaphoreType.DMA((3 * na,)), pltpu.SemaphoreType.DMA((3 * na,))] if na else [],
        compiler_params=pltpu.CompilerParams(dimension_semantics=("arbitrary", "arbitrary"), vmem_limit_bytes=VMEM_LIMIT,
                                             has_side_effects=bool(na)))(q, kp, v, *(nxt or []))
    return res[0], res[1], (list(res[2:]) if na else None)


def _mla_bwd(q, kp, v, do_cat, o_cat, lse, rc, rs, ride=None):
    S = q.shape[0]
    TQ, TK, n = _att_blocks(S)
    nq = S // TQ
    na = len(ride) if ride else 0

    def body(q_ref, k_ref, v_ref, do_ref, o_ref, lse_ref, c_ref, s_ref, *rest):
        dq_ref, dk_hbm, dv_hbm = rest[na:na + 3]
        dk_acc, dv_acc = rest[2 * na + 3:2 * na + 5]
        h, i = pl.program_id(0), pl.program_id(1)
        if na:
            @pl.when(jnp.logical_and(h == 0, i == 0))
            def _():
                for cp in _exchange_copies(rest[:na], rest[na + 3:2 * na + 3], *rest[2 * na + 5:]):
                    cp.start()

            @pl.when(jnp.logical_and(h == MLA_HEADS - 1, i == nq - 1))
            def _():
                for cp in _exchange_copies(rest[:na], rest[na + 3:2 * na + 3], *rest[2 * na + 5:]):
                    cp.wait()

        @pl.when(i == 0)
        def _():
            dk_acc[...] = jnp.zeros_like(dk_acc)
            dv_acc[...] = jnp.zeros_like(dv_acc)

        do32 = do_ref[...]
        dob = do32.astype(BF16)
        delta = jnp.sum(do32 * o_ref[...], axis=1, keepdims=True)
        lse_col = lse_ref[:, :1]

        def blk(j, dq, r0, masked):
            sl = pl.ds(pl.multiple_of(j * TK, TK), TK)
            kb, vb, qb = k_ref[sl, :], v_ref[sl, :], q_ref[r0:, :]
            s = _dot_nt(qb, kb)
            if masked:
                s = jnp.where(_tail_masks(TQ - r0, TK)[0], s, -1e30)
            p = jnp.exp(s - lse_col[r0:])
            ds = (p * (_dot_nt(dob[r0:], vb) - delta[r0:])).astype(BF16)
            dk_acc[sl, :] += _dot_tn(ds, qb)
            dv_acc[sl, :] += _dot_tn(p.astype(BF16), dob[r0:])
            return _put_rows(dq, dq[r0:] + _dot(ds, kb), r0)

        dq = lax.fori_loop(0, i * n, lambda j, c: blk(j, c, 0, False), jnp.zeros((TQ, 256), F32))
        for t in range(n):
            dq = blk(i * n + t, dq, t * TK, True)
        dq_ref[:, :LANE] = (dq[:, :LANE] * MLA_SCALE).astype(BF16)
        dq_ref[:, LANE:] = _rope_bwd(dq[:, LANE:] * MLA_SCALE, c_ref[...], s_ref[...]).astype(BF16)

        @pl.when(i == nq - 1)
        def _():
            pltpu.sync_copy(dk_acc, dk_hbm.at[h])
            pltpu.sync_copy(dv_acc, dv_hbm.at[h])

    any_spec = pl.BlockSpec(memory_space=pl.ANY)
    T = TQ
    rows = pl.BlockSpec((T, LANE), lambda h, i: (i, 0))
    res = pl.pallas_call(
        body, name="mla_bwd_exchange" if na else "mla_bwd", grid=(MLA_HEADS, nq),
        in_specs=[pl.BlockSpec((T, 256), lambda h, i: (i, h)),
                  pl.BlockSpec((S, 256), lambda h, i: (0, h), pipeline_mode=pl.Buffered(1)),
                  pl.BlockSpec((S, LANE), lambda h, i: (0, h), pipeline_mode=pl.Buffered(1)),
                  pl.BlockSpec((T, LANE), lambda h, i: (i, h)),
                  pl.BlockSpec((T, LANE), lambda h, i: (i, h)), pl.BlockSpec((T, LANE), lambda h, i: (i, h)), rows, rows]
                 + [HBM_SPEC] * na,
        out_specs=[pl.BlockSpec((T, 256), lambda h, i: (i, h)), any_spec, any_spec] + [HBM_SPEC] * na,
        out_shape=[jax.ShapeDtypeStruct((S, MLA_HEADS * 256), BF16), jax.ShapeDtypeStruct((MLA_HEADS, S, 256), F32),
                   jax.ShapeDtypeStruct((MLA_HEADS, S, LANE), F32)]
                  + [jax.ShapeDtypeStruct((3,) + p.shape[1:], p.dtype) for p in (ride or [])],
        scratch_shapes=[pltpu.VMEM((S, 256), F32), pltpu.VMEM((S, LANE), F32)]
                       + ([pltpu.SemaphoreType.DMA((3 * na,)), pltpu.SemaphoreType.DMA((3 * na,))] if na else []),
        compiler_params=pltpu.CompilerParams(dimension_semantics=("arbitrary", "arbitrary"), vmem_limit_bytes=VMEM_LIMIT,
                                             has_side_effects=bool(na)))(q, kp, v, do_cat, o_cat, lse, rc, rs, *(ride or []))
    return res[0], res[1], res[2], (list(res[3:]) if na else None)


def _split_dot(x, tri):
    top = lax.bitcast_convert_type(lax.bitcast_convert_type(x, jnp.uint32) & jnp.uint32(0xFFFF0000), F32)
    return _dot(top.astype(BF16), tri) + _dot((x - top).astype(BF16), tri)


def _sb_block(qb, kb, tri, carry, masked):
    z = _dot_nt(qb, kb)
    lb = jnp.minimum(z, 0.0) - jnp.log(1.0 + jnp.exp(-jnp.abs(z)))
    lm = lb - z
    strict = None
    if masked:
        strict = _tail_masks(z.shape[0], z.shape[1])[1]
        lm = jnp.where(strict, lm, 0.0)
    a = jnp.exp(lb + carry + _split_dot(lm, tri))
    if masked:
        a = jnp.where(strict, a, 0.0)
    return a, lb, lm, strict


def _sb_walk(blk, j0, state):
    def alive(c):
        return jnp.logical_and(c[0] >= 0, jnp.max(c[1][0]) > SB_DEAD)

    return lax.while_loop(alive, lambda c: (c[0] - 1, blk(c[0], c[1], 0, False)), (j0, state))[1]


def _triangle(tk):
    row = lax.broadcasted_iota(jnp.int32, (tk, tk), 0)
    col = lax.broadcasted_iota(jnp.int32, (tk, tk), 1)
    return (row > col).astype(BF16)


def _sb_fwd(qkv):
    S = qkv.shape[0]
    TQ, TK, n = _att_blocks(S, SB_Q_BLK, SB_K_BLK)
    T = TQ

    def body(q_ref, k_ref, v_ref, o_ref):
        i = pl.program_id(1)
        tri = _triangle(TK)

        def blk(j, state, r0, masked):
            carry, acc = (c[r0:] for c in state)
            sl = pl.ds(pl.multiple_of(j * TK, TK), TK)
            a, _, lm, _ = _sb_block(q_ref[r0:, :], k_ref[sl, :], tri, carry, masked)
            new = (carry + jnp.sum(lm, axis=1, keepdims=True), acc + _dot(a.astype(BF16), v_ref[sl, :]))
            return tuple(_put_rows(c, u, r0) for c, u in zip(state, new))

        state = (jnp.zeros((TQ, 1), F32), jnp.zeros((TQ, LANE), F32))
        for t in reversed(range(n)):
            state = blk(i * n + t, state, t * TK, True)
        state = _sb_walk(blk, i * n - 1, state)
        o_ref[...] = state[1]

    return pl.pallas_call(
        body, name="sb_fwd", grid=(SB_HEADS, S // T),
        in_specs=[pl.BlockSpec((T, LANE), lambda h, i: (i, h)), pl.BlockSpec((S, LANE), lambda h, i: (0, 4 + h)),
                  pl.BlockSpec((S, LANE), lambda h, i: (0, 8 + h))],
        out_specs=pl.BlockSpec((T, LANE), lambda h, i: (i, h)),
        out_shape=jax.ShapeDtypeStruct((S, SB_HEADS * LANE), F32),
        compiler_params=_params(("parallel", "arbitrary")))(qkv, qkv, qkv)


def _sb_bwd(qkv, do_cat, o_cat, col0):
    S = qkv.shape[0]
    TQ, TK, n = _att_blocks(S, SB_Q_BLK, SB_K_BLK)
    T = TQ
    nq = S // TQ

    def body(q_ref, k_ref, v_ref, do_ref, o_ref, dq_ref, dk_hbm, dv_hbm, dk_acc, dv_acc):
        h, i = pl.program_id(0), pl.program_id(1)

        @pl.when(i == 0)
        def _():
            dk_acc[...] = jnp.zeros_like(dk_acc)
            dv_acc[...] = jnp.zeros_like(dv_acc)

        dob = do_ref[...].astype(BF16)
        tri = _triangle(TK)
        rest0 = jnp.sum(dob.astype(F32) * o_ref[...], axis=1, keepdims=True)

        def blk(j, state, r0, masked):
            carry, rest, dq = (c[r0:] for c in state)
            sl = pl.ds(pl.multiple_of(j * TK, TK), TK)
            kb, vb, qb = k_ref[sl, :], v_ref[sl, :], q_ref[r0:, :]
            a, lb, lm, strict = _sb_block(qb, kb, tri, carry, masked)
            ab = a.astype(BF16)
            e = ab.astype(F32) * _dot_nt(dob[r0:], vb)
            dz = e - jnp.exp(lb) * (rest - _split_dot(e, tri))
            if masked:
                dz = jnp.where(strict, dz, 0.0)
            dzb = dz.astype(BF16)
            dk_acc[sl, :] += _dot_tn(dzb, qb)
            dv_acc[sl, :] += _dot_tn(ab, dob[r0:])
            new = (carry + jnp.sum(lm, axis=1, keepdims=True), rest - jnp.sum(e, axis=1, keepdims=True),
                   dq + _dot(dzb, kb))
            return tuple(_put_rows(c, u, r0) for c, u in zip(state, new))

        state = (jnp.zeros((TQ, 1), F32), rest0, jnp.zeros((TQ, LANE), F32))
        for t in reversed(range(n)):
            state = blk(i * n + t, state, t * TK, True)
        state = _sb_walk(blk, i * n - 1, state)
        dq_ref[...] = state[2] * SB_SCALE

        @pl.when(i == nq - 1)
        def _():
            lanes = pl.ds(pl.multiple_of(h * LANE, LANE), LANE)
            pltpu.sync_copy(dk_acc, dk_hbm.at[:, lanes])
            pltpu.sync_copy(dv_acc, dv_hbm.at[:, lanes])

    any_spec = pl.BlockSpec(memory_space=pl.ANY)
    return pl.pallas_call(
        body, name="sb_bwd", grid=(SB_HEADS, nq),
        in_specs=[pl.BlockSpec((T, LANE), lambda h, i: (i, h)), pl.BlockSpec((S, LANE), lambda h, i: (0, 4 + h)),
                  pl.BlockSpec((S, LANE), lambda h, i: (0, 8 + h)),
                  pl.BlockSpec((T, LANE), lambda h, i: (i, col0 + h)), pl.BlockSpec((T, LANE), lambda h, i: (i, col0 + h))],
        out_specs=[pl.BlockSpec((T, LANE), lambda h, i: (i, h)), any_spec, any_spec],
        out_shape=[jax.ShapeDtypeStruct((S, SB_HEADS * LANE), F32)] * 3,
        scratch_shapes=[pltpu.VMEM((S, LANE), F32), pltpu.VMEM((S, LANE), F32)],
        compiler_params=_params(("arbitrary", "arbitrary")))(qkv, qkv, qkv, do_cat, o_cat)


def _mem_probs(q, k_ref, hh):
    lane = lax.broadcasted_iota(jnp.int32, (1, 256), 1) // 64
    msk = lane == hh
    qh = jnp.where(msk, q, 0.0).astype(BF16)
    s = _dot_nt(qh, k_ref[...]) * MEM_SCALE
    p = jnp.exp(s - jnp.max(s, axis=1, keepdims=True))
    return msk, qh, p / jnp.sum(p, axis=1, keepdims=True)


def _mem_fwd(h, mk, mv):
    S = h.shape[0]
    tm = _tile(S, 512)

    def body(q_ref, k_ref, v_ref, o_ref):
        q = q_ref[...]
        out = jnp.zeros((tm, 256), F32)
        for hh in range(MEM_HEADS):
            msk, _, p = _mem_probs(q, k_ref, hh)
            out = out + jnp.where(msk, _dot(p.astype(BF16), v_ref[...]), 0.0)
        o_ref[...] = out

    return pl.pallas_call(
        body, name="mem_fwd", grid=(S // tm,),
        in_specs=[_row_spec(tm, 256, MQ // 256), _fix_spec((256, 256)), _fix_spec((256, 256))],
        out_specs=_row_spec(tm, 256), out_shape=jax.ShapeDtypeStruct((S, 256), F32),
        compiler_params=_params(("parallel",)))(h, mk, mv)


def _mem_bwd(h, mk, mv, do_cat, col0):
    S = h.shape[0]
    tm = _tile(S, 512)

    def body(q_ref, k_ref, v_ref, do_ref, dq_ref, dk_ref, dv_ref):
        @pl.when(pl.program_id(0) == 0)
        def _():
            dk_ref[...] = jnp.zeros_like(dk_ref)
            dv_ref[...] = jnp.zeros_like(dv_ref)

        q, do = q_ref[...], do_ref[...]
        dq = jnp.zeros((tm, 256), F32)
        for hh in range(MEM_HEADS):
            msk, qh, p = _mem_probs(q, k_ref, hh)
            doh = jnp.where(msk, do, 0.0).astype(BF16)
            dp = _dot_nt(doh, v_ref[...])
            ds = (p * (dp - jnp.sum(p * dp, axis=1, keepdims=True)) * MEM_SCALE).astype(BF16)
            dq = dq + jnp.where(msk, _dot(ds, k_ref[...]), 0.0)
            dk_ref[...] += _dot_tn(ds, qh)
            dv_ref[...] += _dot_tn(p.astype(BF16), doh)
        dq_ref[...] = dq

    return pl.pallas_call(
        body, name="mem_bwd", grid=(S // tm,),
        in_specs=[_row_spec(tm, 256, MQ // 256), _fix_spec((256, 256)), _fix_spec((256, 256)),
                  _row_spec(tm, 256, col0 // 256)],
        out_specs=[_row_spec(tm, 256), _fix_spec((256, 256)), _fix_spec((256, 256))],
        out_shape=[jax.ShapeDtypeStruct((S, 256), F32), jax.ShapeDtypeStruct((256, 256), F32),
                   jax.ShapeDtypeStruct((256, 256), F32)],
        compiler_params=_params(("arbitrary",)))(h, mk, mv, do_cat)


SG_T = 128


def _sg_norm(sv, g, b):
    gv = _gelu(sv)
    xc = gv - jnp.mean(gv, axis=1, keepdims=True)
    rstd = lax.rsqrt(jnp.mean(xc * xc, axis=1, keepdims=True) + LN_EPS)
    xhat = xc * rstd
    return xhat, rstd, xhat * g + b


def _sg_fwd(h, lng, lnb, w, bias_t):
    S = h.shape[0]
    tm = _tile(S, 512)

    def body(u_ref, v_ref, g_ref, b_ref, w_ref, bias_ref, o_ref):
        mask = _chunk_mask(SG_T)
        for n in range(tm // SG_T):
            rows = slice(n * SG_T, (n + 1) * SG_T)
            u = _gelu(u_ref[rows, :])
            _, _, vn = _sg_norm(v_ref[rows, :], g_ref[...], b_ref[...])
            vb = vn.astype(BF16)
            for gi in range(4):
                cols = slice(gi * LANE, (gi + 1) * LANE)
                wg = jnp.where(mask, w_ref[gi], 0.0).astype(BF16)
                mixed = _dot(wg, vb[:, cols]) + bias_ref[:, gi:gi + 1]
                o_ref[rows, cols] = u[:, cols] * mixed

    return pl.pallas_call(
        body, name="sg_fwd", grid=(S // tm,),
        in_specs=[_row_spec(tm, 512, SGU // 512), _row_spec(tm, 512, SGV // 512), _fix_spec((1, 512)),
                  _fix_spec((1, 512)), _fix_spec((4, SG_T, SG_T)), _fix_spec((SG_T, 4))],
        out_specs=_row_spec(tm, 512), out_shape=jax.ShapeDtypeStruct((S, 512), F32),
        compiler_params=_params(("parallel",)))(h, h, lng.reshape(1, 512), lnb.reshape(1, 512), w, bias_t)


def _sg_bwd(h, lng, lnb, w, bias_t, do_cat, col0):
    S = h.shape[0]
    tm = _tile(S, 512)
    nsteps = S // tm

    def body(u_ref, v_ref, g_ref, b_ref, w_ref, bias_ref, do0_ref, do1_ref, do2_ref, do3_ref,
             du_ref, dv_ref, dw_ref, dbias_ref, dg_ref, db_ref, dvn_scr, dbias_acc):
        do_refs = (do0_ref, do1_ref, do2_ref, do3_ref)
        step = pl.program_id(0)

        @pl.when(step == 0)
        def _():
            dw_ref[...] = jnp.zeros_like(dw_ref)
            dg_ref[...] = jnp.zeros_like(dg_ref)
            db_ref[...] = jnp.zeros_like(db_ref)
            dbias_acc[...] = jnp.zeros_like(dbias_acc)

        mask = _chunk_mask(SG_T)
        for n in range(tm // SG_T):
            rows = slice(n * SG_T, (n + 1) * SG_T)
            su, sv = u_ref[rows, :], v_ref[rows, :]
            u = _gelu(su)
            xhat, rstd, vn = _sg_norm(sv, g_ref[...], b_ref[...])
            vb = vn.astype(BF16)
            ugrad = _gelu_grad(su)
            for gi in range(4):
                cols = slice(gi * LANE, (gi + 1) * LANE)
                do = do_refs[gi][rows, :]
                wg = jnp.where(mask, w_ref[gi], 0.0).astype(BF16)
                mixed = _dot(wg, vb[:, cols]) + bias_ref[:, gi:gi + 1]
                dmixed = do * u[:, cols]
                dmb = dmixed.astype(BF16)
                du_ref[rows, cols] = do * mixed * ugrad[:, cols]
                dvn_scr[:, cols] = _dot_tn(wg, dmb)
                dw_ref[gi] += jnp.where(mask, _dot_nt(dmb, vb[:, cols]), 0.0)
                dbias_acc[gi] += dmixed
            dvn = dvn_scr[...]
            dg_ref[...] += jnp.sum(dvn * xhat, axis=0, keepdims=True)
            db_ref[...] += jnp.sum(dvn, axis=0, keepdims=True)
            dxh = dvn * g_ref[...]
            dgv = rstd * (dxh - jnp.mean(dxh, axis=1, keepdims=True)
                          - xhat * jnp.mean(dxh * xhat, axis=1, keepdims=True))
            dv_ref[rows, :] = dgv * _gelu_grad(sv)

        @pl.when(step == nsteps - 1)
        def _():
            for gi in range(4):
                dbias_ref[:, gi:gi + 1] = jnp.sum(dbias_acc[gi], axis=1, keepdims=True)

    return pl.pallas_call(
        body, name="sg_bwd", grid=(nsteps,),
        in_specs=[_row_spec(tm, 512, SGU // 512), _row_spec(tm, 512, SGV // 512), _fix_spec((1, 512)),
                  _fix_spec((1, 512)), _fix_spec((4, SG_T, SG_T)), _fix_spec((SG_T, 4))]
                 + [_row_spec(tm, LANE, col0 // LANE + gi) for gi in range(4)],
        out_specs=[_row_spec(tm, 512), _row_spec(tm, 512), _fix_spec((4, SG_T, SG_T)), _fix_spec((SG_T, 4)),
                   _fix_spec((1, 512)), _fix_spec((1, 512))],
        out_shape=[jax.ShapeDtypeStruct((S, 512), F32), jax.ShapeDtypeStruct((S, 512), F32),
                   jax.ShapeDtypeStruct((4, SG_T, SG_T), F32), jax.ShapeDtypeStruct((SG_T, 4), F32),
                   jax.ShapeDtypeStruct((1, 512), F32), jax.ShapeDtypeStruct((1, 512), F32)],
        scratch_shapes=[pltpu.VMEM((SG_T, 512), F32), pltpu.VMEM((4, SG_T, SG_T), F32)],
        compiler_params=_params(("arbitrary",)))(h, h, lng.reshape(1, 512), lnb.reshape(1, 512), w, bias_t,
                                                 do_cat, do_cat, do_cat, do_cat)


def _gate_out_ln(branches, h, w_out, x, g, b):
    S = h.shape[0]
    tm = _tile(S, 256)
    widths = [a.shape[1] for a in branches]

    def body(oa_ref, ob_ref, oc_ref, om_ref, g0_ref, g1_ref, g2_ref, g3_ref, w_ref, x_ref, lg_ref, lb_ref,
             cat_ref, yg_ref, xo_ref, xb_ref, r_ref):
        at = 0
        for ref, width in zip((oa_ref, ob_ref, oc_ref, om_ref), widths):
            cat_ref[:, at:at + width] = ref[...]
            at += width
        for j, g_ref in enumerate((g0_ref, g1_ref, g2_ref, g3_ref)):
            gate = g_ref[...]
            cols = slice(j * 512, (j + 1) * 512)
            yg_ref[:, cols] = (cat_ref[:, cols] * (gate * jax.nn.sigmoid(gate))).astype(BF16)
        r = ALPHA * x_ref[...] + _dot(yg_ref[...], w_ref[...])
        r_ref[...] = r
        xc = r - jnp.mean(r, axis=1, keepdims=True)
        o = xc * lax.rsqrt(jnp.mean(xc * xc, axis=1, keepdims=True) + LN_EPS) * lg_ref[...] + lb_ref[...]
        xo_ref[...] = o
        xb_ref[...] = o.astype(BF16)

    row = _row_spec(tm, D_MODEL)
    return pl.pallas_call(
        body, name="gate_out_ln", grid=(S // tm,),
        in_specs=[_row_spec(tm, width) for width in widths] + [_row_spec(tm, 512, GATE // 512 + j) for j in range(4)]
                 + [_fix_spec((D_MODEL, D_MODEL)), row, _fix_spec((1, D_MODEL)), _fix_spec((1, D_MODEL))],
        out_specs=[row] * 5,
        out_shape=[jax.ShapeDtypeStruct((S, D_MODEL), t) for t in (F32, BF16, F32, BF16, F32)],
        compiler_params=_params(("parallel",)))(*branches, h, h, h, h, w_out, x, g.reshape(1, D_MODEL), b.reshape(1, D_MODEL))


def _out_proj_gate_bwd(dr, w_out, o_cat, h):
    S = h.shape[0]
    tm = _tile(S, 1024)

    def body(dr_ref, w_ref, o_ref, g_ref, do_ref, dg_ref, drb):
        @pl.when(pl.program_id(1) == 0)
        def _():
            drb[...] = dr_ref[...].astype(BF16)

        d = _dot_nt(drb[...], w_ref[...])
        g = g_ref[...]
        sig = jax.nn.sigmoid(g)
        do_ref[...] = d * (g * sig)
        dg_ref[...] = d * o_ref[...] * (sig * (1.0 + g * (1.0 - sig)))

    blk = pl.BlockSpec((tm, 512), lambda i, j: (i, j))
    return pl.pallas_call(
        body, name="d_out_proj_gate", grid=(S // tm, 4),
        in_specs=[pl.BlockSpec((tm, D_MODEL), lambda i, j: (i, 0)), pl.BlockSpec((512, D_MODEL), lambda i, j: (j, 0)),
                  blk, pl.BlockSpec((tm, 512), lambda i, j: (i, GATE // 512 + j))],
        out_specs=[blk, blk],
        out_shape=[jax.ShapeDtypeStruct((S, D_MODEL), F32), jax.ShapeDtypeStruct((S, D_MODEL), F32)],
        scratch_shapes=[pltpu.VMEM((tm, D_MODEL), BF16)],
        compiler_params=_params(("parallel", "arbitrary")))(dr, w_out, o_cat, h)


def _ln_res_bwd(dout, r, g):
    S = r.shape[0]
    tm = _tile(S, 512)

    def body(d_ref, r_ref, g_ref, dr_ref, dg_ref, db_ref):
        @pl.when(pl.program_id(0) == 0)
        def _():
            dg_ref[...] = jnp.zeros_like(dg_ref)
            db_ref[...] = jnp.zeros_like(db_ref)

        d, r = d_ref[...], r_ref[...]
        xc = r - jnp.mean(r, axis=1, keepdims=True)
        rstd = lax.rsqrt(jnp.mean(xc * xc, axis=1, keepdims=True) + LN_EPS)
        xhat = xc * rstd
        dxh = d * g_ref[...]
        dr_ref[...] = rstd * (dxh - jnp.mean(dxh, axis=1, keepdims=True)
                              - xhat * jnp.mean(dxh * xhat, axis=1, keepdims=True))
        dg_ref[...] += jnp.sum(d * xhat, axis=0, keepdims=True)
        db_ref[...] += jnp.sum(d, axis=0, keepdims=True)

    return pl.pallas_call(
        body, name="ln_res_bwd", grid=(S // tm,),
        in_specs=[_row_spec(tm, D_MODEL), _row_spec(tm, D_MODEL), _fix_spec((1, D_MODEL))],
        out_specs=[_row_spec(tm, D_MODEL), _fix_spec((1, D_MODEL)), _fix_spec((1, D_MODEL))],
        out_shape=[jax.ShapeDtypeStruct((S, D_MODEL), F32), jax.ShapeDtypeStruct((1, D_MODEL), F32),
                   jax.ShapeDtypeStruct((1, D_MODEL), F32)],
        compiler_params=_params(("arbitrary",)))(dout, r, g.reshape(1, D_MODEL))


def _loss_head(y, target):
    S = y.shape[0]
    tm = _tile(S, 512)

    def body(y_ref, t_ref, l_ref, d_ref):
        @pl.when(pl.program_id(0) == 0)
        def _():
            l_ref[...] = jnp.zeros_like(l_ref)

        diff = y_ref[...] - t_ref[...]
        d_ref[...] = diff * (1.0 / D_MODEL)
        per_row = jnp.mean(diff * diff, axis=1, keepdims=True)
        l_ref[...] += 0.5 * jnp.sum(per_row, axis=0, keepdims=True)

    return pl.pallas_call(
        body, name="loss_head", grid=(S // tm,), in_specs=[_row_spec(tm, D_MODEL), _row_spec(tm, D_MODEL)],
        out_specs=[_fix_spec((8, LANE)), _row_spec(tm, D_MODEL)],
        out_shape=[jax.ShapeDtypeStruct((8, LANE), F32), jax.ShapeDtypeStruct((S, D_MODEL), F32)],
        compiler_params=_params(("arbitrary",)))(y, target)


def _perm_table():
    table, at = [], 0
    for name in PERM_ORDER:
        start, width = ORIG[name]
        table.append((name, start, width, at))
        at += width
    return table


def _permute_w_in(by_chip):
    wc = by_chip.shape[-1]
    parts = []
    for _, start, width, _ in _perm_table():
        lo = start
        while lo < start + width:
            k = lo // wc
            hi = min(start + width, (k + 1) * wc)
            parts.append(by_chip[k, ..., lo - k * wc:hi - k * wc])
            lo = hi
    parts.append(jnp.zeros(by_chip.shape[1:-1] + (HP - D_IN,), by_chip.dtype))
    return jnp.concatenate(parts, axis=-1)


def _model_cols(wp, lo, hi):
    parts = []
    for _, start, width, at in sorted(_perm_table(), key=lambda t: t[1]):
        a, b = max(lo, start), min(hi, start + width)
        if a < b:
            parts.append(wp[..., at + a - start:at + b - start])
    return jnp.concatenate(parts, axis=-1)


def _rope_tables(positions):
    inv_freq = ROPE_THETA ** (-jnp.arange(0, 64, 2, dtype=F32) / 64)
    ang = positions.astype(F32)[:, None] * inv_freq[None, :]
    cos, sin, zero = jnp.cos(ang), jnp.sin(ang), jnp.zeros((positions.shape[0], 64), F32)
    return jnp.concatenate([cos, cos, zero], axis=1), jnp.concatenate([-sin, sin, zero], axis=1)


def _layer_weights(by_chip):
    w_in, w_uq, w_ukv, w_mem_k, w_mem_v, w_out = by_chip
    w_uq = jnp.concatenate([w_uq[k] for k in range(4)], axis=1)
    w_uq = jnp.pad(w_uq.reshape(512, MLA_HEADS, 192), ((0, 0), (0, 0), (0, 64))).reshape(512, MLA_HEADS * 256)
    return (_permute_w_in(w_in), w_uq, jnp.concatenate([w_ukv[k] for k in range(4)], axis=1),
            w_mem_k.reshape(D_MODEL, 256), w_mem_v.reshape(D_MODEL, 256), w_out.reshape(D_MODEL, D_MODEL))


def _local_step(x, mem, positions, target, w, layer_source, next_shards, early_reduce=None):
    rc, rs = _rope_tables(positions)
    mem_b = mem.astype(BF16)
    xb = x.astype(BF16)
    saved = []
    fetched = None
    for l in range(DEPTH):
        w_in, w_uq, w_ukv, w_mem_k, w_mem_v, w_out = layer_source(l, fetched)
        h = _mm(xb, w_in, tm=1024, tn=1152, tk=2048, name="in_proj")
        cq_n = _rms_fwd(h, CQ, 512, w["q_norm_g"][l], "rms_q")
        ckv_n = _rms_fwd(h, CKV, 256, w["kv_norm_g"][l], "rms_kv")
        q = _q_proj(cq_n, w_uq, rc, rs)
        kp, v = _kv_proj(ckv_n, w_ukv, h, rc, rs)
        o_a, lse, fetched = _mla_fwd(q, kp, v, next_shards(l))
        bias_t = w["sg_b"][l].T
        o_b = _sg_fwd(h, w["sg_ln_g"][l], w["sg_ln_b"][l], w["sg_w"][l], bias_t)
        qkv = jnp.concatenate([h[:, SBQ:SBQ + 512] * SB_SCALE, h[:, SBQ + 512:SBQ + 1536]], axis=1).astype(BF16)
        o_c = _sb_fwd(qkv)
        mk = _mm(mem_b, w_mem_k, out_dtype=BF16, name="mem_kv")
        mv = _mm(mem_b, w_mem_v, out_dtype=BF16, name="mem_kv")
        o_m = _mem_fwd(h, mk, mv)
        o_cat, yg, x_new, xb_new, r = _gate_out_ln((o_a, o_b, o_c, o_m), h, w_out, x, w["ln_g"][l], w["ln_b"][l])
        saved.append(dict(xb=xb, h=h, cq_n=cq_n, ckv_n=ckv_n, q=q, kp=kp, v=v, lse=lse, qkv=qkv, mk=mk, mv=mv,
                          o_cat=o_cat, yg=yg, r=r, w_in=w_in, w_uq=w_uq, w_ukv=w_ukv, w_out=w_out, bias_t=bias_t))
        x, xb = x_new, xb_new

    loss, dx = _loss_head(x, target)

    grads = {n: [None] * DEPTH for n in SHARDED + SMALL}
    early = None
    for l in reversed(range(DEPTH)):
        s = saved[l]
        h = s["h"]
        dr, dlg, dlb = _ln_res_bwd(dx, s["r"], w["ln_g"][l])
        grads["ln_g"][l], grads["ln_b"][l] = dlg[0], dlb[0]
        grads["w_out"][l] = _mm(s["yg"], dr, ta=True, tm=1024, tn=1024, tk=2048, name="dw_out")
        do_cat, dgates = _out_proj_gate_bwd(dr, s["w_out"], s["o_cat"], h)
        dmq, dmk, dmv = _mem_bwd(h, s["mk"], s["mv"], do_cat, 1792)
        grads["w_mem_k"][l] = _mm(mem_b, dmk, ta=True, name="dw_mem")
        grads["w_mem_v"][l] = _mm(mem_b, dmv, ta=True, name="dw_mem")
        dsq, dsk, dsv = _sb_bwd(s["qkv"], do_cat, s["o_cat"], 1280 // LANE)
        du, dv, dsgw, dsgb, dsg_g, dsg_b = _sg_bwd(h, w["sg_ln_g"][l], w["sg_ln_b"][l], w["sg_w"][l], s["bias_t"],
                                                   do_cat, 768)
        grads["sg_w"][l], grads["sg_b"][l] = dsgw, dsgb.T
        grads["sg_ln_g"][l], grads["sg_ln_b"][l] = dsg_g[0], dsg_b[0]
        ride = early_reduce(grads) if early_reduce and l == 0 else None
        dq_raw, dk, dvv, arrived = _mla_bwd(s["q"], s["kp"], s["v"], do_cat, s["o_cat"], s["lse"], rc, rs, ride)
        if ride:
            early = (ride, arrived)
        dkv, dkpe = _kv_bwd_prep(dk, dvv, rc, rs)
        dw_uq = _mm(s["cq_n"], dq_raw, ta=True, tn=1536, tk=2048, name="dw_uq")
        grads["w_uq"][l] = dw_uq.reshape(512, MLA_HEADS, 256)[:, :, :192].reshape(512, MLA_HEADS * 192)
        grads["w_ukv"][l] = _mm(s["ckv_n"], dkv, ta=True, tn=1536, tk=2048, name="dw_ukv")
        dcq_n = _mm(dq_raw, s["w_uq"], tb=True, tm=2048, tk=1536, name="d_cq")
        dckv_n = _mm(dkv, s["w_ukv"], tb=True, tm=2048, tk=1536, name="d_ckv")
        dcq, dqg = _rms_bwd(h, CQ, 512, w["q_norm_g"][l], dcq_n, "rms_q_bwd")
        dckv, dkvg = _rms_bwd(h, CKV, 256, w["kv_norm_g"][l], dckv_n, "rms_kv_bwd")
        grads["q_norm_g"][l], grads["kv_norm_g"][l] = dqg[0], dkvg[0]
        dh = jnp.concatenate([dcq, dckv, dmq, du, dv, dsq, dsk, dsv, dgates, dkpe], axis=1).astype(BF16)
        dw_in = _mm(s["xb"], dh, ta=True, tm=1024, tn=1152, tk=2048, name="dw_in")
        grads["w_in"][l] = dw_in
        dx = _mm(dh, s["w_in"], tb=True, add=dr, add_scale=ALPHA, tm=1024, tn=1024, tk=1920, name="d_in_proj")

    return loss, dx, grads, early


MESH = pl.DeviceIdType.MESH
HBM_SPEC = pl.BlockSpec(memory_space=pltpu.HBM)


def _place():
    x, y, c = lax.axis_index("x"), lax.axis_index("y"), lax.axis_index("c")
    return x, y, c, [(1 - x, y), (x, 1 - y), (1 - x, 1 - y)]


def _comm_call(body, name, arrays, out_shapes, n_sems):
    return pl.pallas_call(
        body, name=name, in_specs=[HBM_SPEC] * len(arrays), out_specs=[HBM_SPEC] * len(out_shapes), out_shape=out_shapes,
        scratch_shapes=[pltpu.SemaphoreType.DMA((n_sems,)), pltpu.SemaphoreType.DMA((n_sems,))],
        compiler_params=pltpu.CompilerParams(has_side_effects=True))(*arrays)


def _gather_weights(shards):
    na = len(shards)

    def body(*refs):
        srcs, outs, (send_sems, recv_sems) = refs[:na], refs[na:2 * na], refs[2 * na:]
        x, y, c, chips = _place()
        mine = [pl.ds((s.shape[0] // 2) * c, s.shape[0] // 2) for s in shards]
        theirs = [pl.ds((s.shape[0] // 2) * (1 - c), s.shape[0] // 2) for s in shards]

        def copy(a, k, src_ref, chip, part, to):
            return pltpu.make_async_remote_copy(
                src_ref=src_ref, dst_ref=outs[a].at[chip, part], send_sem=send_sems.at[6 * a + k],
                recv_sem=recv_sems.at[6 * a + k], device_id=to, device_id_type=MESH)

        sent = [copy(a, j, srcs[a].at[mine[a]], 2 * x + y, mine[a], (px, py, c))
                for a in range(na) for j, (px, py) in enumerate(chips)]
        for cp in sent:
            cp.start()
        passed = []
        for j, (px, py) in enumerate(chips):
            for a in range(na):
                copy(a, j, srcs[a].at[mine[a]], 2 * px + py, mine[a], (px, py, c)).wait_recv()
                cp = copy(a, 3 + j, outs[a].at[2 * px + py, mine[a]], 2 * px + py, mine[a], (x, y, 1 - c))
                cp.start()
                passed.append(cp)
        for j, (px, py) in enumerate(chips):
            for a in range(na):
                copy(a, 3 + j, srcs[a].at[theirs[a]], 2 * px + py, theirs[a], (x, y, 1 - c)).wait_recv()
        for cp in sent + passed:
            cp.wait_send()

    return _comm_call(body, "gather_weights", shards, [jax.ShapeDtypeStruct((4,) + s.shape, s.dtype) for s in shards], 6 * na)


def _swap_halves(gs):
    na = len(gs)

    def body(*refs):
        srcs, outs, (send_sems, recv_sems) = refs[:na], refs[na:2 * na], refs[2 * na:]
        x, y, c, _ = _place()
        cps = [pltpu.make_async_remote_copy(
            src_ref=srcs[a].at[:, :, pl.ds((gs[a].shape[2] // 2) * (1 - c), gs[a].shape[2] // 2)], dst_ref=outs[a],
            send_sem=send_sems.at[a], recv_sem=recv_sems.at[a], device_id=(x, y, 1 - c), device_id_type=MESH)
            for a in range(na)]
        for cp in cps:
            cp.start()
        for cp in cps:
            cp.wait()

    return _comm_call(body, "swap_halves", gs,
                      [jax.ShapeDtypeStruct(g.shape[:2] + (g.shape[2] // 2, g.shape[3]), g.dtype) for g in gs], na)


def _pair_sum(g, other, c):
    _, L, R, C = g.shape
    tr = _row_tile(R // 2, 3 * C * 4)
    nb = R // 2 // tr

    def body(c_ref, a_ref, b_ref, o_ref):
        o_ref[...] = (a_ref[...] + b_ref[...]).astype(BF16)

    blk = pl.BlockSpec((None, None, tr, C), lambda d, l, i, c_ref: (d, l, i, 0))
    return pl.pallas_call(
        body, name="pair_sum",
        grid_spec=pltpu.PrefetchScalarGridSpec(
            num_scalar_prefetch=1, grid=(4, L, nb),
            in_specs=[pl.BlockSpec((None, None, tr, C), lambda d, l, i, c_ref: (d, l, nb * c_ref[0] + i, 0)), blk],
            out_specs=blk),
        out_shape=jax.ShapeDtypeStruct((4, L, R // 2, C), BF16),
        compiler_params=_params(("parallel", "parallel", "parallel")))(c, g, other)


def _exchange_copies(srcs, outs, send_sems, recv_sems):
    x, y, c, chips = _place()
    return [pltpu.make_async_remote_copy(
        src_ref=srcs[a].at[2 * px + py], dst_ref=outs[a].at[j], send_sem=send_sems.at[3 * a + j],
        recv_sem=recv_sems.at[3 * a + j], device_id=(px, py, c), device_id_type=MESH)
        for a in range(len(srcs)) for j, (px, py) in enumerate(chips)]


def _exchange_chips(ps):
    na = len(ps)

    def body(*refs):
        cps = _exchange_copies(refs[:na], refs[na:2 * na], *refs[2 * na:])
        for cp in cps:
            cp.start()
        for cp in cps:
            cp.wait()

    return _comm_call(body, "exchange_chips", ps, [jax.ShapeDtypeStruct((3,) + p.shape[1:], p.dtype) for p in ps], 3 * na)


def _chip_sum(p, got, me):
    _, L, R, C = p.shape
    tr = _row_tile(R, 4 * C * 4)

    def body(me_ref, p_ref, g_ref, o_ref):
        acc = p_ref[...].astype(F32)
        for k in range(3):
            acc = acc + g_ref[k].astype(F32)
        o_ref[...] = acc

    return pl.pallas_call(
        body, name="chip_sum",
        grid_spec=pltpu.PrefetchScalarGridSpec(
            num_scalar_prefetch=1, grid=(L, R // tr),
            in_specs=[pl.BlockSpec((None, None, tr, C), lambda l, i, me_ref: (me_ref[0], l, i, 0)),
                      pl.BlockSpec((3, None, tr, C), lambda l, i, me_ref: (0, l, i, 0))],
            out_specs=pl.BlockSpec((None, tr, C), lambda l, i, me_ref: (l, i, 0))),
        out_shape=jax.ShapeDtypeStruct((L, R, C), F32), compiler_params=_params(("parallel", "parallel")))(me, p, got)


def _sum_parts(t, name):
    n, H, W = t.shape
    th = _row_tile(H, (n + 1) * W * 4)

    def body(t_ref, o_ref):
        acc = t_ref[0]
        for k in range(1, n):
            acc = acc + t_ref[k]
        o_ref[...] = acc

    return pl.pallas_call(
        body, name=name, grid=(H // th,), in_specs=[pl.BlockSpec((n, th, W), lambda i: (0, i, 0))],
        out_specs=pl.BlockSpec((th, W), lambda i: (i, 0)), out_shape=jax.ShapeDtypeStruct((H, W), F32),
        compiler_params=_params(("parallel",)))(t)


def _share_with_sibling(halves):
    na = len(halves)

    def body(*refs):
        srcs, outs, (send_sems, recv_sems) = refs[:na], refs[na:2 * na], refs[2 * na:]
        x, y, c, _ = _place()

        def copy(a, which):
            hr = halves[a].shape[1]
            return pltpu.make_async_remote_copy(
                src_ref=srcs[a], dst_ref=outs[a].at[:, pl.ds(hr * which, hr)], send_sem=send_sems.at[a],
                recv_sem=recv_sems.at[a], device_id=(x, y, 1 - c), device_id_type=MESH)

        sent = [copy(a, c) for a in range(na)]
        for cp in sent:
            cp.start()
        for a in range(na):
            copy(a, 1 - c).wait_recv()
        for cp in sent:
            cp.wait_send()

    return _comm_call(body, "share_with_sibling", halves,
                      [jax.ShapeDtypeStruct((h.shape[0], 2 * h.shape[1], h.shape[2]), h.dtype) for h in halves], na)


def _gather_all(v):
    n, W = v.shape

    def body(src, out, send_sems, recv_sems, own_sem):
        x, y, c, _ = _place()
        own = pltpu.make_async_copy(src, out.at[4 * x + 2 * y + c], own_sem)
        own.start()
        flips = [(fx, fy, fc) for fx in (0, 1) for fy in (0, 1) for fc in (0, 1)][1:]
        sent = []
        for k, (fx, fy, fc) in enumerate(flips):
            cp = pltpu.make_async_remote_copy(
                src_ref=src, dst_ref=out.at[4 * x + 2 * y + c], send_sem=send_sems.at[k], recv_sem=recv_sems.at[k],
                device_id=(x ^ fx, y ^ fy, c ^ fc), device_id_type=MESH)
            cp.start()
            sent.append(cp)
        for k, (fx, fy, fc) in enumerate(flips):
            pltpu.make_async_remote_copy(
                src_ref=src, dst_ref=out.at[4 * (x ^ fx) + 2 * (y ^ fy) + (c ^ fc)], send_sem=send_sems.at[k],
                recv_sem=recv_sems.at[k], device_id=(x ^ fx, y ^ fy, c ^ fc), device_id_type=MESH).wait_recv()
        for cp in sent:
            cp.wait_send()
        own.wait()

    return pl.pallas_call(
        body, name="gather_all", in_specs=[HBM_SPEC], out_specs=HBM_SPEC,
        out_shape=jax.ShapeDtypeStruct((8, n, W), v.dtype),
        scratch_shapes=[pltpu.SemaphoreType.DMA((7,)), pltpu.SemaphoreType.DMA((7,)), pltpu.SemaphoreType.DMA(())],
        compiler_params=pltpu.CompilerParams(has_side_effects=True))(v)


def _adamw(w, g, m, v):
    shape = w.shape
    cols = shape[-1]
    w2, g2, m2, v2 = (a.reshape(-1, cols) for a in (w, g, m, v))
    rows = w2.shape[0]
    tr = next((t for t in (1024, 512, 256, 128, 64, 32, 16, 8) if rows % t == 0 and t * cols * 4 <= (2 << 20)), rows)

    def body(w_ref, g_ref, m_ref, v_ref, d_ref, nm_ref, nv_ref):
        g_ = g_ref[...]
        nm = ADAM_B1 * m_ref[...] + (1.0 - ADAM_B1) * g_
        nv = ADAM_B2 * v_ref[...] + (1.0 - ADAM_B2) * (g_ * g_)
        m_hat = nm / (1.0 - ADAM_B1 ** ADAM_STEP)
        v_hat = nv / (1.0 - ADAM_B2 ** ADAM_STEP)
        d_ref[...] = -ADAM_LR * (m_hat / (jnp.sqrt(v_hat) + ADAM_EPS) + ADAM_WD * w_ref[...])
        nm_ref[...] = nm
        nv_ref[...] = nv

    blk = pl.BlockSpec((tr, cols), lambda i: (i, 0))
    outs = pl.pallas_call(
        body, name="adamw", grid=(rows // tr,), in_specs=[blk] * 4, out_specs=[blk] * 3,
        out_shape=[jax.ShapeDtypeStruct((rows, cols), F32)] * 3, compiler_params=_params(("parallel",)))(w2, g2, m2, v2)
    return tuple(o.reshape(shape) for o in outs)


BY_COLUMNS = ("w_in", "w_uq", "w_ukv")


def _chip_part(name, a, k):
    if name == "w_in":
        n = D_IN // 4
        return _model_cols(a, k * n, (k + 1) * n)
    n = a.shape[1 if name in BY_COLUMNS else 0] // 4
    return a[:, k * n:(k + 1) * n] if name in BY_COLUMNS else a[k * n:(k + 1) * n]


def kernel(x, mem, positions, w_in, q_norm_g, w_uq, kv_norm_g, w_ukv, sg_ln_g, sg_ln_b, sg_w, sg_b, w_mem_k, w_mem_v, w_out, ln_g, ln_b, loss_target, m_w_in, m_q_norm_g, m_w_uq, m_kv_norm_g, m_w_ukv, m_sg_ln_g, m_sg_ln_b, m_sg_w, m_sg_b, m_w_mem_k, m_w_mem_v, m_w_out, m_ln_g, m_ln_b, v_w_in, v_q_norm_g, v_w_uq, v_kv_norm_g, v_w_ukv, v_sg_ln_g, v_sg_ln_b, v_sg_w, v_sg_b, v_w_mem_k, v_w_mem_v, v_w_out, v_ln_g, v_ln_b):
    weights = dict(w_in=w_in, q_norm_g=q_norm_g, w_uq=w_uq, kv_norm_g=kv_norm_g, w_ukv=w_ukv, sg_ln_g=sg_ln_g,
                   sg_ln_b=sg_ln_b, sg_w=sg_w, sg_b=sg_b, w_mem_k=w_mem_k, w_mem_v=w_mem_v, w_out=w_out, ln_g=ln_g, ln_b=ln_b)
    mom_m = dict(w_in=m_w_in, q_norm_g=m_q_norm_g, w_uq=m_w_uq, kv_norm_g=m_kv_norm_g, w_ukv=m_w_ukv, sg_ln_g=m_sg_ln_g,
                 sg_ln_b=m_sg_ln_b, sg_w=m_sg_w, sg_b=m_sg_b, w_mem_k=m_w_mem_k, w_mem_v=m_w_mem_v, w_out=m_w_out,
                 ln_g=m_ln_g, ln_b=m_ln_b)
    mom_v = dict(w_in=v_w_in, q_norm_g=v_q_norm_g, w_uq=v_w_uq, kv_norm_g=v_kv_norm_g, w_ukv=v_w_ukv, sg_ln_g=v_sg_ln_g,
                 sg_ln_b=v_sg_ln_b, sg_w=v_sg_w, sg_b=v_sg_b, w_mem_k=v_w_mem_k, w_mem_v=v_w_mem_v, w_out=v_w_out,
                 ln_g=v_ln_g, ln_b=v_ln_b)
    c_idx = lax.axis_index("c").astype(jnp.int32).reshape(1)

    me = 2 * lax.axis_index("x") + lax.axis_index("y")
    shards = [[weights[n][l].astype(BF16) for n in SHARDED] for l in range(DEPTH)]

    def layer_source(l, fetched):
        if l == 0:
            fetched = _gather_weights(shards[0])
        return _layer_weights([lax.dynamic_update_slice(g, s[None], (me, 0, 0)) for g, s in zip(fetched, shards[l])])

    def pair_sums(grads, layers):
        gs = [jnp.stack([jnp.stack([_chip_part(n, grads[n][l], k) for l in layers]) for k in range(4)]) for n in SHARDED]
        return [_pair_sum(g, o, c_idx) for g, o in zip(gs, _swap_halves(gs))]

    loss_dev, grad_x, grads, (pairs_hi, got_hi) = _local_step(
        x[0], mem[0], positions[0], loss_target[0], {n: weights[n] for n in SMALL}, layer_source,
        lambda l: shards[l + 1] if l + 1 < DEPTH else None, lambda grads: pair_sums(grads, range(1, DEPTH)))
    pairs_lo = pair_sums(grads, [0])
    got_lo = _exchange_chips(pairs_lo)
    me1 = me.astype(jnp.int32).reshape(1)
    halves = [_chip_sum(p, o, me1) for p, o in zip(pairs_lo + pairs_hi, list(got_lo) + list(got_hi))]
    whole = [lax.dynamic_update_slice(r, h, (0, h.shape[1] * c_idx[0], 0)) for r, h in zip(_share_with_sibling(halves), halves)]
    grad_out = {n: jnp.concatenate([lo, hi]) for n, lo, hi in zip(SHARDED, whole[:len(SHARDED)], whole[len(SHARDED):])}

    small_sizes = [weights[n].size for n in SMALL]
    vec = jnp.concatenate([g.reshape(-1) for n in SMALL for g in grads[n]] + [loss_dev[0]])
    n_small = vec.shape[0]
    rows_small = -(-n_small // (8 * FLAT_W)) * 8
    vec = jnp.pad(vec, (0, rows_small * FLAT_W - n_small)).reshape(rows_small, FLAT_W)
    total = _sum_parts(_gather_all(vec), "device_sum").reshape(-1)
    at = 0
    for n, size in zip(SMALL, small_sizes):
        grad_out[n] = total[at:at + size].reshape(weights[n].shape)
        at += size
    loss = total[at]

    names = list(weights)
    upd = {n: _adamw(weights[n], grad_out[n], mom_m[n], mom_v[n]) for n in names}
    return (loss, grad_x[None], *[grad_out[n] for n in names], *[upd[n][0] for n in names],
            *[upd[n][1] for n in names], *[upd[n][2] for n in names])
```

```python
import math

import jax
import jax.numpy as jnp
from jax import lax
from jax.experimental import pallas as pl
from jax.experimental.pallas import tpu as pltpu

F32, BF16 = jnp.float32, jnp.bfloat16

D_MODEL = 2048
DEPTH = 4
CHUNK = 64
MLA_HEADS = 6
MLA_SCALE = 1.0 / math.sqrt(192.0)
SB_HEADS = 4
SB_SCALE = 1.0 / math.sqrt(128.0)
MEM_HEADS = 4
MEM_SCALE = 1.0 / math.sqrt(64.0)
ROPE_THETA = 10000.0
ALPHA = (2.0 * DEPTH) ** 0.25
LN_EPS = 1e-5
RMS_EPS = 1e-6
ADAM_LR, ADAM_B1, ADAM_B2, ADAM_EPS, ADAM_WD, ADAM_STEP = 0.001, 0.9, 0.999, 1e-08, 0.01, 10

ORIG = dict(c_q=(0, 512), c_kv=(512, 256), k_pe=(768, 64), g_a=(832, 768), sg_u=(1600, 512), sg_v=(2112, 512),
            g_b=(2624, 512), sb_q=(3136, 512), sb_k=(3648, 512), sb_v=(4160, 512), g_c=(4672, 512),
            m_q=(5184, 256), g_m=(5440, 256))
D_IN = 5696
PERM_ORDER = ("c_q", "c_kv", "m_q", "sg_u", "sg_v", "sb_q", "sb_k", "sb_v", "g_a", "g_b", "g_c", "g_m", "k_pe")
HP = 5760
CQ, CKV, MQ, SGU, SGV, SBQ, GATE, KPE = 0, 512, 768, 1024, 1536, 2048, 3584, 5632

Q_BLK = 2048
K_BLK = 512
MLA_FWD_K_BLK = 1024
SB_Q_BLK = 512
SB_K_BLK = 256
SB_DEAD = -110.0
LANE = 128
VMEM_LIMIT = 56 * 1024 * 1024

FLAT_W = 1024
SHARDED = ("w_in", "w_uq", "w_ukv", "w_mem_k", "w_mem_v", "w_out")
SMALL = ("q_norm_g", "kv_norm_g", "sg_ln_g", "sg_ln_b", "sg_w", "sg_b", "ln_g", "ln_b")


def _params(sem=None):
    return pltpu.CompilerParams(dimension_semantics=sem, vmem_limit_bytes=VMEM_LIMIT)


def _tile(dim, pref):
    if dim <= pref:
        return dim
    t = (pref // LANE) * LANE
    while t >= LANE:
        if dim % t == 0:
            return t
        t -= LANE
    return dim


def _row_tile(rows, bytes_per_row, budget=8 << 20):
    best = None
    for t in range(8, rows + 1, 8):
        if rows % t == 0 and t * bytes_per_row <= budget:
            best = t
    return best if best else rows


def _dot_nt(a, b):
    return lax.dot_general(a, b, (((1,), (1,)), ((), ())), preferred_element_type=F32)


def _dot_tn(a, b):
    return lax.dot_general(a, b, (((0,), (0,)), ((), ())), preferred_element_type=F32)


def _dot(a, b):
    return jnp.dot(a, b, preferred_element_type=F32)


def _mm(a, b, *, ta=False, tb=False, add=None, add_scale=1.0, out_dtype=F32, tm=512, tn=512, tk=512, name="mm",
        ride=None):
    (K, M) = a.shape if ta else a.shape[::-1]
    (N, Kb) = b.shape if tb else b.shape[::-1]
    assert K == Kb, (a.shape, b.shape, ta, tb)
    tm, tn, tk = _tile(M, tm), _tile(N, tn), _tile(K, tk)
    gm, gn, nk = M // tm, N // tn, K // tk
    a_spec = pl.BlockSpec((tk, tm), lambda i, j, k: (k, i)) if ta else pl.BlockSpec((tm, tk), lambda i, j, k: (i, k))
    b_spec = pl.BlockSpec((tn, tk), lambda i, j, k: (j, k)) if tb else pl.BlockSpec((tk, tn), lambda i, j, k: (k, j))
    o_spec = pl.BlockSpec((tm, tn), lambda i, j, k: (i, j))
    dn = (((0 if ta else 1,), (1 if tb else 0,)), ((), ()))
    has_add = add is not None
    na = len(ride) if ride else 0
    n_in = 2 + has_add + na

    def body(*refs):
        a_ref, b_ref = refs[:2]
        add_ref = refs[2] if has_add else None
        o_ref = refs[n_in]
        scratch = refs[n_in + 1 + na:]
        i, j, k = pl.program_id(0), pl.program_id(1), pl.program_id(2)

        def riders():
            return _exchange_copies(refs[n_in - na:n_in], refs[n_in + 1:n_in + 1 + na], *scratch[-2:])

        if na:
            @pl.when(jnp.logical_and(jnp.logical_and(i == 0, j == 0), k == 0))
            def _():
                for cp in riders():
                    cp.start()

        part = lax.dot_general(a_ref[...].astype(BF16), b_ref[...].astype(BF16), dn, preferred_element_type=F32)

        def finish(r):
            if has_add:
                r = r + add_scale * add_ref[...]
            o_ref[...] = r.astype(o_ref.dtype)

        if nk == 1:
            finish(part)
        else:
            acc_ref = scratch[0]

            @pl.when(k == 0)
            def _():
                acc_ref[...] = part

            @pl.when(k > 0)
            def _():
                acc_ref[...] += part

            @pl.when(k == nk - 1)
            def _():
                finish(acc_ref[...])

        if na:
            @pl.when(jnp.logical_and(jnp.logical_and(i == gm - 1, j == gn - 1), k == nk - 1))
            def _():
                for cp in riders():
                    cp.wait()

    ins = [a, b] + ([add] if has_add else []) + list(ride or [])
    specs = [a_spec, b_spec] + ([o_spec] if has_add else []) + [HBM_SPEC] * na
    res = pl.pallas_call(
        body, name=name, grid=(gm, gn, nk), in_specs=specs, out_specs=[o_spec] + [HBM_SPEC] * na,
        out_shape=[jax.ShapeDtypeStruct((M, N), out_dtype)]
                  + [jax.ShapeDtypeStruct((3,) + p.shape[1:], p.dtype) for p in (ride or [])],
        scratch_shapes=([pltpu.VMEM((tm, tn), F32)] if nk > 1 else [])
                       + ([pltpu.SemaphoreType.DMA((3 * na,)), pltpu.SemaphoreType.DMA((3 * na,))] if na else []),
        compiler_params=pltpu.CompilerParams(
            dimension_semantics=("arbitrary",) * 3 if na else ("parallel", "parallel", "arbitrary"),
            vmem_limit_bytes=VMEM_LIMIT, has_side_effects=bool(na)))(*ins)
    return (res[0], list(res[1:])) if na else res[0]


GELU_K = math.sqrt(2.0 / math.pi)


def _gelu(x):
    t = jnp.tanh(GELU_K * (x + 0.044715 * (x * x * x)))
    return 0.5 * x * (1.0 + t)


def _gelu_grad(x):
    t = jnp.tanh(GELU_K * (x + 0.044715 * (x * x * x)))
    return 0.5 * (1.0 + t) + 0.5 * x * (1.0 - t * t) * GELU_K * (1.0 + 3.0 * 0.044715 * x * x)


def _rope_swap(t):
    lane = lax.broadcasted_iota(jnp.int32, t.shape, 1)
    return jnp.where(lane < 32, pltpu.roll(t, 96, axis=1), pltpu.roll(t, 32, axis=1))


def _rope(t, c, s):
    return t * c + _rope_swap(t) * s


def _rope_bwd(dt, c, s):
    return dt * c - _rope_swap(dt) * s


def _row_spec(tm, w, cb=0):
    return pl.BlockSpec((tm, w), lambda i: (i, cb))


def _fix_spec(shape):
    return pl.BlockSpec(shape, lambda *_: (0,) * len(shape))


def _rms_fwd(h, off, width, g, name):
    S = h.shape[0]
    tm = _tile(S, 2048)

    def body(x_ref, g_ref, o_ref):
        x = x_ref[...]
        r = lax.rsqrt(jnp.mean(x * x, axis=1, keepdims=True) + RMS_EPS)
        o_ref[...] = (x * r * g_ref[...]).astype(BF16)

    return pl.pallas_call(
        body, name=name, grid=(S // tm,), in_specs=[_row_spec(tm, width, off // width), _fix_spec((1, width))],
        out_specs=_row_spec(tm, width), out_shape=jax.ShapeDtypeStruct((S, width), BF16),
        compiler_params=_params(("parallel",)))(h, g.reshape(1, width))


def _rms_bwd(h, off, width, g, dxn, name):
    S = h.shape[0]
    tm = _tile(S, 2048)

    def body(x_ref, g_ref, d_ref, dx_ref, dg_ref):
        @pl.when(pl.program_id(0) == 0)
        def _():
            dg_ref[...] = jnp.zeros_like(dg_ref)

        x, d = x_ref[...], d_ref[...]
        r = lax.rsqrt(jnp.mean(x * x, axis=1, keepdims=True) + RMS_EPS)
        gd = d * g_ref[...]
        dx_ref[...] = gd * r - x * (r * r * r) * jnp.mean(gd * x, axis=1, keepdims=True)
        dg_ref[...] += jnp.sum(d * x * r, axis=0, keepdims=True)

    return pl.pallas_call(
        body, name=name, grid=(S // tm,),
        in_specs=[_row_spec(tm, width, off // width), _fix_spec((1, width)), _row_spec(tm, width)],
        out_specs=[_row_spec(tm, width), _fix_spec((1, width))],
        out_shape=[jax.ShapeDtypeStruct((S, width), F32), jax.ShapeDtypeStruct((1, width), F32)],
        compiler_params=_params(("arbitrary",)))(h, g.reshape(1, width), dxn)


def _q_proj(xn, w, rc, rs):
    S = xn.shape[0]
    tm = _tile(S, 2048)

    def body(x_ref, w_ref, c_ref, s_ref, q_ref):
        q = _dot(x_ref[...], w_ref[...]) * MLA_SCALE
        q_ref[:, :LANE] = q[:, :LANE].astype(BF16)
        q_ref[:, LANE:] = _rope(q[:, LANE:], c_ref[...], s_ref[...]).astype(BF16)

    return pl.pallas_call(
        body, name="q_proj", grid=(S // tm, MLA_HEADS),
        in_specs=[pl.BlockSpec((tm, 512), lambda i, j: (i, 0)), pl.BlockSpec((512, 256), lambda i, j: (0, j)),
                  pl.BlockSpec((tm, LANE), lambda i, j: (i, 0)), pl.BlockSpec((tm, LANE), lambda i, j: (i, 0))],
        out_specs=pl.BlockSpec((tm, 256), lambda i, j: (i, j)),
        out_shape=jax.ShapeDtypeStruct((S, MLA_HEADS * 256), BF16),
        compiler_params=_params(("parallel", "parallel")))(xn, w, rc, rs)


def _kv_proj(xn, w, h, rc, rs):
    S = xn.shape[0]
    tm = _tile(S, 2048)

    def body(x_ref, w_ref, pe_ref, c_ref, s_ref, k_ref, v_ref):
        kv = _dot(x_ref[...], w_ref[...])
        k_ref[:, :LANE] = kv[:, :LANE].astype(BF16)
        k_ref[:, LANE:] = _rope(pe_ref[...], c_ref[...], s_ref[...]).astype(BF16)
        v_ref[...] = kv[:, LANE:].astype(BF16)

    return pl.pallas_call(
        body, name="kv_proj", grid=(S // tm, MLA_HEADS),
        in_specs=[pl.BlockSpec((tm, 256), lambda i, j: (i, 0)), pl.BlockSpec((256, 256), lambda i, j: (0, j)),
                  pl.BlockSpec((tm, LANE), lambda i, j: (i, KPE // LANE)),
                  pl.BlockSpec((tm, LANE), lambda i, j: (i, 0)), pl.BlockSpec((tm, LANE), lambda i, j: (i, 0))],
        out_specs=[pl.BlockSpec((tm, 256), lambda i, j: (i, j)), pl.BlockSpec((tm, LANE), lambda i, j: (i, j))],
        out_shape=[jax.ShapeDtypeStruct((S, MLA_HEADS * 256), BF16), jax.ShapeDtypeStruct((S, MLA_HEADS * LANE), BF16)],
        compiler_params=_params(("parallel", "parallel")))(xn, w, h, rc, rs)


def _kv_bwd_prep(dk, dv, rc, rs):
    S = dk.shape[1]
    tm = _tile(S, 1024)

    def body(dk_ref, dv_ref, c_ref, s_ref, o_ref, pe_ref):
        rot = jnp.zeros((tm, LANE), F32)
        for hh in range(MLA_HEADS):
            o_ref[:, hh * 256:hh * 256 + LANE] = dk_ref[hh, :, :LANE].astype(BF16)
            o_ref[:, hh * 256 + LANE:(hh + 1) * 256] = dv_ref[hh].astype(BF16)
            rot = rot + dk_ref[hh, :, LANE:]
        pe_ref[...] = _rope_bwd(rot, c_ref[...], s_ref[...])

    return pl.pallas_call(
        body, name="kv_bwd_prep", grid=(S // tm,),
        in_specs=[pl.BlockSpec((MLA_HEADS, tm, 256), lambda i: (0, i, 0)),
                  pl.BlockSpec((MLA_HEADS, tm, LANE), lambda i: (0, i, 0)), _row_spec(tm, LANE), _row_spec(tm, LANE)],
        out_specs=[_row_spec(tm, MLA_HEADS * 256), _row_spec(tm, LANE)],
        out_shape=[jax.ShapeDtypeStruct((S, MLA_HEADS * 256), BF16), jax.ShapeDtypeStruct((S, LANE), F32)],
        compiler_params=_params(("parallel",)))(dk, dv, rc, rs)


def _chunk_mask(T):
    row = lax.broadcasted_iota(jnp.int32, (T, T), 0)
    col = lax.broadcasted_iota(jnp.int32, (T, T), 1)
    return (col // CHUNK) <= (row // CHUNK)


def _att_blocks(S, q_blk=None, k_blk=None):
    tq = min(q_blk or Q_BLK, S)
    tk = min(k_blk or K_BLK, tq)
    return tq, tk, tq // tk


def _tail_masks(rows, tk):
    row = lax.broadcasted_iota(jnp.int32, (rows, tk), 0)
    col = lax.broadcasted_iota(jnp.int32, (rows, tk), 1)
    return (col // CHUNK) <= (row // CHUNK), col < row


def _span_masks(tk, r):
    row = lax.broadcasted_iota(jnp.int32, (tk, (r + 1) * tk), 0) + r * tk
    col = lax.broadcasted_iota(jnp.int32, (tk, (r + 1) * tk), 1)
    return (col // CHUNK) <= (row // CHUNK), col < row


def _put_rows(old, new, r0):
    return new if r0 == 0 else jnp.concatenate([old[:r0], new], axis=0)


def _mla_fwd(q, kp, v, nxt=None):
    S = q.shape[0]
    TQ, TK, n = _att_blocks(S, None, MLA_FWD_K_BLK)
    nq = S // TQ
    na = len(nxt) if nxt else 0

    def ride_along(srcs, outs, send_sems, recv_sems):
        x, y, c, chips = _place()

        def copy(a, j, px, py, chip):
            return pltpu.make_async_remote_copy(
                src_ref=srcs[a], dst_ref=outs[a].at[chip], send_sem=send_sems.at[3 * a + j],
                recv_sem=recv_sems.at[3 * a + j], device_id=(px, py, c), device_id_type=MESH)

        first = jnp.logical_and(pl.program_id(0) == 0, pl.program_id(1) == 0)
        last = jnp.logical_and(pl.program_id(0) == MLA_HEADS - 1, pl.program_id(1) == nq - 1)

        @pl.when(first)
        def _():
            for a in range(na):
                for j, (px, py) in enumerate(chips):
                    copy(a, j, px, py, 2 * x + y).start()

        @pl.when(last)
        def _():
            for a in range(na):
                for j, (px, py) in enumerate(chips):
                    copy(a, j, px, py, 2 * x + y).wait_send()
                    copy(a, j, px, py, 2 * px + py).wait_recv()

    def body(q_ref, k_ref, v_ref, *rest):
        if na:
            o_ref, lse_ref = rest[na:na + 2]
            ride_along(rest[:na], rest[na + 2:2 * na + 2], *rest[2 * na + 2:])
        else:
            o_ref, lse_ref = rest
        i = pl.program_id(1)

        def update(carry, qb, keys, mask):
            m, l, acc = carry
            s = _dot_nt(qb, k_ref[keys, :])
            if mask is not None:
                s = jnp.where(mask, s, -1e30)
            m_new = jnp.maximum(m, jnp.max(s, axis=1, keepdims=True))
            a = jnp.exp(m - m_new)
            p = jnp.exp(s - m_new)
            return m_new, a * l + jnp.sum(p, axis=1, keepdims=True), a * acc + _dot(p.astype(BF16), v_ref[keys, :])

        carry = (jnp.full((TQ, 1), -1e30, F32), jnp.zeros((TQ, 1), F32), jnp.zeros((TQ, LANE), F32))
        carry = lax.fori_loop(
            0, i * n, lambda j, c: update(c, q_ref[...], pl.ds(pl.multiple_of(j * TK, TK), TK), None), carry)
        for r in range(n):
            rows = slice(r * TK, (r + 1) * TK)
            m, l, acc = update(tuple(c[rows] for c in carry), q_ref[rows, :],
                               pl.ds(pl.multiple_of(i * TQ, TQ), (r + 1) * TK), _span_masks(TK, r)[0])
            o_ref[rows, :] = acc / l
            lse_ref[rows, :] = jnp.broadcast_to(m + jnp.log(l), (TK, LANE))

    res = pl.pallas_call(
        body, name="mla_fwd_gather" if na else "mla_fwd", grid=(MLA_HEADS, nq),
        in_specs=[pl.BlockSpec((TQ, 256), lambda h, i: (i, h)),
                  pl.BlockSpec((S, 256), lambda h, i: (0, h), pipeline_mode=pl.Buffered(1)),
                  pl.BlockSpec((S, LANE), lambda h, i: (0, h), pipeline_mode=pl.Buffered(1))] + [HBM_SPEC] * na,
        out_specs=[pl.BlockSpec((TQ, LANE), lambda h, i: (i, h)), pl.BlockSpec((TQ, LANE), lambda h, i: (i, h))]
                  + [HBM_SPEC] * na,
        out_shape=[jax.ShapeDtypeStruct((S, MLA_HEADS * LANE), F32), jax.ShapeDtypeStruct((S, MLA_HEADS * LANE), F32)]
                  + [jax.ShapeDtypeStruct((4,) + s.shape, s.dtype) for s in (nxt or [])],
        scratch_shapes=[pltpu.SemaphoreType.DMA((3 * na,)), pltpu.SemaphoreType.DMA((3 * na,))] if na else [],
        compiler_params=pltpu.CompilerParams(dimension_semantics=("arbitrary", "arbitrary"), vmem_limit_bytes=VMEM_LIMIT,
                                             has_side_effects=bool(na)))(q, kp, v, *(nxt or []))
    return res[0], res[1], (list(res[2:]) if na else None)


def _mla_bwd(q, kp, v, do_cat, o_cat, lse, rc, rs, ride=None):
    S = q.shape[0]
    TQ, TK, n = _att_blocks(S)
    nq = S // TQ
    na = len(ride) if ride else 0

    def body(q_ref, k_ref, v_ref, do_ref, o_ref, lse_ref, c_ref, s_ref, *rest):
        dq_ref, dk_hbm, dv_hbm = rest[na:na + 3]
        dk_acc, dv_acc = rest[2 * na + 3:2 * na + 5]
        h, i = pl.program_id(0), pl.program_id(1)
        if na:
            @pl.when(jnp.logical_and(h == 0, i == 0))
            def _():
                for cp in _exchange_copies(rest[:na], rest[na + 3:2 * na + 3], *rest[2 * na + 5:]):
                    cp.start()

            @pl.when(jnp.logical_and(h == MLA_HEADS - 1, i == nq - 1))
            def _():
                for cp in _exchange_copies(rest[:na], rest[na + 3:2 * na + 3], *rest[2 * na + 5:]):
                    cp.wait()

        @pl.when(i == 0)
        def _():
            dk_acc[...] = jnp.zeros_like(dk_acc)
            dv_acc[...] = jnp.zeros_like(dv_acc)

        do32 = do_ref[...]
        dob = do32.astype(BF16)
        delta = jnp.sum(do32 * o_ref[...], axis=1, keepdims=True)
        lse_col = lse_ref[:, :1]

        def blk(j, dq, r0, masked):
            sl = pl.ds(pl.multiple_of(j * TK, TK), TK)
            kb, vb, qb = k_ref[sl, :], v_ref[sl, :], q_ref[r0:, :]
            s = _dot_nt(qb, kb)
            if masked:
                s = jnp.where(_tail_masks(TQ - r0, TK)[0], s, -1e30)
            p = jnp.exp(s - lse_col[r0:])
            ds = (p * (_dot_nt(dob[r0:], vb) - delta[r0:])).astype(BF16)
            dk_acc[sl, :] += _dot_tn(ds, qb)
            dv_acc[sl, :] += _dot_tn(p.astype(BF16), dob[r0:])
            return _put_rows(dq, dq[r0:] + _dot(ds, kb), r0)

        dq = lax.fori_loop(0, i * n, lambda j, c: blk(j, c, 0, False), jnp.zeros((TQ, 256), F32))
        for t in range(n):
            dq = blk(i * n + t, dq, t * TK, True)
        dq_ref[:, :LANE] = (dq[:, :LANE] * MLA_SCALE).astype(BF16)
        dq_ref[:, LANE:] = _rope_bwd(dq[:, LANE:] * MLA_SCALE, c_ref[...], s_ref[...]).astype(BF16)

        @pl.when(i == nq - 1)
        def _():
            pltpu.sync_copy(dk_acc, dk_hbm.at[h])
            pltpu.sync_copy(dv_acc, dv_hbm.at[h])

    any_spec = pl.BlockSpec(memory_space=pl.ANY)
    T = TQ
    rows = pl.BlockSpec((T, LANE), lambda h, i: (i, 0))
    res = pl.pallas_call(
        body, name="mla_bwd_exchange" if na else "mla_bwd", grid=(MLA_HEADS, nq),
        in_specs=[pl.BlockSpec((T, 256), lambda h, i: (i, h)),
                  pl.BlockSpec((S, 256), lambda h, i: (0, h), pipeline_mode=pl.Buffered(1)),
                  pl.BlockSpec((S, LANE), lambda h, i: (0, h), pipeline_mode=pl.Buffered(1)),
                  pl.BlockSpec((T, LANE), lambda h, i: (i, h)),
                  pl.BlockSpec((T, LANE), lambda h, i: (i, h)), pl.BlockSpec((T, LANE), lambda h, i: (i, h)), rows, rows]
                 + [HBM_SPEC] * na,
        out_specs=[pl.BlockSpec((T, 256), lambda h, i: (i, h)), any_spec, any_spec] + [HBM_SPEC] * na,
        out_shape=[jax.ShapeDtypeStruct((S, MLA_HEADS * 256), BF16), jax.ShapeDtypeStruct((MLA_HEADS, S, 256), F32),
                   jax.ShapeDtypeStruct((MLA_HEADS, S, LANE), F32)]
                  + [jax.ShapeDtypeStruct((3,) + p.shape[1:], p.dtype) for p in (ride or [])],
        scratch_shapes=[pltpu.VMEM((S, 256), F32), pltpu.VMEM((S, LANE), F32)]
                       + ([pltpu.SemaphoreType.DMA((3 * na,)), pltpu.SemaphoreType.DMA((3 * na,))] if na else []),
        compiler_params=pltpu.CompilerParams(dimension_semantics=("arbitrary", "arbitrary"), vmem_limit_bytes=VMEM_LIMIT,
                                             has_side_effects=bool(na)))(q, kp, v, do_cat, o_cat, lse, rc, rs, *(ride or []))
    return res[0], res[1], res[2], (list(res[3:]) if na else None)


def _split_dot(x, tri):
    top = lax.bitcast_convert_type(lax.bitcast_convert_type(x, jnp.uint32) & jnp.uint32(0xFFFF0000), F32)
    return _dot(top.astype(BF16), tri) + _dot((x - top).astype(BF16), tri)


def _sb_block(qb, kb, tri, carry, masked):
    z = _dot_nt(qb, kb)
    lb = jnp.minimum(z, 0.0) - jnp.log(1.0 + jnp.exp(-jnp.abs(z)))
    lm = lb - z
    strict = None
    if masked:
        strict = _tail_masks(z.shape[0], z.shape[1])[1]
        lm = jnp.where(strict, lm, 0.0)
    a = jnp.exp(lb + carry + _split_dot(lm, tri))
    if masked:
        a = jnp.where(strict, a, 0.0)
    return a, lb, lm, strict


def _sb_walk(blk, j0, state):
    def alive(c):
        return jnp.logical_and(c[0] >= 0, jnp.max(c[1][0]) > SB_DEAD)

    return lax.while_loop(alive, lambda c: (c[0] - 1, blk(c[0], c[1], 0, False)), (j0, state))[1]


def _triangle(tk):
    row = lax.broadcasted_iota(jnp.int32, (tk, tk), 0)
    col = lax.broadcasted_iota(jnp.int32, (tk, tk), 1)
    return (row > col).astype(BF16)


def _sb_fwd(qkv):
    S = qkv.shape[0]
    TQ, TK, n = _att_blocks(S, SB_Q_BLK, SB_K_BLK)
    T = TQ

    def body(q_ref, k_ref, v_ref, o_ref):
        i = pl.program_id(1)
        tri = _triangle(TK)

        def blk(j, state, r0, masked):
            carry, acc = (c[r0:] for c in state)
            sl = pl.ds(pl.multiple_of(j * TK, TK), TK)
            a, _, lm, _ = _sb_block(q_ref[r0:, :], k_ref[sl, :], tri, carry, masked)
            new = (carry + jnp.sum(lm, axis=1, keepdims=True), acc + _dot(a.astype(BF16), v_ref[sl, :]))
            return tuple(_put_rows(c, u, r0) for c, u in zip(state, new))

        state = (jnp.zeros((TQ, 1), F32), jnp.zeros((TQ, LANE), F32))
        for t in reversed(range(n)):
            state = blk(i * n + t, state, t * TK, True)
        state = _sb_walk(blk, i * n - 1, state)
        o_ref[...] = state[1]

    return pl.pallas_call(
        body, name="sb_fwd", grid=(SB_HEADS, S // T),
        in_specs=[pl.BlockSpec((T, LANE), lambda h, i: (i, h)), pl.BlockSpec((S, LANE), lambda h, i: (0, 4 + h)),
                  pl.BlockSpec((S, LANE), lambda h, i: (0, 8 + h))],
        out_specs=pl.BlockSpec((T, LANE), lambda h, i: (i, h)),
        out_shape=jax.ShapeDtypeStruct((S, SB_HEADS * LANE), F32),
        compiler_params=_params(("parallel", "arbitrary")))(qkv, qkv, qkv)


def _sb_bwd(qkv, do_cat, o_cat, col0):
    S = qkv.shape[0]
    TQ, TK, n = _att_blocks(S, SB_Q_BLK, SB_K_BLK)
    T = TQ
    nq = S // TQ

    def body(q_ref, k_ref, v_ref, do_ref, o_ref, dq_ref, dk_hbm, dv_hbm, dk_acc, dv_acc):
        h, i = pl.program_id(0), pl.program_id(1)

        @pl.when(i == 0)
        def _():
            dk_acc[...] = jnp.zeros_like(dk_acc)
            dv_acc[...] = jnp.zeros_like(dv_acc)

        dob = do_ref[...].astype(BF16)
        tri = _triangle(TK)
        rest0 = jnp.sum(dob.astype(F32) * o_ref[...], axis=1, keepdims=True)

        def blk(j, state, r0, masked):
            carry, rest, dq = (c[r0:] for c in state)
            sl = pl.ds(pl.multiple_of(j * TK, TK), TK)
            kb, vb, qb = k_ref[sl, :], v_ref[sl, :], q_ref[r0:, :]
            a, lb, lm, strict = _sb_block(qb, kb, tri, carry, masked)
            ab = a.astype(BF16)
            e = ab.astype(F32) * _dot_nt(dob[r0:], vb)
            dz = e - jnp.exp(lb) * (rest - _split_dot(e, tri))
            if masked:
                dz = jnp.where(strict, dz, 0.0)
            dzb = dz.astype(BF16)
            dk_acc[sl, :] += _dot_tn(dzb, qb)
            dv_acc[sl, :] += _dot_tn(ab, dob[r0:])
            new = (carry + jnp.sum(lm, axis=1, keepdims=True), rest - jnp.sum(e, axis=1, keepdims=True),
                   dq + _dot(dzb, kb))
            return tuple(_put_rows(c, u, r0) for c, u in zip(state, new))

        state = (jnp.zeros((TQ, 1), F32), rest0, jnp.zeros((TQ, LANE), F32))
        for t in reversed(range(n)):
            state = blk(i * n + t, state, t * TK, True)
        state = _sb_walk(blk, i * n - 1, state)
        dq_ref[...] = state[2] * SB_SCALE

        @pl.when(i == nq - 1)
        def _():
            lanes = pl.ds(pl.multiple_of(h * LANE, LANE), LANE)
            pltpu.sync_copy(dk_acc, dk_hbm.at[:, lanes])
            pltpu.sync_copy(dv_acc, dv_hbm.at[:, lanes])

    any_spec = pl.BlockSpec(memory_space=pl.ANY)
    return pl.pallas_call(
        body, name="sb_bwd", grid=(SB_HEADS, nq),
        in_specs=[pl.BlockSpec((T, LANE), lambda h, i: (i, h)), pl.BlockSpec((S, LANE), lambda h, i: (0, 4 + h)),
                  pl.BlockSpec((S, LANE), lambda h, i: (0, 8 + h)),
                  pl.BlockSpec((T, LANE), lambda h, i: (i, col0 + h)), pl.BlockSpec((T, LANE), lambda h, i: (i, col0 + h))],
        out_specs=[pl.BlockSpec((T, LANE), lambda h, i: (i, h)), any_spec, any_spec],
        out_shape=[jax.ShapeDtypeStruct((S, SB_HEADS * LANE), F32)] * 3,
        scratch_shapes=[pltpu.VMEM((S, LANE), F32), pltpu.VMEM((S, LANE), F32)],
        compiler_params=_params(("arbitrary", "arbitrary")))(qkv, qkv, qkv, do_cat, o_cat)


def _mem_probs(q, k_ref, hh):
    lane = lax.broadcasted_iota(jnp.int32, (1, 256), 1) // 64
    msk = lane == hh
    qh = jnp.where(msk, q, 0.0).astype(BF16)
    s = _dot_nt(qh, k_ref[...]) * MEM_SCALE
    p = jnp.exp(s - jnp.max(s, axis=1, keepdims=True))
    return msk, qh, p / jnp.sum(p, axis=1, keepdims=True)


def _mem_fwd(h, mk, mv):
    S = h.shape[0]
    tm = _tile(S, 512)

    def body(q_ref, k_ref, v_ref, o_ref):
        q = q_ref[...]
        out = jnp.zeros((tm, 256), F32)
        for hh in range(MEM_HEADS):
            msk, _, p = _mem_probs(q, k_ref, hh)
            out = out + jnp.where(msk, _dot(p.astype(BF16), v_ref[...]), 0.0)
        o_ref[...] = out

    return pl.pallas_call(
        body, name="mem_fwd", grid=(S // tm,),
        in_specs=[_row_spec(tm, 256, MQ // 256), _fix_spec((256, 256)), _fix_spec((256, 256))],
        out_specs=_row_spec(tm, 256), out_shape=jax.ShapeDtypeStruct((S, 256), F32),
        compiler_params=_params(("parallel",)))(h, mk, mv)


def _mem_bwd(h, mk, mv, do_cat, col0):
    S = h.shape[0]
    tm = _tile(S, 512)

    def body(q_ref, k_ref, v_ref, do_ref, dq_ref, dk_ref, dv_ref):
        @pl.when(pl.program_id(0) == 0)
        def _():
            dk_ref[...] = jnp.zeros_like(dk_ref)
            dv_ref[...] = jnp.zeros_like(dv_ref)

        q, do = q_ref[...], do_ref[...]
        dq = jnp.zeros((tm, 256), F32)
        for hh in range(MEM_HEADS):
            msk, qh, p = _mem_probs(q, k_ref, hh)
            doh = jnp.where(msk, do, 0.0).astype(BF16)
            dp = _dot_nt(doh, v_ref[...])
            ds = (p * (dp - jnp.sum(p * dp, axis=1, keepdims=True)) * MEM_SCALE).astype(BF16)
            dq = dq + jnp.where(msk, _dot(ds, k_ref[...]), 0.0)
            dk_ref[...] += _dot_tn(ds, qh)
            dv_ref[...] += _dot_tn(p.astype(BF16), doh)
        dq_ref[...] = dq

    return pl.pallas_call(
        body, name="mem_bwd", grid=(S // tm,),
        in_specs=[_row_spec(tm, 256, MQ // 256), _fix_spec((256, 256)), _fix_spec((256, 256)),
                  _row_spec(tm, 256, col0 // 256)],
        out_specs=[_row_spec(tm, 256), _fix_spec((256, 256)), _fix_spec((256, 256))],
        out_shape=[jax.ShapeDtypeStruct((S, 256), F32), jax.ShapeDtypeStruct((256, 256), F32),
                   jax.ShapeDtypeStruct((256, 256), F32)],
        compiler_params=_params(("arbitrary",)))(h, mk, mv, do_cat)


SG_T = 128


def _sg_norm(sv, g, b):
    gv = _gelu(sv)
    xc = gv - jnp.mean(gv, axis=1, keepdims=True)
    rstd = lax.rsqrt(jnp.mean(xc * xc, axis=1, keepdims=True) + LN_EPS)
    xhat = xc * rstd
    return xhat, rstd, xhat * g + b


def _sg_fwd(h, lng, lnb, w, bias_t):
    S = h.shape[0]
    tm = _tile(S, 512)

    def body(u_ref, v_ref, g_ref, b_ref, w_ref, bias_ref, o_ref):
        mask = _chunk_mask(SG_T)
        for n in range(tm // SG_T):
            rows = slice(n * SG_T, (n + 1) * SG_T)
            u = _gelu(u_ref[rows, :])
            _, _, vn = _sg_norm(v_ref[rows, :], g_ref[...], b_ref[...])
            vb = vn.astype(BF16)
            for gi in range(4):
                cols = slice(gi * LANE, (gi + 1) * LANE)
                wg = jnp.where(mask, w_ref[gi], 0.0).astype(BF16)
                mixed = _dot(wg, vb[:, cols]) + bias_ref[:, gi:gi + 1]
                o_ref[rows, cols] = u[:, cols] * mixed

    return pl.pallas_call(
        body, name="sg_fwd", grid=(S // tm,),
        in_specs=[_row_spec(tm, 512, SGU // 512), _row_spec(tm, 512, SGV // 512), _fix_spec((1, 512)),
                  _fix_spec((1, 512)), _fix_spec((4, SG_T, SG_T)), _fix_spec((SG_T, 4))],
        out_specs=_row_spec(tm, 512), out_shape=jax.ShapeDtypeStruct((S, 512), F32),
        compiler_params=_params(("parallel",)))(h, h, lng.reshape(1, 512), lnb.reshape(1, 512), w, bias_t)


def _sg_bwd(h, lng, lnb, w, bias_t, do_cat, col0):
    S = h.shape[0]
    tm = _tile(S, 512)
    nsteps = S // tm

    def body(u_ref, v_ref, g_ref, b_ref, w_ref, bias_ref, do0_ref, do1_ref, do2_ref, do3_ref,
             du_ref, dv_ref, dw_ref, dbias_ref, dg_ref, db_ref, dvn_scr, dbias_acc):
        do_refs = (do0_ref, do1_ref, do2_ref, do3_ref)
        step = pl.program_id(0)

        @pl.when(step == 0)
        def _():
            dw_ref[...] = jnp.zeros_like(dw_ref)
            dg_ref[...] = jnp.zeros_like(dg_ref)
            db_ref[...] = jnp.zeros_like(db_ref)
            dbias_acc[...] = jnp.zeros_like(dbias_acc)

        mask = _chunk_mask(SG_T)
        for n in range(tm // SG_T):
            rows = slice(n * SG_T, (n + 1) * SG_T)
            su, sv = u_ref[rows, :], v_ref[rows, :]
            u = _gelu(su)
            xhat, rstd, vn = _sg_norm(sv, g_ref[...], b_ref[...])
            vb = vn.astype(BF16)
            ugrad = _gelu_grad(su)
            for gi in range(4):
                cols = slice(gi * LANE, (gi + 1) * LANE)
                do = do_refs[gi][rows, :]
                wg = jnp.where(mask, w_ref[gi], 0.0).astype(BF16)
                mixed = _dot(wg, vb[:, cols]) + bias_ref[:, gi:gi + 1]
                dmixed = do * u[:, cols]
                dmb = dmixed.astype(BF16)
                du_ref[rows, cols] = do * mixed * ugrad[:, cols]
                dvn_scr[:, cols] = _dot_tn(wg, dmb)
                dw_ref[gi] += jnp.where(mask, _dot_nt(dmb, vb[:, cols]), 0.0)
                dbias_acc[gi] += dmixed
            dvn = dvn_scr[...]
            dg_ref[...] += jnp.sum(dvn * xhat, axis=0, keepdims=True)
            db_ref[...] += jnp.sum(dvn, axis=0, keepdims=True)
            dxh = dvn * g_ref[...]
            dgv = rstd * (dxh - jnp.mean(dxh, axis=1, keepdims=True)
                          - xhat * jnp.mean(dxh * xhat, axis=1, keepdims=True))
            dv_ref[rows, :] = dgv * _gelu_grad(sv)

        @pl.when(step == nsteps - 1)
        def _():
            for gi in range(4):
                dbias_ref[:, gi:gi + 1] = jnp.sum(dbias_acc[gi], axis=1, keepdims=True)

    return pl.pallas_call(
        body, name="sg_bwd", grid=(nsteps,),
        in_specs=[_row_spec(tm, 512, SGU // 512), _row_spec(tm, 512, SGV // 512), _fix_spec((1, 512)),
                  _fix_spec((1, 512)), _fix_spec((4, SG_T, SG_T)), _fix_spec((SG_T, 4))]
                 + [_row_spec(tm, LANE, col0 // LANE + gi) for gi in range(4)],
        out_specs=[_row_spec(tm, 512), _row_spec(tm, 512), _fix_spec((4, SG_T, SG_T)), _fix_spec((SG_T, 4)),
                   _fix_spec((1, 512)), _fix_spec((1, 512))],
        out_shape=[jax.ShapeDtypeStruct((S, 512), F32), jax.ShapeDtypeStruct((S, 512), F32),
                   jax.ShapeDtypeStruct((4, SG_T, SG_T), F32), jax.ShapeDtypeStruct((SG_T, 4), F32),
                   jax.ShapeDtypeStruct((1, 512), F32), jax.ShapeDtypeStruct((1, 512), F32)],
        scratch_shapes=[pltpu.VMEM((SG_T, 512), F32), pltpu.VMEM((4, SG_T, SG_T), F32)],
        compiler_params=_params(("arbitrary",)))(h, h, lng.reshape(1, 512), lnb.reshape(1, 512), w, bias_t,
                                                 do_cat, do_cat, do_cat, do_cat)


def _gate_out_ln(branches, h, w_out, x, g, b):
    S = h.shape[0]
    tm = _tile(S, 256)
    widths = [a.shape[1] for a in branches]

    def body(oa_ref, ob_ref, oc_ref, om_ref, g0_ref, g1_ref, g2_ref, g3_ref, w_ref, x_ref, lg_ref, lb_ref,
             cat_ref, yg_ref, xo_ref, xb_ref, r_ref):
        at = 0
        for ref, width in zip((oa_ref, ob_ref, oc_ref, om_ref), widths):
            cat_ref[:, at:at + width] = ref[...]
            at += width
        for j, g_ref in enumerate((g0_ref, g1_ref, g2_ref, g3_ref)):
            gate = g_ref[...]
            cols = slice(j * 512, (j + 1) * 512)
            yg_ref[:, cols] = (cat_ref[:, cols] * (gate * jax.nn.sigmoid(gate))).astype(BF16)
        r = ALPHA * x_ref[...] + _dot(yg_ref[...], w_ref[...])
        r_ref[...] = r
        xc = r - jnp.mean(r, axis=1, keepdims=True)
        o = xc * lax.rsqrt(jnp.mean(xc * xc, axis=1, keepdims=True) + LN_EPS) * lg_ref[...] + lb_ref[...]
        xo_ref[...] = o
        xb_ref[...] = o.astype(BF16)

    row = _row_spec(tm, D_MODEL)
    return pl.pallas_call(
        body, name="gate_out_ln", grid=(S // tm,),
        in_specs=[_row_spec(tm, width) for width in widths] + [_row_spec(tm, 512, GATE // 512 + j) for j in range(4)]
                 + [_fix_spec((D_MODEL, D_MODEL)), row, _fix_spec((1, D_MODEL)), _fix_spec((1, D_MODEL))],
        out_specs=[row] * 5,
        out_shape=[jax.ShapeDtypeStruct((S, D_MODEL), t) for t in (F32, BF16, F32, BF16, F32)],
        compiler_params=_params(("parallel",)))(*branches, h, h, h, h, w_out, x, g.reshape(1, D_MODEL), b.reshape(1, D_MODEL))


def _out_proj_gate_bwd(dr, w_out, o_cat, h):
    S = h.shape[0]
    tm = _tile(S, 1024)

    def body(dr_ref, w_ref, o_ref, g_ref, do_ref, dg_ref, drb):
        @pl.when(pl.program_id(1) == 0)
        def _():
            drb[...] = dr_ref[...].astype(BF16)

        d = _dot_nt(drb[...], w_ref[...])
        g = g_ref[...]
        sig = jax.nn.sigmoid(g)
        do_ref[...] = d * (g * sig)
        dg_ref[...] = d * o_ref[...] * (sig * (1.0 + g * (1.0 - sig)))

    blk = pl.BlockSpec((tm, 512), lambda i, j: (i, j))
    return pl.pallas_call(
        body, name="d_out_proj_gate", grid=(S // tm, 4),
        in_specs=[pl.BlockSpec((tm, D_MODEL), lambda i, j: (i, 0)), pl.BlockSpec((512, D_MODEL), lambda i, j: (j, 0)),
                  blk, pl.BlockSpec((tm, 512), lambda i, j: (i, GATE // 512 + j))],
        out_specs=[blk, blk],
        out_shape=[jax.ShapeDtypeStruct((S, D_MODEL), F32), jax.ShapeDtypeStruct((S, D_MODEL), F32)],
        scratch_shapes=[pltpu.VMEM((tm, D_MODEL), BF16)],
        compiler_params=_params(("parallel", "arbitrary")))(dr, w_out, o_cat, h)


def _ln_res_bwd(dout, r, g):
    S = r.shape[0]
    tm = _tile(S, 512)

    def body(d_ref, r_ref, g_ref, dr_ref, dg_ref, db_ref):
        @pl.when(pl.program_id(0) == 0)
        def _():
            dg_ref[...] = jnp.zeros_like(dg_ref)
            db_ref[...] = jnp.zeros_like(db_ref)

        d, r = d_ref[...], r_ref[...]
        xc = r - jnp.mean(r, axis=1, keepdims=True)
        rstd = lax.rsqrt(jnp.mean(xc * xc, axis=1, keepdims=True) + LN_EPS)
        xhat = xc * rstd
        dxh = d * g_ref[...]
        dr_ref[...] = rstd * (dxh - jnp.mean(dxh, axis=1, keepdims=True)
                              - xhat * jnp.mean(dxh * xhat, axis=1, keepdims=True))
        dg_ref[...] += jnp.sum(d * xhat, axis=0, keepdims=True)
        db_ref[...] += jnp.sum(d, axis=0, keepdims=True)

    return pl.pallas_call(
        body, name="ln_res_bwd", grid=(S // tm,),
        in_specs=[_row_spec(tm, D_MODEL), _row_spec(tm, D_MODEL), _fix_spec((1, D_MODEL))],
        out_specs=[_row_spec(tm, D_MODEL), _fix_spec((1, D_MODEL)), _fix_spec((1, D_MODEL))],
        out_shape=[jax.ShapeDtypeStruct((S, D_MODEL), F32), jax.ShapeDtypeStruct((1, D_MODEL), F32),
                   jax.ShapeDtypeStruct((1, D_MODEL), F32)],
        compiler_params=_params(("arbitrary",)))(dout, r, g.reshape(1, D_MODEL))


def _loss_head(y, target):
    S = y.shape[0]
    tm = _tile(S, 512)

    def body(y_ref, t_ref, l_ref, d_ref):
        @pl.when(pl.program_id(0) == 0)
        def _():
            l_ref[...] = jnp.zeros_like(l_ref)

        diff = y_ref[...] - t_ref[...]
        d_ref[...] = diff * (1.0 / D_MODEL)
        per_row = jnp.mean(diff * diff, axis=1, keepdims=True)
        l_ref[...] += 0.5 * jnp.sum(per_row, axis=0, keepdims=True)

    return pl.pallas_call(
        body, name="loss_head", grid=(S // tm,), in_specs=[_row_spec(tm, D_MODEL), _row_spec(tm, D_MODEL)],
        out_specs=[_fix_spec((8, LANE)), _row_spec(tm, D_MODEL)],
        out_shape=[jax.ShapeDtypeStruct((8, LANE), F32), jax.ShapeDtypeStruct((S, D_MODEL), F32)],
        compiler_params=_params(("arbitrary",)))(y, target)


def _perm_table():
    table, at = [], 0
    for name in PERM_ORDER:
        start, width = ORIG[name]
        table.append((name, start, width, at))
        at += width
    return table


def _permute_w_in(by_chip):
    wc = by_chip.shape[-1]
    parts = []
    for _, start, width, _ in _perm_table():
        lo = start
        while lo < start + width:
            k = lo // wc
            hi = min(start + width, (k + 1) * wc)
            parts.append(by_chip[k, ..., lo - k * wc:hi - k * wc])
            lo = hi
    parts.append(jnp.zeros(by_chip.shape[1:-1] + (HP - D_IN,), by_chip.dtype))
    return jnp.concatenate(parts, axis=-1)


def _model_cols(wp, lo, hi):
    parts = []
    for _, start, width, at in sorted(_perm_table(), key=lambda t: t[1]):
        a, b = max(lo, start), min(hi, start + width)
        if a < b:
            parts.append(wp[..., at + a - start:at + b - start])
    return jnp.concatenate(parts, axis=-1)


def _rope_tables(positions):
    inv_freq = ROPE_THETA ** (-jnp.arange(0, 64, 2, dtype=F32) / 64)
    ang = positions.astype(F32)[:, None] * inv_freq[None, :]
    cos, sin, zero = jnp.cos(ang), jnp.sin(ang), jnp.zeros((positions.shape[0], 64), F32)
    return jnp.concatenate([cos, cos, zero], axis=1), jnp.concatenate([-sin, sin, zero], axis=1)


def _layer_weights(by_chip):
    w_in, w_uq, w_ukv, w_mem_k, w_mem_v, w_out = by_chip
    w_uq = jnp.concatenate([w_uq[k] for k in range(4)], axis=1)
    w_uq = jnp.pad(w_uq.reshape(512, MLA_HEADS, 192), ((0, 0), (0, 0), (0, 64))).reshape(512, MLA_HEADS * 256)
    return (_permute_w_in(w_in), w_uq, jnp.concatenate([w_ukv[k] for k in range(4)], axis=1),
            w_mem_k.reshape(D_MODEL, 256), w_mem_v.reshape(D_MODEL, 256), w_out.reshape(D_MODEL, D_MODEL))


def _local_step(x, mem, positions, target, w, layer_source, next_shards, early_reduce=None, late_reduce=None):
    rc, rs = _rope_tables(positions)
    mem_b = mem.astype(BF16)
    xb = x.astype(BF16)
    saved = []
    fetched = None
    for l in range(DEPTH):
        w_in, w_uq, w_ukv, w_mem_k, w_mem_v, w_out = layer_source(l, fetched)
        h = _mm(xb, w_in, tm=1024, tn=1152, tk=2048, name="in_proj")
        cq_n = _rms_fwd(h, CQ, 512, w["q_norm_g"][l], "rms_q")
        ckv_n = _rms_fwd(h, CKV, 256, w["kv_norm_g"][l], "rms_kv")
        q = _q_proj(cq_n, w_uq, rc, rs)
        kp, v = _kv_proj(ckv_n, w_ukv, h, rc, rs)
        o_a, lse, fetched = _mla_fwd(q, kp, v, next_shards(l))
        bias_t = w["sg_b"][l].T
        o_b = _sg_fwd(h, w["sg_ln_g"][l], w["sg_ln_b"][l], w["sg_w"][l], bias_t)
        qkv = jnp.concatenate([h[:, SBQ:SBQ + 512] * SB_SCALE, h[:, SBQ + 512:SBQ + 1536]], axis=1).astype(BF16)
        o_c = _sb_fwd(qkv)
        mk = _mm(mem_b, w_mem_k, out_dtype=BF16, name="mem_kv")
        mv = _mm(mem_b, w_mem_v, out_dtype=BF16, name="mem_kv")
        o_m = _mem_fwd(h, mk, mv)
        o_cat, yg, x_new, xb_new, r = _gate_out_ln((o_a, o_b, o_c, o_m), h, w_out, x, w["ln_g"][l], w["ln_b"][l])
        saved.append(dict(xb=xb, h=h, cq_n=cq_n, ckv_n=ckv_n, q=q, kp=kp, v=v, lse=lse, qkv=qkv, mk=mk, mv=mv,
                          o_cat=o_cat, yg=yg, r=r, w_in=w_in, w_uq=w_uq, w_ukv=w_ukv, w_out=w_out, bias_t=bias_t))
        x, xb = x_new, xb_new

    loss, dx = _loss_head(x, target)

    grads = {n: [None] * DEPTH for n in SHARDED + SMALL}
    early = late = None
    for l in reversed(range(DEPTH)):
        s = saved[l]
        h = s["h"]
        dr, dlg, dlb = _ln_res_bwd(dx, s["r"], w["ln_g"][l])
        grads["ln_g"][l], grads["ln_b"][l] = dlg[0], dlb[0]
        grads["w_out"][l] = _mm(s["yg"], dr, ta=True, tm=1024, tn=1024, tk=2048, name="dw_out")
        do_cat, dgates = _out_proj_gate_bwd(dr, s["w_out"], s["o_cat"], h)
        dmq, dmk, dmv = _mem_bwd(h, s["mk"], s["mv"], do_cat, 1792)
        grads["w_mem_k"][l] = _mm(mem_b, dmk, ta=True, name="dw_mem")
        grads["w_mem_v"][l] = _mm(mem_b, dmv, ta=True, name="dw_mem")
        dsq, dsk, dsv = _sb_bwd(s["qkv"], do_cat, s["o_cat"], 1280 // LANE)
        du, dv, dsgw, dsgb, dsg_g, dsg_b = _sg_bwd(h, w["sg_ln_g"][l], w["sg_ln_b"][l], w["sg_w"][l], s["bias_t"],
                                                   do_cat, 768)
        grads["sg_w"][l], grads["sg_b"][l] = dsgw, dsgb.T
        grads["sg_ln_g"][l], grads["sg_ln_b"][l] = dsg_g[0], dsg_b[0]
        ride = early_reduce(grads) if early_reduce and l == 0 else None
        dq_raw, dk, dvv, arrived = _mla_bwd(s["q"], s["kp"], s["v"], do_cat, s["o_cat"], s["lse"], rc, rs, ride)
        if ride:
            early = (ride, arrived)
        dkv, dkpe = _kv_bwd_prep(dk, dvv, rc, rs)
        dw_uq = _mm(s["cq_n"], dq_raw, ta=True, tn=1536, tk=2048, name="dw_uq")
        grads["w_uq"][l] = dw_uq.reshape(512, MLA_HEADS, 256)[:, :, :192].reshape(512, MLA_HEADS * 192)
        grads["w_ukv"][l] = _mm(s["ckv_n"], dkv, ta=True, tn=1536, tk=2048, name="dw_ukv")
        dcq_n = _mm(dq_raw, s["w_uq"], tb=True, tm=2048, tk=1536, name="d_cq")
        dckv_n = _mm(dkv, s["w_ukv"], tb=True, tm=2048, tk=1536, name="d_ckv")
        dcq, dqg = _rms_bwd(h, CQ, 512, w["q_norm_g"][l], dcq_n, "rms_q_bwd")
        dckv, dkvg = _rms_bwd(h, CKV, 256, w["kv_norm_g"][l], dckv_n, "rms_kv_bwd")
        grads["q_norm_g"][l], grads["kv_norm_g"][l] = dqg[0], dkvg[0]
        dh = jnp.concatenate([dcq, dckv, dmq, du, dv, dsq, dsk, dsv, dgates, dkpe], axis=1).astype(BF16)
        dw_in = _mm(s["xb"], dh, ta=True, tm=1024, tn=1152, tk=2048, name="dw_in")
        grads["w_in"][l] = dw_in
        ride = late_reduce(grads) if late_reduce and l == 0 else None
        dx = _mm(dh, s["w_in"], tb=True, add=dr, add_scale=ALPHA, tm=1024, tn=1024, tk=1920, name="d_in_proj", ride=ride)
        if ride:
            dx, arrived = dx
            late = (ride, arrived)

    return loss, dx, grads, early, late


MESH = pl.DeviceIdType.MESH
HBM_SPEC = pl.BlockSpec(memory_space=pltpu.HBM)


def _place():
    x, y, c = lax.axis_index("x"), lax.axis_index("y"), lax.axis_index("c")
    return x, y, c, [(1 - x, y), (x, 1 - y), (1 - x, 1 - y)]


def _comm_call(body, name, arrays, out_shapes, n_sems):
    return pl.pallas_call(
        body, name=name, in_specs=[HBM_SPEC] * len(arrays), out_specs=[HBM_SPEC] * len(out_shapes), out_shape=out_shapes,
        scratch_shapes=[pltpu.SemaphoreType.DMA((n_sems,)), pltpu.SemaphoreType.DMA((n_sems,))],
        compiler_params=pltpu.CompilerParams(has_side_effects=True))(*arrays)


def _gather_weights(shards):
    na = len(shards)

    def body(*refs):
        srcs, outs, (send_sems, recv_sems) = refs[:na], refs[na:2 * na], refs[2 * na:]
        x, y, c, chips = _place()
        mine = [pl.ds((s.shape[0] // 2) * c, s.shape[0] // 2) for s in shards]
        theirs = [pl.ds((s.shape[0] // 2) * (1 - c), s.shape[0] // 2) for s in shards]

        def copy(a, k, src_ref, chip, part, to):
            return pltpu.make_async_remote_copy(
                src_ref=src_ref, dst_ref=outs[a].at[chip, part], send_sem=send_sems.at[6 * a + k],
                recv_sem=recv_sems.at[6 * a + k], device_id=to, device_id_type=MESH)

        sent = [copy(a, j, srcs[a].at[mine[a]], 2 * x + y, mine[a], (px, py, c))
                for a in range(na) for j, (px, py) in enumerate(chips)]
        for cp in sent:
            cp.start()
        passed = []
        for j, (px, py) in enumerate(chips):
            for a in range(na):
                copy(a, j, srcs[a].at[mine[a]], 2 * px + py, mine[a], (px, py, c)).wait_recv()
                cp = copy(a, 3 + j, outs[a].at[2 * px + py, mine[a]], 2 * px + py, mine[a], (x, y, 1 - c))
                cp.start()
                passed.append(cp)
        for j, (px, py) in enumerate(chips):
            for a in range(na):
                copy(a, 3 + j, srcs[a].at[theirs[a]], 2 * px + py, theirs[a], (x, y, 1 - c)).wait_recv()
        for cp in sent + passed:
            cp.wait_send()

    return _comm_call(body, "gather_weights", shards, [jax.ShapeDtypeStruct((4,) + s.shape, s.dtype) for s in shards], 6 * na)


def _swap_halves(gs):
    na = len(gs)

    def body(*refs):
        srcs, outs, (send_sems, recv_sems) = refs[:na], refs[na:2 * na], refs[2 * na:]
        x, y, c, _ = _place()
        cps = [pltpu.make_async_remote_copy(
            src_ref=srcs[a].at[:, :, pl.ds((gs[a].shape[2] // 2) * (1 - c), gs[a].shape[2] // 2)], dst_ref=outs[a],
            send_sem=send_sems.at[a], recv_sem=recv_sems.at[a], device_id=(x, y, 1 - c), device_id_type=MESH)
            for a in range(na)]
        for cp in cps:
            cp.start()
        for cp in cps:
            cp.wait()

    return _comm_call(body, "swap_halves", gs,
                      [jax.ShapeDtypeStruct(g.shape[:2] + (g.shape[2] // 2, g.shape[3]), g.dtype) for g in gs], na)


def _pair_sum(g, other, c):
    _, L, R, C = g.shape
    tr = _row_tile(R // 2, 3 * C * 4)
    nb = R // 2 // tr

    def body(c_ref, a_ref, b_ref, o_ref):
        o_ref[...] = (a_ref[...] + b_ref[...]).astype(BF16)

    blk = pl.BlockSpec((None, None, tr, C), lambda d, l, i, c_ref: (d, l, i, 0))
    return pl.pallas_call(
        body, name="pair_sum",
        grid_spec=pltpu.PrefetchScalarGridSpec(
            num_scalar_prefetch=1, grid=(4, L, nb),
            in_specs=[pl.BlockSpec((None, None, tr, C), lambda d, l, i, c_ref: (d, l, nb * c_ref[0] + i, 0)), blk],
            out_specs=blk),
        out_shape=jax.ShapeDtypeStruct((4, L, R // 2, C), BF16),
        compiler_params=_params(("parallel", "parallel", "parallel")))(c, g, other)


def _exchange_copies(srcs, outs, send_sems, recv_sems):
    x, y, c, chips = _place()
    return [pltpu.make_async_remote_copy(
        src_ref=srcs[a].at[2 * px + py], dst_ref=outs[a].at[j], send_sem=send_sems.at[3 * a + j],
        recv_sem=recv_sems.at[3 * a + j], device_id=(px, py, c), device_id_type=MESH)
        for a in range(len(srcs)) for j, (px, py) in enumerate(chips)]


def _chip_sum(p, got, me):
    _, L, R, C = p.shape
    tr = _row_tile(R, 4 * C * 4)

    def body(me_ref, p_ref, g_ref, o_ref):
        acc = p_ref[...].astype(F32)
        for k in range(3):
            acc = acc + g_ref[k].astype(F32)
        o_ref[...] = acc

    return pl.pallas_call(
        body, name="chip_sum",
        grid_spec=pltpu.PrefetchScalarGridSpec(
            num_scalar_prefetch=1, grid=(L, R // tr),
            in_specs=[pl.BlockSpec((None, None, tr, C), lambda l, i, me_ref: (me_ref[0], l, i, 0)),
                      pl.BlockSpec((3, None, tr, C), lambda l, i, me_ref: (0, l, i, 0))],
            out_specs=pl.BlockSpec((None, tr, C), lambda l, i, me_ref: (l, i, 0))),
        out_shape=jax.ShapeDtypeStruct((L, R, C), F32), compiler_params=_params(("parallel", "parallel")))(me, p, got)


def _sum_parts(t, name):
    n, H, W = t.shape
    th = _row_tile(H, (n + 1) * W * 4)

    def body(t_ref, o_ref):
        acc = t_ref[0]
        for k in range(1, n):
            acc = acc + t_ref[k]
        o_ref[...] = acc

    return pl.pallas_call(
        body, name=name, grid=(H // th,), in_specs=[pl.BlockSpec((n, th, W), lambda i: (0, i, 0))],
        out_specs=pl.BlockSpec((th, W), lambda i: (i, 0)), out_shape=jax.ShapeDtypeStruct((H, W), F32),
        compiler_params=_params(("parallel",)))(t)


def _share_with_sibling(halves):
    na = len(halves)

    def body(*refs):
        srcs, outs, (send_sems, recv_sems) = refs[:na], refs[na:2 * na], refs[2 * na:]
        x, y, c, _ = _place()

        def copy(a, which):
            hr = halves[a].shape[1]
            return pltpu.make_async_remote_copy(
                src_ref=srcs[a], dst_ref=outs[a].at[:, pl.ds(hr * which, hr)], send_sem=send_sems.at[a],
                recv_sem=recv_sems.at[a], device_id=(x, y, 1 - c), device_id_type=MESH)

        sent = [copy(a, c) for a in range(na)]
        for cp in sent:
            cp.start()
        for a in range(na):
            copy(a, 1 - c).wait_recv()
        for cp in sent:
            cp.wait_send()

    return _comm_call(body, "share_with_sibling", halves,
                      [jax.ShapeDtypeStruct((h.shape[0], 2 * h.shape[1], h.shape[2]), h.dtype) for h in halves], na)


def _gather_all(v):
    n, W = v.shape

    def body(src, out, send_sems, recv_sems, own_sem):
        x, y, c, _ = _place()
        own = pltpu.make_async_copy(src, out.at[4 * x + 2 * y + c], own_sem)
        own.start()
        flips = [(fx, fy, fc) for fx in (0, 1) for fy in (0, 1) for fc in (0, 1)][1:]
        sent = []
        for k, (fx, fy, fc) in enumerate(flips):
            cp = pltpu.make_async_remote_copy(
                src_ref=src, dst_ref=out.at[4 * x + 2 * y + c], send_sem=send_sems.at[k], recv_sem=recv_sems.at[k],
                device_id=(x ^ fx, y ^ fy, c ^ fc), device_id_type=MESH)
            cp.start()
            sent.append(cp)
        for k, (fx, fy, fc) in enumerate(flips):
            pltpu.make_async_remote_copy(
                src_ref=src, dst_ref=out.at[4 * (x ^ fx) + 2 * (y ^ fy) + (c ^ fc)], send_sem=send_sems.at[k],
                recv_sem=recv_sems.at[k], device_id=(x ^ fx, y ^ fy, c ^ fc), device_id_type=MESH).wait_recv()
        for cp in sent:
            cp.wait_send()
        own.wait()

    return pl.pallas_call(
        body, name="gather_all", in_specs=[HBM_SPEC], out_specs=HBM_SPEC,
        out_shape=jax.ShapeDtypeStruct((8, n, W), v.dtype),
        scratch_shapes=[pltpu.SemaphoreType.DMA((7,)), pltpu.SemaphoreType.DMA((7,)), pltpu.SemaphoreType.DMA(())],
        compiler_params=pltpu.CompilerParams(has_side_effects=True))(v)


def _adamw(w, g, m, v):
    shape = w.shape
    cols = shape[-1]
    w2, g2, m2, v2 = (a.reshape(-1, cols) for a in (w, g, m, v))
    rows = w2.shape[0]
    tr = next((t for t in (1024, 512, 256, 128, 64, 32, 16, 8) if rows % t == 0 and t * cols * 4 <= (2 << 20)), rows)

    def body(w_ref, g_ref, m_ref, v_ref, d_ref, nm_ref, nv_ref):
        g_ = g_ref[...]
        nm = ADAM_B1 * m_ref[...] + (1.0 - ADAM_B1) * g_
        nv = ADAM_B2 * v_ref[...] + (1.0 - ADAM_B2) * (g_ * g_)
        m_hat = nm / (1.0 - ADAM_B1 ** ADAM_STEP)
        v_hat = nv / (1.0 - ADAM_B2 ** ADAM_STEP)
        d_ref[...] = -ADAM_LR * (m_hat / (jnp.sqrt(v_hat) + ADAM_EPS) + ADAM_WD * w_ref[...])
        nm_ref[...] = nm
        nv_ref[...] = nv

    blk = pl.BlockSpec((tr, cols), lambda i: (i, 0))
    outs = pl.pallas_call(
        body, name="adamw", grid=(rows // tr,), in_specs=[blk] * 4, out_specs=[blk] * 3,
        out_shape=[jax.ShapeDtypeStruct((rows, cols), F32)] * 3, compiler_params=_params(("parallel",)))(w2, g2, m2, v2)
    return tuple(o.reshape(shape) for o in outs)


BY_COLUMNS = ("w_in", "w_uq", "w_ukv")


def _chip_part(name, a, k):
    if name == "w_in":
        n = D_IN // 4
        return _model_cols(a, k * n, (k + 1) * n)
    n = a.shape[1 if name in BY_COLUMNS else 0] // 4
    return a[:, k * n:(k + 1) * n] if name in BY_COLUMNS else a[k * n:(k + 1) * n]


def kernel(x, mem, positions, w_in, q_norm_g, w_uq, kv_norm_g, w_ukv, sg_ln_g, sg_ln_b, sg_w, sg_b, w_mem_k, w_mem_v, w_out, ln_g, ln_b, loss_target, m_w_in, m_q_norm_g, m_w_uq, m_kv_norm_g, m_w_ukv, m_sg_ln_g, m_sg_ln_b, m_sg_w, m_sg_b, m_w_mem_k, m_w_mem_v, m_w_out, m_ln_g, m_ln_b, v_w_in, v_q_norm_g, v_w_uq, v_kv_norm_g, v_w_ukv, v_sg_ln_g, v_sg_ln_b, v_sg_w, v_sg_b, v_w_mem_k, v_w_mem_v, v_w_out, v_ln_g, v_ln_b):
    weights = dict(w_in=w_in, q_norm_g=q_norm_g, w_uq=w_uq, kv_norm_g=kv_norm_g, w_ukv=w_ukv, sg_ln_g=sg_ln_g,
                   sg_ln_b=sg_ln_b, sg_w=sg_w, sg_b=sg_b, w_mem_k=w_mem_k, w_mem_v=w_mem_v, w_out=w_out, ln_g=ln_g, ln_b=ln_b)
    mom_m = dict(w_in=m_w_in, q_norm_g=m_q_norm_g, w_uq=m_w_uq, kv_norm_g=m_kv_norm_g, w_ukv=m_w_ukv, sg_ln_g=m_sg_ln_g,
                 sg_ln_b=m_sg_ln_b, sg_w=m_sg_w, sg_b=m_sg_b, w_mem_k=m_w_mem_k, w_mem_v=m_w_mem_v, w_out=m_w_out,
                 ln_g=m_ln_g, ln_b=m_ln_b)
    mom_v = dict(w_in=v_w_in, q_norm_g=v_q_norm_g, w_uq=v_w_uq, kv_norm_g=v_kv_norm_g, w_ukv=v_w_ukv, sg_ln_g=v_sg_ln_g,
                 sg_ln_b=v_sg_ln_b, sg_w=v_sg_w, sg_b=v_sg_b, w_mem_k=v_w_mem_k, w_mem_v=v_w_mem_v, w_out=v_w_out,
                 ln_g=v_ln_g, ln_b=v_ln_b)
    c_idx = lax.axis_index("c").astype(jnp.int32).reshape(1)

    me = 2 * lax.axis_index("x") + lax.axis_index("y")
    shards = [[weights[n][l].astype(BF16) for n in SHARDED] for l in range(DEPTH)]

    def layer_source(l, fetched):
        if l == 0:
            fetched = _gather_weights(shards[0])
        return _layer_weights([lax.dynamic_update_slice(g, s[None], (me, 0, 0)) for g, s in zip(fetched, shards[l])])

    def pair_sums(grads, layers):
        gs = [jnp.stack([jnp.stack([_chip_part(n, grads[n][l], k) for l in layers]) for k in range(4)]) for n in SHARDED]
        return [_pair_sum(g, o, c_idx) for g, o in zip(gs, _swap_halves(gs))]

    loss_dev, grad_x, grads, (pairs_hi, got_hi), (pairs_lo, got_lo) = _local_step(
        x[0], mem[0], positions[0], loss_target[0], {n: weights[n] for n in SMALL}, layer_source,
        lambda l: shards[l + 1] if l + 1 < DEPTH else None, lambda grads: pair_sums(grads, range(1, DEPTH)),
        lambda grads: pair_sums(grads, [0]))
    me1 = me.astype(jnp.int32).reshape(1)
    halves = [_chip_sum(p, o, me1) for p, o in zip(pairs_lo + pairs_hi, list(got_lo) + list(got_hi))]
    whole = [lax.dynamic_update_slice(r, h, (0, h.shape[1] * c_idx[0], 0)) for r, h in zip(_share_with_sibling(halves), halves)]
    grad_out = {n: jnp.concatenate([lo, hi]) for n, lo, hi in zip(SHARDED, whole[:len(SHARDED)], whole[len(SHARDED):])}

    small_sizes = [weights[n].size for n in SMALL]
    vec = jnp.concatenate([g.reshape(-1) for n in SMALL for g in grads[n]] + [loss_dev[0]])
    n_small = vec.shape[0]
    rows_small = -(-n_small // (8 * FLAT_W)) * 8
    vec = jnp.pad(vec, (0, rows_small * FLAT_W - n_small)).reshape(rows_small, FLAT_W)
    total = _sum_parts(_gather_all(vec), "device_sum").reshape(-1)
    at = 0
    for n, size in zip(SMALL, small_sizes):
        grad_out[n] = total[at:at + size].reshape(weights[n].shape)
        at += size
    loss = total[at]

    names = list(weights)
    upd = {n: _adamw(weights[n], grad_out[n], mom_m[n], mom_v[n]) for n in names}
    return (loss, grad_x[None], *[grad_out[n] for n in names], *[upd[n][0] for n in names],
            *[upd[n][1] for n in names], *[upd[n][2] for n in names])
```

```python
import math

import jax
import jax.numpy as jnp
from jax import lax
from jax.experimental import pallas as pl
from jax.experimental.pallas import tpu as pltpu

F32, BF16 = jnp.float32, jnp.bfloat16

D_MODEL = 2048
DEPTH = 4
CHUNK = 64
MLA_HEADS = 6
MLA_SCALE = 1.0 / math.sqrt(192.0)
SB_HEADS = 4
SB_SCALE = 1.0 / math.sqrt(128.0)
MEM_HEADS = 4
MEM_SCALE = 1.0 / math.sqrt(64.0)
ROPE_THETA = 10000.0
ALPHA = (2.0 * DEPTH) ** 0.25
LN_EPS = 1e-5
RMS_EPS = 1e-6
ADAM_LR, ADAM_B1, ADAM_B2, ADAM_EPS, ADAM_WD, ADAM_STEP = 0.001, 0.9, 0.999, 1e-08, 0.01, 10

ORIG = dict(c_q=(0, 512), c_kv=(512, 256), k_pe=(768, 64), g_a=(832, 768), sg_u=(1600, 512), sg_v=(2112, 512),
            g_b=(2624, 512), sb_q=(3136, 512), sb_k=(3648, 512), sb_v=(4160, 512), g_c=(4672, 512),
            m_q=(5184, 256), g_m=(5440, 256))
D_IN = 5696
PERM_ORDER = ("c_q", "c_kv", "m_q", "sg_u", "sg_v", "sb_q", "sb_k", "sb_v", "g_a", "g_b", "g_c", "g_m", "k_pe")
HP = 5760
CQ, CKV, MQ, SGU, SGV, SBQ, GATE, KPE = 0, 512, 768, 1024, 1536, 2048, 3584, 5632

Q_BLK = 2048
K_BLK = 512
MLA_FWD_K_BLK = 1024
SB_Q_BLK = 512
SB_K_BLK = 256
SB_DEAD = -110.0
LANE = 128
VMEM_LIMIT = 56 * 1024 * 1024

FLAT_W = 1024
SHARDED = ("w_in", "w_uq", "w_ukv", "w_mem_k", "w_mem_v", "w_out")
SMALL = ("q_norm_g", "kv_norm_g", "sg_ln_g", "sg_ln_b", "sg_w", "sg_b", "ln_g", "ln_b")


def _params(sem=None):
    return pltpu.CompilerParams(dimension_semantics=sem, vmem_limit_bytes=VMEM_LIMIT)


def _tile(dim, pref):
    if dim <= pref:
        return dim
    t = (pref // LANE) * LANE
    while t >= LANE:
        if dim % t == 0:
            return t
        t -= LANE
    return dim


def _row_tile(rows, bytes_per_row, budget=8 << 20):
    best = None
    for t in range(8, rows + 1, 8):
        if rows % t == 0 and t * bytes_per_row <= budget:
            best = t
    return best if best else rows


def _dot_nt(a, b):
    return lax.dot_general(a, b, (((1,), (1,)), ((), ())), preferred_element_type=F32)


def _dot_tn(a, b):
    return lax.dot_general(a, b, (((0,), (0,)), ((), ())), preferred_element_type=F32)


def _dot(a, b):
    return jnp.dot(a, b, preferred_element_type=F32)


def _mm(a, b, *, ta=False, tb=False, add=None, add_scale=1.0, out_dtype=F32, tm=512, tn=512, tk=512, name="mm",
        ride=None):
    (K, M) = a.shape if ta else a.shape[::-1]
    (N, Kb) = b.shape if tb else b.shape[::-1]
    assert K == Kb, (a.shape, b.shape, ta, tb)
    tm, tn, tk = _tile(M, tm), _tile(N, tn), _tile(K, tk)
    gm, gn, nk = M // tm, N // tn, K // tk
    a_spec = pl.BlockSpec((tk, tm), lambda i, j, k: (k, i)) if ta else pl.BlockSpec((tm, tk), lambda i, j, k: (i, k))
    b_spec = pl.BlockSpec((tn, tk), lambda i, j, k: (j, k)) if tb else pl.BlockSpec((tk, tn), lambda i, j, k: (k, j))
    o_spec = pl.BlockSpec((tm, tn), lambda i, j, k: (i, j))
    dn = (((0 if ta else 1,), (1 if tb else 0,)), ((), ()))
    has_add = add is not None
    na = len(ride["arrays"]) if ride else 0
    n_in = 2 + has_add + na

    def body(*refs):
        a_ref, b_ref = refs[:2]
        add_ref = refs[2] if has_add else None
        o_ref = refs[n_in]
        scratch = refs[n_in + 1 + na:]
        i, j, k = pl.program_id(0), pl.program_id(1), pl.program_id(2)

        def riders():
            return ride["copies"](refs[n_in - na:n_in], refs[n_in + 1:n_in + 1 + na], *scratch[-2:])

        if na:
            @pl.when(jnp.logical_and(jnp.logical_and(i == 0, j == 0), k == 0))
            def _():
                for cp in riders():
                    cp.start()

        part = lax.dot_general(a_ref[...].astype(BF16), b_ref[...].astype(BF16), dn, preferred_element_type=F32)

        def finish(r):
            if has_add:
                r = r + add_scale * add_ref[...]
            o_ref[...] = r.astype(o_ref.dtype)

        if nk == 1:
            finish(part)
        else:
            acc_ref = scratch[0]

            @pl.when(k == 0)
            def _():
                acc_ref[...] = part

            @pl.when(k > 0)
            def _():
                acc_ref[...] += part

            @pl.when(k == nk - 1)
            def _():
                finish(acc_ref[...])

        if na:
            @pl.when(jnp.logical_and(jnp.logical_and(i == gm - 1, j == gn - 1), k == nk - 1))
            def _():
                for cp in riders():
                    cp.wait()

    ins = [a, b] + ([add] if has_add else []) + (ride["arrays"] if ride else [])
    specs = [a_spec, b_spec] + ([o_spec] if has_add else []) + [HBM_SPEC] * na
    res = pl.pallas_call(
        body, name=name, grid=(gm, gn, nk), in_specs=specs, out_specs=[o_spec] + [HBM_SPEC] * na,
        out_shape=[jax.ShapeDtypeStruct((M, N), out_dtype)] + (ride["out_shapes"] if ride else []),
        scratch_shapes=([pltpu.VMEM((tm, tn), F32)] if nk > 1 else [])
                       + ([pltpu.SemaphoreType.DMA((ride["n_sems"],))] * 2 if na else []),
        compiler_params=pltpu.CompilerParams(
            dimension_semantics=("arbitrary",) * 3 if na else ("parallel", "parallel", "arbitrary"),
            vmem_limit_bytes=VMEM_LIMIT, has_side_effects=bool(na)))(*ins)
    return (res[0], list(res[1:])) if na else res[0]


GELU_K = math.sqrt(2.0 / math.pi)


def _gelu(x):
    t = jnp.tanh(GELU_K * (x + 0.044715 * (x * x * x)))
    return 0.5 * x * (1.0 + t)


def _gelu_grad(x):
    t = jnp.tanh(GELU_K * (x + 0.044715 * (x * x * x)))
    return 0.5 * (1.0 + t) + 0.5 * x * (1.0 - t * t) * GELU_K * (1.0 + 3.0 * 0.044715 * x * x)


def _rope_swap(t):
    lane = lax.broadcasted_iota(jnp.int32, t.shape, 1)
    return jnp.where(lane < 32, pltpu.roll(t, 96, axis=1), pltpu.roll(t, 32, axis=1))


def _rope(t, c, s):
    return t * c + _rope_swap(t) * s


def _rope_bwd(dt, c, s):
    return dt * c - _rope_swap(dt) * s


def _row_spec(tm, w, cb=0):
    return pl.BlockSpec((tm, w), lambda i: (i, cb))


def _fix_spec(shape):
    return pl.BlockSpec(shape, lambda *_: (0,) * len(shape))


def _rms_fwd(h, off, width, g, name):
    S = h.shape[0]
    tm = _tile(S, 2048)

    def body(x_ref, g_ref, o_ref):
        x = x_ref[...]
        r = lax.rsqrt(jnp.mean(x * x, axis=1, keepdims=True) + RMS_EPS)
        o_ref[...] = (x * r * g_ref[...]).astype(BF16)

    return pl.pallas_call(
        body, name=name, grid=(S // tm,), in_specs=[_row_spec(tm, width, off // width), _fix_spec((1, width))],
        out_specs=_row_spec(tm, width), out_shape=jax.ShapeDtypeStruct((S, width), BF16),
        compiler_params=_params(("parallel",)))(h, g.reshape(1, width))


def _rms_bwd(h, off, width, g, dxn, name):
    S = h.shape[0]
    tm = _tile(S, 2048)

    def body(x_ref, g_ref, d_ref, dx_ref, dg_ref):
        @pl.when(pl.program_id(0) == 0)
        def _():
            dg_ref[...] = jnp.zeros_like(dg_ref)

        x, d = x_ref[...], d_ref[...]
        r = lax.rsqrt(jnp.mean(x * x, axis=1, keepdims=True) + RMS_EPS)
        gd = d * g_ref[...]
        dx_ref[...] = gd * r - x * (r * r * r) * jnp.mean(gd * x, axis=1, keepdims=True)
        dg_ref[...] += jnp.sum(d * x * r, axis=0, keepdims=True)

    return pl.pallas_call(
        body, name=name, grid=(S // tm,),
        in_specs=[_row_spec(tm, width, off // width), _fix_spec((1, width)), _row_spec(tm, width)],
        out_specs=[_row_spec(tm, width), _fix_spec((1, width))],
        out_shape=[jax.ShapeDtypeStruct((S, width), F32), jax.ShapeDtypeStruct((1, width), F32)],
        compiler_params=_params(("arbitrary",)))(h, g.reshape(1, width), dxn)


def _q_proj(xn, w, rc, rs):
    S = xn.shape[0]
    tm = _tile(S, 2048)

    def body(x_ref, w_ref, c_ref, s_ref, q_ref):
        q = _dot(x_ref[...], w_ref[...]) * MLA_SCALE
        q_ref[:, :LANE] = q[:, :LANE].astype(BF16)
        q_ref[:, LANE:] = _rope(q[:, LANE:], c_ref[...], s_ref[...]).astype(BF16)

    return pl.pallas_call(
        body, name="q_proj", grid=(S // tm, MLA_HEADS),
        in_specs=[pl.BlockSpec((tm, 512), lambda i, j: (i, 0)), pl.BlockSpec((512, 256), lambda i, j: (0, j)),
                  pl.BlockSpec((tm, LANE), lambda i, j: (i, 0)), pl.BlockSpec((tm, LANE), lambda i, j: (i, 0))],
        out_specs=pl.BlockSpec((tm, 256), lambda i, j: (i, j)),
        out_shape=jax.ShapeDtypeStruct((S, MLA_HEADS * 256), BF16),
        compiler_params=_params(("parallel", "parallel")))(xn, w, rc, rs)


def _kv_proj(xn, w, h, rc, rs):
    S = xn.shape[0]
    tm = _tile(S, 2048)

    def body(x_ref, w_ref, pe_ref, c_ref, s_ref, k_ref, v_ref):
        kv = _dot(x_ref[...], w_ref[...])
        k_ref[:, :LANE] = kv[:, :LANE].astype(BF16)
        k_ref[:, LANE:] = _rope(pe_ref[...], c_ref[...], s_ref[...]).astype(BF16)
        v_ref[...] = kv[:, LANE:].astype(BF16)

    return pl.pallas_call(
        body, name="kv_proj", grid=(S // tm, MLA_HEADS),
        in_specs=[pl.BlockSpec((tm, 256), lambda i, j: (i, 0)), pl.BlockSpec((256, 256), lambda i, j: (0, j)),
                  pl.BlockSpec((tm, LANE), lambda i, j: (i, KPE // LANE)),
                  pl.BlockSpec((tm, LANE), lambda i, j: (i, 0)), pl.BlockSpec((tm, LANE), lambda i, j: (i, 0))],
        out_specs=[pl.BlockSpec((tm, 256), lambda i, j: (i, j)), pl.BlockSpec((tm, LANE), lambda i, j: (i, j))],
        out_shape=[jax.ShapeDtypeStruct((S, MLA_HEADS * 256), BF16), jax.ShapeDtypeStruct((S, MLA_HEADS * LANE), BF16)],
        compiler_params=_params(("parallel", "parallel")))(xn, w, h, rc, rs)


def _kv_bwd_prep(dk, dv, rc, rs):
    S = dk.shape[1]
    tm = _tile(S, 1024)

    def body(dk_ref, dv_ref, c_ref, s_ref, o_ref, pe_ref):
        rot = jnp.zeros((tm, LANE), F32)
        for hh in range(MLA_HEADS):
            o_ref[:, hh * 256:hh * 256 + LANE] = dk_ref[hh, :, :LANE].astype(BF16)
            o_ref[:, hh * 256 + LANE:(hh + 1) * 256] = dv_ref[hh].astype(BF16)
            rot = rot + dk_ref[hh, :, LANE:]
        pe_ref[...] = _rope_bwd(rot, c_ref[...], s_ref[...])

    return pl.pallas_call(
        body, name="kv_bwd_prep", grid=(S // tm,),
        in_specs=[pl.BlockSpec((MLA_HEADS, tm, 256), lambda i: (0, i, 0)),
                  pl.BlockSpec((MLA_HEADS, tm, LANE), lambda i: (0, i, 0)), _row_spec(tm, LANE), _row_spec(tm, LANE)],
        out_specs=[_row_spec(tm, MLA_HEADS * 256), _row_spec(tm, LANE)],
        out_shape=[jax.ShapeDtypeStruct((S, MLA_HEADS * 256), BF16), jax.ShapeDtypeStruct((S, LANE), F32)],
        compiler_params=_params(("parallel",)))(dk, dv, rc, rs)


def _chunk_mask(T):
    row = lax.broadcasted_iota(jnp.int32, (T, T), 0)
    col = lax.broadcasted_iota(jnp.int32, (T, T), 1)
    return (col // CHUNK) <= (row // CHUNK)


def _att_blocks(S, q_blk=None, k_blk=None):
    tq = min(q_blk or Q_BLK, S)
    tk = min(k_blk or K_BLK, tq)
    return tq, tk, tq // tk


def _tail_masks(rows, tk):
    row = lax.broadcasted_iota(jnp.int32, (rows, tk), 0)
    col = lax.broadcasted_iota(jnp.int32, (rows, tk), 1)
    return (col // CHUNK) <= (row // CHUNK), col < row


def _span_masks(tk, r):
    row = lax.broadcasted_iota(jnp.int32, (tk, (r + 1) * tk), 0) + r * tk
    col = lax.broadcasted_iota(jnp.int32, (tk, (r + 1) * tk), 1)
    return (col // CHUNK) <= (row // CHUNK), col < row


def _put_rows(old, new, r0):
    return new if r0 == 0 else jnp.concatenate([old[:r0], new], axis=0)


def _mla_fwd(q, kp, v, nxt=None):
    S = q.shape[0]
    TQ, TK, n = _att_blocks(S, None, MLA_FWD_K_BLK)
    nq = S // TQ
    na = len(nxt) if nxt else 0

    def ride_along(srcs, outs, send_sems, recv_sems):
        x, y, c, chips = _place()

        def copy(a, j, px, py, chip):
            return pltpu.make_async_remote_copy(
                src_ref=srcs[a], dst_ref=outs[a].at[chip], send_sem=send_sems.at[3 * a + j],
                recv_sem=recv_sems.at[3 * a + j], device_id=(px, py, c), device_id_type=MESH)

        first = jnp.logical_and(pl.program_id(0) == 0, pl.program_id(1) == 0)
        last = jnp.logical_and(pl.program_id(0) == MLA_HEADS - 1, pl.program_id(1) == nq - 1)

        @pl.when(first)
        def _():
            for a in range(na):
                for j, (px, py) in enumerate(chips):
                    copy(a, j, px, py, 2 * x + y).start()

        @pl.when(last)
        def _():
            for a in range(na):
                for j, (px, py) in enumerate(chips):
                    copy(a, j, px, py, 2 * x + y).wait_send()
                    copy(a, j, px, py, 2 * px + py).wait_recv()

    def body(q_ref, k_ref, v_ref, *rest):
        if na:
            o_ref, lse_ref = rest[na:na + 2]
            ride_along(rest[:na], rest[na + 2:2 * na + 2], *rest[2 * na + 2:])
        else:
            o_ref, lse_ref = rest
        i = pl.program_id(1)

        def update(carry, qb, keys, mask):
            m, l, acc = carry
            s = _dot_nt(qb, k_ref[keys, :])
            if mask is not None:
                s = jnp.where(mask, s, -1e30)
            m_new = jnp.maximum(m, jnp.max(s, axis=1, keepdims=True))
            a = jnp.exp(m - m_new)
            p = jnp.exp(s - m_new)
            return m_new, a * l + jnp.sum(p, axis=1, keepdims=True), a * acc + _dot(p.astype(BF16), v_ref[keys, :])

        carry = (jnp.full((TQ, 1), -1e30, F32), jnp.zeros((TQ, 1), F32), jnp.zeros((TQ, LANE), F32))
        carry = lax.fori_loop(
            0, i * n, lambda j, c: update(c, q_ref[...], pl.ds(pl.multiple_of(j * TK, TK), TK), None), carry)
        for r in range(n):
            rows = slice(r * TK, (r + 1) * TK)
            m, l, acc = update(tuple(c[rows] for c in carry), q_ref[rows, :],
                               pl.ds(pl.multiple_of(i * TQ, TQ), (r + 1) * TK), _span_masks(TK, r)[0])
            o_ref[rows, :] = acc / l
            lse_ref[rows, :] = jnp.broadcast_to(m + jnp.log(l), (TK, LANE))

    res = pl.pallas_call(
        body, name="mla_fwd_gather" if na else "mla_fwd", grid=(MLA_HEADS, nq),
        in_specs=[pl.BlockSpec((TQ, 256), lambda h, i: (i, h)),
                  pl.BlockSpec((S, 256), lambda h, i: (0, h), pipeline_mode=pl.Buffered(1)),
                  pl.BlockSpec((S, LANE), lambda h, i: (0, h), pipeline_mode=pl.Buffered(1))] + [HBM_SPEC] * na,
        out_specs=[pl.BlockSpec((TQ, LANE), lambda h, i: (i, h)), pl.BlockSpec((TQ, LANE), lambda h, i: (i, h))]
                  + [HBM_SPEC] * na,
        out_shape=[jax.ShapeDtypeStruct((S, MLA_HEADS * LANE), F32), jax.ShapeDtypeStruct((S, MLA_HEADS * LANE), F32)]
                  + [jax.ShapeDtypeStruct((4,) + s.shape, s.dtype) for s in (nxt or [])],
        scratch_shapes=[pltpu.SemaphoreType.DMA((3 * na,)), pltpu.SemaphoreType.DMA((3 * na,))] if na else [],
        compiler_params=pltpu.CompilerParams(dimension_semantics=("arbitrary", "arbitrary"), vmem_limit_bytes=VMEM_LIMIT,
                                             has_side_effects=bool(na)))(q, kp, v, *(nxt or []))
    return res[0], res[1], (list(res[2:]) if na else None)


def _mla_bwd(q, kp, v, do_cat, o_cat, lse, rc, rs, ride=None):
    S = q.shape[0]
    TQ, TK, n = _att_blocks(S)
    nq = S // TQ
    na = len(ride["arrays"]) if ride else 0

    def body(q_ref, k_ref, v_ref, do_ref, o_ref, lse_ref, c_ref, s_ref, *rest):
        dq_ref, dk_hbm, dv_hbm = rest[na:na + 3]
        dk_acc, dv_acc = rest[2 * na + 3:2 * na + 5]
        h, i = pl.program_id(0), pl.program_id(1)
        if na:
            @pl.when(jnp.logical_and(h == 0, i == 0))
            def _():
                for cp in ride["copies"](rest[:na], rest[na + 3:2 * na + 3], *rest[2 * na + 5:]):
                    cp.start()

            @pl.when(jnp.logical_and(h == MLA_HEADS - 1, i == nq - 1))
            def _():
                for cp in ride["copies"](rest[:na], rest[na + 3:2 * na + 3], *rest[2 * na + 5:]):
                    cp.wait()

        @pl.when(i == 0)
        def _():
            dk_acc[...] = jnp.zeros_like(dk_acc)
            dv_acc[...] = jnp.zeros_like(dv_acc)

        do32 = do_ref[...]
        dob = do32.astype(BF16)
        delta = jnp.sum(do32 * o_ref[...], axis=1, keepdims=True)
        lse_col = lse_ref[:, :1]

        def blk(j, dq, r0, masked):
            sl = pl.ds(pl.multiple_of(j * TK, TK), TK)
            kb, vb, qb = k_ref[sl, :], v_ref[sl, :], q_ref[r0:, :]
            s = _dot_nt(qb, kb)
            if masked:
                s = jnp.where(_tail_masks(TQ - r0, TK)[0], s, -1e30)
            p = jnp.exp(s - lse_col[r0:])
            ds = (p * (_dot_nt(dob[r0:], vb) - delta[r0:])).astype(BF16)
            dk_acc[sl, :] += _dot_tn(ds, qb)
            dv_acc[sl, :] += _dot_tn(p.astype(BF16), dob[r0:])
            return _put_rows(dq, dq[r0:] + _dot(ds, kb), r0)

        dq = lax.fori_loop(0, i * n, lambda j, c: blk(j, c, 0, False), jnp.zeros((TQ, 256), F32))
        for t in range(n):
            dq = blk(i * n + t, dq, t * TK, True)
        dq_ref[:, :LANE] = (dq[:, :LANE] * MLA_SCALE).astype(BF16)
        dq_ref[:, LANE:] = _rope_bwd(dq[:, LANE:] * MLA_SCALE, c_ref[...], s_ref[...]).astype(BF16)

        @pl.when(i == nq - 1)
        def _():
            pltpu.sync_copy(dk_acc, dk_hbm.at[h])
            pltpu.sync_copy(dv_acc, dv_hbm.at[h])

    any_spec = pl.BlockSpec(memory_space=pl.ANY)
    T = TQ
    rows = pl.BlockSpec((T, LANE), lambda h, i: (i, 0))
    res = pl.pallas_call(
        body, name="mla_bwd_exchange" if na else "mla_bwd", grid=(MLA_HEADS, nq),
        in_specs=[pl.BlockSpec((T, 256), lambda h, i: (i, h)),
                  pl.BlockSpec((S, 256), lambda h, i: (0, h), pipeline_mode=pl.Buffered(1)),
                  pl.BlockSpec((S, LANE), lambda h, i: (0, h), pipeline_mode=pl.Buffered(1)),
                  pl.BlockSpec((T, LANE), lambda h, i: (i, h)),
                  pl.BlockSpec((T, LANE), lambda h, i: (i, h)), pl.BlockSpec((T, LANE), lambda h, i: (i, h)), rows, rows]
                 + [HBM_SPEC] * na,
        out_specs=[pl.BlockSpec((T, 256), lambda h, i: (i, h)), any_spec, any_spec] + [HBM_SPEC] * na,
        out_shape=[jax.ShapeDtypeStruct((S, MLA_HEADS * 256), BF16), jax.ShapeDtypeStruct((MLA_HEADS, S, 256), F32),
                   jax.ShapeDtypeStruct((MLA_HEADS, S, LANE), F32)] + (ride["out_shapes"] if ride else []),
        scratch_shapes=[pltpu.VMEM((S, 256), F32), pltpu.VMEM((S, LANE), F32)]
                       + ([pltpu.SemaphoreType.DMA((ride["n_sems"],))] * 2 if na else []),
        compiler_params=pltpu.CompilerParams(dimension_semantics=("arbitrary", "arbitrary"), vmem_limit_bytes=VMEM_LIMIT,
                                             has_side_effects=bool(na)))(q, kp, v, do_cat, o_cat, lse, rc, rs, *(ride["arrays"] if ride else []))
    return res[0], res[1], res[2], (list(res[3:]) if na else None)


def _split_dot(x, tri):
    top = lax.bitcast_convert_type(lax.bitcast_convert_type(x, jnp.uint32) & jnp.uint32(0xFFFF0000), F32)
    return _dot(top.astype(BF16), tri) + _dot((x - top).astype(BF16), tri)


def _sb_block(qb, kb, tri, carry, masked):
    z = _dot_nt(qb, kb)
    lb = jnp.minimum(z, 0.0) - jnp.log(1.0 + jnp.exp(-jnp.abs(z)))
    lm = lb - z
    strict = None
    if masked:
        strict = _tail_masks(z.shape[0], z.shape[1])[1]
        lm = jnp.where(strict, lm, 0.0)
    a = jnp.exp(lb + carry + _split_dot(lm, tri))
    if masked:
        a = jnp.where(strict, a, 0.0)
    return a, lb, lm, strict


def _sb_walk(blk, j0, state):
    def alive(c):
        return jnp.logical_and(c[0] >= 0, jnp.max(c[1][0]) > SB_DEAD)

    return lax.while_loop(alive, lambda c: (c[0] - 1, blk(c[0], c[1], 0, False)), (j0, state))[1]


def _triangle(tk):
    row = lax.broadcasted_iota(jnp.int32, (tk, tk), 0)
    col = lax.broadcasted_iota(jnp.int32, (tk, tk), 1)
    return (row > col).astype(BF16)


def _sb_fwd(qkv):
    S = qkv.shape[0]
    TQ, TK, n = _att_blocks(S, SB_Q_BLK, SB_K_BLK)
    T = TQ

    def body(q_ref, k_ref, v_ref, o_ref):
        i = pl.program_id(1)
        tri = _triangle(TK)

        def blk(j, state, r0, masked):
            carry, acc = (c[r0:] for c in state)
            sl = pl.ds(pl.multiple_of(j * TK, TK), TK)
            a, _, lm, _ = _sb_block(q_ref[r0:, :], k_ref[sl, :], tri, carry, masked)
            new = (carry + jnp.sum(lm, axis=1, keepdims=True), acc + _dot(a.astype(BF16), v_ref[sl, :]))
            return tuple(_put_rows(c, u, r0) for c, u in zip(state, new))

        state = (jnp.zeros((TQ, 1), F32), jnp.zeros((TQ, LANE), F32))
        for t in reversed(range(n)):
            state = blk(i * n + t, state, t * TK, True)
        state = _sb_walk(blk, i * n - 1, state)
        o_ref[...] = state[1]

    return pl.pallas_call(
        body, name="sb_fwd", grid=(SB_HEADS, S // T),
        in_specs=[pl.BlockSpec((T, LANE), lambda h, i: (i, h)), pl.BlockSpec((S, LANE), lambda h, i: (0, 4 + h)),
                  pl.BlockSpec((S, LANE), lambda h, i: (0, 8 + h))],
        out_specs=pl.BlockSpec((T, LANE), lambda h, i: (i, h)),
        out_shape=jax.ShapeDtypeStruct((S, SB_HEADS * LANE), F32),
        compiler_params=_params(("parallel", "arbitrary")))(qkv, qkv, qkv)


def _sb_bwd(qkv, do_cat, o_cat, col0):
    S = qkv.shape[0]
    TQ, TK, n = _att_blocks(S, SB_Q_BLK, SB_K_BLK)
    T = TQ
    nq = S // TQ

    def body(q_ref, k_ref, v_ref, do_ref, o_ref, dq_ref, dk_hbm, dv_hbm, dk_acc, dv_acc):
        h, i = pl.program_id(0), pl.program_id(1)

        @pl.when(i == 0)
        def _():
            dk_acc[...] = jnp.zeros_like(dk_acc)
            dv_acc[...] = jnp.zeros_like(dv_acc)

        dob = do_ref[...].astype(BF16)
        tri = _triangle(TK)
        rest0 = jnp.sum(dob.astype(F32) * o_ref[...], axis=1, keepdims=True)

        def blk(j, state, r0, masked):
            carry, rest, dq = (c[r0:] for c in state)
            sl = pl.ds(pl.multiple_of(j * TK, TK), TK)
            kb, vb, qb = k_ref[sl, :], v_ref[sl, :], q_ref[r0:, :]
            a, lb, lm, strict = _sb_block(qb, kb, tri, carry, masked)
            ab = a.astype(BF16)
            e = ab.astype(F32) * _dot_nt(dob[r0:], vb)
            dz = e - jnp.exp(lb) * (rest - _split_dot(e, tri))
            if masked:
                dz = jnp.where(strict, dz, 0.0)
            dzb = dz.astype(BF16)
            dk_acc[sl, :] += _dot_tn(dzb, qb)
            dv_acc[sl, :] += _dot_tn(ab, dob[r0:])
            new = (carry + jnp.sum(lm, axis=1, keepdims=True), rest - jnp.sum(e, axis=1, keepdims=True),
                   dq + _dot(dzb, kb))
            return tuple(_put_rows(c, u, r0) for c, u in zip(state, new))

        state = (jnp.zeros((TQ, 1), F32), rest0, jnp.zeros((TQ, LANE), F32))
        for t in reversed(range(n)):
            state = blk(i * n + t, state, t * TK, True)
        state = _sb_walk(blk, i * n - 1, state)
        dq_ref[...] = state[2] * SB_SCALE

        @pl.when(i == nq - 1)
        def _():
            lanes = pl.ds(pl.multiple_of(h * LANE, LANE), LANE)
            pltpu.sync_copy(dk_acc, dk_hbm.at[:, lanes])
            pltpu.sync_copy(dv_acc, dv_hbm.at[:, lanes])

    any_spec = pl.BlockSpec(memory_space=pl.ANY)
    return pl.pallas_call(
        body, name="sb_bwd", grid=(SB_HEADS, nq),
        in_specs=[pl.BlockSpec((T, LANE), lambda h, i: (i, h)), pl.BlockSpec((S, LANE), lambda h, i: (0, 4 + h)),
                  pl.BlockSpec((S, LANE), lambda h, i: (0, 8 + h)),
                  pl.BlockSpec((T, LANE), lambda h, i: (i, col0 + h)), pl.BlockSpec((T, LANE), lambda h, i: (i, col0 + h))],
        out_specs=[pl.BlockSpec((T, LANE), lambda h, i: (i, h)), any_spec, any_spec],
        out_shape=[jax.ShapeDtypeStruct((S, SB_HEADS * LANE), F32)] * 3,
        scratch_shapes=[pltpu.VMEM((S, LANE), F32), pltpu.VMEM((S, LANE), F32)],
        compiler_params=_params(("arbitrary", "arbitrary")))(qkv, qkv, qkv, do_cat, o_cat)


def _mem_probs(q, k_ref, hh):
    lane = lax.broadcasted_iota(jnp.int32, (1, 256), 1) // 64
    msk = lane == hh
    qh = jnp.where(msk, q, 0.0).astype(BF16)
    s = _dot_nt(qh, k_ref[...]) * MEM_SCALE
    p = jnp.exp(s - jnp.max(s, axis=1, keepdims=True))
    return msk, qh, p / jnp.sum(p, axis=1, keepdims=True)


def _mem_fwd(h, mk, mv):
    S = h.shape[0]
    tm = _tile(S, 512)

    def body(q_ref, k_ref, v_ref, o_ref):
        q = q_ref[...]
        out = jnp.zeros((tm, 256), F32)
        for hh in range(MEM_HEADS):
            msk, _, p = _mem_probs(q, k_ref, hh)
            out = out + jnp.where(msk, _dot(p.astype(BF16), v_ref[...]), 0.0)
        o_ref[...] = out

    return pl.pallas_call(
        body, name="mem_fwd", grid=(S // tm,),
        in_specs=[_row_spec(tm, 256, MQ // 256), _fix_spec((256, 256)), _fix_spec((256, 256))],
        out_specs=_row_spec(tm, 256), out_shape=jax.ShapeDtypeStruct((S, 256), F32),
        compiler_params=_params(("parallel",)))(h, mk, mv)


def _mem_bwd(h, mk, mv, do_cat, col0):
    S = h.shape[0]
    tm = _tile(S, 512)

    def body(q_ref, k_ref, v_ref, do_ref, dq_ref, dk_ref, dv_ref):
        @pl.when(pl.program_id(0) == 0)
        def _():
            dk_ref[...] = jnp.zeros_like(dk_ref)
            dv_ref[...] = jnp.zeros_like(dv_ref)

        q, do = q_ref[...], do_ref[...]
        dq = jnp.zeros((tm, 256), F32)
        for hh in range(MEM_HEADS):
            msk, qh, p = _mem_probs(q, k_ref, hh)
            doh = jnp.where(msk, do, 0.0).astype(BF16)
            dp = _dot_nt(doh, v_ref[...])
            ds = (p * (dp - jnp.sum(p * dp, axis=1, keepdims=True)) * MEM_SCALE).astype(BF16)
            dq = dq + jnp.where(msk, _dot(ds, k_ref[...]), 0.0)
            dk_ref[...] += _dot_tn(ds, qh)
            dv_ref[...] += _dot_tn(p.astype(BF16), doh)
        dq_ref[...] = dq

    return pl.pallas_call(
        body, name="mem_bwd", grid=(S // tm,),
        in_specs=[_row_spec(tm, 256, MQ // 256), _fix_spec((256, 256)), _fix_spec((256, 256)),
                  _row_spec(tm, 256, col0 // 256)],
        out_specs=[_row_spec(tm, 256), _fix_spec((256, 256)), _fix_spec((256, 256))],
        out_shape=[jax.ShapeDtypeStruct((S, 256), F32), jax.ShapeDtypeStruct((256, 256), F32),
                   jax.ShapeDtypeStruct((256, 256), F32)],
        compiler_params=_params(("arbitrary",)))(h, mk, mv, do_cat)


SG_T = 128


def _sg_norm(sv, g, b):
    gv = _gelu(sv)
    xc = gv - jnp.mean(gv, axis=1, keepdims=True)
    rstd = lax.rsqrt(jnp.mean(xc * xc, axis=1, keepdims=True) + LN_EPS)
    xhat = xc * rstd
    return xhat, rstd, xhat * g + b


def _sg_fwd(h, lng, lnb, w, bias_t):
    S = h.shape[0]
    tm = _tile(S, 512)

    def body(u_ref, v_ref, g_ref, b_ref, w_ref, bias_ref, o_ref):
        mask = _chunk_mask(SG_T)
        for n in range(tm // SG_T):
            rows = slice(n * SG_T, (n + 1) * SG_T)
            u = _gelu(u_ref[rows, :])
            _, _, vn = _sg_norm(v_ref[rows, :], g_ref[...], b_ref[...])
            vb = vn.astype(BF16)
            for gi in range(4):
                cols = slice(gi * LANE, (gi + 1) * LANE)
                wg = jnp.where(mask, w_ref[gi], 0.0).astype(BF16)
                mixed = _dot(wg, vb[:, cols]) + bias_ref[:, gi:gi + 1]
                o_ref[rows, cols] = u[:, cols] * mixed

    return pl.pallas_call(
        body, name="sg_fwd", grid=(S // tm,),
        in_specs=[_row_spec(tm, 512, SGU // 512), _row_spec(tm, 512, SGV // 512), _fix_spec((1, 512)),
                  _fix_spec((1, 512)), _fix_spec((4, SG_T, SG_T)), _fix_spec((SG_T, 4))],
        out_specs=_row_spec(tm, 512), out_shape=jax.ShapeDtypeStruct((S, 512), F32),
        compiler_params=_params(("parallel",)))(h, h, lng.reshape(1, 512), lnb.reshape(1, 512), w, bias_t)


def _sg_bwd(h, lng, lnb, w, bias_t, do_cat, col0):
    S = h.shape[0]
    tm = _tile(S, 512)
    nsteps = S // tm

    def body(u_ref, v_ref, g_ref, b_ref, w_ref, bias_ref, do0_ref, do1_ref, do2_ref, do3_ref,
             du_ref, dv_ref, dw_ref, dbias_ref, dg_ref, db_ref, dvn_scr, dbias_acc):
        do_refs = (do0_ref, do1_ref, do2_ref, do3_ref)
        step = pl.program_id(0)

        @pl.when(step == 0)
        def _():
            dw_ref[...] = jnp.zeros_like(dw_ref)
            dg_ref[...] = jnp.zeros_like(dg_ref)
            db_ref[...] = jnp.zeros_like(db_ref)
            dbias_acc[...] = jnp.zeros_like(dbias_acc)

        mask = _chunk_mask(SG_T)
        for n in range(tm // SG_T):
            rows = slice(n * SG_T, (n + 1) * SG_T)
            su, sv = u_ref[rows, :], v_ref[rows, :]
            u = _gelu(su)
            xhat, rstd, vn = _sg_norm(sv, g_ref[...], b_ref[...])
            vb = vn.astype(BF16)
            ugrad = _gelu_grad(su)
            for gi in range(4):
                cols = slice(gi * LANE, (gi + 1) * LANE)
                do = do_refs[gi][rows, :]
                wg = jnp.where(mask, w_ref[gi], 0.0).astype(BF16)
                mixed = _dot(wg, vb[:, cols]) + bias_ref[:, gi:gi + 1]
                dmixed = do * u[:, cols]
                dmb = dmixed.astype(BF16)
                du_ref[rows, cols] = do * mixed * ugrad[:, cols]
                dvn_scr[:, cols] = _dot_tn(wg, dmb)
                dw_ref[gi] += jnp.where(mask, _dot_nt(dmb, vb[:, cols]), 0.0)
                dbias_acc[gi] += dmixed
            dvn = dvn_scr[...]
            dg_ref[...] += jnp.sum(dvn * xhat, axis=0, keepdims=True)
            db_ref[...] += jnp.sum(dvn, axis=0, keepdims=True)
            dxh = dvn * g_ref[...]
            dgv = rstd * (dxh - jnp.mean(dxh, axis=1, keepdims=True)
                          - xhat * jnp.mean(dxh * xhat, axis=1, keepdims=True))
            dv_ref[rows, :] = dgv * _gelu_grad(sv)

        @pl.when(step == nsteps - 1)
        def _():
            for gi in range(4):
                dbias_ref[:, gi:gi + 1] = jnp.sum(dbias_acc[gi], axis=1, keepdims=True)

    return pl.pallas_call(
        body, name="sg_bwd", grid=(nsteps,),
        in_specs=[_row_spec(tm, 512, SGU // 512), _row_spec(tm, 512, SGV // 512), _fix_spec((1, 512)),
                  _fix_spec((1, 512)), _fix_spec((4, SG_T, SG_T)), _fix_spec((SG_T, 4))]
                 + [_row_spec(tm, LANE, col0 // LANE + gi) for gi in range(4)],
        out_specs=[_row_spec(tm, 512), _row_spec(tm, 512), _fix_spec((4, SG_T, SG_T)), _fix_spec((SG_T, 4)),
                   _fix_spec((1, 512)), _fix_spec((1, 512))],
        out_shape=[jax.ShapeDtypeStruct((S, 512), F32), jax.ShapeDtypeStruct((S, 512), F32),
                   jax.ShapeDtypeStruct((4, SG_T, SG_T), F32), jax.ShapeDtypeStruct((SG_T, 4), F32),
                   jax.ShapeDtypeStruct((1, 512), F32), jax.ShapeDtypeStruct((1, 512), F32)],
        scratch_shapes=[pltpu.VMEM((SG_T, 512), F32), pltpu.VMEM((4, SG_T, SG_T), F32)],
        compiler_params=_params(("arbitrary",)))(h, h, lng.reshape(1, 512), lnb.reshape(1, 512), w, bias_t,
                                                 do_cat, do_cat, do_cat, do_cat)


def _gate_out_ln(branches, h, w_out, x, g, b):
    S = h.shape[0]
    tm = _tile(S, 256)
    widths = [a.shape[1] for a in branches]

    def body(oa_ref, ob_ref, oc_ref, om_ref, g0_ref, g1_ref, g2_ref, g3_ref, w_ref, x_ref, lg_ref, lb_ref,
             cat_ref, yg_ref, xo_ref, xb_ref, r_ref):
        at = 0
        for ref, width in zip((oa_ref, ob_ref, oc_ref, om_ref), widths):
            cat_ref[:, at:at + width] = ref[...]
            at += width
        for j, g_ref in enumerate((g0_ref, g1_ref, g2_ref, g3_ref)):
            gate = g_ref[...]
            cols = slice(j * 512, (j + 1) * 512)
            yg_ref[:, cols] = (cat_ref[:, cols] * (gate * jax.nn.sigmoid(gate))).astype(BF16)
        r = ALPHA * x_ref[...] + _dot(yg_ref[...], w_ref[...])
        r_ref[...] = r
        xc = r - jnp.mean(r, axis=1, keepdims=True)
        o = xc * lax.rsqrt(jnp.mean(xc * xc, axis=1, keepdims=True) + LN_EPS) * lg_ref[...] + lb_ref[...]
        xo_ref[...] = o
        xb_ref[...] = o.astype(BF16)

    row = _row_spec(tm, D_MODEL)
    return pl.pallas_call(
        body, name="gate_out_ln", grid=(S // tm,),
        in_specs=[_row_spec(tm, width) for width in widths] + [_row_spec(tm, 512, GATE // 512 + j) for j in range(4)]
                 + [_fix_spec((D_MODEL, D_MODEL)), row, _fix_spec((1, D_MODEL)), _fix_spec((1, D_MODEL))],
        out_specs=[row] * 5,
        out_shape=[jax.ShapeDtypeStruct((S, D_MODEL), t) for t in (F32, BF16, F32, BF16, F32)],
        compiler_params=_params(("parallel",)))(*branches, h, h, h, h, w_out, x, g.reshape(1, D_MODEL), b.reshape(1, D_MODEL))


def _out_proj_gate_bwd(dr, w_out, o_cat, h):
    S = h.shape[0]
    tm = _tile(S, 1024)

    def body(dr_ref, w_ref, o_ref, g_ref, do_ref, dg_ref, drb):
        @pl.when(pl.program_id(1) == 0)
        def _():
            drb[...] = dr_ref[...].astype(BF16)

        d = _dot_nt(drb[...], w_ref[...])
        g = g_ref[...]
        sig = jax.nn.sigmoid(g)
        do_ref[...] = d * (g * sig)
        dg_ref[...] = d * o_ref[...] * (sig * (1.0 + g * (1.0 - sig)))

    blk = pl.BlockSpec((tm, 512), lambda i, j: (i, j))
    return pl.pallas_call(
        body, name="d_out_proj_gate", grid=(S // tm, 4),
        in_specs=[pl.BlockSpec((tm, D_MODEL), lambda i, j: (i, 0)), pl.BlockSpec((512, D_MODEL), lambda i, j: (j, 0)),
                  blk, pl.BlockSpec((tm, 512), lambda i, j: (i, GATE // 512 + j))],
        out_specs=[blk, blk],
        out_shape=[jax.ShapeDtypeStruct((S, D_MODEL), F32), jax.ShapeDtypeStruct((S, D_MODEL), F32)],
        scratch_shapes=[pltpu.VMEM((tm, D_MODEL), BF16)],
        compiler_params=_params(("parallel", "arbitrary")))(dr, w_out, o_cat, h)


def _ln_res_bwd(dout, r, g):
    S = r.shape[0]
    tm = _tile(S, 512)

    def body(d_ref, r_ref, g_ref, dr_ref, dg_ref, db_ref):
        @pl.when(pl.program_id(0) == 0)
        def _():
            dg_ref[...] = jnp.zeros_like(dg_ref)
            db_ref[...] = jnp.zeros_like(db_ref)

        d, r = d_ref[...], r_ref[...]
        xc = r - jnp.mean(r, axis=1, keepdims=True)
        rstd = lax.rsqrt(jnp.mean(xc * xc, axis=1, keepdims=True) + LN_EPS)
        xhat = xc * rstd
        dxh = d * g_ref[...]
        dr_ref[...] = rstd * (dxh - jnp.mean(dxh, axis=1, keepdims=True)
                              - xhat * jnp.mean(dxh * xhat, axis=1, keepdims=True))
        dg_ref[...] += jnp.sum(d * xhat, axis=0, keepdims=True)
        db_ref[...] += jnp.sum(d, axis=0, keepdims=True)

    return pl.pallas_call(
        body, name="ln_res_bwd", grid=(S // tm,),
        in_specs=[_row_spec(tm, D_MODEL), _row_spec(tm, D_MODEL), _fix_spec((1, D_MODEL))],
        out_specs=[_row_spec(tm, D_MODEL), _fix_spec((1, D_MODEL)), _fix_spec((1, D_MODEL))],
        out_shape=[jax.ShapeDtypeStruct((S, D_MODEL), F32), jax.ShapeDtypeStruct((1, D_MODEL), F32),
                   jax.ShapeDtypeStruct((1, D_MODEL), F32)],
        compiler_params=_params(("arbitrary",)))(dout, r, g.reshape(1, D_MODEL))


def _loss_head(y, target):
    S = y.shape[0]
    tm = _tile(S, 512)

    def body(y_ref, t_ref, l_ref, d_ref):
        @pl.when(pl.program_id(0) == 0)
        def _():
            l_ref[...] = jnp.zeros_like(l_ref)

        diff = y_ref[...] - t_ref[...]
        d_ref[...] = diff * (1.0 / D_MODEL)
        per_row = jnp.mean(diff * diff, axis=1, keepdims=True)
        l_ref[...] += 0.5 * jnp.sum(per_row, axis=0, keepdims=True)

    return pl.pallas_call(
        body, name="loss_head", grid=(S // tm,), in_specs=[_row_spec(tm, D_MODEL), _row_spec(tm, D_MODEL)],
        out_specs=[_fix_spec((8, LANE)), _row_spec(tm, D_MODEL)],
        out_shape=[jax.ShapeDtypeStruct((8, LANE), F32), jax.ShapeDtypeStruct((S, D_MODEL), F32)],
        compiler_params=_params(("arbitrary",)))(y, target)


def _perm_table():
    table, at = [], 0
    for name in PERM_ORDER:
        start, width = ORIG[name]
        table.append((name, start, width, at))
        at += width
    return table


def _permute_w_in(by_chip):
    wc = by_chip.shape[-1]
    parts = []
    for _, start, width, _ in _perm_table():
        lo = start
        while lo < start + width:
            k = lo // wc
            hi = min(start + width, (k + 1) * wc)
            parts.append(by_chip[k, ..., lo - k * wc:hi - k * wc])
            lo = hi
    parts.append(jnp.zeros(by_chip.shape[1:-1] + (HP - D_IN,), by_chip.dtype))
    return jnp.concatenate(parts, axis=-1)


def _model_cols(wp, lo, hi):
    parts = []
    for _, start, width, at in sorted(_perm_table(), key=lambda t: t[1]):
        a, b = max(lo, start), min(hi, start + width)
        if a < b:
            parts.append(wp[..., at + a - start:at + b - start])
    return jnp.concatenate(parts, axis=-1)


def _rope_tables(positions):
    inv_freq = ROPE_THETA ** (-jnp.arange(0, 64, 2, dtype=F32) / 64)
    ang = positions.astype(F32)[:, None] * inv_freq[None, :]
    cos, sin, zero = jnp.cos(ang), jnp.sin(ang), jnp.zeros((positions.shape[0], 64), F32)
    return jnp.concatenate([cos, cos, zero], axis=1), jnp.concatenate([-sin, sin, zero], axis=1)


def _layer_weights(by_chip):
    w_in, w_uq, w_ukv, w_mem_k, w_mem_v, w_out = by_chip
    w_uq = jnp.concatenate([w_uq[k] for k in range(4)], axis=1)
    w_uq = jnp.pad(w_uq.reshape(512, MLA_HEADS, 192), ((0, 0), (0, 0), (0, 64))).reshape(512, MLA_HEADS * 256)
    return (_permute_w_in(w_in), w_uq, jnp.concatenate([w_ukv[k] for k in range(4)], axis=1),
            w_mem_k.reshape(D_MODEL, 256), w_mem_v.reshape(D_MODEL, 256), w_out.reshape(D_MODEL, D_MODEL))


def _local_step(x, mem, positions, target, w, layer_source, next_shards, hooks=None):
    rc, rs = _rope_tables(positions)
    mem_b = mem.astype(BF16)
    xb = x.astype(BF16)
    saved = []
    fetched = None
    for l in range(DEPTH):
        w_in, w_uq, w_ukv, w_mem_k, w_mem_v, w_out = layer_source(l, fetched)
        h = _mm(xb, w_in, tm=1024, tn=1152, tk=2048, name="in_proj")
        cq_n = _rms_fwd(h, CQ, 512, w["q_norm_g"][l], "rms_q")
        ckv_n = _rms_fwd(h, CKV, 256, w["kv_norm_g"][l], "rms_kv")
        q = _q_proj(cq_n, w_uq, rc, rs)
        kp, v = _kv_proj(ckv_n, w_ukv, h, rc, rs)
        o_a, lse, fetched = _mla_fwd(q, kp, v, next_shards(l))
        bias_t = w["sg_b"][l].T
        o_b = _sg_fwd(h, w["sg_ln_g"][l], w["sg_ln_b"][l], w["sg_w"][l], bias_t)
        qkv = jnp.concatenate([h[:, SBQ:SBQ + 512] * SB_SCALE, h[:, SBQ + 512:SBQ + 1536]], axis=1).astype(BF16)
        o_c = _sb_fwd(qkv)
        mk = _mm(mem_b, w_mem_k, out_dtype=BF16, name="mem_kv")
        mv = _mm(mem_b, w_mem_v, out_dtype=BF16, name="mem_kv")
        o_m = _mem_fwd(h, mk, mv)
        o_cat, yg, x_new, xb_new, r = _gate_out_ln((o_a, o_b, o_c, o_m), h, w_out, x, w["ln_g"][l], w["ln_b"][l])
        saved.append(dict(xb=xb, h=h, cq_n=cq_n, ckv_n=ckv_n, q=q, kp=kp, v=v, lse=lse, qkv=qkv, mk=mk, mv=mv,
                          o_cat=o_cat, yg=yg, r=r, w_in=w_in, w_uq=w_uq, w_ukv=w_ukv, w_out=w_out, bias_t=bias_t))
        x, xb = x_new, xb_new

    loss, dx = _loss_head(x, target)

    grads = {n: [None] * DEPTH for n in SHARDED + SMALL}
    for l in reversed(range(DEPTH)):
        s = saved[l]
        h = s["h"]
        dr, dlg, dlb = _ln_res_bwd(dx, s["r"], w["ln_g"][l])
        grads["ln_g"][l], grads["ln_b"][l] = dlg[0], dlb[0]
        grads["w_out"][l] = _mm(s["yg"], dr, ta=True, tm=1024, tn=1024, tk=2048, name="dw_out")
        do_cat, dgates = _out_proj_gate_bwd(dr, s["w_out"], s["o_cat"], h)
        dmq, dmk, dmv = _mem_bwd(h, s["mk"], s["mv"], do_cat, 1792)
        grads["w_mem_k"][l] = _mm(mem_b, dmk, ta=True, name="dw_mem")
        grads["w_mem_v"][l] = _mm(mem_b, dmv, ta=True, name="dw_mem")
        dsq, dsk, dsv = _sb_bwd(s["qkv"], do_cat, s["o_cat"], 1280 // LANE)
        du, dv, dsgw, dsgb, dsg_g, dsg_b = _sg_bwd(h, w["sg_ln_g"][l], w["sg_ln_b"][l], w["sg_w"][l], s["bias_t"],
                                                   do_cat, 768)
        grads["sg_w"][l], grads["sg_b"][l] = dsgw, dsgb.T
        grads["sg_ln_g"][l], grads["sg_ln_b"][l] = dsg_g[0], dsg_b[0]
        ride = hooks["early"][0](grads) if hooks and l == 0 else None
        dq_raw, dk, dvv, arrived = _mla_bwd(s["q"], s["kp"], s["v"], do_cat, s["o_cat"], s["lse"], rc, rs, ride)
        if ride:
            hooks["early"][1](arrived)
        dkv, dkpe = _kv_bwd_prep(dk, dvv, rc, rs)
        dw_uq = _mm(s["cq_n"], dq_raw, ta=True, tn=1536, tk=2048, name="dw_uq")
        grads["w_uq"][l] = dw_uq.reshape(512, MLA_HEADS, 256)[:, :, :192].reshape(512, MLA_HEADS * 192)
        grads["w_ukv"][l] = _mm(s["ckv_n"], dkv, ta=True, tn=1536, tk=2048, name="dw_ukv")
        dcq_n = _mm(dq_raw, s["w_uq"], tb=True, tm=2048, tk=1536, name="d_cq")
        dckv_n = _mm(dkv, s["w_ukv"], tb=True, tm=2048, tk=1536, name="d_ckv")
        dcq, dqg = _rms_bwd(h, CQ, 512, w["q_norm_g"][l], dcq_n, "rms_q_bwd")
        dckv, dkvg = _rms_bwd(h, CKV, 256, w["kv_norm_g"][l], dckv_n, "rms_kv_bwd")
        grads["q_norm_g"][l], grads["kv_norm_g"][l] = dqg[0], dkvg[0]
        dh = jnp.concatenate([dcq, dckv, dmq, du, dv, dsq, dsk, dsv, dgates, dkpe], axis=1).astype(BF16)
        dw_in = _mm(s["xb"], dh, ta=True, tm=1024, tn=1152, tk=2048, name="dw_in")
        grads["w_in"][l] = dw_in
        hook = hooks and {0: hooks["late"], 1: hooks["mid"]}.get(l)
        ride = hook[0](grads) if hook else None
        dx = _mm(dh, s["w_in"], tb=True, add=dr, add_scale=ALPHA, tm=1024, tn=1024, tk=1920, name="d_in_proj", ride=ride)
        if ride:
            dx, arrived = dx
            hook[1](arrived)

    return loss, dx, grads


MESH = pl.DeviceIdType.MESH
HBM_SPEC = pl.BlockSpec(memory_space=pltpu.HBM)


def _place():
    x, y, c = lax.axis_index("x"), lax.axis_index("y"), lax.axis_index("c")
    return x, y, c, [(1 - x, y), (x, 1 - y), (1 - x, 1 - y)]


def _comm_call(body, name, arrays, out_shapes, n_sems):
    return pl.pallas_call(
        body, name=name, in_specs=[HBM_SPEC] * len(arrays), out_specs=[HBM_SPEC] * len(out_shapes), out_shape=out_shapes,
        scratch_shapes=[pltpu.SemaphoreType.DMA((n_sems,)), pltpu.SemaphoreType.DMA((n_sems,))],
        compiler_params=pltpu.CompilerParams(has_side_effects=True))(*arrays)


def _gather_weights(shards):
    na = len(shards)

    def body(*refs):
        srcs, outs, (send_sems, recv_sems) = refs[:na], refs[na:2 * na], refs[2 * na:]
        x, y, c, chips = _place()
        mine = [pl.ds((s.shape[0] // 2) * c, s.shape[0] // 2) for s in shards]
        theirs = [pl.ds((s.shape[0] // 2) * (1 - c), s.shape[0] // 2) for s in shards]

        def copy(a, k, src_ref, chip, part, to):
            return pltpu.make_async_remote_copy(
                src_ref=src_ref, dst_ref=outs[a].at[chip, part], send_sem=send_sems.at[6 * a + k],
                recv_sem=recv_sems.at[6 * a + k], device_id=to, device_id_type=MESH)

        sent = [copy(a, j, srcs[a].at[mine[a]], 2 * x + y, mine[a], (px, py, c))
                for a in range(na) for j, (px, py) in enumerate(chips)]
        for cp in sent:
            cp.start()
        passed = []
        for j, (px, py) in enumerate(chips):
            for a in range(na):
                copy(a, j, srcs[a].at[mine[a]], 2 * px + py, mine[a], (px, py, c)).wait_recv()
                cp = copy(a, 3 + j, outs[a].at[2 * px + py, mine[a]], 2 * px + py, mine[a], (x, y, 1 - c))
                cp.start()
                passed.append(cp)
        for j, (px, py) in enumerate(chips):
            for a in range(na):
                copy(a, 3 + j, srcs[a].at[theirs[a]], 2 * px + py, theirs[a], (x, y, 1 - c)).wait_recv()
        for cp in sent + passed:
            cp.wait_send()

    return _comm_call(body, "gather_weights", shards, [jax.ShapeDtypeStruct((4,) + s.shape, s.dtype) for s in shards], 6 * na)


def _ride(arrays, out_shapes, copies, n_sems):
    return dict(arrays=list(arrays), out_shapes=list(out_shapes), copies=copies, n_sems=n_sems)


def _swap_copies(srcs, outs, send_sems, recv_sems):
    x, y, c, _ = _place()
    return [pltpu.make_async_remote_copy(
        src_ref=src.at[:, :, pl.ds((src.shape[2] // 2) * (1 - c), src.shape[2] // 2)], dst_ref=out,
        send_sem=send_sems.at[a], recv_sem=recv_sems.at[a], device_id=(x, y, 1 - c), device_id_type=MESH)
        for a, (src, out) in enumerate(zip(srcs, outs))]


def _swap_ride(gs):
    return _ride(gs, [jax.ShapeDtypeStruct(g.shape[:2] + (g.shape[2] // 2, g.shape[3]), g.dtype) for g in gs],
                 _swap_copies, len(gs))


def _swap_halves(gs):
    ride = _swap_ride(gs)

    def body(*refs):
        cps = _swap_copies(refs[:len(gs)], refs[len(gs):2 * len(gs)], *refs[2 * len(gs):])
        for cp in cps:
            cp.start()
        for cp in cps:
            cp.wait()

    return _comm_call(body, "swap_halves", gs, ride["out_shapes"], ride["n_sems"])


def _pair_sum(g, other, c):
    _, L, R, C = g.shape
    tr = _row_tile(R // 2, 3 * C * 4)
    nb = R // 2 // tr

    def body(c_ref, a_ref, b_ref, o_ref):
        o_ref[...] = (a_ref[...] + b_ref[...]).astype(BF16)

    blk = pl.BlockSpec((None, None, tr, C), lambda d, l, i, c_ref: (d, l, i, 0))
    return pl.pallas_call(
        body, name="pair_sum",
        grid_spec=pltpu.PrefetchScalarGridSpec(
            num_scalar_prefetch=1, grid=(4, L, nb),
            in_specs=[pl.BlockSpec((None, None, tr, C), lambda d, l, i, c_ref: (d, l, nb * c_ref[0] + i, 0)), blk],
            out_specs=blk),
        out_shape=jax.ShapeDtypeStruct((4, L, R // 2, C), BF16),
        compiler_params=_params(("parallel", "parallel", "parallel")))(c, g, other)


def _exchange_copies(srcs, outs, send_sems, recv_sems):
    x, y, c, chips = _place()
    return [pltpu.make_async_remote_copy(
        src_ref=srcs[a].at[2 * px + py], dst_ref=outs[a].at[j], send_sem=send_sems.at[3 * a + j],
        recv_sem=recv_sems.at[3 * a + j], device_id=(px, py, c), device_id_type=MESH)
        for a in range(len(srcs)) for j, (px, py) in enumerate(chips)]


def _exchange_ride(ps):
    return _ride(ps, [jax.ShapeDtypeStruct((3,) + p.shape[1:], p.dtype) for p in ps], _exchange_copies, 3 * len(ps))


def _chip_sum(p, got, me):
    _, L, R, C = p.shape
    tr = _row_tile(R, 4 * C * 4)

    def body(me_ref, p_ref, g_ref, o_ref):
        acc = p_ref[...].astype(F32)
        for k in range(3):
            acc = acc + g_ref[k].astype(F32)
        o_ref[...] = acc

    return pl.pallas_call(
        body, name="chip_sum",
        grid_spec=pltpu.PrefetchScalarGridSpec(
            num_scalar_prefetch=1, grid=(L, R // tr),
            in_specs=[pl.BlockSpec((None, None, tr, C), lambda l, i, me_ref: (me_ref[0], l, i, 0)),
                      pl.BlockSpec((3, None, tr, C), lambda l, i, me_ref: (0, l, i, 0))],
            out_specs=pl.BlockSpec((None, tr, C), lambda l, i, me_ref: (l, i, 0))),
        out_shape=jax.ShapeDtypeStruct((L, R, C), F32), compiler_params=_params(("parallel", "parallel")))(me, p, got)


def _sum_parts(t, name):
    n, H, W = t.shape
    th = _row_tile(H, (n + 1) * W * 4)

    def body(t_ref, o_ref):
        acc = t_ref[0]
        for k in range(1, n):
            acc = acc + t_ref[k]
        o_ref[...] = acc

    return pl.pallas_call(
        body, name=name, grid=(H // th,), in_specs=[pl.BlockSpec((n, th, W), lambda i: (0, i, 0))],
        out_specs=pl.BlockSpec((th, W), lambda i: (i, 0)), out_shape=jax.ShapeDtypeStruct((H, W), F32),
        compiler_params=_params(("parallel",)))(t)


def _share_with_sibling(halves):
    na = len(halves)

    def body(*refs):
        srcs, outs, (send_sems, recv_sems) = refs[:na], refs[na:2 * na], refs[2 * na:]
        x, y, c, _ = _place()

        def copy(a, which):
            hr = halves[a].shape[1]
            return pltpu.make_async_remote_copy(
                src_ref=srcs[a], dst_ref=outs[a].at[:, pl.ds(hr * which, hr)], send_sem=send_sems.at[a],
                recv_sem=recv_sems.at[a], device_id=(x, y, 1 - c), device_id_type=MESH)

        sent = [copy(a, c) for a in range(na)]
        for cp in sent:
            cp.start()
        for a in range(na):
            copy(a, 1 - c).wait_recv()
        for cp in sent:
            cp.wait_send()

    return _comm_call(body, "share_with_sibling", halves,
                      [jax.ShapeDtypeStruct((h.shape[0], 2 * h.shape[1], h.shape[2]), h.dtype) for h in halves], na)


def _gather_all(v):
    n, W = v.shape

    def body(src, out, send_sems, recv_sems, own_sem):
        x, y, c, _ = _place()
        own = pltpu.make_async_copy(src, out.at[4 * x + 2 * y + c], own_sem)
        own.start()
        flips = [(fx, fy, fc) for fx in (0, 1) for fy in (0, 1) for fc in (0, 1)][1:]
        sent = []
        for k, (fx, fy, fc) in enumerate(flips):
            cp = pltpu.make_async_remote_copy(
                src_ref=src, dst_ref=out.at[4 * x + 2 * y + c], send_sem=send_sems.at[k], recv_sem=recv_sems.at[k],
                device_id=(x ^ fx, y ^ fy, c ^ fc), device_id_type=MESH)
            cp.start()
            sent.append(cp)
        for k, (fx, fy, fc) in enumerate(flips):
            pltpu.make_async_remote_copy(
                src_ref=src, dst_ref=out.at[4 * (x ^ fx) + 2 * (y ^ fy) + (c ^ fc)], send_sem=send_sems.at[k],
                recv_sem=recv_sems.at[k], device_id=(x ^ fx, y ^ fy, c ^ fc), device_id_type=MESH).wait_recv()
        for cp in sent:
            cp.wait_send()
        own.wait()

    return pl.pallas_call(
        body, name="gather_all", in_specs=[HBM_SPEC], out_specs=HBM_SPEC,
        out_shape=jax.ShapeDtypeStruct((8, n, W), v.dtype),
        scratch_shapes=[pltpu.SemaphoreType.DMA((7,)), pltpu.SemaphoreType.DMA((7,)), pltpu.SemaphoreType.DMA(())],
        compiler_params=pltpu.CompilerParams(has_side_effects=True))(v)


def _adamw(w, g, m, v):
    shape = w.shape
    cols = shape[-1]
    w2, g2, m2, v2 = (a.reshape(-1, cols) for a in (w, g, m, v))
    rows = w2.shape[0]
    tr = next((t for t in (1024, 512, 256, 128, 64, 32, 16, 8) if rows % t == 0 and t * cols * 4 <= (2 << 20)), rows)

    def body(w_ref, g_ref, m_ref, v_ref, d_ref, nm_ref, nv_ref):
        g_ = g_ref[...]
        nm = ADAM_B1 * m_ref[...] + (1.0 - ADAM_B1) * g_
        nv = ADAM_B2 * v_ref[...] + (1.0 - ADAM_B2) * (g_ * g_)
        m_hat = nm / (1.0 - ADAM_B1 ** ADAM_STEP)
        v_hat = nv / (1.0 - ADAM_B2 ** ADAM_STEP)
        d_ref[...] = -ADAM_LR * (m_hat / (jnp.sqrt(v_hat) + ADAM_EPS) + ADAM_WD * w_ref[...])
        nm_ref[...] = nm
        nv_ref[...] = nv

    blk = pl.BlockSpec((tr, cols), lambda i: (i, 0))
    outs = pl.pallas_call(
        body, name="adamw", grid=(rows // tr,), in_specs=[blk] * 4, out_specs=[blk] * 3,
        out_shape=[jax.ShapeDtypeStruct((rows, cols), F32)] * 3, compiler_params=_params(("parallel",)))(w2, g2, m2, v2)
    return tuple(o.reshape(shape) for o in outs)


BY_COLUMNS = ("w_in", "w_uq", "w_ukv")


def _chip_part(name, a, k):
    if name == "w_in":
        n = D_IN // 4
        return _model_cols(a, k * n, (k + 1) * n)
    n = a.shape[1 if name in BY_COLUMNS else 0] // 4
    return a[:, k * n:(k + 1) * n] if name in BY_COLUMNS else a[k * n:(k + 1) * n]


def kernel(x, mem, positions, w_in, q_norm_g, w_uq, kv_norm_g, w_ukv, sg_ln_g, sg_ln_b, sg_w, sg_b, w_mem_k, w_mem_v, w_out, ln_g, ln_b, loss_target, m_w_in, m_q_norm_g, m_w_uq, m_kv_norm_g, m_w_ukv, m_sg_ln_g, m_sg_ln_b, m_sg_w, m_sg_b, m_w_mem_k, m_w_mem_v, m_w_out, m_ln_g, m_ln_b, v_w_in, v_q_norm_g, v_w_uq, v_kv_norm_g, v_w_ukv, v_sg_ln_g, v_sg_ln_b, v_sg_w, v_sg_b, v_w_mem_k, v_w_mem_v, v_w_out, v_ln_g, v_ln_b):
    weights = dict(w_in=w_in, q_norm_g=q_norm_g, w_uq=w_uq, kv_norm_g=kv_norm_g, w_ukv=w_ukv, sg_ln_g=sg_ln_g,
                   sg_ln_b=sg_ln_b, sg_w=sg_w, sg_b=sg_b, w_mem_k=w_mem_k, w_mem_v=w_mem_v, w_out=w_out, ln_g=ln_g, ln_b=ln_b)
    mom_m = dict(w_in=m_w_in, q_norm_g=m_q_norm_g, w_uq=m_w_uq, kv_norm_g=m_kv_norm_g, w_ukv=m_w_ukv, sg_ln_g=m_sg_ln_g,
                 sg_ln_b=m_sg_ln_b, sg_w=m_sg_w, sg_b=m_sg_b, w_mem_k=m_w_mem_k, w_mem_v=m_w_mem_v, w_out=m_w_out,
                 ln_g=m_ln_g, ln_b=m_ln_b)
    mom_v = dict(w_in=v_w_in, q_norm_g=v_q_norm_g, w_uq=v_w_uq, kv_norm_g=v_kv_norm_g, w_ukv=v_w_ukv, sg_ln_g=v_sg_ln_g,
                 sg_ln_b=v_sg_ln_b, sg_w=v_sg_w, sg_b=v_sg_b, w_mem_k=v_w_mem_k, w_mem_v=v_w_mem_v, w_out=v_w_out,
                 ln_g=v_ln_g, ln_b=v_ln_b)
    c_idx = lax.axis_index("c").astype(jnp.int32).reshape(1)

    me = 2 * lax.axis_index("x") + lax.axis_index("y")
    shards = [[weights[n][l].astype(BF16) for n in SHARDED] for l in range(DEPTH)]

    def layer_source(l, fetched):
        if l == 0:
            fetched = _gather_weights(shards[0])
        return _layer_weights([lax.dynamic_update_slice(g, s[None], (me, 0, 0)) for g, s in zip(fetched, shards[l])])

    def stacked(grads, layers):
        return [jnp.stack([jnp.stack([_chip_part(n, grads[n][l], k) for l in layers]) for k in range(4)]) for n in SHARDED]

    got = {}
    hi = range(1, DEPTH)

    def mid(grads):
        got["gs_hi"] = stacked(grads, hi)
        return _swap_ride(got["gs_hi"])

    def early(grads):
        got["pairs_hi"] = [_pair_sum(g, o, c_idx) for g, o in zip(got["gs_hi"], got["swapped_hi"])]
        return _exchange_ride(got["pairs_hi"])

    def late(grads):
        gs = stacked(grads, [0])
        got["pairs_lo"] = [_pair_sum(g, o, c_idx) for g, o in zip(gs, _swap_halves(gs))]
        return _exchange_ride(got["pairs_lo"])

    hooks = {"mid": (mid, lambda arrived: got.update(swapped_hi=arrived)),
             "early": (early, lambda arrived: got.update(got_hi=arrived)),
             "late": (late, lambda arrived: got.update(got_lo=arrived))}
    loss_dev, grad_x, grads = _local_step(
        x[0], mem[0], positions[0], loss_target[0], {n: weights[n] for n in SMALL}, layer_source,
        lambda l: shards[l + 1] if l + 1 < DEPTH else None, hooks)
    pairs_lo, pairs_hi, got_lo, got_hi = got["pairs_lo"], got["pairs_hi"], got["got_lo"], got["got_hi"]
    me1 = me.astype(jnp.int32).reshape(1)
    halves = [_chip_sum(p, o, me1) for p, o in zip(pairs_lo + pairs_hi, list(got_lo) + list(got_hi))]
    whole = [lax.dynamic_update_slice(r, h, (0, h.shape[1] * c_idx[0], 0)) for r, h in zip(_share_with_sibling(halves), halves)]
    grad_out = {n: jnp.concatenate([lo, hi]) for n, lo, hi in zip(SHARDED, whole[:len(SHARDED)], whole[len(SHARDED):])}

    small_sizes = [weights[n].size for n in SMALL]
    vec = jnp.concatenate([g.reshape(-1) for n in SMALL for g in grads[n]] + [loss_dev[0]])
    n_small = vec.shape[0]
    rows_small = -(-n_small // (8 * FLAT_W)) * 8
    vec = jnp.pad(vec, (0, rows_small * FLAT_W - n_small)).reshape(rows_small, FLAT_W)
    total = _sum_parts(_gather_all(vec), "device_sum").reshape(-1)
    at = 0
    for n, size in zip(SMALL, small_sizes):
        grad_out[n] = total[at:at + size].reshape(weights[n].shape)
        at += size
    loss = total[at]

    names = list(weights)
    upd = {n: _adamw(weights[n], grad_out[n], mom_m[n], mom_v[n]) for n in names}
    return (loss, grad_x[None], *[grad_out[n] for n in names], *[upd[n][0] for n in names],
            *[upd[n][1] for n in names], *[upd[n][2] for n in names])
```

```python
import math

import jax
import jax.numpy as jnp
from jax import lax
from jax.experimental import pallas as pl
from jax.experimental.pallas import tpu as pltpu

F32, BF16 = jnp.float32, jnp.bfloat16

D_MODEL = 2048
DEPTH = 4
CHUNK = 64
MLA_HEADS = 6
MLA_SCALE = 1.0 / math.sqrt(192.0)
SB_HEADS = 4
SB_SCALE = 1.0 / math.sqrt(128.0)
MEM_HEADS = 4
MEM_SCALE = 1.0 / math.sqrt(64.0)
ROPE_THETA = 10000.0
ALPHA = (2.0 * DEPTH) ** 0.25
LN_EPS = 1e-5
RMS_EPS = 1e-6
ADAM_LR, ADAM_B1, ADAM_B2, ADAM_EPS, ADAM_WD, ADAM_STEP = 0.001, 0.9, 0.999, 1e-08, 0.01, 10

ORIG = dict(c_q=(0, 512), c_kv=(512, 256), k_pe=(768, 64), g_a=(832, 768), sg_u=(1600, 512), sg_v=(2112, 512),
            g_b=(2624, 512), sb_q=(3136, 512), sb_k=(3648, 512), sb_v=(4160, 512), g_c=(4672, 512),
            m_q=(5184, 256), g_m=(5440, 256))
D_IN = 5696
PERM_ORDER = ("c_q", "c_kv", "m_q", "sg_u", "sg_v", "sb_q", "sb_k", "sb_v", "g_a", "g_b", "g_c", "g_m", "k_pe")
HP = 5760
CQ, CKV, MQ, SGU, SGV, SBQ, GATE, KPE = 0, 512, 768, 1024, 1536, 2048, 3584, 5632

Q_BLK = 2048
K_BLK = 512
MLA_FWD_K_BLK = 1024
SB_Q_BLK = 512
SB_K_BLK = 256
SB_DEAD = -110.0
LANE = 128
VMEM_LIMIT = 56 * 1024 * 1024

FLAT_W = 1024
SHARDED = ("w_in", "w_uq", "w_ukv", "w_mem_k", "w_mem_v", "w_out")
SMALL = ("q_norm_g", "kv_norm_g", "sg_ln_g", "sg_ln_b", "sg_w", "sg_b", "ln_g", "ln_b")


def _params(sem=None):
    return pltpu.CompilerParams(dimension_semantics=sem, vmem_limit_bytes=VMEM_LIMIT)


def _tile(dim, pref):
    if dim <= pref:
        return dim
    t = (pref // LANE) * LANE
    while t >= LANE:
        if dim % t == 0:
            return t
        t -= LANE
    return dim


def _row_tile(rows, bytes_per_row, budget=8 << 20):
    best = None
    for t in range(8, rows + 1, 8):
        if rows % t == 0 and t * bytes_per_row <= budget:
            best = t
    return best if best else rows


def _dot_nt(a, b):
    return lax.dot_general(a, b, (((1,), (1,)), ((), ())), preferred_element_type=F32)


def _dot_tn(a, b):
    return lax.dot_general(a, b, (((0,), (0,)), ((), ())), preferred_element_type=F32)


def _dot(a, b):
    return jnp.dot(a, b, preferred_element_type=F32)


def _mm(a, b, *, ta=False, tb=False, add=None, add_scale=1.0, out_dtype=F32, tm=512, tn=512, tk=512, name="mm",
        ride=None):
    (K, M) = a.shape if ta else a.shape[::-1]
    (N, Kb) = b.shape if tb else b.shape[::-1]
    assert K == Kb, (a.shape, b.shape, ta, tb)
    tm, tn, tk = _tile(M, tm), _tile(N, tn), _tile(K, tk)
    gm, gn, nk = M // tm, N // tn, K // tk
    a_spec = pl.BlockSpec((tk, tm), lambda i, j, k: (k, i)) if ta else pl.BlockSpec((tm, tk), lambda i, j, k: (i, k))
    b_spec = pl.BlockSpec((tn, tk), lambda i, j, k: (j, k)) if tb else pl.BlockSpec((tk, tn), lambda i, j, k: (k, j))
    o_spec = pl.BlockSpec((tm, tn), lambda i, j, k: (i, j))
    dn = (((0 if ta else 1,), (1 if tb else 0,)), ((), ()))
    has_add = add is not None
    na = len(ride["arrays"]) if ride else 0
    n_in = 2 + has_add + na

    def body(*refs):
        a_ref, b_ref = refs[:2]
        add_ref = refs[2] if has_add else None
        o_ref = refs[n_in]
        scratch = refs[n_in + 1 + na:]
        i, j, k = pl.program_id(0), pl.program_id(1), pl.program_id(2)

        def riders():
            return ride["copies"](refs[n_in - na:n_in], refs[n_in + 1:n_in + 1 + na], *scratch[-2:])

        if na:
            @pl.when(jnp.logical_and(jnp.logical_and(i == 0, j == 0), k == 0))
            def _():
                for cp in riders():
                    cp.start()

        part = lax.dot_general(a_ref[...].astype(BF16), b_ref[...].astype(BF16), dn, preferred_element_type=F32)

        def finish(r):
            if has_add:
                r = r + add_scale * add_ref[...]
            o_ref[...] = r.astype(o_ref.dtype)

        if nk == 1:
            finish(part)
        else:
            acc_ref = scratch[0]

            @pl.when(k == 0)
            def _():
                acc_ref[...] = part

            @pl.when(k > 0)
            def _():
                acc_ref[...] += part

            @pl.when(k == nk - 1)
            def _():
                finish(acc_ref[...])

        if na:
            @pl.when(jnp.logical_and(jnp.logical_and(i == gm - 1, j == gn - 1), k == nk - 1))
            def _():
                for cp in riders():
                    cp.wait()

    ins = [a, b] + ([add] if has_add else []) + (ride["arrays"] if ride else [])
    specs = [a_spec, b_spec] + ([o_spec] if has_add else []) + [HBM_SPEC] * na
    res = pl.pallas_call(
        body, name=name, grid=(gm, gn, nk), in_specs=specs, out_specs=[o_spec] + [HBM_SPEC] * na,
        out_shape=[jax.ShapeDtypeStruct((M, N), out_dtype)] + (ride["out_shapes"] if ride else []),
        scratch_shapes=([pltpu.VMEM((tm, tn), F32)] if nk > 1 else [])
                       + ([pltpu.SemaphoreType.DMA((ride["n_sems"],))] * 2 if na else []),
        compiler_params=pltpu.CompilerParams(
            dimension_semantics=("arbitrary",) * 3 if na else ("parallel", "parallel", "arbitrary"),
            vmem_limit_bytes=VMEM_LIMIT, has_side_effects=bool(na)))(*ins)
    return (res[0], list(res[1:])) if na else res[0]


GELU_K = math.sqrt(2.0 / math.pi)


def _gelu(x):
    t = jnp.tanh(GELU_K * (x + 0.044715 * (x * x * x)))
    return 0.5 * x * (1.0 + t)


def _gelu_grad(x):
    t = jnp.tanh(GELU_K * (x + 0.044715 * (x * x * x)))
    return 0.5 * (1.0 + t) + 0.5 * x * (1.0 - t * t) * GELU_K * (1.0 + 3.0 * 0.044715 * x * x)


def _rope_swap(t):
    lane = lax.broadcasted_iota(jnp.int32, t.shape, 1)
    return jnp.where(lane < 32, pltpu.roll(t, 96, axis=1), pltpu.roll(t, 32, axis=1))


def _rope(t, c, s):
    return t * c + _rope_swap(t) * s


def _rope_bwd(dt, c, s):
    return dt * c - _rope_swap(dt) * s


def _row_spec(tm, w, cb=0):
    return pl.BlockSpec((tm, w), lambda i: (i, cb))


def _fix_spec(shape):
    return pl.BlockSpec(shape, lambda *_: (0,) * len(shape))


def _rms_fwd(h, off, width, g, name):
    S = h.shape[0]
    tm = _tile(S, 2048)

    def body(x_ref, g_ref, o_ref):
        x = x_ref[...]
        r = lax.rsqrt(jnp.mean(x * x, axis=1, keepdims=True) + RMS_EPS)
        o_ref[...] = (x * r * g_ref[...]).astype(BF16)

    return pl.pallas_call(
        body, name=name, grid=(S // tm,), in_specs=[_row_spec(tm, width, off // width), _fix_spec((1, width))],
        out_specs=_row_spec(tm, width), out_shape=jax.ShapeDtypeStruct((S, width), BF16),
        compiler_params=_params(("parallel",)))(h, g.reshape(1, width))


def _rms_bwd(h, off, width, g, dxn, name):
    S = h.shape[0]
    tm = _tile(S, 2048)

    def body(x_ref, g_ref, d_ref, dx_ref, dg_ref):
        @pl.when(pl.program_id(0) == 0)
        def _():
            dg_ref[...] = jnp.zeros_like(dg_ref)

        x, d = x_ref[...], d_ref[...]
        r = lax.rsqrt(jnp.mean(x * x, axis=1, keepdims=True) + RMS_EPS)
        gd = d * g_ref[...]
        dx_ref[...] = gd * r - x * (r * r * r) * jnp.mean(gd * x, axis=1, keepdims=True)
        dg_ref[...] += jnp.sum(d * x * r, axis=0, keepdims=True)

    return pl.pallas_call(
        body, name=name, grid=(S // tm,),
        in_specs=[_row_spec(tm, width, off // width), _fix_spec((1, width)), _row_spec(tm, width)],
        out_specs=[_row_spec(tm, width), _fix_spec((1, width))],
        out_shape=[jax.ShapeDtypeStruct((S, width), F32), jax.ShapeDtypeStruct((1, width), F32)],
        compiler_params=_params(("arbitrary",)))(h, g.reshape(1, width), dxn)


def _q_proj(xn, w, rc, rs):
    S = xn.shape[0]
    tm = _tile(S, 2048)

    def body(x_ref, w_ref, c_ref, s_ref, q_ref):
        q = _dot(x_ref[...], w_ref[...]) * MLA_SCALE
        q_ref[:, :LANE] = q[:, :LANE].astype(BF16)
        q_ref[:, LANE:] = _rope(q[:, LANE:], c_ref[...], s_ref[...]).astype(BF16)

    return pl.pallas_call(
        body, name="q_proj", grid=(S // tm, MLA_HEADS),
        in_specs=[pl.BlockSpec((tm, 512), lambda i, j: (i, 0)), pl.BlockSpec((512, 256), lambda i, j: (0, j)),
                  pl.BlockSpec((tm, LANE), lambda i, j: (i, 0)), pl.BlockSpec((tm, LANE), lambda i, j: (i, 0))],
        out_specs=pl.BlockSpec((tm, 256), lambda i, j: (i, j)),
        out_shape=jax.ShapeDtypeStruct((S, MLA_HEADS * 256), BF16),
        compiler_params=_params(("parallel", "parallel")))(xn, w, rc, rs)


def _kv_proj(xn, w, h, rc, rs):
    S = xn.shape[0]
    tm = _tile(S, 2048)

    def body(x_ref, w_ref, pe_ref, c_ref, s_ref, k_ref, v_ref):
        kv = _dot(x_ref[...], w_ref[...])
        k_ref[:, :LANE] = kv[:, :LANE].astype(BF16)
        k_ref[:, LANE:] = _rope(pe_ref[...], c_ref[...], s_ref[...]).astype(BF16)
        v_ref[...] = kv[:, LANE:].astype(BF16)

    return pl.pallas_call(
        body, name="kv_proj", grid=(S // tm, MLA_HEADS),
        in_specs=[pl.BlockSpec((tm, 256), lambda i, j: (i, 0)), pl.BlockSpec((256, 256), lambda i, j: (0, j)),
                  pl.BlockSpec((tm, LANE), lambda i, j: (i, KPE // LANE)),
                  pl.BlockSpec((tm, LANE), lambda i, j: (i, 0)), pl.BlockSpec((tm, LANE), lambda i, j: (i, 0))],
        out_specs=[pl.BlockSpec((tm, 256), lambda i, j: (i, j)), pl.BlockSpec((tm, LANE), lambda i, j: (i, j))],
        out_shape=[jax.ShapeDtypeStruct((S, MLA_HEADS * 256), BF16), jax.ShapeDtypeStruct((S, MLA_HEADS * LANE), BF16)],
        compiler_params=_params(("parallel", "parallel")))(xn, w, h, rc, rs)


def _kv_bwd_prep(dk, dv, rc, rs):
    S = dk.shape[1]
    tm = _tile(S, 1024)

    def body(dk_ref, dv_ref, c_ref, s_ref, o_ref, pe_ref):
        rot = jnp.zeros((tm, LANE), F32)
        for hh in range(MLA_HEADS):
            o_ref[:, hh * 256:hh * 256 + LANE] = dk_ref[hh, :, :LANE].astype(BF16)
            o_ref[:, hh * 256 + LANE:(hh + 1) * 256] = dv_ref[hh].astype(BF16)
            rot = rot + dk_ref[hh, :, LANE:]
        pe_ref[...] = _rope_bwd(rot, c_ref[...], s_ref[...])

    return pl.pallas_call(
        body, name="kv_bwd_prep", grid=(S // tm,),
        in_specs=[pl.BlockSpec((MLA_HEADS, tm, 256), lambda i: (0, i, 0)),
                  pl.BlockSpec((MLA_HEADS, tm, LANE), lambda i: (0, i, 0)), _row_spec(tm, LANE), _row_spec(tm, LANE)],
        out_specs=[_row_spec(tm, MLA_HEADS * 256), _row_spec(tm, LANE)],
        out_shape=[jax.ShapeDtypeStruct((S, MLA_HEADS * 256), BF16), jax.ShapeDtypeStruct((S, LANE), F32)],
        compiler_params=_params(("parallel",)))(dk, dv, rc, rs)


def _chunk_mask(T):
    row = lax.broadcasted_iota(jnp.int32, (T, T), 0)
    col = lax.broadcasted_iota(jnp.int32, (T, T), 1)
    return (col // CHUNK) <= (row // CHUNK)


def _att_blocks(S, q_blk=None, k_blk=None):
    tq = min(q_blk or Q_BLK, S)
    tk = min(k_blk or K_BLK, tq)
    return tq, tk, tq // tk


def _tail_masks(rows, tk):
    row = lax.broadcasted_iota(jnp.int32, (rows, tk), 0)
    col = lax.broadcasted_iota(jnp.int32, (rows, tk), 1)
    return (col // CHUNK) <= (row // CHUNK), col < row


def _span_masks(tk, r):
    row = lax.broadcasted_iota(jnp.int32, (tk, (r + 1) * tk), 0) + r * tk
    col = lax.broadcasted_iota(jnp.int32, (tk, (r + 1) * tk), 1)
    return (col // CHUNK) <= (row // CHUNK), col < row


def _put_rows(old, new, r0):
    return new if r0 == 0 else jnp.concatenate([old[:r0], new], axis=0)


def _mla_fwd(q, kp, v, nxt=None):
    S = q.shape[0]
    TQ, TK, n = _att_blocks(S, None, MLA_FWD_K_BLK)
    nq = S // TQ
    na = len(nxt) if nxt else 0

    def ride_along(srcs, outs, send_sems, recv_sems):
        x, y, c, chips = _place()

        def copy(a, j, px, py, chip):
            return pltpu.make_async_remote_copy(
                src_ref=srcs[a], dst_ref=outs[a].at[chip], send_sem=send_sems.at[3 * a + j],
                recv_sem=recv_sems.at[3 * a + j], device_id=(px, py, c), device_id_type=MESH)

        first = jnp.logical_and(pl.program_id(0) == 0, pl.program_id(1) == 0)
        last = jnp.logical_and(pl.program_id(0) == MLA_HEADS - 1, pl.program_id(1) == nq - 1)

        @pl.when(first)
        def _():
            for a in range(na):
                for j, (px, py) in enumerate(chips):
                    copy(a, j, px, py, 2 * x + y).start()

        @pl.when(last)
        def _():
            for a in range(na):
                for j, (px, py) in enumerate(chips):
                    copy(a, j, px, py, 2 * x + y).wait_send()
                    copy(a, j, px, py, 2 * px + py).wait_recv()

    def body(q_ref, k_ref, v_ref, *rest):
        if na:
            o_ref, lse_ref = rest[na:na + 2]
            ride_along(rest[:na], rest[na + 2:2 * na + 2], *rest[2 * na + 2:])
        else:
            o_ref, lse_ref = rest
        i = pl.program_id(1)

        def update(carry, qb, keys, mask):
            m, acc = carry
            s = _dot_nt(qb, k_ref[keys, :])
            if mask is not None:
                s = jnp.where(mask, s, -1e30)
            m_new = jnp.maximum(m, jnp.max(s, axis=1, keepdims=True))
            p = jnp.exp(s - m_new).astype(BF16)
            vb = v_ref[keys, :]
            return m_new, jnp.exp(m - m_new) * acc + _dot(p, jnp.concatenate([vb, jnp.ones_like(vb)], axis=1))

        carry = (jnp.full((TQ, 1), -1e30, F32), jnp.zeros((TQ, 2 * LANE), F32))
        carry = lax.fori_loop(
            0, i * n, lambda j, c: update(c, q_ref[...], pl.ds(pl.multiple_of(j * TK, TK), TK), None), carry)
        for r in range(n):
            rows = slice(r * TK, (r + 1) * TK)
            m, acc = update(tuple(c[rows] for c in carry), q_ref[rows, :],
                            pl.ds(pl.multiple_of(i * TQ, TQ), (r + 1) * TK), _span_masks(TK, r)[0])
            o_ref[rows, :] = acc[:, :LANE] / acc[:, LANE:]
            lse_ref[rows, :] = m + jnp.log(acc[:, LANE:])

    res = pl.pallas_call(
        body, name="mla_fwd_gather" if na else "mla_fwd", grid=(MLA_HEADS, nq),
        in_specs=[pl.BlockSpec((TQ, 256), lambda h, i: (i, h)),
                  pl.BlockSpec((S, 256), lambda h, i: (0, h), pipeline_mode=pl.Buffered(1)),
                  pl.BlockSpec((S, LANE), lambda h, i: (0, h), pipeline_mode=pl.Buffered(1))] + [HBM_SPEC] * na,
        out_specs=[pl.BlockSpec((TQ, LANE), lambda h, i: (i, h)), pl.BlockSpec((TQ, LANE), lambda h, i: (i, h))]
                  + [HBM_SPEC] * na,
        out_shape=[jax.ShapeDtypeStruct((S, MLA_HEADS * LANE), F32), jax.ShapeDtypeStruct((S, MLA_HEADS * LANE), F32)]
                  + [jax.ShapeDtypeStruct((4,) + s.shape, s.dtype) for s in (nxt or [])],
        scratch_shapes=[pltpu.SemaphoreType.DMA((3 * na,)), pltpu.SemaphoreType.DMA((3 * na,))] if na else [],
        compiler_params=pltpu.CompilerParams(dimension_semantics=("arbitrary", "arbitrary"), vmem_limit_bytes=VMEM_LIMIT,
                                             has_side_effects=bool(na)))(q, kp, v, *(nxt or []))
    return res[0], res[1], (list(res[2:]) if na else None)


def _mla_bwd(q, kp, v, do_cat, o_cat, lse, rc, rs, ride=None):
    S = q.shape[0]
    TQ, TK, n = _att_blocks(S)
    nq = S // TQ
    na = len(ride["arrays"]) if ride else 0

    def body(q_ref, k_ref, v_ref, do_ref, o_ref, lse_ref, c_ref, s_ref, *rest):
        dq_ref, dk_hbm, dv_hbm = rest[na:na + 3]
        dk_acc, dv_acc = rest[2 * na + 3:2 * na + 5]
        h, i = pl.program_id(0), pl.program_id(1)
        if na:
            @pl.when(jnp.logical_and(h == 0, i == 0))
            def _():
                for cp in ride["copies"](rest[:na], rest[na + 3:2 * na + 3], *rest[2 * na + 5:]):
                    cp.start()

            @pl.when(jnp.logical_and(h == MLA_HEADS - 1, i == nq - 1))
            def _():
                for cp in ride["copies"](rest[:na], rest[na + 3:2 * na + 3], *rest[2 * na + 5:]):
                    cp.wait()

        @pl.when(i == 0)
        def _():
            dk_acc[...] = jnp.zeros_like(dk_acc)
            dv_acc[...] = jnp.zeros_like(dv_acc)

        do32 = do_ref[...]
        dob = do32.astype(BF16)
        delta = jnp.sum(do32 * o_ref[...], axis=1, keepdims=True)
        lse_col = lse_ref[:, :1]

        def blk(j, dq, r0, masked):
            sl = pl.ds(pl.multiple_of(j * TK, TK), TK)
            kb, vb, qb = k_ref[sl, :], v_ref[sl, :], q_ref[r0:, :]
            s = _dot_nt(qb, kb)
            if masked:
                s = jnp.where(_tail_masks(TQ - r0, TK)[0], s, -1e30)
            p = jnp.exp(s - lse_col[r0:])
            ds = (p * (_dot_nt(dob[r0:], vb) - delta[r0:])).astype(BF16)
            dk_acc[sl, :] += _dot_tn(ds, qb)
            dv_acc[sl, :] += _dot_tn(p.astype(BF16), dob[r0:])
            return _put_rows(dq, dq[r0:] + _dot(ds, kb), r0)

        dq = lax.fori_loop(0, i * n, lambda j, c: blk(j, c, 0, False), jnp.zeros((TQ, 256), F32))
        for t in range(n):
            dq = blk(i * n + t, dq, t * TK, True)
        dq_ref[:, :LANE] = (dq[:, :LANE] * MLA_SCALE).astype(BF16)
        dq_ref[:, LANE:] = _rope_bwd(dq[:, LANE:] * MLA_SCALE, c_ref[...], s_ref[...]).astype(BF16)

        @pl.when(i == nq - 1)
        def _():
            pltpu.sync_copy(dk_acc, dk_hbm.at[h])
            pltpu.sync_copy(dv_acc, dv_hbm.at[h])

    any_spec = pl.BlockSpec(memory_space=pl.ANY)
    T = TQ
    rows = pl.BlockSpec((T, LANE), lambda h, i: (i, 0))
    res = pl.pallas_call(
        body, name="mla_bwd_exchange" if na else "mla_bwd", grid=(MLA_HEADS, nq),
        in_specs=[pl.BlockSpec((T, 256), lambda h, i: (i, h)),
                  pl.BlockSpec((S, 256), lambda h, i: (0, h), pipeline_mode=pl.Buffered(1)),
                  pl.BlockSpec((S, LANE), lambda h, i: (0, h), pipeline_mode=pl.Buffered(1)),
                  pl.BlockSpec((T, LANE), lambda h, i: (i, h)),
                  pl.BlockSpec((T, LANE), lambda h, i: (i, h)), pl.BlockSpec((T, LANE), lambda h, i: (i, h)), rows, rows]
                 + [HBM_SPEC] * na,
        out_specs=[pl.BlockSpec((T, 256), lambda h, i: (i, h)), any_spec, any_spec] + [HBM_SPEC] * na,
        out_shape=[jax.ShapeDtypeStruct((S, MLA_HEADS * 256), BF16), jax.ShapeDtypeStruct((MLA_HEADS, S, 256), F32),
                   jax.ShapeDtypeStruct((MLA_HEADS, S, LANE), F32)] + (ride["out_shapes"] if ride else []),
        scratch_shapes=[pltpu.VMEM((S, 256), F32), pltpu.VMEM((S, LANE), F32)]
                       + ([pltpu.SemaphoreType.DMA((ride["n_sems"],))] * 2 if na else []),
        compiler_params=pltpu.CompilerParams(dimension_semantics=("arbitrary", "arbitrary"), vmem_limit_bytes=VMEM_LIMIT,
                                             has_side_effects=bool(na)))(q, kp, v, do_cat, o_cat, lse, rc, rs, *(ride["arrays"] if ride else []))
    return res[0], res[1], res[2], (list(res[3:]) if na else None)


def _split_dot(x, tri):
    top = lax.bitcast_convert_type(lax.bitcast_convert_type(x, jnp.uint32) & jnp.uint32(0xFFFF0000), F32)
    return _dot(top.astype(BF16), tri) + _dot((x - top).astype(BF16), tri)


def _sb_block(qb, kb, tri, carry, masked):
    z = _dot_nt(qb, kb)
    lb = jnp.minimum(z, 0.0) - jnp.log(1.0 + jnp.exp(-jnp.abs(z)))
    lm = lb - z
    strict = None
    if masked:
        strict = _tail_masks(z.shape[0], z.shape[1])[1]
        lm = jnp.where(strict, lm, 0.0)
    a = jnp.exp(lb + carry + _split_dot(lm, tri))
    if masked:
        a = jnp.where(strict, a, 0.0)
    return a, lb, lm, strict


def _sb_walk(blk, j0, state):
    def alive(c):
        return jnp.logical_and(c[0] >= 0, jnp.max(c[1][0]) > SB_DEAD)

    return lax.while_loop(alive, lambda c: (c[0] - 1, blk(c[0], c[1], 0, False)), (j0, state))[1]


def _triangle(tk):
    row = lax.broadcasted_iota(jnp.int32, (tk, tk), 0)
    col = lax.broadcasted_iota(jnp.int32, (tk, tk), 1)
    return (row > col).astype(BF16)


def _sb_fwd(qkv):
    S = qkv.shape[0]
    TQ, TK, n = _att_blocks(S, SB_Q_BLK, SB_K_BLK)
    T = TQ

    def body(q_ref, k_ref, v_ref, o_ref):
        i = pl.program_id(1)
        tri = _triangle(TK)

        def blk(j, state, r0, masked):
            carry, acc = (c[r0:] for c in state)
            sl = pl.ds(pl.multiple_of(j * TK, TK), TK)
            a, _, lm, _ = _sb_block(q_ref[r0:, :], k_ref[sl, :], tri, carry, masked)
            new = (carry + jnp.sum(lm, axis=1, keepdims=True), acc + _dot(a.astype(BF16), v_ref[sl, :]))
            return tuple(_put_rows(c, u, r0) for c, u in zip(state, new))

        state = (jnp.zeros((TQ, 1), F32), jnp.zeros((TQ, LANE), F32))
        for t in reversed(range(n)):
            state = blk(i * n + t, state, t * TK, True)
        state = _sb_walk(blk, i * n - 1, state)
        o_ref[...] = state[1]

    return pl.pallas_call(
        body, name="sb_fwd", grid=(SB_HEADS, S // T),
        in_specs=[pl.BlockSpec((T, LANE), lambda h, i: (i, h)), pl.BlockSpec((S, LANE), lambda h, i: (0, 4 + h)),
                  pl.BlockSpec((S, LANE), lambda h, i: (0, 8 + h))],
        out_specs=pl.BlockSpec((T, LANE), lambda h, i: (i, h)),
        out_shape=jax.ShapeDtypeStruct((S, SB_HEADS * LANE), F32),
        compiler_params=_params(("parallel", "arbitrary")))(qkv, qkv, qkv)


def _sb_bwd(qkv, do_cat, o_cat, col0):
    S = qkv.shape[0]
    TQ, TK, n = _att_blocks(S, SB_Q_BLK, SB_K_BLK)
    T = TQ
    nq = S // TQ

    def body(q_ref, k_ref, v_ref, do_ref, o_ref, dq_ref, dk_hbm, dv_hbm, dk_acc, dv_acc):
        h, i = pl.program_id(0), pl.program_id(1)

        @pl.when(i == 0)
        def _():
            dk_acc[...] = jnp.zeros_like(dk_acc)
            dv_acc[...] = jnp.zeros_like(dv_acc)

        dob = do_ref[...].astype(BF16)
        tri = _triangle(TK)
        rest0 = jnp.sum(dob.astype(F32) * o_ref[...], axis=1, keepdims=True)

        def blk(j, state, r0, masked):
            carry, rest, dq = (c[r0:] for c in state)
            sl = pl.ds(pl.multiple_of(j * TK, TK), TK)
            kb, vb, qb = k_ref[sl, :], v_ref[sl, :], q_ref[r0:, :]
            a, lb, lm, strict = _sb_block(qb, kb, tri, carry, masked)
            ab = a.astype(BF16)
            e = ab.astype(F32) * _dot_nt(dob[r0:], vb)
            dz = e - jnp.exp(lb) * (rest - _split_dot(e, tri))
            if masked:
                dz = jnp.where(strict, dz, 0.0)
            dzb = dz.astype(BF16)
            dk_acc[sl, :] += _dot_tn(dzb, qb)
            dv_acc[sl, :] += _dot_tn(ab, dob[r0:])
            new = (carry + jnp.sum(lm, axis=1, keepdims=True), rest - jnp.sum(e, axis=1, keepdims=True),
                   dq + _dot(dzb, kb))
            return tuple(_put_rows(c, u, r0) for c, u in zip(state, new))

        state = (jnp.zeros((TQ, 1), F32), rest0, jnp.zeros((TQ, LANE), F32))
        for t in reversed(range(n)):
            state = blk(i * n + t, state, t * TK, True)
        state = _sb_walk(blk, i * n - 1, state)
        dq_ref[...] = state[2] * SB_SCALE

        @pl.when(i == nq - 1)
        def _():
            lanes = pl.ds(pl.multiple_of(h * LANE, LANE), LANE)
            pltpu.sync_copy(dk_acc, dk_hbm.at[:, lanes])
            pltpu.sync_copy(dv_acc, dv_hbm.at[:, lanes])

    any_spec = pl.BlockSpec(memory_space=pl.ANY)
    return pl.pallas_call(
        body, name="sb_bwd", grid=(SB_HEADS, nq),
        in_specs=[pl.BlockSpec((T, LANE), lambda h, i: (i, h)), pl.BlockSpec((S, LANE), lambda h, i: (0, 4 + h)),
                  pl.BlockSpec((S, LANE), lambda h, i: (0, 8 + h)),
                  pl.BlockSpec((T, LANE), lambda h, i: (i, col0 + h)), pl.BlockSpec((T, LANE), lambda h, i: (i, col0 + h))],
        out_specs=[pl.BlockSpec((T, LANE), lambda h, i: (i, h)), any_spec, any_spec],
        out_shape=[jax.ShapeDtypeStruct((S, SB_HEADS * LANE), F32)] * 3,
        scratch_shapes=[pltpu.VMEM((S, LANE), F32), pltpu.VMEM((S, LANE), F32)],
        compiler_params=_params(("arbitrary", "arbitrary")))(qkv, qkv, qkv, do_cat, o_cat)


def _mem_probs(q, k_ref, hh):
    lane = lax.broadcasted_iota(jnp.int32, (1, 256), 1) // 64
    msk = lane == hh
    qh = jnp.where(msk, q, 0.0).astype(BF16)
    s = _dot_nt(qh, k_ref[...]) * MEM_SCALE
    p = jnp.exp(s - jnp.max(s, axis=1, keepdims=True))
    return msk, qh, p / jnp.sum(p, axis=1, keepdims=True)


def _mem_fwd(h, mk, mv):
    S = h.shape[0]
    tm = _tile(S, 512)

    def body(q_ref, k_ref, v_ref, o_ref):
        q = q_ref[...]
        out = jnp.zeros((tm, 256), F32)
        for hh in range(MEM_HEADS):
            msk, _, p = _mem_probs(q, k_ref, hh)
            out = out + jnp.where(msk, _dot(p.astype(BF16), v_ref[...]), 0.0)
        o_ref[...] = out

    return pl.pallas_call(
        body, name="mem_fwd", grid=(S // tm,),
        in_specs=[_row_spec(tm, 256, MQ // 256), _fix_spec((256, 256)), _fix_spec((256, 256))],
        out_specs=_row_spec(tm, 256), out_shape=jax.ShapeDtypeStruct((S, 256), F32),
        compiler_params=_params(("parallel",)))(h, mk, mv)


def _mem_bwd(h, mk, mv, do_cat, col0):
    S = h.shape[0]
    tm = _tile(S, 512)

    def body(q_ref, k_ref, v_ref, do_ref, dq_ref, dk_ref, dv_ref):
        @pl.when(pl.program_id(0) == 0)
        def _():
            dk_ref[...] = jnp.zeros_like(dk_ref)
            dv_ref[...] = jnp.zeros_like(dv_ref)

        q, do = q_ref[...], do_ref[...]
        dq = jnp.zeros((tm, 256), F32)
        for hh in range(MEM_HEADS):
            msk, qh, p = _mem_probs(q, k_ref, hh)
            doh = jnp.where(msk, do, 0.0).astype(BF16)
            dp = _dot_nt(doh, v_ref[...])
            ds = (p * (dp - jnp.sum(p * dp, axis=1, keepdims=True)) * MEM_SCALE).astype(BF16)
            dq = dq + jnp.where(msk, _dot(ds, k_ref[...]), 0.0)
            dk_ref[...] += _dot_tn(ds, qh)
            dv_ref[...] += _dot_tn(p.astype(BF16), doh)
        dq_ref[...] = dq

    return pl.pallas_call(
        body, name="mem_bwd", grid=(S // tm,),
        in_specs=[_row_spec(tm, 256, MQ // 256), _fix_spec((256, 256)), _fix_spec((256, 256)),
                  _row_spec(tm, 256, col0 // 256)],
        out_specs=[_row_spec(tm, 256), _fix_spec((256, 256)), _fix_spec((256, 256))],
        out_shape=[jax.ShapeDtypeStruct((S, 256), F32), jax.ShapeDtypeStruct((256, 256), F32),
                   jax.ShapeDtypeStruct((256, 256), F32)],
        compiler_params=_params(("arbitrary",)))(h, mk, mv, do_cat)


SG_T = 128


def _sg_norm(sv, g, b):
    gv = _gelu(sv)
    xc = gv - jnp.mean(gv, axis=1, keepdims=True)
    rstd = lax.rsqrt(jnp.mean(xc * xc, axis=1, keepdims=True) + LN_EPS)
    xhat = xc * rstd
    return xhat, rstd, xhat * g + b


def _sg_fwd(h, lng, lnb, w, bias_t):
    S = h.shape[0]
    tm = _tile(S, 512)

    def body(u_ref, v_ref, g_ref, b_ref, w_ref, bias_ref, o_ref):
        mask = _chunk_mask(SG_T)
        for n in range(tm // SG_T):
            rows = slice(n * SG_T, (n + 1) * SG_T)
            u = _gelu(u_ref[rows, :])
            _, _, vn = _sg_norm(v_ref[rows, :], g_ref[...], b_ref[...])
            vb = vn.astype(BF16)
            for gi in range(4):
                cols = slice(gi * LANE, (gi + 1) * LANE)
                wg = jnp.where(mask, w_ref[gi], 0.0).astype(BF16)
                mixed = _dot(wg, vb[:, cols]) + bias_ref[:, gi:gi + 1]
                o_ref[rows, cols] = u[:, cols] * mixed

    return pl.pallas_call(
        body, name="sg_fwd", grid=(S // tm,),
        in_specs=[_row_spec(tm, 512, SGU // 512), _row_spec(tm, 512, SGV // 512), _fix_spec((1, 512)),
                  _fix_spec((1, 512)), _fix_spec((4, SG_T, SG_T)), _fix_spec((SG_T, 4))],
        out_specs=_row_spec(tm, 512), out_shape=jax.ShapeDtypeStruct((S, 512), F32),
        compiler_params=_params(("parallel",)))(h, h, lng.reshape(1, 512), lnb.reshape(1, 512), w, bias_t)


def _sg_bwd(h, lng, lnb, w, bias_t, do_cat, col0):
    S = h.shape[0]
    tm = _tile(S, 512)
    nsteps = S // tm

    def body(u_ref, v_ref, g_ref, b_ref, w_ref, bias_ref, do0_ref, do1_ref, do2_ref, do3_ref,
             du_ref, dv_ref, dw_ref, dbias_ref, dg_ref, db_ref, dvn_scr, dbias_acc):
        do_refs = (do0_ref, do1_ref, do2_ref, do3_ref)
        step = pl.program_id(0)

        @pl.when(step == 0)
        def _():
            dw_ref[...] = jnp.zeros_like(dw_ref)
            dg_ref[...] = jnp.zeros_like(dg_ref)
            db_ref[...] = jnp.zeros_like(db_ref)
            dbias_acc[...] = jnp.zeros_like(dbias_acc)

        mask = _chunk_mask(SG_T)
        for n in range(tm // SG_T):
            rows = slice(n * SG_T, (n + 1) * SG_T)
            su, sv = u_ref[rows, :], v_ref[rows, :]
            u = _gelu(su)
            xhat, rstd, vn = _sg_norm(sv, g_ref[...], b_ref[...])
            vb = vn.astype(BF16)
            ugrad = _gelu_grad(su)
            for gi in range(4):
                cols = slice(gi * LANE, (gi + 1) * LANE)
                do = do_refs[gi][rows, :]
                wg = jnp.where(mask, w_ref[gi], 0.0).astype(BF16)
                mixed = _dot(wg, vb[:, cols]) + bias_ref[:, gi:gi + 1]
                dmixed = do * u[:, cols]
                dmb = dmixed.astype(BF16)
                du_ref[rows, cols] = do * mixed * ugrad[:, cols]
                dvn_scr[:, cols] = _dot_tn(wg, dmb)
                dw_ref[gi] += jnp.where(mask, _dot_nt(dmb, vb[:, cols]), 0.0)
                dbias_acc[gi] += dmixed
            dvn = dvn_scr[...]
            dg_ref[...] += jnp.sum(dvn * xhat, axis=0, keepdims=True)
            db_ref[...] += jnp.sum(dvn, axis=0, keepdims=True)
            dxh = dvn * g_ref[...]
            dgv = rstd * (dxh - jnp.mean(dxh, axis=1, keepdims=True)
                          - xhat * jnp.mean(dxh * xhat, axis=1, keepdims=True))
            dv_ref[rows, :] = dgv * _gelu_grad(sv)

        @pl.when(step == nsteps - 1)
        def _():
            for gi in range(4):
                dbias_ref[:, gi:gi + 1] = jnp.sum(dbias_acc[gi], axis=1, keepdims=True)

    return pl.pallas_call(
        body, name="sg_bwd", grid=(nsteps,),
        in_specs=[_row_spec(tm, 512, SGU // 512), _row_spec(tm, 512, SGV // 512), _fix_spec((1, 512)),
                  _fix_spec((1, 512)), _fix_spec((4, SG_T, SG_T)), _fix_spec((SG_T, 4))]
                 + [_row_spec(tm, LANE, col0 // LANE + gi) for gi in range(4)],
        out_specs=[_row_spec(tm, 512), _row_spec(tm, 512), _fix_spec((4, SG_T, SG_T)), _fix_spec((SG_T, 4)),
                   _fix_spec((1, 512)), _fix_spec((1, 512))],
        out_shape=[jax.ShapeDtypeStruct((S, 512), F32), jax.ShapeDtypeStruct((S, 512), F32),
                   jax.ShapeDtypeStruct((4, SG_T, SG_T), F32), jax.ShapeDtypeStruct((SG_T, 4), F32),
                   jax.ShapeDtypeStruct((1, 512), F32), jax.ShapeDtypeStruct((1, 512), F32)],
        scratch_shapes=[pltpu.VMEM((SG_T, 512), F32), pltpu.VMEM((4, SG_T, SG_T), F32)],
        compiler_params=_params(("arbitrary",)))(h, h, lng.reshape(1, 512), lnb.reshape(1, 512), w, bias_t,
                                                 do_cat, do_cat, do_cat, do_cat)


def _gate_out_ln(branches, h, w_out, x, g, b):
    S = h.shape[0]
    tm = _tile(S, 256)
    widths = [a.shape[1] for a in branches]

    def body(oa_ref, ob_ref, oc_ref, om_ref, g0_ref, g1_ref, g2_ref, g3_ref, w_ref, x_ref, lg_ref, lb_ref,
             cat_ref, yg_ref, xo_ref, xb_ref, r_ref):
        at = 0
        for ref, width in zip((oa_ref, ob_ref, oc_ref, om_ref), widths):
            cat_ref[:, at:at + width] = ref[...]
            at += width
        for j, g_ref in enumerate((g0_ref, g1_ref, g2_ref, g3_ref)):
            gate = g_ref[...]
            cols = slice(j * 512, (j + 1) * 512)
            yg_ref[:, cols] = (cat_ref[:, cols] * (gate * jax.nn.sigmoid(gate))).astype(BF16)
        r = ALPHA * x_ref[...] + _dot(yg_ref[...], w_ref[...])
        r_ref[...] = r
        xc = r - jnp.mean(r, axis=1, keepdims=True)
        o = xc * lax.rsqrt(jnp.mean(xc * xc, axis=1, keepdims=True) + LN_EPS) * lg_ref[...] + lb_ref[...]
        xo_ref[...] = o
        xb_ref[...] = o.astype(BF16)

    row = _row_spec(tm, D_MODEL)
    return pl.pallas_call(
        body, name="gate_out_ln", grid=(S // tm,),
        in_specs=[_row_spec(tm, width) for width in widths] + [_row_spec(tm, 512, GATE // 512 + j) for j in range(4)]
                 + [_fix_spec((D_MODEL, D_MODEL)), row, _fix_spec((1, D_MODEL)), _fix_spec((1, D_MODEL))],
        out_specs=[row] * 5,
        out_shape=[jax.ShapeDtypeStruct((S, D_MODEL), t) for t in (F32, BF16, F32, BF16, F32)],
        compiler_params=_params(("parallel",)))(*branches, h, h, h, h, w_out, x, g.reshape(1, D_MODEL), b.reshape(1, D_MODEL))


def _out_proj_gate_bwd(dr, w_out, o_cat, h):
    S = h.shape[0]
    tm = _tile(S, 1024)

    def body(dr_ref, w_ref, o_ref, g_ref, do_ref, dg_ref, drb):
        @pl.when(pl.program_id(1) == 0)
        def _():
            drb[...] = dr_ref[...].astype(BF16)

        d = _dot_nt(drb[...], w_ref[...])
        g = g_ref[...]
        sig = jax.nn.sigmoid(g)
        do_ref[...] = d * (g * sig)
        dg_ref[...] = d * o_ref[...] * (sig * (1.0 + g * (1.0 - sig)))

    blk = pl.BlockSpec((tm, 512), lambda i, j: (i, j))
    return pl.pallas_call(
        body, name="d_out_proj_gate", grid=(S // tm, 4),
        in_specs=[pl.BlockSpec((tm, D_MODEL), lambda i, j: (i, 0)), pl.BlockSpec((512, D_MODEL), lambda i, j: (j, 0)),
                  blk, pl.BlockSpec((tm, 512), lambda i, j: (i, GATE // 512 + j))],
        out_specs=[blk, blk],
        out_shape=[jax.ShapeDtypeStruct((S, D_MODEL), F32), jax.ShapeDtypeStruct((S, D_MODEL), F32)],
        scratch_shapes=[pltpu.VMEM((tm, D_MODEL), BF16)],
        compiler_params=_params(("parallel", "arbitrary")))(dr, w_out, o_cat, h)


def _ln_res_bwd(dout, r, g):
    S = r.shape[0]
    tm = _tile(S, 512)

    def body(d_ref, r_ref, g_ref, dr_ref, dg_ref, db_ref):
        @pl.when(pl.program_id(0) == 0)
        def _():
            dg_ref[...] = jnp.zeros_like(dg_ref)
            db_ref[...] = jnp.zeros_like(db_ref)

        d, r = d_ref[...], r_ref[...]
        xc = r - jnp.mean(r, axis=1, keepdims=True)
        rstd = lax.rsqrt(jnp.mean(xc * xc, axis=1, keepdims=True) + LN_EPS)
        xhat = xc * rstd
        dxh = d * g_ref[...]
        dr_ref[...] = rstd * (dxh - jnp.mean(dxh, axis=1, keepdims=True)
                              - xhat * jnp.mean(dxh * xhat, axis=1, keepdims=True))
        dg_ref[...] += jnp.sum(d * xhat, axis=0, keepdims=True)
        db_ref[...] += jnp.sum(d, axis=0, keepdims=True)

    return pl.pallas_call(
        body, name="ln_res_bwd", grid=(S // tm,),
        in_specs=[_row_spec(tm, D_MODEL), _row_spec(tm, D_MODEL), _fix_spec((1, D_MODEL))],
        out_specs=[_row_spec(tm, D_MODEL), _fix_spec((1, D_MODEL)), _fix_spec((1, D_MODEL))],
        out_shape=[jax.ShapeDtypeStruct((S, D_MODEL), F32), jax.ShapeDtypeStruct((1, D_MODEL), F32),
                   jax.ShapeDtypeStruct((1, D_MODEL), F32)],
        compiler_params=_params(("arbitrary",)))(dout, r, g.reshape(1, D_MODEL))


def _loss_head(y, target):
    S = y.shape[0]
    tm = _tile(S, 512)

    def body(y_ref, t_ref, l_ref, d_ref):
        @pl.when(pl.program_id(0) == 0)
        def _():
            l_ref[...] = jnp.zeros_like(l_ref)

        diff = y_ref[...] - t_ref[...]
        d_ref[...] = diff * (1.0 / D_MODEL)
        per_row = jnp.mean(diff * diff, axis=1, keepdims=True)
        l_ref[...] += 0.5 * jnp.sum(per_row, axis=0, keepdims=True)

    return pl.pallas_call(
        body, name="loss_head", grid=(S // tm,), in_specs=[_row_spec(tm, D_MODEL), _row_spec(tm, D_MODEL)],
        out_specs=[_fix_spec((8, LANE)), _row_spec(tm, D_MODEL)],
        out_shape=[jax.ShapeDtypeStruct((8, LANE), F32), jax.ShapeDtypeStruct((S, D_MODEL), F32)],
        compiler_params=_params(("arbitrary",)))(y, target)


def _perm_table():
    table, at = [], 0
    for name in PERM_ORDER:
        start, width = ORIG[name]
        table.append((name, start, width, at))
        at += width
    return table


def _permute_w_in(by_chip):
    wc = by_chip.shape[-1]
    parts = []
    for _, start, width, _ in _perm_table():
        lo = start
        while lo < start + width:
            k = lo // wc
            hi = min(start + width, (k + 1) * wc)
            parts.append(by_chip[k, ..., lo - k * wc:hi - k * wc])
            lo = hi
    parts.append(jnp.zeros(by_chip.shape[1:-1] + (HP - D_IN,), by_chip.dtype))
    return jnp.concatenate(parts, axis=-1)


def _model_cols(wp, lo, hi):
    parts = []
    for _, start, width, at in sorted(_perm_table(), key=lambda t: t[1]):
        a, b = max(lo, start), min(hi, start + width)
        if a < b:
            parts.append(wp[..., at + a - start:at + b - start])
    return jnp.concatenate(parts, axis=-1)


def _rope_tables(positions):
    inv_freq = ROPE_THETA ** (-jnp.arange(0, 64, 2, dtype=F32) / 64)
    ang = positions.astype(F32)[:, None] * inv_freq[None, :]
    cos, sin, zero = jnp.cos(ang), jnp.sin(ang), jnp.zeros((positions.shape[0], 64), F32)
    return jnp.concatenate([cos, cos, zero], axis=1), jnp.concatenate([-sin, sin, zero], axis=1)


def _layer_weights(by_chip):
    w_in, w_uq, w_ukv, w_mem_k, w_mem_v, w_out = by_chip
    w_uq = jnp.concatenate([w_uq[k] for k in range(4)], axis=1)
    w_uq = jnp.pad(w_uq.reshape(512, MLA_HEADS, 192), ((0, 0), (0, 0), (0, 64))).reshape(512, MLA_HEADS * 256)
    return (_permute_w_in(w_in), w_uq, jnp.concatenate([w_ukv[k] for k in range(4)], axis=1),
            w_mem_k.reshape(D_MODEL, 256), w_mem_v.reshape(D_MODEL, 256), w_out.reshape(D_MODEL, D_MODEL))


def _local_step(x, mem, positions, target, w, layer_source, next_shards, hooks=None):
    rc, rs = _rope_tables(positions)
    mem_b = mem.astype(BF16)
    xb = x.astype(BF16)
    saved = []
    fetched = None
    for l in range(DEPTH):
        w_in, w_uq, w_ukv, w_mem_k, w_mem_v, w_out = layer_source(l, fetched)
        h = _mm(xb, w_in, tm=1024, tn=1152, tk=2048, name="in_proj")
        cq_n = _rms_fwd(h, CQ, 512, w["q_norm_g"][l], "rms_q")
        ckv_n = _rms_fwd(h, CKV, 256, w["kv_norm_g"][l], "rms_kv")
        q = _q_proj(cq_n, w_uq, rc, rs)
        kp, v = _kv_proj(ckv_n, w_ukv, h, rc, rs)
        o_a, lse, fetched = _mla_fwd(q, kp, v, next_shards(l))
        bias_t = w["sg_b"][l].T
        o_b = _sg_fwd(h, w["sg_ln_g"][l], w["sg_ln_b"][l], w["sg_w"][l], bias_t)
        qkv = jnp.concatenate([h[:, SBQ:SBQ + 512] * SB_SCALE, h[:, SBQ + 512:SBQ + 1536]], axis=1).astype(BF16)
        o_c = _sb_fwd(qkv)
        mk = _mm(mem_b, w_mem_k, out_dtype=BF16, name="mem_kv")
        mv = _mm(mem_b, w_mem_v, out_dtype=BF16, name="mem_kv")
        o_m = _mem_fwd(h, mk, mv)
        o_cat, yg, x_new, xb_new, r = _gate_out_ln((o_a, o_b, o_c, o_m), h, w_out, x, w["ln_g"][l], w["ln_b"][l])
        saved.append(dict(xb=xb, h=h, cq_n=cq_n, ckv_n=ckv_n, q=q, kp=kp, v=v, lse=lse, qkv=qkv, mk=mk, mv=mv,
                          o_cat=o_cat, yg=yg, r=r, w_in=w_in, w_uq=w_uq, w_ukv=w_ukv, w_out=w_out, bias_t=bias_t))
        x, xb = x_new, xb_new

    loss, dx = _loss_head(x, target)

    grads = {n: [None] * DEPTH for n in SHARDED + SMALL}
    for l in reversed(range(DEPTH)):
        s = saved[l]
        h = s["h"]
        dr, dlg, dlb = _ln_res_bwd(dx, s["r"], w["ln_g"][l])
        grads["ln_g"][l], grads["ln_b"][l] = dlg[0], dlb[0]
        grads["w_out"][l] = _mm(s["yg"], dr, ta=True, tm=1024, tn=1024, tk=2048, name="dw_out")
        do_cat, dgates = _out_proj_gate_bwd(dr, s["w_out"], s["o_cat"], h)
        dmq, dmk, dmv = _mem_bwd(h, s["mk"], s["mv"], do_cat, 1792)
        grads["w_mem_k"][l] = _mm(mem_b, dmk, ta=True, name="dw_mem")
        grads["w_mem_v"][l] = _mm(mem_b, dmv, ta=True, name="dw_mem")
        dsq, dsk, dsv = _sb_bwd(s["qkv"], do_cat, s["o_cat"], 1280 // LANE)
        du, dv, dsgw, dsgb, dsg_g, dsg_b = _sg_bwd(h, w["sg_ln_g"][l], w["sg_ln_b"][l], w["sg_w"][l], s["bias_t"],
                                                   do_cat, 768)
        grads["sg_w"][l], grads["sg_b"][l] = dsgw, dsgb.T
        grads["sg_ln_g"][l], grads["sg_ln_b"][l] = dsg_g[0], dsg_b[0]
        ride = hooks["early"][0](grads) if hooks and l == 0 else None
        dq_raw, dk, dvv, arrived = _mla_bwd(s["q"], s["kp"], s["v"], do_cat, s["o_cat"], s["lse"], rc, rs, ride)
        if ride:
            hooks["early"][1](arrived)
        dkv, dkpe = _kv_bwd_prep(dk, dvv, rc, rs)
        dw_uq = _mm(s["cq_n"], dq_raw, ta=True, tn=1536, tk=2048, name="dw_uq")
        grads["w_uq"][l] = dw_uq.reshape(512, MLA_HEADS, 256)[:, :, :192].reshape(512, MLA_HEADS * 192)
        grads["w_ukv"][l] = _mm(s["ckv_n"], dkv, ta=True, tn=1536, tk=2048, name="dw_ukv")
        dcq_n = _mm(dq_raw, s["w_uq"], tb=True, tm=2048, tk=1536, name="d_cq")
        dckv_n = _mm(dkv, s["w_ukv"], tb=True, tm=2048, tk=1536, name="d_ckv")
        dcq, dqg = _rms_bwd(h, CQ, 512, w["q_norm_g"][l], dcq_n, "rms_q_bwd")
        dckv, dkvg = _rms_bwd(h, CKV, 256, w["kv_norm_g"][l], dckv_n, "rms_kv_bwd")
        grads["q_norm_g"][l], grads["kv_norm_g"][l] = dqg[0], dkvg[0]
        dh = jnp.concatenate([dcq, dckv, dmq, du, dv, dsq, dsk, dsv, dgates, dkpe], axis=1).astype(BF16)
        dw_in = _mm(s["xb"], dh, ta=True, tm=1024, tn=1152, tk=2048, name="dw_in")
        grads["w_in"][l] = dw_in
        hook = hooks and {0: hooks["late"], 1: hooks["mid"]}.get(l)
        ride = hook[0](grads) if hook else None
        dx = _mm(dh, s["w_in"], tb=True, add=dr, add_scale=ALPHA, tm=1024, tn=1024, tk=1920, name="d_in_proj", ride=ride)
        if ride:
            dx, arrived = dx
            hook[1](arrived)

    return loss, dx, grads


MESH = pl.DeviceIdType.MESH
HBM_SPEC = pl.BlockSpec(memory_space=pltpu.HBM)


def _place():
    x, y, c = lax.axis_index("x"), lax.axis_index("y"), lax.axis_index("c")
    return x, y, c, [(1 - x, y), (x, 1 - y), (1 - x, 1 - y)]


def _comm_call(body, name, arrays, out_shapes, n_sems):
    return pl.pallas_call(
        body, name=name, in_specs=[HBM_SPEC] * len(arrays), out_specs=[HBM_SPEC] * len(out_shapes), out_shape=out_shapes,
        scratch_shapes=[pltpu.SemaphoreType.DMA((n_sems,)), pltpu.SemaphoreType.DMA((n_sems,))],
        compiler_params=pltpu.CompilerParams(has_side_effects=True))(*arrays)


def _gather_weights(shards):
    na = len(shards)

    def body(*refs):
        srcs, outs, (send_sems, recv_sems) = refs[:na], refs[na:2 * na], refs[2 * na:]
        x, y, c, chips = _place()
        mine = [pl.ds((s.shape[0] // 2) * c, s.shape[0] // 2) for s in shards]
        theirs = [pl.ds((s.shape[0] // 2) * (1 - c), s.shape[0] // 2) for s in shards]

        def copy(a, k, src_ref, chip, part, to):
            return pltpu.make_async_remote_copy(
                src_ref=src_ref, dst_ref=outs[a].at[chip, part], send_sem=send_sems.at[6 * a + k],
                recv_sem=recv_sems.at[6 * a + k], device_id=to, device_id_type=MESH)

        sent = [copy(a, j, srcs[a].at[mine[a]], 2 * x + y, mine[a], (px, py, c))
                for a in range(na) for j, (px, py) in enumerate(chips)]
        for cp in sent:
            cp.start()
        passed = []
        for j, (px, py) in enumerate(chips):
            for a in range(na):
                copy(a, j, srcs[a].at[mine[a]], 2 * px + py, mine[a], (px, py, c)).wait_recv()
                cp = copy(a, 3 + j, outs[a].at[2 * px + py, mine[a]], 2 * px + py, mine[a], (x, y, 1 - c))
                cp.start()
                passed.append(cp)
        for j, (px, py) in enumerate(chips):
            for a in range(na):
                copy(a, 3 + j, srcs[a].at[theirs[a]], 2 * px + py, theirs[a], (x, y, 1 - c)).wait_recv()
        for cp in sent + passed:
            cp.wait_send()

    return _comm_call(body, "gather_weights", shards, [jax.ShapeDtypeStruct((4,) + s.shape, s.dtype) for s in shards], 6 * na)


def _ride(arrays, out_shapes, copies, n_sems):
    return dict(arrays=list(arrays), out_shapes=list(out_shapes), copies=copies, n_sems=n_sems)


def _swap_copies(srcs, outs, send_sems, recv_sems):
    x, y, c, _ = _place()
    return [pltpu.make_async_remote_copy(
        src_ref=src.at[:, :, pl.ds((src.shape[2] // 2) * (1 - c), src.shape[2] // 2)], dst_ref=out,
        send_sem=send_sems.at[a], recv_sem=recv_sems.at[a], device_id=(x, y, 1 - c), device_id_type=MESH)
        for a, (src, out) in enumerate(zip(srcs, outs))]


def _swap_ride(gs):
    return _ride(gs, [jax.ShapeDtypeStruct(g.shape[:2] + (g.shape[2] // 2, g.shape[3]), g.dtype) for g in gs],
                 _swap_copies, len(gs))


def _swap_halves(gs):
    ride = _swap_ride(gs)

    def body(*refs):
        cps = _swap_copies(refs[:len(gs)], refs[len(gs):2 * len(gs)], *refs[2 * len(gs):])
        for cp in cps:
            cp.start()
        for cp in cps:
            cp.wait()

    return _comm_call(body, "swap_halves", gs, ride["out_shapes"], ride["n_sems"])


def _pair_sum(g, other, c):
    _, L, R, C = g.shape
    tr = _row_tile(R // 2, 3 * C * 4)
    nb = R // 2 // tr

    def body(c_ref, a_ref, b_ref, o_ref):
        o_ref[...] = (a_ref[...] + b_ref[...]).astype(BF16)

    blk = pl.BlockSpec((None, None, tr, C), lambda d, l, i, c_ref: (d, l, i, 0))
    return pl.pallas_call(
        body, name="pair_sum",
        grid_spec=pltpu.PrefetchScalarGridSpec(
            num_scalar_prefetch=1, grid=(4, L, nb),
            in_specs=[pl.BlockSpec((None, None, tr, C), lambda d, l, i, c_ref: (d, l, nb * c_ref[0] + i, 0)), blk],
            out_specs=blk),
        out_shape=jax.ShapeDtypeStruct((4, L, R // 2, C), BF16),
        compiler_params=_params(("parallel", "parallel", "parallel")))(c, g, other)


def _exchange_copies(srcs, outs, send_sems, recv_sems):
    x, y, c, chips = _place()
    return [pltpu.make_async_remote_copy(
        src_ref=srcs[a].at[2 * px + py], dst_ref=outs[a].at[j], send_sem=send_sems.at[3 * a + j],
        recv_sem=recv_sems.at[3 * a + j], device_id=(px, py, c), device_id_type=MESH)
        for a in range(len(srcs)) for j, (px, py) in enumerate(chips)]


def _exchange_ride(ps):
    return _ride(ps, [jax.ShapeDtypeStruct((3,) + p.shape[1:], p.dtype) for p in ps], _exchange_copies, 3 * len(ps))


def _chip_sum(p, got, me):
    _, L, R, C = p.shape
    tr = _row_tile(R, 4 * C * 4)

    def body(me_ref, p_ref, g_ref, o_ref):
        acc = p_ref[...].astype(F32)
        for k in range(3):
            acc = acc + g_ref[k].astype(F32)
        o_ref[...] = acc

    return pl.pallas_call(
        body, name="chip_sum",
        grid_spec=pltpu.PrefetchScalarGridSpec(
            num_scalar_prefetch=1, grid=(L, R // tr),
            in_specs=[pl.BlockSpec((None, None, tr, C), lambda l, i, me_ref: (me_ref[0], l, i, 0)),
                      pl.BlockSpec((3, None, tr, C), lambda l, i, me_ref: (0, l, i, 0))],
            out_specs=pl.BlockSpec((None, tr, C), lambda l, i, me_ref: (l, i, 0))),
        out_shape=jax.ShapeDtypeStruct((L, R, C), F32), compiler_params=_params(("parallel", "parallel")))(me, p, got)


def _sum_parts(t, name):
    n, H, W = t.shape
    th = _row_tile(H, (n + 1) * W * 4)

    def body(t_ref, o_ref):
        acc = t_ref[0]
        for k in range(1, n):
            acc = acc + t_ref[k]
        o_ref[...] = acc

    return pl.pallas_call(
        body, name=name, grid=(H // th,), in_specs=[pl.BlockSpec((n, th, W), lambda i: (0, i, 0))],
        out_specs=pl.BlockSpec((th, W), lambda i: (i, 0)), out_shape=jax.ShapeDtypeStruct((H, W), F32),
        compiler_params=_params(("parallel",)))(t)


def _share_with_sibling(halves):
    na = len(halves)

    def body(*refs):
        srcs, outs, (send_sems, recv_sems) = refs[:na], refs[na:2 * na], refs[2 * na:]
        x, y, c, _ = _place()

        def copy(a, which):
            hr = halves[a].shape[1]
            return pltpu.make_async_remote_copy(
                src_ref=srcs[a], dst_ref=outs[a].at[:, pl.ds(hr * which, hr)], send_sem=send_sems.at[a],
                recv_sem=recv_sems.at[a], device_id=(x, y, 1 - c), device_id_type=MESH)

        sent = [copy(a, c) for a in range(na)]
        for cp in sent:
            cp.start()
        for a in range(na):
            copy(a, 1 - c).wait_recv()
        for cp in sent:
            cp.wait_send()

    return _comm_call(body, "share_with_sibling", halves,
                      [jax.ShapeDtypeStruct((h.shape[0], 2 * h.shape[1], h.shape[2]), h.dtype) for h in halves], na)


def _gather_all(v):
    n, W = v.shape

    def body(src, out, send_sems, recv_sems, own_sem):
        x, y, c, _ = _place()
        own = pltpu.make_async_copy(src, out.at[4 * x + 2 * y + c], own_sem)
        own.start()
        flips = [(fx, fy, fc) for fx in (0, 1) for fy in (0, 1) for fc in (0, 1)][1:]
        sent = []
        for k, (fx, fy, fc) in enumerate(flips):
            cp = pltpu.make_async_remote_copy(
                src_ref=src, dst_ref=out.at[4 * x + 2 * y + c], send_sem=send_sems.at[k], recv_sem=recv_sems.at[k],
                device_id=(x ^ fx, y ^ fy, c ^ fc), device_id_type=MESH)
            cp.start()
            sent.append(cp)
        for k, (fx, fy, fc) in enumerate(flips):
            pltpu.make_async_remote_copy(
                src_ref=src, dst_ref=out.at[4 * (x ^ fx) + 2 * (y ^ fy) + (c ^ fc)], send_sem=send_sems.at[k],
                recv_sem=recv_sems.at[k], device_id=(x ^ fx, y ^ fy, c ^ fc), device_id_type=MESH).wait_recv()
        for cp in sent:
            cp.wait_send()
        own.wait()

    return pl.pallas_call(
        body, name="gather_all", in_specs=[HBM_SPEC], out_specs=HBM_SPEC,
        out_shape=jax.ShapeDtypeStruct((8, n, W), v.dtype),
        scratch_shapes=[pltpu.SemaphoreType.DMA((7,)), pltpu.SemaphoreType.DMA((7,)), pltpu.SemaphoreType.DMA(())],
        compiler_params=pltpu.CompilerParams(has_side_effects=True))(v)


def _adamw(w, g, m, v):
    shape = w.shape
    cols = shape[-1]
    w2, g2, m2, v2 = (a.reshape(-1, cols) for a in (w, g, m, v))
    rows = w2.shape[0]
    tr = next((t for t in (1024, 512, 256, 128, 64, 32, 16, 8) if rows % t == 0 and t * cols * 4 <= (2 << 20)), rows)

    def body(w_ref, g_ref, m_ref, v_ref, d_ref, nm_ref, nv_ref):
        g_ = g_ref[...]
        nm = ADAM_B1 * m_ref[...] + (1.0 - ADAM_B1) * g_
        nv = ADAM_B2 * v_ref[...] + (1.0 - ADAM_B2) * (g_ * g_)
        m_hat = nm / (1.0 - ADAM_B1 ** ADAM_STEP)
        v_hat = nv / (1.0 - ADAM_B2 ** ADAM_STEP)
        d_ref[...] = -ADAM_LR * (m_hat / (jnp.sqrt(v_hat) + ADAM_EPS) + ADAM_WD * w_ref[...])
        nm_ref[...] = nm
        nv_ref[...] = nv

    blk = pl.BlockSpec((tr, cols), lambda i: (i, 0))
    outs = pl.pallas_call(
        body, name="adamw", grid=(rows // tr,), in_specs=[blk] * 4, out_specs=[blk] * 3,
        out_shape=[jax.ShapeDtypeStruct((rows, cols), F32)] * 3, compiler_params=_params(("parallel",)))(w2, g2, m2, v2)
    return tuple(o.reshape(shape) for o in outs)


BY_COLUMNS = ("w_in", "w_uq", "w_ukv")


def _chip_part(name, a, k):
    if name == "w_in":
        n = D_IN // 4
        return _model_cols(a, k * n, (k + 1) * n)
    n = a.shape[1 if name in BY_COLUMNS else 0] // 4
    return a[:, k * n:(k + 1) * n] if name in BY_COLUMNS else a[k * n:(k + 1) * n]


def kernel(x, mem, positions, w_in, q_norm_g, w_uq, kv_norm_g, w_ukv, sg_ln_g, sg_ln_b, sg_w, sg_b, w_mem_k, w_mem_v, w_out, ln_g, ln_b, loss_target, m_w_in, m_q_norm_g, m_w_uq, m_kv_norm_g, m_w_ukv, m_sg_ln_g, m_sg_ln_b, m_sg_w, m_sg_b, m_w_mem_k, m_w_mem_v, m_w_out, m_ln_g, m_ln_b, v_w_in, v_q_norm_g, v_w_uq, v_kv_norm_g, v_w_ukv, v_sg_ln_g, v_sg_ln_b, v_sg_w, v_sg_b, v_w_mem_k, v_w_mem_v, v_w_out, v_ln_g, v_ln_b):
    weights = dict(w_in=w_in, q_norm_g=q_norm_g, w_uq=w_uq, kv_norm_g=kv_norm_g, w_ukv=w_ukv, sg_ln_g=sg_ln_g,
                   sg_ln_b=sg_ln_b, sg_w=sg_w, sg_b=sg_b, w_mem_k=w_mem_k, w_mem_v=w_mem_v, w_out=w_out, ln_g=ln_g, ln_b=ln_b)
    mom_m = dict(w_in=m_w_in, q_norm_g=m_q_norm_g, w_uq=m_w_uq, kv_norm_g=m_kv_norm_g, w_ukv=m_w_ukv, sg_ln_g=m_sg_ln_g,
                 sg_ln_b=m_sg_ln_b, sg_w=m_sg_w, sg_b=m_sg_b, w_mem_k=m_w_mem_k, w_mem_v=m_w_mem_v, w_out=m_w_out,
                 ln_g=m_ln_g, ln_b=m_ln_b)
    mom_v = dict(w_in=v_w_in, q_norm_g=v_q_norm_g, w_uq=v_w_uq, kv_norm_g=v_kv_norm_g, w_ukv=v_w_ukv, sg_ln_g=v_sg_ln_g,
                 sg_ln_b=v_sg_ln_b, sg_w=v_sg_w, sg_b=v_sg_b, w_mem_k=v_w_mem_k, w_mem_v=v_w_mem_v, w_out=v_w_out,
                 ln_g=v_ln_g, ln_b=v_ln_b)
    c_idx = lax.axis_index("c").astype(jnp.int32).reshape(1)

    me = 2 * lax.axis_index("x") + lax.axis_index("y")
    shards = [[weights[n][l].astype(BF16) for n in SHARDED] for l in range(DEPTH)]

    def layer_source(l, fetched):
        if l == 0:
            fetched = _gather_weights(shards[0])
        return _layer_weights([lax.dynamic_update_slice(g, s[None], (me, 0, 0)) for g, s in zip(fetched, shards[l])])

    def stacked(grads, layers):
        return [jnp.stack([jnp.stack([_chip_part(n, grads[n][l], k) for l in layers]) for k in range(4)]) for n in SHARDED]

    got = {}
    hi = range(1, DEPTH)

    def mid(grads):
        got["gs_hi"] = stacked(grads, hi)
        return _swap_ride(got["gs_hi"])

    def early(grads):
        got["pairs_hi"] = [_pair_sum(g, o, c_idx) for g, o in zip(got["gs_hi"], got["swapped_hi"])]
        return _exchange_ride(got["pairs_hi"])

    def late(grads):
        gs = stacked(grads, [0])
        got["pairs_lo"] = [_pair_sum(g, o, c_idx) for g, o in zip(gs, _swap_halves(gs))]
        return _exchange_ride(got["pairs_lo"])

    hooks = {"mid": (mid, lambda arrived: got.update(swapped_hi=arrived)),
             "early": (early, lambda arrived: got.update(got_hi=arrived)),
             "late": (late, lambda arrived: got.update(got_lo=arrived))}
    loss_dev, grad_x, grads = _local_step(
        x[0], mem[0], positions[0], loss_target[0], {n: weights[n] for n in SMALL}, layer_source,
        lambda l: shards[l + 1] if l + 1 < DEPTH else None, hooks)
    pairs_lo, pairs_hi, got_lo, got_hi = got["pairs_lo"], got["pairs_hi"], got["got_lo"], got["got_hi"]
    me1 = me.astype(jnp.int32).reshape(1)
    halves = [_chip_sum(p, o, me1) for p, o in zip(pairs_lo + pairs_hi, list(got_lo) + list(got_hi))]
    whole = [lax.dynamic_update_slice(r, h, (0, h.shape[1] * c_idx[0], 0)) for r, h in zip(_share_with_sibling(halves), halves)]
    grad_out = {n: jnp.concatenate([lo, hi]) for n, lo, hi in zip(SHARDED, whole[:len(SHARDED)], whole[len(SHARDED):])}

    small_sizes = [weights[n].size for n in SMALL]
    vec = jnp.concatenate([g.reshape(-1) for n in SMALL for g in grads[n]] + [loss_dev[0]])
    n_small = vec.shape[0]
    rows_small = -(-n_small // (8 * FLAT_W)) * 8
    vec = jnp.pad(vec, (0, rows_small * FLAT_W - n_small)).reshape(rows_small, FLAT_W)
    total = _sum_parts(_gather_all(vec), "device_sum").reshape(-1)
    at = 0
    for n, size in zip(SMALL, small_sizes):
        grad_out[n] = total[at:at + size].reshape(weights[n].shape)
        at += size
    loss = total[at]

    names = list(weights)
    upd = {n: _adamw(weights[n], grad_out[n], mom_m[n], mom_v[n]) for n in names}
    return (loss, grad_x[None], *[grad_out[n] for n in names], *[upd[n][0] for n in names],
            *[upd[n][1] for n in names], *[upd[n][2] for n in names])
```
